```python
import math
import jax
import jax.numpy as jnp
from jax import lax
import numpy as np

D_MODEL = 1024
BATCH = 4
SEQ = 8192
DEPTH = 2

ROPE_THETA = 10000.0
NORM_EPS = 1e-6
Q_BLOCK = 128
NEG_INF = -1e30
FORCE_SCORE = 1e6

DA_HEADS = 4
DA_HEAD_DIM = 64
DA_V_DIM = 2 * DA_HEAD_DIM
SSM_HEADS = 8
SSM_HEAD_DIM = 64
SSM_D_INNER = SSM_HEADS * SSM_HEAD_DIM
SSM_GROUPS = 2
SSM_STATE = 128
SSM_CONV = 4
SSM_CHUNK = 256
SSM_CONV_CH = SSM_D_INNER + 2 * SSM_GROUPS * SSM_STATE
NSA_HEADS = 8
NSA_KV_GROUPS = 2
NSA_HEAD_DIM = 64
NSA_CMP_BLOCK = 32
NSA_CMP_STRIDE = 16
NSA_SEL_BLOCK = 64
NSA_TOP_N = 16
NSA_WINDOW = 512
NSA_KV = NSA_KV_GROUPS * NSA_HEAD_DIM
MLA_HEADS = 4
MLA_Q_RANK = 256
MLA_KV_RANK = 128
MLA_NOPE_DIM = 128
MLA_ROPE_DIM = 64
MLA_V_DIM = 128
D_FF = 2816
N_EXPERTS = 8
TOP_K = 2

EVEN_SPLITS = (DA_HEADS * 2 * DA_HEAD_DIM, DA_HEADS * 2 * DA_HEAD_DIM, DA_HEADS * DA_V_DIM,
               SSM_D_INNER, SSM_CONV_CH, SSM_HEADS)
EVEN_IN = sum(EVEN_SPLITS)
EVEN_MIX = DA_HEADS * DA_V_DIM + SSM_D_INNER
ODD_SPLITS = (NSA_HEADS * NSA_HEAD_DIM, NSA_KV, NSA_KV, NSA_KV, NSA_KV, NSA_KV, NSA_KV,
              NSA_HEADS * 3, MLA_Q_RANK, MLA_KV_RANK, MLA_ROPE_DIM)
ODD_IN = sum(ODD_SPLITS)
ODD_MIX = NSA_HEADS * NSA_HEAD_DIM + MLA_HEADS * MLA_V_DIM

kernel_name = 'hybrid_diffattn_ssd_nsa_mla_moe'


def rms_norm(t, gain):
    tf = t.astype(jnp.float32)
    y = tf * lax.rsqrt(jnp.mean(tf * tf, axis=-1, keepdims=True) + NORM_EPS)
    return (y * gain.astype(jnp.float32)).astype(t.dtype)


def split_cols(t, sizes):
    return jnp.split(t, [int(v) for v in np.cumsum(sizes)[:-1]], axis=-1)


def rope_tables(seq_len, dim):
    inv_freq = 1.0 / (ROPE_THETA ** (jnp.arange(0, dim, 2, dtype=jnp.float32) / dim))
    ang = jnp.arange(seq_len, dtype=jnp.float32)[:, None] * inv_freq[None, :]
    return jnp.cos(ang), jnp.sin(ang)


def apply_rope(t, cos, sin):
    half = t.shape[-1] // 2
    c = cos[:, None, :].astype(t.dtype)
    s = sin[:, None, :].astype(t.dtype)
    t1, t2 = t[..., :half], t[..., half:]
    return jnp.concatenate([t1 * c - t2 * s, t2 * c + t1 * s], axis=-1)


def swiglu(h, w_gate, w_up, w_down):
    return (jax.nn.silu(h @ w_gate) * (h @ w_up)) @ w_down


def causal_attention_blocked(q, k, v, scale):
    b, s, h, dq = q.shape
    nb = s // Q_BLOCK
    qb = q.reshape(b, nb, Q_BLOCK, h, dq).transpose(1, 0, 2, 3, 4)
    k_pos = jnp.arange(s)

    def one_block(args):
        q_blk, blk = args
        q_pos = blk * Q_BLOCK + jnp.arange(Q_BLOCK)
        causal = k_pos[None, :] <= q_pos[:, None]
        sc = jnp.einsum('bqhd,bkhd->bhqk', q_blk, k).astype(jnp.float32) * scale
        p = jax.nn.softmax(jnp.where(causal, sc, NEG_INF), axis=-1)
        return jnp.einsum('bhqk,bkhd->bqhd', p.astype(v.dtype), v)

    o = lax.map(one_block, (qb, jnp.arange(nb)))
    return o.transpose(1, 0, 2, 3, 4).reshape(b, s, h, v.shape[-1])


def diff_attention(q, k, v, q_gain, k_gain, lam, subln_gain, layer_idx):
    b, s, _ = q.shape
    h, d = DA_HEADS, DA_HEAD_DIM
    cos, sin = rope_tables(s, d)
    q = apply_rope(rms_norm(q.reshape(b, s, 2 * h, d), q_gain), cos, sin).reshape(b, s, h, 2, d)
    k = apply_rope(rms_norm(k.reshape(b, s, 2 * h, d), k_gain), cos, sin).reshape(b, s, h, 2, d)
    v = v.reshape(b, s, h, DA_V_DIM)
    lam_init = 0.8 - 0.6 * math.exp(-0.3 * layer_idx)
    lf = lam.astype(jnp.float32)
    lam_full = jnp.exp(jnp.sum(lf[0] * lf[1])) - jnp.exp(jnp.sum(lf[2] * lf[3])) + lam_init
    scale = d ** -0.5
    nb = s // Q_BLOCK
    qb = q.reshape(b, nb, Q_BLOCK, h, 2, d).transpose(1, 0, 2, 3, 4, 5)
    k_pos = jnp.arange(s)

    def one_block(args):
        q_blk, blk = args
        q_pos = blk * Q_BLOCK + jnp.arange(Q_BLOCK)
        causal = k_pos[None, :] <= q_pos[:, None]
        sc = jnp.einsum('bqhcd,bkhcd->bhcqk', q_blk, k).astype(jnp.float32) * scale
        p = jax.nn.softmax(jnp.where(causal, sc, NEG_INF), axis=-1)
        attn = p[:, :, 0] - lam_full * p[:, :, 1]
        return jnp.einsum('bhqk,bkhe->bqhe', attn.astype(v.dtype), v)

    o = lax.map(one_block, (qb, jnp.arange(nb)))
    o = o.transpose(1, 0, 2, 3, 4).reshape(b, s, h, DA_V_DIM)
    o = rms_norm(o, subln_gain) * (1.0 - lam_init)
    return o.reshape(b, s, h * DA_V_DIM)


def causal_depthwise_conv(t, w, bias):
    width, ch = w.shape
    y = lax.conv_general_dilated(t, w[:, None, :].astype(t.dtype), window_strides=(1,),
                                 padding=[(width - 1, 0)],
                                 dimension_numbers=('NWC', 'WIO', 'NWC'),
                                 feature_group_count=ch)
    return y + bias.astype(t.dtype)


def segsum(a):
    t = a.shape[-1]
    cs = jnp.cumsum(a, axis=-1)
    diff = cs[..., :, None] - cs[..., None, :]
    return jnp.where(jnp.tril(jnp.ones((t, t), dtype=bool)), diff, -jnp.inf)


def ssd_chunked_scan(xh, dt, a, bmat, cmat):
    b, L, h, p = xh.shape
    n = bmat.shape[-1]
    nc = L // SSM_CHUNK
    xs = (xh * dt[..., None]).reshape(b, nc, SSM_CHUNK, h, p)
    ad = (dt * a).reshape(b, nc, SSM_CHUNK, h).transpose(0, 3, 1, 2)
    bc = bmat.reshape(b, nc, SSM_CHUNK, h, n)
    cc = cmat.reshape(b, nc, SSM_CHUNK, h, n)
    a_cs = jnp.cumsum(ad, axis=-1)
    decay_in = jnp.exp(segsum(ad))
    y_diag = jnp.einsum('bclhn,bcshn,bhcls,bcshp->bclhp', cc, bc, decay_in, xs)
    decay_to_end = jnp.exp(a_cs[..., -1:] - a_cs)
    chunk_states = jnp.einsum('bclhn,bhcl,bclhp->bchpn', bc, decay_to_end, xs)
    chunk_decay = jnp.exp(a_cs[..., -1])

    def step(state, inp):
        st, dec = inp
        return state * dec[:, :, None, None] + st, state

    init = jnp.zeros((b, h, p, n), xs.dtype)
    _, prev = lax.scan(step, init, (chunk_states.transpose(1, 0, 2, 3, 4),
                                    chunk_decay.transpose(2, 0, 1)))
    prev = prev.transpose(1, 0, 2, 3, 4)
    y_off = jnp.einsum('bclhn,bchpn,bhcl->bclhp', cc, prev, jnp.exp(a_cs))
    return (y_diag + y_off).reshape(b, L, h, p)


def mamba2_ssd(z, xbc, dt_raw, conv_w, conv_b, dt_bias, a_log, d_skip, norm_gain):
    b, s, _ = z.shape
    f32 = jnp.float32
    xbc = jax.nn.silu(causal_depthwise_conv(xbc, conv_w, conv_b))
    xs, bm, cm = split_cols(xbc, (SSM_D_INNER, SSM_GROUPS * SSM_STATE, SSM_GROUPS * SSM_STATE))
    rep = SSM_HEADS // SSM_GROUPS
    xh = xs.reshape(b, s, SSM_HEADS, SSM_HEAD_DIM)
    bm = jnp.repeat(bm.reshape(b, s, SSM_GROUPS, SSM_STATE), rep, axis=2)
    cm = jnp.repeat(cm.reshape(b, s, SSM_GROUPS, SSM_STATE), rep, axis=2)
    dt = jax.nn.softplus(dt_raw.astype(f32) + dt_bias.astype(f32))
    a = -jnp.exp(a_log.astype(f32))
    pad = (-s) % SSM_CHUNK

    def padf(t):
        return jnp.pad(t.astype(f32), [(0, 0), (0, pad)] + [(0, 0)] * (t.ndim - 2))

    y = ssd_chunked_scan(padf(xh), padf(dt), a, padf(bm), padf(cm))[:, :s]
    y = y + xh.astype(f32) * d_skip.astype(f32)[:, None]
    y = y.reshape(b, s, SSM_D_INNER) * jax.nn.silu(z.astype(f32))
    y = rms_norm(y.reshape(b, s, SSM_GROUPS, SSM_D_INNER // SSM_GROUPS),
                 norm_gain.reshape(SSM_GROUPS, SSM_D_INNER // SSM_GROUPS))
    return y.reshape(b, s, SSM_D_INNER).astype(z.dtype)


def compress_blocks(t, pos_emb, w1, w2):
    b, s, g, d = t.shape
    r = NSA_CMP_BLOCK // NSA_CMP_STRIDE
    n_chunks = s // NSA_CMP_STRIDE
    n_cmp = n_chunks - r + 1
    ch = t.reshape(b, n_chunks, NSA_CMP_STRIDE, g, d)
    blocks = jnp.concatenate([ch[:, j:j + n_cmp] for j in range(r)], axis=2)
    blocks = blocks + pos_emb[None, None, :, None, :].astype(t.dtype)
    flat = blocks.transpose(0, 1, 3, 2, 4).reshape(b, n_cmp, g, NSA_CMP_BLOCK * d)
    return jax.nn.silu(flat @ w1) @ w2


def nsa_overlap(n_cmp, n_sel):
    c_start = np.arange(n_cmp)[:, None] * NSA_CMP_STRIDE
    s_start = np.arange(n_sel)[None, :] * NSA_SEL_BLOCK
    hit = (c_start < s_start + NSA_SEL_BLOCK) & (c_start + NSA_CMP_BLOCK > s_start)
    return jnp.asarray(hit.astype(np.float32))


def _take_blocks(blocks, idx):
    return blocks[idx]


_gather_blocks = jax.vmap(jax.vmap(_take_blocks))


def nsa_attention(q, kc, vc, ksl, vsl, kwn, vwn, gate_logits, q_gain, k_gain, cmp_pos, cmp_w1, cmp_w2):
    b, s, _ = q.shape
    h, g, d = NSA_HEADS, NSA_KV_GROUPS, NSA_HEAD_DIM
    r = h // g
    scale = d ** -0.5
    cos, sin = rope_tables(s, d)
    q = apply_rope(rms_norm(q.reshape(b, s, h, d), q_gain), cos, sin).reshape(b, s, g, r, d)
    n_cmp = s // NSA_CMP_STRIDE - NSA_CMP_BLOCK // NSA_CMP_STRIDE + 1
    cmp_end = jnp.arange(n_cmp) * NSA_CMP_STRIDE + NSA_CMP_BLOCK - 1
    ck = compress_blocks(kc.reshape(b, s, g, d), cmp_pos[0], cmp_w1[0], cmp_w2[0])
    ck = apply_rope(rms_norm(ck, k_gain[0]), cos[cmp_end], sin[cmp_end])
    cv = compress_blocks(vc.reshape(b, s, g, d), cmp_pos[1], cmp_w1[1], cmp_w2[1])
    n_sel = s // NSA_SEL_BLOCK
    n_top = min(NSA_TOP_N, n_sel)
    ks_full = apply_rope(rms_norm(ksl.reshape(b, s, g, d), k_gain[1]), cos, sin)
    ks_blocks = ks_full.reshape(b, n_sel, NSA_SEL_BLOCK, g, d).transpose(0, 3, 1, 2, 4)
    vs_blocks = vsl.reshape(b, n_sel, NSA_SEL_BLOCK, g, d).transpose(0, 3, 1, 2, 4)
    overlap = nsa_overlap(n_cmp, n_sel)
    sel_idx = jnp.arange(n_sel)
    sel_start = sel_idx * NSA_SEL_BLOCK
    kw = apply_rope(rms_norm(kwn.reshape(b, s, g, d), k_gain[2]), cos, sin)
    pad_w = ((0, 0), (NSA_WINDOW, 0), (0, 0), (0, 0))
    kw_pad = jnp.pad(kw, pad_w)
    vw_pad = jnp.pad(vwn.reshape(b, s, g, d), pad_w)
    gates = jax.nn.sigmoid(gate_logits.astype(jnp.float32)).reshape(b, s, g, r, 3)
    nb = s // Q_BLOCK
    qb = q.reshape(b, nb, Q_BLOCK, g, r, d).transpose(1, 0, 2, 3, 4, 5)
    gb = gates.reshape(b, nb, Q_BLOCK, g, r, 3).transpose(1, 0, 2, 3, 4, 5)

    def one_block(args):
        q_blk, g_blk, blk = args
        q_pos = blk * Q_BLOCK + jnp.arange(Q_BLOCK)
        valid_c = cmp_end[None, :] <= q_pos[:, None]
        s_c = jnp.einsum('bqgrd,bngd->bgrqn', q_blk, ck).astype(jnp.float32) * scale
        p_c = jax.nn.softmax(jnp.where(valid_c, s_c, NEG_INF), axis=-1) * valid_c
        o_c = jnp.einsum('bgrqn,bngd->bqgrd', p_c.astype(cv.dtype), cv)
        imp = jnp.einsum('bgrqn,nj->bgqj', p_c, overlap)
        cur = (q_pos // NSA_SEL_BLOCK)[:, None]
        forced = (sel_idx[None, :] == 0) | (sel_idx[None, :] == cur) | (sel_idx[None, :] == cur - 1)
        future = sel_start[None, :] > q_pos[:, None]
        imp = jnp.where(future, -FORCE_SCORE, jnp.where(forced, FORCE_SCORE, imp))
        _, idx = lax.top_k(imp, n_top)
        k_sel = _gather_blocks(ks_blocks, idx).reshape(b, g, Q_BLOCK, n_top * NSA_SEL_BLOCK, d)
        v_sel = _gather_blocks(vs_blocks, idx).reshape(b, g, Q_BLOCK, n_top * NSA_SEL_BLOCK, d)
        pos_sel = (idx[..., None] * NSA_SEL_BLOCK + jnp.arange(NSA_SEL_BLOCK)).reshape(
            b, g, Q_BLOCK, n_top * NSA_SEL_BLOCK)
        valid_s = pos_sel <= q_pos[None, None, :, None]
        s_s = jnp.einsum('bqgrd,bgqkd->bgrqk', q_blk, k_sel).astype(jnp.float32) * scale
        p_s = jax.nn.softmax(jnp.where(valid_s[:, :, None], s_s, NEG_INF), axis=-1)
        o_s = jnp.einsum('bgrqk,bgqkd->bqgrd', p_s.astype(v_sel.dtype), v_sel)
        start = blk * Q_BLOCK
        k_win = lax.dynamic_slice_in_dim(kw_pad, start, NSA_WINDOW + Q_BLOCK, axis=1)
        v_win = lax.dynamic_slice_in_dim(vw_pad, start, NSA_WINDOW + Q_BLOCK, axis=1)
        pos_w = start - NSA_WINDOW + jnp.arange(NSA_WINDOW + Q_BLOCK)
        valid_w = ((pos_w[None, :] <= q_pos[:, None]) & (pos_w[None, :] > q_pos[:, None] - NSA_WINDOW)
                   & (pos_w[None, :] >= 0))
        s_w = jnp.einsum('bqgrd,bkgd->bgrqk', q_blk, k_win).astype(jnp.float32) * scale
        p_w = jax.nn.softmax(jnp.where(valid_w, s_w, NEG_INF), axis=-1)
        o_w = jnp.einsum('bgrqk,bkgd->bqgrd', p_w.astype(v_win.dtype), v_win)
        out = g_blk[..., 0:1] * o_c + g_blk[..., 1:2] * o_s + g_blk[..., 2:3] * o_w
        return out.astype(q_blk.dtype)

    o = lax.map(one_block, (qb, gb, jnp.arange(nb)))
    return o.transpose(1, 0, 2, 3, 4, 5).reshape(b, s, h * d)


def mla_attention(c_q, c_kv, k_rope, cq_gain, ckv_gain, w_uq, w_ukv, qn_gain, qr_gain, kn_gain, kr_gain):
    b, s, _ = c_q.shape
    h = MLA_HEADS
    cos, sin = rope_tables(s, MLA_ROPE_DIM)
    q = (rms_norm(c_q, cq_gain) @ w_uq).reshape(b, s, h, MLA_NOPE_DIM + MLA_ROPE_DIM)
    kv = (rms_norm(c_kv, ckv_gain) @ w_ukv).reshape(b, s, h, MLA_NOPE_DIM + MLA_V_DIM)
    q_nope = rms_norm(q[..., :MLA_NOPE_DIM], qn_gain)
    q_rope = apply_rope(rms_norm(q[..., MLA_NOPE_DIM:], qr_gain), cos, sin)
    k_nope = rms_norm(kv[..., :MLA_NOPE_DIM], kn_gain)
    v = kv[..., MLA_NOPE_DIM:]
    k_r = apply_rope(rms_norm(k_rope.reshape(b, s, 1, MLA_ROPE_DIM), kr_gain), cos, sin)
    q_full = jnp.concatenate([q_nope, q_rope], axis=-1)
    k_full = jnp.concatenate([k_nope, jnp.broadcast_to(k_r, (b, s, h, MLA_ROPE_DIM))], axis=-1)
    o = causal_attention_blocked(q_full, k_full, v, (MLA_NOPE_DIM + MLA_ROPE_DIM) ** -0.5)
    return o.reshape(b, s, h * MLA_V_DIM)


def moe_swiglu(h, router, w_gate, w_up, w_down):
    b, s, d = h.shape
    t = h.reshape(b * s, d)
    logits = (t @ router).astype(jnp.float32)
    top_val, top_idx = lax.top_k(logits, TOP_K)
    top_w = jax.nn.softmax(top_val, axis=-1)
    gates = jnp.sum(jax.nn.one_hot(top_idx, N_EXPERTS, dtype=jnp.float32) * top_w[..., None], axis=1)
    out = jnp.zeros_like(t)
    for e in range(N_EXPERTS):
        out = out + gates[:, e:e + 1].astype(t.dtype) * swiglu(t, w_gate[e], w_up[e], w_down[e])
    return out.reshape(b, s, d)


def setup_inputs(seed: int = 0) -> dict:
    key = jax.random.key(seed)
    ks = jax.random.split(key, 40)
    f32 = jnp.float32
    n_ev = (DEPTH + 1) // 2
    n_od = DEPTH // 2

    def nrm(k, shape, scale):
        return jax.random.normal(k, shape, f32) * scale

    def gain(k, shape):
        return 1.0 + 0.1 * jax.random.normal(k, shape, f32)

    dt0 = jnp.exp(jax.random.uniform(ks[9], (n_ev, SSM_HEADS), f32, math.log(1e-3), math.log(1e-1)))
    return {
        'x': jax.random.normal(ks[0], (BATCH, SEQ, D_MODEL), f32),
        'ev_norm_mix': gain(ks[1], (n_ev, D_MODEL)),
        'ev_w_in': nrm(ks[2], (n_ev, D_MODEL, EVEN_IN), D_MODEL ** -0.5),
        'da_q_gain': gain(ks[3], (n_ev, DA_HEAD_DIM)),
        'da_k_gain': gain(ks[4], (n_ev, DA_HEAD_DIM)),
        'da_lambda': nrm(ks[5], (n_ev, 4, DA_HEAD_DIM), 0.1),
        'da_subln_gain': gain(ks[6], (n_ev, DA_V_DIM)),
        'ssm_conv_w': nrm(ks[7], (n_ev, SSM_CONV, SSM_CONV_CH), SSM_CONV ** -0.5),
        'ssm_conv_b': nrm(ks[8], (n_ev, SSM_CONV_CH), 0.02),
        'ssm_dt_bias': dt0 + jnp.log(-jnp.expm1(-dt0)),
        'ssm_a_log': jnp.log(jax.random.uniform(ks[10], (n_ev, SSM_HEADS), f32, 1.0, 16.0)),
        'ssm_d': gain(ks[11], (n_ev, SSM_HEADS)),
        'ssm_norm_gain': gain(ks[12], (n_ev, SSM_D_INNER)),
        'ev_w_out': nrm(ks[13], (n_ev, EVEN_MIX, D_MODEL), EVEN_MIX ** -0.5),
        'ev_norm_ffn': gain(ks[14], (n_ev, D_MODEL)),
        'ffn_w_gate': nrm(ks[15], (n_ev, D_MODEL, D_FF), D_MODEL ** -0.5),
        'ffn_w_up': nrm(ks[16], (n_ev, D_MODEL, D_FF), D_MODEL ** -0.5),
        'ffn_w_down': nrm(ks[17], (n_ev, D_FF, D_MODEL), D_FF ** -0.5),
        'od_norm_mix': gain(ks[18], (n_od, D_MODEL)),
        'od_w_in': nrm(ks[19], (n_od, D_MODEL, ODD_IN), D_MODEL ** -0.5),
        'nsa_q_gain': gain(ks[20], (n_od, NSA_HEAD_DIM)),
        'nsa_k_gain': gain(ks[21], (n_od, 3, NSA_HEAD_DIM)),
        'nsa_cmp_pos': nrm(ks[22], (n_od, 2, NSA_CMP_BLOCK, NSA_HEAD_DIM), 0.1),
        'nsa_cmp_w1': nrm(ks[23], (n_od, 2, NSA_CMP_BLOCK * NSA_HEAD_DIM, NSA_HEAD_DIM),
                          (NSA_CMP_BLOCK * NSA_HEAD_DIM) ** -0.5),
        'nsa_cmp_w2': nrm(ks[24], (n_od, 2, NSA_HEAD_DIM, NSA_HEAD_DIM), NSA_HEAD_DIM ** -0.5),
        'mla_cq_gain': gain(ks[25], (n_od, MLA_Q_RANK)),
        'mla_ckv_gain': gain(ks[26], (n_od, MLA_KV_RANK)),
        'mla_w_uq': nrm(ks[27], (n_od, MLA_Q_RANK, MLA_HEADS * (MLA_NOPE_DIM + MLA_ROPE_DIM)), MLA_Q_RANK ** -0.5),
        'mla_w_ukv': nrm(ks[28], (n_od, MLA_KV_RANK, MLA_HEADS * (MLA_NOPE_DIM + MLA_V_DIM)), MLA_KV_RANK ** -0.5),
        'mla_qn_gain': gain(ks[29], (n_od, MLA_NOPE_DIM)),
        'mla_qr_gain': gain(ks[30], (n_od, MLA_ROPE_DIM)),
        'mla_kn_gain': gain(ks[31], (n_od, MLA_NOPE_DIM)),
        'mla_kr_gain': gain(ks[32], (n_od, MLA_ROPE_DIM)),
        'od_w_out': nrm(ks[33], (n_od, ODD_MIX, D_MODEL), ODD_MIX ** -0.5),
        'od_norm_ffn': gain(ks[34], (n_od, D_MODEL)),
        'moe_router': nrm(ks[35], (n_od, D_MODEL, N_EXPERTS), D_MODEL ** -0.5),
        'moe_w_gate': nrm(ks[36], (n_od, N_EXPERTS, D_MODEL, D_FF), D_MODEL ** -0.5),
        'moe_w_up': nrm(ks[37], (n_od, N_EXPERTS, D_MODEL, D_FF), D_MODEL ** -0.5),
        'moe_w_down': nrm(ks[38], (n_od, N_EXPERTS, D_FF, D_MODEL), D_FF ** -0.5),
    }


def reference(x, ev_norm_mix, ev_w_in, da_q_gain, da_k_gain, da_lambda, da_subln_gain,
              ssm_conv_w, ssm_conv_b, ssm_dt_bias, ssm_a_log, ssm_d, ssm_norm_gain,
              ev_w_out, ev_norm_ffn, ffn_w_gate, ffn_w_up, ffn_w_down,
              od_norm_mix, od_w_in, nsa_q_gain, nsa_k_gain, nsa_cmp_pos, nsa_cmp_w1, nsa_cmp_w2,
              mla_cq_gain, mla_ckv_gain, mla_w_uq, mla_w_ukv, mla_qn_gain, mla_qr_gain,
              mla_kn_gain, mla_kr_gain, od_w_out, od_norm_ffn,
              moe_router, moe_w_gate, moe_w_up, moe_w_down):
    for layer in range(DEPTH):
        i = layer // 2
        if layer % 2 == 0:
            h = rms_norm(x, ev_norm_mix[i])
            q, k, v, z, xbc, dt = split_cols(h @ ev_w_in[i], EVEN_SPLITS)
            a_out = diff_attention(q, k, v, da_q_gain[i], da_k_gain[i], da_lambda[i],
                                   da_subln_gain[i], layer)
            b_out = mamba2_ssd(z, xbc, dt, ssm_conv_w[i], ssm_conv_b[i], ssm_dt_bias[i],
                               ssm_a_log[i], ssm_d[i], ssm_norm_gain[i])
            x = x + jnp.concatenate([a_out, b_out], axis=-1) @ ev_w_out[i]
            x = x + swiglu(rms_norm(x, ev_norm_ffn[i]), ffn_w_gate[i], ffn_w_up[i], ffn_w_down[i])
        else:
            h = rms_norm(x, od_norm_mix[i])
            q, kc, vc, ksl, vsl, kwn, vwn, gl, cq, ckv, kr = split_cols(h @ od_w_in[i], ODD_SPLITS)
            c_out = nsa_attention(q, kc, vc, ksl, vsl, kwn, vwn, gl, nsa_q_gain[i], nsa_k_gain[i],
                                  nsa_cmp_pos[i], nsa_cmp_w1[i], nsa_cmp_w2[i])
            d_out = mla_attention(cq, ckv, kr, mla_cq_gain[i], mla_ckv_gain[i], mla_w_uq[i],
                                  mla_w_ukv[i], mla_qn_gain[i], mla_qr_gain[i], mla_kn_gain[i],
                                  mla_kr_gain[i])
            x = x + jnp.concatenate([c_out, d_out], axis=-1) @ od_w_out[i]
            x = x + moe_swiglu(rms_norm(x, od_norm_ffn[i]), moe_router[i], moe_w_gate[i],
                               moe_w_up[i], moe_w_down[i])
    return x
```

```python
import functools
import math

import numpy as np
import jax
import jax.numpy as jnp
from jax import lax
from jax.experimental import pallas as pl
from jax.experimental.pallas import tpu as pltpu

F32 = jnp.float32
BF16 = jnp.bfloat16

ROPE_THETA = 10000.0
NORM_EPS = 1e-6
NEG_INF = -1e30
FORCE_SCORE = 1e6

DA_HEADS = 4
DA_HEAD_DIM = 64
DA_V_DIM = 2 * DA_HEAD_DIM
SSM_HEADS = 8
SSM_HEAD_DIM = 64
SSM_D_INNER = SSM_HEADS * SSM_HEAD_DIM
SSM_GROUPS = 2
SSM_STATE = 128
SSM_CONV = 4
SSM_CHUNK = 256
NSA_HEADS = 8
NSA_KV_GROUPS = 2
NSA_HEAD_DIM = 64
NSA_CMP_BLOCK = 32
NSA_CMP_STRIDE = 16
NSA_SEL_BLOCK = 64
NSA_TOP_N = 16
NSA_WINDOW = 512
MLA_HEADS = 4
MLA_NOPE_DIM = 128
MLA_ROPE_DIM = 64
MLA_V_DIM = 128
N_EXPERTS = 8

LANES = 128
VMEM_LIMIT = 48 * 1024 * 1024

NT_DIMS = (((1,), (1,)), ((), ()))


def _cparams(semantics):
    return pltpu.CompilerParams(dimension_semantics=semantics, vmem_limit_bytes=VMEM_LIMIT)


def _dot(a, b):
    return jnp.dot(a, b, preferred_element_type=F32)


def _dot_nt(a, b):
    return lax.dot_general(a, b, NT_DIMS, preferred_element_type=F32)


def _split_bf16(x, parts):
    out = []
    for _ in range(parts):
        hi = x.astype(BF16)
        out.append(hi)
        x = x - hi.astype(F32)
    return out


def _sigmoid(x):
    return 1.0 / (1.0 + jnp.exp(-x))


def _silu(x):
    return x * _sigmoid(x)


def _softplus(x):
    return jnp.maximum(x, 0.0) + jnp.log(1.0 + jnp.exp(-jnp.abs(x)))


def _rms(x, gain):
    ms = jnp.mean(x * x, axis=-1, keepdims=True)
    return x * lax.rsqrt(ms + NORM_EPS) * gain


def _norm_proj_body(x_ref, g_ref, *refs, n_out):
    h = _rms(x_ref[...], g_ref[...]).astype(BF16)
    for w_ref, o_ref in zip(refs[:n_out], refs[n_out:]):
        o_ref[...] = _dot(h, w_ref[...]).astype(o_ref.dtype)


def norm_proj(x2, gain, weights, out_dtypes, tm=512):
    t, d = x2.shape
    n_out = len(weights)
    in_specs = [pl.BlockSpec((tm, d), lambda i: (i, 0)),
                pl.BlockSpec((1, d), lambda i: (0, 0))]
    in_specs += [pl.BlockSpec(w.shape, lambda i: (0, 0)) for w in weights]
    out_specs = [pl.BlockSpec((tm, w.shape[1]), lambda i: (i, 0)) for w in weights]
    out_shape = [jax.ShapeDtypeStruct((t, w.shape[1]), dt) for w, dt in zip(weights, out_dtypes)]
    return pl.pallas_call(
        functools.partial(_norm_proj_body, n_out=n_out),
        grid=(t // tm,), in_specs=in_specs, out_specs=out_specs, out_shape=out_shape,
        compiler_params=_cparams(("parallel",)), name="norm_proj",
    )(x2, gain.reshape(1, d), *weights)


def _head_norm_body(y_ref, gain_ref, bd_ref, *rest, hd, rope):
    o_ref = rest[-1]
    y = y_ref[...]
    n = y.shape[1]
    hi, lo = _split_bf16(y * y, 2)
    ss = _dot(hi, bd_ref[...]) + _dot(lo, bd_ref[...])
    yn = y * lax.rsqrt(ss * (1.0 / hd) + NORM_EPS) * gain_ref[...]
    if rope:
        cos_ref, sin_ref = rest[0], rest[1]
        reps = n // LANES
        cos = jnp.concatenate([cos_ref[...]] * reps, axis=1) if reps > 1 else cos_ref[...]
        sin = jnp.concatenate([sin_ref[...]] * reps, axis=1) if reps > 1 else sin_ref[...]
        lane = lax.broadcasted_iota(jnp.int32, yn.shape, 1)
        first_half = (lane & (hd - 1)) < (hd // 2)
        partner = jnp.where(first_half, pltpu.roll(yn, n - hd // 2, 1), pltpu.roll(yn, hd // 2, 1))
        yn = yn * cos + partner * sin
    o_ref[...] = yn.astype(o_ref.dtype)


def _block_diag_ones(n, hd):
    idx = np.arange(n) // hd
    return jnp.asarray((idx[:, None] == idx[None, :]).astype(np.float32), dtype=BF16)


def _rope_tables(seq, hd):
    inv_freq = 1.0 / (ROPE_THETA ** (jnp.arange(0, hd, 2, dtype=F32) / hd))
    ang = jnp.arange(seq, dtype=F32)[:, None] * inv_freq[None, :]
    cos, sin = jnp.cos(ang), jnp.sin(ang)
    reps = LANES // hd
    cos_t = jnp.tile(jnp.concatenate([cos, cos], axis=1), (1, reps))
    sin_t = jnp.tile(jnp.concatenate([-sin, sin], axis=1), (1, reps))
    return cos_t, sin_t


def head_norm(y, gain, hd, seq, rope_tables=None, out_dtype=BF16, tm=512):
    t, n = y.shape
    gain_t = jnp.tile(gain.astype(F32), n // hd).reshape(1, n)
    args = [y, gain_t, _block_diag_ones(n, hd)]
    in_specs = [pl.BlockSpec((tm, n), lambda i: (i, 0)),
                pl.BlockSpec((1, n), lambda i: (0, 0)),
                pl.BlockSpec((n, n), lambda i: (0, 0))]
    if rope_tables is not None:
        per_seq = seq // tm
        args += list(rope_tables)
        in_specs += [pl.BlockSpec((tm, LANES), lambda i: (i % per_seq, 0))] * 2
    return pl.pallas_call(
        functools.partial(_head_norm_body, hd=hd, rope=rope_tables is not None),
        grid=(t // tm,), in_specs=in_specs,
        out_specs=pl.BlockSpec((tm, n), lambda i: (i, 0)),
        out_shape=jax.ShapeDtypeStruct((t, n), out_dtype),
        compiler_params=_cparams(("parallel",)), name="head_norm",
    )(*args)


def _flash_body(lam_ref, *refs, n_qk, diff, scale, out_scale):
    q_refs = refs[:n_qk]
    k_refs = refs[n_qk:2 * n_qk]
    v_ref, gain_ref, o_ref, m_ref, l_ref, acc_ref = refs[2 * n_qk:]
    i = pl.program_id(2)
    j = pl.program_id(3)
    n_sm = 2 if diff else 1

    @pl.when(j == 0)
    def _():
        m_ref[...] = jnp.full(m_ref.shape, NEG_INF, F32)
        l_ref[...] = jnp.zeros(l_ref.shape, F32)
        acc_ref[...] = jnp.zeros(acc_ref.shape, F32)

    def step(on_diagonal):
        qs = [r[0] for r in q_refs]
        ks = [r[0] for r in k_refs]
        q = qs[0] if n_qk == 1 else jnp.concatenate(qs, axis=1)
        k = ks[0] if n_qk == 1 else jnp.concatenate(ks, axis=1)
        v = v_ref[0]
        if diff:
            lane = lax.broadcasted_iota(jnp.int32, q.shape, 1)
            half = q.shape[1] // 2
            zero = jnp.zeros_like(q)
            q_parts = [jnp.where(lane < half, q, zero), jnp.where(lane >= half, q, zero)]
        else:
            q_parts = [q]
        for c in range(n_sm):
            s = _dot_nt(q_parts[c], k) * scale
            if on_diagonal:
                row = lax.broadcasted_iota(jnp.int32, s.shape, 0)
                col = lax.broadcasted_iota(jnp.int32, s.shape, 1)
                s = jnp.where(col <= row, s, NEG_INF)
            m_prev = m_ref[c]
            m_new = jnp.maximum(m_prev, jnp.max(s, axis=1, keepdims=True))
            alpha = jnp.exp(m_prev - m_new)
            p = jnp.exp(s - m_new)
            l_ref[c] = alpha * l_ref[c] + jnp.sum(p, axis=1, keepdims=True)
            acc_ref[c] = alpha * acc_ref[c] + _dot(p.astype(BF16), v)
            m_ref[c] = m_new

    @pl.when(j < i)
    def _():
        step(False)

    @pl.when(j == i)
    def _():
        step(True)
        o = acc_ref[0] / l_ref[0]
        if diff:
            o = o - lam_ref[0] * (acc_ref[1] / l_ref[1])
            o = _rms(o, gain_ref[...]) * out_scale
        o_ref[0] = o.astype(o_ref.dtype)


def flash_attention(lam, qs, ks, v, gain, n_heads, dv, *, diff, scale, out_scale=1.0, tile=512):
    b, s, _ = v.shape
    nt = s // tile
    n_qk = len(qs)
    in_specs = [pl.BlockSpec(memory_space=pltpu.SMEM)]
    for q in qs:
        w = q.shape[2] // n_heads
        in_specs.append(pl.BlockSpec((1, tile, w), lambda bb, h, i, j: (bb, i, h)))
    for q, k in zip(qs, ks):
        w = q.shape[2] // n_heads
        if k.shape[2] == w:
            in_specs.append(pl.BlockSpec((1, tile, w), lambda bb, h, i, j: (bb, jnp.minimum(i, j), 0)))
        else:
            in_specs.append(pl.BlockSpec((1, tile, w), lambda bb, h, i, j: (bb, jnp.minimum(i, j), h)))
    in_specs.append(pl.BlockSpec((1, tile, dv), lambda bb, h, i, j: (bb, jnp.minimum(i, j), h)))
    in_specs.append(pl.BlockSpec((1, dv), lambda bb, h, i, j: (0, 0)))
    n_sm = 2 if diff else 1
    return pl.pallas_call(
        functools.partial(_flash_body, n_qk=n_qk, diff=diff, scale=scale, out_scale=out_scale),
        grid=(b, n_heads, nt, nt), in_specs=in_specs,
        out_specs=pl.BlockSpec((1, tile, dv), lambda bb, h, i, j: (bb, i, h)),
        out_shape=jax.ShapeDtypeStruct((b, s, n_heads * dv), BF16),
        scratch_shapes=[pltpu.VMEM((n_sm, tile, 1), F32), pltpu.VMEM((n_sm, tile, 1), F32),
                        pltpu.VMEM((n_sm, tile, dv), F32)],
        compiler_params=_cparams(("parallel", "parallel", "parallel", "arbitrary")),
        name="flash_diff" if diff else "flash_plain",
    )(lam, *qs, *ks, v, gain.reshape(1, dv).astype(F32))


def _ssd_body(xbc_ref, z_ref, dt_ref, dtt_ref, cw_ref, cb_ref, dtb_ref, dtbt_ref, al_ref, alt_ref,
              dsk_ref, ng_ref, o_ref, xpad_ref, state_ref):
    chunk = xbc_ref.shape[1]
    d_in = z_ref.shape[2]
    gn = SSM_GROUPS * SSM_STATE
    c = pl.program_id(1)

    @pl.when(c == 0)
    def _():
        xpad_ref[0:8, :] = jnp.zeros((8, xpad_ref.shape[1]), F32)
        state_ref[...] = jnp.zeros(state_ref.shape, F32)

    xpad_ref[8:8 + chunk, :] = xbc_ref[0]
    conv = cb_ref[...]
    for w in range(SSM_CONV):
        conv = conv + cw_ref[w:w + 1, :] * xpad_ref[pl.ds(8 - (SSM_CONV - 1) + w, chunk), :]
    xpad_ref[0:8, :] = xpad_ref[chunk:chunk + 8, :]
    u = _silu(conv)
    xs = u[:, :d_in]
    bmat = u[:, d_in:d_in + gn]
    cmat = u[:, d_in + gn:]

    dt = _softplus(dt_ref[0] + dtb_ref[...])
    ad = dt * (-jnp.exp(al_ref[...]))
    dtt = _softplus(dtt_ref[0] + dtbt_ref[...])
    adt = dtt * (-jnp.exp(alt_ref[...]))
    row = lax.broadcasted_iota(jnp.int32, (chunk, chunk), 0)
    col = lax.broadcasted_iota(jnp.int32, (chunk, chunk), 1)
    lower = row >= col
    tril = jnp.where(lower, 1.0, 0.0).astype(BF16)
    triu = jnp.where(row <= col, 1.0, 0.0).astype(BF16)
    cs = sum(_dot(tril, part) for part in _split_bf16(ad, 3))
    cst = sum(_dot(part, triu) for part in _split_bf16(adt, 3))

    heads_per_group = SSM_HEADS // SSM_GROUPS
    dsk = dsk_ref[...]
    ys = []
    for g in range(SSM_GROUPS):
        bg = bmat[:, g * SSM_STATE:(g + 1) * SSM_STATE]
        cg = cmat[:, g * SSM_STATE:(g + 1) * SSM_STATE].astype(BF16)
        cb = _dot_nt(cg, bg.astype(BF16))
        bgt = bg.T.astype(BF16)
        for r in range(heads_per_group):
            h = g * heads_per_group + r
            ccol = cs[:, h:h + 1]
            crow = cst[h:h + 1, :]
            decay = jnp.exp(jnp.where(lower, ccol - crow, NEG_INF))
            x_h = xs[:, h * SSM_HEAD_DIM:(h + 1) * SSM_HEAD_DIM]
            xdt = x_h * dt[:, h:h + 1]
            y = _dot((cb * decay).astype(BF16), xdt.astype(BF16))
            st = state_ref[h]
            y = y + _dot(cg, st.astype(BF16)) * jnp.exp(ccol)
            last = cst[h:h + 1, chunk - 1:chunk]
            to_end = jnp.exp(last - ccol)
            state_ref[h] = st * jnp.exp(last) + _dot(bgt, (xdt * to_end).astype(BF16))
            ys.append(y + x_h * dsk[:, h * SSM_HEAD_DIM:(h + 1) * SSM_HEAD_DIM])

    y = jnp.concatenate(ys, axis=1) * _silu(z_ref[0])
    gw = d_in // SSM_GROUPS
    for g in range(SSM_GROUPS):
        seg = y[:, g * gw:(g + 1) * gw]
        o_ref[0, :, g * gw:(g + 1) * gw] = _rms(seg, ng_ref[:, g * gw:(g + 1) * gw]).astype(o_ref.dtype)


def ssd_mixer(xbc, z, dt_raw, conv_w, conv_b, dt_bias, a_log, d_skip, norm_gain):
    b, s, cch = xbc.shape
    d_in = z.shape[2]
    nc = s // SSM_CHUNK
    hpad = dt_raw.shape[2]
    dtt = jnp.transpose(dt_raw[:, :, :SSM_HEADS], (0, 2, 1))

    def lane_pad(v):
        return jnp.pad(v.astype(F32), (0, hpad - SSM_HEADS)).reshape(1, hpad)

    args = (xbc, z, dt_raw, dtt, conv_w.astype(F32), conv_b.reshape(1, cch).astype(F32),
            lane_pad(dt_bias), dt_bias.reshape(SSM_HEADS, 1).astype(F32),
            lane_pad(a_log), a_log.reshape(SSM_HEADS, 1).astype(F32),
            jnp.repeat(d_skip.astype(F32), SSM_HEAD_DIM).reshape(1, d_in),
            norm_gain.reshape(1, d_in).astype(F32))
    const = lambda bb, c: (0, 0)
    in_specs = [pl.BlockSpec((1, SSM_CHUNK, cch), lambda bb, c: (bb, c, 0)),
                pl.BlockSpec((1, SSM_CHUNK, d_in), lambda bb, c: (bb, c, 0)),
                pl.BlockSpec((1, SSM_CHUNK, hpad), lambda bb, c: (bb, c, 0)),
                pl.BlockSpec((1, SSM_HEADS, SSM_CHUNK), lambda bb, c: (bb, 0, c)),
                pl.BlockSpec((SSM_CONV, cch), const), pl.BlockSpec((1, cch), const),
                pl.BlockSpec((1, hpad), const), pl.BlockSpec((SSM_HEADS, 1), const),
                pl.BlockSpec((1, hpad), const), pl.BlockSpec((SSM_HEADS, 1), const),
                pl.BlockSpec((1, d_in), const), pl.BlockSpec((1, d_in), const)]
    return pl.pallas_call(
        _ssd_body, grid=(b, nc), in_specs=in_specs,
        out_specs=pl.BlockSpec((1, SSM_CHUNK, d_in), lambda bb, c: (bb, c, 0)),
        out_shape=jax.ShapeDtypeStruct((b, s, d_in), BF16),
        scratch_shapes=[pltpu.VMEM((SSM_CHUNK + 8, cch), F32),
                        pltpu.VMEM((SSM_HEADS, SSM_STATE, SSM_HEAD_DIM), F32)],
        compiler_params=_cparams(("parallel", "arbitrary")), name="ssd_mixer",
    )(*args)


def _out_proj_body(x_ref, a_ref, b_ref, wa_ref, wb_ref, o_ref):
    o_ref[...] = x_ref[...] + _dot(a_ref[...], wa_ref[...]) + _dot(b_ref[...], wb_ref[...])


def out_proj_residual(x2, a, bm, wa, wb, tm=512):
    t, d = x2.shape
    return pl.pallas_call(
        _out_proj_body, grid=(t // tm,),
        in_specs=[pl.BlockSpec((tm, d), lambda i: (i, 0)),
                  pl.BlockSpec((tm, a.shape[1]), lambda i: (i, 0)),
                  pl.BlockSpec((tm, bm.shape[1]), lambda i: (i, 0)),
                  pl.BlockSpec(wa.shape, lambda i: (0, 0)),
                  pl.BlockSpec(wb.shape, lambda i: (0, 0))],
        out_specs=pl.BlockSpec((tm, d), lambda i: (i, 0)),
        out_shape=jax.ShapeDtypeStruct((t, d), F32),
        compiler_params=_cparams(("parallel",)), name="out_proj",
    )(x2, a, bm, wa, wb)


def _ffn_body(x_ref, g_ref, *refs, gated):
    if gated:
        gate_ref, wg_ref, wu_ref, wd_ref, o_ref, h_ref, acc_ref = refs
    else:
        wg_ref, wu_ref, wd_ref, o_ref, h_ref, acc_ref = refs
    e = pl.program_id(1)
    f = pl.program_id(2)

    @pl.when((e == 0) & (f == 0))
    def _():
        h_ref[...] = _rms(x_ref[...], g_ref[...]).astype(BF16)
        acc_ref[...] = jnp.zeros(acc_ref.shape, F32)

    h = h_ref[...]
    act = (_silu(_dot(h, wg_ref[0])) * _dot(h, wu_ref[0])).astype(BF16)
    y = _dot(act, wd_ref[0])
    if gated:
        gates = gate_ref[...]
        lane = lax.broadcasted_iota(jnp.int32, gates.shape, 1)
        y = y * jnp.sum(jnp.where(lane == e, gates, 0.0), axis=1, keepdims=True)
    acc_ref[...] += y

    @pl.when((e == pl.num_programs(1) - 1) & (f == pl.num_programs(2) - 1))
    def _():
        o_ref[...] = x_ref[...] + acc_ref[...]


def ffn_residual(x2, gain, w_gate, w_up, w_down, gates=None, tm=512, tf=1408):
    t, d = x2.shape
    n_e, _, d_ff = w_gate.shape
    gated = gates is not None
    in_specs = [pl.BlockSpec((tm, d), lambda i, e, f: (i, 0)),
                pl.BlockSpec((1, d), lambda i, e, f: (0, 0))]
    args = [x2, gain.reshape(1, d).astype(F32)]
    if gated:
        in_specs.append(pl.BlockSpec((tm, LANES), lambda i, e, f: (i, 0)))
        args.append(gates)
    in_specs += [pl.BlockSpec((1, d, tf), lambda i, e, f: (e, 0, f)),
                 pl.BlockSpec((1, d, tf), lambda i, e, f: (e, 0, f)),
                 pl.BlockSpec((1, tf, d), lambda i, e, f: (e, f, 0))]
    args += [w_gate, w_up, w_down]
    return pl.pallas_call(
        functools.partial(_ffn_body, gated=gated),
        grid=(t // tm, n_e, d_ff // tf), in_specs=in_specs,
        out_specs=pl.BlockSpec((tm, d), lambda i, e, f: (i, 0)),
        out_shape=jax.ShapeDtypeStruct((t, d), F32),
        scratch_shapes=[pltpu.VMEM((tm, d), BF16), pltpu.VMEM((tm, d), F32)],
        compiler_params=_cparams(("parallel", "arbitrary", "arbitrary")),
        name="moe_ffn" if gated else "ffn",
    )(*args)


def _router_body(x_ref, g_ref, r_ref, o_ref):
    h = _rms(x_ref[...], g_ref[...])
    logits = jnp.dot(h, r_ref[...], precision=lax.Precision.HIGHEST, preferred_element_type=F32)
    lane = lax.broadcasted_iota(jnp.int32, logits.shape, 1).astype(F32)
    low = jnp.float32(-3.0e38)
    logits = jnp.where(lane < N_EXPERTS, logits, low)
    m1 = jnp.max(logits, axis=1, keepdims=True)
    i1 = jnp.min(jnp.where(logits == m1, lane, float(LANES)), axis=1, keepdims=True)
    rest = jnp.where(lane == i1, low, logits)
    m2 = jnp.max(rest, axis=1, keepdims=True)
    i2 = jnp.min(jnp.where(rest == m2, lane, float(LANES)), axis=1, keepdims=True)
    ex = jnp.exp(m2 - m1)
    w1 = 1.0 / (1.0 + ex)
    w2 = ex / (1.0 + ex)
    o_ref[...] = jnp.where(lane == i1, w1, jnp.where(lane == i2, w2, 0.0))


def router_gates(x2, gain, router, tm=512):
    t, d = x2.shape
    r_pad = jnp.pad(router.astype(F32), ((0, 0), (0, LANES - router.shape[1])))
    return pl.pallas_call(
        _router_body, grid=(t // tm,),
        in_specs=[pl.BlockSpec((tm, d), lambda i: (i, 0)),
                  pl.BlockSpec((1, d), lambda i: (0, 0)),
                  pl.BlockSpec((d, LANES), lambda i: (0, 0))],
        out_specs=pl.BlockSpec((tm, LANES), lambda i: (i, 0)),
        out_shape=jax.ShapeDtypeStruct((t, LANES), F32),
        compiler_params=_cparams(("parallel",)), name="router",
    )(x2, gain.reshape(1, d).astype(F32), r_pad)


def _compress_body(ch_ref, nx_ref, pos_ref, w1_ref, w2_ref, gain_ref, cos_ref, sin_ref, rot_ref, o_ref,
                   *, is_key):
    a = _dot((ch_ref[0, 0] + pos_ref[0]).astype(BF16), w1_ref[0])
    a = a + _dot((nx_ref[0, 0] + pos_ref[1]).astype(BF16), w1_ref[1])
    out = _dot(_silu(a).astype(BF16), w2_ref[...])
    if is_key:
        out = _rms(out, gain_ref[...])
        hi, lo = _split_bf16(out, 2)
        partner = _dot(hi, rot_ref[...]) + _dot(lo, rot_ref[...])
        out = out * cos_ref[...] + partner * sin_ref[...]
    o_ref[0, 0] = out


def nsa_compress(t, pos, w1, w2, gain, seq, is_key):
    b, s, _ = t.shape
    g, d = NSA_KV_GROUPS, NSA_HEAD_DIM
    n_ch = s // NSA_CMP_STRIDE
    half = NSA_CMP_STRIDE * d
    ch = t.reshape(b, n_ch, NSA_CMP_STRIDE, g, d).transpose(0, 3, 1, 2, 4).reshape(b, g, n_ch, half)
    nxt = jnp.concatenate([ch[:, :, 1:], jnp.zeros((b, g, 1, half), F32)], axis=2)
    pos2 = pos.astype(F32).reshape(2, 1, half)
    w1s = w1.astype(BF16).reshape(2, half, d)
    cmp_end = jnp.arange(n_ch) * NSA_CMP_STRIDE + NSA_CMP_BLOCK - 1
    inv_freq = 1.0 / (ROPE_THETA ** (jnp.arange(0, d, 2, dtype=F32) / d))
    ang = cmp_end.astype(F32)[:, None] * inv_freq[None, :]
    cos = jnp.concatenate([jnp.cos(ang)] * 2, axis=1)
    sin = jnp.concatenate([jnp.sin(ang)] * 2, axis=1)
    rot = np.zeros((d, d), np.float32)
    rot[np.arange(d // 2) + d // 2, np.arange(d // 2)] = -1.0
    rot[np.arange(d // 2), np.arange(d // 2) + d // 2] = 1.0
    blk = lambda bb, gg: (bb, gg, 0, 0)
    c2 = lambda bb, gg: (0, 0)
    c3 = lambda bb, gg: (0, 0, 0)
    return pl.pallas_call(
        functools.partial(_compress_body, is_key=is_key), grid=(b, g),
        in_specs=[pl.BlockSpec((1, 1, n_ch, half), blk), pl.BlockSpec((1, 1, n_ch, half), blk),
                  pl.BlockSpec((2, 1, half), c3), pl.BlockSpec((2, half, d), c3),
                  pl.BlockSpec((d, d), c2), pl.BlockSpec((1, d), c2),
                  pl.BlockSpec((n_ch, d), c2), pl.BlockSpec((n_ch, d), c2), pl.BlockSpec((d, d), c2)],
        out_specs=pl.BlockSpec((1, 1, n_ch, d), blk),
        out_shape=jax.ShapeDtypeStruct((b, g, n_ch, d), F32),
        compiler_params=_cparams(("parallel", "parallel")), name="nsa_compress",
    )(ch, nxt, pos2, w1s, w2.astype(BF16), gain.reshape(1, d).astype(F32), cos, sin,
      jnp.asarray(rot, dtype=BF16))


def _nsa_body(q_ref, ck_ref, cvt_ref, ksl_ref, vslt_ref, kwn_ref, vwnt_ref, ovt_ref, glt_ref, o_ref,
              imp_ref, sel_ref, *, tq, scale):
    g = pl.program_id(1)
    i = pl.program_id(2)
    d = NSA_HEAD_DIM
    rep = NSA_HEADS // NSA_KV_GROUPS
    t0 = i * tq
    n_cmp = ck_ref.shape[1]
    n_sel = ovt_ref.shape[0]
    width = rep * tq

    qb = q_ref[0]
    q4 = jnp.concatenate([qb[:, r * d:(r + 1) * d] for r in range(rep)], axis=0)
    q4 = jnp.concatenate([q4, q4], axis=1)
    lane = lax.broadcasted_iota(jnp.int32, q4.shape, 1)
    q4 = jnp.where(jnp.right_shift(lane, d.bit_length() - 1) == g, q4, jnp.zeros_like(q4))

    def qpos_of(shape):
        return t0 + (lax.broadcasted_iota(jnp.int32, shape, 1) & (tq - 1))

    sc = _dot_nt(ck_ref[0], q4) * scale
    cmp_end = lax.broadcasted_iota(jnp.int32, sc.shape, 0) * NSA_CMP_STRIDE + (NSA_CMP_BLOCK - 1)
    valid = cmp_end <= qpos_of(sc.shape)
    sc = jnp.where(valid, sc, NEG_INF)
    e = jnp.exp(sc - jnp.max(sc, axis=0, keepdims=True))
    p = jnp.where(valid, e / jnp.sum(e, axis=0, keepdims=True), 0.0)
    cvt = cvt_ref[0, pl.ds(pl.multiple_of(g * d, d), d), :]
    o_c = _dot(cvt, p.astype(BF16))

    psum = p[:, 0:tq]
    for r in range(1, rep):
        psum = psum + p[:, r * tq:(r + 1) * tq]
    hi, lo = _split_bf16(psum, 2)
    imp = _dot(ovt_ref[...], hi) + _dot(ovt_ref[...], lo)
    blk = lax.broadcasted_iota(jnp.int32, imp.shape, 0)
    qp = t0 + lax.broadcasted_iota(jnp.int32, imp.shape, 1)
    cur = jnp.right_shift(qp, NSA_SEL_BLOCK.bit_length() - 1)
    forced = (blk == 0) | (blk == cur) | (blk == cur - 1)
    future = blk * NSA_SEL_BLOCK > qp
    imp = jnp.where(future, -FORCE_SCORE, jnp.where(forced, FORCE_SCORE, imp))
    imp_ref[...] = imp

    def count(i2, cnt):
        other = imp_ref[pl.ds(i2, 1), :]
        beats = (other > imp) | ((other == imp) & (blk > i2))
        return cnt + jnp.where(beats, 1.0, 0.0)

    rank = lax.fori_loop(0, n_sel, count, jnp.zeros(imp.shape, F32))
    sel = jnp.where(rank < float(min(NSA_TOP_N, n_sel)), 1.0, 0.0)
    sel_ref[...] = jnp.concatenate([sel] * rep, axis=1)

    def online_softmax(n_lo, n_hi, chunk, k_ref, vt_ref, mask_fn):
        def body(c, carry):
            m_prev, l_prev, acc = carry
            start = pl.multiple_of(c * chunk, chunk)
            s = _dot_nt(k_ref[0, pl.ds(start, chunk), :], q4) * scale
            kpos = start + lax.broadcasted_iota(jnp.int32, s.shape, 0)
            mask = mask_fn(c, kpos, qpos_of(s.shape))
            s = jnp.where(mask, s, NEG_INF)
            m_new = jnp.maximum(m_prev, jnp.max(s, axis=0, keepdims=True))
            alpha = jnp.exp(m_prev - m_new)
            pr = jnp.where(mask, jnp.exp(s - m_new), 0.0)
            l_new = alpha * l_prev + jnp.sum(pr, axis=0, keepdims=True)
            vt = vt_ref[0, pl.ds(pl.multiple_of(g * d, d), d), pl.ds(start, chunk)]
            return m_new, l_new, alpha * acc + _dot(vt, pr.astype(BF16))

        init = (jnp.full((1, width), NEG_INF, F32), jnp.zeros((1, width), F32),
                jnp.zeros((d, width), F32))
        _, l_fin, acc = lax.fori_loop(n_lo, n_hi, body, init)
        return acc / l_fin

    sel_chunk = 4 * NSA_SEL_BLOCK

    def sel_mask(c, kpos, qpos):
        rows = [jnp.broadcast_to(sel_ref[pl.ds(c * 4 + jb, 1), :], (NSA_SEL_BLOCK, width))
                for jb in range(4)]
        return (jnp.concatenate(rows, axis=0) > 0.5) & (kpos <= qpos)

    o_s = online_softmax(0, (t0 + tq + sel_chunk - 1) // sel_chunk, sel_chunk, ksl_ref, vslt_ref, sel_mask)

    def win_mask(c, kpos, qpos):
        return (kpos <= qpos) & (kpos > qpos - NSA_WINDOW)

    o_w = online_softmax(jnp.maximum(i - NSA_WINDOW // tq, 0), i + 1, tq, kwn_ref, vwnt_ref, win_mask)

    def gate(branch):
        rows = [glt_ref[0, pl.ds((g * rep + r) * 3 + branch, 1), :] for r in range(rep)]
        return _sigmoid(jnp.concatenate(rows, axis=1))

    o_ref[0, 0, 0] = gate(0) * o_c + gate(1) * o_s + gate(2) * o_w


def nsa_overlap_t(n_cmp, n_sel):
    c_start = np.arange(n_cmp)[None, :] * NSA_CMP_STRIDE
    s_start = np.arange(n_sel)[:, None] * NSA_SEL_BLOCK
    hit = (c_start < s_start + NSA_SEL_BLOCK) & (c_start + NSA_CMP_BLOCK > s_start)
    hit = hit & (np.arange(n_cmp)[None, :] < n_cmp - NSA_CMP_BLOCK // NSA_CMP_STRIDE + 1)
    return jnp.asarray(hit.astype(np.float32), dtype=BF16)


def nsa_attention(qn, ck, cvt, ksl, vslt, kwn, vwnt, glt, tq=128):
    b, s, _ = qn.shape
    g, d = NSA_KV_GROUPS, NSA_HEAD_DIM
    rep = NSA_HEADS // g
    n_cmp = ck.shape[1]
    n_sel = s // NSA_SEL_BLOCK
    nq = s // tq
    ovt = nsa_overlap_t(n_cmp, n_sel)
    full3 = lambda bb, gg, i: (bb, 0, 0)
    return pl.pallas_call(
        functools.partial(_nsa_body, tq=tq, scale=d ** -0.5), grid=(b, g, nq),
        in_specs=[pl.BlockSpec((1, tq, rep * d), lambda bb, gg, i: (bb, i, gg)),
                  pl.BlockSpec((1, n_cmp, g * d), full3), pl.BlockSpec((1, g * d, n_cmp), full3),
                  pl.BlockSpec((1, s, g * d), full3), pl.BlockSpec((1, g * d, s), full3),
                  pl.BlockSpec((1, s, g * d), full3), pl.BlockSpec((1, g * d, s), full3),
                  pl.BlockSpec((n_sel, n_cmp), lambda bb, gg, i: (0, 0)),
                  pl.BlockSpec((1, glt.shape[1], tq), lambda bb, gg, i: (bb, 0, i))],
        out_specs=pl.BlockSpec((1, 1, 1, d, rep * tq), lambda bb, gg, i: (bb, gg, i, 0, 0)),
        out_shape=jax.ShapeDtypeStruct((b, g, nq, d, rep * tq), F32),
        scratch_shapes=[pltpu.VMEM((n_sel, tq), F32), pltpu.VMEM((n_sel, rep * tq), F32)],
        compiler_params=_cparams(("parallel", "parallel", "arbitrary")), name="nsa_attention",
    )(qn, ck, cvt, ksl, vslt, kwn, vwnt, ovt, glt)


def _pad_cols(w, n):
    return jnp.pad(w, ((0, 0), (0, n - w.shape[1])))


def _even_layer(x2, b, s, layer_idx, norm_mix, w_in, q_gain, k_gain, lam, subln_gain, conv_w, conv_b,
                dt_bias, a_log, d_skip, ssm_norm_gain, w_out, norm_ffn, w_gate, w_up, w_down):
    nq = DA_HEADS * 2 * DA_HEAD_DIM
    nv = DA_HEADS * DA_V_DIM
    cch = SSM_D_INNER + 2 * SSM_GROUPS * SSM_STATE
    offs = np.cumsum([0, nq, nq, nv, SSM_D_INNER, cch, SSM_HEADS])
    wb = w_in.astype(BF16)
    pieces = [wb[:, offs[k]:offs[k + 1]] for k in range(6)]
    pieces[5] = _pad_cols(pieces[5], LANES)
    q, k, v, z, xbc, dt = norm_proj(x2, norm_mix, pieces, [F32, F32, BF16, F32, F32, F32])

    tables = _rope_tables(s, DA_HEAD_DIM)
    qn = head_norm(q, q_gain, DA_HEAD_DIM, s, tables).reshape(b, s, nq)
    kn = head_norm(k, k_gain, DA_HEAD_DIM, s, tables).reshape(b, s, nq)
    lam_init = 0.8 - 0.6 * math.exp(-0.3 * layer_idx)
    lf = lam.astype(F32)
    lam_full = jnp.exp(jnp.sum(lf[0] * lf[1])) - jnp.exp(jnp.sum(lf[2] * lf[3])) + lam_init
    a_out = flash_attention(lam_full.reshape(1), [qn], [kn], v.reshape(b, s, nv), subln_gain,
                            DA_HEADS, DA_V_DIM, diff=True, scale=DA_HEAD_DIM ** -0.5,
                            out_scale=1.0 - lam_init)
    b_out = ssd_mixer(xbc.reshape(b, s, cch), z.reshape(b, s, SSM_D_INNER), dt.reshape(b, s, LANES),
                      conv_w, conv_b, dt_bias, a_log, d_skip, ssm_norm_gain)
    wo = w_out.astype(BF16)
    x2 = out_proj_residual(x2, a_out.reshape(-1, nv), b_out.reshape(-1, SSM_D_INNER), wo[:nv], wo[nv:])
    return ffn_residual(x2, norm_ffn, w_gate.astype(BF16)[None], w_up.astype(BF16)[None],
                        w_down.astype(BF16)[None])


def _odd_layer(x2, b, s, norm_mix, w_in, q_gain, k_gain, cmp_pos, cmp_w1, cmp_w2, cq_gain, ckv_gain,
               w_uq, w_ukv, qn_gain, qr_gain, kn_gain, kr_gain, w_out, norm_ffn, router, w_gate, w_up,
               w_down):
    g, d = NSA_KV_GROUPS, NSA_HEAD_DIM
    nq = NSA_HEADS * d
    nkv = g * d
    sizes = [nq] + [nkv] * 6 + [NSA_HEADS * 3, w_uq.shape[0], w_ukv.shape[0], MLA_ROPE_DIM]
    offs = np.cumsum([0] + sizes)
    wb = w_in.astype(BF16)
    pieces = [wb[:, offs[k]:offs[k + 1]] for k in range(len(sizes))]
    pieces[7] = _pad_cols(pieces[7], LANES)
    pieces[10] = _pad_cols(pieces[10], LANES)
    (q, kc, vc, ksl, vsl, kwn, vwn, gl, cq, ckv, kr) = norm_proj(
        x2, norm_mix, pieces, [F32, F32, F32, F32, BF16, F32, BF16, F32, F32, F32, F32])

    tables = _rope_tables(s, d)
    qn = head_norm(q, q_gain, d, s, tables).reshape(b, s, nq)
    ksl_n = head_norm(ksl, k_gain[1], d, s, tables).reshape(b, s, nkv)
    kwn_n = head_norm(kwn, k_gain[2], d, s, tables).reshape(b, s, nkv)
    ck = nsa_compress(kc.reshape(b, s, nkv), cmp_pos[0], cmp_w1[0], cmp_w2[0], k_gain[0], s, True)
    cv = nsa_compress(vc.reshape(b, s, nkv), cmp_pos[1], cmp_w1[1], cmp_w2[1], k_gain[0], s, False)
    n_cmp = ck.shape[2]
    ck = ck.transpose(0, 2, 1, 3).reshape(b, n_cmp, nkv).astype(BF16)
    cvt = cv.transpose(0, 1, 3, 2).reshape(b, nkv, n_cmp).astype(BF16)
    vslt = vsl.reshape(b, s, nkv).transpose(0, 2, 1)
    vwnt = vwn.reshape(b, s, nkv).transpose(0, 2, 1)
    glt = gl.reshape(b, s, LANES)[:, :, :32].transpose(0, 2, 1)
    tq = 128
    o = nsa_attention(qn, ck, cvt, ksl_n, vslt, kwn_n, vwnt, glt, tq=tq)
    rep = NSA_HEADS // g
    c_out = o.reshape(b, g, s // tq, d, rep, tq).transpose(0, 2, 5, 1, 4, 3).reshape(b * s, nq)
    c_out = c_out.astype(BF16)

    h = MLA_HEADS
    dqk = MLA_NOPE_DIM + MLA_ROPE_DIM
    wq = w_uq.astype(BF16).reshape(-1, h, dqk)
    wq_nope = wq[:, :, :MLA_NOPE_DIM].reshape(-1, h * MLA_NOPE_DIM)
    wq_rope = jnp.pad(wq[:, :, MLA_NOPE_DIM:], ((0, 0), (0, 0), (0, LANES - MLA_ROPE_DIM)))
    wq_rope = wq_rope.reshape(-1, h * LANES)
    wkv = w_ukv.astype(BF16).reshape(-1, h, MLA_NOPE_DIM + MLA_V_DIM)
    wk_nope = wkv[:, :, :MLA_NOPE_DIM].reshape(-1, h * MLA_NOPE_DIM)
    wv = wkv[:, :, MLA_NOPE_DIM:].reshape(-1, h * MLA_V_DIM)
    q_nope, q_rope = norm_proj(cq, cq_gain, [wq_nope, wq_rope], [F32, F32])
    k_nope, v = norm_proj(ckv, ckv_gain, [wk_nope, wv], [F32, BF16])
    q_nope = head_norm(q_nope, qn_gain, MLA_NOPE_DIM, s)
    k_nope = head_norm(k_nope, kn_gain, MLA_NOPE_DIM, s)
    q_rope = head_norm(q_rope, qr_gain, MLA_ROPE_DIM, s, tables)
    k_rope = head_norm(kr, kr_gain, MLA_ROPE_DIM, s, tables)
    shp = lambda t: t.reshape(b, s, t.shape[-1])
    d_out = flash_attention(jnp.zeros((1,), F32), [shp(q_nope), shp(q_rope)], [shp(k_nope), shp(k_rope)],
                            shp(v), jnp.ones((MLA_V_DIM,), F32), h, MLA_V_DIM, diff=False,
                            scale=dqk ** -0.5)

    wo = w_out.astype(BF16)
    x2 = out_proj_residual(x2, c_out, d_out.reshape(b * s, h * MLA_V_DIM), wo[:nq], wo[nq:])
    gates = router_gates(x2, norm_ffn, router)
    return ffn_residual(x2, norm_ffn, w_gate.astype(BF16), w_up.astype(BF16), w_down.astype(BF16),
                        gates=gates)


def kernel(x, ev_norm_mix, ev_w_in, da_q_gain, da_k_gain, da_lambda, da_subln_gain, ssm_conv_w, ssm_conv_b, ssm_dt_bias, ssm_a_log, ssm_d, ssm_norm_gain, ev_w_out, ev_norm_ffn, ffn_w_gate, ffn_w_up, ffn_w_down, od_norm_mix, od_w_in, nsa_q_gain, nsa_k_gain, nsa_cmp_pos, nsa_cmp_w1, nsa_cmp_w2, mla_cq_gain, mla_ckv_gain, mla_w_uq, mla_w_ukv, mla_qn_gain, mla_qr_gain, mla_kn_gain, mla_kr_gain, od_w_out, od_norm_ffn, moe_router, moe_w_gate, moe_w_up, moe_w_down):
    b, s, d = x.shape
    x2 = x.reshape(b * s, d)
    depth = ev_norm_mix.shape[0] + od_norm_mix.shape[0]
    for layer in range(depth):
        i = layer // 2
        if layer % 2 == 0:
            x2 = _even_layer(x2, b, s, layer, ev_norm_mix[i], ev_w_in[i], da_q_gain[i], da_k_gain[i],
                             da_lambda[i], da_subln_gain[i], ssm_conv_w[i], ssm_conv_b[i],
                             ssm_dt_bias[i], ssm_a_log[i], ssm_d[i], ssm_norm_gain[i], ev_w_out[i],
                             ev_norm_ffn[i], ffn_w_gate[i], ffn_w_up[i], ffn_w_down[i])
        else:
            x2 = _odd_layer(x2, b, s, od_norm_mix[i], od_w_in[i], nsa_q_gain[i], nsa_k_gain[i],
                            nsa_cmp_pos[i], nsa_cmp_w1[i], nsa_cmp_w2[i], mla_cq_gain[i],
                            mla_ckv_gain[i], mla_w_uq[i], mla_w_ukv[i], mla_qn_gain[i], mla_qr_gain[i],
                            mla_kn_gain[i], mla_kr_gain[i], od_w_out[i], od_norm_ffn[i], moe_router[i],
                            moe_w_gate[i], moe_w_up[i], moe_w_down[i])
    return x2.reshape(b, s, d)
```

```python
import functools
import math

import numpy as np
import jax
import jax.numpy as jnp
from jax import lax
from jax.experimental import pallas as pl
from jax.experimental.pallas import tpu as pltpu

F32 = jnp.float32
BF16 = jnp.bfloat16

ROPE_THETA = 10000.0
NORM_EPS = 1e-6
NEG_INF = -1e30
FORCE_SCORE = 1e6
LOG2E = 1.4426950408889634

DA_HEADS = 4
DA_HEAD_DIM = 64
DA_V_DIM = 2 * DA_HEAD_DIM
SSM_HEADS = 8
SSM_HEAD_DIM = 64
SSM_D_INNER = SSM_HEADS * SSM_HEAD_DIM
SSM_GROUPS = 2
SSM_STATE = 128
SSM_CONV = 4
SSM_CHUNK = 256
NSA_HEADS = 8
NSA_KV_GROUPS = 2
NSA_HEAD_DIM = 64
NSA_CMP_BLOCK = 32
NSA_CMP_STRIDE = 16
NSA_SEL_BLOCK = 64
NSA_TOP_N = 16
NSA_WINDOW = 512
MLA_HEADS = 4
MLA_NOPE_DIM = 128
MLA_ROPE_DIM = 64
MLA_V_DIM = 128
N_EXPERTS = 8

LANES = 128
VMEM_LIMIT = 48 * 1024 * 1024
MOE_VMEM_LIMIT = 58 * 1024 * 1024

NT_DIMS = (((1,), (1,)), ((), ()))


def _cparams(semantics):
    return pltpu.CompilerParams(dimension_semantics=semantics, vmem_limit_bytes=VMEM_LIMIT)


def _dot(a, b):
    return jnp.dot(a, b, preferred_element_type=F32)


def _dot_nt(a, b):
    return lax.dot_general(a, b, NT_DIMS, preferred_element_type=F32)


def _split_bf16(x, parts):
    out = []
    for _ in range(parts):
        hi = x.astype(BF16)
        out.append(hi)
        x = x - hi.astype(F32)
    return out


def _sigmoid(x):
    return 1.0 / (1.0 + jnp.exp(-x))


def _silu(x):
    return x * _sigmoid(x)


def _softplus(x):
    return jnp.maximum(x, 0.0) + jnp.log(1.0 + jnp.exp(-jnp.abs(x)))


def _rms(x, gain):
    ms = jnp.mean(x * x, axis=-1, keepdims=True)
    return x * lax.rsqrt(ms + NORM_EPS) * gain


def _norm_proj_body(x_ref, g_ref, *refs, n_out):
    h = _rms(x_ref[...], g_ref[...]).astype(BF16)
    for w_ref, o_ref in zip(refs[:n_out], refs[n_out:]):
        o_ref[...] = _dot(h, w_ref[...]).astype(o_ref.dtype)


def norm_proj(x2, gain, weights, out_dtypes, tm=512):
    t, d = x2.shape
    n_out = len(weights)
    in_specs = [pl.BlockSpec((tm, d), lambda i: (i, 0)),
                pl.BlockSpec((1, d), lambda i: (0, 0))]
    in_specs += [pl.BlockSpec(w.shape, lambda i: (0, 0)) for w in weights]
    out_specs = [pl.BlockSpec((tm, w.shape[1]), lambda i: (i, 0)) for w in weights]
    out_shape = [jax.ShapeDtypeStruct((t, w.shape[1]), dt) for w, dt in zip(weights, out_dtypes)]
    return pl.pallas_call(
        functools.partial(_norm_proj_body, n_out=n_out),
        grid=(t // tm,), in_specs=in_specs, out_specs=out_specs, out_shape=out_shape,
        compiler_params=_cparams(("parallel",)), name="norm_proj",
    )(x2, gain.reshape(1, d), *weights)


def _head_norm_body(y_ref, gain_ref, bd_ref, *rest, hd, rope, out_mul):
    o_ref = rest[-1]
    y = y_ref[...]
    n = y.shape[1]
    hi, lo = _split_bf16(y * y, 2)
    ss = _dot(hi, bd_ref[...]) + _dot(lo, bd_ref[...])
    yn = y * lax.rsqrt(ss * (1.0 / hd) + NORM_EPS) * gain_ref[...]
    if rope:
        cos_ref, sin_ref = rest[0], rest[1]
        reps = n // LANES
        cos = jnp.concatenate([cos_ref[...]] * reps, axis=1) if reps > 1 else cos_ref[...]
        sin = jnp.concatenate([sin_ref[...]] * reps, axis=1) if reps > 1 else sin_ref[...]
        lane = lax.broadcasted_iota(jnp.int32, yn.shape, 1)
        first_half = (lane & (hd - 1)) < (hd // 2)
        partner = jnp.where(first_half, pltpu.roll(yn, n - hd // 2, 1), pltpu.roll(yn, hd // 2, 1))
        yn = yn * cos + partner * sin
    if out_mul != 1.0:
        yn = yn * out_mul
    o_ref[...] = yn.astype(o_ref.dtype)


def _block_diag_ones(n, hd):
    idx = np.arange(n) // hd
    return jnp.asarray((idx[:, None] == idx[None, :]).astype(np.float32), dtype=BF16)


def _rope_tables(seq, hd):
    inv_freq = 1.0 / (ROPE_THETA ** (jnp.arange(0, hd, 2, dtype=F32) / hd))
    ang = jnp.arange(seq, dtype=F32)[:, None] * inv_freq[None, :]
    cos, sin = jnp.cos(ang), jnp.sin(ang)
    reps = LANES // hd
    cos_t = jnp.tile(jnp.concatenate([cos, cos], axis=1), (1, reps))
    sin_t = jnp.tile(jnp.concatenate([-sin, sin], axis=1), (1, reps))
    return cos_t, sin_t


def head_norm(y, gain, hd, seq, rope_tables=None, out_mul=1.0, out_dtype=BF16, tm=512):
    t, n = y.shape
    gain_t = jnp.tile(gain.astype(F32), n // hd).reshape(1, n)
    args = [y, gain_t, _block_diag_ones(n, hd)]
    in_specs = [pl.BlockSpec((tm, n), lambda i: (i, 0)),
                pl.BlockSpec((1, n), lambda i: (0, 0)),
                pl.BlockSpec((n, n), lambda i: (0, 0))]
    if rope_tables is not None:
        per_seq = seq // tm
        args += list(rope_tables)
        in_specs += [pl.BlockSpec((tm, LANES), lambda i: (i % per_seq, 0))] * 2
    return pl.pallas_call(
        functools.partial(_head_norm_body, hd=hd, rope=rope_tables is not None, out_mul=out_mul),
        grid=(t // tm,), in_specs=in_specs,
        out_specs=pl.BlockSpec((tm, n), lambda i: (i, 0)),
        out_shape=jax.ShapeDtypeStruct((t, n), out_dtype),
        compiler_params=_cparams(("parallel",)), name="head_norm",
    )(*args)


def _flash_body(lam_ref, *refs, n_qk, diff, out_scale, sub):
    q_refs = refs[:n_qk]
    k_refs = refs[n_qk:2 * n_qk]
    vt_ref, gain_ref, o_ref, m_ref, l_ref, acc_ref, s_ref, p_ref = refs[2 * n_qk:]
    i = pl.program_id(2)
    j = pl.program_id(3)
    n_sm = 2 if diff else 1
    _, tk, tq = s_ref.shape

    @pl.when(j == 0)
    def _():
        m_ref[...] = jnp.full(m_ref.shape, NEG_INF, F32)
        l_ref[...] = jnp.zeros(l_ref.shape, F32)
        acc_ref[...] = jnp.zeros(acc_ref.shape, F32)

    def step(on_diagonal):
        qs = [r[0] for r in q_refs]
        ks = [r[0] for r in k_refs]
        q = qs[0] if n_qk == 1 else jnp.concatenate(qs, axis=1)
        k = ks[0] if n_qk == 1 else jnp.concatenate(ks, axis=1)
        vt = vt_ref[0]
        if diff:
            lane = lax.broadcasted_iota(jnp.int32, q.shape, 1)
            half = q.shape[1] // 2
            zero = jnp.zeros_like(q)
            q_parts = [jnp.where(lane < half, q, zero), jnp.where(lane >= half, q, zero)]
        else:
            q_parts = [q]
        m_news, alphas = [], []
        for c in range(n_sm):
            s = _dot_nt(k, q_parts[c])
            if on_diagonal:
                row = lax.broadcasted_iota(jnp.int32, s.shape, 0)
                col = lax.broadcasted_iota(jnp.int32, s.shape, 1)
                s = jnp.where(row <= col, s, NEG_INF)
            s_ref[c] = s
            m_prev = m_ref[c]
            m_new = jnp.maximum(m_prev, jnp.max(s, axis=0, keepdims=True))
            m_ref[c] = m_new
            m_news.append(m_new)
            alphas.append(jnp.exp2(m_prev - m_new))
        for c in range(n_sm):
            lpart = jnp.zeros((sub, tq), F32)
            for r in range(tk // sub):
                p = jnp.exp2(s_ref[c, r * sub:(r + 1) * sub, :] - m_news[c])
                lpart = lpart + p
                p_ref[c, r * sub:(r + 1) * sub, :] = p.astype(BF16)
            l_ref[c] = alphas[c] * l_ref[c] + jnp.sum(lpart, axis=0, keepdims=True)
            acc_ref[c] = alphas[c] * acc_ref[c] + _dot(vt, p_ref[c])

    @pl.when(j < i)
    def _():
        step(False)

    @pl.when(j == i)
    def _():
        step(True)
        o = acc_ref[0] / l_ref[0]
        if diff:
            o = o - lam_ref[0] * (acc_ref[1] / l_ref[1])
            ms = jnp.mean(o * o, axis=0, keepdims=True)
            o = o * lax.rsqrt(ms + NORM_EPS) * gain_ref[...] * out_scale
        o_ref[0] = o.T.astype(o_ref.dtype)


def flash_attention(lam, qs, ks, vt, gain, n_heads, dv, *, diff, out_scale=1.0, tile=1024, sub=64):
    b, _, s = vt.shape
    tile = min(tile, s)
    nt = s // tile
    n_qk = len(qs)
    in_specs = [pl.BlockSpec(memory_space=pltpu.SMEM)]
    for q in qs:
        w = q.shape[2] // n_heads
        in_specs.append(pl.BlockSpec((1, tile, w), lambda bb, h, i, j: (bb, i, h)))
    for q, k in zip(qs, ks):
        w = q.shape[2] // n_heads
        if k.shape[2] == w:
            in_specs.append(pl.BlockSpec((1, tile, w), lambda bb, h, i, j: (bb, jnp.minimum(i, j), 0)))
        else:
            in_specs.append(pl.BlockSpec((1, tile, w), lambda bb, h, i, j: (bb, jnp.minimum(i, j), h)))
    in_specs.append(pl.BlockSpec((1, dv, tile), lambda bb, h, i, j: (bb, h, jnp.minimum(i, j))))
    in_specs.append(pl.BlockSpec((dv, 1), lambda bb, h, i, j: (0, 0)))
    n_sm = 2 if diff else 1
    return pl.pallas_call(
        functools.partial(_flash_body, n_qk=n_qk, diff=diff, out_scale=out_scale, sub=sub),
        grid=(b, n_heads, nt, nt), in_specs=in_specs,
        out_specs=pl.BlockSpec((1, tile, dv), lambda bb, h, i, j: (bb, i, h)),
        out_shape=jax.ShapeDtypeStruct((b, s, n_heads * dv), BF16),
        scratch_shapes=[pltpu.VMEM((n_sm, 1, tile), F32), pltpu.VMEM((n_sm, 1, tile), F32),
                        pltpu.VMEM((n_sm, dv, tile), F32), pltpu.VMEM((n_sm, tile, tile), F32),
                        pltpu.VMEM((n_sm, tile, tile), BF16)],
        compiler_params=_cparams(("parallel", "parallel", "parallel", "arbitrary")),
        name="flash_diff" if diff else "flash_plain",
    )(lam, *qs, *ks, vt, gain.reshape(dv, 1).astype(F32))


def _ssd_body(xbc_ref, z_ref, dt_ref, dtt_ref, cw_ref, cb_ref, dtb_ref, dtbt_ref, al_ref, alt_ref,
              dsk_ref, ng_ref, o_ref, xpad_ref, state_ref):
    chunk = xbc_ref.shape[1]
    d_in = z_ref.shape[2]
    gn = SSM_GROUPS * SSM_STATE
    c = pl.program_id(1)

    @pl.when(c == 0)
    def _():
        xpad_ref[0:8, :] = jnp.zeros((8, xpad_ref.shape[1]), F32)
        state_ref[...] = jnp.zeros(state_ref.shape, F32)

    xpad_ref[8:8 + chunk, :] = xbc_ref[0]
    conv = cb_ref[...]
    for w in range(SSM_CONV):
        conv = conv + cw_ref[w:w + 1, :] * xpad_ref[pl.ds(8 - (SSM_CONV - 1) + w, chunk), :]
    xpad_ref[0:8, :] = xpad_ref[chunk:chunk + 8, :]
    u = _silu(conv)
    xs = u[:, :d_in]
    bmat = u[:, d_in:d_in + gn]
    cmat = u[:, d_in + gn:]

    dt = _softplus(dt_ref[0] + dtb_ref[...])
    ad = dt * (-jnp.exp(al_ref[...]))
    dtt = _softplus(dtt_ref[0] + dtbt_ref[...])
    adt = dtt * (-jnp.exp(alt_ref[...]))
    row = lax.broadcasted_iota(jnp.int32, (chunk, chunk), 0)
    col = lax.broadcasted_iota(jnp.int32, (chunk, chunk), 1)
    lower = row >= col
    tril = jnp.where(lower, 1.0, 0.0).astype(BF16)
    triu = jnp.where(row <= col, 1.0, 0.0).astype(BF16)
    cs = sum(_dot(tril, part) for part in _split_bf16(ad, 3))
    cst = sum(_dot(part, triu) for part in _split_bf16(adt, 3))

    heads_per_group = SSM_HEADS // SSM_GROUPS
    dsk = dsk_ref[...]
    ys = []
    for g in range(SSM_GROUPS):
        bg = bmat[:, g * SSM_STATE:(g + 1) * SSM_STATE]
        cg = cmat[:, g * SSM_STATE:(g + 1) * SSM_STATE].astype(BF16)
        cb = _dot_nt(cg, bg.astype(BF16))
        bgt = bg.T.astype(BF16)
        for r in range(heads_per_group):
            h = g * heads_per_group + r
            ccol = cs[:, h:h + 1]
            crow = cst[h:h + 1, :]
            decay = jnp.exp(jnp.where(lower, ccol - crow, NEG_INF))
            x_h = xs[:, h * SSM_HEAD_DIM:(h + 1) * SSM_HEAD_DIM]
            xdt = x_h * dt[:, h:h + 1]
            y = _dot((cb * decay).astype(BF16), xdt.astype(BF16))
            st = state_ref[h]
            y = y + _dot(cg, st.astype(BF16)) * jnp.exp(ccol)
            last = cst[h:h + 1, chunk - 1:chunk]
            to_end = jnp.exp(last - ccol)
            state_ref[h] = st * jnp.exp(last) + _dot(bgt, (xdt * to_end).astype(BF16))
            ys.append(y + x_h * dsk[:, h * SSM_HEAD_DIM:(h + 1) * SSM_HEAD_DIM])

    y = jnp.concatenate(ys, axis=1) * _silu(z_ref[0])
    gw = d_in // SSM_GROUPS
    for g in range(SSM_GROUPS):
        seg = y[:, g * gw:(g + 1) * gw]
        o_ref[0, :, g * gw:(g + 1) * gw] = _rms(seg, ng_ref[:, g * gw:(g + 1) * gw]).astype(o_ref.dtype)


def ssd_mixer(xbc, z, dt_raw, conv_w, conv_b, dt_bias, a_log, d_skip, norm_gain):
    b, s, cch = xbc.shape
    d_in = z.shape[2]
    nc = s // SSM_CHUNK
    hpad = dt_raw.shape[2]
    dtt = jnp.transpose(dt_raw[:, :, :SSM_HEADS], (0, 2, 1))

    def lane_pad(v):
        return jnp.pad(v.astype(F32), (0, hpad - SSM_HEADS)).reshape(1, hpad)

    args = (xbc, z, dt_raw, dtt, conv_w.astype(F32), conv_b.reshape(1, cch).astype(F32),
            lane_pad(dt_bias), dt_bias.reshape(SSM_HEADS, 1).astype(F32),
            lane_pad(a_log), a_log.reshape(SSM_HEADS, 1).astype(F32),
            jnp.repeat(d_skip.astype(F32), SSM_HEAD_DIM).reshape(1, d_in),
            norm_gain.reshape(1, d_in).astype(F32))
    const = lambda bb, c: (0, 0)
    in_specs = [pl.BlockSpec((1, SSM_CHUNK, cch), lambda bb, c: (bb, c, 0)),
                pl.BlockSpec((1, SSM_CHUNK, d_in), lambda bb, c: (bb, c, 0)),
                pl.BlockSpec((1, SSM_CHUNK, hpad), lambda bb, c: (bb, c, 0)),
                pl.BlockSpec((1, SSM_HEADS, SSM_CHUNK), lambda bb, c: (bb, 0, c)),
                pl.BlockSpec((SSM_CONV, cch), const), pl.BlockSpec((1, cch), const),
                pl.BlockSpec((1, hpad), const), pl.BlockSpec((SSM_HEADS, 1), const),
                pl.BlockSpec((1, hpad), const), pl.BlockSpec((SSM_HEADS, 1), const),
                pl.BlockSpec((1, d_in), const), pl.BlockSpec((1, d_in), const)]
    return pl.pallas_call(
        _ssd_body, grid=(b, nc), in_specs=in_specs,
        out_specs=pl.BlockSpec((1, SSM_CHUNK, d_in), lambda bb, c: (bb, c, 0)),
        out_shape=jax.ShapeDtypeStruct((b, s, d_in), BF16),
        scratch_shapes=[pltpu.VMEM((SSM_CHUNK + 8, cch), F32),
                        pltpu.VMEM((SSM_HEADS, SSM_STATE, SSM_HEAD_DIM), F32)],
        compiler_params=_cparams(("parallel", "arbitrary")), name="ssd_mixer",
    )(*args)


def _out_proj_body(x_ref, a_ref, b_ref, wa_ref, wb_ref, o_ref):
    o_ref[...] = x_ref[...] + _dot(a_ref[...], wa_ref[...]) + _dot(b_ref[...], wb_ref[...])


def out_proj_residual(x2, a, bm, wa, wb, tm=512):
    t, d = x2.shape
    return pl.pallas_call(
        _out_proj_body, grid=(t // tm,),
        in_specs=[pl.BlockSpec((tm, d), lambda i: (i, 0)),
                  pl.BlockSpec((tm, a.shape[1]), lambda i: (i, 0)),
                  pl.BlockSpec((tm, bm.shape[1]), lambda i: (i, 0)),
                  pl.BlockSpec(wa.shape, lambda i: (0, 0)),
                  pl.BlockSpec(wb.shape, lambda i: (0, 0))],
        out_specs=pl.BlockSpec((tm, d), lambda i: (i, 0)),
        out_shape=jax.ShapeDtypeStruct((t, d), F32),
        compiler_params=_cparams(("parallel",)), name="out_proj",
    )(x2, a, bm, wa, wb)


def _ffn_body(x_ref, g_ref, wg_ref, wu_ref, wd_ref, o_ref, h_ref):
    f = pl.program_id(1)

    @pl.when(f == 0)
    def _():
        x = x_ref[...]
        h_ref[...] = _rms(x, g_ref[...]).astype(BF16)
        o_ref[...] = x

    h = h_ref[...]
    act = (_silu(_dot(h, wg_ref[...])) * _dot(h, wu_ref[...])).astype(BF16)
    o_ref[...] += _dot(act, wd_ref[...])


def ffn_residual(x2, gain, w_gate, w_up, w_down, tm=512, tf=1408):
    t, d = x2.shape
    d_ff = w_gate.shape[1]
    return pl.pallas_call(
        _ffn_body, grid=(t // tm, d_ff // tf),
        in_specs=[pl.BlockSpec((tm, d), lambda i, f: (i, 0)),
                  pl.BlockSpec((1, d), lambda i, f: (0, 0)),
                  pl.BlockSpec((d, tf), lambda i, f: (0, f)),
                  pl.BlockSpec((d, tf), lambda i, f: (0, f)),
                  pl.BlockSpec((tf, d), lambda i, f: (f, 0))],
        out_specs=pl.BlockSpec((tm, d), lambda i, f: (i, 0)),
        out_shape=jax.ShapeDtypeStruct((t, d), F32),
        scratch_shapes=[pltpu.VMEM((tm, d), BF16)],
        compiler_params=_cparams(("parallel", "arbitrary")), name="ffn",
    )(x2, gain.reshape(1, d).astype(F32), w_gate, w_up, w_down)


MOE_TOKENS = 1024
MOE_SLOTS = 288


def _moe_body(x_ref, g_ref, gates_ref, wg_ref, wu_ref, wd_ref, o_ref,
              h_ref, pos_ref, post_ref, gatest_ref, xe_ref, ye_ref):
    e = pl.program_id(1)
    f = pl.program_id(2)
    tm = x_ref.shape[0]
    slots = xe_ref.shape[1]
    sub = 128

    @pl.when((e == 0) & (f == 0))
    def _():
        x = x_ref[...]
        h_ref[...] = _rms(x, g_ref[...]).astype(BF16)
        o_ref[...] = x
        gates = gates_ref[...]
        routed = jnp.where(gates != 0.0, 1.0, 0.0).astype(BF16)
        for rb in range(tm // sub):
            row = rb * sub + lax.broadcasted_iota(jnp.int32, (sub, tm), 0)
            col = lax.broadcasted_iota(jnp.int32, (sub, tm), 1)
            before = jnp.where(col < row, 1.0, 0.0).astype(BF16)
            pos_ref[rb * sub:(rb + 1) * sub, :] = _dot(before, routed)
        post_ref[...] = pos_ref[...].T[:N_EXPERTS, :]
        gatest_ref[...] = gates.T[:N_EXPERTS, :]

    gate_row = gatest_ref[pl.ds(e, 1), :]
    n_routed = jnp.sum(jnp.where(gate_row != 0.0, 1.0, 0.0)).astype(jnp.int32)
    n_chunks = (n_routed + slots - 1) // slots

    @pl.when(f == 0)
    def _():
        pos_row = post_ref[pl.ds(e, 1), :]

        def gather(c, carry):
            slot = (c * slots + lax.broadcasted_iota(jnp.int32, (slots, tm), 0)).astype(F32)
            onehot = jnp.where((pos_row == slot) & (gate_row != 0.0), 1.0, 0.0).astype(BF16)
            xe_ref[c] = _dot(onehot, h_ref[...]).astype(BF16)
            ye_ref[c] = jnp.zeros(ye_ref.shape[1:], F32)
            return carry

        lax.fori_loop(0, n_chunks, gather, 0)

    def expert(c, carry):
        xc = xe_ref[c]
        act = (_silu(_dot(xc, wg_ref[0])) * _dot(xc, wu_ref[0])).astype(BF16)
        ye_ref[c] += _dot(act, wd_ref[0])
        return carry

    lax.fori_loop(0, n_chunks, expert, 0)

    @pl.when(f == pl.num_programs(2) - 1)
    def _():
        lane = lax.broadcasted_iota(jnp.int32, (tm, LANES), 1)
        gate_col = jnp.sum(jnp.where(lane == e, gates_ref[...], 0.0), axis=1, keepdims=True)
        pos_col = jnp.sum(jnp.where(lane == e, pos_ref[...], 0.0), axis=1, keepdims=True)

        def scatter(c, carry):
            slot = (c * slots + lax.broadcasted_iota(jnp.int32, (tm, slots), 1)).astype(F32)
            onehot = jnp.where((pos_col == slot) & (gate_col != 0.0), 1.0, 0.0).astype(BF16)
            o_ref[...] += _dot(onehot, ye_ref[c].astype(BF16)) * gate_col
            return carry

        lax.fori_loop(0, n_chunks, scatter, 0)


def moe_residual(x2, gain, gates, w_gate, w_up, w_down, tf=1408):
    t, d = x2.shape
    n_e, _, d_ff = w_gate.shape
    tm, slots = MOE_TOKENS, MOE_SLOTS
    max_chunks = -(-tm // slots)
    return pl.pallas_call(
        _moe_body, grid=(t // tm, n_e, d_ff // tf),
        in_specs=[pl.BlockSpec((tm, d), lambda i, e, f: (i, 0)),
                  pl.BlockSpec((1, d), lambda i, e, f: (0, 0)),
                  pl.BlockSpec((tm, LANES), lambda i, e, f: (i, 0)),
                  pl.BlockSpec((1, d, tf), lambda i, e, f: (e, 0, f)),
                  pl.BlockSpec((1, d, tf), lambda i, e, f: (e, 0, f)),
                  pl.BlockSpec((1, tf, d), lambda i, e, f: (e, f, 0))],
        out_specs=pl.BlockSpec((tm, d), lambda i, e, f: (i, 0)),
        out_shape=jax.ShapeDtypeStruct((t, d), F32),
        scratch_shapes=[pltpu.VMEM((tm, d), BF16), pltpu.VMEM((tm, LANES), F32),
                        pltpu.VMEM((N_EXPERTS, tm), F32), pltpu.VMEM((N_EXPERTS, tm), F32),
                        pltpu.VMEM((max_chunks, slots, d), BF16), pltpu.VMEM((max_chunks, slots, d), F32)],
        compiler_params=pltpu.CompilerParams(
            dimension_semantics=("parallel", "arbitrary", "arbitrary"), vmem_limit_bytes=MOE_VMEM_LIMIT),
        name="moe_routed",
    )(x2, gain.reshape(1, d).astype(F32), gates, w_gate, w_up, w_down)


def _router_body(x_ref, g_ref, r_ref, o_ref):
    h = _rms(x_ref[...], g_ref[...])
    logits = jnp.dot(h, r_ref[...], precision=lax.Precision.HIGHEST, preferred_element_type=F32)
    lane = lax.broadcasted_iota(jnp.int32, logits.shape, 1).astype(F32)
    low = jnp.float32(-3.0e38)
    logits = jnp.where(lane < N_EXPERTS, logits, low)
    m1 = jnp.max(logits, axis=1, keepdims=True)
    i1 = jnp.min(jnp.where(logits == m1, lane, float(LANES)), axis=1, keepdims=True)
    rest = jnp.where(lane == i1, low, logits)
    m2 = jnp.max(rest, axis=1, keepdims=True)
    i2 = jnp.min(jnp.where(rest == m2, lane, float(LANES)), axis=1, keepdims=True)
    ex = jnp.exp(m2 - m1)
    w1 = 1.0 / (1.0 + ex)
    w2 = ex / (1.0 + ex)
    o_ref[...] = jnp.where(lane == i1, w1, jnp.where(lane == i2, w2, 0.0))


def router_gates(x2, gain, router, tm=512):
    t, d = x2.shape
    r_pad = jnp.pad(router.astype(F32), ((0, 0), (0, LANES - router.shape[1])))
    return pl.pallas_call(
        _router_body, grid=(t // tm,),
        in_specs=[pl.BlockSpec((tm, d), lambda i: (i, 0)),
                  pl.BlockSpec((1, d), lambda i: (0, 0)),
                  pl.BlockSpec((d, LANES), lambda i: (0, 0))],
        out_specs=pl.BlockSpec((tm, LANES), lambda i: (i, 0)),
        out_shape=jax.ShapeDtypeStruct((t, LANES), F32),
        compiler_params=_cparams(("parallel",)), name="router",
    )(x2, gain.reshape(1, d).astype(F32), r_pad)


def _compress_body(ch_ref, nx_ref, pos_ref, w1_ref, w2_ref, gain_ref, cos_ref, sin_ref, rot_ref, o_ref,
                   *, is_key):
    a = _dot((ch_ref[0, 0] + pos_ref[0]).astype(BF16), w1_ref[0])
    a = a + _dot((nx_ref[0, 0] + pos_ref[1]).astype(BF16), w1_ref[1])
    out = _dot(_silu(a).astype(BF16), w2_ref[...])
    if is_key:
        out = _rms(out, gain_ref[...])
        hi, lo = _split_bf16(out, 2)
        partner = _dot(hi, rot_ref[...]) + _dot(lo, rot_ref[...])
        out = out * cos_ref[...] + partner * sin_ref[...]
    o_ref[0, 0] = out


def nsa_compress(t, pos, w1, w2, gain, seq, is_key):
    b, s, _ = t.shape
    g, d = NSA_KV_GROUPS, NSA_HEAD_DIM
    n_ch = s // NSA_CMP_STRIDE
    half = NSA_CMP_STRIDE * d
    ch = t.reshape(b, n_ch, NSA_CMP_STRIDE, g, d).transpose(0, 3, 1, 2, 4).reshape(b, g, n_ch, half)
    nxt = jnp.concatenate([ch[:, :, 1:], jnp.zeros((b, g, 1, half), F32)], axis=2)
    pos2 = pos.astype(F32).reshape(2, 1, half)
    w1s = w1.astype(BF16).reshape(2, half, d)
    cmp_end = jnp.arange(n_ch) * NSA_CMP_STRIDE + NSA_CMP_BLOCK - 1
    inv_freq = 1.0 / (ROPE_THETA ** (jnp.arange(0, d, 2, dtype=F32) / d))
    ang = cmp_end.astype(F32)[:, None] * inv_freq[None, :]
    cos = jnp.concatenate([jnp.cos(ang)] * 2, axis=1)
    sin = jnp.concatenate([jnp.sin(ang)] * 2, axis=1)
    rot = np.zeros((d, d), np.float32)
    rot[np.arange(d // 2) + d // 2, np.arange(d // 2)] = -1.0
    rot[np.arange(d // 2), np.arange(d // 2) + d // 2] = 1.0
    blk = lambda bb, gg: (bb, gg, 0, 0)
    c2 = lambda bb, gg: (0, 0)
    c3 = lambda bb, gg: (0, 0, 0)
    return pl.pallas_call(
        functools.partial(_compress_body, is_key=is_key), grid=(b, g),
        in_specs=[pl.BlockSpec((1, 1, n_ch, half), blk), pl.BlockSpec((1, 1, n_ch, half), blk),
                  pl.BlockSpec((2, 1, half), c3), pl.BlockSpec((2, half, d), c3),
                  pl.BlockSpec((d, d), c2), pl.BlockSpec((1, d), c2),
                  pl.BlockSpec((n_ch, d), c2), pl.BlockSpec((n_ch, d), c2), pl.BlockSpec((d, d), c2)],
        out_specs=pl.BlockSpec((1, 1, n_ch, d), blk),
        out_shape=jax.ShapeDtypeStruct((b, g, n_ch, d), F32),
        compiler_params=_cparams(("parallel", "parallel")), name="nsa_compress",
    )(ch, nxt, pos2, w1s, w2.astype(BF16), gain.reshape(1, d).astype(F32), cos, sin,
      jnp.asarray(rot, dtype=BF16))


def _nsa_body(q_ref, ck_ref, cvt_ref, ksl_ref, vslt_ref, kwn_ref, vwnt_ref, ovt_ref, glt_ref, o_ref,
              imp_ref, sel_ref, *, tq, scale):
    g = pl.program_id(1)
    i = pl.program_id(2)
    d = NSA_HEAD_DIM
    rep = NSA_HEADS // NSA_KV_GROUPS
    t0 = i * tq
    n_cmp = ck_ref.shape[1]
    n_sel = ovt_ref.shape[0]
    width = rep * tq

    qb = q_ref[0]
    q4 = jnp.concatenate([qb[:, r * d:(r + 1) * d] for r in range(rep)], axis=0)
    q4 = jnp.concatenate([q4, q4], axis=1)
    lane = lax.broadcasted_iota(jnp.int32, q4.shape, 1)
    q4 = jnp.where(jnp.right_shift(lane, d.bit_length() - 1) == g, q4, jnp.zeros_like(q4))

    def qpos_of(shape):
        return t0 + (lax.broadcasted_iota(jnp.int32, shape, 1) & (tq - 1))

    sc = _dot_nt(ck_ref[0], q4) * scale
    cmp_end = lax.broadcasted_iota(jnp.int32, sc.shape, 0) * NSA_CMP_STRIDE + (NSA_CMP_BLOCK - 1)
    valid = cmp_end <= qpos_of(sc.shape)
    sc = jnp.where(valid, sc, NEG_INF)
    e = jnp.exp(sc - jnp.max(sc, axis=0, keepdims=True))
    p = jnp.where(valid, e / jnp.sum(e, axis=0, keepdims=True), 0.0)
    cvt = cvt_ref[0, pl.ds(pl.multiple_of(g * d, d), d), :]
    o_c = _dot(cvt, p.astype(BF16))

    psum = p[:, 0:tq]
    for r in range(1, rep):
        psum = psum + p[:, r * tq:(r + 1) * tq]
    hi, lo = _split_bf16(psum, 2)
    imp = _dot(ovt_ref[...], hi) + _dot(ovt_ref[...], lo)
    blk = lax.broadcasted_iota(jnp.int32, imp.shape, 0)
    qp = t0 + lax.broadcasted_iota(jnp.int32, imp.shape, 1)
    cur = jnp.right_shift(qp, NSA_SEL_BLOCK.bit_length() - 1)
    forced = (blk == 0) | (blk == cur) | (blk == cur - 1)
    future = blk * NSA_SEL_BLOCK > qp
    imp = jnp.where(future, -FORCE_SCORE, jnp.where(forced, FORCE_SCORE, imp))
    imp_ref[...] = imp

    def count(i2, cnt):
        other = imp_ref[pl.ds(i2, 1), :]
        beats = (other > imp) | ((other == imp) & (blk > i2))
        return cnt + jnp.where(beats, 1.0, 0.0)

    rank = lax.fori_loop(0, n_sel, count, jnp.zeros(imp.shape, F32))
    sel = jnp.where(rank < float(min(NSA_TOP_N, n_sel)), 1.0, 0.0)
    sel_ref[...] = jnp.concatenate([sel] * rep, axis=1)

    def online_softmax(n_lo, n_hi, chunk, k_ref, vt_ref, mask_fn):
        def body(c, carry):
            m_prev, l_prev, acc = carry
            start = pl.multiple_of(c * chunk, chunk)
            s = _dot_nt(k_ref[0, pl.ds(start, chunk), :], q4) * scale
            kpos = start + lax.broadcasted_iota(jnp.int32, s.shape, 0)
            mask = mask_fn(c, kpos, qpos_of(s.shape))
            s = jnp.where(mask, s, NEG_INF)
            m_new = jnp.maximum(m_prev, jnp.max(s, axis=0, keepdims=True))
            alpha = jnp.exp(m_prev - m_new)
            pr = jnp.where(mask, jnp.exp(s - m_new), 0.0)
            l_new = alpha * l_prev + jnp.sum(pr, axis=0, keepdims=True)
            vt = vt_ref[0, pl.ds(pl.multiple_of(g * d, d), d), pl.ds(start, chunk)]
            return m_new, l_new, alpha * acc + _dot(vt, pr.astype(BF16))

        init = (jnp.full((1, width), NEG_INF, F32), jnp.zeros((1, width), F32),
                jnp.zeros((d, width), F32))
        _, l_fin, acc = lax.fori_loop(n_lo, n_hi, body, init)
        return acc / l_fin

    sel_chunk = 4 * NSA_SEL_BLOCK

    def sel_mask(c, kpos, qpos):
        rows = [jnp.broadcast_to(sel_ref[pl.ds(c * 4 + jb, 1), :], (NSA_SEL_BLOCK, width))
                for jb in range(4)]
        return (jnp.concatenate(rows, axis=0) > 0.5) & (kpos <= qpos)

    o_s = online_softmax(0, (t0 + tq + sel_chunk - 1) // sel_chunk, sel_chunk, ksl_ref, vslt_ref, sel_mask)

    def win_mask(c, kpos, qpos):
        return (kpos <= qpos) & (kpos > qpos - NSA_WINDOW)

    o_w = online_softmax(jnp.maximum(i - NSA_WINDOW // tq, 0), i + 1, tq, kwn_ref, vwnt_ref, win_mask)

    def gate(branch):
        rows = [glt_ref[0, pl.ds((g * rep + r) * 3 + branch, 1), :] for r in range(rep)]
        return _sigmoid(jnp.concatenate(rows, axis=1))

    o_ref[0, 0, 0] = gate(0) * o_c + gate(1) * o_s + gate(2) * o_w


def nsa_overlap_t(n_cmp, n_sel):
    c_start = np.arange(n_cmp)[None, :] * NSA_CMP_STRIDE
    s_start = np.arange(n_sel)[:, None] * NSA_SEL_BLOCK
    hit = (c_start < s_start + NSA_SEL_BLOCK) & (c_start + NSA_CMP_BLOCK > s_start)
    hit = hit & (np.arange(n_cmp)[None, :] < n_cmp - NSA_CMP_BLOCK // NSA_CMP_STRIDE + 1)
    return jnp.asarray(hit.astype(np.float32), dtype=BF16)


def nsa_attention(qn, ck, cvt, ksl, vslt, kwn, vwnt, glt, tq=128):
    b, s, _ = qn.shape
    g, d = NSA_KV_GROUPS, NSA_HEAD_DIM
    rep = NSA_HEADS // g
    n_cmp = ck.shape[1]
    n_sel = s // NSA_SEL_BLOCK
    nq = s // tq
    ovt = nsa_overlap_t(n_cmp, n_sel)
    full3 = lambda bb, gg, i: (bb, 0, 0)
    return pl.pallas_call(
        functools.partial(_nsa_body, tq=tq, scale=d ** -0.5), grid=(b, g, nq),
        in_specs=[pl.BlockSpec((1, tq, rep * d), lambda bb, gg, i: (bb, i, gg)),
                  pl.BlockSpec((1, n_cmp, g * d), full3), pl.BlockSpec((1, g * d, n_cmp), full3),
                  pl.BlockSpec((1, s, g * d), full3), pl.BlockSpec((1, g * d, s), full3),
                  pl.BlockSpec((1, s, g * d), full3), pl.BlockSpec((1, g * d, s), full3),
                  pl.BlockSpec((n_sel, n_cmp), lambda bb, gg, i: (0, 0)),
                  pl.BlockSpec((1, glt.shape[1], tq), lambda bb, gg, i: (bb, 0, i))],
        out_specs=pl.BlockSpec((1, 1, 1, d, rep * tq), lambda bb, gg, i: (bb, gg, i, 0, 0)),
        out_shape=jax.ShapeDtypeStruct((b, g, nq, d, rep * tq), F32),
        scratch_shapes=[pltpu.VMEM((n_sel, tq), F32), pltpu.VMEM((n_sel, rep * tq), F32)],
        compiler_params=_cparams(("parallel", "parallel", "arbitrary")), name="nsa_attention",
    )(qn, ck, cvt, ksl, vslt, kwn, vwnt, ovt, glt)


def _pad_cols(w, n):
    return jnp.pad(w, ((0, 0), (0, n - w.shape[1])))


def _even_layer(x2, b, s, layer_idx, norm_mix, w_in, q_gain, k_gain, lam, subln_gain, conv_w, conv_b,
                dt_bias, a_log, d_skip, ssm_norm_gain, w_out, norm_ffn, w_gate, w_up, w_down):
    nq = DA_HEADS * 2 * DA_HEAD_DIM
    nv = DA_HEADS * DA_V_DIM
    cch = SSM_D_INNER + 2 * SSM_GROUPS * SSM_STATE
    offs = np.cumsum([0, nq, nq, nv, SSM_D_INNER, cch, SSM_HEADS])
    wb = w_in.astype(BF16)
    pieces = [wb[:, offs[k]:offs[k + 1]] for k in range(6)]
    pieces[5] = _pad_cols(pieces[5], LANES)
    q, k, v, z, xbc, dt = norm_proj(x2, norm_mix, pieces, [F32, F32, BF16, F32, F32, F32])

    tables = _rope_tables(s, DA_HEAD_DIM)
    qn = head_norm(q, q_gain, DA_HEAD_DIM, s, tables, out_mul=DA_HEAD_DIM ** -0.5 * LOG2E).reshape(b, s, nq)
    kn = head_norm(k, k_gain, DA_HEAD_DIM, s, tables).reshape(b, s, nq)
    vt = v.reshape(b, s, nv).transpose(0, 2, 1)
    lam_init = 0.8 - 0.6 * math.exp(-0.3 * layer_idx)
    lf = lam.astype(F32)
    lam_full = jnp.exp(jnp.sum(lf[0] * lf[1])) - jnp.exp(jnp.sum(lf[2] * lf[3])) + lam_init
    a_out = flash_attention(lam_full.reshape(1), [qn], [kn], vt, subln_gain, DA_HEADS, DA_V_DIM,
                            diff=True, out_scale=1.0 - lam_init)
    b_out = ssd_mixer(xbc.reshape(b, s, cch), z.reshape(b, s, SSM_D_INNER), dt.reshape(b, s, LANES),
                      conv_w, conv_b, dt_bias, a_log, d_skip, ssm_norm_gain)
    wo = w_out.astype(BF16)
    x2 = out_proj_residual(x2, a_out.reshape(-1, nv), b_out.reshape(-1, SSM_D_INNER), wo[:nv], wo[nv:])
    return ffn_residual(x2, norm_ffn, w_gate.astype(BF16), w_up.astype(BF16), w_down.astype(BF16))


def _odd_layer(x2, b, s, norm_mix, w_in, q_gain, k_gain, cmp_pos, cmp_w1, cmp_w2, cq_gain, ckv_gain,
               w_uq, w_ukv, qn_gain, qr_gain, kn_gain, kr_gain, w_out, norm_ffn, router, w_gate, w_up,
               w_down):
    g, d = NSA_KV_GROUPS, NSA_HEAD_DIM
    nq = NSA_HEADS * d
    nkv = g * d
    sizes = [nq] + [nkv] * 6 + [NSA_HEADS * 3, w_uq.shape[0], w_ukv.shape[0], MLA_ROPE_DIM]
    offs = np.cumsum([0] + sizes)
    wb = w_in.astype(BF16)
    pieces = [wb[:, offs[k]:offs[k + 1]] for k in range(len(sizes))]
    pieces[7] = _pad_cols(pieces[7], LANES)
    pieces[10] = _pad_cols(pieces[10], LANES)
    (q, kc, vc, ksl, vsl, kwn, vwn, gl, cq, ckv, kr) = norm_proj(
        x2, norm_mix, pieces, [F32, F32, F32, F32, BF16, F32, BF16, F32, F32, F32, F32])

    tables = _rope_tables(s, d)
    qn = head_norm(q, q_gain, d, s, tables).reshape(b, s, nq)
    ksl_n = head_norm(ksl, k_gain[1], d, s, tables).reshape(b, s, nkv)
    kwn_n = head_norm(kwn, k_gain[2], d, s, tables).reshape(b, s, nkv)
    ck = nsa_compress(kc.reshape(b, s, nkv), cmp_pos[0], cmp_w1[0], cmp_w2[0], k_gain[0], s, True)
    cv = nsa_compress(vc.reshape(b, s, nkv), cmp_pos[1], cmp_w1[1], cmp_w2[1], k_gain[0], s, False)
    n_cmp = ck.shape[2]
    ck = ck.transpose(0, 2, 1, 3).reshape(b, n_cmp, nkv).astype(BF16)
    cvt = cv.transpose(0, 1, 3, 2).reshape(b, nkv, n_cmp).astype(BF16)
    vslt = vsl.reshape(b, s, nkv).transpose(0, 2, 1)
    vwnt = vwn.reshape(b, s, nkv).transpose(0, 2, 1)
    glt = gl.reshape(b, s, LANES)[:, :, :32].transpose(0, 2, 1)
    tq = 128
    o = nsa_attention(qn, ck, cvt, ksl_n, vslt, kwn_n, vwnt, glt, tq=tq)
    rep = NSA_HEADS // g
    c_out = o.reshape(b, g, s // tq, d, rep, tq).transpose(0, 2, 5, 1, 4, 3).reshape(b * s, nq)
    c_out = c_out.astype(BF16)

    h = MLA_HEADS
    dqk = MLA_NOPE_DIM + MLA_ROPE_DIM
    wq = w_uq.astype(BF16).reshape(-1, h, dqk)
    wq_nope = wq[:, :, :MLA_NOPE_DIM].reshape(-1, h * MLA_NOPE_DIM)
    wq_rope = jnp.pad(wq[:, :, MLA_NOPE_DIM:], ((0, 0), (0, 0), (0, LANES - MLA_ROPE_DIM)))
    wq_rope = wq_rope.reshape(-1, h * LANES)
    wkv = w_ukv.astype(BF16).reshape(-1, h, MLA_NOPE_DIM + MLA_V_DIM)
    wk_nope = wkv[:, :, :MLA_NOPE_DIM].reshape(-1, h * MLA_NOPE_DIM)
    wv = wkv[:, :, MLA_NOPE_DIM:].reshape(-1, h * MLA_V_DIM)
    q_nope, q_rope = norm_proj(cq, cq_gain, [wq_nope, wq_rope], [F32, F32])
    k_nope, v = norm_proj(ckv, ckv_gain, [wk_nope, wv], [F32, BF16])
    q_mul = dqk ** -0.5 * LOG2E
    q_nope = head_norm(q_nope, qn_gain, MLA_NOPE_DIM, s, out_mul=q_mul)
    k_nope = head_norm(k_nope, kn_gain, MLA_NOPE_DIM, s)
    q_rope = head_norm(q_rope, qr_gain, MLA_ROPE_DIM, s, tables, out_mul=q_mul)
    k_rope = head_norm(kr, kr_gain, MLA_ROPE_DIM, s, tables)
    shp = lambda t: t.reshape(b, s, t.shape[-1])
    d_out = flash_attention(jnp.zeros((1,), F32), [shp(q_nope), shp(q_rope)], [shp(k_nope), shp(k_rope)],
                            shp(v).transpose(0, 2, 1), jnp.ones((MLA_V_DIM,), F32), h, MLA_V_DIM,
                            diff=False)

    wo = w_out.astype(BF16)
    x2 = out_proj_residual(x2, c_out, d_out.reshape(b * s, h * MLA_V_DIM), wo[:nq], wo[nq:])
    gates = router_gates(x2, norm_ffn, router)
    return moe_residual(x2, norm_ffn, gates, w_gate.astype(BF16), w_up.astype(BF16), w_down.astype(BF16))


def kernel(x, ev_norm_mix, ev_w_in, da_q_gain, da_k_gain, da_lambda, da_subln_gain, ssm_conv_w, ssm_conv_b, ssm_dt_bias, ssm_a_log, ssm_d, ssm_norm_gain, ev_w_out, ev_norm_ffn, ffn_w_gate, ffn_w_up, ffn_w_down, od_norm_mix, od_w_in, nsa_q_gain, nsa_k_gain, nsa_cmp_pos, nsa_cmp_w1, nsa_cmp_w2, mla_cq_gain, mla_ckv_gain, mla_w_uq, mla_w_ukv, mla_qn_gain, mla_qr_gain, mla_kn_gain, mla_kr_gain, od_w_out, od_norm_ffn, moe_router, moe_w_gate, moe_w_up, moe_w_down):
    b, s, d = x.shape
    x2 = x.reshape(b * s, d)
    depth = ev_norm_mix.shape[0] + od_norm_mix.shape[0]
    for layer in range(depth):
        i = layer // 2
        if layer % 2 == 0:
            x2 = _even_layer(x2, b, s, layer, ev_norm_mix[i], ev_w_in[i], da_q_gain[i], da_k_gain[i],
                             da_lambda[i], da_subln_gain[i], ssm_conv_w[i], ssm_conv_b[i],
                             ssm_dt_bias[i], ssm_a_log[i], ssm_d[i], ssm_norm_gain[i], ev_w_out[i],
                             ev_norm_ffn[i], ffn_w_gate[i], ffn_w_up[i], ffn_w_down[i])
        else:
            x2 = _odd_layer(x2, b, s, od_norm_mix[i], od_w_in[i], nsa_q_gain[i], nsa_k_gain[i],
                            nsa_cmp_pos[i], nsa_cmp_w1[i], nsa_cmp_w2[i], mla_cq_gain[i],
                            mla_ckv_gain[i], mla_w_uq[i], mla_w_ukv[i], mla_qn_gain[i], mla_qr_gain[i],
                            mla_kn_gain[i], mla_kr_gain[i], od_w_out[i], od_norm_ffn[i], moe_router[i],
                            moe_w_gate[i], moe_w_up[i], moe_w_down[i])
    return x2.reshape(b, s, d)
```

```python
import functools
import math

import numpy as np
import jax
import jax.numpy as jnp
from jax import lax
from jax.experimental import pallas as pl
from jax.experimental.pallas import tpu as pltpu

F32 = jnp.float32
BF16 = jnp.bfloat16

ROPE_THETA = 10000.0
NORM_EPS = 1e-6
NEG_INF = -1e30
FORCE_SCORE = 1e6
LOG2E = 1.4426950408889634

DA_HEADS = 4
DA_HEAD_DIM = 64
DA_V_DIM = 2 * DA_HEAD_DIM
SSM_HEADS = 8
SSM_HEAD_DIM = 64
SSM_D_INNER = SSM_HEADS * SSM_HEAD_DIM
SSM_GROUPS = 2
SSM_STATE = 128
SSM_CONV = 4
SSM_CHUNK = 256
NSA_HEADS = 8
NSA_KV_GROUPS = 2
NSA_HEAD_DIM = 64
NSA_CMP_BLOCK = 32
NSA_CMP_STRIDE = 16
NSA_SEL_BLOCK = 64
NSA_TOP_N = 16
NSA_WINDOW = 512
MLA_HEADS = 4
MLA_NOPE_DIM = 128
MLA_ROPE_DIM = 64
MLA_V_DIM = 128
N_EXPERTS = 8

LANES = 128
VMEM_LIMIT = 48 * 1024 * 1024
MOE_VMEM_LIMIT = 58 * 1024 * 1024

NT_DIMS = (((1,), (1,)), ((), ()))


def _cparams(semantics):
    return pltpu.CompilerParams(dimension_semantics=semantics, vmem_limit_bytes=VMEM_LIMIT)


def _dot(a, b):
    return jnp.dot(a, b, preferred_element_type=F32)


def _dot_nt(a, b):
    return lax.dot_general(a, b, NT_DIMS, preferred_element_type=F32)


def _split_bf16(x, parts):
    out = []
    for _ in range(parts):
        hi = x.astype(BF16)
        out.append(hi)
        x = x - hi.astype(F32)
    return out


def _sigmoid(x):
    return 1.0 / (1.0 + jnp.exp(-x))


def _silu(x):
    return x * _sigmoid(x)


def _softplus(x):
    return jnp.maximum(x, 0.0) + jnp.log(1.0 + jnp.exp(-jnp.abs(x)))


def _rms(x, gain):
    ms = jnp.mean(x * x, axis=-1, keepdims=True)
    return x * lax.rsqrt(ms + NORM_EPS) * gain


def _norm_proj_body(x_ref, g_ref, *refs, n_out):
    h = _rms(x_ref[...], g_ref[...]).astype(BF16)
    for w_ref, o_ref in zip(refs[:n_out], refs[n_out:]):
        o_ref[...] = _dot(h, w_ref[...]).astype(o_ref.dtype)


def norm_proj(x2, gain, weights, out_dtypes, tm=512):
    t, d = x2.shape
    n_out = len(weights)
    in_specs = [pl.BlockSpec((tm, d), lambda i: (i, 0)),
                pl.BlockSpec((1, d), lambda i: (0, 0))]
    in_specs += [pl.BlockSpec(w.shape, lambda i: (0, 0)) for w in weights]
    out_specs = [pl.BlockSpec((tm, w.shape[1]), lambda i: (i, 0)) for w in weights]
    out_shape = [jax.ShapeDtypeStruct((t, w.shape[1]), dt) for w, dt in zip(weights, out_dtypes)]
    return pl.pallas_call(
        functools.partial(_norm_proj_body, n_out=n_out),
        grid=(t // tm,), in_specs=in_specs, out_specs=out_specs, out_shape=out_shape,
        compiler_params=_cparams(("parallel",)), name="norm_proj",
    )(x2, gain.reshape(1, d), *weights)


def _head_norm_body(y_ref, gain_ref, bd_ref, *rest, hd, rope, out_mul):
    o_ref = rest[-1]
    y = y_ref[...]
    n = y.shape[1]
    hi, lo = _split_bf16(y * y, 2)
    ss = _dot(hi, bd_ref[...]) + _dot(lo, bd_ref[...])
    yn = y * lax.rsqrt(ss * (1.0 / hd) + NORM_EPS) * gain_ref[...]
    if rope:
        cos_ref, sin_ref = rest[0], rest[1]
        reps = n // LANES
        cos = jnp.concatenate([cos_ref[...]] * reps, axis=1) if reps > 1 else cos_ref[...]
        sin = jnp.concatenate([sin_ref[...]] * reps, axis=1) if reps > 1 else sin_ref[...]
        lane = lax.broadcasted_iota(jnp.int32, yn.shape, 1)
        first_half = (lane & (hd - 1)) < (hd // 2)
        partner = jnp.where(first_half, pltpu.roll(yn, n - hd // 2, 1), pltpu.roll(yn, hd // 2, 1))
        yn = yn * cos + partner * sin
    if out_mul != 1.0:
        yn = yn * out_mul
    o_ref[...] = yn.astype(o_ref.dtype)


def _block_diag_ones(n, hd):
    idx = np.arange(n) // hd
    return jnp.asarray((idx[:, None] == idx[None, :]).astype(np.float32), dtype=BF16)


def _rope_tables(seq, hd):
    inv_freq = 1.0 / (ROPE_THETA ** (jnp.arange(0, hd, 2, dtype=F32) / hd))
    ang = jnp.arange(seq, dtype=F32)[:, None] * inv_freq[None, :]
    cos, sin = jnp.cos(ang), jnp.sin(ang)
    reps = LANES // hd
    cos_t = jnp.tile(jnp.concatenate([cos, cos], axis=1), (1, reps))
    sin_t = jnp.tile(jnp.concatenate([-sin, sin], axis=1), (1, reps))
    return cos_t, sin_t


def head_norm(y, gain, hd, seq, rope_tables=None, out_mul=1.0, out_dtype=BF16, tm=512):
    t, n = y.shape
    gain_t = jnp.tile(gain.astype(F32), n // hd).reshape(1, n)
    args = [y, gain_t, _block_diag_ones(n, hd)]
    in_specs = [pl.BlockSpec((tm, n), lambda i: (i, 0)),
                pl.BlockSpec((1, n), lambda i: (0, 0)),
                pl.BlockSpec((n, n), lambda i: (0, 0))]
    if rope_tables is not None:
        per_seq = seq // tm
        args += list(rope_tables)
        in_specs += [pl.BlockSpec((tm, LANES), lambda i: (i % per_seq, 0))] * 2
    return pl.pallas_call(
        functools.partial(_head_norm_body, hd=hd, rope=rope_tables is not None, out_mul=out_mul),
        grid=(t // tm,), in_specs=in_specs,
        out_specs=pl.BlockSpec((tm, n), lambda i: (i, 0)),
        out_shape=jax.ShapeDtypeStruct((t, n), out_dtype),
        compiler_params=_cparams(("parallel",)), name="head_norm",
    )(*args)


def _flash_body(lam_ref, *refs, n_qk, diff, out_scale, sub):
    q_refs = refs[:n_qk]
    k_refs = refs[n_qk:2 * n_qk]
    vt_ref, gain_ref, o_ref, m_ref, l_ref, acc_ref, s0_ref, s1_ref, p0_ref, p1_ref = refs[2 * n_qk:]
    i = pl.program_id(2)
    n_sm = 2 if diff else 1
    _, tk, tq = s0_ref.shape
    s_slots = (s0_ref, s1_ref)
    p_slots = (p0_ref, p1_ref)

    m_ref[...] = jnp.full(m_ref.shape, NEG_INF, F32)
    l_ref[...] = jnp.zeros(l_ref.shape, F32)
    acc_ref[...] = jnp.zeros(acc_ref.shape, F32)

    qs = [r[0] for r in q_refs]
    q = qs[0] if n_qk == 1 else jnp.concatenate(qs, axis=1)
    if diff:
        lane = lax.broadcasted_iota(jnp.int32, q.shape, 1)
        half = q.shape[1] // 2
        zero = jnp.zeros_like(q)
        q_parts = [jnp.where(lane < half, q, zero), jnp.where(lane >= half, q, zero)]
    else:
        q_parts = [q]

    def scores(c, slot, diagonal=False):
        rows = pl.ds(pl.multiple_of(c * tk, tk), tk)
        ks = [r[0, rows, :] for r in k_refs]
        k = ks[0] if n_qk == 1 else jnp.concatenate(ks, axis=1)
        for sm in range(n_sm):
            s = _dot_nt(k, q_parts[sm])
            if diagonal:
                row = lax.broadcasted_iota(jnp.int32, s.shape, 0)
                col = lax.broadcasted_iota(jnp.int32, s.shape, 1)
                s = jnp.where(row <= col, s, NEG_INF)
            s_slots[slot][sm] = s

    def update(c, slot):
        vt = vt_ref[0, :, pl.ds(pl.multiple_of(c * tk, tk), tk)]
        for sm in range(n_sm):
            s_ref, p_ref = s_slots[slot], p_slots[slot]
            m_prev = m_ref[sm]
            m_new = jnp.maximum(m_prev, jnp.max(s_ref[sm], axis=0, keepdims=True))
            m_ref[sm] = m_new
            alpha = jnp.exp2(m_prev - m_new)
            lpart = jnp.zeros((sub, tq), F32)
            for r in range(tk // sub):
                p = jnp.exp2(s_ref[sm, r * sub:(r + 1) * sub, :] - m_new)
                lpart = lpart + p
                p_ref[sm, r * sub:(r + 1) * sub, :] = p.astype(BF16)
            l_ref[sm] = alpha * l_ref[sm] + jnp.sum(lpart, axis=0, keepdims=True)
            acc_ref[sm] = alpha * acc_ref[sm] + _dot(vt, p_ref[sm])

    last_past = jnp.maximum(i - 1, 0)

    @pl.when(i > 0)
    def _():
        scores(0, 0)

    def pair(k2, carry):
        scores(jnp.minimum(2 * k2 + 1, last_past), 1)
        update(2 * k2, 0)
        scores(jnp.minimum(2 * k2 + 2, last_past), 0)
        update(2 * k2 + 1, 1)
        return carry

    lax.fori_loop(0, i // 2, pair, 0)

    @pl.when(i % 2 == 1)
    def _():
        update(i - 1, 0)

    scores(i, 0, diagonal=True)
    update(i, 0)
    o = acc_ref[0] / l_ref[0]
    if diff:
        o = o - lam_ref[0] * (acc_ref[1] / l_ref[1])
        ms = jnp.mean(o * o, axis=0, keepdims=True)
        o = o * lax.rsqrt(ms + NORM_EPS) * gain_ref[...] * out_scale
    o_ref[0] = o.T.astype(o_ref.dtype)


def flash_attention(lam, qs, ks, vt, gain, n_heads, dv, *, diff, out_scale=1.0, tile=512, sub=64):
    b, _, s = vt.shape
    nt = s // tile
    n_qk = len(qs)
    in_specs = [pl.BlockSpec(memory_space=pltpu.SMEM)]
    for q in qs:
        w = q.shape[2] // n_heads
        in_specs.append(pl.BlockSpec((1, tile, w), lambda bb, h, i: (bb, i, h)))
    for q, k in zip(qs, ks):
        w = q.shape[2] // n_heads
        if k.shape[2] == w:
            in_specs.append(pl.BlockSpec((1, s, w), lambda bb, h, i: (bb, 0, 0)))
        else:
            in_specs.append(pl.BlockSpec((1, s, w), lambda bb, h, i: (bb, 0, h)))
    in_specs.append(pl.BlockSpec((1, dv, s), lambda bb, h, i: (bb, h, 0)))
    in_specs.append(pl.BlockSpec((dv, 1), lambda bb, h, i: (0, 0)))
    n_sm = 2 if diff else 1
    return pl.pallas_call(
        functools.partial(_flash_body, n_qk=n_qk, diff=diff, out_scale=out_scale, sub=sub),
        grid=(b, n_heads, nt), in_specs=in_specs,
        out_specs=pl.BlockSpec((1, tile, dv), lambda bb, h, i: (bb, i, h)),
        out_shape=jax.ShapeDtypeStruct((b, s, n_heads * dv), BF16),
        scratch_shapes=[pltpu.VMEM((n_sm, 1, tile), F32), pltpu.VMEM((n_sm, 1, tile), F32),
                        pltpu.VMEM((n_sm, dv, tile), F32),
                        pltpu.VMEM((n_sm, tile, tile), F32), pltpu.VMEM((n_sm, tile, tile), F32),
                        pltpu.VMEM((n_sm, tile, tile), BF16), pltpu.VMEM((n_sm, tile, tile), BF16)],
        compiler_params=_cparams(("parallel", "parallel", "arbitrary")),
        name="flash_diff" if diff else "flash_plain",
    )(lam, *qs, *ks, vt, gain.reshape(dv, 1).astype(F32))


def _ssd_body(xbc_ref, z_ref, dt_ref, dtt_ref, cw_ref, cb_ref, dtb_ref, dtbt_ref, al_ref, alt_ref,
              dsk_ref, ng_ref, o_ref, xpad_ref, state_ref):
    chunk = xbc_ref.shape[1]
    d_in = z_ref.shape[2]
    gn = SSM_GROUPS * SSM_STATE
    c = pl.program_id(1)

    @pl.when(c == 0)
    def _():
        xpad_ref[0:8, :] = jnp.zeros((8, xpad_ref.shape[1]), F32)
        state_ref[...] = jnp.zeros(state_ref.shape, F32)

    xpad_ref[8:8 + chunk, :] = xbc_ref[0]
    conv = cb_ref[...]
    for w in range(SSM_CONV):
        conv = conv + cw_ref[w:w + 1, :] * xpad_ref[pl.ds(8 - (SSM_CONV - 1) + w, chunk), :]
    xpad_ref[0:8, :] = xpad_ref[chunk:chunk + 8, :]
    u = _silu(conv)
    xs = u[:, :d_in]
    bmat = u[:, d_in:d_in + gn]
    cmat = u[:, d_in + gn:]

    dt = _softplus(dt_ref[0] + dtb_ref[...])
    ad = dt * (-jnp.exp(al_ref[...]))
    dtt = _softplus(dtt_ref[0] + dtbt_ref[...])
    adt = dtt * (-jnp.exp(alt_ref[...]))
    row = lax.broadcasted_iota(jnp.int32, (chunk, chunk), 0)
    col = lax.broadcasted_iota(jnp.int32, (chunk, chunk), 1)
    lower = row >= col
    tril = jnp.where(lower, 1.0, 0.0).astype(BF16)
    triu = jnp.where(row <= col, 1.0, 0.0).astype(BF16)
    cs = sum(_dot(tril, part) for part in _split_bf16(ad, 3))
    cst = sum(_dot(part, triu) for part in _split_bf16(adt, 3))

    heads_per_group = SSM_HEADS // SSM_GROUPS
    dsk = dsk_ref[...]
    ys = []
    for g in range(SSM_GROUPS):
        bg = bmat[:, g * SSM_STATE:(g + 1) * SSM_STATE]
        cg = cmat[:, g * SSM_STATE:(g + 1) * SSM_STATE].astype(BF16)
        cb = _dot_nt(cg, bg.astype(BF16))
        bgt = bg.T.astype(BF16)
        for r in range(heads_per_group):
            h = g * heads_per_group + r
            ccol = cs[:, h:h + 1]
            crow = cst[h:h + 1, :]
            decay = jnp.exp(jnp.where(lower, ccol - crow, NEG_INF))
            x_h = xs[:, h * SSM_HEAD_DIM:(h + 1) * SSM_HEAD_DIM]
            xdt = x_h * dt[:, h:h + 1]
            y = _dot((cb * decay).astype(BF16), xdt.astype(BF16))
            st = state_ref[h]
            y = y + _dot(cg, st.astype(BF16)) * jnp.exp(ccol)
            last = cst[h:h + 1, chunk - 1:chunk]
            to_end = jnp.exp(last - ccol)
            state_ref[h] = st * jnp.exp(last) + _dot(bgt, (xdt * to_end).astype(BF16))
            ys.append(y + x_h * dsk[:, h * SSM_HEAD_DIM:(h + 1) * SSM_HEAD_DIM])

    y = jnp.concatenate(ys, axis=1) * _silu(z_ref[0])
    gw = d_in // SSM_GROUPS
    for g in range(SSM_GROUPS):
        seg = y[:, g * gw:(g + 1) * gw]
        o_ref[0, :, g * gw:(g + 1) * gw] = _rms(seg, ng_ref[:, g * gw:(g + 1) * gw]).astype(o_ref.dtype)


def ssd_mixer(xbc, z, dt_raw, conv_w, conv_b, dt_bias, a_log, d_skip, norm_gain):
    b, s, cch = xbc.shape
    d_in = z.shape[2]
    nc = s // SSM_CHUNK
    hpad = dt_raw.shape[2]
    dtt = jnp.transpose(dt_raw[:, :, :SSM_HEADS], (0, 2, 1))

    def lane_pad(v):
        return jnp.pad(v.astype(F32), (0, hpad - SSM_HEADS)).reshape(1, hpad)

    args = (xbc, z, dt_raw, dtt, conv_w.astype(F32), conv_b.reshape(1, cch).astype(F32),
            lane_pad(dt_bias), dt_bias.reshape(SSM_HEADS, 1).astype(F32),
            lane_pad(a_log), a_log.reshape(SSM_HEADS, 1).astype(F32),
            jnp.repeat(d_skip.astype(F32), SSM_HEAD_DIM).reshape(1, d_in),
            norm_gain.reshape(1, d_in).astype(F32))
    const = lambda bb, c: (0, 0)
    in_specs = [pl.BlockSpec((1, SSM_CHUNK, cch), lambda bb, c: (bb, c, 0)),
                pl.BlockSpec((1, SSM_CHUNK, d_in), lambda bb, c: (bb, c, 0)),
                pl.BlockSpec((1, SSM_CHUNK, hpad), lambda bb, c: (bb, c, 0)),
                pl.BlockSpec((1, SSM_HEADS, SSM_CHUNK), lambda bb, c: (bb, 0, c)),
                pl.BlockSpec((SSM_CONV, cch), const), pl.BlockSpec((1, cch), const),
                pl.BlockSpec((1, hpad), const), pl.BlockSpec((SSM_HEADS, 1), const),
                pl.BlockSpec((1, hpad), const), pl.BlockSpec((SSM_HEADS, 1), const),
                pl.BlockSpec((1, d_in), const), pl.BlockSpec((1, d_in), const)]
    return pl.pallas_call(
        _ssd_body, grid=(b, nc), in_specs=in_specs,
        out_specs=pl.BlockSpec((1, SSM_CHUNK, d_in), lambda bb, c: (bb, c, 0)),
        out_shape=jax.ShapeDtypeStruct((b, s, d_in), BF16),
        scratch_shapes=[pltpu.VMEM((SSM_CHUNK + 8, cch), F32),
                        pltpu.VMEM((SSM_HEADS, SSM_STATE, SSM_HEAD_DIM), F32)],
        compiler_params=_cparams(("parallel", "arbitrary")), name="ssd_mixer",
    )(*args)


def _out_proj_body(x_ref, a_ref, b_ref, wa_ref, wb_ref, o_ref):
    o_ref[...] = x_ref[...] + _dot(a_ref[...], wa_ref[...]) + _dot(b_ref[...], wb_ref[...])


def out_proj_residual(x2, a, bm, wa, wb, tm=512):
    t, d = x2.shape
    return pl.pallas_call(
        _out_proj_body, grid=(t // tm,),
        in_specs=[pl.BlockSpec((tm, d), lambda i: (i, 0)),
                  pl.BlockSpec((tm, a.shape[1]), lambda i: (i, 0)),
                  pl.BlockSpec((tm, bm.shape[1]), lambda i: (i, 0)),
                  pl.BlockSpec(wa.shape, lambda i: (0, 0)),
                  pl.BlockSpec(wb.shape, lambda i: (0, 0))],
        out_specs=pl.BlockSpec((tm, d), lambda i: (i, 0)),
        out_shape=jax.ShapeDtypeStruct((t, d), F32),
        compiler_params=_cparams(("parallel",)), name="out_proj",
    )(x2, a, bm, wa, wb)


def _ffn_body(x_ref, g_ref, wg_ref, wu_ref, wd_ref, o_ref, h_ref):
    f = pl.program_id(1)

    @pl.when(f == 0)
    def _():
        x = x_ref[...]
        h_ref[...] = _rms(x, g_ref[...]).astype(BF16)
        o_ref[...] = x

    h = h_ref[...]
    act = (_silu(_dot(h, wg_ref[...])) * _dot(h, wu_ref[...])).astype(BF16)
    o_ref[...] += _dot(act, wd_ref[...])


def ffn_residual(x2, gain, w_gate, w_up, w_down, tm=512, tf=1408):
    t, d = x2.shape
    d_ff = w_gate.shape[1]
    return pl.pallas_call(
        _ffn_body, grid=(t // tm, d_ff // tf),
        in_specs=[pl.BlockSpec((tm, d), lambda i, f: (i, 0)),
                  pl.BlockSpec((1, d), lambda i, f: (0, 0)),
                  pl.BlockSpec((d, tf), lambda i, f: (0, f)),
                  pl.BlockSpec((d, tf), lambda i, f: (0, f)),
                  pl.BlockSpec((tf, d), lambda i, f: (f, 0))],
        out_specs=pl.BlockSpec((tm, d), lambda i, f: (i, 0)),
        out_shape=jax.ShapeDtypeStruct((t, d), F32),
        scratch_shapes=[pltpu.VMEM((tm, d), BF16)],
        compiler_params=_cparams(("parallel", "arbitrary")), name="ffn",
    )(x2, gain.reshape(1, d).astype(F32), w_gate, w_up, w_down)


MOE_TOKENS = 1024
MOE_SLOTS = 288


def _moe_body(x_ref, g_ref, gates_ref, wg_ref, wu_ref, wd_ref, o_ref,
              h_ref, pos_ref, post_ref, gatest_ref, xe_ref, ye_ref):
    e = pl.program_id(1)
    f = pl.program_id(2)
    tm = x_ref.shape[0]
    slots = xe_ref.shape[1]
    sub = 128

    @pl.when((e == 0) & (f == 0))
    def _():
        x = x_ref[...]
        h_ref[...] = _rms(x, g_ref[...]).astype(BF16)
        o_ref[...] = x
        gates = gates_ref[...]
        routed = jnp.where(gates != 0.0, 1.0, 0.0).astype(BF16)
        for rb in range(tm // sub):
            row = rb * sub + lax.broadcasted_iota(jnp.int32, (sub, tm), 0)
            col = lax.broadcasted_iota(jnp.int32, (sub, tm), 1)
            before = jnp.where(col < row, 1.0, 0.0).astype(BF16)
            pos_ref[rb * sub:(rb + 1) * sub, :] = _dot(before, routed)
        post_ref[...] = pos_ref[...].T[:N_EXPERTS, :]
        gatest_ref[...] = gates.T[:N_EXPERTS, :]

    gate_row = gatest_ref[pl.ds(e, 1), :]
    n_routed = jnp.sum(jnp.where(gate_row != 0.0, 1.0, 0.0)).astype(jnp.int32)
    n_chunks = (n_routed + slots - 1) // slots

    @pl.when(f == 0)
    def _():
        pos_row = post_ref[pl.ds(e, 1), :]

        def gather(c, carry):
            slot = (c * slots + lax.broadcasted_iota(jnp.int32, (slots, tm), 0)).astype(F32)
            onehot = jnp.where((pos_row == slot) & (gate_row != 0.0), 1.0, 0.0).astype(BF16)
            xe_ref[c] = _dot(onehot, h_ref[...]).astype(BF16)
            ye_ref[c] = jnp.zeros(ye_ref.shape[1:], F32)
            return carry

        lax.fori_loop(0, n_chunks, gather, 0)

    def expert(c, carry):
        xc = xe_ref[c]
        act = (_silu(_dot(xc, wg_ref[0])) * _dot(xc, wu_ref[0])).astype(BF16)
        ye_ref[c] += _dot(act, wd_ref[0])
        return carry

    lax.fori_loop(0, n_chunks, expert, 0)

    @pl.when(f == pl.num_programs(2) - 1)
    def _():
        lane = lax.broadcasted_iota(jnp.int32, (tm, LANES), 1)
        gate_col = jnp.sum(jnp.where(lane == e, gates_ref[...], 0.0), axis=1, keepdims=True)
        pos_col = jnp.sum(jnp.where(lane == e, pos_ref[...], 0.0), axis=1, keepdims=True)

        def scatter(c, carry):
            slot = (c * slots + lax.broadcasted_iota(jnp.int32, (tm, slots), 1)).astype(F32)
            onehot = jnp.where((pos_col == slot) & (gate_col != 0.0), 1.0, 0.0).astype(BF16)
            o_ref[...] += _dot(onehot, ye_ref[c].astype(BF16)) * gate_col
            return carry

        lax.fori_loop(0, n_chunks, scatter, 0)


def moe_residual(x2, gain, gates, w_gate, w_up, w_down, tf=1408):
    t, d = x2.shape
    n_e, _, d_ff = w_gate.shape
    tm, slots = MOE_TOKENS, MOE_SLOTS
    max_chunks = -(-tm // slots)
    return pl.pallas_call(
        _moe_body, grid=(t // tm, n_e, d_ff // tf),
        in_specs=[pl.BlockSpec((tm, d), lambda i, e, f: (i, 0)),
                  pl.BlockSpec((1, d), lambda i, e, f: (0, 0)),
                  pl.BlockSpec((tm, LANES), lambda i, e, f: (i, 0)),
                  pl.BlockSpec((1, d, tf), lambda i, e, f: (e, 0, f)),
                  pl.BlockSpec((1, d, tf), lambda i, e, f: (e, 0, f)),
                  pl.BlockSpec((1, tf, d), lambda i, e, f: (e, f, 0))],
        out_specs=pl.BlockSpec((tm, d), lambda i, e, f: (i, 0)),
        out_shape=jax.ShapeDtypeStruct((t, d), F32),
        scratch_shapes=[pltpu.VMEM((tm, d), BF16), pltpu.VMEM((tm, LANES), F32),
                        pltpu.VMEM((N_EXPERTS, tm), F32), pltpu.VMEM((N_EXPERTS, tm), F32),
                        pltpu.VMEM((max_chunks, slots, d), BF16), pltpu.VMEM((max_chunks, slots, d), F32)],
        compiler_params=pltpu.CompilerParams(
            dimension_semantics=("parallel", "arbitrary", "arbitrary"), vmem_limit_bytes=MOE_VMEM_LIMIT),
        name="moe_routed",
    )(x2, gain.reshape(1, d).astype(F32), gates, w_gate, w_up, w_down)


def _router_body(x_ref, g_ref, r_ref, o_ref):
    h = _rms(x_ref[...], g_ref[...])
    logits = jnp.dot(h, r_ref[...], precision=lax.Precision.HIGHEST, preferred_element_type=F32)
    lane = lax.broadcasted_iota(jnp.int32, logits.shape, 1).astype(F32)
    low = jnp.float32(-3.0e38)
    logits = jnp.where(lane < N_EXPERTS, logits, low)
    m1 = jnp.max(logits, axis=1, keepdims=True)
    i1 = jnp.min(jnp.where(logits == m1, lane, float(LANES)), axis=1, keepdims=True)
    rest = jnp.where(lane == i1, low, logits)
    m2 = jnp.max(rest, axis=1, keepdims=True)
    i2 = jnp.min(jnp.where(rest == m2, lane, float(LANES)), axis=1, keepdims=True)
    ex = jnp.exp(m2 - m1)
    w1 = 1.0 / (1.0 + ex)
    w2 = ex / (1.0 + ex)
    o_ref[...] = jnp.where(lane == i1, w1, jnp.where(lane == i2, w2, 0.0))


def router_gates(x2, gain, router, tm=512):
    t, d = x2.shape
    r_pad = jnp.pad(router.astype(F32), ((0, 0), (0, LANES - router.shape[1])))
    return pl.pallas_call(
        _router_body, grid=(t // tm,),
        in_specs=[pl.BlockSpec((tm, d), lambda i: (i, 0)),
                  pl.BlockSpec((1, d), lambda i: (0, 0)),
                  pl.BlockSpec((d, LANES), lambda i: (0, 0))],
        out_specs=pl.BlockSpec((tm, LANES), lambda i: (i, 0)),
        out_shape=jax.ShapeDtypeStruct((t, LANES), F32),
        compiler_params=_cparams(("parallel",)), name="router",
    )(x2, gain.reshape(1, d).astype(F32), r_pad)


def _compress_body(ch_ref, nx_ref, pos_ref, w1_ref, w2_ref, gain_ref, cos_ref, sin_ref, rot_ref, o_ref,
                   *, is_key):
    a = _dot((ch_ref[0, 0] + pos_ref[0]).astype(BF16), w1_ref[0])
    a = a + _dot((nx_ref[0, 0] + pos_ref[1]).astype(BF16), w1_ref[1])
    out = _dot(_silu(a).astype(BF16), w2_ref[...])
    if is_key:
        out = _rms(out, gain_ref[...])
        hi, lo = _split_bf16(out, 2)
        partner = _dot(hi, rot_ref[...]) + _dot(lo, rot_ref[...])
        out = out * cos_ref[...] + partner * sin_ref[...]
    o_ref[0, 0] = out


def nsa_compress(t, pos, w1, w2, gain, seq, is_key):
    b, s, _ = t.shape
    g, d = NSA_KV_GROUPS, NSA_HEAD_DIM
    n_ch = s // NSA_CMP_STRIDE
    half = NSA_CMP_STRIDE * d
    ch = t.reshape(b, n_ch, NSA_CMP_STRIDE, g, d).transpose(0, 3, 1, 2, 4).reshape(b, g, n_ch, half)
    nxt = jnp.concatenate([ch[:, :, 1:], jnp.zeros((b, g, 1, half), F32)], axis=2)
    pos2 = pos.astype(F32).reshape(2, 1, half)
    w1s = w1.astype(BF16).reshape(2, half, d)
    cmp_end = jnp.arange(n_ch) * NSA_CMP_STRIDE + NSA_CMP_BLOCK - 1
    inv_freq = 1.0 / (ROPE_THETA ** (jnp.arange(0, d, 2, dtype=F32) / d))
    ang = cmp_end.astype(F32)[:, None] * inv_freq[None, :]
    cos = jnp.concatenate([jnp.cos(ang)] * 2, axis=1)
    sin = jnp.concatenate([jnp.sin(ang)] * 2, axis=1)
    rot = np.zeros((d, d), np.float32)
    rot[np.arange(d // 2) + d // 2, np.arange(d // 2)] = -1.0
    rot[np.arange(d // 2), np.arange(d // 2) + d // 2] = 1.0
    blk = lambda bb, gg: (bb, gg, 0, 0)
    c2 = lambda bb, gg: (0, 0)
    c3 = lambda bb, gg: (0, 0, 0)
    return pl.pallas_call(
        functools.partial(_compress_body, is_key=is_key), grid=(b, g),
        in_specs=[pl.BlockSpec((1, 1, n_ch, half), blk), pl.BlockSpec((1, 1, n_ch, half), blk),
                  pl.BlockSpec((2, 1, half), c3), pl.BlockSpec((2, half, d), c3),
                  pl.BlockSpec((d, d), c2), pl.BlockSpec((1, d), c2),
                  pl.BlockSpec((n_ch, d), c2), pl.BlockSpec((n_ch, d), c2), pl.BlockSpec((d, d), c2)],
        out_specs=pl.BlockSpec((1, 1, n_ch, d), blk),
        out_shape=jax.ShapeDtypeStruct((b, g, n_ch, d), F32),
        compiler_params=_cparams(("parallel", "parallel")), name="nsa_compress",
    )(ch, nxt, pos2, w1s, w2.astype(BF16), gain.reshape(1, d).astype(F32), cos, sin,
      jnp.asarray(rot, dtype=BF16))


def _nsa_body(q_ref, ck_ref, cvt_ref, ksl_ref, vslt_ref, kwn_ref, vwnt_ref, ovt_ref, glt_ref, o_ref,
              sc_ref, phi_ref, plo_ref, imp_ref, bias_ref, ss_ref, ps_ref, ss1_ref, ps1_ref, sw_ref, pw_ref,
              ow_ref, *, tq):
    g = pl.program_id(1)
    i = pl.program_id(2)
    d = NSA_HEAD_DIM
    rep = NSA_HEADS // NSA_KV_GROUPS
    t0 = i * tq
    n_cmp = ck_ref.shape[1]
    n_sel = ovt_ref.shape[0]
    width = rep * tq
    sub = NSA_SEL_BLOCK
    dead = 0.5 * NEG_INF
    v_rows = pl.ds(pl.multiple_of(g * d, d), d)

    qb = q_ref[0]
    q4 = jnp.concatenate([qb[:, r * d:(r + 1) * d] for r in range(rep)], axis=0)
    q4 = jnp.concatenate([q4, q4], axis=1)
    lane = lax.broadcasted_iota(jnp.int32, q4.shape, 1)
    q4 = jnp.where(jnp.right_shift(lane, d.bit_length() - 1) == g, q4, jnp.zeros_like(q4))

    def qpos_of(shape):
        return t0 + (lax.broadcasted_iota(jnp.int32, shape, 1) & (tq - 1))

    s = _dot_nt(ck_ref[0], q4)
    cmp_end = lax.broadcasted_iota(jnp.int32, s.shape, 0) * NSA_CMP_STRIDE + (NSA_CMP_BLOCK - 1)
    s = jnp.where(cmp_end <= qpos_of(s.shape), s, NEG_INF)
    sc_ref[...] = s
    m_c = jnp.max(s, axis=0, keepdims=True)
    lpart = jnp.zeros((sub, width), F32)
    for r in range(n_cmp // sub):
        e = jnp.exp2(sc_ref[r * sub:(r + 1) * sub, :] - m_c)
        lpart = lpart + e
        hi = e.astype(BF16)
        phi_ref[r * sub:(r + 1) * sub, :] = hi
        plo_ref[r * sub:(r + 1) * sub, :] = (e - hi.astype(F32)).astype(BF16)
    inv_c = jnp.where(m_c > dead, 1.0 / jnp.sum(lpart, axis=0, keepdims=True), 0.0)
    o_c = _dot(cvt_ref[0, v_rows, :], phi_ref[...]) * inv_c

    imp4 = (_dot(ovt_ref[...], phi_ref[...]) + _dot(ovt_ref[...], plo_ref[...])) * inv_c
    imp = imp4[:, 0:tq]
    for r in range(1, rep):
        imp = imp + imp4[:, r * tq:(r + 1) * tq]
    blk = lax.broadcasted_iota(jnp.int32, imp.shape, 0)
    qp = t0 + lax.broadcasted_iota(jnp.int32, imp.shape, 1)
    cur = jnp.right_shift(qp, NSA_SEL_BLOCK.bit_length() - 1)
    forced = (blk == 0) | (blk == cur) | (blk == cur - 1)
    future = blk * NSA_SEL_BLOCK > qp
    imp_ref[...] = jnp.where(future, -FORCE_SCORE, jnp.where(forced, FORCE_SCORE, imp))
    bias_ref[...] = jnp.full(bias_ref.shape, NEG_INF, F32)

    n_live = jnp.minimum((t0 + tq - 1) // NSA_SEL_BLOCK + 1, n_sel)
    n_var = max(n_sel // 32, 1)
    rows_per = n_sel // n_var
    top_n = float(min(NSA_TOP_N, n_sel))
    for v in range(n_var):
        rows = rows_per * (v + 1)

        @pl.when((n_live > rows_per * v) & (n_live <= rows))
        def _():
            mine = imp_ref[0:rows, :]
            blk_r = lax.broadcasted_iota(jnp.int32, mine.shape, 0)

            def count(i2, cnt):
                other = imp_ref[pl.ds(i2, 1), :]
                beats = (other > mine) | ((other == mine) & (blk_r > i2))
                return cnt + jnp.where(beats, 1.0, 0.0)

            rank = lax.fori_loop(0, n_live, count, jnp.zeros(mine.shape, F32))
            bias = jnp.where(rank < top_n, 0.0, NEG_INF)
            bias_ref[0:rows, :] = jnp.concatenate([bias] * rep, axis=1)

    init = (jnp.full((1, width), NEG_INF, F32), jnp.zeros((1, width), F32), jnp.zeros((d, width), F32))

    chunk = 8 * sub
    n_sub = chunk // sub

    s_slots = (ss_ref, ss1_ref)
    p_slots = (ps_ref, ps1_ref)

    def sel_scores(c, slot, diagonal=False):
        start = pl.multiple_of(c * chunk, chunk)
        s = _dot_nt(ksl_ref[0, pl.ds(start, chunk), :], q4)
        if diagonal:
            kpos = start + lax.broadcasted_iota(jnp.int32, s.shape, 0)
            s = jnp.where(kpos <= qpos_of(s.shape), s, NEG_INF)
        s_slots[slot][...] = s

    def sel_update(c, slot, carry):
        m_prev, l_prev, acc = carry
        s_ref, p_ref = s_slots[slot], p_slots[slot]
        biases = [bias_ref[pl.ds(c * n_sub + r, 1), :] for r in range(n_sub)]
        m_new = m_prev
        for r in range(n_sub):
            m_new = jnp.maximum(
                m_new, jnp.max(s_ref[r * sub:(r + 1) * sub, :], axis=0, keepdims=True) + biases[r])
        alpha = jnp.exp2(m_prev - m_new)
        live = m_new > dead
        lpart = jnp.zeros((sub, width), F32)
        for r in range(n_sub):
            shift = jnp.where(live, biases[r] - m_new, NEG_INF)
            p = jnp.exp2(s_ref[r * sub:(r + 1) * sub, :] + shift)
            lpart = lpart + p
            p_ref[r * sub:(r + 1) * sub, :] = p.astype(BF16)
        l_new = alpha * l_prev + jnp.sum(lpart, axis=0, keepdims=True)
        vt = vslt_ref[0, v_rows, pl.ds(pl.multiple_of(c * chunk, chunk), chunk)]
        return m_new, l_new, alpha * acc + _dot(vt, p_ref[...])

    c_last = (t0 + tq - 1) // chunk
    last_past = jnp.maximum(c_last - 1, 0)

    @pl.when(c_last > 0)
    def _():
        sel_scores(0, 0)

    def pair(k, carry):
        sel_scores(jnp.minimum(2 * k + 1, last_past), 1)
        carry = sel_update(2 * k, 0, carry)
        sel_scores(jnp.minimum(2 * k + 2, last_past), 0)
        return sel_update(2 * k + 1, 1, carry)

    carry = lax.fori_loop(0, c_last // 2, pair, init)
    carry = lax.cond(c_last % 2 == 1, lambda cr: sel_update(c_last - 1, 0, cr), lambda cr: cr, carry)
    sel_scores(c_last, 0, diagonal=True)
    _, l_s, acc_s = sel_update(c_last, 0, carry)
    o_s = acc_s / l_s

    def win_chunk(c, carry):
        m_prev, l_prev, acc = carry
        start = pl.multiple_of(c * tq, tq)
        s = _dot_nt(kwn_ref[0, pl.ds(start, tq), :], q4)
        kpos = start + lax.broadcasted_iota(jnp.int32, s.shape, 0)
        qpos = qpos_of(s.shape)
        s = jnp.where((kpos <= qpos) & (kpos > qpos - NSA_WINDOW), s, NEG_INF)
        m_new = jnp.maximum(m_prev, jnp.max(s, axis=0, keepdims=True))
        alpha = jnp.exp2(m_prev - m_new)
        p = jnp.exp2(s + jnp.where(m_new > dead, -m_new, NEG_INF))
        l_new = alpha * l_prev + jnp.sum(p, axis=0, keepdims=True)
        vt = vwnt_ref[0, v_rows, pl.ds(start, tq)]
        return m_new, l_new, alpha * acc + _dot(vt, p.astype(BF16))

    n_back = NSA_WINDOW // tq

    @pl.when(i < n_back)
    def _():
        _, l_w, acc_w = lax.fori_loop(0, i + 1, win_chunk, init)
        ow_ref[...] = acc_w / l_w

    @pl.when(i >= n_back)
    def _():
        start = pl.multiple_of(t0 - NSA_WINDOW, tq)
        s = _dot_nt(kwn_ref[0, pl.ds(start, NSA_WINDOW + tq), :], q4)
        kpos = start + lax.broadcasted_iota(jnp.int32, (tq, width), 0)
        qpos = qpos_of((tq, width))
        sw_ref[0:tq, :] = jnp.where(kpos > qpos - NSA_WINDOW, s[0:tq, :], NEG_INF)
        sw_ref[tq:NSA_WINDOW, :] = s[tq:NSA_WINDOW, :]
        sw_ref[NSA_WINDOW:, :] = jnp.where(kpos + NSA_WINDOW <= qpos, s[NSA_WINDOW:, :], NEG_INF)
        m_w = jnp.max(sw_ref[...], axis=0, keepdims=True)
        lpart = jnp.zeros((sub, width), F32)
        for r in range((NSA_WINDOW + tq) // sub):
            p = jnp.exp2(sw_ref[r * sub:(r + 1) * sub, :] - m_w)
            lpart = lpart + p
            pw_ref[r * sub:(r + 1) * sub, :] = p.astype(BF16)
        vt = vwnt_ref[0, v_rows, pl.ds(start, NSA_WINDOW + tq)]
        ow_ref[...] = _dot(vt, pw_ref[...]) / jnp.sum(lpart, axis=0, keepdims=True)

    o_w = ow_ref[...]

    def gate(branch):
        rows = [glt_ref[0, pl.ds((g * rep + r) * 3 + branch, 1), :] for r in range(rep)]
        return _sigmoid(jnp.concatenate(rows, axis=1))

    o_ref[0, 0, 0] = gate(0) * o_c + gate(1) * o_s + gate(2) * o_w


def nsa_overlap_t(n_cmp, n_sel):
    c_start = np.arange(n_cmp)[None, :] * NSA_CMP_STRIDE
    s_start = np.arange(n_sel)[:, None] * NSA_SEL_BLOCK
    hit = (c_start < s_start + NSA_SEL_BLOCK) & (c_start + NSA_CMP_BLOCK > s_start)
    hit = hit & (np.arange(n_cmp)[None, :] < n_cmp - NSA_CMP_BLOCK // NSA_CMP_STRIDE + 1)
    return jnp.asarray(hit.astype(np.float32), dtype=BF16)


def nsa_attention(qn, ck, cvt, ksl, vslt, kwn, vwnt, glt, tq=128):
    b, s, _ = qn.shape
    g, d = NSA_KV_GROUPS, NSA_HEAD_DIM
    rep = NSA_HEADS // g
    n_cmp = ck.shape[1]
    n_sel = s // NSA_SEL_BLOCK
    nq = s // tq
    ovt = nsa_overlap_t(n_cmp, n_sel)
    width = rep * tq
    chunk = 8 * NSA_SEL_BLOCK
    full3 = lambda bb, gg, i: (bb, 0, 0)
    return pl.pallas_call(
        functools.partial(_nsa_body, tq=tq), grid=(b, g, nq),
        in_specs=[pl.BlockSpec((1, tq, rep * d), lambda bb, gg, i: (bb, i, gg)),
                  pl.BlockSpec((1, n_cmp, g * d), full3), pl.BlockSpec((1, g * d, n_cmp), full3),
                  pl.BlockSpec((1, s, g * d), full3), pl.BlockSpec((1, g * d, s), full3),
                  pl.BlockSpec((1, s, g * d), full3), pl.BlockSpec((1, g * d, s), full3),
                  pl.BlockSpec((n_sel, n_cmp), lambda bb, gg, i: (0, 0)),
                  pl.BlockSpec((1, glt.shape[1], tq), lambda bb, gg, i: (bb, 0, i))],
        out_specs=pl.BlockSpec((1, 1, 1, d, rep * tq), lambda bb, gg, i: (bb, gg, i, 0, 0)),
        out_shape=jax.ShapeDtypeStruct((b, g, nq, d, rep * tq), F32),
        scratch_shapes=[pltpu.VMEM((n_cmp, width), F32), pltpu.VMEM((n_cmp, width), BF16),
                        pltpu.VMEM((n_cmp, width), BF16), pltpu.VMEM((n_sel, tq), F32),
                        pltpu.VMEM((n_sel, width), F32), pltpu.VMEM((chunk, width), F32),
                        pltpu.VMEM((chunk, width), BF16), pltpu.VMEM((chunk, width), F32),
                        pltpu.VMEM((chunk, width), BF16), pltpu.VMEM((NSA_WINDOW + tq, width), F32),
                        pltpu.VMEM((NSA_WINDOW + tq, width), BF16), pltpu.VMEM((d, width), F32)],
        compiler_params=_cparams(("parallel", "parallel", "arbitrary")), name="nsa_attention",
    )(qn, ck, cvt, ksl, vslt, kwn, vwnt, ovt, glt)


def _pad_cols(w, n):
    return jnp.pad(w, ((0, 0), (0, n - w.shape[1])))


def _even_layer(x2, b, s, layer_idx, norm_mix, w_in, q_gain, k_gain, lam, subln_gain, conv_w, conv_b,
                dt_bias, a_log, d_skip, ssm_norm_gain, w_out, norm_ffn, w_gate, w_up, w_down):
    nq = DA_HEADS * 2 * DA_HEAD_DIM
    nv = DA_HEADS * DA_V_DIM
    cch = SSM_D_INNER + 2 * SSM_GROUPS * SSM_STATE
    offs = np.cumsum([0, nq, nq, nv, SSM_D_INNER, cch, SSM_HEADS])
    wb = w_in.astype(BF16)
    pieces = [wb[:, offs[k]:offs[k + 1]] for k in range(6)]
    pieces[5] = _pad_cols(pieces[5], LANES)
    q, k, v, z, xbc, dt = norm_proj(x2, norm_mix, pieces, [F32, F32, BF16, F32, F32, F32])

    tables = _rope_tables(s, DA_HEAD_DIM)
    qn = head_norm(q, q_gain, DA_HEAD_DIM, s, tables, out_mul=DA_HEAD_DIM ** -0.5 * LOG2E).reshape(b, s, nq)
    kn = head_norm(k, k_gain, DA_HEAD_DIM, s, tables).reshape(b, s, nq)
    vt = v.reshape(b, s, nv).transpose(0, 2, 1)
    lam_init = 0.8 - 0.6 * math.exp(-0.3 * layer_idx)
    lf = lam.astype(F32)
    lam_full = jnp.exp(jnp.sum(lf[0] * lf[1])) - jnp.exp(jnp.sum(lf[2] * lf[3])) + lam_init
    a_out = flash_attention(lam_full.reshape(1), [qn], [kn], vt, subln_gain, DA_HEADS, DA_V_DIM,
                            diff=True, out_scale=1.0 - lam_init)
    b_out = ssd_mixer(xbc.reshape(b, s, cch), z.reshape(b, s, SSM_D_INNER), dt.reshape(b, s, LANES),
                      conv_w, conv_b, dt_bias, a_log, d_skip, ssm_norm_gain)
    wo = w_out.astype(BF16)
    x2 = out_proj_residual(x2, a_out.reshape(-1, nv), b_out.reshape(-1, SSM_D_INNER), wo[:nv], wo[nv:])
    return ffn_residual(x2, norm_ffn, w_gate.astype(BF16), w_up.astype(BF16), w_down.astype(BF16))


def _odd_layer(x2, b, s, norm_mix, w_in, q_gain, k_gain, cmp_pos, cmp_w1, cmp_w2, cq_gain, ckv_gain,
               w_uq, w_ukv, qn_gain, qr_gain, kn_gain, kr_gain, w_out, norm_ffn, router, w_gate, w_up,
               w_down):
    g, d = NSA_KV_GROUPS, NSA_HEAD_DIM
    nq = NSA_HEADS * d
    nkv = g * d
    sizes = [nq] + [nkv] * 6 + [NSA_HEADS * 3, w_uq.shape[0], w_ukv.shape[0], MLA_ROPE_DIM]
    offs = np.cumsum([0] + sizes)
    wb = w_in.astype(BF16)
    pieces = [wb[:, offs[k]:offs[k + 1]] for k in range(len(sizes))]
    pieces[7] = _pad_cols(pieces[7], LANES)
    pieces[10] = _pad_cols(pieces[10], LANES)
    (q, kc, vc, ksl, vsl, kwn, vwn, gl, cq, ckv, kr) = norm_proj(
        x2, norm_mix, pieces, [F32, F32, F32, F32, BF16, F32, BF16, F32, F32, F32, F32])

    tables = _rope_tables(s, d)
    qn = head_norm(q, q_gain, d, s, tables, out_mul=d ** -0.5 * LOG2E).reshape(b, s, nq)
    ksl_n = head_norm(ksl, k_gain[1], d, s, tables).reshape(b, s, nkv)
    kwn_n = head_norm(kwn, k_gain[2], d, s, tables).reshape(b, s, nkv)
    ck = nsa_compress(kc.reshape(b, s, nkv), cmp_pos[0], cmp_w1[0], cmp_w2[0], k_gain[0], s, True)
    cv = nsa_compress(vc.reshape(b, s, nkv), cmp_pos[1], cmp_w1[1], cmp_w2[1], k_gain[0], s, False)
    n_cmp = ck.shape[2]
    ck = ck.transpose(0, 2, 1, 3).reshape(b, n_cmp, nkv).astype(BF16)
    cvt = cv.transpose(0, 1, 3, 2).reshape(b, nkv, n_cmp).astype(BF16)
    vslt = vsl.reshape(b, s, nkv).transpose(0, 2, 1)
    vwnt = vwn.reshape(b, s, nkv).transpose(0, 2, 1)
    glt = gl.reshape(b, s, LANES)[:, :, :32].transpose(0, 2, 1)
    tq = 128
    o = nsa_attention(qn, ck, cvt, ksl_n, vslt, kwn_n, vwnt, glt, tq=tq)
    rep = NSA_HEADS // g
    c_out = o.reshape(b, g, s // tq, d, rep, tq).transpose(0, 2, 5, 1, 4, 3).reshape(b * s, nq)
    c_out = c_out.astype(BF16)

    h = MLA_HEADS
    dqk = MLA_NOPE_DIM + MLA_ROPE_DIM
    wq = w_uq.astype(BF16).reshape(-1, h, dqk)
    wq_nope = wq[:, :, :MLA_NOPE_DIM].reshape(-1, h * MLA_NOPE_DIM)
    wq_rope = jnp.pad(wq[:, :, MLA_NOPE_DIM:], ((0, 0), (0, 0), (0, LANES - MLA_ROPE_DIM)))
    wq_rope = wq_rope.reshape(-1, h * LANES)
    wkv = w_ukv.astype(BF16).reshape(-1, h, MLA_NOPE_DIM + MLA_V_DIM)
    wk_nope = wkv[:, :, :MLA_NOPE_DIM].reshape(-1, h * MLA_NOPE_DIM)
    wv = wkv[:, :, MLA_NOPE_DIM:].reshape(-1, h * MLA_V_DIM)
    q_nope, q_rope = norm_proj(cq, cq_gain, [wq_nope, wq_rope], [F32, F32])
    k_nope, v = norm_proj(ckv, ckv_gain, [wk_nope, wv], [F32, BF16])
    q_mul = dqk ** -0.5 * LOG2E
    q_nope = head_norm(q_nope, qn_gain, MLA_NOPE_DIM, s, out_mul=q_mul)
    k_nope = head_norm(k_nope, kn_gain, MLA_NOPE_DIM, s)
    q_rope = head_norm(q_rope, qr_gain, MLA_ROPE_DIM, s, tables, out_mul=q_mul)
    k_rope = head_norm(kr, kr_gain, MLA_ROPE_DIM, s, tables)
    shp = lambda t: t.reshape(b, s, t.shape[-1])
    d_out = flash_attention(jnp.zeros((1,), F32), [shp(q_nope), shp(q_rope)], [shp(k_nope), shp(k_rope)],
                            shp(v).transpose(0, 2, 1), jnp.ones((MLA_V_DIM,), F32), h, MLA_V_DIM,
                            diff=False)

    wo = w_out.astype(BF16)
    x2 = out_proj_residual(x2, c_out, d_out.reshape(b * s, h * MLA_V_DIM), wo[:nq], wo[nq:])
    gates = router_gates(x2, norm_ffn, router)
    return moe_residual(x2, norm_ffn, gates, w_gate.astype(BF16), w_up.astype(BF16), w_down.astype(BF16))


def kernel(x, ev_norm_mix, ev_w_in, da_q_gain, da_k_gain, da_lambda, da_subln_gain, ssm_conv_w, ssm_conv_b, ssm_dt_bias, ssm_a_log, ssm_d, ssm_norm_gain, ev_w_out, ev_norm_ffn, ffn_w_gate, ffn_w_up, ffn_w_down, od_norm_mix, od_w_in, nsa_q_gain, nsa_k_gain, nsa_cmp_pos, nsa_cmp_w1, nsa_cmp_w2, mla_cq_gain, mla_ckv_gain, mla_w_uq, mla_w_ukv, mla_qn_gain, mla_qr_gain, mla_kn_gain, mla_kr_gain, od_w_out, od_norm_ffn, moe_router, moe_w_gate, moe_w_up, moe_w_down):
    b, s, d = x.shape
    x2 = x.reshape(b * s, d)
    depth = ev_norm_mix.shape[0] + od_norm_mix.shape[0]
    for layer in range(depth):
        i = layer // 2
        if layer % 2 == 0:
            x2 = _even_layer(x2, b, s, layer, ev_norm_mix[i], ev_w_in[i], da_q_gain[i], da_k_gain[i],
                             da_lambda[i], da_subln_gain[i], ssm_conv_w[i], ssm_conv_b[i],
                             ssm_dt_bias[i], ssm_a_log[i], ssm_d[i], ssm_norm_gain[i], ev_w_out[i],
                             ev_norm_ffn[i], ffn_w_gate[i], ffn_w_up[i], ffn_w_down[i])
        else:
            x2 = _odd_layer(x2, b, s, od_norm_mix[i], od_w_in[i], nsa_q_gain[i], nsa_k_gain[i],
                            nsa_cmp_pos[i], nsa_cmp_w1[i], nsa_cmp_w2[i], mla_cq_gain[i],
                            mla_ckv_gain[i], mla_w_uq[i], mla_w_ukv[i], mla_qn_gain[i], mla_qr_gain[i],
                            mla_kn_gain[i], mla_kr_gain[i], od_w_out[i], od_norm_ffn[i], moe_router[i],
                            moe_w_gate[i], moe_w_up[i], moe_w_down[i])
    return x2.reshape(b, s, d)
```

```python
import functools
import math

import numpy as np
import jax
import jax.numpy as jnp
from jax import lax
from jax.experimental import pallas as pl
from jax.experimental.pallas import tpu as pltpu

F32 = jnp.float32
BF16 = jnp.bfloat16

ROPE_THETA = 10000.0
NORM_EPS = 1e-6
NEG_INF = -1e30
FORCE_SCORE = 1e6
LOG2E = 1.4426950408889634

DA_HEADS = 4
DA_HEAD_DIM = 64
DA_V_DIM = 2 * DA_HEAD_DIM
SSM_HEADS = 8
SSM_HEAD_DIM = 64
SSM_D_INNER = SSM_HEADS * SSM_HEAD_DIM
SSM_GROUPS = 2
SSM_STATE = 128
SSM_CONV = 4
SSM_CHUNK = 256
NSA_HEADS = 8
NSA_KV_GROUPS = 2
NSA_HEAD_DIM = 64
NSA_CMP_BLOCK = 32
NSA_CMP_STRIDE = 16
NSA_SEL_BLOCK = 64
NSA_TOP_N = 16
NSA_WINDOW = 512
MLA_HEADS = 4
MLA_NOPE_DIM = 128
MLA_ROPE_DIM = 64
MLA_V_DIM = 128
N_EXPERTS = 8

LANES = 128
VMEM_LIMIT = 48 * 1024 * 1024
MOE_VMEM_LIMIT = 58 * 1024 * 1024

NT_DIMS = (((1,), (1,)), ((), ()))


def _cparams(semantics):
    return pltpu.CompilerParams(dimension_semantics=semantics, vmem_limit_bytes=VMEM_LIMIT)


def _dot(a, b):
    return jnp.dot(a, b, preferred_element_type=F32)


def _dot_nt(a, b):
    return lax.dot_general(a, b, NT_DIMS, preferred_element_type=F32)


def _split_bf16(x, parts):
    out = []
    for _ in range(parts):
        hi = x.astype(BF16)
        out.append(hi)
        x = x - hi.astype(F32)
    return out


def _sigmoid(x):
    return 1.0 / (1.0 + jnp.exp(-x))


def _silu(x):
    return x * _sigmoid(x)


def _softplus(x):
    return jnp.maximum(x, 0.0) + jnp.log(1.0 + jnp.exp(-jnp.abs(x)))


def _rms(x, gain):
    ms = jnp.mean(x * x, axis=-1, keepdims=True)
    return x * lax.rsqrt(ms + NORM_EPS) * gain


class HeadNorm:
    def __init__(self, gain, hd, rope=False, mul=1.0):
        self.gain, self.hd, self.rope, self.mul = gain, hd, rope, mul


def _head_norm(y, gain, bd, cos_ref, sin_ref, post):
    n = y.shape[1]
    hd = post.hd
    hi, lo = _split_bf16(y * y, 2)
    ss = _dot(hi, bd) + _dot(lo, bd)
    yn = y * lax.rsqrt(ss * (1.0 / hd) + NORM_EPS) * gain
    if post.rope:
        reps = n // LANES
        cos = jnp.concatenate([cos_ref[...]] * reps, axis=1) if reps > 1 else cos_ref[...]
        sin = jnp.concatenate([sin_ref[...]] * reps, axis=1) if reps > 1 else sin_ref[...]
        lane = lax.broadcasted_iota(jnp.int32, yn.shape, 1)
        first_half = (lane & (hd - 1)) < (hd // 2)
        partner = jnp.where(first_half, pltpu.roll(yn, n - hd // 2, 1), pltpu.roll(yn, hd // 2, 1))
        yn = yn * cos + partner * sin
    if post.mul != 1.0:
        yn = yn * post.mul
    return yn


def _norm_proj_body(x_ref, g_ref, *refs, posts, use_rope):
    if use_rope:
        cos_ref, sin_ref = refs[0], refs[1]
        refs = refs[2:]
    else:
        cos_ref = sin_ref = None
    n_out = len(posts)
    n_aux = 2 * sum(p is not None for p in posts)
    w_refs, aux, o_refs = refs[:n_out], refs[n_out:n_out + n_aux], refs[n_out + n_aux:]
    h = _rms(x_ref[...], g_ref[...]).astype(BF16)
    a = 0
    for w_ref, o_ref, post in zip(w_refs, o_refs, posts):
        y = _dot(h, w_ref[...])
        if post is not None:
            y = _head_norm(y, aux[a][...], aux[a + 1][...], cos_ref, sin_ref, post)
            a += 2
        o_ref[...] = y.astype(o_ref.dtype)


def norm_proj(x2, gain, weights, out_dtypes, posts=None, seq=None, rope_tables=None, tm=512):
    t, d = x2.shape
    posts = posts or [None] * len(weights)
    use_rope = any(p is not None and p.rope for p in posts)
    const = lambda i: (0, 0)
    args = [x2, gain.reshape(1, d).astype(F32)]
    in_specs = [pl.BlockSpec((tm, d), lambda i: (i, 0)), pl.BlockSpec((1, d), const)]
    if use_rope:
        per_seq = seq // tm
        args += list(rope_tables)
        in_specs += [pl.BlockSpec((tm, LANES), lambda i: (i % per_seq, 0))] * 2
    args += list(weights)
    in_specs += [pl.BlockSpec(w.shape, const) for w in weights]
    for w, p in zip(weights, posts):
        if p is not None:
            n = w.shape[1]
            args += [jnp.tile(p.gain.astype(F32), n // p.hd).reshape(1, n), _block_diag_ones(n, p.hd)]
            in_specs += [pl.BlockSpec((1, n), const), pl.BlockSpec((n, n), const)]
    out_specs = [pl.BlockSpec((tm, w.shape[1]), lambda i: (i, 0)) for w in weights]
    out_shape = [jax.ShapeDtypeStruct((t, w.shape[1]), dt) for w, dt in zip(weights, out_dtypes)]
    return pl.pallas_call(
        functools.partial(_norm_proj_body, posts=tuple(posts), use_rope=use_rope),
        grid=(t // tm,), in_specs=in_specs, out_specs=out_specs, out_shape=out_shape,
        compiler_params=_cparams(("parallel",)), name="norm_proj",
    )(*args)


def _block_diag_ones(n, hd):
    idx = np.arange(n) // hd
    return jnp.asarray((idx[:, None] == idx[None, :]).astype(np.float32), dtype=BF16)


def _rope_tables(seq, hd):
    inv_freq = 1.0 / (ROPE_THETA ** (jnp.arange(0, hd, 2, dtype=F32) / hd))
    ang = jnp.arange(seq, dtype=F32)[:, None] * inv_freq[None, :]
    cos, sin = jnp.cos(ang), jnp.sin(ang)
    reps = LANES // hd
    cos_t = jnp.tile(jnp.concatenate([cos, cos], axis=1), (1, reps))
    sin_t = jnp.tile(jnp.concatenate([-sin, sin], axis=1), (1, reps))
    return cos_t, sin_t


def _flash_body(lam_ref, *refs, n_qk, diff, out_scale, sub):
    q_refs = refs[:n_qk]
    k_refs = refs[n_qk:2 * n_qk]
    vt_ref, gain_ref, o_ref, m_ref, l_ref, acc_ref, s0_ref, s1_ref, p0_ref, p1_ref = refs[2 * n_qk:]
    i = pl.program_id(2)
    n_sm = 2 if diff else 1
    _, tk, tq = s0_ref.shape
    s_slots = (s0_ref, s1_ref)
    p_slots = (p0_ref, p1_ref)

    m_ref[...] = jnp.full(m_ref.shape, NEG_INF, F32)
    l_ref[...] = jnp.zeros(l_ref.shape, F32)
    acc_ref[...] = jnp.zeros(acc_ref.shape, F32)

    qs = [r[0] for r in q_refs]
    q = qs[0] if n_qk == 1 else jnp.concatenate(qs, axis=1)
    if diff:
        lane = lax.broadcasted_iota(jnp.int32, q.shape, 1)
        half = q.shape[1] // 2
        zero = jnp.zeros_like(q)
        q_parts = [jnp.where(lane < half, q, zero), jnp.where(lane >= half, q, zero)]
    else:
        q_parts = [q]

    def scores(c, slot, diagonal=False):
        rows = pl.ds(pl.multiple_of(c * tk, tk), tk)
        ks = [r[0, rows, :] for r in k_refs]
        k = ks[0] if n_qk == 1 else jnp.concatenate(ks, axis=1)
        for sm in range(n_sm):
            s = _dot_nt(k, q_parts[sm])
            if diagonal:
                row = lax.broadcasted_iota(jnp.int32, s.shape, 0)
                col = lax.broadcasted_iota(jnp.int32, s.shape, 1)
                s = jnp.where(row <= col, s, NEG_INF)
            s_slots[slot][sm] = s

    def update(c, slot):
        vt = vt_ref[0, :, pl.ds(pl.multiple_of(c * tk, tk), tk)]
        for sm in range(n_sm):
            s_ref, p_ref = s_slots[slot], p_slots[slot]
            m_prev = m_ref[sm]
            m_new = jnp.maximum(m_prev, jnp.max(s_ref[sm], axis=0, keepdims=True))
            m_ref[sm] = m_new
            alpha = jnp.exp2(m_prev - m_new)
            lpart = jnp.zeros((sub, tq), F32)
            for r in range(tk // sub):
                p = jnp.exp2(s_ref[sm, r * sub:(r + 1) * sub, :] - m_new)
                lpart = lpart + p
                p_ref[sm, r * sub:(r + 1) * sub, :] = p.astype(BF16)
            l_ref[sm] = alpha * l_ref[sm] + jnp.sum(lpart, axis=0, keepdims=True)
            acc_ref[sm] = alpha * acc_ref[sm] + _dot(vt, p_ref[sm])

    last_past = jnp.maximum(i - 1, 0)

    @pl.when(i > 0)
    def _():
        scores(0, 0)

    def pair(k2, carry):
        scores(jnp.minimum(2 * k2 + 1, last_past), 1)
        update(2 * k2, 0)
        scores(jnp.minimum(2 * k2 + 2, last_past), 0)
        update(2 * k2 + 1, 1)
        return carry

    lax.fori_loop(0, i // 2, pair, 0)

    @pl.when(i % 2 == 1)
    def _():
        update(i - 1, 0)

    scores(i, 0, diagonal=True)
    update(i, 0)
    o = acc_ref[0] / l_ref[0]
    if diff:
        o = o - lam_ref[0] * (acc_ref[1] / l_ref[1])
        ms = jnp.mean(o * o, axis=0, keepdims=True)
        o = o * lax.rsqrt(ms + NORM_EPS) * gain_ref[...] * out_scale
    o_ref[0] = o.T.astype(o_ref.dtype)


def flash_attention(lam, qs, ks, vt, gain, n_heads, dv, *, diff, out_scale=1.0, tile=512, sub=64):
    b, _, s = vt.shape
    nt = s // tile
    n_qk = len(qs)
    in_specs = [pl.BlockSpec(memory_space=pltpu.SMEM)]
    for q in qs:
        w = q.shape[2] // n_heads
        in_specs.append(pl.BlockSpec((1, tile, w), lambda bb, h, i: (bb, i, h)))
    for q, k in zip(qs, ks):
        w = q.shape[2] // n_heads
        if k.shape[2] == w:
            in_specs.append(pl.BlockSpec((1, s, w), lambda bb, h, i: (bb, 0, 0)))
        else:
            in_specs.append(pl.BlockSpec((1, s, w), lambda bb, h, i: (bb, 0, h)))
    in_specs.append(pl.BlockSpec((1, dv, s), lambda bb, h, i: (bb, h, 0)))
    in_specs.append(pl.BlockSpec((dv, 1), lambda bb, h, i: (0, 0)))
    n_sm = 2 if diff else 1
    return pl.pallas_call(
        functools.partial(_flash_body, n_qk=n_qk, diff=diff, out_scale=out_scale, sub=sub),
        grid=(b, n_heads, nt), in_specs=in_specs,
        out_specs=pl.BlockSpec((1, tile, dv), lambda bb, h, i: (bb, i, h)),
        out_shape=jax.ShapeDtypeStruct((b, s, n_heads * dv), BF16),
        scratch_shapes=[pltpu.VMEM((n_sm, 1, tile), F32), pltpu.VMEM((n_sm, 1, tile), F32),
                        pltpu.VMEM((n_sm, dv, tile), F32),
                        pltpu.VMEM((n_sm, tile, tile), F32), pltpu.VMEM((n_sm, tile, tile), F32),
                        pltpu.VMEM((n_sm, tile, tile), BF16), pltpu.VMEM((n_sm, tile, tile), BF16)],
        compiler_params=_cparams(("parallel", "parallel", "arbitrary")),
        name="flash_diff" if diff else "flash_plain",
    )(lam, *qs, *ks, vt, gain.reshape(dv, 1).astype(F32))


def _ssd_body(xbc_ref, z_ref, dt_ref, dtt_ref, cw_ref, cb_ref, dtb_ref, dtbt_ref, al_ref, alt_ref,
              dsk_ref, ng_ref, o_ref, xpad_ref, state_ref):
    chunk = xbc_ref.shape[1]
    d_in = z_ref.shape[2]
    gn = SSM_GROUPS * SSM_STATE
    c = pl.program_id(1)

    @pl.when(c == 0)
    def _():
        xpad_ref[0:8, :] = jnp.zeros((8, xpad_ref.shape[1]), F32)
        state_ref[...] = jnp.zeros(state_ref.shape, F32)

    xpad_ref[8:8 + chunk, :] = xbc_ref[0]
    conv = cb_ref[...]
    for w in range(SSM_CONV):
        conv = conv + cw_ref[w:w + 1, :] * xpad_ref[pl.ds(8 - (SSM_CONV - 1) + w, chunk), :]
    xpad_ref[0:8, :] = xpad_ref[chunk:chunk + 8, :]
    u = _silu(conv)
    xs = u[:, :d_in]
    bmat = u[:, d_in:d_in + gn]
    cmat = u[:, d_in + gn:]

    dt = _softplus(dt_ref[0] + dtb_ref[...])
    ad = dt * (-jnp.exp(al_ref[...]))
    dtt = _softplus(dtt_ref[0] + dtbt_ref[...])
    adt = dtt * (-jnp.exp(alt_ref[...]))
    row = lax.broadcasted_iota(jnp.int32, (chunk, chunk), 0)
    col = lax.broadcasted_iota(jnp.int32, (chunk, chunk), 1)
    lower = row >= col
    tril = jnp.where(lower, 1.0, 0.0).astype(BF16)
    triu = jnp.where(row <= col, 1.0, 0.0).astype(BF16)
    cs = sum(_dot(tril, part) for part in _split_bf16(ad, 3))
    cst = sum(_dot(part, triu) for part in _split_bf16(adt, 3))

    heads_per_group = SSM_HEADS // SSM_GROUPS
    dsk = dsk_ref[...]
    ys = []
    for g in range(SSM_GROUPS):
        bg = bmat[:, g * SSM_STATE:(g + 1) * SSM_STATE]
        cg = cmat[:, g * SSM_STATE:(g + 1) * SSM_STATE].astype(BF16)
        cb = _dot_nt(cg, bg.astype(BF16))
        bgt = bg.T.astype(BF16)
        for r in range(heads_per_group):
            h = g * heads_per_group + r
            ccol = cs[:, h:h + 1]
            crow = cst[h:h + 1, :]
            decay = jnp.exp(jnp.where(lower, ccol - crow, NEG_INF))
            x_h = xs[:, h * SSM_HEAD_DIM:(h + 1) * SSM_HEAD_DIM]
            xdt = x_h * dt[:, h:h + 1]
            y = _dot((cb * decay).astype(BF16), xdt.astype(BF16))
            st = state_ref[h]
            y = y + _dot(cg, st.astype(BF16)) * jnp.exp(ccol)
            last = cst[h:h + 1, chunk - 1:chunk]
            to_end = jnp.exp(last - ccol)
            state_ref[h] = st * jnp.exp(last) + _dot(bgt, (xdt * to_end).astype(BF16))
            ys.append(y + x_h * dsk[:, h * SSM_HEAD_DIM:(h + 1) * SSM_HEAD_DIM])

    y = jnp.concatenate(ys, axis=1) * _silu(z_ref[0])
    gw = d_in // SSM_GROUPS
    for g in range(SSM_GROUPS):
        seg = y[:, g * gw:(g + 1) * gw]
        o_ref[0, :, g * gw:(g + 1) * gw] = _rms(seg, ng_ref[:, g * gw:(g + 1) * gw]).astype(o_ref.dtype)


def ssd_mixer(xbc, z, dt_raw, conv_w, conv_b, dt_bias, a_log, d_skip, norm_gain):
    b, s, cch = xbc.shape
    d_in = z.shape[2]
    nc = s // SSM_CHUNK
    hpad = dt_raw.shape[2]
    dtt = jnp.transpose(dt_raw[:, :, :SSM_HEADS], (0, 2, 1))

    def lane_pad(v):
        return jnp.pad(v.astype(F32), (0, hpad - SSM_HEADS)).reshape(1, hpad)

    args = (xbc, z, dt_raw, dtt, conv_w.astype(F32), conv_b.reshape(1, cch).astype(F32),
            lane_pad(dt_bias), dt_bias.reshape(SSM_HEADS, 1).astype(F32),
            lane_pad(a_log), a_log.reshape(SSM_HEADS, 1).astype(F32),
            jnp.repeat(d_skip.astype(F32), SSM_HEAD_DIM).reshape(1, d_in),
            norm_gain.reshape(1, d_in).astype(F32))
    const = lambda bb, c: (0, 0)
    in_specs = [pl.BlockSpec((1, SSM_CHUNK, cch), lambda bb, c: (bb, c, 0)),
                pl.BlockSpec((1, SSM_CHUNK, d_in), lambda bb, c: (bb, c, 0)),
                pl.BlockSpec((1, SSM_CHUNK, hpad), lambda bb, c: (bb, c, 0)),
                pl.BlockSpec((1, SSM_HEADS, SSM_CHUNK), lambda bb, c: (bb, 0, c)),
                pl.BlockSpec((SSM_CONV, cch), const), pl.BlockSpec((1, cch), const),
                pl.BlockSpec((1, hpad), const), pl.BlockSpec((SSM_HEADS, 1), const),
                pl.BlockSpec((1, hpad), const), pl.BlockSpec((SSM_HEADS, 1), const),
                pl.BlockSpec((1, d_in), const), pl.BlockSpec((1, d_in), const)]
    return pl.pallas_call(
        _ssd_body, grid=(b, nc), in_specs=in_specs,
        out_specs=pl.BlockSpec((1, SSM_CHUNK, d_in), lambda bb, c: (bb, c, 0)),
        out_shape=jax.ShapeDtypeStruct((b, s, d_in), BF16),
        scratch_shapes=[pltpu.VMEM((SSM_CHUNK + 8, cch), F32),
                        pltpu.VMEM((SSM_HEADS, SSM_STATE, SSM_HEAD_DIM), F32)],
        compiler_params=_cparams(("parallel", "arbitrary")), name="ssd_mixer",
    )(*args)


def _out_proj_body(x_ref, a_ref, b_ref, wa_ref, wb_ref, o_ref):
    o_ref[...] = x_ref[...] + _dot(a_ref[...], wa_ref[...]) + _dot(b_ref[...], wb_ref[...])


def out_proj_residual(x2, a, bm, wa, wb, tm=512):
    t, d = x2.shape
    return pl.pallas_call(
        _out_proj_body, grid=(t // tm,),
        in_specs=[pl.BlockSpec((tm, d), lambda i: (i, 0)),
                  pl.BlockSpec((tm, a.shape[1]), lambda i: (i, 0)),
                  pl.BlockSpec((tm, bm.shape[1]), lambda i: (i, 0)),
                  pl.BlockSpec(wa.shape, lambda i: (0, 0)),
                  pl.BlockSpec(wb.shape, lambda i: (0, 0))],
        out_specs=pl.BlockSpec((tm, d), lambda i: (i, 0)),
        out_shape=jax.ShapeDtypeStruct((t, d), F32),
        compiler_params=_cparams(("parallel",)), name="out_proj",
    )(x2, a, bm, wa, wb)


def _ffn_body(x_ref, g_ref, wg_ref, wu_ref, wd_ref, o_ref, h_ref):
    f = pl.program_id(1)

    @pl.when(f == 0)
    def _():
        x = x_ref[...]
        h_ref[...] = _rms(x, g_ref[...]).astype(BF16)
        o_ref[...] = x

    half = h_ref.shape[0] // 2
    for r in (slice(0, half), slice(half, 2 * half)):
        h = h_ref[r, :]
        act = (_silu(_dot(h, wg_ref[...])) * _dot(h, wu_ref[...])).astype(BF16)
        o_ref[r, :] += _dot(act, wd_ref[...])


def ffn_residual(x2, gain, w_gate, w_up, w_down, tm=1024, tf=1408):
    t, d = x2.shape
    d_ff = w_gate.shape[1]
    return pl.pallas_call(
        _ffn_body, grid=(t // tm, d_ff // tf),
        in_specs=[pl.BlockSpec((tm, d), lambda i, f: (i, 0)),
                  pl.BlockSpec((1, d), lambda i, f: (0, 0)),
                  pl.BlockSpec((d, tf), lambda i, f: (0, f)),
                  pl.BlockSpec((d, tf), lambda i, f: (0, f)),
                  pl.BlockSpec((tf, d), lambda i, f: (f, 0))],
        out_specs=pl.BlockSpec((tm, d), lambda i, f: (i, 0)),
        out_shape=jax.ShapeDtypeStruct((t, d), F32),
        scratch_shapes=[pltpu.VMEM((tm, d), BF16)],
        compiler_params=_cparams(("parallel", "arbitrary")), name="ffn",
    )(x2, gain.reshape(1, d).astype(F32), w_gate, w_up, w_down)


MOE_TOKENS = 1024
MOE_GROUPS = 2
MOE_SLOTS = 288


def _moe_body(x_hbm, g_ref, gates_ref, wg_ref, wu_ref, wd_ref, o_ref,
              h_ref, pos_ref, post_ref, gatest_ref, xe_ref, ye_ref, sem):
    i = pl.program_id(0)
    e = pl.program_id(1)
    f = pl.program_id(2)
    tm = MOE_TOKENS
    n_grp = o_ref.shape[0] // tm
    slots = xe_ref.shape[2]
    sub = 128

    @pl.when((e == 0) & (f == 0))
    def _():
        tile_rows = pl.ds(pl.multiple_of(i * (n_grp * tm), n_grp * tm), n_grp * tm)
        copy = pltpu.make_async_copy(x_hbm.at[tile_rows, :], o_ref, sem)
        copy.start()
        copy.wait()
        for grp in range(n_grp):
            rows = slice(grp * tm, (grp + 1) * tm)
            h_ref[rows, :] = _rms(o_ref[rows, :], g_ref[...]).astype(BF16)
            gates = gates_ref[rows, :]
            routed = jnp.where(gates != 0.0, 1.0, 0.0).astype(BF16)
            for rb in range(tm // sub):
                row = rb * sub + lax.broadcasted_iota(jnp.int32, (sub, tm), 0)
                col = lax.broadcasted_iota(jnp.int32, (sub, tm), 1)
                before = jnp.where(col < row, 1.0, 0.0).astype(BF16)
                pos_ref[grp * tm + rb * sub:grp * tm + (rb + 1) * sub, :] = _dot(before, routed)
            post_ref[grp] = pos_ref[rows, :].T[:N_EXPERTS, :]
            gatest_ref[grp] = gates.T[:N_EXPERTS, :]

    for grp in range(n_grp):
        _moe_group(grp, e, f, tm, slots, gates_ref, wg_ref, wu_ref, wd_ref, o_ref,
                   h_ref, pos_ref, post_ref, gatest_ref, xe_ref, ye_ref)


def _moe_group(grp, e, f, tm, slots, gates_ref, wg_ref, wu_ref, wd_ref, o_ref,
               h_ref, pos_ref, post_ref, gatest_ref, xe_ref, ye_ref):
    rows = slice(grp * tm, (grp + 1) * tm)
    gate_row = gatest_ref[grp, pl.ds(e, 1), :]
    n_routed = jnp.sum(jnp.where(gate_row != 0.0, 1.0, 0.0)).astype(jnp.int32)
    n_chunks = (n_routed + slots - 1) // slots

    @pl.when(f == 0)
    def _():
        pos_row = post_ref[grp, pl.ds(e, 1), :]

        def gather(c, carry):
            slot = (c * slots + lax.broadcasted_iota(jnp.int32, (slots, tm), 0)).astype(F32)
            onehot = jnp.where((pos_row == slot) & (gate_row != 0.0), 1.0, 0.0).astype(BF16)
            xe_ref[grp, c] = _dot(onehot, h_ref[rows, :]).astype(BF16)
            ye_ref[grp, c] = jnp.zeros(ye_ref.shape[2:], F32)
            return carry

        lax.fori_loop(0, n_chunks, gather, 0)

    def expert(c, carry):
        xc = xe_ref[grp, c]
        act = (_silu(_dot(xc, wg_ref[0])) * _dot(xc, wu_ref[0])).astype(BF16)
        ye_ref[grp, c] += _dot(act, wd_ref[0])
        return carry

    lax.fori_loop(0, n_chunks, expert, 0)

    @pl.when(f == pl.num_programs(2) - 1)
    def _():
        lane = lax.broadcasted_iota(jnp.int32, (tm, LANES), 1)
        gate_col = jnp.sum(jnp.where(lane == e, gates_ref[rows, :], 0.0), axis=1, keepdims=True)
        pos_col = jnp.sum(jnp.where(lane == e, pos_ref[rows, :], 0.0), axis=1, keepdims=True)

        def scatter(c, carry):
            slot = (c * slots + lax.broadcasted_iota(jnp.int32, (tm, slots), 1)).astype(F32)
            onehot = jnp.where((pos_col == slot) & (gate_col != 0.0), 1.0, 0.0).astype(BF16)
            o_ref[rows, :] += _dot(onehot, ye_ref[grp, c].astype(BF16)) * gate_col
            return carry

        lax.fori_loop(0, n_chunks, scatter, 0)


def moe_residual(x2, gain, gates, w_gate, w_up, w_down, tf=1408):
    t, d = x2.shape
    n_e, _, d_ff = w_gate.shape
    tm, slots = MOE_TOKENS, MOE_SLOTS
    n_grp = MOE_GROUPS if t % (MOE_GROUPS * tm) == 0 else 1
    tile = n_grp * tm
    max_chunks = -(-tm // slots)
    return pl.pallas_call(
        _moe_body, grid=(t // tile, n_e, d_ff // tf),
        in_specs=[pl.BlockSpec(memory_space=pl.ANY),
                  pl.BlockSpec((1, d), lambda i, e, f: (0, 0)),
                  pl.BlockSpec((tile, LANES), lambda i, e, f: (i, 0)),
                  pl.BlockSpec((1, d, tf), lambda i, e, f: (e, 0, f)),
                  pl.BlockSpec((1, d, tf), lambda i, e, f: (e, 0, f)),
                  pl.BlockSpec((1, tf, d), lambda i, e, f: (e, f, 0))],
        out_specs=pl.BlockSpec((tile, d), lambda i, e, f: (i, 0), pipeline_mode=pl.Buffered(1)),
        out_shape=jax.ShapeDtypeStruct((t, d), F32),
        scratch_shapes=[pltpu.VMEM((tile, d), BF16), pltpu.VMEM((tile, LANES), F32),
                        pltpu.VMEM((n_grp, N_EXPERTS, tm), F32), pltpu.VMEM((n_grp, N_EXPERTS, tm), F32),
                        pltpu.VMEM((n_grp, max_chunks, slots, d), BF16),
                        pltpu.VMEM((n_grp, max_chunks, slots, d), F32),
                        pltpu.SemaphoreType.DMA(())],
        compiler_params=pltpu.CompilerParams(
            dimension_semantics=("parallel", "arbitrary", "arbitrary"), vmem_limit_bytes=MOE_VMEM_LIMIT),
        name="moe_routed",
    )(x2, gain.reshape(1, d).astype(F32), gates, w_gate, w_up, w_down)


def _router_body(x_ref, g_ref, r_ref, o_ref):
    h = _rms(x_ref[...], g_ref[...])
    logits = jnp.dot(h, r_ref[...], precision=lax.Precision.HIGHEST, preferred_element_type=F32)
    lane = lax.broadcasted_iota(jnp.int32, logits.shape, 1).astype(F32)
    low = jnp.float32(-3.0e38)
    logits = jnp.where(lane < N_EXPERTS, logits, low)
    m1 = jnp.max(logits, axis=1, keepdims=True)
    i1 = jnp.min(jnp.where(logits == m1, lane, float(LANES)), axis=1, keepdims=True)
    rest = jnp.where(lane == i1, low, logits)
    m2 = jnp.max(rest, axis=1, keepdims=True)
    i2 = jnp.min(jnp.where(rest == m2, lane, float(LANES)), axis=1, keepdims=True)
    ex = jnp.exp(m2 - m1)
    w1 = 1.0 / (1.0 + ex)
    w2 = ex / (1.0 + ex)
    o_ref[...] = jnp.where(lane == i1, w1, jnp.where(lane == i2, w2, 0.0))


def router_gates(x2, gain, router, tm=512):
    t, d = x2.shape
    r_pad = jnp.pad(router.astype(F32), ((0, 0), (0, LANES - router.shape[1])))
    return pl.pallas_call(
        _router_body, grid=(t // tm,),
        in_specs=[pl.BlockSpec((tm, d), lambda i: (i, 0)),
                  pl.BlockSpec((1, d), lambda i: (0, 0)),
                  pl.BlockSpec((d, LANES), lambda i: (0, 0))],
        out_specs=pl.BlockSpec((tm, LANES), lambda i: (i, 0)),
        out_shape=jax.ShapeDtypeStruct((t, LANES), F32),
        compiler_params=_cparams(("parallel",)), name="router",
    )(x2, gain.reshape(1, d).astype(F32), r_pad)


def _compress_body(ch_ref, nx_ref, pos_ref, w1_ref, w2_ref, gain_ref, cos_ref, sin_ref, rot_ref, o_ref,
                   *, is_key):
    a = _dot((ch_ref[0, 0] + pos_ref[0]).astype(BF16), w1_ref[0])
    a = a + _dot((nx_ref[0, 0] + pos_ref[1]).astype(BF16), w1_ref[1])
    out = _dot(_silu(a).astype(BF16), w2_ref[...])
    if is_key:
        out = _rms(out, gain_ref[...])
        hi, lo = _split_bf16(out, 2)
        partner = _dot(hi, rot_ref[...]) + _dot(lo, rot_ref[...])
        out = out * cos_ref[...] + partner * sin_ref[...]
    o_ref[0, 0] = out


def nsa_compress(t, pos, w1, w2, gain, seq, is_key):
    b, s, _ = t.shape
    g, d = NSA_KV_GROUPS, NSA_HEAD_DIM
    n_ch = s // NSA_CMP_STRIDE
    half = NSA_CMP_STRIDE * d
    ch = t.reshape(b, n_ch, NSA_CMP_STRIDE, g, d).transpose(0, 3, 1, 2, 4).reshape(b, g, n_ch, half)
    nxt = jnp.concatenate([ch[:, :, 1:], jnp.zeros((b, g, 1, half), F32)], axis=2)
    pos2 = pos.astype(F32).reshape(2, 1, half)
    w1s = w1.astype(BF16).reshape(2, half, d)
    cmp_end = jnp.arange(n_ch) * NSA_CMP_STRIDE + NSA_CMP_BLOCK - 1
    inv_freq = 1.0 / (ROPE_THETA ** (jnp.arange(0, d, 2, dtype=F32) / d))
    ang = cmp_end.astype(F32)[:, None] * inv_freq[None, :]
    cos = jnp.concatenate([jnp.cos(ang)] * 2, axis=1)
    sin = jnp.concatenate([jnp.sin(ang)] * 2, axis=1)
    rot = np.zeros((d, d), np.float32)
    rot[np.arange(d // 2) + d // 2, np.arange(d // 2)] = -1.0
    rot[np.arange(d // 2), np.arange(d // 2) + d // 2] = 1.0
    blk = lambda bb, gg: (bb, gg, 0, 0)
    c2 = lambda bb, gg: (0, 0)
    c3 = lambda bb, gg: (0, 0, 0)
    return pl.pallas_call(
        functools.partial(_compress_body, is_key=is_key), grid=(b, g),
        in_specs=[pl.BlockSpec((1, 1, n_ch, half), blk), pl.BlockSpec((1, 1, n_ch, half), blk),
                  pl.BlockSpec((2, 1, half), c3), pl.BlockSpec((2, half, d), c3),
                  pl.BlockSpec((d, d), c2), pl.BlockSpec((1, d), c2),
                  pl.BlockSpec((n_ch, d), c2), pl.BlockSpec((n_ch, d), c2), pl.BlockSpec((d, d), c2)],
        out_specs=pl.BlockSpec((1, 1, n_ch, d), blk),
        out_shape=jax.ShapeDtypeStruct((b, g, n_ch, d), F32),
        compiler_params=_cparams(("parallel", "parallel")), name="nsa_compress",
    )(ch, nxt, pos2, w1s, w2.astype(BF16), gain.reshape(1, d).astype(F32), cos, sin,
      jnp.asarray(rot, dtype=BF16))


def _nsa_body(q_ref, ck_ref, cvt_ref, ksl_ref, vslt_ref, kwn_ref, vwnt_ref, ovt_ref, glt_ref, o_ref,
              sc_ref, phi_ref, plo_ref, imp_ref, bias_ref, ss_ref, ps_ref, ss1_ref, ps1_ref, sw_ref, pw_ref,
              ow_ref, *, tq):
    g = pl.program_id(1)
    i = pl.program_id(2)
    d = NSA_HEAD_DIM
    rep = NSA_HEADS // NSA_KV_GROUPS
    t0 = i * tq
    n_cmp = ck_ref.shape[1]
    n_sel = ovt_ref.shape[0]
    width = rep * tq
    sub = NSA_SEL_BLOCK
    dead = 0.5 * NEG_INF
    v_rows = pl.ds(pl.multiple_of(g * d, d), d)

    qb = q_ref[0]
    q4 = jnp.concatenate([qb[:, r * d:(r + 1) * d] for r in range(rep)], axis=0)
    q4 = jnp.concatenate([q4, q4], axis=1)
    lane = lax.broadcasted_iota(jnp.int32, q4.shape, 1)
    q4 = jnp.where(jnp.right_shift(lane, d.bit_length() - 1) == g, q4, jnp.zeros_like(q4))

    def qpos_of(shape):
        return t0 + (lax.broadcasted_iota(jnp.int32, shape, 1) & (tq - 1))

    s = _dot_nt(ck_ref[0], q4)
    cmp_end = lax.broadcasted_iota(jnp.int32, s.shape, 0) * NSA_CMP_STRIDE + (NSA_CMP_BLOCK - 1)
    s = jnp.where(cmp_end <= qpos_of(s.shape), s, NEG_INF)
    sc_ref[...] = s
    m_c = jnp.max(s, axis=0, keepdims=True)
    lpart = jnp.zeros((sub, width), F32)
    for r in range(n_cmp // sub):
        e = jnp.exp2(sc_ref[r * sub:(r + 1) * sub, :] - m_c)
        lpart = lpart + e
        hi = e.astype(BF16)
        phi_ref[r * sub:(r + 1) * sub, :] = hi
        plo_ref[r * sub:(r + 1) * sub, :] = (e - hi.astype(F32)).astype(BF16)
    inv_c = jnp.where(m_c > dead, 1.0 / jnp.sum(lpart, axis=0, keepdims=True), 0.0)
    o_c = _dot(cvt_ref[0, v_rows, :], phi_ref[...]) * inv_c

    imp4 = (_dot(ovt_ref[...], phi_ref[...]) + _dot(ovt_ref[...], plo_ref[...])) * inv_c
    imp = imp4[:, 0:tq]
    for r in range(1, rep):
        imp = imp + imp4[:, r * tq:(r + 1) * tq]
    blk = lax.broadcasted_iota(jnp.int32, imp.shape, 0)
    qp = t0 + lax.broadcasted_iota(jnp.int32, imp.shape, 1)
    cur = jnp.right_shift(qp, NSA_SEL_BLOCK.bit_length() - 1)
    forced = (blk == 0) | (blk == cur) | (blk == cur - 1)
    future = blk * NSA_SEL_BLOCK > qp
    imp_ref[...] = jnp.where(future, -FORCE_SCORE, jnp.where(forced, FORCE_SCORE, imp))
    bias_ref[...] = jnp.full(bias_ref.shape, NEG_INF, F32)

    n_live = jnp.minimum((t0 + tq - 1) // NSA_SEL_BLOCK + 1, n_sel)
    n_var = max(n_sel // 32, 1)
    rows_per = n_sel // n_var
    top_n = float(min(NSA_TOP_N, n_sel))
    for v in range(n_var):
        rows = rows_per * (v + 1)

        @pl.when((n_live > rows_per * v) & (n_live <= rows))
        def _():
            mine = imp_ref[0:rows, :]
            blk_r = lax.broadcasted_iota(jnp.int32, mine.shape, 0)

            def count(i2, cnt):
                other = imp_ref[pl.ds(i2, 1), :]
                beats = (other > mine) | ((other == mine) & (blk_r > i2))
                return cnt + jnp.where(beats, 1.0, 0.0)

            rank = lax.fori_loop(0, n_live, count, jnp.zeros(mine.shape, F32))
            bias = jnp.where(rank < top_n, 0.0, NEG_INF)
            bias_ref[0:rows, :] = jnp.concatenate([bias] * rep, axis=1)

    init = (jnp.full((1, width), NEG_INF, F32), jnp.zeros((1, width), F32), jnp.zeros((d, width), F32))

    chunk = 8 * sub
    n_sub = chunk // sub

    s_slots = (ss_ref, ss1_ref)
    p_slots = (ps_ref, ps1_ref)

    def sel_scores(c, slot, diagonal=False):
        start = pl.multiple_of(c * chunk, chunk)
        s = _dot_nt(ksl_ref[0, pl.ds(start, chunk), :], q4)
        if diagonal:
            kpos = start + lax.broadcasted_iota(jnp.int32, s.shape, 0)
            s = jnp.where(kpos <= qpos_of(s.shape), s, NEG_INF)
        s_slots[slot][...] = s

    def sel_update(c, slot, carry):
        m_prev, l_prev, acc = carry
        s_ref, p_ref = s_slots[slot], p_slots[slot]
        biases = [bias_ref[pl.ds(c * n_sub + r, 1), :] for r in range(n_sub)]
        m_new = m_prev
        for r in range(n_sub):
            m_new = jnp.maximum(
                m_new, jnp.max(s_ref[r * sub:(r + 1) * sub, :], axis=0, keepdims=True) + biases[r])
        alpha = jnp.exp2(m_prev - m_new)
        live = m_new > dead
        lpart = jnp.zeros((sub, width), F32)
        for r in range(n_sub):
            shift = jnp.where(live, biases[r] - m_new, NEG_INF)
            p = jnp.exp2(s_ref[r * sub:(r + 1) * sub, :] + shift)
            lpart = lpart + p
            p_ref[r * sub:(r + 1) * sub, :] = p.astype(BF16)
        l_new = alpha * l_prev + jnp.sum(lpart, axis=0, keepdims=True)
        vt = vslt_ref[0, v_rows, pl.ds(pl.multiple_of(c * chunk, chunk), chunk)]
        return m_new, l_new, alpha * acc + _dot(vt, p_ref[...])

    c_last = (t0 + tq - 1) // chunk
    last_past = jnp.maximum(c_last - 1, 0)

    @pl.when(c_last > 0)
    def _():
        sel_scores(0, 0)

    def pair(k, carry):
        sel_scores(jnp.minimum(2 * k + 1, last_past), 1)
        carry = sel_update(2 * k, 0, carry)
        sel_scores(jnp.minimum(2 * k + 2, last_past), 0)
        return sel_update(2 * k + 1, 1, carry)

    carry = lax.fori_loop(0, c_last // 2, pair, init)
    carry = lax.cond(c_last % 2 == 1, lambda cr: sel_update(c_last - 1, 0, cr), lambda cr: cr, carry)
    sel_scores(c_last, 0, diagonal=True)
    _, l_s, acc_s = sel_update(c_last, 0, carry)
    o_s = acc_s / l_s

    def win_chunk(c, carry):
        m_prev, l_prev, acc = carry
        start = pl.multiple_of(c * tq, tq)
        s = _dot_nt(kwn_ref[0, pl.ds(start, tq), :], q4)
        kpos = start + lax.broadcasted_iota(jnp.int32, s.shape, 0)
        qpos = qpos_of(s.shape)
        s = jnp.where((kpos <= qpos) & (kpos > qpos - NSA_WINDOW), s, NEG_INF)
        m_new = jnp.maximum(m_prev, jnp.max(s, axis=0, keepdims=True))
        alpha = jnp.exp2(m_prev - m_new)
        p = jnp.exp2(s + jnp.where(m_new > dead, -m_new, NEG_INF))
        l_new = alpha * l_prev + jnp.sum(p, axis=0, keepdims=True)
        vt = vwnt_ref[0, v_rows, pl.ds(start, tq)]
        return m_new, l_new, alpha * acc + _dot(vt, p.astype(BF16))

    n_back = NSA_WINDOW // tq

    @pl.when(i < n_back)
    def _():
        _, l_w, acc_w = lax.fori_loop(0, i + 1, win_chunk, init)
        ow_ref[...] = acc_w / l_w

    @pl.when(i >= n_back)
    def _():
        start = pl.multiple_of(t0 - NSA_WINDOW, tq)
        s = _dot_nt(kwn_ref[0, pl.ds(start, NSA_WINDOW + tq), :], q4)
        kpos = start + lax.broadcasted_iota(jnp.int32, (tq, width), 0)
        qpos = qpos_of((tq, width))
        sw_ref[0:tq, :] = jnp.where(kpos > qpos - NSA_WINDOW, s[0:tq, :], NEG_INF)
        sw_ref[tq:NSA_WINDOW, :] = s[tq:NSA_WINDOW, :]
        sw_ref[NSA_WINDOW:, :] = jnp.where(kpos + NSA_WINDOW <= qpos, s[NSA_WINDOW:, :], NEG_INF)
        m_w = jnp.max(sw_ref[...], axis=0, keepdims=True)
        lpart = jnp.zeros((sub, width), F32)
        for r in range((NSA_WINDOW + tq) // sub):
            p = jnp.exp2(sw_ref[r * sub:(r + 1) * sub, :] - m_w)
            lpart = lpart + p
            pw_ref[r * sub:(r + 1) * sub, :] = p.astype(BF16)
        vt = vwnt_ref[0, v_rows, pl.ds(start, NSA_WINDOW + tq)]
        ow_ref[...] = _dot(vt, pw_ref[...]) / jnp.sum(lpart, axis=0, keepdims=True)

    o_w = ow_ref[...]

    def gate(branch):
        rows = [glt_ref[0, pl.ds((g * rep + r) * 3 + branch, 1), :] for r in range(rep)]
        return _sigmoid(jnp.concatenate(rows, axis=1))

    o_ref[0, 0, 0] = gate(0) * o_c + gate(1) * o_s + gate(2) * o_w


def nsa_overlap_t(n_cmp, n_sel):
    c_start = np.arange(n_cmp)[None, :] * NSA_CMP_STRIDE
    s_start = np.arange(n_sel)[:, None] * NSA_SEL_BLOCK
    hit = (c_start < s_start + NSA_SEL_BLOCK) & (c_start + NSA_CMP_BLOCK > s_start)
    hit = hit & (np.arange(n_cmp)[None, :] < n_cmp - NSA_CMP_BLOCK // NSA_CMP_STRIDE + 1)
    return jnp.asarray(hit.astype(np.float32), dtype=BF16)


def nsa_attention(qn, ck, cvt, ksl, vslt, kwn, vwnt, glt, tq=128):
    b, s, _ = qn.shape
    g, d = NSA_KV_GROUPS, NSA_HEAD_DIM
    rep = NSA_HEADS // g
    n_cmp = ck.shape[1]
    n_sel = s // NSA_SEL_BLOCK
    nq = s // tq
    ovt = nsa_overlap_t(n_cmp, n_sel)
    width = rep * tq
    chunk = 8 * NSA_SEL_BLOCK
    full3 = lambda bb, gg, i: (bb, 0, 0)
    return pl.pallas_call(
        functools.partial(_nsa_body, tq=tq), grid=(b, g, nq),
        in_specs=[pl.BlockSpec((1, tq, rep * d), lambda bb, gg, i: (bb, i, gg)),
                  pl.BlockSpec((1, n_cmp, g * d), full3), pl.BlockSpec((1, g * d, n_cmp), full3),
                  pl.BlockSpec((1, s, g * d), full3), pl.BlockSpec((1, g * d, s), full3),
                  pl.BlockSpec((1, s, g * d), full3), pl.BlockSpec((1, g * d, s), full3),
                  pl.BlockSpec((n_sel, n_cmp), lambda bb, gg, i: (0, 0)),
                  pl.BlockSpec((1, glt.shape[1], tq), lambda bb, gg, i: (bb, 0, i))],
        out_specs=pl.BlockSpec((1, 1, 1, d, rep * tq), lambda bb, gg, i: (bb, gg, i, 0, 0)),
        out_shape=jax.ShapeDtypeStruct((b, g, nq, d, rep * tq), F32),
        scratch_shapes=[pltpu.VMEM((n_cmp, width), F32), pltpu.VMEM((n_cmp, width), BF16),
                        pltpu.VMEM((n_cmp, width), BF16), pltpu.VMEM((n_sel, tq), F32),
                        pltpu.VMEM((n_sel, width), F32), pltpu.VMEM((chunk, width), F32),
                        pltpu.VMEM((chunk, width), BF16), pltpu.VMEM((chunk, width), F32),
                        pltpu.VMEM((chunk, width), BF16), pltpu.VMEM((NSA_WINDOW + tq, width), F32),
                        pltpu.VMEM((NSA_WINDOW + tq, width), BF16), pltpu.VMEM((d, width), F32)],
        compiler_params=_cparams(("parallel", "parallel", "arbitrary")), name="nsa_attention",
    )(qn, ck, cvt, ksl, vslt, kwn, vwnt, ovt, glt)


def _pad_cols(w, n):
    return jnp.pad(w, ((0, 0), (0, n - w.shape[1])))


def _even_layer(x2, b, s, layer_idx, norm_mix, w_in, q_gain, k_gain, lam, subln_gain, conv_w, conv_b,
                dt_bias, a_log, d_skip, ssm_norm_gain, w_out, norm_ffn, w_gate, w_up, w_down):
    nq = DA_HEADS * 2 * DA_HEAD_DIM
    nv = DA_HEADS * DA_V_DIM
    cch = SSM_D_INNER + 2 * SSM_GROUPS * SSM_STATE
    offs = np.cumsum([0, nq, nq, nv, SSM_D_INNER, cch, SSM_HEADS])
    wb = w_in.astype(BF16)
    pieces = [wb[:, offs[k]:offs[k + 1]] for k in range(6)]
    pieces[5] = _pad_cols(pieces[5], LANES)
    posts = [HeadNorm(q_gain, DA_HEAD_DIM, rope=True, mul=DA_HEAD_DIM ** -0.5 * LOG2E),
             HeadNorm(k_gain, DA_HEAD_DIM, rope=True), None, None, None, None]
    q, k, v, z, xbc, dt = norm_proj(x2, norm_mix, pieces, [BF16, BF16, BF16, F32, F32, F32], posts, s,
                                    _rope_tables(s, DA_HEAD_DIM))
    qn = q.reshape(b, s, nq)
    kn = k.reshape(b, s, nq)
    vt = v.reshape(b, s, nv).transpose(0, 2, 1)
    lam_init = 0.8 - 0.6 * math.exp(-0.3 * layer_idx)
    lf = lam.astype(F32)
    lam_full = jnp.exp(jnp.sum(lf[0] * lf[1])) - jnp.exp(jnp.sum(lf[2] * lf[3])) + lam_init
    a_out = flash_attention(lam_full.reshape(1), [qn], [kn], vt, subln_gain, DA_HEADS, DA_V_DIM,
                            diff=True, out_scale=1.0 - lam_init)
    b_out = ssd_mixer(xbc.reshape(b, s, cch), z.reshape(b, s, SSM_D_INNER), dt.reshape(b, s, LANES),
                      conv_w, conv_b, dt_bias, a_log, d_skip, ssm_norm_gain)
    wo = w_out.astype(BF16)
    x2 = out_proj_residual(x2, a_out.reshape(-1, nv), b_out.reshape(-1, SSM_D_INNER), wo[:nv], wo[nv:])
    return ffn_residual(x2, norm_ffn, w_gate.astype(BF16), w_up.astype(BF16), w_down.astype(BF16))


def _odd_layer(x2, b, s, norm_mix, w_in, q_gain, k_gain, cmp_pos, cmp_w1, cmp_w2, cq_gain, ckv_gain,
               w_uq, w_ukv, qn_gain, qr_gain, kn_gain, kr_gain, w_out, norm_ffn, router, w_gate, w_up,
               w_down):
    g, d = NSA_KV_GROUPS, NSA_HEAD_DIM
    nq = NSA_HEADS * d
    nkv = g * d
    sizes = [nq] + [nkv] * 6 + [NSA_HEADS * 3, w_uq.shape[0], w_ukv.shape[0], MLA_ROPE_DIM]
    offs = np.cumsum([0] + sizes)
    wb = w_in.astype(BF16)
    pieces = [wb[:, offs[k]:offs[k + 1]] for k in range(len(sizes))]
    pieces[7] = _pad_cols(pieces[7], LANES)
    pieces[10] = _pad_cols(pieces[10], LANES)
    tables = _rope_tables(s, d)
    posts = [None] * len(sizes)
    posts[0] = HeadNorm(q_gain, d, rope=True, mul=d ** -0.5 * LOG2E)
    posts[3] = HeadNorm(k_gain[1], d, rope=True)
    posts[5] = HeadNorm(k_gain[2], d, rope=True)
    posts[10] = HeadNorm(kr_gain, MLA_ROPE_DIM, rope=True)
    (q, kc, vc, ksl, vsl, kwn, vwn, gl, cq, ckv, k_rope) = norm_proj(
        x2, norm_mix, pieces, [BF16, F32, F32, BF16, BF16, BF16, BF16, F32, F32, F32, BF16], posts, s, tables)

    qn = q.reshape(b, s, nq)
    ksl_n = ksl.reshape(b, s, nkv)
    kwn_n = kwn.reshape(b, s, nkv)
    ck = nsa_compress(kc.reshape(b, s, nkv), cmp_pos[0], cmp_w1[0], cmp_w2[0], k_gain[0], s, True)
    cv = nsa_compress(vc.reshape(b, s, nkv), cmp_pos[1], cmp_w1[1], cmp_w2[1], k_gain[0], s, False)
    n_cmp = ck.shape[2]
    ck = ck.transpose(0, 2, 1, 3).reshape(b, n_cmp, nkv).astype(BF16)
    cvt = cv.transpose(0, 1, 3, 2).reshape(b, nkv, n_cmp).astype(BF16)
    vslt = vsl.reshape(b, s, nkv).transpose(0, 2, 1)
    vwnt = vwn.reshape(b, s, nkv).transpose(0, 2, 1)
    glt = gl.reshape(b, s, LANES)[:, :, :32].transpose(0, 2, 1)
    tq = 128
    o = nsa_attention(qn, ck, cvt, ksl_n, vslt, kwn_n, vwnt, glt, tq=tq)
    rep = NSA_HEADS // g
    c_out = o.reshape(b, g, s // tq, d, rep, tq).transpose(0, 2, 5, 1, 4, 3).reshape(b * s, nq)
    c_out = c_out.astype(BF16)

    h = MLA_HEADS
    dqk = MLA_NOPE_DIM + MLA_ROPE_DIM
    wq = w_uq.astype(BF16).reshape(-1, h, dqk)
    wq_nope = wq[:, :, :MLA_NOPE_DIM].reshape(-1, h * MLA_NOPE_DIM)
    wq_rope = jnp.pad(wq[:, :, MLA_NOPE_DIM:], ((0, 0), (0, 0), (0, LANES - MLA_ROPE_DIM)))
    wq_rope = wq_rope.reshape(-1, h * LANES)
    wkv = w_ukv.astype(BF16).reshape(-1, h, MLA_NOPE_DIM + MLA_V_DIM)
    wk_nope = wkv[:, :, :MLA_NOPE_DIM].reshape(-1, h * MLA_NOPE_DIM)
    wv = wkv[:, :, MLA_NOPE_DIM:].reshape(-1, h * MLA_V_DIM)
    q_mul = dqk ** -0.5 * LOG2E
    q_nope, q_rope = norm_proj(
        cq, cq_gain, [wq_nope, wq_rope], [BF16, BF16],
        [HeadNorm(qn_gain, MLA_NOPE_DIM, mul=q_mul), HeadNorm(qr_gain, MLA_ROPE_DIM, rope=True, mul=q_mul)],
        s, tables)
    k_nope, v = norm_proj(ckv, ckv_gain, [wk_nope, wv], [BF16, BF16], [HeadNorm(kn_gain, MLA_NOPE_DIM), None])
    shp = lambda t: t.reshape(b, s, t.shape[-1])
    d_out = flash_attention(jnp.zeros((1,), F32), [shp(q_nope), shp(q_rope)], [shp(k_nope), shp(k_rope)],
                            shp(v).transpose(0, 2, 1), jnp.ones((MLA_V_DIM,), F32), h, MLA_V_DIM,
                            diff=False)

    wo = w_out.astype(BF16)
    x2 = out_proj_residual(x2, c_out, d_out.reshape(b * s, h * MLA_V_DIM), wo[:nq], wo[nq:])
    gates = router_gates(x2, norm_ffn, router)
    return moe_residual(x2, norm_ffn, gates, w_gate.astype(BF16), w_up.astype(BF16), w_down.astype(BF16))


def kernel(x, ev_norm_mix, ev_w_in, da_q_gain, da_k_gain, da_lambda, da_subln_gain, ssm_conv_w, ssm_conv_b, ssm_dt_bias, ssm_a_log, ssm_d, ssm_norm_gain, ev_w_out, ev_norm_ffn, ffn_w_gate, ffn_w_up, ffn_w_down, od_norm_mix, od_w_in, nsa_q_gain, nsa_k_gain, nsa_cmp_pos, nsa_cmp_w1, nsa_cmp_w2, mla_cq_gain, mla_ckv_gain, mla_w_uq, mla_w_ukv, mla_qn_gain, mla_qr_gain, mla_kn_gain, mla_kr_gain, od_w_out, od_norm_ffn, moe_router, moe_w_gate, moe_w_up, moe_w_down):
    b, s, d = x.shape
    x2 = x.reshape(b * s, d)
    depth = ev_norm_mix.shape[0] + od_norm_mix.shape[0]
    for layer in range(depth):
        i = layer // 2
        if layer % 2 == 0:
            x2 = _even_layer(x2, b, s, layer, ev_norm_mix[i], ev_w_in[i], da_q_gain[i], da_k_gain[i],
                             da_lambda[i], da_subln_gain[i], ssm_conv_w[i], ssm_conv_b[i],
                             ssm_dt_bias[i], ssm_a_log[i], ssm_d[i], ssm_norm_gain[i], ev_w_out[i],
                             ev_norm_ffn[i], ffn_w_gate[i], ffn_w_up[i], ffn_w_down[i])
        else:
            x2 = _odd_layer(x2, b, s, od_norm_mix[i], od_w_in[i], nsa_q_gain[i], nsa_k_gain[i],
                            nsa_cmp_pos[i], nsa_cmp_w1[i], nsa_cmp_w2[i], mla_cq_gain[i],
                            mla_ckv_gain[i], mla_w_uq[i], mla_w_ukv[i], mla_qn_gain[i], mla_qr_gain[i],
                            mla_kn_gain[i], mla_kr_gain[i], od_w_out[i], od_norm_ffn[i], moe_router[i],
                            moe_w_gate[i], moe_w_up[i], moe_w_down[i])
    return x2.reshape(b, s, d)
```

```python
import functools
import math

import numpy as np
import jax
import jax.numpy as jnp
from jax import lax
from jax.experimental import pallas as pl
from jax.experimental.pallas import tpu as pltpu

F32 = jnp.float32
BF16 = jnp.bfloat16

ROPE_THETA = 10000.0
NORM_EPS = 1e-6
NEG_INF = -1e30
FORCE_SCORE = 1e6
LOG2E = 1.4426950408889634

DA_HEADS = 4
DA_HEAD_DIM = 64
DA_V_DIM = 2 * DA_HEAD_DIM
SSM_HEADS = 8
SSM_HEAD_DIM = 64
SSM_D_INNER = SSM_HEADS * SSM_HEAD_DIM
SSM_GROUPS = 2
SSM_STATE = 128
SSM_CONV = 4
SSM_CHUNK = 256
NSA_HEADS = 8
NSA_KV_GROUPS = 2
NSA_HEAD_DIM = 64
NSA_CMP_BLOCK = 32
NSA_CMP_STRIDE = 16
NSA_SEL_BLOCK = 64
NSA_TOP_N = 16
NSA_WINDOW = 512
MLA_HEADS = 4
MLA_NOPE_DIM = 128
MLA_ROPE_DIM = 64
MLA_V_DIM = 128
N_EXPERTS = 8

LANES = 128
VMEM_LIMIT = 48 * 1024 * 1024
MOE_VMEM_LIMIT = 58 * 1024 * 1024

NT_DIMS = (((1,), (1,)), ((), ()))


def _cparams(semantics):
    return pltpu.CompilerParams(dimension_semantics=semantics, vmem_limit_bytes=VMEM_LIMIT)


def _dot(a, b):
    return jnp.dot(a, b, preferred_element_type=F32)


def _dot_nt(a, b):
    return lax.dot_general(a, b, NT_DIMS, preferred_element_type=F32)


def _split_bf16(x, parts):
    out = []
    for _ in range(parts):
        hi = x.astype(BF16)
        out.append(hi)
        x = x - hi.astype(F32)
    return out


def _sigmoid(x):
    return 1.0 / (1.0 + jnp.exp(-x))


def _silu(x):
    return x * _sigmoid(x)


def _softplus(x):
    return jnp.maximum(x, 0.0) + jnp.log(1.0 + jnp.exp(-jnp.abs(x)))


def _rms(x, gain):
    ms = jnp.mean(x * x, axis=-1, keepdims=True)
    return x * lax.rsqrt(ms + NORM_EPS) * gain


class HeadNorm:
    def __init__(self, gain, hd, rope=False, mul=1.0):
        self.gain, self.hd, self.rope, self.mul = gain, hd, rope, mul


def _head_norm(y, gain, bd, cos_ref, sin_ref, post):
    n = y.shape[1]
    hd = post.hd
    hi, lo = _split_bf16(y * y, 2)
    ss = _dot(hi, bd) + _dot(lo, bd)
    yn = y * lax.rsqrt(ss * (1.0 / hd) + NORM_EPS) * gain
    if post.rope:
        reps = n // LANES
        cos = jnp.concatenate([cos_ref[...]] * reps, axis=1) if reps > 1 else cos_ref[...]
        sin = jnp.concatenate([sin_ref[...]] * reps, axis=1) if reps > 1 else sin_ref[...]
        lane = lax.broadcasted_iota(jnp.int32, yn.shape, 1)
        first_half = (lane & (hd - 1)) < (hd // 2)
        partner = jnp.where(first_half, pltpu.roll(yn, n - hd // 2, 1), pltpu.roll(yn, hd // 2, 1))
        yn = yn * cos + partner * sin
    if post.mul != 1.0:
        yn = yn * post.mul
    return yn


def _norm_proj_body(x_ref, g_ref, *refs, posts, use_rope):
    if use_rope:
        cos_ref, sin_ref = refs[0], refs[1]
        refs = refs[2:]
    else:
        cos_ref = sin_ref = None
    n_out = len(posts)
    n_aux = 2 * sum(p is not None for p in posts)
    w_refs, aux, o_refs = refs[:n_out], refs[n_out:n_out + n_aux], refs[n_out + n_aux:]
    h = _rms(x_ref[...], g_ref[...]).astype(BF16)
    a = 0
    for w_ref, o_ref, post in zip(w_refs, o_refs, posts):
        y = _dot(h, w_ref[...])
        if post is not None:
            y = _head_norm(y, aux[a][...], aux[a + 1][...], cos_ref, sin_ref, post)
            a += 2
        o_ref[...] = y.astype(o_ref.dtype)


def norm_proj(x2, gain, weights, out_dtypes, posts=None, seq=None, rope_tables=None, tm=512):
    t, d = x2.shape
    posts = posts or [None] * len(weights)
    use_rope = any(p is not None and p.rope for p in posts)
    const = lambda i: (0, 0)
    args = [x2, gain.reshape(1, d).astype(F32)]
    in_specs = [pl.BlockSpec((tm, d), lambda i: (i, 0)), pl.BlockSpec((1, d), const)]
    if use_rope:
        per_seq = seq // tm
        args += list(rope_tables)
        in_specs += [pl.BlockSpec((tm, LANES), lambda i: (i % per_seq, 0))] * 2
    args += list(weights)
    in_specs += [pl.BlockSpec(w.shape, const) for w in weights]
    for w, p in zip(weights, posts):
        if p is not None:
            n = w.shape[1]
            args += [jnp.tile(p.gain.astype(F32), n // p.hd).reshape(1, n), _block_diag_ones(n, p.hd)]
            in_specs += [pl.BlockSpec((1, n), const), pl.BlockSpec((n, n), const)]
    out_specs = [pl.BlockSpec((tm, w.shape[1]), lambda i: (i, 0)) for w in weights]
    out_shape = [jax.ShapeDtypeStruct((t, w.shape[1]), dt) for w, dt in zip(weights, out_dtypes)]
    return pl.pallas_call(
        functools.partial(_norm_proj_body, posts=tuple(posts), use_rope=use_rope),
        grid=(t // tm,), in_specs=in_specs, out_specs=out_specs, out_shape=out_shape,
        compiler_params=_cparams(("parallel",)), name="norm_proj",
    )(*args)


def _block_diag_ones(n, hd):
    idx = np.arange(n) // hd
    return jnp.asarray((idx[:, None] == idx[None, :]).astype(np.float32), dtype=BF16)


def _rope_tables(seq, hd):
    inv_freq = 1.0 / (ROPE_THETA ** (jnp.arange(0, hd, 2, dtype=F32) / hd))
    ang = jnp.arange(seq, dtype=F32)[:, None] * inv_freq[None, :]
    cos, sin = jnp.cos(ang), jnp.sin(ang)
    reps = LANES // hd
    cos_t = jnp.tile(jnp.concatenate([cos, cos], axis=1), (1, reps))
    sin_t = jnp.tile(jnp.concatenate([-sin, sin], axis=1), (1, reps))
    return cos_t, sin_t


def _flash_body(lam_ref, *refs, n_qk, diff, out_scale, sub):
    q_refs = refs[:n_qk]
    k_refs = refs[n_qk:2 * n_qk]
    vt_ref, gain_ref, o_ref, m_ref, l_ref, acc_ref, s0_ref, s1_ref, p0_ref, p1_ref = refs[2 * n_qk:]
    i = pl.program_id(2)
    n_sm = 2 if diff else 1
    _, tk, tq = s0_ref.shape
    s_slots = (s0_ref, s1_ref)
    p_slots = (p0_ref, p1_ref)

    m_ref[...] = jnp.full(m_ref.shape, NEG_INF, F32)
    l_ref[...] = jnp.zeros(l_ref.shape, F32)
    acc_ref[...] = jnp.zeros(acc_ref.shape, F32)

    qs = [r[0] for r in q_refs]
    q = qs[0] if n_qk == 1 else jnp.concatenate(qs, axis=1)
    if diff:
        lane = lax.broadcasted_iota(jnp.int32, q.shape, 1)
        half = q.shape[1] // 2
        zero = jnp.zeros_like(q)
        q_parts = [jnp.where(lane < half, q, zero), jnp.where(lane >= half, q, zero)]
    else:
        q_parts = [q]

    def scores(c, slot, diagonal=False):
        rows = pl.ds(pl.multiple_of(c * tk, tk), tk)
        ks = [r[0, rows, :] for r in k_refs]
        k = ks[0] if n_qk == 1 else jnp.concatenate(ks, axis=1)
        for sm in range(n_sm):
            s = _dot_nt(k, q_parts[sm])
            if diagonal:
                row = lax.broadcasted_iota(jnp.int32, s.shape, 0)
                col = lax.broadcasted_iota(jnp.int32, s.shape, 1)
                s = jnp.where(row <= col, s, NEG_INF)
            s_slots[slot][sm] = s

    def update(c, slot):
        vt = vt_ref[0, :, pl.ds(pl.multiple_of(c * tk, tk), tk)]
        for sm in range(n_sm):
            s_ref, p_ref = s_slots[slot], p_slots[slot]
            m_prev = m_ref[sm]
            m_new = jnp.maximum(m_prev, jnp.max(s_ref[sm], axis=0, keepdims=True))
            m_ref[sm] = m_new
            alpha = jnp.exp2(m_prev - m_new)
            lpart = jnp.zeros((sub, tq), F32)
            for r in range(tk // sub):
                p = jnp.exp2(s_ref[sm, r * sub:(r + 1) * sub, :] - m_new)
                lpart = lpart + p
                p_ref[sm, r * sub:(r + 1) * sub, :] = p.astype(BF16)
            l_ref[sm] = alpha * l_ref[sm] + jnp.sum(lpart, axis=0, keepdims=True)
            acc_ref[sm] = alpha * acc_ref[sm] + _dot(vt, p_ref[sm])

    last_past = jnp.maximum(i - 1, 0)

    @pl.when(i > 0)
    def _():
        scores(0, 0)

    def pair(k2, carry):
        scores(jnp.minimum(2 * k2 + 1, last_past), 1)
        update(2 * k2, 0)
        scores(jnp.minimum(2 * k2 + 2, last_past), 0)
        update(2 * k2 + 1, 1)
        return carry

    lax.fori_loop(0, i // 2, pair, 0)

    @pl.when(i % 2 == 1)
    def _():
        update(i - 1, 0)

    scores(i, 0, diagonal=True)
    update(i, 0)
    o = acc_ref[0] / l_ref[0]
    if diff:
        o = o - lam_ref[0] * (acc_ref[1] / l_ref[1])
        ms = jnp.mean(o * o, axis=0, keepdims=True)
        o = o * lax.rsqrt(ms + NORM_EPS) * gain_ref[...] * out_scale
    o_ref[0] = o.T.astype(o_ref.dtype)


def flash_attention(lam, qs, ks, vt, gain, n_heads, dv, *, diff, out_scale=1.0, tile=512, sub=64):
    b, _, s = vt.shape
    nt = s // tile
    n_qk = len(qs)
    in_specs = [pl.BlockSpec(memory_space=pltpu.SMEM)]
    for q in qs:
        w = q.shape[2] // n_heads
        in_specs.append(pl.BlockSpec((1, tile, w), lambda bb, h, i: (bb, i, h)))
    for q, k in zip(qs, ks):
        w = q.shape[2] // n_heads
        if k.shape[2] == w:
            in_specs.append(pl.BlockSpec((1, s, w), lambda bb, h, i: (bb, 0, 0)))
        else:
            in_specs.append(pl.BlockSpec((1, s, w), lambda bb, h, i: (bb, 0, h)))
    in_specs.append(pl.BlockSpec((1, dv, s), lambda bb, h, i: (bb, h, 0)))
    in_specs.append(pl.BlockSpec((dv, 1), lambda bb, h, i: (0, 0)))
    n_sm = 2 if diff else 1
    return pl.pallas_call(
        functools.partial(_flash_body, n_qk=n_qk, diff=diff, out_scale=out_scale, sub=sub),
        grid=(b, n_heads, nt), in_specs=in_specs,
        out_specs=pl.BlockSpec((1, tile, dv), lambda bb, h, i: (bb, i, h)),
        out_shape=jax.ShapeDtypeStruct((b, s, n_heads * dv), BF16),
        scratch_shapes=[pltpu.VMEM((n_sm, 1, tile), F32), pltpu.VMEM((n_sm, 1, tile), F32),
                        pltpu.VMEM((n_sm, dv, tile), F32),
                        pltpu.VMEM((n_sm, tile, tile), F32), pltpu.VMEM((n_sm, tile, tile), F32),
                        pltpu.VMEM((n_sm, tile, tile), BF16), pltpu.VMEM((n_sm, tile, tile), BF16)],
        compiler_params=_cparams(("parallel", "parallel", "arbitrary")),
        name="flash_diff" if diff else "flash_plain",
    )(lam, *qs, *ks, vt, gain.reshape(dv, 1).astype(F32))


def _ssd_body(xbc_ref, z_ref, dt_ref, dtt_ref, cw_ref, cb_ref, dtb_ref, dtbt_ref, al_ref, alt_ref,
              dsk_ref, ng_ref, o_ref, xpad_ref, state_ref):
    chunk = xbc_ref.shape[1]
    d_in = z_ref.shape[2]
    gn = SSM_GROUPS * SSM_STATE
    c = pl.program_id(1)

    @pl.when(c == 0)
    def _():
        xpad_ref[0:8, :] = jnp.zeros((8, xpad_ref.shape[1]), F32)
        state_ref[...] = jnp.zeros(state_ref.shape, F32)

    xpad_ref[8:8 + chunk, :] = xbc_ref[0]
    conv = cb_ref[...]
    for w in range(SSM_CONV):
        conv = conv + cw_ref[w:w + 1, :] * xpad_ref[pl.ds(8 - (SSM_CONV - 1) + w, chunk), :]
    xpad_ref[0:8, :] = xpad_ref[chunk:chunk + 8, :]
    u = _silu(conv)
    xs = u[:, :d_in]
    bmat = u[:, d_in:d_in + gn]
    cmat = u[:, d_in + gn:]

    dt = _softplus(dt_ref[0] + dtb_ref[...])
    ad = dt * (-jnp.exp(al_ref[...]))
    dtt = _softplus(dtt_ref[0] + dtbt_ref[...])
    adt = dtt * (-jnp.exp(alt_ref[...]))
    row = lax.broadcasted_iota(jnp.int32, (chunk, chunk), 0)
    col = lax.broadcasted_iota(jnp.int32, (chunk, chunk), 1)
    lower = row >= col
    tril = jnp.where(lower, 1.0, 0.0).astype(BF16)
    triu = jnp.where(row <= col, 1.0, 0.0).astype(BF16)
    cs = sum(_dot(tril, part) for part in _split_bf16(ad, 3))
    cst = sum(_dot(part, triu) for part in _split_bf16(adt, 3))

    heads_per_group = SSM_HEADS // SSM_GROUPS
    dsk = dsk_ref[...]
    ys = []
    for g in range(SSM_GROUPS):
        bg = bmat[:, g * SSM_STATE:(g + 1) * SSM_STATE]
        cg = cmat[:, g * SSM_STATE:(g + 1) * SSM_STATE].astype(BF16)
        cb = _dot_nt(cg, bg.astype(BF16))
        bgt = bg.T.astype(BF16)
        for r in range(heads_per_group):
            h = g * heads_per_group + r
            ccol = cs[:, h:h + 1]
            crow = cst[h:h + 1, :]
            decay = jnp.exp(jnp.where(lower, ccol - crow, NEG_INF))
            x_h = xs[:, h * SSM_HEAD_DIM:(h + 1) * SSM_HEAD_DIM]
            xdt = x_h * dt[:, h:h + 1]
            y = _dot((cb * decay).astype(BF16), xdt.astype(BF16))
            st = state_ref[h]
            y = y + _dot(cg, st.astype(BF16)) * jnp.exp(ccol)
            last = cst[h:h + 1, chunk - 1:chunk]
            to_end = jnp.exp(last - ccol)
            state_ref[h] = st * jnp.exp(last) + _dot(bgt, (xdt * to_end).astype(BF16))
            ys.append(y + x_h * dsk[:, h * SSM_HEAD_DIM:(h + 1) * SSM_HEAD_DIM])

    y = jnp.concatenate(ys, axis=1) * _silu(z_ref[0])
    gw = d_in // SSM_GROUPS
    for g in range(SSM_GROUPS):
        seg = y[:, g * gw:(g + 1) * gw]
        o_ref[0, :, g * gw:(g + 1) * gw] = _rms(seg, ng_ref[:, g * gw:(g + 1) * gw]).astype(o_ref.dtype)


def ssd_mixer(xbc, z, dt_raw, conv_w, conv_b, dt_bias, a_log, d_skip, norm_gain):
    b, s, cch = xbc.shape
    d_in = z.shape[2]
    nc = s // SSM_CHUNK
    hpad = dt_raw.shape[2]
    dtt = jnp.transpose(dt_raw[:, :, :SSM_HEADS], (0, 2, 1))

    def lane_pad(v):
        return jnp.pad(v.astype(F32), (0, hpad - SSM_HEADS)).reshape(1, hpad)

    args = (xbc, z, dt_raw, dtt, conv_w.astype(F32), conv_b.reshape(1, cch).astype(F32),
            lane_pad(dt_bias), dt_bias.reshape(SSM_HEADS, 1).astype(F32),
            lane_pad(a_log), a_log.reshape(SSM_HEADS, 1).astype(F32),
            jnp.repeat(d_skip.astype(F32), SSM_HEAD_DIM).reshape(1, d_in),
            norm_gain.reshape(1, d_in).astype(F32))
    const = lambda bb, c: (0, 0)
    in_specs = [pl.BlockSpec((1, SSM_CHUNK, cch), lambda bb, c: (bb, c, 0)),
                pl.BlockSpec((1, SSM_CHUNK, d_in), lambda bb, c: (bb, c, 0)),
                pl.BlockSpec((1, SSM_CHUNK, hpad), lambda bb, c: (bb, c, 0)),
                pl.BlockSpec((1, SSM_HEADS, SSM_CHUNK), lambda bb, c: (bb, 0, c)),
                pl.BlockSpec((SSM_CONV, cch), const), pl.BlockSpec((1, cch), const),
                pl.BlockSpec((1, hpad), const), pl.BlockSpec((SSM_HEADS, 1), const),
                pl.BlockSpec((1, hpad), const), pl.BlockSpec((SSM_HEADS, 1), const),
                pl.BlockSpec((1, d_in), const), pl.BlockSpec((1, d_in), const)]
    return pl.pallas_call(
        _ssd_body, grid=(b, nc), in_specs=in_specs,
        out_specs=pl.BlockSpec((1, SSM_CHUNK, d_in), lambda bb, c: (bb, c, 0)),
        out_shape=jax.ShapeDtypeStruct((b, s, d_in), BF16),
        scratch_shapes=[pltpu.VMEM((SSM_CHUNK + 8, cch), F32),
                        pltpu.VMEM((SSM_HEADS, SSM_STATE, SSM_HEAD_DIM), F32)],
        compiler_params=_cparams(("parallel", "arbitrary")), name="ssd_mixer",
    )(*args)


def _out_proj_body(x_ref, a_ref, b_ref, wa_ref, wb_ref, o_ref):
    o_ref[...] = x_ref[...] + _dot(a_ref[...], wa_ref[...]) + _dot(b_ref[...], wb_ref[...])


def out_proj_residual(x2, a, bm, wa, wb, tm=512):
    t, d = x2.shape
    return pl.pallas_call(
        _out_proj_body, grid=(t // tm,),
        in_specs=[pl.BlockSpec((tm, d), lambda i: (i, 0)),
                  pl.BlockSpec((tm, a.shape[1]), lambda i: (i, 0)),
                  pl.BlockSpec((tm, bm.shape[1]), lambda i: (i, 0)),
                  pl.BlockSpec(wa.shape, lambda i: (0, 0)),
                  pl.BlockSpec(wb.shape, lambda i: (0, 0))],
        out_specs=pl.BlockSpec((tm, d), lambda i: (i, 0)),
        out_shape=jax.ShapeDtypeStruct((t, d), F32),
        compiler_params=_cparams(("parallel",)), name="out_proj",
    )(x2, a, bm, wa, wb)


def _ffn_body(x_ref, g_ref, wg_ref, wu_ref, wd_ref, o_ref, h_ref):
    f = pl.program_id(1)

    @pl.when(f == 0)
    def _():
        x = x_ref[...]
        h_ref[...] = _rms(x, g_ref[...]).astype(BF16)
        o_ref[...] = x

    half = h_ref.shape[0] // 2
    for r in (slice(0, half), slice(half, 2 * half)):
        h = h_ref[r, :]
        act = (_silu(_dot(h, wg_ref[...])) * _dot(h, wu_ref[...])).astype(BF16)
        o_ref[r, :] += _dot(act, wd_ref[...])


def ffn_residual(x2, gain, w_gate, w_up, w_down, tm=1024, tf=1408):
    t, d = x2.shape
    d_ff = w_gate.shape[1]
    return pl.pallas_call(
        _ffn_body, grid=(t // tm, d_ff // tf),
        in_specs=[pl.BlockSpec((tm, d), lambda i, f: (i, 0)),
                  pl.BlockSpec((1, d), lambda i, f: (0, 0)),
                  pl.BlockSpec((d, tf), lambda i, f: (0, f)),
                  pl.BlockSpec((d, tf), lambda i, f: (0, f)),
                  pl.BlockSpec((tf, d), lambda i, f: (f, 0))],
        out_specs=pl.BlockSpec((tm, d), lambda i, f: (i, 0)),
        out_shape=jax.ShapeDtypeStruct((t, d), F32),
        scratch_shapes=[pltpu.VMEM((tm, d), BF16)],
        compiler_params=_cparams(("parallel", "arbitrary")), name="ffn",
    )(x2, gain.reshape(1, d).astype(F32), w_gate, w_up, w_down)


MOE_ROWS = 256
ROUTE_IDX = 0
ROUTE_W = 2


def _moe_ffn_body(block_expert_ref, n_used_ref, xs_ref, wg_ref, wu_ref, wd_ref, o_ref):
    i = pl.program_id(0)

    @pl.when(i < n_used_ref[0])
    def _():
        x = xs_ref[...]
        act = (_silu(_dot(x, wg_ref[0])) * _dot(x, wu_ref[0])).astype(BF16)
        o_ref[...] = _dot(act, wd_ref[0]).astype(o_ref.dtype)

    @pl.when(i >= n_used_ref[0])
    def _():
        o_ref[...] = jnp.zeros(o_ref.shape, o_ref.dtype)


def moe_expert_ffn(xs, block_expert, n_used, w_gate, w_up, w_down):
    p, d = xs.shape
    d_ff = w_gate.shape[2]
    rows = MOE_ROWS
    grid_spec = pltpu.PrefetchScalarGridSpec(
        num_scalar_prefetch=2, grid=(p // rows,),
        in_specs=[pl.BlockSpec((rows, d), lambda i, be, nu: (i, 0)),
                  pl.BlockSpec((1, d, d_ff), lambda i, be, nu: (be[i], 0, 0)),
                  pl.BlockSpec((1, d, d_ff), lambda i, be, nu: (be[i], 0, 0)),
                  pl.BlockSpec((1, d_ff, d), lambda i, be, nu: (be[i], 0, 0))],
        out_specs=pl.BlockSpec((rows, d), lambda i, be, nu: (i, 0)))
    return pl.pallas_call(
        _moe_ffn_body, grid_spec=grid_spec, out_shape=jax.ShapeDtypeStruct((p, d), BF16),
        compiler_params=pltpu.CompilerParams(
            dimension_semantics=("arbitrary",), vmem_limit_bytes=MOE_VMEM_LIMIT),
        name="moe_expert_ffn",
    )(block_expert, n_used, xs, w_gate, w_up, w_down)


def _moe_combine_body(x_ref, y0_ref, y1_ref, route_ref, o_ref):
    route = route_ref[...]
    lane = lax.broadcasted_iota(jnp.int32, route.shape, 1)
    w0 = jnp.sum(jnp.where(lane == ROUTE_W, route, 0.0), axis=1, keepdims=True)
    w1 = jnp.sum(jnp.where(lane == ROUTE_W + 1, route, 0.0), axis=1, keepdims=True)
    o_ref[...] = x_ref[...] + w0 * y0_ref[...].astype(F32) + w1 * y1_ref[...].astype(F32)


def moe_combine(x2, y0, y1, route, tm=512):
    t, d = x2.shape
    row = lambda i: (i, 0)
    return pl.pallas_call(
        _moe_combine_body, grid=(t // tm,),
        in_specs=[pl.BlockSpec((tm, d), row), pl.BlockSpec((tm, d), row), pl.BlockSpec((tm, d), row),
                  pl.BlockSpec((tm, LANES), row)],
        out_specs=pl.BlockSpec((tm, d), row),
        out_shape=jax.ShapeDtypeStruct((t, d), F32),
        compiler_params=_cparams(("parallel",)), name="moe_combine",
    )(x2, y0, y1, route)


def moe_residual(x2, h, route, w_gate, w_up, w_down):
    t, d = x2.shape
    n_e = w_gate.shape[0]
    rows = MOE_ROWS
    expert = route[:, ROUTE_IDX:ROUTE_IDX + 2].astype(jnp.int32).reshape(-1)
    onehot = (expert[:, None] == jnp.arange(n_e, dtype=jnp.int32)[None, :]).astype(jnp.int32)
    rank = jnp.take_along_axis(jnp.cumsum(onehot, axis=0), expert[:, None], axis=1)[:, 0] - 1
    counts = jnp.sum(onehot, axis=0)
    padded = (counts + rows - 1) // rows * rows
    ends = jnp.cumsum(padded)
    slot = (ends - padded)[expert] + rank
    p_rows = 2 * t + n_e * rows
    token_of_slot = jnp.zeros((p_rows,), jnp.int32).at[slot].set(jnp.arange(2 * t, dtype=jnp.int32) // 2)
    block_start = jnp.arange(p_rows // rows, dtype=jnp.int32) * rows
    block_expert = jnp.minimum(jnp.searchsorted(ends, block_start, side="right"), n_e - 1).astype(jnp.int32)
    n_used = (ends[-1] // rows).astype(jnp.int32).reshape(1)

    xs = jnp.take(h, token_of_slot, axis=0)
    ys = moe_expert_ffn(xs, block_expert, n_used, w_gate, w_up, w_down)
    slot2 = slot.reshape(t, 2)
    return moe_combine(x2, jnp.take(ys, slot2[:, 0], axis=0), jnp.take(ys, slot2[:, 1], axis=0), route)


def _router_body(x_ref, g_ref, r_ref, h_ref, o_ref):
    h = _rms(x_ref[...], g_ref[...])
    h_ref[...] = h.astype(h_ref.dtype)
    logits = jnp.dot(h, r_ref[...], precision=lax.Precision.HIGHEST, preferred_element_type=F32)
    lane = lax.broadcasted_iota(jnp.int32, logits.shape, 1).astype(F32)
    low = jnp.float32(-3.0e38)
    logits = jnp.where(lane < N_EXPERTS, logits, low)
    m1 = jnp.max(logits, axis=1, keepdims=True)
    i1 = jnp.min(jnp.where(logits == m1, lane, float(LANES)), axis=1, keepdims=True)
    rest = jnp.where(lane == i1, low, logits)
    m2 = jnp.max(rest, axis=1, keepdims=True)
    i2 = jnp.min(jnp.where(rest == m2, lane, float(LANES)), axis=1, keepdims=True)
    ex = jnp.exp(m2 - m1)
    w1 = 1.0 / (1.0 + ex)
    w2 = ex / (1.0 + ex)
    o_ref[...] = jnp.where(lane == ROUTE_IDX, i1, jnp.where(lane == ROUTE_IDX + 1, i2, jnp.where(
        lane == ROUTE_W, w1, jnp.where(lane == ROUTE_W + 1, w2, 0.0))))


def router(x2, gain, router_w, tm=512):
    t, d = x2.shape
    r_pad = jnp.pad(router_w.astype(F32), ((0, 0), (0, LANES - router_w.shape[1])))
    return pl.pallas_call(
        _router_body, grid=(t // tm,),
        in_specs=[pl.BlockSpec((tm, d), lambda i: (i, 0)),
                  pl.BlockSpec((1, d), lambda i: (0, 0)),
                  pl.BlockSpec((d, LANES), lambda i: (0, 0))],
        out_specs=[pl.BlockSpec((tm, d), lambda i: (i, 0)), pl.BlockSpec((tm, LANES), lambda i: (i, 0))],
        out_shape=[jax.ShapeDtypeStruct((t, d), BF16), jax.ShapeDtypeStruct((t, LANES), F32)],
        compiler_params=_cparams(("parallel",)), name="router",
    )(x2, gain.reshape(1, d).astype(F32), r_pad)


def _compress_body(ch_ref, nx_ref, pos_ref, w1_ref, w2_ref, gain_ref, cos_ref, sin_ref, rot_ref, o_ref,
                   *, is_key):
    a = _dot((ch_ref[0, 0] + pos_ref[0]).astype(BF16), w1_ref[0])
    a = a + _dot((nx_ref[0, 0] + pos_ref[1]).astype(BF16), w1_ref[1])
    out = _dot(_silu(a).astype(BF16), w2_ref[...])
    if is_key:
        out = _rms(out, gain_ref[...])
        hi, lo = _split_bf16(out, 2)
        partner = _dot(hi, rot_ref[...]) + _dot(lo, rot_ref[...])
        out = out * cos_ref[...] + partner * sin_ref[...]
    o_ref[0, 0] = out


def nsa_compress(t, pos, w1, w2, gain, seq, is_key):
    b, s, _ = t.shape
    g, d = NSA_KV_GROUPS, NSA_HEAD_DIM
    n_ch = s // NSA_CMP_STRIDE
    half = NSA_CMP_STRIDE * d
    ch = t.reshape(b, n_ch, NSA_CMP_STRIDE, g, d).transpose(0, 3, 1, 2, 4).reshape(b, g, n_ch, half)
    nxt = jnp.concatenate([ch[:, :, 1:], jnp.zeros((b, g, 1, half), F32)], axis=2)
    pos2 = pos.astype(F32).reshape(2, 1, half)
    w1s = w1.astype(BF16).reshape(2, half, d)
    cmp_end = jnp.arange(n_ch) * NSA_CMP_STRIDE + NSA_CMP_BLOCK - 1
    inv_freq = 1.0 / (ROPE_THETA ** (jnp.arange(0, d, 2, dtype=F32) / d))
    ang = cmp_end.astype(F32)[:, None] * inv_freq[None, :]
    cos = jnp.concatenate([jnp.cos(ang)] * 2, axis=1)
    sin = jnp.concatenate([jnp.sin(ang)] * 2, axis=1)
    rot = np.zeros((d, d), np.float32)
    rot[np.arange(d // 2) + d // 2, np.arange(d // 2)] = -1.0
    rot[np.arange(d // 2), np.arange(d // 2) + d // 2] = 1.0
    blk = lambda bb, gg: (bb, gg, 0, 0)
    c2 = lambda bb, gg: (0, 0)
    c3 = lambda bb, gg: (0, 0, 0)
    return pl.pallas_call(
        functools.partial(_compress_body, is_key=is_key), grid=(b, g),
        in_specs=[pl.BlockSpec((1, 1, n_ch, half), blk), pl.BlockSpec((1, 1, n_ch, half), blk),
                  pl.BlockSpec((2, 1, half), c3), pl.BlockSpec((2, half, d), c3),
                  pl.BlockSpec((d, d), c2), pl.BlockSpec((1, d), c2),
                  pl.BlockSpec((n_ch, d), c2), pl.BlockSpec((n_ch, d), c2), pl.BlockSpec((d, d), c2)],
        out_specs=pl.BlockSpec((1, 1, n_ch, d), blk),
        out_shape=jax.ShapeDtypeStruct((b, g, n_ch, d), F32),
        compiler_params=_cparams(("parallel", "parallel")), name="nsa_compress",
    )(ch, nxt, pos2, w1s, w2.astype(BF16), gain.reshape(1, d).astype(F32), cos, sin,
      jnp.asarray(rot, dtype=BF16))


def _nsa_body(q_ref, ck_ref, cvt_ref, ksl_ref, vslt_ref, kwn_ref, vwnt_ref, ovt_ref, glt_ref, o_ref,
              sc_ref, phi_ref, plo_ref, imp_ref, bias_ref, ss_ref, ps_ref, ss1_ref, ps1_ref, sw_ref, pw_ref,
              ow_ref, *, tq):
    g = pl.program_id(1)
    i = pl.program_id(2)
    d = NSA_HEAD_DIM
    rep = NSA_HEADS // NSA_KV_GROUPS
    t0 = i * tq
    n_cmp = ck_ref.shape[1]
    n_sel = ovt_ref.shape[0]
    width = rep * tq
    sub = NSA_SEL_BLOCK
    dead = 0.5 * NEG_INF
    v_rows = pl.ds(pl.multiple_of(g * d, d), d)

    qb = q_ref[0]
    q4 = jnp.concatenate([qb[:, r * d:(r + 1) * d] for r in range(rep)], axis=0)
    q4 = jnp.concatenate([q4, q4], axis=1)
    lane = lax.broadcasted_iota(jnp.int32, q4.shape, 1)
    q4 = jnp.where(jnp.right_shift(lane, d.bit_length() - 1) == g, q4, jnp.zeros_like(q4))

    def qpos_of(shape):
        return t0 + (lax.broadcasted_iota(jnp.int32, shape, 1) & (tq - 1))

    s = _dot_nt(ck_ref[0], q4)
    cmp_end = lax.broadcasted_iota(jnp.int32, s.shape, 0) * NSA_CMP_STRIDE + (NSA_CMP_BLOCK - 1)
    s = jnp.where(cmp_end <= qpos_of(s.shape), s, NEG_INF)
    sc_ref[...] = s
    m_c = jnp.max(s, axis=0, keepdims=True)
    lpart = jnp.zeros((sub, width), F32)
    for r in range(n_cmp // sub):
        e = jnp.exp2(sc_ref[r * sub:(r + 1) * sub, :] - m_c)
        lpart = lpart + e
        hi = e.astype(BF16)
        phi_ref[r * sub:(r + 1) * sub, :] = hi
        plo_ref[r * sub:(r + 1) * sub, :] = (e - hi.astype(F32)).astype(BF16)
    inv_c = jnp.where(m_c > dead, 1.0 / jnp.sum(lpart, axis=0, keepdims=True), 0.0)
    o_c = _dot(cvt_ref[0, v_rows, :], phi_ref[...]) * inv_c

    imp4 = (_dot(ovt_ref[...], phi_ref[...]) + _dot(ovt_ref[...], plo_ref[...])) * inv_c
    imp = imp4[:, 0:tq]
    for r in range(1, rep):
        imp = imp + imp4[:, r * tq:(r + 1) * tq]
    blk = lax.broadcasted_iota(jnp.int32, imp.shape, 0)
    qp = t0 + lax.broadcasted_iota(jnp.int32, imp.shape, 1)
    cur = jnp.right_shift(qp, NSA_SEL_BLOCK.bit_length() - 1)
    forced = (blk == 0) | (blk == cur) | (blk == cur - 1)
    future = blk * NSA_SEL_BLOCK > qp
    imp_ref[...] = jnp.where(future, -FORCE_SCORE, jnp.where(forced, FORCE_SCORE, imp))
    bias_ref[...] = jnp.full(bias_ref.shape, NEG_INF, F32)

    n_live = jnp.minimum((t0 + tq - 1) // NSA_SEL_BLOCK + 1, n_sel)
    n_var = max(n_sel // 32, 1)
    rows_per = n_sel // n_var
    top_n = float(min(NSA_TOP_N, n_sel))
    for v in range(n_var):
        rows = rows_per * (v + 1)

        @pl.when((n_live > rows_per * v) & (n_live <= rows))
        def _():
            mine = imp_ref[0:rows, :]
            blk_r = lax.broadcasted_iota(jnp.int32, mine.shape, 0)

            def count(i2, cnt):
                other = imp_ref[pl.ds(i2, 1), :]
                beats = (other > mine) | ((other == mine) & (blk_r > i2))
                return cnt + jnp.where(beats, 1.0, 0.0)

            rank = lax.fori_loop(0, n_live, count, jnp.zeros(mine.shape, F32))
            bias = jnp.where(rank < top_n, 0.0, NEG_INF)
            bias_ref[0:rows, :] = jnp.concatenate([bias] * rep, axis=1)

    init = (jnp.full((1, width), NEG_INF, F32), jnp.zeros((1, width), F32), jnp.zeros((d, width), F32))

    chunk = 8 * sub
    n_sub = chunk // sub

    s_slots = (ss_ref, ss1_ref)
    p_slots = (ps_ref, ps1_ref)

    def sel_scores(c, slot, diagonal=False):
        start = pl.multiple_of(c * chunk, chunk)
        s = _dot_nt(ksl_ref[0, pl.ds(start, chunk), :], q4)
        if diagonal:
            kpos = start + lax.broadcasted_iota(jnp.int32, s.shape, 0)
            s = jnp.where(kpos <= qpos_of(s.shape), s, NEG_INF)
        s_slots[slot][...] = s

    def sel_update(c, slot, carry):
        m_prev, l_prev, acc = carry
        s_ref, p_ref = s_slots[slot], p_slots[slot]
        biases = [bias_ref[pl.ds(c * n_sub + r, 1), :] for r in range(n_sub)]
        m_new = m_prev
        for r in range(n_sub):
            m_new = jnp.maximum(
                m_new, jnp.max(s_ref[r * sub:(r + 1) * sub, :], axis=0, keepdims=True) + biases[r])
        alpha = jnp.exp2(m_prev - m_new)
        live = m_new > dead
        lpart = jnp.zeros((sub, width), F32)
        for r in range(n_sub):
            shift = jnp.where(live, biases[r] - m_new, NEG_INF)
            p = jnp.exp2(s_ref[r * sub:(r + 1) * sub, :] + shift)
            lpart = lpart + p
            p_ref[r * sub:(r + 1) * sub, :] = p.astype(BF16)
        l_new = alpha * l_prev + jnp.sum(lpart, axis=0, keepdims=True)
        vt = vslt_ref[0, v_rows, pl.ds(pl.multiple_of(c * chunk, chunk), chunk)]
        return m_new, l_new, alpha * acc + _dot(vt, p_ref[...])

    c_last = (t0 + tq - 1) // chunk
    last_past = jnp.maximum(c_last - 1, 0)

    @pl.when(c_last > 0)
    def _():
        sel_scores(0, 0)

    def pair(k, carry):
        sel_scores(jnp.minimum(2 * k + 1, last_past), 1)
        carry = sel_update(2 * k, 0, carry)
        sel_scores(jnp.minimum(2 * k + 2, last_past), 0)
        return sel_update(2 * k + 1, 1, carry)

    carry = lax.fori_loop(0, c_last // 2, pair, init)
    carry = lax.cond(c_last % 2 == 1, lambda cr: sel_update(c_last - 1, 0, cr), lambda cr: cr, carry)
    sel_scores(c_last, 0, diagonal=True)
    _, l_s, acc_s = sel_update(c_last, 0, carry)
    o_s = acc_s / l_s

    def win_chunk(c, carry):
        m_prev, l_prev, acc = carry
        start = pl.multiple_of(c * tq, tq)
        s = _dot_nt(kwn_ref[0, pl.ds(start, tq), :], q4)
        kpos = start + lax.broadcasted_iota(jnp.int32, s.shape, 0)
        qpos = qpos_of(s.shape)
        s = jnp.where((kpos <= qpos) & (kpos > qpos - NSA_WINDOW), s, NEG_INF)
        m_new = jnp.maximum(m_prev, jnp.max(s, axis=0, keepdims=True))
        alpha = jnp.exp2(m_prev - m_new)
        p = jnp.exp2(s + jnp.where(m_new > dead, -m_new, NEG_INF))
        l_new = alpha * l_prev + jnp.sum(p, axis=0, keepdims=True)
        vt = vwnt_ref[0, v_rows, pl.ds(start, tq)]
        return m_new, l_new, alpha * acc + _dot(vt, p.astype(BF16))

    n_back = NSA_WINDOW // tq

    @pl.when(i < n_back)
    def _():
        _, l_w, acc_w = lax.fori_loop(0, i + 1, win_chunk, init)
        ow_ref[...] = acc_w / l_w

    @pl.when(i >= n_back)
    def _():
        start = pl.multiple_of(t0 - NSA_WINDOW, tq)
        s = _dot_nt(kwn_ref[0, pl.ds(start, NSA_WINDOW + tq), :], q4)
        kpos = start + lax.broadcasted_iota(jnp.int32, (tq, width), 0)
        qpos = qpos_of((tq, width))
        sw_ref[0:tq, :] = jnp.where(kpos > qpos - NSA_WINDOW, s[0:tq, :], NEG_INF)
        sw_ref[tq:NSA_WINDOW, :] = s[tq:NSA_WINDOW, :]
        sw_ref[NSA_WINDOW:, :] = jnp.where(kpos + NSA_WINDOW <= qpos, s[NSA_WINDOW:, :], NEG_INF)
        m_w = jnp.max(sw_ref[...], axis=0, keepdims=True)
        lpart = jnp.zeros((sub, width), F32)
        for r in range((NSA_WINDOW + tq) // sub):
            p = jnp.exp2(sw_ref[r * sub:(r + 1) * sub, :] - m_w)
            lpart = lpart + p
            pw_ref[r * sub:(r + 1) * sub, :] = p.astype(BF16)
        vt = vwnt_ref[0, v_rows, pl.ds(start, NSA_WINDOW + tq)]
        ow_ref[...] = _dot(vt, pw_ref[...]) / jnp.sum(lpart, axis=0, keepdims=True)

    o_w = ow_ref[...]

    def gate(branch):
        rows = [glt_ref[0, pl.ds((g * rep + r) * 3 + branch, 1), :] for r in range(rep)]
        return _sigmoid(jnp.concatenate(rows, axis=1))

    o_ref[0, 0, 0] = gate(0) * o_c + gate(1) * o_s + gate(2) * o_w


def nsa_overlap_t(n_cmp, n_sel):
    c_start = np.arange(n_cmp)[None, :] * NSA_CMP_STRIDE
    s_start = np.arange(n_sel)[:, None] * NSA_SEL_BLOCK
    hit = (c_start < s_start + NSA_SEL_BLOCK) & (c_start + NSA_CMP_BLOCK > s_start)
    hit = hit & (np.arange(n_cmp)[None, :] < n_cmp - NSA_CMP_BLOCK // NSA_CMP_STRIDE + 1)
    return jnp.asarray(hit.astype(np.float32), dtype=BF16)


def nsa_attention(qn, ck, cvt, ksl, vslt, kwn, vwnt, glt, tq=128):
    b, s, _ = qn.shape
    g, d = NSA_KV_GROUPS, NSA_HEAD_DIM
    rep = NSA_HEADS // g
    n_cmp = ck.shape[1]
    n_sel = s // NSA_SEL_BLOCK
    nq = s // tq
    ovt = nsa_overlap_t(n_cmp, n_sel)
    width = rep * tq
    chunk = 8 * NSA_SEL_BLOCK
    full3 = lambda bb, gg, i: (bb, 0, 0)
    return pl.pallas_call(
        functools.partial(_nsa_body, tq=tq), grid=(b, g, nq),
        in_specs=[pl.BlockSpec((1, tq, rep * d), lambda bb, gg, i: (bb, i, gg)),
                  pl.BlockSpec((1, n_cmp, g * d), full3), pl.BlockSpec((1, g * d, n_cmp), full3),
                  pl.BlockSpec((1, s, g * d), full3), pl.BlockSpec((1, g * d, s), full3),
                  pl.BlockSpec((1, s, g * d), full3), pl.BlockSpec((1, g * d, s), full3),
                  pl.BlockSpec((n_sel, n_cmp), lambda bb, gg, i: (0, 0)),
                  pl.BlockSpec((1, glt.shape[1], tq), lambda bb, gg, i: (bb, 0, i))],
        out_specs=pl.BlockSpec((1, 1, 1, d, rep * tq), lambda bb, gg, i: (bb, gg, i, 0, 0)),
        out_shape=jax.ShapeDtypeStruct((b, g, nq, d, rep * tq), F32),
        scratch_shapes=[pltpu.VMEM((n_cmp, width), F32), pltpu.VMEM((n_cmp, width), BF16),
                        pltpu.VMEM((n_cmp, width), BF16), pltpu.VMEM((n_sel, tq), F32),
                        pltpu.VMEM((n_sel, width), F32), pltpu.VMEM((chunk, width), F32),
                        pltpu.VMEM((chunk, width), BF16), pltpu.VMEM((chunk, width), F32),
                        pltpu.VMEM((chunk, width), BF16), pltpu.VMEM((NSA_WINDOW + tq, width), F32),
                        pltpu.VMEM((NSA_WINDOW + tq, width), BF16), pltpu.VMEM((d, width), F32)],
        compiler_params=_cparams(("parallel", "parallel", "arbitrary")), name="nsa_attention",
    )(qn, ck, cvt, ksl, vslt, kwn, vwnt, ovt, glt)


def _pad_cols(w, n):
    return jnp.pad(w, ((0, 0), (0, n - w.shape[1])))


def _even_layer(x2, b, s, layer_idx, norm_mix, w_in, q_gain, k_gain, lam, subln_gain, conv_w, conv_b,
                dt_bias, a_log, d_skip, ssm_norm_gain, w_out, norm_ffn, w_gate, w_up, w_down):
    nq = DA_HEADS * 2 * DA_HEAD_DIM
    nv = DA_HEADS * DA_V_DIM
    cch = SSM_D_INNER + 2 * SSM_GROUPS * SSM_STATE
    offs = np.cumsum([0, nq, nq, nv, SSM_D_INNER, cch, SSM_HEADS])
    wb = w_in.astype(BF16)
    pieces = [wb[:, offs[k]:offs[k + 1]] for k in range(6)]
    pieces[5] = _pad_cols(pieces[5], LANES)
    posts = [HeadNorm(q_gain, DA_HEAD_DIM, rope=True, mul=DA_HEAD_DIM ** -0.5 * LOG2E),
             HeadNorm(k_gain, DA_HEAD_DIM, rope=True), None, None, None, None]
    q, k, v, z, xbc, dt = norm_proj(x2, norm_mix, pieces, [BF16, BF16, BF16, F32, F32, F32], posts, s,
                                    _rope_tables(s, DA_HEAD_DIM))
    qn = q.reshape(b, s, nq)
    kn = k.reshape(b, s, nq)
    vt = v.reshape(b, s, nv).transpose(0, 2, 1)
    lam_init = 0.8 - 0.6 * math.exp(-0.3 * layer_idx)
    lf = lam.astype(F32)
    lam_full = jnp.exp(jnp.sum(lf[0] * lf[1])) - jnp.exp(jnp.sum(lf[2] * lf[3])) + lam_init
    a_out = flash_attention(lam_full.reshape(1), [qn], [kn], vt, subln_gain, DA_HEADS, DA_V_DIM,
                            diff=True, out_scale=1.0 - lam_init)
    b_out = ssd_mixer(xbc.reshape(b, s, cch), z.reshape(b, s, SSM_D_INNER), dt.reshape(b, s, LANES),
                      conv_w, conv_b, dt_bias, a_log, d_skip, ssm_norm_gain)
    wo = w_out.astype(BF16)
    x2 = out_proj_residual(x2, a_out.reshape(-1, nv), b_out.reshape(-1, SSM_D_INNER), wo[:nv], wo[nv:])
    return ffn_residual(x2, norm_ffn, w_gate.astype(BF16), w_up.astype(BF16), w_down.astype(BF16))


def _odd_layer(x2, b, s, norm_mix, w_in, q_gain, k_gain, cmp_pos, cmp_w1, cmp_w2, cq_gain, ckv_gain,
               w_uq, w_ukv, qn_gain, qr_gain, kn_gain, kr_gain, w_out, norm_ffn, router_w, w_gate, w_up,
               w_down):
    g, d = NSA_KV_GROUPS, NSA_HEAD_DIM
    nq = NSA_HEADS * d
    nkv = g * d
    sizes = [nq] + [nkv] * 6 + [NSA_HEADS * 3, w_uq.shape[0], w_ukv.shape[0], MLA_ROPE_DIM]
    offs = np.cumsum([0] + sizes)
    wb = w_in.astype(BF16)
    pieces = [wb[:, offs[k]:offs[k + 1]] for k in range(len(sizes))]
    pieces[7] = _pad_cols(pieces[7], LANES)
    pieces[10] = _pad_cols(pieces[10], LANES)
    tables = _rope_tables(s, d)
    posts = [None] * len(sizes)
    posts[0] = HeadNorm(q_gain, d, rope=True, mul=d ** -0.5 * LOG2E)
    posts[3] = HeadNorm(k_gain[1], d, rope=True)
    posts[5] = HeadNorm(k_gain[2], d, rope=True)
    posts[10] = HeadNorm(kr_gain, MLA_ROPE_DIM, rope=True)
    (q, kc, vc, ksl, vsl, kwn, vwn, gl, cq, ckv, k_rope) = norm_proj(
        x2, norm_mix, pieces, [BF16, F32, F32, BF16, BF16, BF16, BF16, F32, F32, F32, BF16], posts, s, tables)

    qn = q.reshape(b, s, nq)
    ksl_n = ksl.reshape(b, s, nkv)
    kwn_n = kwn.reshape(b, s, nkv)
    ck = nsa_compress(kc.reshape(b, s, nkv), cmp_pos[0], cmp_w1[0], cmp_w2[0], k_gain[0], s, True)
    cv = nsa_compress(vc.reshape(b, s, nkv), cmp_pos[1], cmp_w1[1], cmp_w2[1], k_gain[0], s, False)
    n_cmp = ck.shape[2]
    ck = ck.transpose(0, 2, 1, 3).reshape(b, n_cmp, nkv).astype(BF16)
    cvt = cv.transpose(0, 1, 3, 2).reshape(b, nkv, n_cmp).astype(BF16)
    vslt = vsl.reshape(b, s, nkv).transpose(0, 2, 1)
    vwnt = vwn.reshape(b, s, nkv).transpose(0, 2, 1)
    glt = gl.reshape(b, s, LANES)[:, :, :32].transpose(0, 2, 1)
    tq = 128
    o = nsa_attention(qn, ck, cvt, ksl_n, vslt, kwn_n, vwnt, glt, tq=tq)
    rep = NSA_HEADS // g
    c_out = o.reshape(b, g, s // tq, d, rep, tq).transpose(0, 2, 5, 1, 4, 3).reshape(b * s, nq)
    c_out = c_out.astype(BF16)

    h = MLA_HEADS
    dqk = MLA_NOPE_DIM + MLA_ROPE_DIM
    wq = w_uq.astype(BF16).reshape(-1, h, dqk)
    wq_nope = wq[:, :, :MLA_NOPE_DIM].reshape(-1, h * MLA_NOPE_DIM)
    wq_rope = jnp.pad(wq[:, :, MLA_NOPE_DIM:], ((0, 0), (0, 0), (0, LANES - MLA_ROPE_DIM)))
    wq_rope = wq_rope.reshape(-1, h * LANES)
    wkv = w_ukv.astype(BF16).reshape(-1, h, MLA_NOPE_DIM + MLA_V_DIM)
    wk_nope = wkv[:, :, :MLA_NOPE_DIM].reshape(-1, h * MLA_NOPE_DIM)
    wv = wkv[:, :, MLA_NOPE_DIM:].reshape(-1, h * MLA_V_DIM)
    q_mul = dqk ** -0.5 * LOG2E
    q_nope, q_rope = norm_proj(
        cq, cq_gain, [wq_nope, wq_rope], [BF16, BF16],
        [HeadNorm(qn_gain, MLA_NOPE_DIM, mul=q_mul), HeadNorm(qr_gain, MLA_ROPE_DIM, rope=True, mul=q_mul)],
        s, tables)
    k_nope, v = norm_proj(ckv, ckv_gain, [wk_nope, wv], [BF16, BF16], [HeadNorm(kn_gain, MLA_NOPE_DIM), None])
    shp = lambda t: t.reshape(b, s, t.shape[-1])
    d_out = flash_attention(jnp.zeros((1,), F32), [shp(q_nope), shp(q_rope)], [shp(k_nope), shp(k_rope)],
                            shp(v).transpose(0, 2, 1), jnp.ones((MLA_V_DIM,), F32), h, MLA_V_DIM,
                            diff=False)

    wo = w_out.astype(BF16)
    x2 = out_proj_residual(x2, c_out, d_out.reshape(b * s, h * MLA_V_DIM), wo[:nq], wo[nq:])
    h, route = router(x2, norm_ffn, router_w)
    return moe_residual(x2, h, route, w_gate.astype(BF16), w_up.astype(BF16), w_down.astype(BF16))


def kernel(x, ev_norm_mix, ev_w_in, da_q_gain, da_k_gain, da_lambda, da_subln_gain, ssm_conv_w, ssm_conv_b, ssm_dt_bias, ssm_a_log, ssm_d, ssm_norm_gain, ev_w_out, ev_norm_ffn, ffn_w_gate, ffn_w_up, ffn_w_down, od_norm_mix, od_w_in, nsa_q_gain, nsa_k_gain, nsa_cmp_pos, nsa_cmp_w1, nsa_cmp_w2, mla_cq_gain, mla_ckv_gain, mla_w_uq, mla_w_ukv, mla_qn_gain, mla_qr_gain, mla_kn_gain, mla_kr_gain, od_w_out, od_norm_ffn, moe_router, moe_w_gate, moe_w_up, moe_w_down):
    b, s, d = x.shape
    x2 = x.reshape(b * s, d)
    depth = ev_norm_mix.shape[0] + od_norm_mix.shape[0]
    for layer in range(depth):
        i = layer // 2
        if layer % 2 == 0:
            x2 = _even_layer(x2, b, s, layer, ev_norm_mix[i], ev_w_in[i], da_q_gain[i], da_k_gain[i],
                             da_lambda[i], da_subln_gain[i], ssm_conv_w[i], ssm_conv_b[i],
                             ssm_dt_bias[i], ssm_a_log[i], ssm_d[i], ssm_norm_gain[i], ev_w_out[i],
                             ev_norm_ffn[i], ffn_w_gate[i], ffn_w_up[i], ffn_w_down[i])
        else:
            x2 = _odd_layer(x2, b, s, od_norm_mix[i], od_w_in[i], nsa_q_gain[i], nsa_k_gain[i],
                            nsa_cmp_pos[i], nsa_cmp_w1[i], nsa_cmp_w2[i], mla_cq_gain[i],
                            mla_ckv_gain[i], mla_w_uq[i], mla_w_ukv[i], mla_qn_gain[i], mla_qr_gain[i],
                            mla_kn_gain[i], mla_kr_gain[i], od_w_out[i], od_norm_ffn[i], moe_router[i],
                            moe_w_gate[i], moe_w_up[i], moe_w_down[i])
    return x2.reshape(b, s, d)
```

```python
import functools
import math

import numpy as np
import jax
import jax.numpy as jnp
from jax import lax
from jax.experimental import pallas as pl
from jax.experimental.pallas import tpu as pltpu

F32 = jnp.float32
BF16 = jnp.bfloat16

ROPE_THETA = 10000.0
NORM_EPS = 1e-6
NEG_INF = -1e30
FORCE_SCORE = 1e6
LOG2E = 1.4426950408889634

DA_HEADS = 4
DA_HEAD_DIM = 64
DA_V_DIM = 2 * DA_HEAD_DIM
SSM_HEADS = 8
SSM_HEAD_DIM = 64
SSM_D_INNER = SSM_HEADS * SSM_HEAD_DIM
SSM_GROUPS = 2
SSM_STATE = 128
SSM_CONV = 4
SSM_CHUNK = 256
NSA_HEADS = 8
NSA_KV_GROUPS = 2
NSA_HEAD_DIM = 64
NSA_CMP_BLOCK = 32
NSA_CMP_STRIDE = 16
NSA_SEL_BLOCK = 64
NSA_TOP_N = 16
NSA_WINDOW = 512
MLA_HEADS = 4
MLA_NOPE_DIM = 128
MLA_ROPE_DIM = 64
MLA_V_DIM = 128
N_EXPERTS = 8

LANES = 128
VMEM_LIMIT = 48 * 1024 * 1024
MOE_VMEM_LIMIT = 58 * 1024 * 1024

NT_DIMS = (((1,), (1,)), ((), ()))


def _cparams(semantics):
    return pltpu.CompilerParams(dimension_semantics=semantics, vmem_limit_bytes=VMEM_LIMIT)


def _dot(a, b):
    return jnp.dot(a, b, preferred_element_type=F32)


def _dot_nt(a, b):
    return lax.dot_general(a, b, NT_DIMS, preferred_element_type=F32)


def _split_bf16(x, parts):
    out = []
    for _ in range(parts):
        hi = x.astype(BF16)
        out.append(hi)
        x = x - hi.astype(F32)
    return out


def _sigmoid(x):
    return 1.0 / (1.0 + jnp.exp(-x))


def _silu(x):
    return x * _sigmoid(x)


def _softplus(x):
    return jnp.maximum(x, 0.0) + jnp.log(1.0 + jnp.exp(-jnp.abs(x)))


def _rms(x, gain):
    ms = jnp.mean(x * x, axis=-1, keepdims=True)
    return x * lax.rsqrt(ms + NORM_EPS) * gain


class HeadNorm:
    def __init__(self, gain, hd, rope=False, mul=1.0):
        self.gain, self.hd, self.rope, self.mul = gain, hd, rope, mul


def _head_norm(y, gain, bd, cos_ref, sin_ref, post):
    n = y.shape[1]
    hd = post.hd
    hi, lo = _split_bf16(y * y, 2)
    ss = _dot(hi, bd) + _dot(lo, bd)
    yn = y * lax.rsqrt(ss * (1.0 / hd) + NORM_EPS) * gain
    if post.rope:
        reps = n // LANES
        cos = jnp.concatenate([cos_ref[...]] * reps, axis=1) if reps > 1 else cos_ref[...]
        sin = jnp.concatenate([sin_ref[...]] * reps, axis=1) if reps > 1 else sin_ref[...]
        lane = lax.broadcasted_iota(jnp.int32, yn.shape, 1)
        first_half = (lane & (hd - 1)) < (hd // 2)
        partner = jnp.where(first_half, pltpu.roll(yn, n - hd // 2, 1), pltpu.roll(yn, hd // 2, 1))
        yn = yn * cos + partner * sin
    if post.mul != 1.0:
        yn = yn * post.mul
    return yn


def _norm_proj_body(x_ref, g_ref, *refs, posts, use_rope, transposed):
    if use_rope:
        cos_ref, sin_ref = refs[0], refs[1]
        refs = refs[2:]
    else:
        cos_ref = sin_ref = None
    n_out = len(posts)
    n_aux = 2 * sum(p is not None for p in posts)
    w_refs, aux, o_refs = refs[:n_out], refs[n_out:n_out + n_aux], refs[n_out + n_aux:]
    h = _rms(x_ref[...], g_ref[...]).astype(BF16)
    a = 0
    for k, (w_ref, o_ref, post) in enumerate(zip(w_refs, o_refs, posts)):
        if k in transposed:
            o_ref[...] = _dot_nt(w_ref[...], h).astype(o_ref.dtype)
            continue
        y = _dot(h, w_ref[...])
        if post is not None:
            y = _head_norm(y, aux[a][...], aux[a + 1][...], cos_ref, sin_ref, post)
            a += 2
        o_ref[...] = y.astype(o_ref.dtype)


def norm_proj(x2, gain, weights, out_dtypes, posts=None, seq=None, rope_tables=None, transposed=(), tm=512):
    t, d = x2.shape
    posts = posts or [None] * len(weights)
    transposed = frozenset(transposed)
    use_rope = any(p is not None and p.rope for p in posts)
    const = lambda i: (0, 0)
    args = [x2, gain.reshape(1, d).astype(F32)]
    in_specs = [pl.BlockSpec((tm, d), lambda i: (i, 0)), pl.BlockSpec((1, d), const)]
    if use_rope:
        per_seq = seq // tm
        args += list(rope_tables)
        in_specs += [pl.BlockSpec((tm, LANES), lambda i: (i % per_seq, 0))] * 2
    args += list(weights)
    in_specs += [pl.BlockSpec(w.shape, const) for w in weights]
    for w, p in zip(weights, posts):
        if p is not None:
            n = w.shape[1]
            args += [jnp.tile(p.gain.astype(F32), n // p.hd).reshape(1, n), _block_diag_ones(n, p.hd)]
            in_specs += [pl.BlockSpec((1, n), const), pl.BlockSpec((n, n), const)]
    out_specs, out_shape = [], []
    for k, (w, dt) in enumerate(zip(weights, out_dtypes)):
        if k in transposed:
            out_specs.append(pl.BlockSpec((w.shape[0], tm), lambda i: (0, i)))
            out_shape.append(jax.ShapeDtypeStruct((w.shape[0], t), dt))
        else:
            out_specs.append(pl.BlockSpec((tm, w.shape[1]), lambda i: (i, 0)))
            out_shape.append(jax.ShapeDtypeStruct((t, w.shape[1]), dt))
    return pl.pallas_call(
        functools.partial(_norm_proj_body, posts=tuple(posts), use_rope=use_rope, transposed=transposed),
        grid=(t // tm,), in_specs=in_specs, out_specs=out_specs, out_shape=out_shape,
        compiler_params=_cparams(("parallel",)), name="norm_proj",
    )(*args)


def _block_diag_ones(n, hd):
    idx = np.arange(n) // hd
    return jnp.asarray((idx[:, None] == idx[None, :]).astype(np.float32), dtype=BF16)


def _rope_tables(seq, hd):
    inv_freq = 1.0 / (ROPE_THETA ** (jnp.arange(0, hd, 2, dtype=F32) / hd))
    ang = jnp.arange(seq, dtype=F32)[:, None] * inv_freq[None, :]
    cos, sin = jnp.cos(ang), jnp.sin(ang)
    reps = LANES // hd
    cos_t = jnp.tile(jnp.concatenate([cos, cos], axis=1), (1, reps))
    sin_t = jnp.tile(jnp.concatenate([-sin, sin], axis=1), (1, reps))
    return cos_t, sin_t


def _flash_body(lam_ref, *refs, n_qk, diff, out_scale, sub):
    q_refs = refs[:n_qk]
    k_refs = refs[n_qk:2 * n_qk]
    vt_ref, gain_ref, o_ref, m_ref, l_ref, acc_ref, s0_ref, s1_ref, p0_ref, p1_ref = refs[2 * n_qk:]
    i = pl.program_id(2)
    n_sm = 2 if diff else 1
    _, tk, tq = s0_ref.shape
    s_slots = (s0_ref, s1_ref)
    p_slots = (p0_ref, p1_ref)

    m_ref[...] = jnp.full(m_ref.shape, NEG_INF, F32)
    l_ref[...] = jnp.zeros(l_ref.shape, F32)
    acc_ref[...] = jnp.zeros(acc_ref.shape, F32)

    qs = [r[0] for r in q_refs]
    q = qs[0] if n_qk == 1 else jnp.concatenate(qs, axis=1)
    if diff:
        lane = lax.broadcasted_iota(jnp.int32, q.shape, 1)
        half = q.shape[1] // 2
        zero = jnp.zeros_like(q)
        q_parts = [jnp.where(lane < half, q, zero), jnp.where(lane >= half, q, zero)]
    else:
        q_parts = [q]

    def scores(c, slot, diagonal=False):
        rows = pl.ds(pl.multiple_of(c * tk, tk), tk)
        ks = [r[0, rows, :] for r in k_refs]
        k = ks[0] if n_qk == 1 else jnp.concatenate(ks, axis=1)
        for sm in range(n_sm):
            s = _dot_nt(k, q_parts[sm])
            if diagonal:
                row = lax.broadcasted_iota(jnp.int32, s.shape, 0)
                col = lax.broadcasted_iota(jnp.int32, s.shape, 1)
                s = jnp.where(row <= col, s, NEG_INF)
            s_slots[slot][sm] = s

    def update(c, slot):
        vt = vt_ref[:, pl.ds(pl.multiple_of(c * tk, tk), tk)]
        for sm in range(n_sm):
            s_ref, p_ref = s_slots[slot], p_slots[slot]
            m_prev = m_ref[sm]
            m_new = jnp.maximum(m_prev, jnp.max(s_ref[sm], axis=0, keepdims=True))
            m_ref[sm] = m_new
            alpha = jnp.exp2(m_prev - m_new)
            lpart = jnp.zeros((sub, tq), F32)
            for r in range(tk // sub):
                p = jnp.exp2(s_ref[sm, r * sub:(r + 1) * sub, :] - m_new)
                lpart = lpart + p
                p_ref[sm, r * sub:(r + 1) * sub, :] = p.astype(BF16)
            l_ref[sm] = alpha * l_ref[sm] + jnp.sum(lpart, axis=0, keepdims=True)
            acc_ref[sm] = alpha * acc_ref[sm] + _dot(vt, p_ref[sm])

    last_past = jnp.maximum(i - 1, 0)

    @pl.when(i > 0)
    def _():
        scores(0, 0)

    def pair(k2, carry):
        scores(jnp.minimum(2 * k2 + 1, last_past), 1)
        update(2 * k2, 0)
        scores(jnp.minimum(2 * k2 + 2, last_past), 0)
        update(2 * k2 + 1, 1)
        return carry

    lax.fori_loop(0, i // 2, pair, 0)

    @pl.when(i % 2 == 1)
    def _():
        update(i - 1, 0)

    scores(i, 0, diagonal=True)
    update(i, 0)
    o = acc_ref[0] / l_ref[0]
    if diff:
        o = o - lam_ref[0] * (acc_ref[1] / l_ref[1])
        ms = jnp.mean(o * o, axis=0, keepdims=True)
        o = o * lax.rsqrt(ms + NORM_EPS) * gain_ref[...] * out_scale
    o_ref[0] = o.T.astype(o_ref.dtype)


def flash_attention(lam, qs, ks, vt, gain, n_heads, dv, *, diff, out_scale=1.0, tile=512, sub=64):
    b, s, _ = qs[0].shape
    nt = s // tile
    n_qk = len(qs)
    in_specs = [pl.BlockSpec(memory_space=pltpu.SMEM)]
    for q in qs:
        w = q.shape[2] // n_heads
        in_specs.append(pl.BlockSpec((1, tile, w), lambda bb, h, i: (bb, i, h)))
    for q, k in zip(qs, ks):
        w = q.shape[2] // n_heads
        if k.shape[2] == w:
            in_specs.append(pl.BlockSpec((1, s, w), lambda bb, h, i: (bb, 0, 0)))
        else:
            in_specs.append(pl.BlockSpec((1, s, w), lambda bb, h, i: (bb, 0, h)))
    in_specs.append(pl.BlockSpec((dv, s), lambda bb, h, i: (h, bb)))
    in_specs.append(pl.BlockSpec((dv, 1), lambda bb, h, i: (0, 0)))
    n_sm = 2 if diff else 1
    return pl.pallas_call(
        functools.partial(_flash_body, n_qk=n_qk, diff=diff, out_scale=out_scale, sub=sub),
        grid=(b, n_heads, nt), in_specs=in_specs,
        out_specs=pl.BlockSpec((1, tile, dv), lambda bb, h, i: (bb, i, h)),
        out_shape=jax.ShapeDtypeStruct((b, s, n_heads * dv), BF16),
        scratch_shapes=[pltpu.VMEM((n_sm, 1, tile), F32), pltpu.VMEM((n_sm, 1, tile), F32),
                        pltpu.VMEM((n_sm, dv, tile), F32),
                        pltpu.VMEM((n_sm, tile, tile), F32), pltpu.VMEM((n_sm, tile, tile), F32),
                        pltpu.VMEM((n_sm, tile, tile), BF16), pltpu.VMEM((n_sm, tile, tile), BF16)],
        compiler_params=_cparams(("parallel", "parallel", "arbitrary")),
        name="flash_diff" if diff else "flash_plain",
    )(lam, *qs, *ks, vt, gain.reshape(dv, 1).astype(F32))


def _ssd_body(xbc_ref, z_ref, dt_ref, dtt_ref, cw_ref, cb_ref, dtb_ref, dtbt_ref, al_ref, alt_ref,
              dsk_ref, ng_ref, o_ref, xpad_ref, state_ref):
    chunk = xbc_ref.shape[1]
    d_in = z_ref.shape[2]
    gn = SSM_GROUPS * SSM_STATE
    c = pl.program_id(1)

    @pl.when(c == 0)
    def _():
        xpad_ref[0:8, :] = jnp.zeros((8, xpad_ref.shape[1]), F32)
        state_ref[...] = jnp.zeros(state_ref.shape, F32)

    xpad_ref[8:8 + chunk, :] = xbc_ref[0]
    conv = cb_ref[...]
    for w in range(SSM_CONV):
        conv = conv + cw_ref[w:w + 1, :] * xpad_ref[pl.ds(8 - (SSM_CONV - 1) + w, chunk), :]
    xpad_ref[0:8, :] = xpad_ref[chunk:chunk + 8, :]
    u = _silu(conv)
    xs = u[:, :d_in]
    bmat = u[:, d_in:d_in + gn]
    cmat = u[:, d_in + gn:]

    dt = _softplus(dt_ref[0] + dtb_ref[...])
    ad = dt * (-jnp.exp(al_ref[...]))
    dtt = _softplus(dtt_ref[0] + dtbt_ref[...])
    adt = dtt * (-jnp.exp(alt_ref[...]))
    row = lax.broadcasted_iota(jnp.int32, (chunk, chunk), 0)
    col = lax.broadcasted_iota(jnp.int32, (chunk, chunk), 1)
    lower = row >= col
    tril = jnp.where(lower, 1.0, 0.0).astype(BF16)
    triu = jnp.where(row <= col, 1.0, 0.0).astype(BF16)
    cs = sum(_dot(tril, part) for part in _split_bf16(ad, 3))
    cst = sum(_dot(part, triu) for part in _split_bf16(adt, 3))

    heads_per_group = SSM_HEADS // SSM_GROUPS
    dsk = dsk_ref[...]
    ys = []
    for g in range(SSM_GROUPS):
        bg = bmat[:, g * SSM_STATE:(g + 1) * SSM_STATE]
        cg = cmat[:, g * SSM_STATE:(g + 1) * SSM_STATE].astype(BF16)
        cb = _dot_nt(cg, bg.astype(BF16))
        bgt = bg.T.astype(BF16)
        for r in range(heads_per_group):
            h = g * heads_per_group + r
            ccol = cs[:, h:h + 1]
            crow = cst[h:h + 1, :]
            decay = jnp.exp(jnp.where(lower, ccol - crow, NEG_INF))
            x_h = xs[:, h * SSM_HEAD_DIM:(h + 1) * SSM_HEAD_DIM]
            xdt = x_h * dt[:, h:h + 1]
            y = _dot((cb * decay).astype(BF16), xdt.astype(BF16))
            st = state_ref[h]
            y = y + _dot(cg, st.astype(BF16)) * jnp.exp(ccol)
            last = cst[h:h + 1, chunk - 1:chunk]
            to_end = jnp.exp(last - ccol)
            state_ref[h] = st * jnp.exp(last) + _dot(bgt, (xdt * to_end).astype(BF16))
            ys.append(y + x_h * dsk[:, h * SSM_HEAD_DIM:(h + 1) * SSM_HEAD_DIM])

    y = jnp.concatenate(ys, axis=1) * _silu(z_ref[0])
    gw = d_in // SSM_GROUPS
    for g in range(SSM_GROUPS):
        seg = y[:, g * gw:(g + 1) * gw]
        o_ref[0, :, g * gw:(g + 1) * gw] = _rms(seg, ng_ref[:, g * gw:(g + 1) * gw]).astype(o_ref.dtype)


def ssd_mixer(xbc, z, dt_raw, conv_w, conv_b, dt_bias, a_log, d_skip, norm_gain):
    b, s, cch = xbc.shape
    d_in = z.shape[2]
    nc = s // SSM_CHUNK
    hpad = dt_raw.shape[2]
    dtt = jnp.transpose(dt_raw[:, :, :SSM_HEADS], (0, 2, 1))

    def lane_pad(v):
        return jnp.pad(v.astype(F32), (0, hpad - SSM_HEADS)).reshape(1, hpad)

    args = (xbc, z, dt_raw, dtt, conv_w.astype(F32), conv_b.reshape(1, cch).astype(F32),
            lane_pad(dt_bias), dt_bias.reshape(SSM_HEADS, 1).astype(F32),
            lane_pad(a_log), a_log.reshape(SSM_HEADS, 1).astype(F32),
            jnp.repeat(d_skip.astype(F32), SSM_HEAD_DIM).reshape(1, d_in),
            norm_gain.reshape(1, d_in).astype(F32))
    const = lambda bb, c: (0, 0)
    in_specs = [pl.BlockSpec((1, SSM_CHUNK, cch), lambda bb, c: (bb, c, 0)),
                pl.BlockSpec((1, SSM_CHUNK, d_in), lambda bb, c: (bb, c, 0)),
                pl.BlockSpec((1, SSM_CHUNK, hpad), lambda bb, c: (bb, c, 0)),
                pl.BlockSpec((1, SSM_HEADS, SSM_CHUNK), lambda bb, c: (bb, 0, c)),
                pl.BlockSpec((SSM_CONV, cch), const), pl.BlockSpec((1, cch), const),
                pl.BlockSpec((1, hpad), const), pl.BlockSpec((SSM_HEADS, 1), const),
                pl.BlockSpec((1, hpad), const), pl.BlockSpec((SSM_HEADS, 1), const),
                pl.BlockSpec((1, d_in), const), pl.BlockSpec((1, d_in), const)]
    return pl.pallas_call(
        _ssd_body, grid=(b, nc), in_specs=in_specs,
        out_specs=pl.BlockSpec((1, SSM_CHUNK, d_in), lambda bb, c: (bb, c, 0)),
        out_shape=jax.ShapeDtypeStruct((b, s, d_in), BF16),
        scratch_shapes=[pltpu.VMEM((SSM_CHUNK + 8, cch), F32),
                        pltpu.VMEM((SSM_HEADS, SSM_STATE, SSM_HEAD_DIM), F32)],
        compiler_params=_cparams(("parallel", "arbitrary")), name="ssd_mixer",
    )(*args)


def _out_proj_body(x_ref, a_ref, b_ref, wa_ref, wb_ref, o_ref):
    o_ref[...] = x_ref[...] + _dot(a_ref[...], wa_ref[...]) + _dot(b_ref[...], wb_ref[...])


def out_proj_residual(x2, a, bm, wa, wb, tm=512):
    t, d = x2.shape
    return pl.pallas_call(
        _out_proj_body, grid=(t // tm,),
        in_specs=[pl.BlockSpec((tm, d), lambda i: (i, 0)),
                  pl.BlockSpec((tm, a.shape[1]), lambda i: (i, 0)),
                  pl.BlockSpec((tm, bm.shape[1]), lambda i: (i, 0)),
                  pl.BlockSpec(wa.shape, lambda i: (0, 0)),
                  pl.BlockSpec(wb.shape, lambda i: (0, 0))],
        out_specs=pl.BlockSpec((tm, d), lambda i: (i, 0)),
        out_shape=jax.ShapeDtypeStruct((t, d), F32),
        compiler_params=_cparams(("parallel",)), name="out_proj",
    )(x2, a, bm, wa, wb)


def _ffn_body(x_ref, g_ref, wg_ref, wu_ref, wd_ref, o_ref, h_ref):
    f = pl.program_id(1)

    @pl.when(f == 0)
    def _():
        x = x_ref[...]
        h_ref[...] = _rms(x, g_ref[...]).astype(BF16)
        o_ref[...] = x

    half = h_ref.shape[0] // 2
    for r in (slice(0, half), slice(half, 2 * half)):
        h = h_ref[r, :]
        act = (_silu(_dot(h, wg_ref[...])) * _dot(h, wu_ref[...])).astype(BF16)
        o_ref[r, :] += _dot(act, wd_ref[...])


def ffn_residual(x2, gain, w_gate, w_up, w_down, tm=1024, tf=1408):
    t, d = x2.shape
    d_ff = w_gate.shape[1]
    return pl.pallas_call(
        _ffn_body, grid=(t // tm, d_ff // tf),
        in_specs=[pl.BlockSpec((tm, d), lambda i, f: (i, 0)),
                  pl.BlockSpec((1, d), lambda i, f: (0, 0)),
                  pl.BlockSpec((d, tf), lambda i, f: (0, f)),
                  pl.BlockSpec((d, tf), lambda i, f: (0, f)),
                  pl.BlockSpec((tf, d), lambda i, f: (f, 0))],
        out_specs=pl.BlockSpec((tm, d), lambda i, f: (i, 0)),
        out_shape=jax.ShapeDtypeStruct((t, d), F32),
        scratch_shapes=[pltpu.VMEM((tm, d), BF16)],
        compiler_params=_cparams(("parallel", "arbitrary")), name="ffn",
    )(x2, gain.reshape(1, d).astype(F32), w_gate, w_up, w_down)


MOE_ROWS = 256
ROUTE_IDX = 0
ROUTE_W = 2


def _moe_ffn_body(block_expert_ref, n_used_ref, xs_ref, wg_ref, wu_ref, wd_ref, o_ref):
    i = pl.program_id(0)

    @pl.when(i < n_used_ref[0])
    def _():
        x = xs_ref[...]
        act = (_silu(_dot(x, wg_ref[0])) * _dot(x, wu_ref[0])).astype(BF16)
        o_ref[...] = _dot(act, wd_ref[0]).astype(o_ref.dtype)

    @pl.when(i >= n_used_ref[0])
    def _():
        o_ref[...] = jnp.zeros(o_ref.shape, o_ref.dtype)


def moe_expert_ffn(xs, block_expert, n_used, w_gate, w_up, w_down):
    p, d = xs.shape
    d_ff = w_gate.shape[2]
    rows = MOE_ROWS
    grid_spec = pltpu.PrefetchScalarGridSpec(
        num_scalar_prefetch=2, grid=(p // rows,),
        in_specs=[pl.BlockSpec((rows, d), lambda i, be, nu: (i, 0)),
                  pl.BlockSpec((1, d, d_ff), lambda i, be, nu: (be[i], 0, 0)),
                  pl.BlockSpec((1, d, d_ff), lambda i, be, nu: (be[i], 0, 0)),
                  pl.BlockSpec((1, d_ff, d), lambda i, be, nu: (be[i], 0, 0))],
        out_specs=pl.BlockSpec((rows, d), lambda i, be, nu: (i, 0)))
    return pl.pallas_call(
        _moe_ffn_body, grid_spec=grid_spec, out_shape=jax.ShapeDtypeStruct((p, d), BF16),
        compiler_params=pltpu.CompilerParams(
            dimension_semantics=("arbitrary",), vmem_limit_bytes=MOE_VMEM_LIMIT),
        name="moe_expert_ffn",
    )(block_expert, n_used, xs, w_gate, w_up, w_down)


def _moe_combine_body(x_ref, y0_ref, y1_ref, route_ref, o_ref):
    route = route_ref[...]
    lane = lax.broadcasted_iota(jnp.int32, route.shape, 1)
    w0 = jnp.sum(jnp.where(lane == ROUTE_W, route, 0.0), axis=1, keepdims=True)
    w1 = jnp.sum(jnp.where(lane == ROUTE_W + 1, route, 0.0), axis=1, keepdims=True)
    o_ref[...] = x_ref[...] + w0 * y0_ref[...].astype(F32) + w1 * y1_ref[...].astype(F32)


def moe_combine(x2, y0, y1, route, tm=512):
    t, d = x2.shape
    row = lambda i: (i, 0)
    return pl.pallas_call(
        _moe_combine_body, grid=(t // tm,),
        in_specs=[pl.BlockSpec((tm, d), row), pl.BlockSpec((tm, d), row), pl.BlockSpec((tm, d), row),
                  pl.BlockSpec((tm, LANES), row)],
        out_specs=pl.BlockSpec((tm, d), row),
        out_shape=jax.ShapeDtypeStruct((t, d), F32),
        compiler_params=_cparams(("parallel",)), name="moe_combine",
    )(x2, y0, y1, route)


def moe_residual(x2, h, route, w_gate, w_up, w_down):
    t, d = x2.shape
    n_e = w_gate.shape[0]
    rows = MOE_ROWS
    expert = route[:, ROUTE_IDX:ROUTE_IDX + 2].astype(jnp.int32).reshape(-1)
    onehot = (expert[:, None] == jnp.arange(n_e, dtype=jnp.int32)[None, :]).astype(jnp.int32)
    rank = jnp.take_along_axis(jnp.cumsum(onehot, axis=0), expert[:, None], axis=1)[:, 0] - 1
    counts = jnp.sum(onehot, axis=0)
    padded = (counts + rows - 1) // rows * rows
    ends = jnp.cumsum(padded)
    slot = (ends - padded)[expert] + rank
    p_rows = 2 * t + n_e * rows
    token_of_slot = jnp.zeros((p_rows,), jnp.int32).at[slot].set(
        jnp.arange(2 * t, dtype=jnp.int32) // 2, unique_indices=True)
    block_start = jnp.arange(p_rows // rows, dtype=jnp.int32) * rows
    block_expert = jnp.minimum(jnp.searchsorted(ends, block_start, side="right"), n_e - 1).astype(jnp.int32)
    n_used = (ends[-1] // rows).astype(jnp.int32).reshape(1)

    xs = jnp.take(h, token_of_slot, axis=0)
    ys = moe_expert_ffn(xs, block_expert, n_used, w_gate, w_up, w_down)
    slot2 = slot.reshape(t, 2)
    return moe_combine(x2, jnp.take(ys, slot2[:, 0], axis=0), jnp.take(ys, slot2[:, 1], axis=0), route)


def _router_body(x_ref, g_ref, r_ref, h_ref, o_ref):
    h = _rms(x_ref[...], g_ref[...])
    h_ref[...] = h.astype(h_ref.dtype)
    logits = jnp.dot(h, r_ref[...], precision=lax.Precision.HIGHEST, preferred_element_type=F32)
    lane = lax.broadcasted_iota(jnp.int32, logits.shape, 1).astype(F32)
    low = jnp.float32(-3.0e38)
    logits = jnp.where(lane < N_EXPERTS, logits, low)
    m1 = jnp.max(logits, axis=1, keepdims=True)
    i1 = jnp.min(jnp.where(logits == m1, lane, float(LANES)), axis=1, keepdims=True)
    rest = jnp.where(lane == i1, low, logits)
    m2 = jnp.max(rest, axis=1, keepdims=True)
    i2 = jnp.min(jnp.where(rest == m2, lane, float(LANES)), axis=1, keepdims=True)
    ex = jnp.exp(m2 - m1)
    w1 = 1.0 / (1.0 + ex)
    w2 = ex / (1.0 + ex)
    o_ref[...] = jnp.where(lane == ROUTE_IDX, i1, jnp.where(lane == ROUTE_IDX + 1, i2, jnp.where(
        lane == ROUTE_W, w1, jnp.where(lane == ROUTE_W + 1, w2, 0.0))))


def router(x2, gain, router_w, tm=512):
    t, d = x2.shape
    r_pad = jnp.pad(router_w.astype(F32), ((0, 0), (0, LANES - router_w.shape[1])))
    return pl.pallas_call(
        _router_body, grid=(t // tm,),
        in_specs=[pl.BlockSpec((tm, d), lambda i: (i, 0)),
                  pl.BlockSpec((1, d), lambda i: (0, 0)),
                  pl.BlockSpec((d, LANES), lambda i: (0, 0))],
        out_specs=[pl.BlockSpec((tm, d), lambda i: (i, 0)), pl.BlockSpec((tm, LANES), lambda i: (i, 0))],
        out_shape=[jax.ShapeDtypeStruct((t, d), BF16), jax.ShapeDtypeStruct((t, LANES), F32)],
        compiler_params=_cparams(("parallel",)), name="router",
    )(x2, gain.reshape(1, d).astype(F32), r_pad)


def _compress_body(ch_ref, nx_ref, pos_ref, w1_ref, w2_ref, gain_ref, cos_ref, sin_ref, rot_ref, o_ref,
                   *, is_key):
    a = _dot((ch_ref[0, 0] + pos_ref[0]).astype(BF16), w1_ref[0])
    a = a + _dot((nx_ref[0, 0] + pos_ref[1]).astype(BF16), w1_ref[1])
    out = _dot(_silu(a).astype(BF16), w2_ref[...])
    if is_key:
        out = _rms(out, gain_ref[...])
        hi, lo = _split_bf16(out, 2)
        partner = _dot(hi, rot_ref[...]) + _dot(lo, rot_ref[...])
        out = out * cos_ref[...] + partner * sin_ref[...]
    o_ref[0, 0] = out


def nsa_compress(t, pos, w1, w2, gain, seq, is_key):
    b, s, _ = t.shape
    g, d = NSA_KV_GROUPS, NSA_HEAD_DIM
    n_ch = s // NSA_CMP_STRIDE
    half = NSA_CMP_STRIDE * d
    ch = t.reshape(b, n_ch, NSA_CMP_STRIDE, g, d).transpose(0, 3, 1, 2, 4).reshape(b, g, n_ch, half)
    nxt = jnp.concatenate([ch[:, :, 1:], jnp.zeros((b, g, 1, half), F32)], axis=2)
    pos2 = pos.astype(F32).reshape(2, 1, half)
    w1s = w1.astype(BF16).reshape(2, half, d)
    cmp_end = jnp.arange(n_ch) * NSA_CMP_STRIDE + NSA_CMP_BLOCK - 1
    inv_freq = 1.0 / (ROPE_THETA ** (jnp.arange(0, d, 2, dtype=F32) / d))
    ang = cmp_end.astype(F32)[:, None] * inv_freq[None, :]
    cos = jnp.concatenate([jnp.cos(ang)] * 2, axis=1)
    sin = jnp.concatenate([jnp.sin(ang)] * 2, axis=1)
    rot = np.zeros((d, d), np.float32)
    rot[np.arange(d // 2) + d // 2, np.arange(d // 2)] = -1.0
    rot[np.arange(d // 2), np.arange(d // 2) + d // 2] = 1.0
    blk = lambda bb, gg: (bb, gg, 0, 0)
    c2 = lambda bb, gg: (0, 0)
    c3 = lambda bb, gg: (0, 0, 0)
    return pl.pallas_call(
        functools.partial(_compress_body, is_key=is_key), grid=(b, g),
        in_specs=[pl.BlockSpec((1, 1, n_ch, half), blk), pl.BlockSpec((1, 1, n_ch, half), blk),
                  pl.BlockSpec((2, 1, half), c3), pl.BlockSpec((2, half, d), c3),
                  pl.BlockSpec((d, d), c2), pl.BlockSpec((1, d), c2),
                  pl.BlockSpec((n_ch, d), c2), pl.BlockSpec((n_ch, d), c2), pl.BlockSpec((d, d), c2)],
        out_specs=pl.BlockSpec((1, 1, n_ch, d), blk),
        out_shape=jax.ShapeDtypeStruct((b, g, n_ch, d), F32),
        compiler_params=_cparams(("parallel", "parallel")), name="nsa_compress",
    )(ch, nxt, pos2, w1s, w2.astype(BF16), gain.reshape(1, d).astype(F32), cos, sin,
      jnp.asarray(rot, dtype=BF16))


def _nsa_body(q_ref, ck_ref, cvt_ref, ksl_ref, vslt_ref, kwn_ref, vwnt_ref, ovt_ref, glt_ref, o_ref,
              sc_ref, phi_ref, plo_ref, imp_ref, bias_ref, ss_ref, ps_ref, ss1_ref, ps1_ref, sw_ref, pw_ref,
              ow_ref, *, tq):
    g = pl.program_id(1)
    i = pl.program_id(2)
    d = NSA_HEAD_DIM
    rep = NSA_HEADS // NSA_KV_GROUPS
    t0 = i * tq
    n_cmp = ck_ref.shape[1]
    n_sel = ovt_ref.shape[0]
    width = rep * tq
    sub = NSA_SEL_BLOCK
    dead = 0.5 * NEG_INF
    v_rows = pl.ds(pl.multiple_of(g * d, d), d)

    qb = q_ref[0]
    q4 = jnp.concatenate([qb[:, r * d:(r + 1) * d] for r in range(rep)], axis=0)
    q4 = jnp.concatenate([q4, q4], axis=1)
    lane = lax.broadcasted_iota(jnp.int32, q4.shape, 1)
    q4 = jnp.where(jnp.right_shift(lane, d.bit_length() - 1) == g, q4, jnp.zeros_like(q4))

    def qpos_of(shape):
        return t0 + (lax.broadcasted_iota(jnp.int32, shape, 1) & (tq - 1))

    s = _dot_nt(ck_ref[0], q4)
    cmp_end = lax.broadcasted_iota(jnp.int32, s.shape, 0) * NSA_CMP_STRIDE + (NSA_CMP_BLOCK - 1)
    s = jnp.where(cmp_end <= qpos_of(s.shape), s, NEG_INF)
    sc_ref[...] = s
    m_c = jnp.max(s, axis=0, keepdims=True)
    lpart = jnp.zeros((sub, width), F32)
    for r in range(n_cmp // sub):
        e = jnp.exp2(sc_ref[r * sub:(r + 1) * sub, :] - m_c)
        lpart = lpart + e
        hi = e.astype(BF16)
        phi_ref[r * sub:(r + 1) * sub, :] = hi
        plo_ref[r * sub:(r + 1) * sub, :] = (e - hi.astype(F32)).astype(BF16)
    inv_c = jnp.where(m_c > dead, 1.0 / jnp.sum(lpart, axis=0, keepdims=True), 0.0)
    o_c = _dot(cvt_ref[0, v_rows, :], phi_ref[...]) * inv_c

    imp4 = (_dot(ovt_ref[...], phi_ref[...]) + _dot(ovt_ref[...], plo_ref[...])) * inv_c
    imp = imp4[:, 0:tq]
    for r in range(1, rep):
        imp = imp + imp4[:, r * tq:(r + 1) * tq]
    blk = lax.broadcasted_iota(jnp.int32, imp.shape, 0)
    qp = t0 + lax.broadcasted_iota(jnp.int32, imp.shape, 1)
    cur = jnp.right_shift(qp, NSA_SEL_BLOCK.bit_length() - 1)
    forced = (blk == 0) | (blk == cur) | (blk == cur - 1)
    future = blk * NSA_SEL_BLOCK > qp
    imp_ref[...] = jnp.where(future, -FORCE_SCORE, jnp.where(forced, FORCE_SCORE, imp))
    bias_ref[...] = jnp.full(bias_ref.shape, NEG_INF, F32)

    n_live = jnp.minimum((t0 + tq - 1) // NSA_SEL_BLOCK + 1, n_sel)
    n_var = max(n_sel // 32, 1)
    rows_per = n_sel // n_var
    top_n = float(min(NSA_TOP_N, n_sel))
    for v in range(n_var):
        rows = rows_per * (v + 1)

        @pl.when((n_live > rows_per * v) & (n_live <= rows))
        def _():
            mine = imp_ref[0:rows, :]
            blk_r = lax.broadcasted_iota(jnp.int32, mine.shape, 0)

            def count(i2, cnt):
                other = imp_ref[pl.ds(i2, 1), :]
                beats = (other > mine) | ((other == mine) & (blk_r > i2))
                return cnt + jnp.where(beats, 1.0, 0.0)

            rank = lax.fori_loop(0, n_live, count, jnp.zeros(mine.shape, F32))
            bias = jnp.where(rank < top_n, 0.0, NEG_INF)
            bias_ref[0:rows, :] = jnp.concatenate([bias] * rep, axis=1)

    init = (jnp.full((1, width), NEG_INF, F32), jnp.zeros((1, width), F32), jnp.zeros((d, width), F32))

    chunk = 8 * sub
    n_sub = chunk // sub

    s_slots = (ss_ref, ss1_ref)
    p_slots = (ps_ref, ps1_ref)

    def sel_scores(c, slot, diagonal=False):
        start = pl.multiple_of(c * chunk, chunk)
        s = _dot_nt(ksl_ref[0, pl.ds(start, chunk), :], q4)
        if diagonal:
            kpos = start + lax.broadcasted_iota(jnp.int32, s.shape, 0)
            s = jnp.where(kpos <= qpos_of(s.shape), s, NEG_INF)
        s_slots[slot][...] = s

    def sel_update(c, slot, carry):
        m_prev, l_prev, acc = carry
        s_ref, p_ref = s_slots[slot], p_slots[slot]
        biases = [bias_ref[pl.ds(c * n_sub + r, 1), :] for r in range(n_sub)]
        m_new = m_prev
        for r in range(n_sub):
            m_new = jnp.maximum(
                m_new, jnp.max(s_ref[r * sub:(r + 1) * sub, :], axis=0, keepdims=True) + biases[r])
        alpha = jnp.exp2(m_prev - m_new)
        live = m_new > dead
        lpart = jnp.zeros((sub, width), F32)
        for r in range(n_sub):
            shift = jnp.where(live, biases[r] - m_new, NEG_INF)
            p = jnp.exp2(s_ref[r * sub:(r + 1) * sub, :] + shift)
            lpart = lpart + p
            p_ref[r * sub:(r + 1) * sub, :] = p.astype(BF16)
        l_new = alpha * l_prev + jnp.sum(lpart, axis=0, keepdims=True)
        vt = vslt_ref[v_rows, pl.ds(pl.multiple_of(c * chunk, chunk), chunk)]
        return m_new, l_new, alpha * acc + _dot(vt, p_ref[...])

    c_last = (t0 + tq - 1) // chunk
    last_past = jnp.maximum(c_last - 1, 0)

    @pl.when(c_last > 0)
    def _():
        sel_scores(0, 0)

    def pair(k, carry):
        sel_scores(jnp.minimum(2 * k + 1, last_past), 1)
        carry = sel_update(2 * k, 0, carry)
        sel_scores(jnp.minimum(2 * k + 2, last_past), 0)
        return sel_update(2 * k + 1, 1, carry)

    carry = lax.fori_loop(0, c_last // 2, pair, init)
    carry = lax.cond(c_last % 2 == 1, lambda cr: sel_update(c_last - 1, 0, cr), lambda cr: cr, carry)
    sel_scores(c_last, 0, diagonal=True)
    _, l_s, acc_s = sel_update(c_last, 0, carry)
    o_s = acc_s / l_s

    def win_chunk(c, carry):
        m_prev, l_prev, acc = carry
        start = pl.multiple_of(c * tq, tq)
        s = _dot_nt(kwn_ref[0, pl.ds(start, tq), :], q4)
        kpos = start + lax.broadcasted_iota(jnp.int32, s.shape, 0)
        qpos = qpos_of(s.shape)
        s = jnp.where((kpos <= qpos) & (kpos > qpos - NSA_WINDOW), s, NEG_INF)
        m_new = jnp.maximum(m_prev, jnp.max(s, axis=0, keepdims=True))
        alpha = jnp.exp2(m_prev - m_new)
        p = jnp.exp2(s + jnp.where(m_new > dead, -m_new, NEG_INF))
        l_new = alpha * l_prev + jnp.sum(p, axis=0, keepdims=True)
        vt = vwnt_ref[v_rows, pl.ds(start, tq)]
        return m_new, l_new, alpha * acc + _dot(vt, p.astype(BF16))

    n_back = NSA_WINDOW // tq

    @pl.when(i < n_back)
    def _():
        _, l_w, acc_w = lax.fori_loop(0, i + 1, win_chunk, init)
        ow_ref[...] = acc_w / l_w

    @pl.when(i >= n_back)
    def _():
        start = pl.multiple_of(t0 - NSA_WINDOW, tq)
        s = _dot_nt(kwn_ref[0, pl.ds(start, NSA_WINDOW + tq), :], q4)
        kpos = start + lax.broadcasted_iota(jnp.int32, (tq, width), 0)
        qpos = qpos_of((tq, width))
        sw_ref[0:tq, :] = jnp.where(kpos > qpos - NSA_WINDOW, s[0:tq, :], NEG_INF)
        sw_ref[tq:NSA_WINDOW, :] = s[tq:NSA_WINDOW, :]
        sw_ref[NSA_WINDOW:, :] = jnp.where(kpos + NSA_WINDOW <= qpos, s[NSA_WINDOW:, :], NEG_INF)
        m_w = jnp.max(sw_ref[...], axis=0, keepdims=True)
        lpart = jnp.zeros((sub, width), F32)
        for r in range((NSA_WINDOW + tq) // sub):
            p = jnp.exp2(sw_ref[r * sub:(r + 1) * sub, :] - m_w)
            lpart = lpart + p
            pw_ref[r * sub:(r + 1) * sub, :] = p.astype(BF16)
        vt = vwnt_ref[v_rows, pl.ds(start, NSA_WINDOW + tq)]
        ow_ref[...] = _dot(vt, pw_ref[...]) / jnp.sum(lpart, axis=0, keepdims=True)

    o_w = ow_ref[...]

    def gate(branch):
        rows = [glt_ref[pl.ds((g * rep + r) * 3 + branch, 1), :] for r in range(rep)]
        return _sigmoid(jnp.concatenate(rows, axis=1))

    out = gate(0) * o_c + gate(1) * o_s + gate(2) * o_w
    out_t = jnp.concatenate([out, jnp.zeros_like(out)], axis=0).T
    o_ref[0] = jnp.concatenate([out_t[r * tq:(r + 1) * tq, :d] for r in range(rep)],
                               axis=1).astype(o_ref.dtype)


def nsa_overlap_t(n_cmp, n_sel):
    c_start = np.arange(n_cmp)[None, :] * NSA_CMP_STRIDE
    s_start = np.arange(n_sel)[:, None] * NSA_SEL_BLOCK
    hit = (c_start < s_start + NSA_SEL_BLOCK) & (c_start + NSA_CMP_BLOCK > s_start)
    hit = hit & (np.arange(n_cmp)[None, :] < n_cmp - NSA_CMP_BLOCK // NSA_CMP_STRIDE + 1)
    return jnp.asarray(hit.astype(np.float32), dtype=BF16)


def nsa_attention(qn, ck, cvt, ksl, vslt, kwn, vwnt, glt, tq=128):
    b, s, _ = qn.shape
    g, d = NSA_KV_GROUPS, NSA_HEAD_DIM
    rep = NSA_HEADS // g
    n_cmp = ck.shape[1]
    n_sel = s // NSA_SEL_BLOCK
    nq = s // tq
    ovt = nsa_overlap_t(n_cmp, n_sel)
    width = rep * tq
    chunk = 8 * NSA_SEL_BLOCK
    full3 = lambda bb, gg, i: (bb, 0, 0)
    seq_t = lambda bb, gg, i: (0, bb)
    return pl.pallas_call(
        functools.partial(_nsa_body, tq=tq), grid=(b, g, nq),
        in_specs=[pl.BlockSpec((1, tq, rep * d), lambda bb, gg, i: (bb, i, gg)),
                  pl.BlockSpec((1, n_cmp, g * d), full3), pl.BlockSpec((1, g * d, n_cmp), full3),
                  pl.BlockSpec((1, s, g * d), full3), pl.BlockSpec((g * d, s), seq_t),
                  pl.BlockSpec((1, s, g * d), full3), pl.BlockSpec((g * d, s), seq_t),
                  pl.BlockSpec((n_sel, n_cmp), lambda bb, gg, i: (0, 0)),
                  pl.BlockSpec((glt.shape[0], tq), lambda bb, gg, i: (0, bb * nq + i))],
        out_specs=pl.BlockSpec((1, tq, rep * d), lambda bb, gg, i: (bb, i, gg)),
        out_shape=jax.ShapeDtypeStruct((b, s, g * rep * d), BF16),
        scratch_shapes=[pltpu.VMEM((n_cmp, width), F32), pltpu.VMEM((n_cmp, width), BF16),
                        pltpu.VMEM((n_cmp, width), BF16), pltpu.VMEM((n_sel, tq), F32),
                        pltpu.VMEM((n_sel, width), F32), pltpu.VMEM((chunk, width), F32),
                        pltpu.VMEM((chunk, width), BF16), pltpu.VMEM((chunk, width), F32),
                        pltpu.VMEM((chunk, width), BF16), pltpu.VMEM((NSA_WINDOW + tq, width), F32),
                        pltpu.VMEM((NSA_WINDOW + tq, width), BF16), pltpu.VMEM((d, width), F32)],
        compiler_params=_cparams(("parallel", "parallel", "arbitrary")), name="nsa_attention",
    )(qn, ck, cvt, ksl, vslt, kwn, vwnt, ovt, glt)


def _pad_cols(w, n):
    return jnp.pad(w, ((0, 0), (0, n - w.shape[1])))


def _even_layer(x2, b, s, layer_idx, norm_mix, w_in, q_gain, k_gain, lam, subln_gain, conv_w, conv_b,
                dt_bias, a_log, d_skip, ssm_norm_gain, w_out, norm_ffn, w_gate, w_up, w_down):
    nq = DA_HEADS * 2 * DA_HEAD_DIM
    nv = DA_HEADS * DA_V_DIM
    cch = SSM_D_INNER + 2 * SSM_GROUPS * SSM_STATE
    offs = np.cumsum([0, nq, nq, nv, SSM_D_INNER, cch, SSM_HEADS])
    wb = w_in.astype(BF16)
    pieces = [wb[:, offs[k]:offs[k + 1]] for k in range(6)]
    pieces[2] = pieces[2].T
    pieces[5] = _pad_cols(pieces[5], LANES)
    posts = [HeadNorm(q_gain, DA_HEAD_DIM, rope=True, mul=DA_HEAD_DIM ** -0.5 * LOG2E),
             HeadNorm(k_gain, DA_HEAD_DIM, rope=True), None, None, None, None]
    q, k, vt, z, xbc, dt = norm_proj(x2, norm_mix, pieces, [BF16, BF16, BF16, F32, F32, F32], posts, s,
                                     _rope_tables(s, DA_HEAD_DIM), transposed=(2,))
    qn = q.reshape(b, s, nq)
    kn = k.reshape(b, s, nq)
    lam_init = 0.8 - 0.6 * math.exp(-0.3 * layer_idx)
    lf = lam.astype(F32)
    lam_full = jnp.exp(jnp.sum(lf[0] * lf[1])) - jnp.exp(jnp.sum(lf[2] * lf[3])) + lam_init
    a_out = flash_attention(lam_full.reshape(1), [qn], [kn], vt, subln_gain, DA_HEADS, DA_V_DIM,
                            diff=True, out_scale=1.0 - lam_init)
    b_out = ssd_mixer(xbc.reshape(b, s, cch), z.reshape(b, s, SSM_D_INNER), dt.reshape(b, s, LANES),
                      conv_w, conv_b, dt_bias, a_log, d_skip, ssm_norm_gain)
    wo = w_out.astype(BF16)
    x2 = out_proj_residual(x2, a_out.reshape(-1, nv), b_out.reshape(-1, SSM_D_INNER), wo[:nv], wo[nv:])
    return ffn_residual(x2, norm_ffn, w_gate.astype(BF16), w_up.astype(BF16), w_down.astype(BF16))


def _odd_layer(x2, b, s, norm_mix, w_in, q_gain, k_gain, cmp_pos, cmp_w1, cmp_w2, cq_gain, ckv_gain,
               w_uq, w_ukv, qn_gain, qr_gain, kn_gain, kr_gain, w_out, norm_ffn, router_w, w_gate, w_up,
               w_down):
    g, d = NSA_KV_GROUPS, NSA_HEAD_DIM
    nq = NSA_HEADS * d
    nkv = g * d
    sizes = [nq] + [nkv] * 6 + [NSA_HEADS * 3, w_uq.shape[0], w_ukv.shape[0], MLA_ROPE_DIM]
    offs = np.cumsum([0] + sizes)
    wb = w_in.astype(BF16)
    pieces = [wb[:, offs[k]:offs[k + 1]] for k in range(len(sizes))]
    for k in (4, 6):
        pieces[k] = pieces[k].T
    pieces[7] = jnp.pad(pieces[7].T, ((0, 32 - NSA_HEADS * 3), (0, 0)))
    pieces[10] = _pad_cols(pieces[10], LANES)
    tables = _rope_tables(s, d)
    posts = [None] * len(sizes)
    posts[0] = HeadNorm(q_gain, d, rope=True, mul=d ** -0.5 * LOG2E)
    posts[3] = HeadNorm(k_gain[1], d, rope=True)
    posts[5] = HeadNorm(k_gain[2], d, rope=True)
    posts[10] = HeadNorm(kr_gain, MLA_ROPE_DIM, rope=True)
    (q, kc, vc, ksl, vslt, kwn, vwnt, glt, cq, ckv, k_rope) = norm_proj(
        x2, norm_mix, pieces, [BF16, F32, F32, BF16, BF16, BF16, BF16, F32, F32, F32, BF16], posts, s, tables,
        transposed=(4, 6, 7))

    qn = q.reshape(b, s, nq)
    ksl_n = ksl.reshape(b, s, nkv)
    kwn_n = kwn.reshape(b, s, nkv)
    ck = nsa_compress(kc.reshape(b, s, nkv), cmp_pos[0], cmp_w1[0], cmp_w2[0], k_gain[0], s, True)
    cv = nsa_compress(vc.reshape(b, s, nkv), cmp_pos[1], cmp_w1[1], cmp_w2[1], k_gain[0], s, False)
    n_cmp = ck.shape[2]
    ck = ck.transpose(0, 2, 1, 3).reshape(b, n_cmp, nkv).astype(BF16)
    cvt = cv.transpose(0, 1, 3, 2).reshape(b, nkv, n_cmp).astype(BF16)
    c_out = nsa_attention(qn, ck, cvt, ksl_n, vslt, kwn_n, vwnt, glt).reshape(b * s, nq)

    h = MLA_HEADS
    dqk = MLA_NOPE_DIM + MLA_ROPE_DIM
    wq = w_uq.astype(BF16).reshape(-1, h, dqk)
    wq_nope = wq[:, :, :MLA_NOPE_DIM].reshape(-1, h * MLA_NOPE_DIM)
    wq_rope = jnp.pad(wq[:, :, MLA_NOPE_DIM:], ((0, 0), (0, 0), (0, LANES - MLA_ROPE_DIM)))
    wq_rope = wq_rope.reshape(-1, h * LANES)
    wkv = w_ukv.astype(BF16).reshape(-1, h, MLA_NOPE_DIM + MLA_V_DIM)
    wk_nope = wkv[:, :, :MLA_NOPE_DIM].reshape(-1, h * MLA_NOPE_DIM)
    wv = wkv[:, :, MLA_NOPE_DIM:].reshape(-1, h * MLA_V_DIM)
    q_mul = dqk ** -0.5 * LOG2E
    q_nope, q_rope = norm_proj(
        cq, cq_gain, [wq_nope, wq_rope], [BF16, BF16],
        [HeadNorm(qn_gain, MLA_NOPE_DIM, mul=q_mul), HeadNorm(qr_gain, MLA_ROPE_DIM, rope=True, mul=q_mul)],
        s, tables)
    k_nope, vt = norm_proj(ckv, ckv_gain, [wk_nope, wv.T], [BF16, BF16],
                           [HeadNorm(kn_gain, MLA_NOPE_DIM), None], transposed=(1,))
    shp = lambda t: t.reshape(b, s, t.shape[-1])
    d_out = flash_attention(jnp.zeros((1,), F32), [shp(q_nope), shp(q_rope)], [shp(k_nope), shp(k_rope)],
                            vt, jnp.ones((MLA_V_DIM,), F32), h, MLA_V_DIM, diff=False)

    wo = w_out.astype(BF16)
    x2 = out_proj_residual(x2, c_out, d_out.reshape(b * s, h * MLA_V_DIM), wo[:nq], wo[nq:])
    h, route = router(x2, norm_ffn, router_w)
    return moe_residual(x2, h, route, w_gate.astype(BF16), w_up.astype(BF16), w_down.astype(BF16))


def kernel(x, ev_norm_mix, ev_w_in, da_q_gain, da_k_gain, da_lambda, da_subln_gain, ssm_conv_w, ssm_conv_b, ssm_dt_bias, ssm_a_log, ssm_d, ssm_norm_gain, ev_w_out, ev_norm_ffn, ffn_w_gate, ffn_w_up, ffn_w_down, od_norm_mix, od_w_in, nsa_q_gain, nsa_k_gain, nsa_cmp_pos, nsa_cmp_w1, nsa_cmp_w2, mla_cq_gain, mla_ckv_gain, mla_w_uq, mla_w_ukv, mla_qn_gain, mla_qr_gain, mla_kn_gain, mla_kr_gain, od_w_out, od_norm_ffn, moe_router, moe_w_gate, moe_w_up, moe_w_down):
    b, s, d = x.shape
    x2 = x.reshape(b * s, d)
    depth = ev_norm_mix.shape[0] + od_norm_mix.shape[0]
    for layer in range(depth):
        i = layer // 2
        if layer % 2 == 0:
            x2 = _even_layer(x2, b, s, layer, ev_norm_mix[i], ev_w_in[i], da_q_gain[i], da_k_gain[i],
                             da_lambda[i], da_subln_gain[i], ssm_conv_w[i], ssm_conv_b[i],
                             ssm_dt_bias[i], ssm_a_log[i], ssm_d[i], ssm_norm_gain[i], ev_w_out[i],
                             ev_norm_ffn[i], ffn_w_gate[i], ffn_w_up[i], ffn_w_down[i])
        else:
            x2 = _odd_layer(x2, b, s, od_norm_mix[i], od_w_in[i], nsa_q_gain[i], nsa_k_gain[i],
                            nsa_cmp_pos[i], nsa_cmp_w1[i], nsa_cmp_w2[i], mla_cq_gain[i],
                            mla_ckv_gain[i], mla_w_uq[i], mla_w_ukv[i], mla_qn_gain[i], mla_qr_gain[i],
                            mla_kn_gain[i], mla_kr_gain[i], od_w_out[i], od_norm_ffn[i], moe_router[i],
                            moe_w_gate[i], moe_w_up[i], moe_w_down[i])
    return x2.reshape(b, s, d)
```

```python
import functools
import math

import numpy as np
import jax
import jax.numpy as jnp
from jax import lax
from jax.experimental import pallas as pl
from jax.experimental.pallas import tpu as pltpu

F32 = jnp.float32
BF16 = jnp.bfloat16

ROPE_THETA = 10000.0
NORM_EPS = 1e-6
NEG_INF = -1e30
FORCE_SCORE = 1e6
LOG2E = 1.4426950408889634

DA_HEADS = 4
DA_HEAD_DIM = 64
DA_V_DIM = 2 * DA_HEAD_DIM
SSM_HEADS = 8
SSM_HEAD_DIM = 64
SSM_D_INNER = SSM_HEADS * SSM_HEAD_DIM
SSM_GROUPS = 2
SSM_STATE = 128
SSM_CONV = 4
SSM_CHUNK = 256
NSA_HEADS = 8
NSA_KV_GROUPS = 2
NSA_HEAD_DIM = 64
NSA_CMP_BLOCK = 32
NSA_CMP_STRIDE = 16
NSA_SEL_BLOCK = 64
NSA_TOP_N = 16
NSA_WINDOW = 512
MLA_HEADS = 4
MLA_NOPE_DIM = 128
MLA_ROPE_DIM = 64
MLA_V_DIM = 128
N_EXPERTS = 8

LANES = 128
SUBLANES = 8
VMEM_LIMIT = 48 * 1024 * 1024
MOE_VMEM_LIMIT = 58 * 1024 * 1024

NT_DIMS = (((1,), (1,)), ((), ()))


def _cparams(semantics):
    return pltpu.CompilerParams(dimension_semantics=semantics, vmem_limit_bytes=VMEM_LIMIT)


def _dot(a, b):
    return jnp.dot(a, b, preferred_element_type=F32)


def _dot_nt(a, b):
    return lax.dot_general(a, b, NT_DIMS, preferred_element_type=F32)


def _split_bf16(x, parts):
    out = []
    for _ in range(parts):
        hi = x.astype(BF16)
        out.append(hi)
        x = x - hi.astype(F32)
    return out


def _fold_rows(x):
    return jnp.sum(x.reshape(x.shape[0] // SUBLANES, SUBLANES, x.shape[1]), axis=0)


def _sigmoid(x):
    return 1.0 / (1.0 + jnp.exp(-x))


def _silu(x):
    return x * _sigmoid(x)


def _softplus(x):
    return jnp.maximum(x, 0.0) + jnp.log(1.0 + jnp.exp(-jnp.abs(x)))


def _rms(x, gain):
    ms = jnp.mean(x * x, axis=-1, keepdims=True)
    return x * lax.rsqrt(ms + NORM_EPS) * gain


class HeadNorm:
    def __init__(self, gain, hd, rope=False, mul=1.0):
        self.gain, self.hd, self.rope, self.mul = gain, hd, rope, mul


def _head_norm(y, gain, bd, cos_ref, sin_ref, post):
    n = y.shape[1]
    hd = post.hd
    hi, lo = _split_bf16(y * y, 2)
    ss = _dot(hi, bd) + _dot(lo, bd)
    yn = y * lax.rsqrt(ss * (1.0 / hd) + NORM_EPS) * gain
    if post.rope:
        reps = n // LANES
        cos = jnp.concatenate([cos_ref[...]] * reps, axis=1) if reps > 1 else cos_ref[...]
        sin = jnp.concatenate([sin_ref[...]] * reps, axis=1) if reps > 1 else sin_ref[...]
        lane = lax.broadcasted_iota(jnp.int32, yn.shape, 1)
        first_half = (lane & (hd - 1)) < (hd // 2)
        partner = jnp.where(first_half, pltpu.roll(yn, n - hd // 2, 1), pltpu.roll(yn, hd // 2, 1))
        yn = yn * cos + partner * sin
    if post.mul != 1.0:
        yn = yn * post.mul
    return yn


def _norm_proj_body(x_ref, g_ref, *refs, posts, use_rope, transposed):
    if use_rope:
        cos_ref, sin_ref = refs[0], refs[1]
        refs = refs[2:]
    else:
        cos_ref = sin_ref = None
    n_out = len(posts)
    n_aux = 2 * sum(p is not None for p in posts)
    w_refs, aux, o_refs = refs[:n_out], refs[n_out:n_out + n_aux], refs[n_out + n_aux:]
    h = _rms(x_ref[...], g_ref[...]).astype(BF16)
    a = 0
    for k, (w_ref, o_ref, post) in enumerate(zip(w_refs, o_refs, posts)):
        if k in transposed:
            o_ref[...] = _dot_nt(w_ref[...], h).astype(o_ref.dtype)
            continue
        y = _dot(h, w_ref[...])
        if post is not None:
            y = _head_norm(y, aux[a][...], aux[a + 1][...], cos_ref, sin_ref, post)
            a += 2
        o_ref[...] = y.astype(o_ref.dtype)


def norm_proj(x2, gain, weights, out_dtypes, posts=None, seq=None, rope_tables=None, transposed=(), tm=512):
    t, d = x2.shape
    posts = posts or [None] * len(weights)
    transposed = frozenset(transposed)
    use_rope = any(p is not None and p.rope for p in posts)
    const = lambda i: (0, 0)
    args = [x2, gain.reshape(1, d).astype(F32)]
    in_specs = [pl.BlockSpec((tm, d), lambda i: (i, 0)), pl.BlockSpec((1, d), const)]
    if use_rope:
        per_seq = seq // tm
        args += list(rope_tables)
        in_specs += [pl.BlockSpec((tm, LANES), lambda i: (i % per_seq, 0))] * 2
    args += list(weights)
    in_specs += [pl.BlockSpec(w.shape, const) for w in weights]
    for w, p in zip(weights, posts):
        if p is not None:
            n = w.shape[1]
            args += [jnp.tile(p.gain.astype(F32), n // p.hd).reshape(1, n), _block_diag_ones(n, p.hd)]
            in_specs += [pl.BlockSpec((1, n), const), pl.BlockSpec((n, n), const)]
    out_specs, out_shape = [], []
    for k, (w, dt) in enumerate(zip(weights, out_dtypes)):
        if k in transposed:
            out_specs.append(pl.BlockSpec((w.shape[0], tm), lambda i: (0, i)))
            out_shape.append(jax.ShapeDtypeStruct((w.shape[0], t), dt))
        else:
            out_specs.append(pl.BlockSpec((tm, w.shape[1]), lambda i: (i, 0)))
            out_shape.append(jax.ShapeDtypeStruct((t, w.shape[1]), dt))
    return pl.pallas_call(
        functools.partial(_norm_proj_body, posts=tuple(posts), use_rope=use_rope, transposed=transposed),
        grid=(t // tm,), in_specs=in_specs, out_specs=out_specs, out_shape=out_shape,
        compiler_params=_cparams(("parallel",)), name="norm_proj",
    )(*args)


def _block_diag_ones(n, hd):
    idx = np.arange(n) // hd
    return jnp.asarray((idx[:, None] == idx[None, :]).astype(np.float32), dtype=BF16)


def _rope_tables(seq, hd):
    inv_freq = 1.0 / (ROPE_THETA ** (jnp.arange(0, hd, 2, dtype=F32) / hd))
    ang = jnp.arange(seq, dtype=F32)[:, None] * inv_freq[None, :]
    cos, sin = jnp.cos(ang), jnp.sin(ang)
    reps = LANES // hd
    cos_t = jnp.tile(jnp.concatenate([cos, cos], axis=1), (1, reps))
    sin_t = jnp.tile(jnp.concatenate([-sin, sin], axis=1), (1, reps))
    return cos_t, sin_t


def _flash_body(lam_ref, *refs, n_qk, diff, out_scale, sub):
    q_refs = refs[:n_qk]
    k_refs = refs[n_qk:2 * n_qk]
    vt_ref, gain_ref, o_ref, m_ref, l_ref, acc_ref, s0_ref, s1_ref, p0_ref, p1_ref = refs[2 * n_qk:]
    i = pl.program_id(2)
    n_sm = 2 if diff else 1
    _, tk, tq = s0_ref.shape
    s_slots = (s0_ref, s1_ref)
    p_slots = (p0_ref, p1_ref)

    m_ref[...] = jnp.full(m_ref.shape, NEG_INF, F32)
    l_ref[...] = jnp.zeros(l_ref.shape, F32)
    acc_ref[...] = jnp.zeros(acc_ref.shape, F32)

    qs = [r[0] for r in q_refs]
    q = qs[0] if n_qk == 1 else jnp.concatenate(qs, axis=1)
    if diff:
        lane = lax.broadcasted_iota(jnp.int32, q.shape, 1)
        half = q.shape[1] // 2
        zero = jnp.zeros_like(q)
        q_parts = [jnp.where(lane < half, q, zero), jnp.where(lane >= half, q, zero)]
    else:
        q_parts = [q]

    def scores(c, slot, diagonal=False):
        rows = pl.ds(pl.multiple_of(c * tk, tk), tk)
        ks = [r[0, rows, :] for r in k_refs]
        k = ks[0] if n_qk == 1 else jnp.concatenate(ks, axis=1)
        for sm in range(n_sm):
            s = _dot_nt(k, q_parts[sm])
            if diagonal:
                row = lax.broadcasted_iota(jnp.int32, s.shape, 0)
                col = lax.broadcasted_iota(jnp.int32, s.shape, 1)
                s = jnp.where(row <= col, s, NEG_INF)
            s_slots[slot][sm] = s

    def update(c, slot):
        vt = vt_ref[:, pl.ds(pl.multiple_of(c * tk, tk), tk)]
        for sm in range(n_sm):
            s_ref, p_ref = s_slots[slot], p_slots[slot]
            m_prev = m_ref[sm]
            m_new = jnp.maximum(m_prev, jnp.max(s_ref[sm], axis=0, keepdims=True))
            m_ref[sm] = m_new
            alpha = jnp.exp2(m_prev - m_new)
            lpart = jnp.zeros((SUBLANES, tq), F32)
            for r in range(tk // sub):
                p = jnp.exp2(s_ref[sm, r * sub:(r + 1) * sub, :] - m_new)
                lpart = lpart + _fold_rows(p)
                p_ref[sm, r * sub:(r + 1) * sub, :] = p.astype(BF16)
            l_ref[sm] = alpha * l_ref[sm] + jnp.sum(lpart, axis=0, keepdims=True)
            acc_ref[sm] = alpha * acc_ref[sm] + _dot(vt, p_ref[sm])

    n_chunks = i + 1
    last_past = jnp.maximum(i - 1, 0)

    def chunk_at(j):
        return jnp.where(j == 0, i, j - 1)

    scores(i, 0, diagonal=True)

    def pair(k2, carry):
        scores(jnp.minimum(2 * k2, last_past), 1)
        update(chunk_at(2 * k2), 0)
        scores(jnp.minimum(2 * k2 + 1, last_past), 0)
        update(2 * k2, 1)
        return carry

    lax.fori_loop(0, n_chunks // 2, pair, 0)

    @pl.when(n_chunks % 2 == 1)
    def _():
        update(chunk_at(n_chunks - 1), 0)

    o = acc_ref[0] / l_ref[0]
    if diff:
        o = o - lam_ref[0] * (acc_ref[1] / l_ref[1])
        ms = jnp.mean(o * o, axis=0, keepdims=True)
        o = o * lax.rsqrt(ms + NORM_EPS) * gain_ref[...] * out_scale
    o_ref[0] = o.T.astype(o_ref.dtype)


def flash_attention(lam, qs, ks, vt, gain, n_heads, dv, *, diff, out_scale=1.0, tile=512, sub=64):
    b, s, _ = qs[0].shape
    nt = s // tile
    n_qk = len(qs)
    in_specs = [pl.BlockSpec(memory_space=pltpu.SMEM)]
    for q in qs:
        w = q.shape[2] // n_heads
        in_specs.append(pl.BlockSpec((1, tile, w), lambda bb, h, i: (bb, i, h)))
    for q, k in zip(qs, ks):
        w = q.shape[2] // n_heads
        if k.shape[2] == w:
            in_specs.append(pl.BlockSpec((1, s, w), lambda bb, h, i: (bb, 0, 0)))
        else:
            in_specs.append(pl.BlockSpec((1, s, w), lambda bb, h, i: (bb, 0, h)))
    in_specs.append(pl.BlockSpec((dv, s), lambda bb, h, i: (h, bb)))
    in_specs.append(pl.BlockSpec((dv, 1), lambda bb, h, i: (0, 0)))
    n_sm = 2 if diff else 1
    return pl.pallas_call(
        functools.partial(_flash_body, n_qk=n_qk, diff=diff, out_scale=out_scale, sub=sub),
        grid=(b, n_heads, nt), in_specs=in_specs,
        out_specs=pl.BlockSpec((1, tile, dv), lambda bb, h, i: (bb, i, h)),
        out_shape=jax.ShapeDtypeStruct((b, s, n_heads * dv), BF16),
        scratch_shapes=[pltpu.VMEM((n_sm, 1, tile), F32), pltpu.VMEM((n_sm, 1, tile), F32),
                        pltpu.VMEM((n_sm, dv, tile), F32),
                        pltpu.VMEM((n_sm, tile, tile), F32), pltpu.VMEM((n_sm, tile, tile), F32),
                        pltpu.VMEM((n_sm, tile, tile), BF16), pltpu.VMEM((n_sm, tile, tile), BF16)],
        compiler_params=_cparams(("parallel", "parallel", "arbitrary")),
        name="flash_diff" if diff else "flash_plain",
    )(lam, *qs, *ks, vt, gain.reshape(dv, 1).astype(F32))


def _ssd_body(xbc_ref, z_ref, dt_ref, dtt_ref, cw_ref, cb_ref, dtb_ref, dtbt_ref, al_ref, alt_ref,
              dsk_ref, ng_ref, o_ref, xpad_ref, state_ref):
    chunk = xbc_ref.shape[1]
    d_in = z_ref.shape[2]
    gn = SSM_GROUPS * SSM_STATE
    c = pl.program_id(1)

    @pl.when(c == 0)
    def _():
        xpad_ref[0:8, :] = jnp.zeros((8, xpad_ref.shape[1]), F32)
        state_ref[...] = jnp.zeros(state_ref.shape, F32)

    xpad_ref[8:8 + chunk, :] = xbc_ref[0]
    conv = cb_ref[...]
    for w in range(SSM_CONV):
        conv = conv + cw_ref[w:w + 1, :] * xpad_ref[pl.ds(8 - (SSM_CONV - 1) + w, chunk), :]
    xpad_ref[0:8, :] = xpad_ref[chunk:chunk + 8, :]
    u = _silu(conv)
    xs = u[:, :d_in]
    bmat = u[:, d_in:d_in + gn]
    cmat = u[:, d_in + gn:]

    dt = _softplus(dt_ref[0] + dtb_ref[...])
    ad = dt * (-jnp.exp(al_ref[...]))
    dtt = _softplus(dtt_ref[0] + dtbt_ref[...])
    adt = dtt * (-jnp.exp(alt_ref[...]))
    row = lax.broadcasted_iota(jnp.int32, (chunk, chunk), 0)
    col = lax.broadcasted_iota(jnp.int32, (chunk, chunk), 1)
    lower = row >= col
    tril = jnp.where(lower, 1.0, 0.0).astype(BF16)
    triu = jnp.where(row <= col, 1.0, 0.0).astype(BF16)
    cs = sum(_dot(tril, part) for part in _split_bf16(ad, 3))
    cst = sum(_dot(part, triu) for part in _split_bf16(adt, 3))

    heads_per_group = SSM_HEADS // SSM_GROUPS
    dsk = dsk_ref[...]
    ys = []
    for g in range(SSM_GROUPS):
        bg = bmat[:, g * SSM_STATE:(g + 1) * SSM_STATE]
        cg = cmat[:, g * SSM_STATE:(g + 1) * SSM_STATE].astype(BF16)
        cb = _dot_nt(cg, bg.astype(BF16))
        bgt = bg.T.astype(BF16)
        for r in range(heads_per_group):
            h = g * heads_per_group + r
            ccol = cs[:, h:h + 1]
            crow = cst[h:h + 1, :]
            decay = jnp.exp(jnp.where(lower, ccol - crow, NEG_INF))
            x_h = xs[:, h * SSM_HEAD_DIM:(h + 1) * SSM_HEAD_DIM]
            xdt = x_h * dt[:, h:h + 1]
            y = _dot((cb * decay).astype(BF16), xdt.astype(BF16))
            st = state_ref[h]
            y = y + _dot(cg, st.astype(BF16)) * jnp.exp(ccol)
            last = cst[h:h + 1, chunk - 1:chunk]
            to_end = jnp.exp(last - ccol)
            state_ref[h] = st * jnp.exp(last) + _dot(bgt, (xdt * to_end).astype(BF16))
            ys.append(y + x_h * dsk[:, h * SSM_HEAD_DIM:(h + 1) * SSM_HEAD_DIM])

    y = jnp.concatenate(ys, axis=1) * _silu(z_ref[0])
    gw = d_in // SSM_GROUPS
    for g in range(SSM_GROUPS):
        seg = y[:, g * gw:(g + 1) * gw]
        o_ref[0, :, g * gw:(g + 1) * gw] = _rms(seg, ng_ref[:, g * gw:(g + 1) * gw]).astype(o_ref.dtype)


def ssd_mixer(xbc, z, dt_raw, conv_w, conv_b, dt_bias, a_log, d_skip, norm_gain):
    b, s, cch = xbc.shape
    d_in = z.shape[2]
    nc = s // SSM_CHUNK
    hpad = dt_raw.shape[2]
    dtt = jnp.transpose(dt_raw[:, :, :SSM_HEADS], (0, 2, 1))

    def lane_pad(v):
        return jnp.pad(v.astype(F32), (0, hpad - SSM_HEADS)).reshape(1, hpad)

    args = (xbc, z, dt_raw, dtt, conv_w.astype(F32), conv_b.reshape(1, cch).astype(F32),
            lane_pad(dt_bias), dt_bias.reshape(SSM_HEADS, 1).astype(F32),
            lane_pad(a_log), a_log.reshape(SSM_HEADS, 1).astype(F32),
            jnp.repeat(d_skip.astype(F32), SSM_HEAD_DIM).reshape(1, d_in),
            norm_gain.reshape(1, d_in).astype(F32))
    const = lambda bb, c: (0, 0)
    in_specs = [pl.BlockSpec((1, SSM_CHUNK, cch), lambda bb, c: (bb, c, 0)),
                pl.BlockSpec((1, SSM_CHUNK, d_in), lambda bb, c: (bb, c, 0)),
                pl.BlockSpec((1, SSM_CHUNK, hpad), lambda bb, c: (bb, c, 0)),
                pl.BlockSpec((1, SSM_HEADS, SSM_CHUNK), lambda bb, c: (bb, 0, c)),
                pl.BlockSpec((SSM_CONV, cch), const), pl.BlockSpec((1, cch), const),
                pl.BlockSpec((1, hpad), const), pl.BlockSpec((SSM_HEADS, 1), const),
                pl.BlockSpec((1, hpad), const), pl.BlockSpec((SSM_HEADS, 1), const),
                pl.BlockSpec((1, d_in), const), pl.BlockSpec((1, d_in), const)]
    return pl.pallas_call(
        _ssd_body, grid=(b, nc), in_specs=in_specs,
        out_specs=pl.BlockSpec((1, SSM_CHUNK, d_in), lambda bb, c: (bb, c, 0)),
        out_shape=jax.ShapeDtypeStruct((b, s, d_in), BF16),
        scratch_shapes=[pltpu.VMEM((SSM_CHUNK + 8, cch), F32),
                        pltpu.VMEM((SSM_HEADS, SSM_STATE, SSM_HEAD_DIM), F32)],
        compiler_params=_cparams(("parallel", "arbitrary")), name="ssd_mixer",
    )(*args)


def _out_proj_body(x_ref, a_ref, b_ref, wa_ref, wb_ref, o_ref):
    o_ref[...] = x_ref[...] + _dot(a_ref[...], wa_ref[...]) + _dot(b_ref[...], wb_ref[...])


def out_proj_residual(x2, a, bm, wa, wb, tm=512):
    t, d = x2.shape
    return pl.pallas_call(
        _out_proj_body, grid=(t // tm,),
        in_specs=[pl.BlockSpec((tm, d), lambda i: (i, 0)),
                  pl.BlockSpec((tm, a.shape[1]), lambda i: (i, 0)),
                  pl.BlockSpec((tm, bm.shape[1]), lambda i: (i, 0)),
                  pl.BlockSpec(wa.shape, lambda i: (0, 0)),
                  pl.BlockSpec(wb.shape, lambda i: (0, 0))],
        out_specs=pl.BlockSpec((tm, d), lambda i: (i, 0)),
        out_shape=jax.ShapeDtypeStruct((t, d), F32),
        compiler_params=_cparams(("parallel",)), name="out_proj",
    )(x2, a, bm, wa, wb)


def _ffn_body(x_ref, g_ref, wg_ref, wu_ref, wd_ref, o_ref, h_ref):
    f = pl.program_id(1)

    @pl.when(f == 0)
    def _():
        x = x_ref[...]
        h_ref[...] = _rms(x, g_ref[...]).astype(BF16)
        o_ref[...] = x

    half = h_ref.shape[0] // 2
    for r in (slice(0, half), slice(half, 2 * half)):
        h = h_ref[r, :]
        act = (_silu(_dot(h, wg_ref[...])) * _dot(h, wu_ref[...])).astype(BF16)
        o_ref[r, :] += _dot(act, wd_ref[...])


def ffn_residual(x2, gain, w_gate, w_up, w_down, tm=1024, tf=1408):
    t, d = x2.shape
    d_ff = w_gate.shape[1]
    return pl.pallas_call(
        _ffn_body, grid=(t // tm, d_ff // tf),
        in_specs=[pl.BlockSpec((tm, d), lambda i, f: (i, 0)),
                  pl.BlockSpec((1, d), lambda i, f: (0, 0)),
                  pl.BlockSpec((d, tf), lambda i, f: (0, f)),
                  pl.BlockSpec((d, tf), lambda i, f: (0, f)),
                  pl.BlockSpec((tf, d), lambda i, f: (f, 0))],
        out_specs=pl.BlockSpec((tm, d), lambda i, f: (i, 0)),
        out_shape=jax.ShapeDtypeStruct((t, d), F32),
        scratch_shapes=[pltpu.VMEM((tm, d), BF16)],
        compiler_params=_cparams(("parallel", "arbitrary")), name="ffn",
    )(x2, gain.reshape(1, d).astype(F32), w_gate, w_up, w_down)


MOE_ROWS = 256
ROUTE_IDX = 0
ROUTE_W = 2


def _moe_ffn_body(block_expert_ref, n_used_ref, xs_ref, wg_ref, wu_ref, wd_ref, o_ref):
    i = pl.program_id(0)

    @pl.when(i < n_used_ref[0])
    def _():
        x = xs_ref[...]
        act = (_silu(_dot(x, wg_ref[0])) * _dot(x, wu_ref[0])).astype(BF16)
        o_ref[...] = _dot(act, wd_ref[0]).astype(o_ref.dtype)

    @pl.when(i >= n_used_ref[0])
    def _():
        o_ref[...] = jnp.zeros(o_ref.shape, o_ref.dtype)


def moe_expert_ffn(xs, block_expert, n_used, w_gate, w_up, w_down):
    p, d = xs.shape
    d_ff = w_gate.shape[2]
    rows = MOE_ROWS
    grid_spec = pltpu.PrefetchScalarGridSpec(
        num_scalar_prefetch=2, grid=(p // rows,),
        in_specs=[pl.BlockSpec((rows, d), lambda i, be, nu: (i, 0)),
                  pl.BlockSpec((1, d, d_ff), lambda i, be, nu: (be[i], 0, 0)),
                  pl.BlockSpec((1, d, d_ff), lambda i, be, nu: (be[i], 0, 0)),
                  pl.BlockSpec((1, d_ff, d), lambda i, be, nu: (be[i], 0, 0))],
        out_specs=pl.BlockSpec((rows, d), lambda i, be, nu: (i, 0)))
    return pl.pallas_call(
        _moe_ffn_body, grid_spec=grid_spec, out_shape=jax.ShapeDtypeStruct((p, d), BF16),
        compiler_params=pltpu.CompilerParams(
            dimension_semantics=("arbitrary",), vmem_limit_bytes=MOE_VMEM_LIMIT),
        name="moe_expert_ffn",
    )(block_expert, n_used, xs, w_gate, w_up, w_down)


def _moe_combine_body(x_ref, y0_ref, y1_ref, route_ref, o_ref):
    route = route_ref[...]
    lane = lax.broadcasted_iota(jnp.int32, route.shape, 1)
    w0 = jnp.sum(jnp.where(lane == ROUTE_W, route, 0.0), axis=1, keepdims=True)
    w1 = jnp.sum(jnp.where(lane == ROUTE_W + 1, route, 0.0), axis=1, keepdims=True)
    o_ref[...] = x_ref[...] + w0 * y0_ref[...].astype(F32) + w1 * y1_ref[...].astype(F32)


def moe_combine(x2, y0, y1, route, tm=512):
    t, d = x2.shape
    row = lambda i: (i, 0)
    return pl.pallas_call(
        _moe_combine_body, grid=(t // tm,),
        in_specs=[pl.BlockSpec((tm, d), row), pl.BlockSpec((tm, d), row), pl.BlockSpec((tm, d), row),
                  pl.BlockSpec((tm, LANES), row)],
        out_specs=pl.BlockSpec((tm, d), row),
        out_shape=jax.ShapeDtypeStruct((t, d), F32),
        compiler_params=_cparams(("parallel",)), name="moe_combine",
    )(x2, y0, y1, route)


def moe_residual(x2, h, route, w_gate, w_up, w_down):
    t, d = x2.shape
    n_e = w_gate.shape[0]
    rows = MOE_ROWS
    expert = route[:, ROUTE_IDX:ROUTE_IDX + 2].astype(jnp.int32).reshape(-1)
    onehot = (expert[:, None] == jnp.arange(n_e, dtype=jnp.int32)[None, :]).astype(jnp.int32)
    rank = jnp.take_along_axis(jnp.cumsum(onehot, axis=0), expert[:, None], axis=1)[:, 0] - 1
    counts = jnp.sum(onehot, axis=0)
    padded = (counts + rows - 1) // rows * rows
    ends = jnp.cumsum(padded)
    slot = (ends - padded)[expert] + rank
    p_rows = 2 * t + n_e * rows
    token_of_slot = jnp.zeros((p_rows,), jnp.int32).at[slot].set(
        jnp.arange(2 * t, dtype=jnp.int32) // 2, unique_indices=True)
    block_start = jnp.arange(p_rows // rows, dtype=jnp.int32) * rows
    block_expert = jnp.minimum(jnp.searchsorted(ends, block_start, side="right"), n_e - 1).astype(jnp.int32)
    n_used = (ends[-1] // rows).astype(jnp.int32).reshape(1)

    xs = jnp.take(h, token_of_slot, axis=0)
    ys = moe_expert_ffn(xs, block_expert, n_used, w_gate, w_up, w_down)
    slot2 = slot.reshape(t, 2)
    return moe_combine(x2, jnp.take(ys, slot2[:, 0], axis=0), jnp.take(ys, slot2[:, 1], axis=0), route)


def _router_body(x_ref, g_ref, r_ref, h_ref, o_ref):
    h = _rms(x_ref[...], g_ref[...])
    h_ref[...] = h.astype(h_ref.dtype)
    logits = jnp.dot(h, r_ref[...], precision=lax.Precision.HIGHEST, preferred_element_type=F32)
    lane = lax.broadcasted_iota(jnp.int32, logits.shape, 1).astype(F32)
    low = jnp.float32(-3.0e38)
    logits = jnp.where(lane < N_EXPERTS, logits, low)
    m1 = jnp.max(logits, axis=1, keepdims=True)
    i1 = jnp.min(jnp.where(logits == m1, lane, float(LANES)), axis=1, keepdims=True)
    rest = jnp.where(lane == i1, low, logits)
    m2 = jnp.max(rest, axis=1, keepdims=True)
    i2 = jnp.min(jnp.where(rest == m2, lane, float(LANES)), axis=1, keepdims=True)
    ex = jnp.exp(m2 - m1)
    w1 = 1.0 / (1.0 + ex)
    w2 = ex / (1.0 + ex)
    o_ref[...] = jnp.where(lane == ROUTE_IDX, i1, jnp.where(lane == ROUTE_IDX + 1, i2, jnp.where(
        lane == ROUTE_W, w1, jnp.where(lane == ROUTE_W + 1, w2, 0.0))))


def router(x2, gain, router_w, tm=512):
    t, d = x2.shape
    r_pad = jnp.pad(router_w.astype(F32), ((0, 0), (0, LANES - router_w.shape[1])))
    return pl.pallas_call(
        _router_body, grid=(t // tm,),
        in_specs=[pl.BlockSpec((tm, d), lambda i: (i, 0)),
                  pl.BlockSpec((1, d), lambda i: (0, 0)),
                  pl.BlockSpec((d, LANES), lambda i: (0, 0))],
        out_specs=[pl.BlockSpec((tm, d), lambda i: (i, 0)), pl.BlockSpec((tm, LANES), lambda i: (i, 0))],
        out_shape=[jax.ShapeDtypeStruct((t, d), BF16), jax.ShapeDtypeStruct((t, LANES), F32)],
        compiler_params=_cparams(("parallel",)), name="router",
    )(x2, gain.reshape(1, d).astype(F32), r_pad)


def _compress_body(ch_ref, nx_ref, pos_ref, w1_ref, w2_ref, gain_ref, cos_ref, sin_ref, rot_ref, o_ref,
                   *, is_key):
    a = _dot((ch_ref[0] + pos_ref[0]).astype(BF16), w1_ref[0, 0])
    a = a + _dot((nx_ref[0] + pos_ref[1]).astype(BF16), w1_ref[0, 1])
    out = _dot(_silu(a).astype(BF16), w2_ref[...])
    if is_key:
        out = _rms(out, gain_ref[...])
        hi, lo = _split_bf16(out, 2)
        partner = _dot(hi, rot_ref[...]) + _dot(lo, rot_ref[...])
        out = out * cos_ref[...] + partner * sin_ref[...]
    o_ref[0, 0] = out


def nsa_compress(t, pos, w1, w2, gain, seq, is_key):
    b, s, _ = t.shape
    g, d = NSA_KV_GROUPS, NSA_HEAD_DIM
    n_ch = s // NSA_CMP_STRIDE
    half = NSA_CMP_STRIDE * g * d
    ch = t.reshape(b, n_ch, half)
    nxt = jnp.concatenate([ch[:, 1:], jnp.zeros((b, 1, half), F32)], axis=1)
    pos2 = jnp.broadcast_to(pos.astype(F32).reshape(2, NSA_CMP_STRIDE, 1, d),
                            (2, NSA_CMP_STRIDE, g, d)).reshape(2, 1, half)
    w1r = w1.astype(BF16).reshape(2, NSA_CMP_STRIDE, 1, d, d)
    own = (jnp.arange(g)[:, None] == jnp.arange(g)[None, :]).reshape(g, 1, 1, g, 1, 1)
    w1s = jnp.where(own, w1r[None], jnp.zeros((), BF16)).reshape(g, 2, half, d)
    cmp_end = jnp.arange(n_ch) * NSA_CMP_STRIDE + NSA_CMP_BLOCK - 1
    inv_freq = 1.0 / (ROPE_THETA ** (jnp.arange(0, d, 2, dtype=F32) / d))
    ang = cmp_end.astype(F32)[:, None] * inv_freq[None, :]
    cos = jnp.concatenate([jnp.cos(ang)] * 2, axis=1)
    sin = jnp.concatenate([jnp.sin(ang)] * 2, axis=1)
    rot = np.zeros((d, d), np.float32)
    rot[np.arange(d // 2) + d // 2, np.arange(d // 2)] = -1.0
    rot[np.arange(d // 2), np.arange(d // 2) + d // 2] = 1.0
    blk = lambda bb, gg: (bb, gg, 0, 0)
    seq = lambda bb, gg: (bb, 0, 0)
    c2 = lambda bb, gg: (0, 0)
    c3 = lambda bb, gg: (0, 0, 0)
    return pl.pallas_call(
        functools.partial(_compress_body, is_key=is_key), grid=(b, g),
        in_specs=[pl.BlockSpec((1, n_ch, half), seq), pl.BlockSpec((1, n_ch, half), seq),
                  pl.BlockSpec((2, 1, half), c3), pl.BlockSpec((1, 2, half, d), lambda bb, gg: (gg, 0, 0, 0)),
                  pl.BlockSpec((d, d), c2), pl.BlockSpec((1, d), c2),
                  pl.BlockSpec((n_ch, d), c2), pl.BlockSpec((n_ch, d), c2), pl.BlockSpec((d, d), c2)],
        out_specs=pl.BlockSpec((1, 1, n_ch, d), blk),
        out_shape=jax.ShapeDtypeStruct((b, g, n_ch, d), F32),
        compiler_params=_cparams(("parallel", "parallel")), name="nsa_compress",
    )(ch, nxt, pos2, w1s, w2.astype(BF16), gain.reshape(1, d).astype(F32), cos, sin,
      jnp.asarray(rot, dtype=BF16))


def _nsa_body(q_ref, ck_ref, cvt_ref, ksl_ref, vslt_ref, kwn_ref, vwnt_ref, ovt_ref, glt_ref, o_ref,
              sc_ref, phi_ref, plo_ref, imp_ref, bias_ref, ss_ref, ps_ref, ss1_ref, ps1_ref, sw_ref, pw_ref,
              ow_ref, *, tq):
    g = pl.program_id(1)
    i = pl.program_id(2)
    d = NSA_HEAD_DIM
    rep = NSA_HEADS // NSA_KV_GROUPS
    t0 = i * tq
    n_cmp = ck_ref.shape[1]
    n_sel = ovt_ref.shape[0]
    width = rep * tq
    sub = NSA_SEL_BLOCK
    dead = 0.5 * NEG_INF
    v_rows = pl.ds(pl.multiple_of(g * d, d), d)

    qb = q_ref[0]
    q4 = jnp.concatenate([qb[:, r * d:(r + 1) * d] for r in range(rep)], axis=0)
    q4 = jnp.concatenate([q4, q4], axis=1)
    lane = lax.broadcasted_iota(jnp.int32, q4.shape, 1)
    q4 = jnp.where(jnp.right_shift(lane, d.bit_length() - 1) == g, q4, jnp.zeros_like(q4))

    def qpos_of(shape):
        return t0 + (lax.broadcasted_iota(jnp.int32, shape, 1) & (tq - 1))

    s = _dot_nt(ck_ref[0], q4)
    cmp_end = lax.broadcasted_iota(jnp.int32, s.shape, 0) * NSA_CMP_STRIDE + (NSA_CMP_BLOCK - 1)
    s = jnp.where(cmp_end <= qpos_of(s.shape), s, NEG_INF)
    sc_ref[...] = s
    m_c = jnp.max(s, axis=0, keepdims=True)
    lpart = jnp.zeros((SUBLANES, width), F32)
    for r in range(n_cmp // sub):
        e = jnp.exp2(sc_ref[r * sub:(r + 1) * sub, :] - m_c)
        lpart = lpart + _fold_rows(e)
        hi = e.astype(BF16)
        phi_ref[r * sub:(r + 1) * sub, :] = hi
        plo_ref[r * sub:(r + 1) * sub, :] = (e - hi.astype(F32)).astype(BF16)
    inv_c = jnp.where(m_c > dead, 1.0 / jnp.sum(lpart, axis=0, keepdims=True), 0.0)
    o_c = _dot(cvt_ref[0, v_rows, :], phi_ref[...]) * inv_c

    imp4 = (_dot(ovt_ref[...], phi_ref[...]) + _dot(ovt_ref[...], plo_ref[...])) * inv_c
    imp = imp4[:, 0:tq]
    for r in range(1, rep):
        imp = imp + imp4[:, r * tq:(r + 1) * tq]
    blk = lax.broadcasted_iota(jnp.int32, imp.shape, 0)
    qp = t0 + lax.broadcasted_iota(jnp.int32, imp.shape, 1)
    cur = jnp.right_shift(qp, NSA_SEL_BLOCK.bit_length() - 1)
    forced = (blk == 0) | (blk == cur) | (blk == cur - 1)
    future = blk * NSA_SEL_BLOCK > qp
    imp_ref[...] = jnp.where(future, -FORCE_SCORE, jnp.where(forced, FORCE_SCORE, imp))
    bias_ref[...] = jnp.full(bias_ref.shape, NEG_INF, F32)

    n_live = jnp.minimum((t0 + tq - 1) // NSA_SEL_BLOCK + 1, n_sel)
    n_var = max(n_sel // 32, 1)
    rows_per = n_sel // n_var
    top_n = float(min(NSA_TOP_N, n_sel))
    for v in range(n_var):
        rows = rows_per * (v + 1)

        @pl.when((n_live > rows_per * v) & (n_live <= rows))
        def _():
            mine = imp_ref[0:rows, :]
            blk_r = lax.broadcasted_iota(jnp.int32, mine.shape, 0)

            def count(i2, cnt):
                other = imp_ref[pl.ds(i2, 1), :]
                beats = (other > mine) | ((other == mine) & (blk_r > i2))
                return cnt + jnp.where(beats, 1.0, 0.0)

            rank = lax.fori_loop(0, n_live, count, jnp.zeros(mine.shape, F32))
            bias = jnp.where(rank < top_n, 0.0, NEG_INF)
            bias_ref[0:rows, :] = jnp.concatenate([bias] * rep, axis=1)

    init = (jnp.full((1, width), NEG_INF, F32), jnp.zeros((1, width), F32), jnp.zeros((d, width), F32))

    chunk = 8 * sub
    n_sub = chunk // sub

    s_slots = (ss_ref, ss1_ref)
    p_slots = (ps_ref, ps1_ref)

    def sel_scores(c, slot, diagonal=False):
        start = pl.multiple_of(c * chunk, chunk)
        s = _dot_nt(ksl_ref[0, pl.ds(start, chunk), :], q4)
        if diagonal:
            kpos = start + lax.broadcasted_iota(jnp.int32, s.shape, 0)
            s = jnp.where(kpos <= qpos_of(s.shape), s, NEG_INF)
        s_slots[slot][...] = s

    def sel_update(c, slot, carry):
        m_prev, l_prev, acc = carry
        s_ref, p_ref = s_slots[slot], p_slots[slot]
        biases = [bias_ref[pl.ds(c * n_sub + r, 1), :] for r in range(n_sub)]
        m_new = m_prev
        for r in range(n_sub):
            m_new = jnp.maximum(
                m_new, jnp.max(s_ref[r * sub:(r + 1) * sub, :], axis=0, keepdims=True) + biases[r])
        alpha = jnp.exp2(m_prev - m_new)
        live = m_new > dead
        lpart = jnp.zeros((SUBLANES, width), F32)
        for r in range(n_sub):
            shift = jnp.where(live, biases[r] - m_new, NEG_INF)
            p = jnp.exp2(s_ref[r * sub:(r + 1) * sub, :] + shift)
            lpart = lpart + _fold_rows(p)
            p_ref[r * sub:(r + 1) * sub, :] = p.astype(BF16)
        l_new = alpha * l_prev + jnp.sum(lpart, axis=0, keepdims=True)
        vt = vslt_ref[v_rows, pl.ds(pl.multiple_of(c * chunk, chunk), chunk)]
        return m_new, l_new, alpha * acc + _dot(vt, p_ref[...])

    c_diag = (t0 + tq - 1) // chunk
    n_chunks = c_diag + 1
    last_past = jnp.maximum(c_diag - 1, 0)

    def chunk_at(j):
        return jnp.where(j == 0, c_diag, j - 1)

    sel_scores(c_diag, 0, diagonal=True)

    def pair(k, carry):
        sel_scores(jnp.minimum(2 * k, last_past), 1)
        carry = sel_update(chunk_at(2 * k), 0, carry)
        sel_scores(jnp.minimum(2 * k + 1, last_past), 0)
        return sel_update(2 * k, 1, carry)

    carry = lax.fori_loop(0, n_chunks // 2, pair, init)
    _, l_s, acc_s = lax.cond(n_chunks % 2 == 1, lambda cr: sel_update(chunk_at(n_chunks - 1), 0, cr),
                             lambda cr: cr, carry)
    o_s = acc_s / l_s

    def win_chunk(c, carry):
        m_prev, l_prev, acc = carry
        start = pl.multiple_of(c * tq, tq)
        s = _dot_nt(kwn_ref[0, pl.ds(start, tq), :], q4)
        kpos = start + lax.broadcasted_iota(jnp.int32, s.shape, 0)
        qpos = qpos_of(s.shape)
        s = jnp.where((kpos <= qpos) & (kpos > qpos - NSA_WINDOW), s, NEG_INF)
        m_new = jnp.maximum(m_prev, jnp.max(s, axis=0, keepdims=True))
        alpha = jnp.exp2(m_prev - m_new)
        p = jnp.exp2(s + jnp.where(m_new > dead, -m_new, NEG_INF))
        l_new = alpha * l_prev + jnp.sum(p, axis=0, keepdims=True)
        vt = vwnt_ref[v_rows, pl.ds(start, tq)]
        return m_new, l_new, alpha * acc + _dot(vt, p.astype(BF16))

    n_back = NSA_WINDOW // tq

    @pl.when(i < n_back)
    def _():
        _, l_w, acc_w = lax.fori_loop(0, i + 1, win_chunk, init)
        ow_ref[...] = acc_w / l_w

    @pl.when(i >= n_back)
    def _():
        start = pl.multiple_of(t0 - NSA_WINDOW, tq)
        s = _dot_nt(kwn_ref[0, pl.ds(start, NSA_WINDOW + tq), :], q4)
        kpos = start + lax.broadcasted_iota(jnp.int32, (tq, width), 0)
        qpos = qpos_of((tq, width))
        sw_ref[0:tq, :] = jnp.where(kpos > qpos - NSA_WINDOW, s[0:tq, :], NEG_INF)
        sw_ref[tq:NSA_WINDOW, :] = s[tq:NSA_WINDOW, :]
        sw_ref[NSA_WINDOW:, :] = jnp.where(kpos + NSA_WINDOW <= qpos, s[NSA_WINDOW:, :], NEG_INF)
        m_w = jnp.max(sw_ref[...], axis=0, keepdims=True)
        lpart = jnp.zeros((SUBLANES, width), F32)
        for r in range((NSA_WINDOW + tq) // sub):
            p = jnp.exp2(sw_ref[r * sub:(r + 1) * sub, :] - m_w)
            lpart = lpart + _fold_rows(p)
            pw_ref[r * sub:(r + 1) * sub, :] = p.astype(BF16)
        vt = vwnt_ref[v_rows, pl.ds(start, NSA_WINDOW + tq)]
        ow_ref[...] = _dot(vt, pw_ref[...]) / jnp.sum(lpart, axis=0, keepdims=True)

    o_w = ow_ref[...]

    def gate(branch):
        rows = [glt_ref[pl.ds((g * rep + r) * 3 + branch, 1), :] for r in range(rep)]
        return _sigmoid(jnp.concatenate(rows, axis=1))

    out = gate(0) * o_c + gate(1) * o_s + gate(2) * o_w
    out_t = jnp.concatenate([out, jnp.zeros_like(out)], axis=0).T
    o_ref[0] = jnp.concatenate([out_t[r * tq:(r + 1) * tq, :d] for r in range(rep)],
                               axis=1).astype(o_ref.dtype)


def nsa_overlap_t(n_cmp, n_sel):
    c_start = np.arange(n_cmp)[None, :] * NSA_CMP_STRIDE
    s_start = np.arange(n_sel)[:, None] * NSA_SEL_BLOCK
    hit = (c_start < s_start + NSA_SEL_BLOCK) & (c_start + NSA_CMP_BLOCK > s_start)
    hit = hit & (np.arange(n_cmp)[None, :] < n_cmp - NSA_CMP_BLOCK // NSA_CMP_STRIDE + 1)
    return jnp.asarray(hit.astype(np.float32), dtype=BF16)


def nsa_attention(qn, ck, cvt, ksl, vslt, kwn, vwnt, glt, tq=128):
    b, s, _ = qn.shape
    g, d = NSA_KV_GROUPS, NSA_HEAD_DIM
    rep = NSA_HEADS // g
    n_cmp = ck.shape[1]
    n_sel = s // NSA_SEL_BLOCK
    nq = s // tq
    ovt = nsa_overlap_t(n_cmp, n_sel)
    width = rep * tq
    chunk = 8 * NSA_SEL_BLOCK
    full3 = lambda bb, gg, i: (bb, 0, 0)
    seq_t = lambda bb, gg, i: (0, bb)
    return pl.pallas_call(
        functools.partial(_nsa_body, tq=tq), grid=(b, g, nq),
        in_specs=[pl.BlockSpec((1, tq, rep * d), lambda bb, gg, i: (bb, i, gg)),
                  pl.BlockSpec((1, n_cmp, g * d), full3), pl.BlockSpec((1, g * d, n_cmp), full3),
                  pl.BlockSpec((1, s, g * d), full3), pl.BlockSpec((g * d, s), seq_t),
                  pl.BlockSpec((1, s, g * d), full3), pl.BlockSpec((g * d, s), seq_t),
                  pl.BlockSpec((n_sel, n_cmp), lambda bb, gg, i: (0, 0)),
                  pl.BlockSpec((glt.shape[0], tq), lambda bb, gg, i: (0, bb * nq + i))],
        out_specs=pl.BlockSpec((1, tq, rep * d), lambda bb, gg, i: (bb, i, gg)),
        out_shape=jax.ShapeDtypeStruct((b, s, g * rep * d), BF16),
        scratch_shapes=[pltpu.VMEM((n_cmp, width), F32), pltpu.VMEM((n_cmp, width), BF16),
                        pltpu.VMEM((n_cmp, width), BF16), pltpu.VMEM((n_sel, tq), F32),
                        pltpu.VMEM((n_sel, width), F32), pltpu.VMEM((chunk, width), F32),
                        pltpu.VMEM((chunk, width), BF16), pltpu.VMEM((chunk, width), F32),
                        pltpu.VMEM((chunk, width), BF16), pltpu.VMEM((NSA_WINDOW + tq, width), F32),
                        pltpu.VMEM((NSA_WINDOW + tq, width), BF16), pltpu.VMEM((d, width), F32)],
        compiler_params=_cparams(("parallel", "parallel", "arbitrary")), name="nsa_attention",
    )(qn, ck, cvt, ksl, vslt, kwn, vwnt, ovt, glt)


def _pad_cols(w, n):
    return jnp.pad(w, ((0, 0), (0, n - w.shape[1])))


def _even_layer(x2, b, s, layer_idx, norm_mix, w_in, q_gain, k_gain, lam, subln_gain, conv_w, conv_b,
                dt_bias, a_log, d_skip, ssm_norm_gain, w_out, norm_ffn, w_gate, w_up, w_down):
    nq = DA_HEADS * 2 * DA_HEAD_DIM
    nv = DA_HEADS * DA_V_DIM
    cch = SSM_D_INNER + 2 * SSM_GROUPS * SSM_STATE
    offs = np.cumsum([0, nq, nq, nv, SSM_D_INNER, cch, SSM_HEADS])
    wb = w_in.astype(BF16)
    pieces = [wb[:, offs[k]:offs[k + 1]] for k in range(6)]
    pieces[2] = pieces[2].T
    pieces[5] = _pad_cols(pieces[5], LANES)
    posts = [HeadNorm(q_gain, DA_HEAD_DIM, rope=True, mul=DA_HEAD_DIM ** -0.5 * LOG2E),
             HeadNorm(k_gain, DA_HEAD_DIM, rope=True), None, None, None, None]
    q, k, vt, z, xbc, dt = norm_proj(x2, norm_mix, pieces, [BF16, BF16, BF16, F32, F32, F32], posts, s,
                                     _rope_tables(s, DA_HEAD_DIM), transposed=(2,))
    qn = q.reshape(b, s, nq)
    kn = k.reshape(b, s, nq)
    lam_init = 0.8 - 0.6 * math.exp(-0.3 * layer_idx)
    lf = lam.astype(F32)
    lam_full = jnp.exp(jnp.sum(lf[0] * lf[1])) - jnp.exp(jnp.sum(lf[2] * lf[3])) + lam_init
    a_out = flash_attention(lam_full.reshape(1), [qn], [kn], vt, subln_gain, DA_HEADS, DA_V_DIM,
                            diff=True, out_scale=1.0 - lam_init)
    b_out = ssd_mixer(xbc.reshape(b, s, cch), z.reshape(b, s, SSM_D_INNER), dt.reshape(b, s, LANES),
                      conv_w, conv_b, dt_bias, a_log, d_skip, ssm_norm_gain)
    wo = w_out.astype(BF16)
    x2 = out_proj_residual(x2, a_out.reshape(-1, nv), b_out.reshape(-1, SSM_D_INNER), wo[:nv], wo[nv:])
    return ffn_residual(x2, norm_ffn, w_gate.astype(BF16), w_up.astype(BF16), w_down.astype(BF16))


def _odd_layer(x2, b, s, norm_mix, w_in, q_gain, k_gain, cmp_pos, cmp_w1, cmp_w2, cq_gain, ckv_gain,
               w_uq, w_ukv, qn_gain, qr_gain, kn_gain, kr_gain, w_out, norm_ffn, router_w, w_gate, w_up,
               w_down):
    g, d = NSA_KV_GROUPS, NSA_HEAD_DIM
    nq = NSA_HEADS * d
    nkv = g * d
    sizes = [nq] + [nkv] * 6 + [NSA_HEADS * 3, w_uq.shape[0], w_ukv.shape[0], MLA_ROPE_DIM]
    offs = np.cumsum([0] + sizes)
    wb = w_in.astype(BF16)
    pieces = [wb[:, offs[k]:offs[k + 1]] for k in range(len(sizes))]
    for k in (4, 6):
        pieces[k] = pieces[k].T
    pieces[7] = jnp.pad(pieces[7].T, ((0, 32 - NSA_HEADS * 3), (0, 0)))
    pieces[10] = _pad_cols(pieces[10], LANES)
    tables = _rope_tables(s, d)
    posts = [None] * len(sizes)
    posts[0] = HeadNorm(q_gain, d, rope=True, mul=d ** -0.5 * LOG2E)
    posts[3] = HeadNorm(k_gain[1], d, rope=True)
    posts[5] = HeadNorm(k_gain[2], d, rope=True)
    posts[10] = HeadNorm(kr_gain, MLA_ROPE_DIM, rope=True)
    (q, kc, vc, ksl, vslt, kwn, vwnt, glt, cq, ckv, k_rope) = norm_proj(
        x2, norm_mix, pieces, [BF16, F32, F32, BF16, BF16, BF16, BF16, F32, F32, F32, BF16], posts, s, tables,
        transposed=(4, 6, 7))

    qn = q.reshape(b, s, nq)
    ksl_n = ksl.reshape(b, s, nkv)
    kwn_n = kwn.reshape(b, s, nkv)
    ck = nsa_compress(kc.reshape(b, s, nkv), cmp_pos[0], cmp_w1[0], cmp_w2[0], k_gain[0], s, True)
    cv = nsa_compress(vc.reshape(b, s, nkv), cmp_pos[1], cmp_w1[1], cmp_w2[1], k_gain[0], s, False)
    n_cmp = ck.shape[2]
    ck = ck.transpose(0, 2, 1, 3).reshape(b, n_cmp, nkv).astype(BF16)
    cvt = cv.transpose(0, 1, 3, 2).reshape(b, nkv, n_cmp).astype(BF16)
    c_out = nsa_attention(qn, ck, cvt, ksl_n, vslt, kwn_n, vwnt, glt).reshape(b * s, nq)

    h = MLA_HEADS
    dqk = MLA_NOPE_DIM + MLA_ROPE_DIM
    wq = w_uq.astype(BF16).reshape(-1, h, dqk)
    wq_nope = wq[:, :, :MLA_NOPE_DIM].reshape(-1, h * MLA_NOPE_DIM)
    wq_rope = jnp.pad(wq[:, :, MLA_NOPE_DIM:], ((0, 0), (0, 0), (0, LANES - MLA_ROPE_DIM)))
    wq_rope = wq_rope.reshape(-1, h * LANES)
    wkv = w_ukv.astype(BF16).reshape(-1, h, MLA_NOPE_DIM + MLA_V_DIM)
    wk_nope = wkv[:, :, :MLA_NOPE_DIM].reshape(-1, h * MLA_NOPE_DIM)
    wv = wkv[:, :, MLA_NOPE_DIM:].reshape(-1, h * MLA_V_DIM)
    q_mul = dqk ** -0.5 * LOG2E
    q_nope, q_rope = norm_proj(
        cq, cq_gain, [wq_nope, wq_rope], [BF16, BF16],
        [HeadNorm(qn_gain, MLA_NOPE_DIM, mul=q_mul), HeadNorm(qr_gain, MLA_ROPE_DIM, rope=True, mul=q_mul)],
        s, tables)
    k_nope, vt = norm_proj(ckv, ckv_gain, [wk_nope, wv.T], [BF16, BF16],
                           [HeadNorm(kn_gain, MLA_NOPE_DIM), None], transposed=(1,))
    shp = lambda t: t.reshape(b, s, t.shape[-1])
    d_out = flash_attention(jnp.zeros((1,), F32), [shp(q_nope), shp(q_rope)], [shp(k_nope), shp(k_rope)],
                            vt, jnp.ones((MLA_V_DIM,), F32), h, MLA_V_DIM, diff=False)

    wo = w_out.astype(BF16)
    x2 = out_proj_residual(x2, c_out, d_out.reshape(b * s, h * MLA_V_DIM), wo[:nq], wo[nq:])
    h, route = router(x2, norm_ffn, router_w)
    return moe_residual(x2, h, route, w_gate.astype(BF16), w_up.astype(BF16), w_down.astype(BF16))


def kernel(x, ev_norm_mix, ev_w_in, da_q_gain, da_k_gain, da_lambda, da_subln_gain, ssm_conv_w, ssm_conv_b, ssm_dt_bias, ssm_a_log, ssm_d, ssm_norm_gain, ev_w_out, ev_norm_ffn, ffn_w_gate, ffn_w_up, ffn_w_down, od_norm_mix, od_w_in, nsa_q_gain, nsa_k_gain, nsa_cmp_pos, nsa_cmp_w1, nsa_cmp_w2, mla_cq_gain, mla_ckv_gain, mla_w_uq, mla_w_ukv, mla_qn_gain, mla_qr_gain, mla_kn_gain, mla_kr_gain, od_w_out, od_norm_ffn, moe_router, moe_w_gate, moe_w_up, moe_w_down):
    b, s, d = x.shape
    x2 = x.reshape(b * s, d)
    depth = ev_norm_mix.shape[0] + od_norm_mix.shape[0]
    for layer in range(depth):
        i = layer // 2
        if layer % 2 == 0:
            x2 = _even_layer(x2, b, s, layer, ev_norm_mix[i], ev_w_in[i], da_q_gain[i], da_k_gain[i],
                             da_lambda[i], da_subln_gain[i], ssm_conv_w[i], ssm_conv_b[i],
                             ssm_dt_bias[i], ssm_a_log[i], ssm_d[i], ssm_norm_gain[i], ev_w_out[i],
                             ev_norm_ffn[i], ffn_w_gate[i], ffn_w_up[i], ffn_w_down[i])
        else:
            x2 = _odd_layer(x2, b, s, od_norm_mix[i], od_w_in[i], nsa_q_gain[i], nsa_k_gain[i],
                            nsa_cmp_pos[i], nsa_cmp_w1[i], nsa_cmp_w2[i], mla_cq_gain[i],
                            mla_ckv_gain[i], mla_w_uq[i], mla_w_ukv[i], mla_qn_gain[i], mla_qr_gain[i],
                            mla_kn_gain[i], mla_kr_gain[i], od_w_out[i], od_norm_ffn[i], moe_router[i],
                            moe_w_gate[i], moe_w_up[i], moe_w_down[i])
    return x2.reshape(b, s, d)
```

```python
import functools
import math

import numpy as np
import jax
import jax.numpy as jnp
from jax import lax
from jax.experimental import pallas as pl
from jax.experimental.pallas import tpu as pltpu

F32 = jnp.float32
BF16 = jnp.bfloat16

ROPE_THETA = 10000.0
NORM_EPS = 1e-6
NEG_INF = -1e30
FORCE_SCORE = 1e6
LOG2E = 1.4426950408889634

DA_HEADS = 4
DA_HEAD_DIM = 64
DA_V_DIM = 2 * DA_HEAD_DIM
SSM_HEADS = 8
SSM_HEAD_DIM = 64
SSM_D_INNER = SSM_HEADS * SSM_HEAD_DIM
SSM_GROUPS = 2
SSM_STATE = 128
SSM_CONV = 4
SSM_CHUNK = 256
NSA_HEADS = 8
NSA_KV_GROUPS = 2
NSA_HEAD_DIM = 64
NSA_CMP_BLOCK = 32
NSA_CMP_STRIDE = 16
NSA_SEL_BLOCK = 64
NSA_TOP_N = 16
NSA_WINDOW = 512
MLA_HEADS = 4
MLA_NOPE_DIM = 128
MLA_ROPE_DIM = 64
MLA_V_DIM = 128
N_EXPERTS = 8

LANES = 128
SUBLANES = 8
VMEM_LIMIT = 48 * 1024 * 1024
MOE_VMEM_LIMIT = 58 * 1024 * 1024

NT_DIMS = (((1,), (1,)), ((), ()))


def _cparams(semantics):
    return pltpu.CompilerParams(dimension_semantics=semantics, vmem_limit_bytes=VMEM_LIMIT)


def _dot(a, b):
    return jnp.dot(a, b, preferred_element_type=F32)


def _dot_nt(a, b):
    return lax.dot_general(a, b, NT_DIMS, preferred_element_type=F32)


def _split_bf16(x, parts):
    out = []
    for _ in range(parts):
        hi = x.astype(BF16)
        out.append(hi)
        x = x - hi.astype(F32)
    return out


def _fold_rows(x):
    return jnp.sum(x.reshape(x.shape[0] // SUBLANES, SUBLANES, x.shape[1]), axis=0)


ONES_ROWS = 16


def _with_ones_rows(vt):
    return jnp.concatenate([vt, jnp.ones((ONES_ROWS, vt.shape[1]), vt.dtype)], axis=0)


def _sigmoid(x):
    return 1.0 / (1.0 + jnp.exp(-x))


def _silu(x):
    return x * _sigmoid(x)


def _softplus(x):
    return jnp.maximum(x, 0.0) + jnp.log(1.0 + jnp.exp(-jnp.abs(x)))


def _rms(x, gain):
    ms = jnp.mean(x * x, axis=-1, keepdims=True)
    return x * lax.rsqrt(ms + NORM_EPS) * gain


class HeadNorm:
    def __init__(self, gain, hd, rope=False, mul=1.0):
        self.gain, self.hd, self.rope, self.mul = gain, hd, rope, mul


def _head_norm(y, gain, bd, cos_ref, sin_ref, post):
    n = y.shape[1]
    hd = post.hd
    hi, lo = _split_bf16(y * y, 2)
    ss = _dot(hi, bd) + _dot(lo, bd)
    yn = y * lax.rsqrt(ss * (1.0 / hd) + NORM_EPS) * gain
    if post.rope:
        reps = n // LANES
        cos = jnp.concatenate([cos_ref[...]] * reps, axis=1) if reps > 1 else cos_ref[...]
        sin = jnp.concatenate([sin_ref[...]] * reps, axis=1) if reps > 1 else sin_ref[...]
        lane = lax.broadcasted_iota(jnp.int32, yn.shape, 1)
        first_half = (lane & (hd - 1)) < (hd // 2)
        partner = jnp.where(first_half, pltpu.roll(yn, n - hd // 2, 1), pltpu.roll(yn, hd // 2, 1))
        yn = yn * cos + partner * sin
    if post.mul != 1.0:
        yn = yn * post.mul
    return yn


def _norm_proj_body(x_ref, g_ref, *refs, posts, use_rope, transposed):
    if use_rope:
        cos_ref, sin_ref = refs[0], refs[1]
        refs = refs[2:]
    else:
        cos_ref = sin_ref = None
    n_out = len(posts)
    n_aux = 2 * sum(p is not None for p in posts)
    w_refs, aux, o_refs = refs[:n_out], refs[n_out:n_out + n_aux], refs[n_out + n_aux:]
    h = _rms(x_ref[...], g_ref[...]).astype(BF16)
    a = 0
    for k, (w_ref, o_ref, post) in enumerate(zip(w_refs, o_refs, posts)):
        if k in transposed:
            o_ref[...] = _dot_nt(w_ref[...], h).astype(o_ref.dtype)
            continue
        y = _dot(h, w_ref[...])
        if post is not None:
            y = _head_norm(y, aux[a][...], aux[a + 1][...], cos_ref, sin_ref, post)
            a += 2
        o_ref[...] = y.astype(o_ref.dtype)


def norm_proj(x2, gain, weights, out_dtypes, posts=None, seq=None, rope_tables=None, transposed=(), tm=512):
    t, d = x2.shape
    posts = posts or [None] * len(weights)
    transposed = frozenset(transposed)
    use_rope = any(p is not None and p.rope for p in posts)
    const = lambda i: (0, 0)
    args = [x2, gain.reshape(1, d).astype(F32)]
    in_specs = [pl.BlockSpec((tm, d), lambda i: (i, 0)), pl.BlockSpec((1, d), const)]
    if use_rope:
        per_seq = seq // tm
        args += list(rope_tables)
        in_specs += [pl.BlockSpec((tm, LANES), lambda i: (i % per_seq, 0))] * 2
    args += list(weights)
    in_specs += [pl.BlockSpec(w.shape, const) for w in weights]
    for w, p in zip(weights, posts):
        if p is not None:
            n = w.shape[1]
            args += [jnp.tile(p.gain.astype(F32), n // p.hd).reshape(1, n), _block_diag_ones(n, p.hd)]
            in_specs += [pl.BlockSpec((1, n), const), pl.BlockSpec((n, n), const)]
    out_specs, out_shape = [], []
    for k, (w, dt) in enumerate(zip(weights, out_dtypes)):
        if k in transposed:
            out_specs.append(pl.BlockSpec((w.shape[0], tm), lambda i: (0, i)))
            out_shape.append(jax.ShapeDtypeStruct((w.shape[0], t), dt))
        else:
            out_specs.append(pl.BlockSpec((tm, w.shape[1]), lambda i: (i, 0)))
            out_shape.append(jax.ShapeDtypeStruct((t, w.shape[1]), dt))
    return pl.pallas_call(
        functools.partial(_norm_proj_body, posts=tuple(posts), use_rope=use_rope, transposed=transposed),
        grid=(t // tm,), in_specs=in_specs, out_specs=out_specs, out_shape=out_shape,
        compiler_params=_cparams(("parallel",)), name="norm_proj",
    )(*args)


def _block_diag_ones(n, hd):
    idx = np.arange(n) // hd
    return jnp.asarray((idx[:, None] == idx[None, :]).astype(np.float32), dtype=BF16)


def _rope_tables(seq, hd):
    inv_freq = 1.0 / (ROPE_THETA ** (jnp.arange(0, hd, 2, dtype=F32) / hd))
    ang = jnp.arange(seq, dtype=F32)[:, None] * inv_freq[None, :]
    cos, sin = jnp.cos(ang), jnp.sin(ang)
    reps = LANES // hd
    cos_t = jnp.tile(jnp.concatenate([cos, cos], axis=1), (1, reps))
    sin_t = jnp.tile(jnp.concatenate([-sin, sin], axis=1), (1, reps))
    return cos_t, sin_t


def _flash_body(lam_ref, *refs, n_qk, diff, out_scale, sub):
    q_refs = refs[:n_qk]
    k_refs = refs[n_qk:2 * n_qk]
    vt_ref, gain_ref, o_ref, m_ref, l_ref, acc_ref, s0_ref, s1_ref, p0_ref, p1_ref = refs[2 * n_qk:]
    i = pl.program_id(2)
    n_sm = 2 if diff else 1
    _, tk, tq = s0_ref.shape
    s_slots = (s0_ref, s1_ref)
    p_slots = (p0_ref, p1_ref)

    m_ref[...] = jnp.full(m_ref.shape, NEG_INF, F32)
    l_ref[...] = jnp.zeros(l_ref.shape, F32)
    acc_ref[...] = jnp.zeros(acc_ref.shape, F32)

    qs = [r[0] for r in q_refs]
    q = qs[0] if n_qk == 1 else jnp.concatenate(qs, axis=1)
    if diff:
        lane = lax.broadcasted_iota(jnp.int32, q.shape, 1)
        half = q.shape[1] // 2
        zero = jnp.zeros_like(q)
        q_parts = [jnp.where(lane < half, q, zero), jnp.where(lane >= half, q, zero)]
    else:
        q_parts = [q]

    def scores(c, slot, diagonal=False):
        rows = pl.ds(pl.multiple_of(c * tk, tk), tk)
        ks = [r[0, rows, :] for r in k_refs]
        k = ks[0] if n_qk == 1 else jnp.concatenate(ks, axis=1)
        for sm in range(n_sm):
            s = _dot_nt(k, q_parts[sm])
            if diagonal:
                row = lax.broadcasted_iota(jnp.int32, s.shape, 0)
                col = lax.broadcasted_iota(jnp.int32, s.shape, 1)
                s = jnp.where(row <= col, s, NEG_INF)
            s_slots[slot][sm] = s

    def update(c, slot):
        vt = vt_ref[:, pl.ds(pl.multiple_of(c * tk, tk), tk)]
        for sm in range(n_sm):
            s_ref, p_ref = s_slots[slot], p_slots[slot]
            m_prev = m_ref[sm]
            m_new = jnp.maximum(m_prev, jnp.max(s_ref[sm], axis=0, keepdims=True))
            m_ref[sm] = m_new
            alpha = jnp.exp2(m_prev - m_new)
            lpart = jnp.zeros((SUBLANES, tq), F32)
            for r in range(tk // sub):
                p = jnp.exp2(s_ref[sm, r * sub:(r + 1) * sub, :] - m_new)
                lpart = lpart + _fold_rows(p)
                p_ref[sm, r * sub:(r + 1) * sub, :] = p.astype(BF16)
            l_ref[sm] = alpha * l_ref[sm] + jnp.sum(lpart, axis=0, keepdims=True)
            acc_ref[sm] = alpha * acc_ref[sm] + _dot(vt, p_ref[sm])

    n_chunks = i + 1
    last_past = jnp.maximum(i - 1, 0)

    def chunk_at(j):
        return jnp.where(j == 0, i, j - 1)

    scores(i, 0, diagonal=True)

    def pair(k2, carry):
        scores(jnp.minimum(2 * k2, last_past), 1)
        update(chunk_at(2 * k2), 0)
        scores(jnp.minimum(2 * k2 + 1, last_past), 0)
        update(2 * k2, 1)
        return carry

    lax.fori_loop(0, n_chunks // 2, pair, 0)

    @pl.when(n_chunks % 2 == 1)
    def _():
        update(chunk_at(n_chunks - 1), 0)

    o = acc_ref[0] / l_ref[0]
    if diff:
        o = o - lam_ref[0] * (acc_ref[1] / l_ref[1])
        ms = jnp.mean(o * o, axis=0, keepdims=True)
        o = o * lax.rsqrt(ms + NORM_EPS) * gain_ref[...] * out_scale
    o_ref[0] = o.T.astype(o_ref.dtype)


def flash_attention(lam, qs, ks, vt, gain, n_heads, dv, *, diff, out_scale=1.0, tile=512, sub=64):
    b, s, _ = qs[0].shape
    nt = s // tile
    n_qk = len(qs)
    in_specs = [pl.BlockSpec(memory_space=pltpu.SMEM)]
    for q in qs:
        w = q.shape[2] // n_heads
        in_specs.append(pl.BlockSpec((1, tile, w), lambda bb, h, i: (bb, i, h)))
    for q, k in zip(qs, ks):
        w = q.shape[2] // n_heads
        if k.shape[2] == w:
            in_specs.append(pl.BlockSpec((1, s, w), lambda bb, h, i: (bb, 0, 0)))
        else:
            in_specs.append(pl.BlockSpec((1, s, w), lambda bb, h, i: (bb, 0, h)))
    in_specs.append(pl.BlockSpec((dv, s), lambda bb, h, i: (h, bb)))
    in_specs.append(pl.BlockSpec((dv, 1), lambda bb, h, i: (0, 0)))
    n_sm = 2 if diff else 1
    return pl.pallas_call(
        functools.partial(_flash_body, n_qk=n_qk, diff=diff, out_scale=out_scale, sub=sub),
        grid=(b, n_heads, nt), in_specs=in_specs,
        out_specs=pl.BlockSpec((1, tile, dv), lambda bb, h, i: (bb, i, h)),
        out_shape=jax.ShapeDtypeStruct((b, s, n_heads * dv), BF16),
        scratch_shapes=[pltpu.VMEM((n_sm, 1, tile), F32), pltpu.VMEM((n_sm, 1, tile), F32),
                        pltpu.VMEM((n_sm, dv, tile), F32),
                        pltpu.VMEM((n_sm, tile, tile), F32), pltpu.VMEM((n_sm, tile, tile), F32),
                        pltpu.VMEM((n_sm, tile, tile), BF16), pltpu.VMEM((n_sm, tile, tile), BF16)],
        compiler_params=_cparams(("parallel", "parallel", "arbitrary")),
        name="flash_diff" if diff else "flash_plain",
    )(lam, *qs, *ks, vt, gain.reshape(dv, 1).astype(F32))


def _ssd_body(xbc_ref, z_ref, dt_ref, dtt_ref, cw_ref, cb_ref, dtb_ref, dtbt_ref, al_ref, alt_ref,
              dsk_ref, ng_ref, o_ref, xpad_ref, state_ref):
    chunk = xbc_ref.shape[1]
    d_in = z_ref.shape[2]
    gn = SSM_GROUPS * SSM_STATE
    c = pl.program_id(1)

    @pl.when(c == 0)
    def _():
        xpad_ref[0:8, :] = jnp.zeros((8, xpad_ref.shape[1]), F32)
        state_ref[...] = jnp.zeros(state_ref.shape, F32)

    xpad_ref[8:8 + chunk, :] = xbc_ref[0]
    conv = cb_ref[...]
    for w in range(SSM_CONV):
        conv = conv + cw_ref[w:w + 1, :] * xpad_ref[pl.ds(8 - (SSM_CONV - 1) + w, chunk), :]
    xpad_ref[0:8, :] = xpad_ref[chunk:chunk + 8, :]
    u = _silu(conv)
    xs = u[:, :d_in]
    bmat = u[:, d_in:d_in + gn]
    cmat = u[:, d_in + gn:]

    dt = _softplus(dt_ref[0] + dtb_ref[...])
    ad = dt * (-jnp.exp(al_ref[...]))
    dtt = _softplus(dtt_ref[0] + dtbt_ref[...])
    adt = dtt * (-jnp.exp(alt_ref[...]))
    row = lax.broadcasted_iota(jnp.int32, (chunk, chunk), 0)
    col = lax.broadcasted_iota(jnp.int32, (chunk, chunk), 1)
    lower = row >= col
    tril = jnp.where(lower, 1.0, 0.0).astype(BF16)
    triu = jnp.where(row <= col, 1.0, 0.0).astype(BF16)
    cs = sum(_dot(tril, part) for part in _split_bf16(ad, 3))
    cst = sum(_dot(part, triu) for part in _split_bf16(adt, 3))

    heads_per_group = SSM_HEADS // SSM_GROUPS
    dsk = dsk_ref[...]
    ys = []
    for g in range(SSM_GROUPS):
        bg = bmat[:, g * SSM_STATE:(g + 1) * SSM_STATE]
        cg = cmat[:, g * SSM_STATE:(g + 1) * SSM_STATE].astype(BF16)
        cb = _dot_nt(cg, bg.astype(BF16))
        bgt = bg.T.astype(BF16)
        for r in range(heads_per_group):
            h = g * heads_per_group + r
            ccol = cs[:, h:h + 1]
            crow = cst[h:h + 1, :]
            decay = jnp.exp(jnp.where(lower, ccol - crow, NEG_INF))
            x_h = xs[:, h * SSM_HEAD_DIM:(h + 1) * SSM_HEAD_DIM]
            xdt = x_h * dt[:, h:h + 1]
            y = _dot((cb * decay).astype(BF16), xdt.astype(BF16))
            st = state_ref[h]
            y = y + _dot(cg, st.astype(BF16)) * jnp.exp(ccol)
            last = cst[h:h + 1, chunk - 1:chunk]
            to_end = jnp.exp(last - ccol)
            state_ref[h] = st * jnp.exp(last) + _dot(bgt, (xdt * to_end).astype(BF16))
            ys.append(y + x_h * dsk[:, h * SSM_HEAD_DIM:(h + 1) * SSM_HEAD_DIM])

    y = jnp.concatenate(ys, axis=1) * _silu(z_ref[0])
    gw = d_in // SSM_GROUPS
    for g in range(SSM_GROUPS):
        seg = y[:, g * gw:(g + 1) * gw]
        o_ref[0, :, g * gw:(g + 1) * gw] = _rms(seg, ng_ref[:, g * gw:(g + 1) * gw]).astype(o_ref.dtype)


def ssd_mixer(xbc, z, dt_raw, conv_w, conv_b, dt_bias, a_log, d_skip, norm_gain):
    b, s, cch = xbc.shape
    d_in = z.shape[2]
    nc = s // SSM_CHUNK
    hpad = dt_raw.shape[2]
    dtt = jnp.transpose(dt_raw[:, :, :SSM_HEADS], (0, 2, 1))

    def lane_pad(v):
        return jnp.pad(v.astype(F32), (0, hpad - SSM_HEADS)).reshape(1, hpad)

    args = (xbc, z, dt_raw, dtt, conv_w.astype(F32), conv_b.reshape(1, cch).astype(F32),
            lane_pad(dt_bias), dt_bias.reshape(SSM_HEADS, 1).astype(F32),
            lane_pad(a_log), a_log.reshape(SSM_HEADS, 1).astype(F32),
            jnp.repeat(d_skip.astype(F32), SSM_HEAD_DIM).reshape(1, d_in),
            norm_gain.reshape(1, d_in).astype(F32))
    const = lambda bb, c: (0, 0)
    in_specs = [pl.BlockSpec((1, SSM_CHUNK, cch), lambda bb, c: (bb, c, 0)),
                pl.BlockSpec((1, SSM_CHUNK, d_in), lambda bb, c: (bb, c, 0)),
                pl.BlockSpec((1, SSM_CHUNK, hpad), lambda bb, c: (bb, c, 0)),
                pl.BlockSpec((1, SSM_HEADS, SSM_CHUNK), lambda bb, c: (bb, 0, c)),
                pl.BlockSpec((SSM_CONV, cch), const), pl.BlockSpec((1, cch), const),
                pl.BlockSpec((1, hpad), const), pl.BlockSpec((SSM_HEADS, 1), const),
                pl.BlockSpec((1, hpad), const), pl.BlockSpec((SSM_HEADS, 1), const),
                pl.BlockSpec((1, d_in), const), pl.BlockSpec((1, d_in), const)]
    return pl.pallas_call(
        _ssd_body, grid=(b, nc), in_specs=in_specs,
        out_specs=pl.BlockSpec((1, SSM_CHUNK, d_in), lambda bb, c: (bb, c, 0)),
        out_shape=jax.ShapeDtypeStruct((b, s, d_in), BF16),
        scratch_shapes=[pltpu.VMEM((SSM_CHUNK + 8, cch), F32),
                        pltpu.VMEM((SSM_HEADS, SSM_STATE, SSM_HEAD_DIM), F32)],
        compiler_params=_cparams(("parallel", "arbitrary")), name="ssd_mixer",
    )(*args)


def _out_proj_body(x_ref, a_ref, b_ref, wa_ref, wb_ref, o_ref):
    o_ref[...] = x_ref[...] + _dot(a_ref[...], wa_ref[...]) + _dot(b_ref[...], wb_ref[...])


def out_proj_residual(x2, a, bm, wa, wb, tm=512):
    t, d = x2.shape
    return pl.pallas_call(
        _out_proj_body, grid=(t // tm,),
        in_specs=[pl.BlockSpec((tm, d), lambda i: (i, 0)),
                  pl.BlockSpec((tm, a.shape[1]), lambda i: (i, 0)),
                  pl.BlockSpec((tm, bm.shape[1]), lambda i: (i, 0)),
                  pl.BlockSpec(wa.shape, lambda i: (0, 0)),
                  pl.BlockSpec(wb.shape, lambda i: (0, 0))],
        out_specs=pl.BlockSpec((tm, d), lambda i: (i, 0)),
        out_shape=jax.ShapeDtypeStruct((t, d), F32),
        compiler_params=_cparams(("parallel",)), name="out_proj",
    )(x2, a, bm, wa, wb)


def _ffn_body(x_ref, g_ref, wg_ref, wu_ref, wd_ref, o_ref, h_ref):
    f = pl.program_id(1)

    @pl.when(f == 0)
    def _():
        x = x_ref[...]
        h_ref[...] = _rms(x, g_ref[...]).astype(BF16)
        o_ref[...] = x

    half = h_ref.shape[0] // 2
    for r in (slice(0, half), slice(half, 2 * half)):
        h = h_ref[r, :]
        act = (_silu(_dot(h, wg_ref[...])) * _dot(h, wu_ref[...])).astype(BF16)
        o_ref[r, :] += _dot(act, wd_ref[...])


def ffn_residual(x2, gain, w_gate, w_up, w_down, tm=1024, tf=1408):
    t, d = x2.shape
    d_ff = w_gate.shape[1]
    return pl.pallas_call(
        _ffn_body, grid=(t // tm, d_ff // tf),
        in_specs=[pl.BlockSpec((tm, d), lambda i, f: (i, 0)),
                  pl.BlockSpec((1, d), lambda i, f: (0, 0)),
                  pl.BlockSpec((d, tf), lambda i, f: (0, f)),
                  pl.BlockSpec((d, tf), lambda i, f: (0, f)),
                  pl.BlockSpec((tf, d), lambda i, f: (f, 0))],
        out_specs=pl.BlockSpec((tm, d), lambda i, f: (i, 0)),
        out_shape=jax.ShapeDtypeStruct((t, d), F32),
        scratch_shapes=[pltpu.VMEM((tm, d), BF16)],
        compiler_params=_cparams(("parallel", "arbitrary")), name="ffn",
    )(x2, gain.reshape(1, d).astype(F32), w_gate, w_up, w_down)


MOE_ROWS = 256
ROUTE_IDX = 0
ROUTE_W = 2
ROUTE_RANK = 4


def _moe_ffn_body(block_expert_ref, n_used_ref, xs_ref, wg_ref, wu_ref, wd_ref, o_ref):
    i = pl.program_id(0)

    @pl.when(i < n_used_ref[0])
    def _():
        x = xs_ref[...]
        act = (_silu(_dot(x, wg_ref[0])) * _dot(x, wu_ref[0])).astype(BF16)
        o_ref[...] = _dot(act, wd_ref[0]).astype(o_ref.dtype)

    @pl.when(i >= n_used_ref[0])
    def _():
        o_ref[...] = jnp.zeros(o_ref.shape, o_ref.dtype)


def moe_expert_ffn(xs, block_expert, n_used, w_gate, w_up, w_down):
    p, d = xs.shape
    d_ff = w_gate.shape[2]
    rows = MOE_ROWS
    grid_spec = pltpu.PrefetchScalarGridSpec(
        num_scalar_prefetch=2, grid=(p // rows,),
        in_specs=[pl.BlockSpec((rows, d), lambda i, be, nu: (i, 0)),
                  pl.BlockSpec((1, d, d_ff), lambda i, be, nu: (be[i], 0, 0)),
                  pl.BlockSpec((1, d, d_ff), lambda i, be, nu: (be[i], 0, 0)),
                  pl.BlockSpec((1, d_ff, d), lambda i, be, nu: (be[i], 0, 0))],
        out_specs=pl.BlockSpec((rows, d), lambda i, be, nu: (i, 0)))
    return pl.pallas_call(
        _moe_ffn_body, grid_spec=grid_spec, out_shape=jax.ShapeDtypeStruct((p, d), BF16),
        compiler_params=pltpu.CompilerParams(
            dimension_semantics=("arbitrary",), vmem_limit_bytes=MOE_VMEM_LIMIT),
        name="moe_expert_ffn",
    )(block_expert, n_used, xs, w_gate, w_up, w_down)


def _moe_combine_body(x_ref, y0_ref, y1_ref, route_ref, o_ref):
    route = route_ref[...]
    lane = lax.broadcasted_iota(jnp.int32, route.shape, 1)
    w0 = jnp.sum(jnp.where(lane == ROUTE_W, route, 0.0), axis=1, keepdims=True)
    w1 = jnp.sum(jnp.where(lane == ROUTE_W + 1, route, 0.0), axis=1, keepdims=True)
    o_ref[...] = x_ref[...] + w0 * y0_ref[...].astype(F32) + w1 * y1_ref[...].astype(F32)


def moe_combine(x2, y0, y1, route, tm=512):
    t, d = x2.shape
    row = lambda i: (i, 0)
    return pl.pallas_call(
        _moe_combine_body, grid=(t // tm,),
        in_specs=[pl.BlockSpec((tm, d), row), pl.BlockSpec((tm, d), row), pl.BlockSpec((tm, d), row),
                  pl.BlockSpec((tm, LANES), row)],
        out_specs=pl.BlockSpec((tm, d), row),
        out_shape=jax.ShapeDtypeStruct((t, d), F32),
        compiler_params=_cparams(("parallel",)), name="moe_combine",
    )(x2, y0, y1, route)


def moe_residual(x2, h, route, counts, w_gate, w_up, w_down):
    t, d = x2.shape
    n_e = w_gate.shape[0]
    rows = MOE_ROWS
    expert = route[:, ROUTE_IDX:ROUTE_IDX + 2].astype(jnp.int32).reshape(-1)
    rank = route[:, ROUTE_RANK:ROUTE_RANK + 2].astype(jnp.int32).reshape(-1)
    padded = (counts[0, :n_e].astype(jnp.int32) + rows - 1) // rows * rows
    ends = jnp.cumsum(padded)
    own = expert[:, None] == jnp.arange(n_e, dtype=jnp.int32)[None, :]
    slot = jnp.sum(jnp.where(own, (ends - padded)[None, :], 0), axis=1) + rank
    p_rows = 2 * t + n_e * rows
    token_of_slot = jnp.zeros((p_rows,), jnp.int32).at[slot].set(
        jnp.arange(2 * t, dtype=jnp.int32) // 2, unique_indices=True)
    block_start = jnp.arange(p_rows // rows, dtype=jnp.int32) * rows
    block_expert = jnp.minimum(jnp.searchsorted(ends, block_start, side="right"), n_e - 1).astype(jnp.int32)
    n_used = (ends[-1] // rows).astype(jnp.int32).reshape(1)

    xs = jnp.take(h, token_of_slot, axis=0)
    ys = moe_expert_ffn(xs, block_expert, n_used, w_gate, w_up, w_down)
    slot2 = slot.reshape(t, 2)
    return moe_combine(x2, jnp.take(ys, slot2[:, 0], axis=0), jnp.take(ys, slot2[:, 1], axis=0), route)


def _router_body(x_ref, g_ref, r_ref, h_ref, o_ref, count_ref, run_ref):
    @pl.when(pl.program_id(0) == 0)
    def _():
        run_ref[...] = jnp.zeros(run_ref.shape, F32)

    h = _rms(x_ref[...], g_ref[...])
    h_ref[...] = h.astype(h_ref.dtype)
    logits = jnp.dot(h, r_ref[...], precision=lax.Precision.HIGHEST, preferred_element_type=F32)
    lane = lax.broadcasted_iota(jnp.int32, logits.shape, 1).astype(F32)
    low = jnp.float32(-3.0e38)
    logits = jnp.where(lane < N_EXPERTS, logits, low)
    m1 = jnp.max(logits, axis=1, keepdims=True)
    i1 = jnp.min(jnp.where(logits == m1, lane, float(LANES)), axis=1, keepdims=True)
    rest = jnp.where(lane == i1, low, logits)
    m2 = jnp.max(rest, axis=1, keepdims=True)
    i2 = jnp.min(jnp.where(rest == m2, lane, float(LANES)), axis=1, keepdims=True)
    ex = jnp.exp(m2 - m1)
    w1 = 1.0 / (1.0 + ex)
    w2 = ex / (1.0 + ex)
    tm = logits.shape[0]
    routed = jnp.where((lane == i1) | (lane == i2), 1.0, 0.0)
    row = lax.broadcasted_iota(jnp.int32, (tm, tm), 0)
    col = lax.broadcasted_iota(jnp.int32, (tm, tm), 1)
    before = jnp.where(col < row, 1.0, 0.0).astype(BF16)
    rank = run_ref[0:1, :] + _dot(before, routed.astype(BF16))
    r1 = jnp.sum(jnp.where(lane == i1, rank, 0.0), axis=1, keepdims=True)
    r2 = jnp.sum(jnp.where(lane == i2, rank, 0.0), axis=1, keepdims=True)
    fields = ((ROUTE_IDX, i1), (ROUTE_IDX + 1, i2), (ROUTE_W, w1), (ROUTE_W + 1, w2),
              (ROUTE_RANK, r1), (ROUTE_RANK + 1, r2))
    out = jnp.zeros(logits.shape, F32)
    for pos, val in fields:
        out = jnp.where(lane == pos, val, out)
    o_ref[...] = out
    run_ref[...] = run_ref[...] + jnp.sum(routed, axis=0, keepdims=True)
    count_ref[...] = run_ref[...]


def router(x2, gain, router_w, tm=512):
    t, d = x2.shape
    r_pad = jnp.pad(router_w.astype(F32), ((0, 0), (0, LANES - router_w.shape[1])))
    return pl.pallas_call(
        _router_body, grid=(t // tm,),
        in_specs=[pl.BlockSpec((tm, d), lambda i: (i, 0)),
                  pl.BlockSpec((1, d), lambda i: (0, 0)),
                  pl.BlockSpec((d, LANES), lambda i: (0, 0))],
        out_specs=[pl.BlockSpec((tm, d), lambda i: (i, 0)), pl.BlockSpec((tm, LANES), lambda i: (i, 0)),
                   pl.BlockSpec((SUBLANES, LANES), lambda i: (0, 0))],
        out_shape=[jax.ShapeDtypeStruct((t, d), BF16), jax.ShapeDtypeStruct((t, LANES), F32),
                   jax.ShapeDtypeStruct((SUBLANES, LANES), F32)],
        scratch_shapes=[pltpu.VMEM((SUBLANES, LANES), F32)],
        compiler_params=_cparams(("arbitrary",)), name="router",
    )(x2, gain.reshape(1, d).astype(F32), r_pad)


def _compress_body(ch_ref, nx_ref, pos_ref, w1_ref, w2_ref, gain_ref, cos_ref, sin_ref, rot_ref, o_ref,
                   *, is_key):
    a = _dot((ch_ref[0] + pos_ref[0]).astype(BF16), w1_ref[0, 0])
    a = a + _dot((nx_ref[0] + pos_ref[1]).astype(BF16), w1_ref[0, 1])
    out = _dot(_silu(a).astype(BF16), w2_ref[...])
    if is_key:
        out = _rms(out, gain_ref[...])
        hi, lo = _split_bf16(out, 2)
        partner = _dot(hi, rot_ref[...]) + _dot(lo, rot_ref[...])
        out = out * cos_ref[...] + partner * sin_ref[...]
    o_ref[0, 0] = out


def nsa_compress(t, pos, w1, w2, gain, seq, is_key):
    b, s, _ = t.shape
    g, d = NSA_KV_GROUPS, NSA_HEAD_DIM
    n_ch = s // NSA_CMP_STRIDE
    half = NSA_CMP_STRIDE * g * d
    ch = t.reshape(b, n_ch, half)
    nxt = jnp.concatenate([ch[:, 1:], jnp.zeros((b, 1, half), F32)], axis=1)
    pos2 = jnp.broadcast_to(pos.astype(F32).reshape(2, NSA_CMP_STRIDE, 1, d),
                            (2, NSA_CMP_STRIDE, g, d)).reshape(2, 1, half)
    w1r = w1.astype(BF16).reshape(2, NSA_CMP_STRIDE, 1, d, d)
    own = (jnp.arange(g)[:, None] == jnp.arange(g)[None, :]).reshape(g, 1, 1, g, 1, 1)
    w1s = jnp.where(own, w1r[None], jnp.zeros((), BF16)).reshape(g, 2, half, d)
    cmp_end = jnp.arange(n_ch) * NSA_CMP_STRIDE + NSA_CMP_BLOCK - 1
    inv_freq = 1.0 / (ROPE_THETA ** (jnp.arange(0, d, 2, dtype=F32) / d))
    ang = cmp_end.astype(F32)[:, None] * inv_freq[None, :]
    cos = jnp.concatenate([jnp.cos(ang)] * 2, axis=1)
    sin = jnp.concatenate([jnp.sin(ang)] * 2, axis=1)
    rot = np.zeros((d, d), np.float32)
    rot[np.arange(d // 2) + d // 2, np.arange(d // 2)] = -1.0
    rot[np.arange(d // 2), np.arange(d // 2) + d // 2] = 1.0
    blk = lambda bb, gg: (bb, gg, 0, 0)
    seq = lambda bb, gg: (bb, 0, 0)
    c2 = lambda bb, gg: (0, 0)
    c3 = lambda bb, gg: (0, 0, 0)
    return pl.pallas_call(
        functools.partial(_compress_body, is_key=is_key), grid=(b, g),
        in_specs=[pl.BlockSpec((1, n_ch, half), seq), pl.BlockSpec((1, n_ch, half), seq),
                  pl.BlockSpec((2, 1, half), c3), pl.BlockSpec((1, 2, half, d), lambda bb, gg: (gg, 0, 0, 0)),
                  pl.BlockSpec((d, d), c2), pl.BlockSpec((1, d), c2),
                  pl.BlockSpec((n_ch, d), c2), pl.BlockSpec((n_ch, d), c2), pl.BlockSpec((d, d), c2)],
        out_specs=pl.BlockSpec((1, 1, n_ch, d), blk),
        out_shape=jax.ShapeDtypeStruct((b, g, n_ch, d), F32),
        compiler_params=_cparams(("parallel", "parallel")), name="nsa_compress",
    )(ch, nxt, pos2, w1s, w2.astype(BF16), gain.reshape(1, d).astype(F32), cos, sin,
      jnp.asarray(rot, dtype=BF16))


def _nsa_body(q_ref, ck_ref, cvt_ref, ksl_ref, vslt_ref, kwn_ref, vwnt_ref, ovt_ref, glt_ref, o_ref,
              sc_ref, phi_ref, plo_ref, imp_ref, bias_ref, ss_ref, ps_ref, ss1_ref, ps1_ref, sw_ref, pw_ref,
              ow_ref, *, tq):
    g = pl.program_id(1)
    i = pl.program_id(2)
    d = NSA_HEAD_DIM
    rep = NSA_HEADS // NSA_KV_GROUPS
    t0 = i * tq
    n_cmp = ck_ref.shape[1]
    n_sel = ovt_ref.shape[0]
    width = rep * tq
    sub = NSA_SEL_BLOCK
    dead = 0.5 * NEG_INF
    v_rows = pl.ds(pl.multiple_of(g * d, d), d)

    qb = q_ref[0]
    q4 = jnp.concatenate([qb[:, r * d:(r + 1) * d] for r in range(rep)], axis=0)
    q4 = jnp.concatenate([q4, q4], axis=1)
    lane = lax.broadcasted_iota(jnp.int32, q4.shape, 1)
    q4 = jnp.where(jnp.right_shift(lane, d.bit_length() - 1) == g, q4, jnp.zeros_like(q4))

    def qpos_of(shape):
        return t0 + (lax.broadcasted_iota(jnp.int32, shape, 1) & (tq - 1))

    s = _dot_nt(ck_ref[0], q4)
    cmp_end = lax.broadcasted_iota(jnp.int32, s.shape, 0) * NSA_CMP_STRIDE + (NSA_CMP_BLOCK - 1)
    s = jnp.where(cmp_end <= qpos_of(s.shape), s, NEG_INF)
    sc_ref[...] = s
    m_c = jnp.max(s, axis=0, keepdims=True)
    lpart = jnp.zeros((SUBLANES, width), F32)
    for r in range(n_cmp // sub):
        e = jnp.exp2(sc_ref[r * sub:(r + 1) * sub, :] - m_c)
        lpart = lpart + _fold_rows(e)
        hi = e.astype(BF16)
        phi_ref[r * sub:(r + 1) * sub, :] = hi
        plo_ref[r * sub:(r + 1) * sub, :] = (e - hi.astype(F32)).astype(BF16)
    inv_c = jnp.where(m_c > dead, 1.0 / jnp.sum(lpart, axis=0, keepdims=True), 0.0)
    o_c = _dot(cvt_ref[0, v_rows, :], phi_ref[...]) * inv_c

    imp4 = (_dot(ovt_ref[...], phi_ref[...]) + _dot(ovt_ref[...], plo_ref[...])) * inv_c
    imp = imp4[:, 0:tq]
    for r in range(1, rep):
        imp = imp + imp4[:, r * tq:(r + 1) * tq]
    blk = lax.broadcasted_iota(jnp.int32, imp.shape, 0)
    qp = t0 + lax.broadcasted_iota(jnp.int32, imp.shape, 1)
    cur = jnp.right_shift(qp, NSA_SEL_BLOCK.bit_length() - 1)
    forced = (blk == 0) | (blk == cur) | (blk == cur - 1)
    future = blk * NSA_SEL_BLOCK > qp
    imp_ref[...] = jnp.where(future, -FORCE_SCORE, jnp.where(forced, FORCE_SCORE, imp))
    bias_ref[...] = jnp.full(bias_ref.shape, NEG_INF, F32)

    n_live = jnp.minimum((t0 + tq - 1) // NSA_SEL_BLOCK + 1, n_sel)
    n_var = max(n_sel // 32, 1)
    rows_per = n_sel // n_var
    top_n = float(min(NSA_TOP_N, n_sel))
    for v in range(n_var):
        rows = rows_per * (v + 1)

        @pl.when((n_live > rows_per * v) & (n_live <= rows))
        def _():
            mine = imp_ref[0:rows, :]
            blk_r = lax.broadcasted_iota(jnp.int32, mine.shape, 0)

            def count(i2, cnt):
                other = imp_ref[pl.ds(i2, 1), :]
                beats = (other > mine) | ((other == mine) & (blk_r > i2))
                return cnt + jnp.where(beats, 1.0, 0.0)

            rank = lax.fori_loop(0, n_live, count, jnp.zeros(mine.shape, F32))
            bias = jnp.where(rank < top_n, 0.0, NEG_INF)
            bias_ref[0:rows, :] = jnp.concatenate([bias] * rep, axis=1)

    init = (jnp.full((1, width), NEG_INF, F32), jnp.zeros((d + ONES_ROWS, width), F32))

    def normalised(acc):
        return acc[:d, :] / acc[d:d + 1, :]

    chunk = 8 * sub
    n_sub = chunk // sub

    s_slots = (ss_ref, ss1_ref)
    p_slots = (ps_ref, ps1_ref)

    def sel_scores(c, slot, diagonal=False):
        start = pl.multiple_of(c * chunk, chunk)
        s = _dot_nt(ksl_ref[0, pl.ds(start, chunk), :], q4)
        if diagonal:
            kpos = start + lax.broadcasted_iota(jnp.int32, s.shape, 0)
            s = jnp.where(kpos <= qpos_of(s.shape), s, NEG_INF)
        s_slots[slot][...] = s

    def sel_update(c, slot, carry):
        m_prev, acc = carry
        s_ref, p_ref = s_slots[slot], p_slots[slot]
        biases = [bias_ref[pl.ds(c * n_sub + r, 1), :] for r in range(n_sub)]
        m8 = jnp.full((SUBLANES, width), NEG_INF, F32)
        for r in range(n_sub):
            block = s_ref[r * sub:(r + 1) * sub, :]
            m8 = jnp.maximum(m8, jnp.max(block.reshape(sub // SUBLANES, SUBLANES, width), axis=0) + biases[r])
        m_new = jnp.maximum(m_prev, jnp.max(m8, axis=0, keepdims=True))
        live = m_new > dead
        for r in range(n_sub):
            shift = jnp.where(live, biases[r] - m_new, NEG_INF)
            p_ref[r * sub:(r + 1) * sub, :] = jnp.exp2(s_ref[r * sub:(r + 1) * sub, :] + shift).astype(BF16)
        vt = _with_ones_rows(vslt_ref[v_rows, pl.ds(pl.multiple_of(c * chunk, chunk), chunk)])
        return m_new, jnp.exp2(m_prev - m_new) * acc + _dot(vt, p_ref[...])

    c_diag = (t0 + tq - 1) // chunk
    n_chunks = c_diag + 1
    last_past = jnp.maximum(c_diag - 1, 0)

    def chunk_at(j):
        return jnp.where(j == 0, c_diag, j - 1)

    sel_scores(c_diag, 0, diagonal=True)

    def pair(k, carry):
        sel_scores(jnp.minimum(2 * k, last_past), 1)
        carry = sel_update(chunk_at(2 * k), 0, carry)
        sel_scores(jnp.minimum(2 * k + 1, last_past), 0)
        return sel_update(2 * k, 1, carry)

    carry = lax.fori_loop(0, n_chunks // 2, pair, init)
    _, acc_s = lax.cond(n_chunks % 2 == 1, lambda cr: sel_update(chunk_at(n_chunks - 1), 0, cr),
                        lambda cr: cr, carry)
    o_s = normalised(acc_s)

    def win_chunk(c, carry):
        m_prev, acc = carry
        start = pl.multiple_of(c * tq, tq)
        s = _dot_nt(kwn_ref[0, pl.ds(start, tq), :], q4)
        kpos = start + lax.broadcasted_iota(jnp.int32, s.shape, 0)
        qpos = qpos_of(s.shape)
        s = jnp.where((kpos <= qpos) & (kpos > qpos - NSA_WINDOW), s, NEG_INF)
        m_new = jnp.maximum(m_prev, jnp.max(s, axis=0, keepdims=True))
        p = jnp.exp2(s + jnp.where(m_new > dead, -m_new, NEG_INF))
        vt = _with_ones_rows(vwnt_ref[v_rows, pl.ds(start, tq)])
        return m_new, jnp.exp2(m_prev - m_new) * acc + _dot(vt, p.astype(BF16))

    n_back = NSA_WINDOW // tq

    @pl.when(i < n_back)
    def _():
        _, acc_w = lax.fori_loop(0, i + 1, win_chunk, init)
        ow_ref[...] = normalised(acc_w)

    @pl.when(i >= n_back)
    def _():
        start = pl.multiple_of(t0 - NSA_WINDOW, tq)
        s = _dot_nt(kwn_ref[0, pl.ds(start, NSA_WINDOW + tq), :], q4)
        kpos = start + lax.broadcasted_iota(jnp.int32, (tq, width), 0)
        qpos = qpos_of((tq, width))
        sw_ref[0:tq, :] = jnp.where(kpos > qpos - NSA_WINDOW, s[0:tq, :], NEG_INF)
        sw_ref[tq:NSA_WINDOW, :] = s[tq:NSA_WINDOW, :]
        sw_ref[NSA_WINDOW:, :] = jnp.where(kpos + NSA_WINDOW <= qpos, s[NSA_WINDOW:, :], NEG_INF)
        m_w = jnp.max(sw_ref[...], axis=0, keepdims=True)
        for r in range((NSA_WINDOW + tq) // sub):
            pw_ref[r * sub:(r + 1) * sub, :] = jnp.exp2(sw_ref[r * sub:(r + 1) * sub, :] - m_w).astype(BF16)
        vt = _with_ones_rows(vwnt_ref[v_rows, pl.ds(start, NSA_WINDOW + tq)])
        ow_ref[...] = normalised(_dot(vt, pw_ref[...]))

    o_w = ow_ref[...]

    def gate(branch):
        rows = [glt_ref[pl.ds((g * rep + r) * 3 + branch, 1), :] for r in range(rep)]
        return _sigmoid(jnp.concatenate(rows, axis=1))

    out = gate(0) * o_c + gate(1) * o_s + gate(2) * o_w
    out_t = jnp.concatenate([out, jnp.zeros_like(out)], axis=0).T
    o_ref[0] = jnp.concatenate([out_t[r * tq:(r + 1) * tq, :d] for r in range(rep)],
                               axis=1).astype(o_ref.dtype)


def nsa_overlap_t(n_cmp, n_sel):
    c_start = np.arange(n_cmp)[None, :] * NSA_CMP_STRIDE
    s_start = np.arange(n_sel)[:, None] * NSA_SEL_BLOCK
    hit = (c_start < s_start + NSA_SEL_BLOCK) & (c_start + NSA_CMP_BLOCK > s_start)
    hit = hit & (np.arange(n_cmp)[None, :] < n_cmp - NSA_CMP_BLOCK // NSA_CMP_STRIDE + 1)
    return jnp.asarray(hit.astype(np.float32), dtype=BF16)


def nsa_attention(qn, ck, cvt, ksl, vslt, kwn, vwnt, glt, tq=128):
    b, s, _ = qn.shape
    g, d = NSA_KV_GROUPS, NSA_HEAD_DIM
    rep = NSA_HEADS // g
    n_cmp = ck.shape[1]
    n_sel = s // NSA_SEL_BLOCK
    nq = s // tq
    ovt = nsa_overlap_t(n_cmp, n_sel)
    width = rep * tq
    chunk = 8 * NSA_SEL_BLOCK
    full3 = lambda bb, gg, i: (bb, 0, 0)
    seq_t = lambda bb, gg, i: (0, bb)
    return pl.pallas_call(
        functools.partial(_nsa_body, tq=tq), grid=(b, g, nq),
        in_specs=[pl.BlockSpec((1, tq, rep * d), lambda bb, gg, i: (bb, i, gg)),
                  pl.BlockSpec((1, n_cmp, g * d), full3), pl.BlockSpec((1, g * d, n_cmp), full3),
                  pl.BlockSpec((1, s, g * d), full3), pl.BlockSpec((g * d, s), seq_t),
                  pl.BlockSpec((1, s, g * d), full3), pl.BlockSpec((g * d, s), seq_t),
                  pl.BlockSpec((n_sel, n_cmp), lambda bb, gg, i: (0, 0)),
                  pl.BlockSpec((glt.shape[0], tq), lambda bb, gg, i: (0, bb * nq + i))],
        out_specs=pl.BlockSpec((1, tq, rep * d), lambda bb, gg, i: (bb, i, gg)),
        out_shape=jax.ShapeDtypeStruct((b, s, g * rep * d), BF16),
        scratch_shapes=[pltpu.VMEM((n_cmp, width), F32), pltpu.VMEM((n_cmp, width), BF16),
                        pltpu.VMEM((n_cmp, width), BF16), pltpu.VMEM((n_sel, tq), F32),
                        pltpu.VMEM((n_sel, width), F32), pltpu.VMEM((chunk, width), F32),
                        pltpu.VMEM((chunk, width), BF16), pltpu.VMEM((chunk, width), F32),
                        pltpu.VMEM((chunk, width), BF16), pltpu.VMEM((NSA_WINDOW + tq, width), F32),
                        pltpu.VMEM((NSA_WINDOW + tq, width), BF16), pltpu.VMEM((d, width), F32)],
        compiler_params=_cparams(("parallel", "parallel", "arbitrary")), name="nsa_attention",
    )(qn, ck, cvt, ksl, vslt, kwn, vwnt, ovt, glt)


def _pad_cols(w, n):
    return jnp.pad(w, ((0, 0), (0, n - w.shape[1])))


def _even_layer(x2, b, s, layer_idx, norm_mix, w_in, q_gain, k_gain, lam, subln_gain, conv_w, conv_b,
                dt_bias, a_log, d_skip, ssm_norm_gain, w_out, norm_ffn, w_gate, w_up, w_down):
    nq = DA_HEADS * 2 * DA_HEAD_DIM
    nv = DA_HEADS * DA_V_DIM
    cch = SSM_D_INNER + 2 * SSM_GROUPS * SSM_STATE
    offs = np.cumsum([0, nq, nq, nv, SSM_D_INNER, cch, SSM_HEADS])
    wb = w_in.astype(BF16)
    pieces = [wb[:, offs[k]:offs[k + 1]] for k in range(6)]
    pieces[2] = pieces[2].T
    pieces[5] = _pad_cols(pieces[5], LANES)
    posts = [HeadNorm(q_gain, DA_HEAD_DIM, rope=True, mul=DA_HEAD_DIM ** -0.5 * LOG2E),
             HeadNorm(k_gain, DA_HEAD_DIM, rope=True), None, None, None, None]
    q, k, vt, z, xbc, dt = norm_proj(x2, norm_mix, pieces, [BF16, BF16, BF16, F32, F32, F32], posts, s,
                                     _rope_tables(s, DA_HEAD_DIM), transposed=(2,))
    qn = q.reshape(b, s, nq)
    kn = k.reshape(b, s, nq)
    lam_init = 0.8 - 0.6 * math.exp(-0.3 * layer_idx)
    lf = lam.astype(F32)
    lam_full = jnp.exp(jnp.sum(lf[0] * lf[1])) - jnp.exp(jnp.sum(lf[2] * lf[3])) + lam_init
    a_out = flash_attention(lam_full.reshape(1), [qn], [kn], vt, subln_gain, DA_HEADS, DA_V_DIM,
                            diff=True, out_scale=1.0 - lam_init)
    b_out = ssd_mixer(xbc.reshape(b, s, cch), z.reshape(b, s, SSM_D_INNER), dt.reshape(b, s, LANES),
                      conv_w, conv_b, dt_bias, a_log, d_skip, ssm_norm_gain)
    wo = w_out.astype(BF16)
    x2 = out_proj_residual(x2, a_out.reshape(-1, nv), b_out.reshape(-1, SSM_D_INNER), wo[:nv], wo[nv:])
    return ffn_residual(x2, norm_ffn, w_gate.astype(BF16), w_up.astype(BF16), w_down.astype(BF16))


def _odd_layer(x2, b, s, norm_mix, w_in, q_gain, k_gain, cmp_pos, cmp_w1, cmp_w2, cq_gain, ckv_gain,
               w_uq, w_ukv, qn_gain, qr_gain, kn_gain, kr_gain, w_out, norm_ffn, router_w, w_gate, w_up,
               w_down):
    g, d = NSA_KV_GROUPS, NSA_HEAD_DIM
    nq = NSA_HEADS * d
    nkv = g * d
    sizes = [nq] + [nkv] * 6 + [NSA_HEADS * 3, w_uq.shape[0], w_ukv.shape[0], MLA_ROPE_DIM]
    offs = np.cumsum([0] + sizes)
    wb = w_in.astype(BF16)
    pieces = [wb[:, offs[k]:offs[k + 1]] for k in range(len(sizes))]
    for k in (4, 6):
        pieces[k] = pieces[k].T
    pieces[7] = jnp.pad(pieces[7].T, ((0, 32 - NSA_HEADS * 3), (0, 0)))
    pieces[10] = _pad_cols(pieces[10], LANES)
    tables = _rope_tables(s, d)
    posts = [None] * len(sizes)
    posts[0] = HeadNorm(q_gain, d, rope=True, mul=d ** -0.5 * LOG2E)
    posts[3] = HeadNorm(k_gain[1], d, rope=True)
    posts[5] = HeadNorm(k_gain[2], d, rope=True)
    posts[10] = HeadNorm(kr_gain, MLA_ROPE_DIM, rope=True)
    (q, kc, vc, ksl, vslt, kwn, vwnt, glt, cq, ckv, k_rope) = norm_proj(
        x2, norm_mix, pieces, [BF16, F32, F32, BF16, BF16, BF16, BF16, F32, F32, F32, BF16], posts, s, tables,
        transposed=(4, 6, 7))

    qn = q.reshape(b, s, nq)
    ksl_n = ksl.reshape(b, s, nkv)
    kwn_n = kwn.reshape(b, s, nkv)
    ck = nsa_compress(kc.reshape(b, s, nkv), cmp_pos[0], cmp_w1[0], cmp_w2[0], k_gain[0], s, True)
    cv = nsa_compress(vc.reshape(b, s, nkv), cmp_pos[1], cmp_w1[1], cmp_w2[1], k_gain[0], s, False)
    n_cmp = ck.shape[2]
    ck = ck.transpose(0, 2, 1, 3).reshape(b, n_cmp, nkv).astype(BF16)
    cvt = cv.transpose(0, 1, 3, 2).reshape(b, nkv, n_cmp).astype(BF16)
    c_out = nsa_attention(qn, ck, cvt, ksl_n, vslt, kwn_n, vwnt, glt).reshape(b * s, nq)

    h = MLA_HEADS
    dqk = MLA_NOPE_DIM + MLA_ROPE_DIM
    wq = w_uq.astype(BF16).reshape(-1, h, dqk)
    wq_nope = wq[:, :, :MLA_NOPE_DIM].reshape(-1, h * MLA_NOPE_DIM)
    wq_rope = jnp.pad(wq[:, :, MLA_NOPE_DIM:], ((0, 0), (0, 0), (0, LANES - MLA_ROPE_DIM)))
    wq_rope = wq_rope.reshape(-1, h * LANES)
    wkv = w_ukv.astype(BF16).reshape(-1, h, MLA_NOPE_DIM + MLA_V_DIM)
    wk_nope = wkv[:, :, :MLA_NOPE_DIM].reshape(-1, h * MLA_NOPE_DIM)
    wv = wkv[:, :, MLA_NOPE_DIM:].reshape(-1, h * MLA_V_DIM)
    q_mul = dqk ** -0.5 * LOG2E
    q_nope, q_rope = norm_proj(
        cq, cq_gain, [wq_nope, wq_rope], [BF16, BF16],
        [HeadNorm(qn_gain, MLA_NOPE_DIM, mul=q_mul), HeadNorm(qr_gain, MLA_ROPE_DIM, rope=True, mul=q_mul)],
        s, tables)
    k_nope, vt = norm_proj(ckv, ckv_gain, [wk_nope, wv.T], [BF16, BF16],
                           [HeadNorm(kn_gain, MLA_NOPE_DIM), None], transposed=(1,))
    shp = lambda t: t.reshape(b, s, t.shape[-1])
    d_out = flash_attention(jnp.zeros((1,), F32), [shp(q_nope), shp(q_rope)], [shp(k_nope), shp(k_rope)],
                            vt, jnp.ones((MLA_V_DIM,), F32), h, MLA_V_DIM, diff=False)

    wo = w_out.astype(BF16)
    x2 = out_proj_residual(x2, c_out, d_out.reshape(b * s, h * MLA_V_DIM), wo[:nq], wo[nq:])
    h, route, counts = router(x2, norm_ffn, router_w)
    return moe_residual(x2, h, route, counts, w_gate.astype(BF16), w_up.astype(BF16), w_down.astype(BF16))


def kernel(x, ev_norm_mix, ev_w_in, da_q_gain, da_k_gain, da_lambda, da_subln_gain, ssm_conv_w, ssm_conv_b, ssm_dt_bias, ssm_a_log, ssm_d, ssm_norm_gain, ev_w_out, ev_norm_ffn, ffn_w_gate, ffn_w_up, ffn_w_down, od_norm_mix, od_w_in, nsa_q_gain, nsa_k_gain, nsa_cmp_pos, nsa_cmp_w1, nsa_cmp_w2, mla_cq_gain, mla_ckv_gain, mla_w_uq, mla_w_ukv, mla_qn_gain, mla_qr_gain, mla_kn_gain, mla_kr_gain, od_w_out, od_norm_ffn, moe_router, moe_w_gate, moe_w_up, moe_w_down):
    b, s, d = x.shape
    x2 = x.reshape(b * s, d)
    depth = ev_norm_mix.shape[0] + od_norm_mix.shape[0]
    for layer in range(depth):
        i = layer // 2
        if layer % 2 == 0:
            x2 = _even_layer(x2, b, s, layer, ev_norm_mix[i], ev_w_in[i], da_q_gain[i], da_k_gain[i],
                             da_lambda[i], da_subln_gain[i], ssm_conv_w[i], ssm_conv_b[i],
                             ssm_dt_bias[i], ssm_a_log[i], ssm_d[i], ssm_norm_gain[i], ev_w_out[i],
                             ev_norm_ffn[i], ffn_w_gate[i], ffn_w_up[i], ffn_w_down[i])
        else:
            x2 = _odd_layer(x2, b, s, od_norm_mix[i], od_w_in[i], nsa_q_gain[i], nsa_k_gain[i],
                            nsa_cmp_pos[i], nsa_cmp_w1[i], nsa_cmp_w2[i], mla_cq_gain[i],
                            mla_ckv_gain[i], mla_w_uq[i], mla_w_ukv[i], mla_qn_gain[i], mla_qr_gain[i],
                            mla_kn_gain[i], mla_kr_gain[i], od_w_out[i], od_norm_ffn[i], moe_router[i],
                            moe_w_gate[i], moe_w_up[i], moe_w_down[i])
    return x2.reshape(b, s, d)
```

```python
import functools
import math

import numpy as np
import jax
import jax.numpy as jnp
from jax import lax
from jax.experimental import pallas as pl
from jax.experimental.pallas import tpu as pltpu

F32 = jnp.float32
BF16 = jnp.bfloat16

ROPE_THETA = 10000.0
NORM_EPS = 1e-6
NEG_INF = -1e30
FORCE_SCORE = 1e6
LOG2E = 1.4426950408889634

DA_HEADS = 4
DA_HEAD_DIM = 64
DA_V_DIM = 2 * DA_HEAD_DIM
SSM_HEADS = 8
SSM_HEAD_DIM = 64
SSM_D_INNER = SSM_HEADS * SSM_HEAD_DIM
SSM_GROUPS = 2
SSM_STATE = 128
SSM_CONV = 4
SSM_CHUNK = 256
NSA_HEADS = 8
NSA_KV_GROUPS = 2
NSA_HEAD_DIM = 64
NSA_CMP_BLOCK = 32
NSA_CMP_STRIDE = 16
NSA_SEL_BLOCK = 64
NSA_TOP_N = 16
NSA_WINDOW = 512
MLA_HEADS = 4
MLA_NOPE_DIM = 128
MLA_ROPE_DIM = 64
MLA_V_DIM = 128
N_EXPERTS = 8

LANES = 128
SUBLANES = 8
VMEM_LIMIT = 48 * 1024 * 1024
MOE_VMEM_LIMIT = 58 * 1024 * 1024

NT_DIMS = (((1,), (1,)), ((), ()))


def _cparams(semantics):
    return pltpu.CompilerParams(dimension_semantics=semantics, vmem_limit_bytes=VMEM_LIMIT)


def _dot(a, b):
    return jnp.dot(a, b, preferred_element_type=F32)


def _dot_nt(a, b):
    return lax.dot_general(a, b, NT_DIMS, preferred_element_type=F32)


def _split_bf16(x, parts):
    out = []
    for _ in range(parts):
        hi = x.astype(BF16)
        out.append(hi)
        x = x - hi.astype(F32)
    return out


def _fold_rows(x):
    return jnp.sum(x.reshape(x.shape[0] // SUBLANES, SUBLANES, x.shape[1]), axis=0)


ONES_ROWS = 16


def _with_ones_rows(vt):
    return jnp.concatenate([vt, jnp.ones((ONES_ROWS, vt.shape[1]), vt.dtype)], axis=0)


def _cast_body(x_ref, o_ref):
    o_ref[...] = x_ref[...].astype(o_ref.dtype)


def cast_bf16(w):
    n_e, a, b = w.shape
    rows = next(r for r in (512, 256, 128, 64, 32, 16) if a % r == 0)
    return pl.pallas_call(
        _cast_body, grid=(n_e, a // rows),
        in_specs=[pl.BlockSpec((1, rows, b), lambda e, i: (e, i, 0))],
        out_specs=pl.BlockSpec((1, rows, b), lambda e, i: (e, i, 0)),
        out_shape=jax.ShapeDtypeStruct(w.shape, BF16),
        compiler_params=_cparams(("parallel", "parallel")), name="cast_bf16",
    )(w)


def _sigmoid(x):
    return 1.0 / (1.0 + jnp.exp(-x))


def _silu(x):
    return x * _sigmoid(x)


def _softplus(x):
    return jnp.maximum(x, 0.0) + jnp.log(1.0 + jnp.exp(-jnp.abs(x)))


def _rms(x, gain):
    ms = jnp.mean(x * x, axis=-1, keepdims=True)
    return x * lax.rsqrt(ms + NORM_EPS) * gain


class HeadNorm:
    def __init__(self, gain, hd, rope=False, mul=1.0):
        self.gain, self.hd, self.rope, self.mul = gain, hd, rope, mul


def _head_norm(y, gain, bd, cos_ref, sin_ref, post):
    n = y.shape[1]
    hd = post.hd
    hi, lo = _split_bf16(y * y, 2)
    ss = _dot(hi, bd) + _dot(lo, bd)
    yn = y * lax.rsqrt(ss * (1.0 / hd) + NORM_EPS) * gain
    if post.rope:
        reps = n // LANES
        cos = jnp.concatenate([cos_ref[...]] * reps, axis=1) if reps > 1 else cos_ref[...]
        sin = jnp.concatenate([sin_ref[...]] * reps, axis=1) if reps > 1 else sin_ref[...]
        lane = lax.broadcasted_iota(jnp.int32, yn.shape, 1)
        first_half = (lane & (hd - 1)) < (hd // 2)
        partner = jnp.where(first_half, pltpu.roll(yn, n - hd // 2, 1), pltpu.roll(yn, hd // 2, 1))
        yn = yn * cos + partner * sin
    if post.mul != 1.0:
        yn = yn * post.mul
    return yn


def _norm_proj_body(x_ref, g_ref, *refs, posts, use_rope, transposed):
    if use_rope:
        cos_ref, sin_ref = refs[0], refs[1]
        refs = refs[2:]
    else:
        cos_ref = sin_ref = None
    n_out = len(posts)
    n_aux = 2 * sum(p is not None for p in posts)
    w_refs, aux, o_refs = refs[:n_out], refs[n_out:n_out + n_aux], refs[n_out + n_aux:]
    h = _rms(x_ref[...], g_ref[...]).astype(BF16)
    a = 0
    for k, (w_ref, o_ref, post) in enumerate(zip(w_refs, o_refs, posts)):
        if k in transposed:
            o_ref[...] = _dot_nt(w_ref[...], h).astype(o_ref.dtype)
            continue
        y = _dot(h, w_ref[...])
        if post is not None:
            y = _head_norm(y, aux[a][...], aux[a + 1][...], cos_ref, sin_ref, post)
            a += 2
        o_ref[...] = y.astype(o_ref.dtype)


def norm_proj(x2, gain, weights, out_dtypes, posts=None, seq=None, rope_tables=None, transposed=(), tm=512):
    t, d = x2.shape
    posts = posts or [None] * len(weights)
    transposed = frozenset(transposed)
    use_rope = any(p is not None and p.rope for p in posts)
    const = lambda i: (0, 0)
    args = [x2, gain.reshape(1, d).astype(F32)]
    in_specs = [pl.BlockSpec((tm, d), lambda i: (i, 0)), pl.BlockSpec((1, d), const)]
    if use_rope:
        per_seq = seq // tm
        args += list(rope_tables)
        in_specs += [pl.BlockSpec((tm, LANES), lambda i: (i % per_seq, 0))] * 2
    args += list(weights)
    in_specs += [pl.BlockSpec(w.shape, const) for w in weights]
    for w, p in zip(weights, posts):
        if p is not None:
            n = w.shape[1]
            args += [jnp.tile(p.gain.astype(F32), n // p.hd).reshape(1, n), _block_diag_ones(n, p.hd)]
            in_specs += [pl.BlockSpec((1, n), const), pl.BlockSpec((n, n), const)]
    out_specs, out_shape = [], []
    for k, (w, dt) in enumerate(zip(weights, out_dtypes)):
        if k in transposed:
            out_specs.append(pl.BlockSpec((w.shape[0], tm), lambda i: (0, i)))
            out_shape.append(jax.ShapeDtypeStruct((w.shape[0], t), dt))
        else:
            out_specs.append(pl.BlockSpec((tm, w.shape[1]), lambda i: (i, 0)))
            out_shape.append(jax.ShapeDtypeStruct((t, w.shape[1]), dt))
    return pl.pallas_call(
        functools.partial(_norm_proj_body, posts=tuple(posts), use_rope=use_rope, transposed=transposed),
        grid=(t // tm,), in_specs=in_specs, out_specs=out_specs, out_shape=out_shape,
        compiler_params=_cparams(("parallel",)), name="norm_proj",
    )(*args)


def _block_diag_ones(n, hd):
    idx = np.arange(n) // hd
    return jnp.asarray((idx[:, None] == idx[None, :]).astype(np.float32), dtype=BF16)


def _rope_tables(seq, hd):
    inv_freq = 1.0 / (ROPE_THETA ** (jnp.arange(0, hd, 2, dtype=F32) / hd))
    ang = jnp.arange(seq, dtype=F32)[:, None] * inv_freq[None, :]
    cos, sin = jnp.cos(ang), jnp.sin(ang)
    reps = LANES // hd
    cos_t = jnp.tile(jnp.concatenate([cos, cos], axis=1), (1, reps))
    sin_t = jnp.tile(jnp.concatenate([-sin, sin], axis=1), (1, reps))
    return cos_t, sin_t


def _flash_body(lam_ref, *refs, n_qk, diff, out_scale, sub):
    q_refs = refs[:n_qk]
    k_refs = refs[n_qk:2 * n_qk]
    vt_ref, gain_ref, o_ref, m_ref, l_ref, acc_ref, s0_ref, s1_ref, p0_ref, p1_ref = refs[2 * n_qk:]
    i = pl.program_id(2)
    n_sm = 2 if diff else 1
    _, tk, tq = s0_ref.shape
    s_slots = (s0_ref, s1_ref)
    p_slots = (p0_ref, p1_ref)

    m_ref[...] = jnp.full(m_ref.shape, NEG_INF, F32)
    l_ref[...] = jnp.zeros(l_ref.shape, F32)
    acc_ref[...] = jnp.zeros(acc_ref.shape, F32)

    qs = [r[0] for r in q_refs]
    q = qs[0] if n_qk == 1 else jnp.concatenate(qs, axis=1)
    if diff:
        lane = lax.broadcasted_iota(jnp.int32, q.shape, 1)
        half = q.shape[1] // 2
        zero = jnp.zeros_like(q)
        q_parts = [jnp.where(lane < half, q, zero), jnp.where(lane >= half, q, zero)]
    else:
        q_parts = [q]

    def scores(c, slot, diagonal=False):
        rows = pl.ds(pl.multiple_of(c * tk, tk), tk)
        ks = [r[0, rows, :] for r in k_refs]
        k = ks[0] if n_qk == 1 else jnp.concatenate(ks, axis=1)
        for sm in range(n_sm):
            s = _dot_nt(k, q_parts[sm])
            if diagonal:
                row = lax.broadcasted_iota(jnp.int32, s.shape, 0)
                col = lax.broadcasted_iota(jnp.int32, s.shape, 1)
                s = jnp.where(row <= col, s, NEG_INF)
            s_slots[slot][sm] = s

    def update(c, slot):
        vt = vt_ref[:, pl.ds(pl.multiple_of(c * tk, tk), tk)]
        for sm in range(n_sm):
            s_ref, p_ref = s_slots[slot], p_slots[slot]
            m_prev = m_ref[sm]
            m_new = jnp.maximum(m_prev, jnp.max(s_ref[sm], axis=0, keepdims=True))
            m_ref[sm] = m_new
            alpha = jnp.exp2(m_prev - m_new)
            lpart = jnp.zeros((SUBLANES, tq), F32)
            for r in range(tk // sub):
                p = jnp.exp2(s_ref[sm, r * sub:(r + 1) * sub, :] - m_new)
                lpart = lpart + _fold_rows(p)
                p_ref[sm, r * sub:(r + 1) * sub, :] = p.astype(BF16)
            l_ref[sm] = alpha * l_ref[sm] + jnp.sum(lpart, axis=0, keepdims=True)
            acc_ref[sm] = alpha * acc_ref[sm] + _dot(vt, p_ref[sm])

    n_chunks = i + 1
    last_past = jnp.maximum(i - 1, 0)

    def chunk_at(j):
        return jnp.where(j == 0, i, j - 1)

    scores(i, 0, diagonal=True)

    def pair(k2, carry):
        scores(jnp.minimum(2 * k2, last_past), 1)
        update(chunk_at(2 * k2), 0)
        scores(jnp.minimum(2 * k2 + 1, last_past), 0)
        update(2 * k2, 1)
        return carry

    lax.fori_loop(0, n_chunks // 2, pair, 0)

    @pl.when(n_chunks % 2 == 1)
    def _():
        update(chunk_at(n_chunks - 1), 0)

    o = acc_ref[0] / l_ref[0]
    if diff:
        o = o - lam_ref[0] * (acc_ref[1] / l_ref[1])
        ms = jnp.mean(o * o, axis=0, keepdims=True)
        o = o * lax.rsqrt(ms + NORM_EPS) * gain_ref[...] * out_scale
    o_ref[0] = o.T.astype(o_ref.dtype)


def flash_attention(lam, qs, ks, vt, gain, n_heads, dv, *, diff, out_scale=1.0, tile=512, sub=64):
    b, s, _ = qs[0].shape
    nt = s // tile
    n_qk = len(qs)
    in_specs = [pl.BlockSpec(memory_space=pltpu.SMEM)]
    for q in qs:
        w = q.shape[2] // n_heads
        in_specs.append(pl.BlockSpec((1, tile, w), lambda bb, h, i: (bb, i, h)))
    for q, k in zip(qs, ks):
        w = q.shape[2] // n_heads
        if k.shape[2] == w:
            in_specs.append(pl.BlockSpec((1, s, w), lambda bb, h, i: (bb, 0, 0)))
        else:
            in_specs.append(pl.BlockSpec((1, s, w), lambda bb, h, i: (bb, 0, h)))
    in_specs.append(pl.BlockSpec((dv, s), lambda bb, h, i: (h, bb)))
    in_specs.append(pl.BlockSpec((dv, 1), lambda bb, h, i: (0, 0)))
    n_sm = 2 if diff else 1
    return pl.pallas_call(
        functools.partial(_flash_body, n_qk=n_qk, diff=diff, out_scale=out_scale, sub=sub),
        grid=(b, n_heads, nt), in_specs=in_specs,
        out_specs=pl.BlockSpec((1, tile, dv), lambda bb, h, i: (bb, i, h)),
        out_shape=jax.ShapeDtypeStruct((b, s, n_heads * dv), BF16),
        scratch_shapes=[pltpu.VMEM((n_sm, 1, tile), F32), pltpu.VMEM((n_sm, 1, tile), F32),
                        pltpu.VMEM((n_sm, dv, tile), F32),
                        pltpu.VMEM((n_sm, tile, tile), F32), pltpu.VMEM((n_sm, tile, tile), F32),
                        pltpu.VMEM((n_sm, tile, tile), BF16), pltpu.VMEM((n_sm, tile, tile), BF16)],
        compiler_params=_cparams(("parallel", "parallel", "arbitrary")),
        name="flash_diff" if diff else "flash_plain",
    )(lam, *qs, *ks, vt, gain.reshape(dv, 1).astype(F32))


def _ssd_body(xbc_ref, z_ref, dt_ref, dtt_ref, cw_ref, cb_ref, dtb_ref, dtbt_ref, al_ref, alt_ref,
              dsk_ref, ng_ref, o_ref, xpad_ref, state_ref):
    chunk = xbc_ref.shape[1]
    d_in = z_ref.shape[2]
    gn = SSM_GROUPS * SSM_STATE
    c = pl.program_id(1)

    @pl.when(c == 0)
    def _():
        xpad_ref[0:8, :] = jnp.zeros((8, xpad_ref.shape[1]), F32)
        state_ref[...] = jnp.zeros(state_ref.shape, F32)

    xpad_ref[8:8 + chunk, :] = xbc_ref[0]
    conv = cb_ref[...]
    for w in range(SSM_CONV):
        conv = conv + cw_ref[w:w + 1, :] * xpad_ref[pl.ds(8 - (SSM_CONV - 1) + w, chunk), :]
    xpad_ref[0:8, :] = xpad_ref[chunk:chunk + 8, :]
    u = _silu(conv)
    xs = u[:, :d_in]
    bmat = u[:, d_in:d_in + gn]
    cmat = u[:, d_in + gn:]

    dt = _softplus(dt_ref[0] + dtb_ref[...])
    ad = dt * (-jnp.exp(al_ref[...]))
    dtt = _softplus(dtt_ref[0] + dtbt_ref[...])
    adt = dtt * (-jnp.exp(alt_ref[...]))
    row = lax.broadcasted_iota(jnp.int32, (chunk, chunk), 0)
    col = lax.broadcasted_iota(jnp.int32, (chunk, chunk), 1)
    lower = row >= col
    tril = jnp.where(lower, 1.0, 0.0).astype(BF16)
    triu = jnp.where(row <= col, 1.0, 0.0).astype(BF16)
    cs = sum(_dot(tril, part) for part in _split_bf16(ad, 3))
    cst = sum(_dot(part, triu) for part in _split_bf16(adt, 3))

    heads_per_group = SSM_HEADS // SSM_GROUPS
    dsk = dsk_ref[...]
    ys = []
    for g in range(SSM_GROUPS):
        bg = bmat[:, g * SSM_STATE:(g + 1) * SSM_STATE]
        cg = cmat[:, g * SSM_STATE:(g + 1) * SSM_STATE].astype(BF16)
        cb = _dot_nt(cg, bg.astype(BF16))
        bgt = bg.T.astype(BF16)
        for r in range(heads_per_group):
            h = g * heads_per_group + r
            ccol = cs[:, h:h + 1]
            crow = cst[h:h + 1, :]
            decay = jnp.exp(jnp.where(lower, ccol - crow, NEG_INF))
            x_h = xs[:, h * SSM_HEAD_DIM:(h + 1) * SSM_HEAD_DIM]
            xdt = x_h * dt[:, h:h + 1]
            y = _dot((cb * decay).astype(BF16), xdt.astype(BF16))
            st = state_ref[h]
            y = y + _dot(cg, st.astype(BF16)) * jnp.exp(ccol)
            last = cst[h:h + 1, chunk - 1:chunk]
            to_end = jnp.exp(last - ccol)
            state_ref[h] = st * jnp.exp(last) + _dot(bgt, (xdt * to_end).astype(BF16))
            ys.append(y + x_h * dsk[:, h * SSM_HEAD_DIM:(h + 1) * SSM_HEAD_DIM])

    y = jnp.concatenate(ys, axis=1) * _silu(z_ref[0])
    gw = d_in // SSM_GROUPS
    for g in range(SSM_GROUPS):
        seg = y[:, g * gw:(g + 1) * gw]
        o_ref[0, :, g * gw:(g + 1) * gw] = _rms(seg, ng_ref[:, g * gw:(g + 1) * gw]).astype(o_ref.dtype)


def ssd_mixer(xbc, z, dt_raw, conv_w, conv_b, dt_bias, a_log, d_skip, norm_gain):
    b, s, cch = xbc.shape
    d_in = z.shape[2]
    nc = s // SSM_CHUNK
    hpad = dt_raw.shape[2]
    dtt = jnp.transpose(dt_raw[:, :, :SSM_HEADS], (0, 2, 1))

    def lane_pad(v):
        return jnp.pad(v.astype(F32), (0, hpad - SSM_HEADS)).reshape(1, hpad)

    args = (xbc, z, dt_raw, dtt, conv_w.astype(F32), conv_b.reshape(1, cch).astype(F32),
            lane_pad(dt_bias), dt_bias.reshape(SSM_HEADS, 1).astype(F32),
            lane_pad(a_log), a_log.reshape(SSM_HEADS, 1).astype(F32),
            jnp.repeat(d_skip.astype(F32), SSM_HEAD_DIM).reshape(1, d_in),
            norm_gain.reshape(1, d_in).astype(F32))
    const = lambda bb, c: (0, 0)
    in_specs = [pl.BlockSpec((1, SSM_CHUNK, cch), lambda bb, c: (bb, c, 0)),
                pl.BlockSpec((1, SSM_CHUNK, d_in), lambda bb, c: (bb, c, 0)),
                pl.BlockSpec((1, SSM_CHUNK, hpad), lambda bb, c: (bb, c, 0)),
                pl.BlockSpec((1, SSM_HEADS, SSM_CHUNK), lambda bb, c: (bb, 0, c)),
                pl.BlockSpec((SSM_CONV, cch), const), pl.BlockSpec((1, cch), const),
                pl.BlockSpec((1, hpad), const), pl.BlockSpec((SSM_HEADS, 1), const),
                pl.BlockSpec((1, hpad), const), pl.BlockSpec((SSM_HEADS, 1), const),
                pl.BlockSpec((1, d_in), const), pl.BlockSpec((1, d_in), const)]
    return pl.pallas_call(
        _ssd_body, grid=(b, nc), in_specs=in_specs,
        out_specs=pl.BlockSpec((1, SSM_CHUNK, d_in), lambda bb, c: (bb, c, 0)),
        out_shape=jax.ShapeDtypeStruct((b, s, d_in), BF16),
        scratch_shapes=[pltpu.VMEM((SSM_CHUNK + 8, cch), F32),
                        pltpu.VMEM((SSM_HEADS, SSM_STATE, SSM_HEAD_DIM), F32)],
        compiler_params=_cparams(("parallel", "arbitrary")), name="ssd_mixer",
    )(*args)


def _out_proj_body(x_ref, a_ref, b_ref, wa_ref, wb_ref, o_ref):
    o_ref[...] = x_ref[...] + _dot(a_ref[...], wa_ref[...]) + _dot(b_ref[...], wb_ref[...])


def out_proj_residual(x2, a, bm, wa, wb, tm=512):
    t, d = x2.shape
    return pl.pallas_call(
        _out_proj_body, grid=(t // tm,),
        in_specs=[pl.BlockSpec((tm, d), lambda i: (i, 0)),
                  pl.BlockSpec((tm, a.shape[1]), lambda i: (i, 0)),
                  pl.BlockSpec((tm, bm.shape[1]), lambda i: (i, 0)),
                  pl.BlockSpec(wa.shape, lambda i: (0, 0)),
                  pl.BlockSpec(wb.shape, lambda i: (0, 0))],
        out_specs=pl.BlockSpec((tm, d), lambda i: (i, 0)),
        out_shape=jax.ShapeDtypeStruct((t, d), F32),
        compiler_params=_cparams(("parallel",)), name="out_proj",
    )(x2, a, bm, wa, wb)


def _ffn_body(x_ref, g_ref, wg_ref, wu_ref, wd_ref, o_ref, h_ref):
    f = pl.program_id(1)

    @pl.when(f == 0)
    def _():
        x = x_ref[...]
        h_ref[...] = _rms(x, g_ref[...]).astype(BF16)
        o_ref[...] = x

    half = h_ref.shape[0] // 2
    for r in (slice(0, half), slice(half, 2 * half)):
        h = h_ref[r, :]
        act = (_silu(_dot(h, wg_ref[...])) * _dot(h, wu_ref[...])).astype(BF16)
        o_ref[r, :] += _dot(act, wd_ref[...])


def ffn_residual(x2, gain, w_gate, w_up, w_down, tm=1024, tf=1408):
    t, d = x2.shape
    d_ff = w_gate.shape[1]
    return pl.pallas_call(
        _ffn_body, grid=(t // tm, d_ff // tf),
        in_specs=[pl.BlockSpec((tm, d), lambda i, f: (i, 0)),
                  pl.BlockSpec((1, d), lambda i, f: (0, 0)),
                  pl.BlockSpec((d, tf), lambda i, f: (0, f)),
                  pl.BlockSpec((d, tf), lambda i, f: (0, f)),
                  pl.BlockSpec((tf, d), lambda i, f: (f, 0))],
        out_specs=pl.BlockSpec((tm, d), lambda i, f: (i, 0)),
        out_shape=jax.ShapeDtypeStruct((t, d), F32),
        scratch_shapes=[pltpu.VMEM((tm, d), BF16)],
        compiler_params=_cparams(("parallel", "arbitrary")), name="ffn",
    )(x2, gain.reshape(1, d).astype(F32), w_gate, w_up, w_down)


MOE_ROWS = 256
ROUTE_IDX = 0
ROUTE_W = 2
ROUTE_RANK = 4


def _moe_ffn_body(block_expert_ref, n_used_ref, xs_ref, wg_ref, wu_ref, wd_ref, o_ref):
    i = pl.program_id(0)

    @pl.when(i < n_used_ref[0])
    def _():
        x = xs_ref[...]
        act = (_silu(_dot(x, wg_ref[0])) * _dot(x, wu_ref[0])).astype(BF16)
        o_ref[...] = _dot(act, wd_ref[0]).astype(o_ref.dtype)

    @pl.when(i >= n_used_ref[0])
    def _():
        o_ref[...] = jnp.zeros(o_ref.shape, o_ref.dtype)


def moe_expert_ffn(xs, block_expert, n_used, w_gate, w_up, w_down):
    p, d = xs.shape
    d_ff = w_gate.shape[2]
    rows = MOE_ROWS
    grid_spec = pltpu.PrefetchScalarGridSpec(
        num_scalar_prefetch=2, grid=(p // rows,),
        in_specs=[pl.BlockSpec((rows, d), lambda i, be, nu: (i, 0)),
                  pl.BlockSpec((1, d, d_ff), lambda i, be, nu: (be[i], 0, 0)),
                  pl.BlockSpec((1, d, d_ff), lambda i, be, nu: (be[i], 0, 0)),
                  pl.BlockSpec((1, d_ff, d), lambda i, be, nu: (be[i], 0, 0))],
        out_specs=pl.BlockSpec((rows, d), lambda i, be, nu: (i, 0)))
    return pl.pallas_call(
        _moe_ffn_body, grid_spec=grid_spec, out_shape=jax.ShapeDtypeStruct((p, d), BF16),
        compiler_params=pltpu.CompilerParams(
            dimension_semantics=("arbitrary",), vmem_limit_bytes=MOE_VMEM_LIMIT),
        name="moe_expert_ffn",
    )(block_expert, n_used, xs, w_gate, w_up, w_down)


def _moe_combine_body(x_ref, y0_ref, y1_ref, route_ref, o_ref):
    route = route_ref[...]
    lane = lax.broadcasted_iota(jnp.int32, route.shape, 1)
    w0 = jnp.sum(jnp.where(lane == ROUTE_W, route, 0.0), axis=1, keepdims=True)
    w1 = jnp.sum(jnp.where(lane == ROUTE_W + 1, route, 0.0), axis=1, keepdims=True)
    o_ref[...] = x_ref[...] + w0 * y0_ref[...].astype(F32) + w1 * y1_ref[...].astype(F32)


def moe_combine(x2, y0, y1, route, tm=512):
    t, d = x2.shape
    row = lambda i: (i, 0)
    return pl.pallas_call(
        _moe_combine_body, grid=(t // tm,),
        in_specs=[pl.BlockSpec((tm, d), row), pl.BlockSpec((tm, d), row), pl.BlockSpec((tm, d), row),
                  pl.BlockSpec((tm, LANES), row)],
        out_specs=pl.BlockSpec((tm, d), row),
        out_shape=jax.ShapeDtypeStruct((t, d), F32),
        compiler_params=_cparams(("parallel",)), name="moe_combine",
    )(x2, y0, y1, route)


def moe_residual(x2, h, route, counts, w_gate, w_up, w_down):
    t, d = x2.shape
    n_e = w_gate.shape[0]
    rows = MOE_ROWS
    expert = route[:, ROUTE_IDX:ROUTE_IDX + 2].astype(jnp.int32).reshape(-1)
    rank = route[:, ROUTE_RANK:ROUTE_RANK + 2].astype(jnp.int32).reshape(-1)
    padded = (counts[0, :n_e].astype(jnp.int32) + rows - 1) // rows * rows
    ends = jnp.cumsum(padded)
    own = expert[:, None] == jnp.arange(n_e, dtype=jnp.int32)[None, :]
    slot = jnp.sum(jnp.where(own, (ends - padded)[None, :], 0), axis=1) + rank
    p_rows = 2 * t + n_e * rows
    token_of_slot = jnp.zeros((p_rows,), jnp.int32).at[slot].set(
        jnp.arange(2 * t, dtype=jnp.int32) // 2, unique_indices=True)
    block_start = jnp.arange(p_rows // rows, dtype=jnp.int32) * rows
    block_expert = jnp.minimum(jnp.searchsorted(ends, block_start, side="right"), n_e - 1).astype(jnp.int32)
    n_used = (ends[-1] // rows).astype(jnp.int32).reshape(1)

    take_rows = lambda a, idx: a.at[idx].get(mode="promise_in_bounds")
    xs = take_rows(h, token_of_slot)
    ys = moe_expert_ffn(xs, block_expert, n_used, w_gate, w_up, w_down)
    slot2 = slot.reshape(t, 2)
    return moe_combine(x2, take_rows(ys, slot2[:, 0]), take_rows(ys, slot2[:, 1]), route)


def _router_body(x_ref, g_ref, r_ref, h_ref, o_ref, count_ref, run_ref):
    @pl.when(pl.program_id(0) == 0)
    def _():
        run_ref[...] = jnp.zeros(run_ref.shape, F32)

    h = _rms(x_ref[...], g_ref[...])
    h_ref[...] = h.astype(h_ref.dtype)
    logits = jnp.dot(h, r_ref[...], precision=lax.Precision.HIGHEST, preferred_element_type=F32)
    lane = lax.broadcasted_iota(jnp.int32, logits.shape, 1).astype(F32)
    low = jnp.float32(-3.0e38)
    logits = jnp.where(lane < N_EXPERTS, logits, low)
    m1 = jnp.max(logits, axis=1, keepdims=True)
    i1 = jnp.min(jnp.where(logits == m1, lane, float(LANES)), axis=1, keepdims=True)
    rest = jnp.where(lane == i1, low, logits)
    m2 = jnp.max(rest, axis=1, keepdims=True)
    i2 = jnp.min(jnp.where(rest == m2, lane, float(LANES)), axis=1, keepdims=True)
    ex = jnp.exp(m2 - m1)
    w1 = 1.0 / (1.0 + ex)
    w2 = ex / (1.0 + ex)
    tm = logits.shape[0]
    routed = jnp.where((lane == i1) | (lane == i2), 1.0, 0.0)
    row = lax.broadcasted_iota(jnp.int32, (tm, tm), 0)
    col = lax.broadcasted_iota(jnp.int32, (tm, tm), 1)
    before = jnp.where(col < row, 1.0, 0.0).astype(BF16)
    rank = run_ref[0:1, :] + _dot(before, routed.astype(BF16))
    r1 = jnp.sum(jnp.where(lane == i1, rank, 0.0), axis=1, keepdims=True)
    r2 = jnp.sum(jnp.where(lane == i2, rank, 0.0), axis=1, keepdims=True)
    fields = ((ROUTE_IDX, i1), (ROUTE_IDX + 1, i2), (ROUTE_W, w1), (ROUTE_W + 1, w2),
              (ROUTE_RANK, r1), (ROUTE_RANK + 1, r2))
    out = jnp.zeros(logits.shape, F32)
    for pos, val in fields:
        out = jnp.where(lane == pos, val, out)
    o_ref[...] = out
    run_ref[...] = run_ref[...] + jnp.sum(routed, axis=0, keepdims=True)
    count_ref[...] = run_ref[...]


def router(x2, gain, router_w, tm=512):
    t, d = x2.shape
    r_pad = jnp.pad(router_w.astype(F32), ((0, 0), (0, LANES - router_w.shape[1])))
    return pl.pallas_call(
        _router_body, grid=(t // tm,),
        in_specs=[pl.BlockSpec((tm, d), lambda i: (i, 0)),
                  pl.BlockSpec((1, d), lambda i: (0, 0)),
                  pl.BlockSpec((d, LANES), lambda i: (0, 0))],
        out_specs=[pl.BlockSpec((tm, d), lambda i: (i, 0)), pl.BlockSpec((tm, LANES), lambda i: (i, 0)),
                   pl.BlockSpec((SUBLANES, LANES), lambda i: (0, 0))],
        out_shape=[jax.ShapeDtypeStruct((t, d), BF16), jax.ShapeDtypeStruct((t, LANES), F32),
                   jax.ShapeDtypeStruct((SUBLANES, LANES), F32)],
        scratch_shapes=[pltpu.VMEM((SUBLANES, LANES), F32)],
        compiler_params=_cparams(("arbitrary",)), name="router",
    )(x2, gain.reshape(1, d).astype(F32), r_pad)


def _compress_body(ch_ref, nx_ref, pos_ref, w1_ref, w2_ref, gain_ref, cos_ref, sin_ref, rot_ref, o_ref,
                   *, is_key):
    a = _dot((ch_ref[0] + pos_ref[0]).astype(BF16), w1_ref[0, 0])
    a = a + _dot((nx_ref[0] + pos_ref[1]).astype(BF16), w1_ref[0, 1])
    out = _dot(_silu(a).astype(BF16), w2_ref[...])
    if is_key:
        out = _rms(out, gain_ref[...])
        hi, lo = _split_bf16(out, 2)
        partner = _dot(hi, rot_ref[...]) + _dot(lo, rot_ref[...])
        out = out * cos_ref[...] + partner * sin_ref[...]
    o_ref[0, 0] = out


def nsa_compress(t, pos, w1, w2, gain, seq, is_key):
    b, s, _ = t.shape
    g, d = NSA_KV_GROUPS, NSA_HEAD_DIM
    n_ch = s // NSA_CMP_STRIDE
    half = NSA_CMP_STRIDE * g * d
    ch = t.reshape(b, n_ch, half)
    nxt = jnp.concatenate([ch[:, 1:], jnp.zeros((b, 1, half), F32)], axis=1)
    pos2 = jnp.broadcast_to(pos.astype(F32).reshape(2, NSA_CMP_STRIDE, 1, d),
                            (2, NSA_CMP_STRIDE, g, d)).reshape(2, 1, half)
    w1r = w1.astype(BF16).reshape(2, NSA_CMP_STRIDE, 1, d, d)
    own = (jnp.arange(g)[:, None] == jnp.arange(g)[None, :]).reshape(g, 1, 1, g, 1, 1)
    w1s = jnp.where(own, w1r[None], jnp.zeros((), BF16)).reshape(g, 2, half, d)
    cmp_end = jnp.arange(n_ch) * NSA_CMP_STRIDE + NSA_CMP_BLOCK - 1
    inv_freq = 1.0 / (ROPE_THETA ** (jnp.arange(0, d, 2, dtype=F32) / d))
    ang = cmp_end.astype(F32)[:, None] * inv_freq[None, :]
    cos = jnp.concatenate([jnp.cos(ang)] * 2, axis=1)
    sin = jnp.concatenate([jnp.sin(ang)] * 2, axis=1)
    rot = np.zeros((d, d), np.float32)
    rot[np.arange(d // 2) + d // 2, np.arange(d // 2)] = -1.0
    rot[np.arange(d // 2), np.arange(d // 2) + d // 2] = 1.0
    blk = lambda bb, gg: (bb, gg, 0, 0)
    seq = lambda bb, gg: (bb, 0, 0)
    c2 = lambda bb, gg: (0, 0)
    c3 = lambda bb, gg: (0, 0, 0)
    return pl.pallas_call(
        functools.partial(_compress_body, is_key=is_key), grid=(b, g),
        in_specs=[pl.BlockSpec((1, n_ch, half), seq), pl.BlockSpec((1, n_ch, half), seq),
                  pl.BlockSpec((2, 1, half), c3), pl.BlockSpec((1, 2, half, d), lambda bb, gg: (gg, 0, 0, 0)),
                  pl.BlockSpec((d, d), c2), pl.BlockSpec((1, d), c2),
                  pl.BlockSpec((n_ch, d), c2), pl.BlockSpec((n_ch, d), c2), pl.BlockSpec((d, d), c2)],
        out_specs=pl.BlockSpec((1, 1, n_ch, d), blk),
        out_shape=jax.ShapeDtypeStruct((b, g, n_ch, d), F32),
        compiler_params=_cparams(("parallel", "parallel")), name="nsa_compress",
    )(ch, nxt, pos2, w1s, w2.astype(BF16), gain.reshape(1, d).astype(F32), cos, sin,
      jnp.asarray(rot, dtype=BF16))


def _nsa_body(q_ref, ck_ref, cvt_ref, ksl_ref, vslt_ref, kwn_ref, vwnt_ref, ovt_ref, glt_ref, o_ref,
              sc_ref, phi_ref, plo_ref, imp_ref, bias_ref, ss_ref, ps_ref, ss1_ref, ps1_ref, sw_ref, pw_ref,
              ow_ref, oc_ref, *, tq):
    g = pl.program_id(1)
    i = pl.program_id(2)
    d = NSA_HEAD_DIM
    rep = NSA_HEADS // NSA_KV_GROUPS
    t0 = i * tq
    n_cmp = ck_ref.shape[1]
    n_sel = ovt_ref.shape[0]
    width = rep * tq
    sub = NSA_SEL_BLOCK
    dead = 0.5 * NEG_INF
    v_rows = pl.ds(pl.multiple_of(g * d, d), d)

    qb = q_ref[0]
    q4 = jnp.concatenate([qb[:, r * d:(r + 1) * d] for r in range(rep)], axis=0)
    q4 = jnp.concatenate([q4, q4], axis=1)
    lane = lax.broadcasted_iota(jnp.int32, q4.shape, 1)
    q4 = jnp.where(jnp.right_shift(lane, d.bit_length() - 1) == g, q4, jnp.zeros_like(q4))

    def qpos_of(shape):
        return t0 + (lax.broadcasted_iota(jnp.int32, shape, 1) & (tq - 1))

    def compressed(rows):
        s = _dot_nt(ck_ref[0, 0:rows, :], q4)
        cmp_end = lax.broadcasted_iota(jnp.int32, s.shape, 0) * NSA_CMP_STRIDE + (NSA_CMP_BLOCK - 1)
        s = jnp.where(cmp_end <= qpos_of(s.shape), s, NEG_INF)
        sc_ref[0:rows, :] = s
        m_c = jnp.max(s, axis=0, keepdims=True)
        lpart = jnp.zeros((SUBLANES, width), F32)
        for r in range(rows // sub):
            e = jnp.exp2(sc_ref[r * sub:(r + 1) * sub, :] - m_c)
            lpart = lpart + _fold_rows(e)
            hi = e.astype(BF16)
            phi_ref[r * sub:(r + 1) * sub, :] = hi
            plo_ref[r * sub:(r + 1) * sub, :] = (e - hi.astype(F32)).astype(BF16)
        inv_c = jnp.where(m_c > dead, 1.0 / jnp.sum(lpart, axis=0, keepdims=True), 0.0)
        oc_ref[...] = _dot(cvt_ref[0, v_rows, 0:rows], phi_ref[0:rows, :]) * inv_c
        ovt = ovt_ref[:, 0:rows]
        imp4 = (_dot(ovt, phi_ref[0:rows, :]) + _dot(ovt, plo_ref[0:rows, :])) * inv_c
        imp = imp4[:, 0:tq]
        for r in range(1, rep):
            imp = imp + imp4[:, r * tq:(r + 1) * tq]
        imp_ref[...] = imp

    cmp_live = jnp.clip((t0 + tq - NSA_CMP_BLOCK) // NSA_CMP_STRIDE + 1, 1, n_cmp)
    cmp_step = min(2 * LANES, n_cmp)
    for v in range(n_cmp // cmp_step):
        @pl.when((cmp_live > cmp_step * v) & (cmp_live <= cmp_step * (v + 1)))
        def _():
            compressed(cmp_step * (v + 1))

    o_c = oc_ref[...]
    imp = imp_ref[...]

    blk = lax.broadcasted_iota(jnp.int32, imp.shape, 0)
    qp = t0 + lax.broadcasted_iota(jnp.int32, imp.shape, 1)
    cur = jnp.right_shift(qp, NSA_SEL_BLOCK.bit_length() - 1)
    forced = (blk == 0) | (blk == cur) | (blk == cur - 1)
    future = blk * NSA_SEL_BLOCK > qp
    imp_ref[...] = jnp.where(future, -FORCE_SCORE, jnp.where(forced, FORCE_SCORE, imp))
    bias_ref[...] = jnp.full(bias_ref.shape, NEG_INF, F32)

    n_live = jnp.minimum((t0 + tq - 1) // NSA_SEL_BLOCK + 1, n_sel)
    n_var = max(n_sel // 32, 1)
    rows_per = n_sel // n_var
    top_n = float(min(NSA_TOP_N, n_sel))
    for v in range(n_var):
        rows = rows_per * (v + 1)

        @pl.when((n_live > rows_per * v) & (n_live <= rows))
        def _():
            mine = imp_ref[0:rows, :]
            blk_r = lax.broadcasted_iota(jnp.int32, mine.shape, 0)

            def count(i2, cnt):
                other = imp_ref[pl.ds(i2, 1), :]
                beats = (other > mine) | ((other == mine) & (blk_r > i2))
                return cnt + jnp.where(beats, 1.0, 0.0)

            rank = lax.fori_loop(0, n_live, count, jnp.zeros(mine.shape, F32))
            bias = jnp.where(rank < top_n, 0.0, NEG_INF)
            bias_ref[0:rows, :] = jnp.concatenate([bias] * rep, axis=1)

    init = (jnp.full((1, width), NEG_INF, F32), jnp.zeros((d + ONES_ROWS, width), F32))

    def normalised(acc):
        return acc[:d, :] / acc[d:d + 1, :]

    chunk = 8 * sub
    n_sub = chunk // sub

    s_slots = (ss_ref, ss1_ref)
    p_slots = (ps_ref, ps1_ref)

    def sel_scores(c, slot, diagonal=False):
        start = pl.multiple_of(c * chunk, chunk)
        s = _dot_nt(ksl_ref[0, pl.ds(start, chunk), :], q4)
        if diagonal:
            kpos = start + lax.broadcasted_iota(jnp.int32, s.shape, 0)
            s = jnp.where(kpos <= qpos_of(s.shape), s, NEG_INF)
        s_slots[slot][...] = s

    def sel_update(c, slot, carry):
        m_prev, acc = carry
        s_ref, p_ref = s_slots[slot], p_slots[slot]
        biases = [bias_ref[pl.ds(c * n_sub + r, 1), :] for r in range(n_sub)]
        m8 = jnp.full((SUBLANES, width), NEG_INF, F32)
        for r in range(n_sub):
            block = s_ref[r * sub:(r + 1) * sub, :]
            m8 = jnp.maximum(m8, jnp.max(block.reshape(sub // SUBLANES, SUBLANES, width), axis=0) + biases[r])
        m_new = jnp.maximum(m_prev, jnp.max(m8, axis=0, keepdims=True))
        live = m_new > dead
        for r in range(n_sub):
            shift = jnp.where(live, biases[r] - m_new, NEG_INF)
            p_ref[r * sub:(r + 1) * sub, :] = jnp.exp2(s_ref[r * sub:(r + 1) * sub, :] + shift).astype(BF16)
        vt = _with_ones_rows(vslt_ref[v_rows, pl.ds(pl.multiple_of(c * chunk, chunk), chunk)])
        return m_new, jnp.exp2(m_prev - m_new) * acc + _dot(vt, p_ref[...])

    c_diag = (t0 + tq - 1) // chunk
    n_chunks = c_diag + 1
    last_past = jnp.maximum(c_diag - 1, 0)

    def chunk_at(j):
        return jnp.where(j == 0, c_diag, j - 1)

    sel_scores(c_diag, 0, diagonal=True)

    def pair(k, carry):
        sel_scores(jnp.minimum(2 * k, last_past), 1)
        carry = sel_update(chunk_at(2 * k), 0, carry)
        sel_scores(jnp.minimum(2 * k + 1, last_past), 0)
        return sel_update(2 * k, 1, carry)

    carry = lax.fori_loop(0, n_chunks // 2, pair, init)
    _, acc_s = lax.cond(n_chunks % 2 == 1, lambda cr: sel_update(chunk_at(n_chunks - 1), 0, cr),
                        lambda cr: cr, carry)
    o_s = normalised(acc_s)

    def win_chunk(c, carry):
        m_prev, acc = carry
        start = pl.multiple_of(c * tq, tq)
        s = _dot_nt(kwn_ref[0, pl.ds(start, tq), :], q4)
        kpos = start + lax.broadcasted_iota(jnp.int32, s.shape, 0)
        qpos = qpos_of(s.shape)
        s = jnp.where((kpos <= qpos) & (kpos > qpos - NSA_WINDOW), s, NEG_INF)
        m_new = jnp.maximum(m_prev, jnp.max(s, axis=0, keepdims=True))
        p = jnp.exp2(s + jnp.where(m_new > dead, -m_new, NEG_INF))
        vt = _with_ones_rows(vwnt_ref[v_rows, pl.ds(start, tq)])
        return m_new, jnp.exp2(m_prev - m_new) * acc + _dot(vt, p.astype(BF16))

    n_back = NSA_WINDOW // tq

    @pl.when(i < n_back)
    def _():
        _, acc_w = lax.fori_loop(0, i + 1, win_chunk, init)
        ow_ref[...] = normalised(acc_w)

    @pl.when(i >= n_back)
    def _():
        start = pl.multiple_of(t0 - NSA_WINDOW, tq)
        s = _dot_nt(kwn_ref[0, pl.ds(start, NSA_WINDOW + tq), :], q4)
        kpos = start + lax.broadcasted_iota(jnp.int32, (tq, width), 0)
        qpos = qpos_of((tq, width))
        sw_ref[0:tq, :] = jnp.where(kpos > qpos - NSA_WINDOW, s[0:tq, :], NEG_INF)
        sw_ref[tq:NSA_WINDOW, :] = s[tq:NSA_WINDOW, :]
        sw_ref[NSA_WINDOW:, :] = jnp.where(kpos + NSA_WINDOW <= qpos, s[NSA_WINDOW:, :], NEG_INF)
        m_w = jnp.max(sw_ref[...], axis=0, keepdims=True)
        for r in range((NSA_WINDOW + tq) // sub):
            pw_ref[r * sub:(r + 1) * sub, :] = jnp.exp2(sw_ref[r * sub:(r + 1) * sub, :] - m_w).astype(BF16)
        vt = _with_ones_rows(vwnt_ref[v_rows, pl.ds(start, NSA_WINDOW + tq)])
        ow_ref[...] = normalised(_dot(vt, pw_ref[...]))

    o_w = ow_ref[...]

    def gate(branch):
        rows = [glt_ref[pl.ds((g * rep + r) * 3 + branch, 1), :] for r in range(rep)]
        return _sigmoid(jnp.concatenate(rows, axis=1))

    out = gate(0) * o_c + gate(1) * o_s + gate(2) * o_w
    out_t = jnp.concatenate([out, jnp.zeros_like(out)], axis=0).T
    o_ref[0] = jnp.concatenate([out_t[r * tq:(r + 1) * tq, :d] for r in range(rep)],
                               axis=1).astype(o_ref.dtype)


def nsa_overlap_t(n_cmp, n_sel):
    c_start = np.arange(n_cmp)[None, :] * NSA_CMP_STRIDE
    s_start = np.arange(n_sel)[:, None] * NSA_SEL_BLOCK
    hit = (c_start < s_start + NSA_SEL_BLOCK) & (c_start + NSA_CMP_BLOCK > s_start)
    hit = hit & (np.arange(n_cmp)[None, :] < n_cmp - NSA_CMP_BLOCK // NSA_CMP_STRIDE + 1)
    return jnp.asarray(hit.astype(np.float32), dtype=BF16)


def nsa_attention(qn, ck, cvt, ksl, vslt, kwn, vwnt, glt, tq=128):
    b, s, _ = qn.shape
    g, d = NSA_KV_GROUPS, NSA_HEAD_DIM
    rep = NSA_HEADS // g
    n_cmp = ck.shape[1]
    n_sel = s // NSA_SEL_BLOCK
    nq = s // tq
    ovt = nsa_overlap_t(n_cmp, n_sel)
    width = rep * tq
    chunk = 8 * NSA_SEL_BLOCK
    full3 = lambda bb, gg, i: (bb, 0, 0)
    seq_t = lambda bb, gg, i: (0, bb)
    return pl.pallas_call(
        functools.partial(_nsa_body, tq=tq), grid=(b, g, nq),
        in_specs=[pl.BlockSpec((1, tq, rep * d), lambda bb, gg, i: (bb, i, gg)),
                  pl.BlockSpec((1, n_cmp, g * d), full3), pl.BlockSpec((1, g * d, n_cmp), full3),
                  pl.BlockSpec((1, s, g * d), full3), pl.BlockSpec((g * d, s), seq_t),
                  pl.BlockSpec((1, s, g * d), full3), pl.BlockSpec((g * d, s), seq_t),
                  pl.BlockSpec((n_sel, n_cmp), lambda bb, gg, i: (0, 0)),
                  pl.BlockSpec((glt.shape[0], tq), lambda bb, gg, i: (0, bb * nq + i))],
        out_specs=pl.BlockSpec((1, tq, rep * d), lambda bb, gg, i: (bb, i, gg)),
        out_shape=jax.ShapeDtypeStruct((b, s, g * rep * d), BF16),
        scratch_shapes=[pltpu.VMEM((n_cmp, width), F32), pltpu.VMEM((n_cmp, width), BF16),
                        pltpu.VMEM((n_cmp, width), BF16), pltpu.VMEM((n_sel, tq), F32),
                        pltpu.VMEM((n_sel, width), F32), pltpu.VMEM((chunk, width), F32),
                        pltpu.VMEM((chunk, width), BF16), pltpu.VMEM((chunk, width), F32),
                        pltpu.VMEM((chunk, width), BF16), pltpu.VMEM((NSA_WINDOW + tq, width), F32),
                        pltpu.VMEM((NSA_WINDOW + tq, width), BF16), pltpu.VMEM((d, width), F32),
                        pltpu.VMEM((d, width), F32)],
        compiler_params=_cparams(("parallel", "parallel", "arbitrary")), name="nsa_attention",
    )(qn, ck, cvt, ksl, vslt, kwn, vwnt, ovt, glt)


def _pad_cols(w, n):
    return jnp.pad(w, ((0, 0), (0, n - w.shape[1])))


def _even_layer(x2, b, s, layer_idx, norm_mix, w_in, q_gain, k_gain, lam, subln_gain, conv_w, conv_b,
                dt_bias, a_log, d_skip, ssm_norm_gain, w_out, norm_ffn, w_gate, w_up, w_down):
    nq = DA_HEADS * 2 * DA_HEAD_DIM
    nv = DA_HEADS * DA_V_DIM
    cch = SSM_D_INNER + 2 * SSM_GROUPS * SSM_STATE
    offs = np.cumsum([0, nq, nq, nv, SSM_D_INNER, cch, SSM_HEADS])
    wb = w_in.astype(BF16)
    pieces = [wb[:, offs[k]:offs[k + 1]] for k in range(6)]
    pieces[2] = pieces[2].T
    pieces[5] = _pad_cols(pieces[5], LANES)
    posts = [HeadNorm(q_gain, DA_HEAD_DIM, rope=True, mul=DA_HEAD_DIM ** -0.5 * LOG2E),
             HeadNorm(k_gain, DA_HEAD_DIM, rope=True), None, None, None, None]
    q, k, vt, z, xbc, dt = norm_proj(x2, norm_mix, pieces, [BF16, BF16, BF16, F32, F32, F32], posts, s,
                                     _rope_tables(s, DA_HEAD_DIM), transposed=(2,))
    qn = q.reshape(b, s, nq)
    kn = k.reshape(b, s, nq)
    lam_init = 0.8 - 0.6 * math.exp(-0.3 * layer_idx)
    lf = lam.astype(F32)
    lam_full = jnp.exp(jnp.sum(lf[0] * lf[1])) - jnp.exp(jnp.sum(lf[2] * lf[3])) + lam_init
    a_out = flash_attention(lam_full.reshape(1), [qn], [kn], vt, subln_gain, DA_HEADS, DA_V_DIM,
                            diff=True, out_scale=1.0 - lam_init)
    b_out = ssd_mixer(xbc.reshape(b, s, cch), z.reshape(b, s, SSM_D_INNER), dt.reshape(b, s, LANES),
                      conv_w, conv_b, dt_bias, a_log, d_skip, ssm_norm_gain)
    wo = w_out.astype(BF16)
    x2 = out_proj_residual(x2, a_out.reshape(-1, nv), b_out.reshape(-1, SSM_D_INNER), wo[:nv], wo[nv:])
    return ffn_residual(x2, norm_ffn, w_gate.astype(BF16), w_up.astype(BF16), w_down.astype(BF16))


def _odd_layer(x2, b, s, norm_mix, w_in, q_gain, k_gain, cmp_pos, cmp_w1, cmp_w2, cq_gain, ckv_gain,
               w_uq, w_ukv, qn_gain, qr_gain, kn_gain, kr_gain, w_out, norm_ffn, router_w, w_gate, w_up,
               w_down):
    g, d = NSA_KV_GROUPS, NSA_HEAD_DIM
    nq = NSA_HEADS * d
    nkv = g * d
    sizes = [nq] + [nkv] * 6 + [NSA_HEADS * 3, w_uq.shape[0], w_ukv.shape[0], MLA_ROPE_DIM]
    offs = np.cumsum([0] + sizes)
    wb = w_in.astype(BF16)
    pieces = [wb[:, offs[k]:offs[k + 1]] for k in range(len(sizes))]
    for k in (4, 6):
        pieces[k] = pieces[k].T
    pieces[7] = jnp.pad(pieces[7].T, ((0, 32 - NSA_HEADS * 3), (0, 0)))
    pieces[10] = _pad_cols(pieces[10], LANES)
    tables = _rope_tables(s, d)
    posts = [None] * len(sizes)
    posts[0] = HeadNorm(q_gain, d, rope=True, mul=d ** -0.5 * LOG2E)
    posts[3] = HeadNorm(k_gain[1], d, rope=True)
    posts[5] = HeadNorm(k_gain[2], d, rope=True)
    posts[10] = HeadNorm(kr_gain, MLA_ROPE_DIM, rope=True)
    (q, kc, vc, ksl, vslt, kwn, vwnt, glt, cq, ckv, k_rope) = norm_proj(
        x2, norm_mix, pieces, [BF16, F32, F32, BF16, BF16, BF16, BF16, F32, F32, F32, BF16], posts, s, tables,
        transposed=(4, 6, 7))

    qn = q.reshape(b, s, nq)
    ksl_n = ksl.reshape(b, s, nkv)
    kwn_n = kwn.reshape(b, s, nkv)
    ck = nsa_compress(kc.reshape(b, s, nkv), cmp_pos[0], cmp_w1[0], cmp_w2[0], k_gain[0], s, True)
    cv = nsa_compress(vc.reshape(b, s, nkv), cmp_pos[1], cmp_w1[1], cmp_w2[1], k_gain[0], s, False)
    n_cmp = ck.shape[2]
    ck = ck.transpose(0, 2, 1, 3).reshape(b, n_cmp, nkv).astype(BF16)
    cvt = cv.transpose(0, 1, 3, 2).reshape(b, nkv, n_cmp).astype(BF16)
    c_out = nsa_attention(qn, ck, cvt, ksl_n, vslt, kwn_n, vwnt, glt).reshape(b * s, nq)

    h = MLA_HEADS
    dqk = MLA_NOPE_DIM + MLA_ROPE_DIM
    wq = w_uq.astype(BF16).reshape(-1, h, dqk)
    wq_nope = wq[:, :, :MLA_NOPE_DIM].reshape(-1, h * MLA_NOPE_DIM)
    wq_rope = jnp.pad(wq[:, :, MLA_NOPE_DIM:], ((0, 0), (0, 0), (0, LANES - MLA_ROPE_DIM)))
    wq_rope = wq_rope.reshape(-1, h * LANES)
    wkv = w_ukv.astype(BF16).reshape(-1, h, MLA_NOPE_DIM + MLA_V_DIM)
    wk_nope = wkv[:, :, :MLA_NOPE_DIM].reshape(-1, h * MLA_NOPE_DIM)
    wv = wkv[:, :, MLA_NOPE_DIM:].reshape(-1, h * MLA_V_DIM)
    q_mul = dqk ** -0.5 * LOG2E
    q_nope, q_rope = norm_proj(
        cq, cq_gain, [wq_nope, wq_rope], [BF16, BF16],
        [HeadNorm(qn_gain, MLA_NOPE_DIM, mul=q_mul), HeadNorm(qr_gain, MLA_ROPE_DIM, rope=True, mul=q_mul)],
        s, tables)
    k_nope, vt = norm_proj(ckv, ckv_gain, [wk_nope, wv.T], [BF16, BF16],
                           [HeadNorm(kn_gain, MLA_NOPE_DIM), None], transposed=(1,))
    shp = lambda t: t.reshape(b, s, t.shape[-1])
    d_out = flash_attention(jnp.zeros((1,), F32), [shp(q_nope), shp(q_rope)], [shp(k_nope), shp(k_rope)],
                            vt, jnp.ones((MLA_V_DIM,), F32), h, MLA_V_DIM, diff=False)

    wo = w_out.astype(BF16)
    x2 = out_proj_residual(x2, c_out, d_out.reshape(b * s, h * MLA_V_DIM), wo[:nq], wo[nq:])
    h, route, counts = router(x2, norm_ffn, router_w)
    return moe_residual(x2, h, route, counts, cast_bf16(w_gate), cast_bf16(w_up), cast_bf16(w_down))


def kernel(x, ev_norm_mix, ev_w_in, da_q_gain, da_k_gain, da_lambda, da_subln_gain, ssm_conv_w, ssm_conv_b, ssm_dt_bias, ssm_a_log, ssm_d, ssm_norm_gain, ev_w_out, ev_norm_ffn, ffn_w_gate, ffn_w_up, ffn_w_down, od_norm_mix, od_w_in, nsa_q_gain, nsa_k_gain, nsa_cmp_pos, nsa_cmp_w1, nsa_cmp_w2, mla_cq_gain, mla_ckv_gain, mla_w_uq, mla_w_ukv, mla_qn_gain, mla_qr_gain, mla_kn_gain, mla_kr_gain, od_w_out, od_norm_ffn, moe_router, moe_w_gate, moe_w_up, moe_w_down):
    b, s, d = x.shape
    x2 = x.reshape(b * s, d)
    depth = ev_norm_mix.shape[0] + od_norm_mix.shape[0]
    for layer in range(depth):
        i = layer // 2
        if layer % 2 == 0:
            x2 = _even_layer(x2, b, s, layer, ev_norm_mix[i], ev_w_in[i], da_q_gain[i], da_k_gain[i],
                             da_lambda[i], da_subln_gain[i], ssm_conv_w[i], ssm_conv_b[i],
                             ssm_dt_bias[i], ssm_a_log[i], ssm_d[i], ssm_norm_gain[i], ev_w_out[i],
                             ev_norm_ffn[i], ffn_w_gate[i], ffn_w_up[i], ffn_w_down[i])
        else:
            x2 = _odd_layer(x2, b, s, od_norm_mix[i], od_w_in[i], nsa_q_gain[i], nsa_k_gain[i],
                            nsa_cmp_pos[i], nsa_cmp_w1[i], nsa_cmp_w2[i], mla_cq_gain[i],
                            mla_ckv_gain[i], mla_w_uq[i], mla_w_ukv[i], mla_qn_gain[i], mla_qr_gain[i],
                            mla_kn_gain[i], mla_kr_gain[i], od_w_out[i], od_norm_ffn[i], moe_router[i],
                            moe_w_gate[i], moe_w_up[i], moe_w_down[i])
    return x2.reshape(b, s, d)
```

```python
import functools
import math

import numpy as np
import jax
import jax.numpy as jnp
from jax import lax
from jax.experimental import pallas as pl
from jax.experimental.pallas import tpu as pltpu

F32 = jnp.float32
BF16 = jnp.bfloat16

ROPE_THETA = 10000.0
NORM_EPS = 1e-6
NEG_INF = -1e30
FORCE_SCORE = 1e6
LOG2E = 1.4426950408889634

DA_HEADS = 4
DA_HEAD_DIM = 64
DA_V_DIM = 2 * DA_HEAD_DIM
SSM_HEADS = 8
SSM_HEAD_DIM = 64
SSM_D_INNER = SSM_HEADS * SSM_HEAD_DIM
SSM_GROUPS = 2
SSM_STATE = 128
SSM_CONV = 4
SSM_CHUNK = 256
NSA_HEADS = 8
NSA_KV_GROUPS = 2
NSA_HEAD_DIM = 64
NSA_CMP_BLOCK = 32
NSA_CMP_STRIDE = 16
NSA_SEL_BLOCK = 64
NSA_TOP_N = 16
NSA_WINDOW = 512
MLA_HEADS = 4
MLA_NOPE_DIM = 128
MLA_ROPE_DIM = 64
MLA_V_DIM = 128
N_EXPERTS = 8

LANES = 128
SUBLANES = 8
VMEM_LIMIT = 48 * 1024 * 1024
MOE_VMEM_LIMIT = 58 * 1024 * 1024

NT_DIMS = (((1,), (1,)), ((), ()))


def _cparams(semantics):
    return pltpu.CompilerParams(dimension_semantics=semantics, vmem_limit_bytes=VMEM_LIMIT)


def _dot(a, b):
    return jnp.dot(a, b, preferred_element_type=F32)


def _dot_nt(a, b):
    return lax.dot_general(a, b, NT_DIMS, preferred_element_type=F32)


def _split_bf16(x, parts):
    out = []
    for _ in range(parts):
        hi = x.astype(BF16)
        out.append(hi)
        x = x - hi.astype(F32)
    return out


def _fold_rows(x):
    return jnp.sum(x.reshape(x.shape[0] // SUBLANES, SUBLANES, x.shape[1]), axis=0)


ONES_ROWS = 16


def _with_ones_rows(vt):
    return jnp.concatenate([vt, jnp.ones((ONES_ROWS, vt.shape[1]), vt.dtype)], axis=0)


def _sigmoid(x):
    return 1.0 / (1.0 + jnp.exp(-x))


def _silu(x):
    return x * _sigmoid(x)


def _softplus(x):
    return jnp.maximum(x, 0.0) + jnp.log(1.0 + jnp.exp(-jnp.abs(x)))


def _rms(x, gain):
    ms = jnp.mean(x * x, axis=-1, keepdims=True)
    return x * lax.rsqrt(ms + NORM_EPS) * gain


class HeadNorm:
    def __init__(self, gain, hd, rope=False, mul=1.0):
        self.gain, self.hd, self.rope, self.mul = gain, hd, rope, mul


def _head_norm(y, gain, bd, cos_ref, sin_ref, post):
    n = y.shape[1]
    hd = post.hd
    hi, lo = _split_bf16(y * y, 2)
    ss = _dot(hi, bd) + _dot(lo, bd)
    yn = y * lax.rsqrt(ss * (1.0 / hd) + NORM_EPS) * gain
    if post.rope:
        reps = n // LANES
        cos = jnp.concatenate([cos_ref[...]] * reps, axis=1) if reps > 1 else cos_ref[...]
        sin = jnp.concatenate([sin_ref[...]] * reps, axis=1) if reps > 1 else sin_ref[...]
        lane = lax.broadcasted_iota(jnp.int32, yn.shape, 1)
        first_half = (lane & (hd - 1)) < (hd // 2)
        partner = jnp.where(first_half, pltpu.roll(yn, n - hd // 2, 1), pltpu.roll(yn, hd // 2, 1))
        yn = yn * cos + partner * sin
    if post.mul != 1.0:
        yn = yn * post.mul
    return yn


def _norm_proj_body(x_ref, g_ref, *refs, posts, use_rope, transposed):
    if use_rope:
        cos_ref, sin_ref = refs[0], refs[1]
        refs = refs[2:]
    else:
        cos_ref = sin_ref = None
    n_out = len(posts)
    n_aux = 2 * sum(p is not None for p in posts)
    w_refs, aux, o_refs = refs[:n_out], refs[n_out:n_out + n_aux], refs[n_out + n_aux:]
    h = _rms(x_ref[...], g_ref[...]).astype(BF16)
    a = 0
    for k, (w_ref, o_ref, post) in enumerate(zip(w_refs, o_refs, posts)):
        if k in transposed:
            o_ref[...] = _dot_nt(w_ref[...], h).astype(o_ref.dtype)
            continue
        y = _dot(h, w_ref[...])
        if post is not None:
            y = _head_norm(y, aux[a][...], aux[a + 1][...], cos_ref, sin_ref, post)
            a += 2
        o_ref[...] = y.astype(o_ref.dtype)


def norm_proj(x2, gain, weights, out_dtypes, posts=None, seq=None, rope_tables=None, transposed=(), tm=512):
    t, d = x2.shape
    posts = posts or [None] * len(weights)
    transposed = frozenset(transposed)
    use_rope = any(p is not None and p.rope for p in posts)
    const = lambda i: (0, 0)
    args = [x2, gain.reshape(1, d).astype(F32)]
    in_specs = [pl.BlockSpec((tm, d), lambda i: (i, 0)), pl.BlockSpec((1, d), const)]
    if use_rope:
        per_seq = seq // tm
        args += list(rope_tables)
        in_specs += [pl.BlockSpec((tm, LANES), lambda i: (i % per_seq, 0))] * 2
    args += list(weights)
    in_specs += [pl.BlockSpec(w.shape, const) for w in weights]
    for w, p in zip(weights, posts):
        if p is not None:
            n = w.shape[1]
            args += [jnp.tile(p.gain.astype(F32), n // p.hd).reshape(1, n), _block_diag_ones(n, p.hd)]
            in_specs += [pl.BlockSpec((1, n), const), pl.BlockSpec((n, n), const)]
    out_specs, out_shape = [], []
    for k, (w, dt) in enumerate(zip(weights, out_dtypes)):
        if k in transposed:
            out_specs.append(pl.BlockSpec((w.shape[0], tm), lambda i: (0, i)))
            out_shape.append(jax.ShapeDtypeStruct((w.shape[0], t), dt))
        else:
            out_specs.append(pl.BlockSpec((tm, w.shape[1]), lambda i: (i, 0)))
            out_shape.append(jax.ShapeDtypeStruct((t, w.shape[1]), dt))
    return pl.pallas_call(
        functools.partial(_norm_proj_body, posts=tuple(posts), use_rope=use_rope, transposed=transposed),
        grid=(t // tm,), in_specs=in_specs, out_specs=out_specs, out_shape=out_shape,
        compiler_params=_cparams(("parallel",)), name="norm_proj",
    )(*args)


def _block_diag_ones(n, hd):
    idx = np.arange(n) // hd
    return jnp.asarray((idx[:, None] == idx[None, :]).astype(np.float32), dtype=BF16)


def _rope_tables(seq, hd):
    inv_freq = 1.0 / (ROPE_THETA ** (jnp.arange(0, hd, 2, dtype=F32) / hd))
    ang = jnp.arange(seq, dtype=F32)[:, None] * inv_freq[None, :]
    cos, sin = jnp.cos(ang), jnp.sin(ang)
    reps = LANES // hd
    cos_t = jnp.tile(jnp.concatenate([cos, cos], axis=1), (1, reps))
    sin_t = jnp.tile(jnp.concatenate([-sin, sin], axis=1), (1, reps))
    return cos_t, sin_t


def _flash_body(lam_ref, *refs, n_qk, diff, out_scale, sub):
    q_refs = refs[:n_qk]
    k_refs = refs[n_qk:2 * n_qk]
    vt_ref, gain_ref, o_ref, m_ref, l_ref, acc_ref, s0_ref, s1_ref, p0_ref, p1_ref = refs[2 * n_qk:]
    i = pl.program_id(2)
    n_sm = 2 if diff else 1
    _, tk, tq = s0_ref.shape
    s_slots = (s0_ref, s1_ref)
    p_slots = (p0_ref, p1_ref)

    m_ref[...] = jnp.full(m_ref.shape, NEG_INF, F32)
    l_ref[...] = jnp.zeros(l_ref.shape, F32)
    acc_ref[...] = jnp.zeros(acc_ref.shape, F32)

    qs = [r[0] for r in q_refs]
    q = qs[0] if n_qk == 1 else jnp.concatenate(qs, axis=1)
    if diff:
        lane = lax.broadcasted_iota(jnp.int32, q.shape, 1)
        half = q.shape[1] // 2
        zero = jnp.zeros_like(q)
        q_parts = [jnp.where(lane < half, q, zero), jnp.where(lane >= half, q, zero)]
    else:
        q_parts = [q]

    def scores(c, slot, diagonal=False):
        rows = pl.ds(pl.multiple_of(c * tk, tk), tk)
        ks = [r[0, rows, :] for r in k_refs]
        k = ks[0] if n_qk == 1 else jnp.concatenate(ks, axis=1)
        for sm in range(n_sm):
            s = _dot_nt(k, q_parts[sm])
            if diagonal:
                row = lax.broadcasted_iota(jnp.int32, s.shape, 0)
                col = lax.broadcasted_iota(jnp.int32, s.shape, 1)
                s = jnp.where(row <= col, s, NEG_INF)
            s_slots[slot][sm] = s

    def update(c, slot):
        vt = vt_ref[:, pl.ds(pl.multiple_of(c * tk, tk), tk)]
        for sm in range(n_sm):
            s_ref, p_ref = s_slots[slot], p_slots[slot]
            m_prev = m_ref[sm]
            m_new = jnp.maximum(m_prev, jnp.max(s_ref[sm], axis=0, keepdims=True))
            m_ref[sm] = m_new
            alpha = jnp.exp2(m_prev - m_new)
            lpart = jnp.zeros((SUBLANES, tq), F32)
            for r in range(tk // sub):
                p = jnp.exp2(s_ref[sm, r * sub:(r + 1) * sub, :] - m_new)
                lpart = lpart + _fold_rows(p)
                p_ref[sm, r * sub:(r + 1) * sub, :] = p.astype(BF16)
            l_ref[sm] = alpha * l_ref[sm] + jnp.sum(lpart, axis=0, keepdims=True)
            acc_ref[sm] = alpha * acc_ref[sm] + _dot(vt, p_ref[sm])

    n_chunks = i + 1
    last_past = jnp.maximum(i - 1, 0)

    def chunk_at(j):
        return jnp.where(j == 0, i, j - 1)

    scores(i, 0, diagonal=True)

    def pair(k2, carry):
        scores(jnp.minimum(2 * k2, last_past), 1)
        update(chunk_at(2 * k2), 0)
        scores(jnp.minimum(2 * k2 + 1, last_past), 0)
        update(2 * k2, 1)
        return carry

    lax.fori_loop(0, n_chunks // 2, pair, 0)

    @pl.when(n_chunks % 2 == 1)
    def _():
        update(chunk_at(n_chunks - 1), 0)

    o = acc_ref[0] / l_ref[0]
    if diff:
        o = o - lam_ref[0] * (acc_ref[1] / l_ref[1])
        ms = jnp.mean(o * o, axis=0, keepdims=True)
        o = o * lax.rsqrt(ms + NORM_EPS) * gain_ref[...] * out_scale
    o_ref[0] = o.T.astype(o_ref.dtype)


def flash_attention(lam, qs, ks, vt, gain, n_heads, dv, *, diff, out_scale=1.0, tile=512, sub=64):
    b, s, _ = qs[0].shape
    nt = s // tile
    n_qk = len(qs)
    in_specs = [pl.BlockSpec(memory_space=pltpu.SMEM)]
    for q in qs:
        w = q.shape[2] // n_heads
        in_specs.append(pl.BlockSpec((1, tile, w), lambda bb, h, i: (bb, i, h)))
    for q, k in zip(qs, ks):
        w = q.shape[2] // n_heads
        if k.shape[2] == w:
            in_specs.append(pl.BlockSpec((1, s, w), lambda bb, h, i: (bb, 0, 0)))
        else:
            in_specs.append(pl.BlockSpec((1, s, w), lambda bb, h, i: (bb, 0, h)))
    in_specs.append(pl.BlockSpec((dv, s), lambda bb, h, i: (h, bb)))
    in_specs.append(pl.BlockSpec((dv, 1), lambda bb, h, i: (0, 0)))
    n_sm = 2 if diff else 1
    return pl.pallas_call(
        functools.partial(_flash_body, n_qk=n_qk, diff=diff, out_scale=out_scale, sub=sub),
        grid=(b, n_heads, nt), in_specs=in_specs,
        out_specs=pl.BlockSpec((1, tile, dv), lambda bb, h, i: (bb, i, h)),
        out_shape=jax.ShapeDtypeStruct((b, s, n_heads * dv), BF16),
        scratch_shapes=[pltpu.VMEM((n_sm, 1, tile), F32), pltpu.VMEM((n_sm, 1, tile), F32),
                        pltpu.VMEM((n_sm, dv, tile), F32),
                        pltpu.VMEM((n_sm, tile, tile), F32), pltpu.VMEM((n_sm, tile, tile), F32),
                        pltpu.VMEM((n_sm, tile, tile), BF16), pltpu.VMEM((n_sm, tile, tile), BF16)],
        compiler_params=_cparams(("parallel", "parallel", "arbitrary")),
        name="flash_diff" if diff else "flash_plain",
    )(lam, *qs, *ks, vt, gain.reshape(dv, 1).astype(F32))


def _ssd_body(xbc_ref, z_ref, dt_ref, dtt_ref, cw_ref, cb_ref, dtb_ref, dtbt_ref, al_ref, alt_ref,
              dsk_ref, ng_ref, o_ref, xpad_ref, state_ref):
    chunk = xbc_ref.shape[1]
    d_in = z_ref.shape[2]
    gn = SSM_GROUPS * SSM_STATE
    c = pl.program_id(1)

    @pl.when(c == 0)
    def _():
        xpad_ref[0:8, :] = jnp.zeros((8, xpad_ref.shape[1]), F32)
        state_ref[...] = jnp.zeros(state_ref.shape, F32)

    xpad_ref[8:8 + chunk, :] = xbc_ref[0]
    conv = cb_ref[...]
    for w in range(SSM_CONV):
        conv = conv + cw_ref[w:w + 1, :] * xpad_ref[pl.ds(8 - (SSM_CONV - 1) + w, chunk), :]
    xpad_ref[0:8, :] = xpad_ref[chunk:chunk + 8, :]
    u = _silu(conv)
    xs = u[:, :d_in]
    bmat = u[:, d_in:d_in + gn]
    cmat = u[:, d_in + gn:]

    dt = _softplus(dt_ref[0] + dtb_ref[...])
    ad = dt * (-jnp.exp(al_ref[...]))
    dtt = _softplus(dtt_ref[0] + dtbt_ref[...])
    adt = dtt * (-jnp.exp(alt_ref[...]))
    row = lax.broadcasted_iota(jnp.int32, (chunk, chunk), 0)
    col = lax.broadcasted_iota(jnp.int32, (chunk, chunk), 1)
    lower = row >= col
    tril = jnp.where(lower, 1.0, 0.0).astype(BF16)
    triu = jnp.where(row <= col, 1.0, 0.0).astype(BF16)
    cs = sum(_dot(tril, part) for part in _split_bf16(ad, 3))
    cst = sum(_dot(part, triu) for part in _split_bf16(adt, 3))

    heads_per_group = SSM_HEADS // SSM_GROUPS
    dsk = dsk_ref[...]
    ys = []
    for g in range(SSM_GROUPS):
        bg = bmat[:, g * SSM_STATE:(g + 1) * SSM_STATE]
        cg = cmat[:, g * SSM_STATE:(g + 1) * SSM_STATE].astype(BF16)
        cb = _dot_nt(cg, bg.astype(BF16))
        bgt = bg.T.astype(BF16)
        for r in range(heads_per_group):
            h = g * heads_per_group + r
            ccol = cs[:, h:h + 1]
            crow = cst[h:h + 1, :]
            decay = jnp.exp(jnp.where(lower, ccol - crow, NEG_INF))
            x_h = xs[:, h * SSM_HEAD_DIM:(h + 1) * SSM_HEAD_DIM]
            xdt = x_h * dt[:, h:h + 1]
            y = _dot((cb * decay).astype(BF16), xdt.astype(BF16))
            st = state_ref[h]
            y = y + _dot(cg, st.astype(BF16)) * jnp.exp(ccol)
            last = cst[h:h + 1, chunk - 1:chunk]
            to_end = jnp.exp(last - ccol)
            state_ref[h] = st * jnp.exp(last) + _dot(bgt, (xdt * to_end).astype(BF16))
            ys.append(y + x_h * dsk[:, h * SSM_HEAD_DIM:(h + 1) * SSM_HEAD_DIM])

    y = jnp.concatenate(ys, axis=1) * _silu(z_ref[0])
    gw = d_in // SSM_GROUPS
    for g in range(SSM_GROUPS):
        seg = y[:, g * gw:(g + 1) * gw]
        o_ref[0, :, g * gw:(g + 1) * gw] = _rms(seg, ng_ref[:, g * gw:(g + 1) * gw]).astype(o_ref.dtype)


def ssd_mixer(xbc, z, dt_raw, conv_w, conv_b, dt_bias, a_log, d_skip, norm_gain):
    b, s, cch = xbc.shape
    d_in = z.shape[2]
    nc = s // SSM_CHUNK
    hpad = dt_raw.shape[2]
    dtt = jnp.transpose(dt_raw[:, :, :SSM_HEADS], (0, 2, 1))

    def lane_pad(v):
        return jnp.pad(v.astype(F32), (0, hpad - SSM_HEADS)).reshape(1, hpad)

    args = (xbc, z, dt_raw, dtt, conv_w.astype(F32), conv_b.reshape(1, cch).astype(F32),
            lane_pad(dt_bias), dt_bias.reshape(SSM_HEADS, 1).astype(F32),
            lane_pad(a_log), a_log.reshape(SSM_HEADS, 1).astype(F32),
            jnp.repeat(d_skip.astype(F32), SSM_HEAD_DIM).reshape(1, d_in),
            norm_gain.reshape(1, d_in).astype(F32))
    const = lambda bb, c: (0, 0)
    in_specs = [pl.BlockSpec((1, SSM_CHUNK, cch), lambda bb, c: (bb, c, 0)),
                pl.BlockSpec((1, SSM_CHUNK, d_in), lambda bb, c: (bb, c, 0)),
                pl.BlockSpec((1, SSM_CHUNK, hpad), lambda bb, c: (bb, c, 0)),
                pl.BlockSpec((1, SSM_HEADS, SSM_CHUNK), lambda bb, c: (bb, 0, c)),
                pl.BlockSpec((SSM_CONV, cch), const), pl.BlockSpec((1, cch), const),
                pl.BlockSpec((1, hpad), const), pl.BlockSpec((SSM_HEADS, 1), const),
                pl.BlockSpec((1, hpad), const), pl.BlockSpec((SSM_HEADS, 1), const),
                pl.BlockSpec((1, d_in), const), pl.BlockSpec((1, d_in), const)]
    return pl.pallas_call(
        _ssd_body, grid=(b, nc), in_specs=in_specs,
        out_specs=pl.BlockSpec((1, SSM_CHUNK, d_in), lambda bb, c: (bb, c, 0)),
        out_shape=jax.ShapeDtypeStruct((b, s, d_in), BF16),
        scratch_shapes=[pltpu.VMEM((SSM_CHUNK + 8, cch), F32),
                        pltpu.VMEM((SSM_HEADS, SSM_STATE, SSM_HEAD_DIM), F32)],
        compiler_params=_cparams(("parallel", "arbitrary")), name="ssd_mixer",
    )(*args)


def _out_proj_body(x_ref, a_ref, b_ref, wa_ref, wb_ref, o_ref):
    o_ref[...] = x_ref[...] + _dot(a_ref[...], wa_ref[...]) + _dot(b_ref[...], wb_ref[...])


def out_proj_residual(x2, a, bm, wa, wb, tm=512):
    t, d = x2.shape
    return pl.pallas_call(
        _out_proj_body, grid=(t // tm,),
        in_specs=[pl.BlockSpec((tm, d), lambda i: (i, 0)),
                  pl.BlockSpec((tm, a.shape[1]), lambda i: (i, 0)),
                  pl.BlockSpec((tm, bm.shape[1]), lambda i: (i, 0)),
                  pl.BlockSpec(wa.shape, lambda i: (0, 0)),
                  pl.BlockSpec(wb.shape, lambda i: (0, 0))],
        out_specs=pl.BlockSpec((tm, d), lambda i: (i, 0)),
        out_shape=jax.ShapeDtypeStruct((t, d), F32),
        compiler_params=_cparams(("parallel",)), name="out_proj",
    )(x2, a, bm, wa, wb)


def _ffn_body(x_ref, g_ref, wg_ref, wu_ref, wd_ref, o_ref, h_ref):
    f = pl.program_id(1)

    @pl.when(f == 0)
    def _():
        x = x_ref[...]
        h_ref[...] = _rms(x, g_ref[...]).astype(BF16)
        o_ref[...] = x

    half = h_ref.shape[0] // 2
    for r in (slice(0, half), slice(half, 2 * half)):
        h = h_ref[r, :]
        act = (_silu(_dot(h, wg_ref[...])) * _dot(h, wu_ref[...])).astype(BF16)
        o_ref[r, :] += _dot(act, wd_ref[...])


def ffn_residual(x2, gain, w_gate, w_up, w_down, tm=1024, tf=1408):
    t, d = x2.shape
    d_ff = w_gate.shape[1]
    return pl.pallas_call(
        _ffn_body, grid=(t // tm, d_ff // tf),
        in_specs=[pl.BlockSpec((tm, d), lambda i, f: (i, 0)),
                  pl.BlockSpec((1, d), lambda i, f: (0, 0)),
                  pl.BlockSpec((d, tf), lambda i, f: (0, f)),
                  pl.BlockSpec((d, tf), lambda i, f: (0, f)),
                  pl.BlockSpec((tf, d), lambda i, f: (f, 0))],
        out_specs=pl.BlockSpec((tm, d), lambda i, f: (i, 0)),
        out_shape=jax.ShapeDtypeStruct((t, d), F32),
        scratch_shapes=[pltpu.VMEM((tm, d), BF16)],
        compiler_params=_cparams(("parallel", "arbitrary")), name="ffn",
    )(x2, gain.reshape(1, d).astype(F32), w_gate, w_up, w_down)


MOE_ROWS = 256
ROUTE_IDX = 0
ROUTE_W = 2
ROUTE_RANK = 4


def _moe_ffn_body(block_expert_ref, n_used_ref, xs_ref, wg_ref, wu_ref, wd_ref, o_ref):
    i = pl.program_id(0)

    @pl.when(i < n_used_ref[0])
    def _():
        x = xs_ref[...]
        act = (_silu(_dot(x, wg_ref[0])) * _dot(x, wu_ref[0])).astype(BF16)
        o_ref[...] = _dot(act, wd_ref[0]).astype(o_ref.dtype)

    @pl.when(i >= n_used_ref[0])
    def _():
        o_ref[...] = jnp.zeros(o_ref.shape, o_ref.dtype)


def moe_expert_ffn(xs, block_expert, n_used, w_gate, w_up, w_down):
    p, d = xs.shape
    d_ff = w_gate.shape[2]
    rows = MOE_ROWS
    grid_spec = pltpu.PrefetchScalarGridSpec(
        num_scalar_prefetch=2, grid=(p // rows,),
        in_specs=[pl.BlockSpec((rows, d), lambda i, be, nu: (i, 0)),
                  pl.BlockSpec((1, d, d_ff), lambda i, be, nu: (be[i], 0, 0)),
                  pl.BlockSpec((1, d, d_ff), lambda i, be, nu: (be[i], 0, 0)),
                  pl.BlockSpec((1, d_ff, d), lambda i, be, nu: (be[i], 0, 0))],
        out_specs=pl.BlockSpec((rows, d), lambda i, be, nu: (i, 0)))
    return pl.pallas_call(
        _moe_ffn_body, grid_spec=grid_spec, out_shape=jax.ShapeDtypeStruct((p, d), BF16),
        compiler_params=pltpu.CompilerParams(
            dimension_semantics=("arbitrary",), vmem_limit_bytes=MOE_VMEM_LIMIT),
        name="moe_expert_ffn",
    )(block_expert, n_used, xs, w_gate, w_up, w_down)


def _moe_combine_body(x_ref, y0_ref, y1_ref, route_ref, o_ref):
    route = route_ref[...]
    lane = lax.broadcasted_iota(jnp.int32, route.shape, 1)
    w0 = jnp.sum(jnp.where(lane == ROUTE_W, route, 0.0), axis=1, keepdims=True)
    w1 = jnp.sum(jnp.where(lane == ROUTE_W + 1, route, 0.0), axis=1, keepdims=True)
    o_ref[...] = x_ref[...] + w0 * y0_ref[...].astype(F32) + w1 * y1_ref[...].astype(F32)


def moe_combine(x2, y0, y1, route, tm=512):
    t, d = x2.shape
    row = lambda i: (i, 0)
    return pl.pallas_call(
        _moe_combine_body, grid=(t // tm,),
        in_specs=[pl.BlockSpec((tm, d), row), pl.BlockSpec((tm, d), row), pl.BlockSpec((tm, d), row),
                  pl.BlockSpec((tm, LANES), row)],
        out_specs=pl.BlockSpec((tm, d), row),
        out_shape=jax.ShapeDtypeStruct((t, d), F32),
        compiler_params=_cparams(("parallel",)), name="moe_combine",
    )(x2, y0, y1, route)


def moe_residual(x2, h, route, counts, w_gate, w_up, w_down):
    t, d = x2.shape
    n_e = w_gate.shape[0]
    rows = MOE_ROWS
    expert = route[:, ROUTE_IDX:ROUTE_IDX + 2].astype(jnp.int32).reshape(-1)
    rank = route[:, ROUTE_RANK:ROUTE_RANK + 2].astype(jnp.int32).reshape(-1)
    padded = (counts[0, :n_e].astype(jnp.int32) + rows - 1) // rows * rows
    ends = jnp.cumsum(padded)
    own = expert[:, None] == jnp.arange(n_e, dtype=jnp.int32)[None, :]
    slot = jnp.sum(jnp.where(own, (ends - padded)[None, :], 0), axis=1) + rank
    p_rows = 2 * t + n_e * rows
    token_of_slot = jnp.zeros((p_rows,), jnp.int32).at[slot].set(
        jnp.arange(2 * t, dtype=jnp.int32) // 2, unique_indices=True)
    block_start = jnp.arange(p_rows // rows, dtype=jnp.int32) * rows
    block_expert = jnp.minimum(jnp.searchsorted(ends, block_start, side="right"), n_e - 1).astype(jnp.int32)
    n_used = (ends[-1] // rows).astype(jnp.int32).reshape(1)

    take_rows = lambda a, idx: a.at[idx].get(mode="promise_in_bounds")
    xs = take_rows(h, token_of_slot)
    ys = moe_expert_ffn(xs, block_expert, n_used, w_gate, w_up, w_down)
    slot2 = slot.reshape(t, 2)
    return moe_combine(x2, take_rows(ys, slot2[:, 0]), take_rows(ys, slot2[:, 1]), route)


def _router_body(x_ref, g_ref, r_ref, h_ref, o_ref, count_ref, run_ref):
    @pl.when(pl.program_id(0) == 0)
    def _():
        run_ref[...] = jnp.zeros(run_ref.shape, F32)

    h = _rms(x_ref[...], g_ref[...])
    h_ref[...] = h.astype(h_ref.dtype)
    logits = jnp.dot(h, r_ref[...], precision=lax.Precision.HIGHEST, preferred_element_type=F32)
    lane = lax.broadcasted_iota(jnp.int32, logits.shape, 1).astype(F32)
    low = jnp.float32(-3.0e38)
    logits = jnp.where(lane < N_EXPERTS, logits, low)
    m1 = jnp.max(logits, axis=1, keepdims=True)
    i1 = jnp.min(jnp.where(logits == m1, lane, float(LANES)), axis=1, keepdims=True)
    rest = jnp.where(lane == i1, low, logits)
    m2 = jnp.max(rest, axis=1, keepdims=True)
    i2 = jnp.min(jnp.where(rest == m2, lane, float(LANES)), axis=1, keepdims=True)
    ex = jnp.exp(m2 - m1)
    w1 = 1.0 / (1.0 + ex)
    w2 = ex / (1.0 + ex)
    tm = logits.shape[0]
    routed = jnp.where((lane == i1) | (lane == i2), 1.0, 0.0)
    row = lax.broadcasted_iota(jnp.int32, (tm, tm), 0)
    col = lax.broadcasted_iota(jnp.int32, (tm, tm), 1)
    before = jnp.where(col < row, 1.0, 0.0).astype(BF16)
    rank = run_ref[0:1, :] + _dot(before, routed.astype(BF16))
    r1 = jnp.sum(jnp.where(lane == i1, rank, 0.0), axis=1, keepdims=True)
    r2 = jnp.sum(jnp.where(lane == i2, rank, 0.0), axis=1, keepdims=True)
    fields = ((ROUTE_IDX, i1), (ROUTE_IDX + 1, i2), (ROUTE_W, w1), (ROUTE_W + 1, w2),
              (ROUTE_RANK, r1), (ROUTE_RANK + 1, r2))
    out = jnp.zeros(logits.shape, F32)
    for pos, val in fields:
        out = jnp.where(lane == pos, val, out)
    o_ref[...] = out
    run_ref[...] = run_ref[...] + jnp.sum(routed, axis=0, keepdims=True)
    count_ref[...] = run_ref[...]


def router(x2, gain, router_w, tm=512):
    t, d = x2.shape
    r_pad = jnp.pad(router_w.astype(F32), ((0, 0), (0, LANES - router_w.shape[1])))
    return pl.pallas_call(
        _router_body, grid=(t // tm,),
        in_specs=[pl.BlockSpec((tm, d), lambda i: (i, 0)),
                  pl.BlockSpec((1, d), lambda i: (0, 0)),
                  pl.BlockSpec((d, LANES), lambda i: (0, 0))],
        out_specs=[pl.BlockSpec((tm, d), lambda i: (i, 0)), pl.BlockSpec((tm, LANES), lambda i: (i, 0)),
                   pl.BlockSpec((SUBLANES, LANES), lambda i: (0, 0))],
        out_shape=[jax.ShapeDtypeStruct((t, d), BF16), jax.ShapeDtypeStruct((t, LANES), F32),
                   jax.ShapeDtypeStruct((SUBLANES, LANES), F32)],
        scratch_shapes=[pltpu.VMEM((SUBLANES, LANES), F32)],
        compiler_params=_cparams(("arbitrary",)), name="router",
    )(x2, gain.reshape(1, d).astype(F32), r_pad)


def _compress_body(ch_ref, nx_ref, pos_ref, w1_ref, w2_ref, gain_ref, cos_ref, sin_ref, rot_ref, o_ref,
                   *, is_key):
    a = _dot((ch_ref[0] + pos_ref[0]).astype(BF16), w1_ref[0, 0])
    a = a + _dot((nx_ref[0] + pos_ref[1]).astype(BF16), w1_ref[0, 1])
    out = _dot(_silu(a).astype(BF16), w2_ref[...])
    if is_key:
        out = _rms(out, gain_ref[...])
        hi, lo = _split_bf16(out, 2)
        partner = _dot(hi, rot_ref[...]) + _dot(lo, rot_ref[...])
        out = out * cos_ref[...] + partner * sin_ref[...]
    o_ref[0, 0] = out


def nsa_compress(t, pos, w1, w2, gain, seq, is_key):
    b, s, _ = t.shape
    g, d = NSA_KV_GROUPS, NSA_HEAD_DIM
    n_ch = s // NSA_CMP_STRIDE
    half = NSA_CMP_STRIDE * g * d
    ch = t.reshape(b, n_ch, half)
    nxt = jnp.concatenate([ch[:, 1:], jnp.zeros((b, 1, half), F32)], axis=1)
    pos2 = jnp.broadcast_to(pos.astype(F32).reshape(2, NSA_CMP_STRIDE, 1, d),
                            (2, NSA_CMP_STRIDE, g, d)).reshape(2, 1, half)
    w1r = w1.astype(BF16).reshape(2, NSA_CMP_STRIDE, 1, d, d)
    own = (jnp.arange(g)[:, None] == jnp.arange(g)[None, :]).reshape(g, 1, 1, g, 1, 1)
    w1s = jnp.where(own, w1r[None], jnp.zeros((), BF16)).reshape(g, 2, half, d)
    cmp_end = jnp.arange(n_ch) * NSA_CMP_STRIDE + NSA_CMP_BLOCK - 1
    inv_freq = 1.0 / (ROPE_THETA ** (jnp.arange(0, d, 2, dtype=F32) / d))
    ang = cmp_end.astype(F32)[:, None] * inv_freq[None, :]
    cos = jnp.concatenate([jnp.cos(ang)] * 2, axis=1)
    sin = jnp.concatenate([jnp.sin(ang)] * 2, axis=1)
    rot = np.zeros((d, d), np.float32)
    rot[np.arange(d // 2) + d // 2, np.arange(d // 2)] = -1.0
    rot[np.arange(d // 2), np.arange(d // 2) + d // 2] = 1.0
    blk = lambda bb, gg: (bb, gg, 0, 0)
    seq = lambda bb, gg: (bb, 0, 0)
    c2 = lambda bb, gg: (0, 0)
    c3 = lambda bb, gg: (0, 0, 0)
    return pl.pallas_call(
        functools.partial(_compress_body, is_key=is_key), grid=(b, g),
        in_specs=[pl.BlockSpec((1, n_ch, half), seq), pl.BlockSpec((1, n_ch, half), seq),
                  pl.BlockSpec((2, 1, half), c3), pl.BlockSpec((1, 2, half, d), lambda bb, gg: (gg, 0, 0, 0)),
                  pl.BlockSpec((d, d), c2), pl.BlockSpec((1, d), c2),
                  pl.BlockSpec((n_ch, d), c2), pl.BlockSpec((n_ch, d), c2), pl.BlockSpec((d, d), c2)],
        out_specs=pl.BlockSpec((1, 1, n_ch, d), blk),
        out_shape=jax.ShapeDtypeStruct((b, g, n_ch, d), F32),
        compiler_params=_cparams(("parallel", "parallel")), name="nsa_compress",
    )(ch, nxt, pos2, w1s, w2.astype(BF16), gain.reshape(1, d).astype(F32), cos, sin,
      jnp.asarray(rot, dtype=BF16))


def _nsa_body(q_ref, ck_ref, cvt_ref, ksl_ref, vslt_ref, kwn_ref, vwnt_ref, ovt_ref, glt_ref, o_ref,
              sc_ref, phi_ref, plo_ref, imp_ref, bias_ref, ss_ref, ps_ref, ss1_ref, ps1_ref, sw_ref, pw_ref,
              ow_ref, oc_ref, *, tq):
    g = pl.program_id(1)
    i = pl.program_id(2)
    d = NSA_HEAD_DIM
    rep = NSA_HEADS // NSA_KV_GROUPS
    t0 = i * tq
    n_cmp = ck_ref.shape[1]
    n_sel = ovt_ref.shape[0]
    width = rep * tq
    sub = NSA_SEL_BLOCK
    dead = 0.5 * NEG_INF
    v_rows = pl.ds(pl.multiple_of(g * d, d), d)

    qb = q_ref[0]
    q4 = jnp.concatenate([qb[:, r * d:(r + 1) * d] for r in range(rep)], axis=0)
    q4 = jnp.concatenate([q4, q4], axis=1)
    lane = lax.broadcasted_iota(jnp.int32, q4.shape, 1)
    q4 = jnp.where(jnp.right_shift(lane, d.bit_length() - 1) == g, q4, jnp.zeros_like(q4))

    def qpos_of(shape):
        return t0 + (lax.broadcasted_iota(jnp.int32, shape, 1) & (tq - 1))

    def compressed(rows):
        s = _dot_nt(ck_ref[0, 0:rows, :], q4)
        cmp_end = lax.broadcasted_iota(jnp.int32, s.shape, 0) * NSA_CMP_STRIDE + (NSA_CMP_BLOCK - 1)
        s = jnp.where(cmp_end <= qpos_of(s.shape), s, NEG_INF)
        sc_ref[0:rows, :] = s
        m_c = jnp.max(s, axis=0, keepdims=True)
        lpart = jnp.zeros((SUBLANES, width), F32)
        for r in range(rows // sub):
            e = jnp.exp2(sc_ref[r * sub:(r + 1) * sub, :] - m_c)
            lpart = lpart + _fold_rows(e)
            hi = e.astype(BF16)
            phi_ref[r * sub:(r + 1) * sub, :] = hi
            plo_ref[r * sub:(r + 1) * sub, :] = (e - hi.astype(F32)).astype(BF16)
        inv_c = jnp.where(m_c > dead, 1.0 / jnp.sum(lpart, axis=0, keepdims=True), 0.0)
        oc_ref[...] = _dot(cvt_ref[0, v_rows, 0:rows], phi_ref[0:rows, :]) * inv_c
        ovt = ovt_ref[:, 0:rows]
        imp4 = (_dot(ovt, phi_ref[0:rows, :]) + _dot(ovt, plo_ref[0:rows, :])) * inv_c
        imp = imp4[:, 0:tq]
        for r in range(1, rep):
            imp = imp + imp4[:, r * tq:(r + 1) * tq]
        imp_ref[...] = imp

    cmp_live = jnp.clip((t0 + tq - NSA_CMP_BLOCK) // NSA_CMP_STRIDE + 1, 1, n_cmp)
    cmp_step = min(2 * LANES, n_cmp)
    for v in range(n_cmp // cmp_step):
        @pl.when((cmp_live > cmp_step * v) & (cmp_live <= cmp_step * (v + 1)))
        def _():
            compressed(cmp_step * (v + 1))

    o_c = oc_ref[...]
    imp = imp_ref[...]

    blk = lax.broadcasted_iota(jnp.int32, imp.shape, 0)
    qp = t0 + lax.broadcasted_iota(jnp.int32, imp.shape, 1)
    cur = jnp.right_shift(qp, NSA_SEL_BLOCK.bit_length() - 1)
    forced = (blk == 0) | (blk == cur) | (blk == cur - 1)
    future = blk * NSA_SEL_BLOCK > qp
    imp_ref[...] = jnp.where(future, -FORCE_SCORE, jnp.where(forced, FORCE_SCORE, imp))
    bias_ref[...] = jnp.full(bias_ref.shape, NEG_INF, F32)

    n_live = jnp.minimum((t0 + tq - 1) // NSA_SEL_BLOCK + 1, n_sel)
    n_var = max(n_sel // 32, 1)
    rows_per = n_sel // n_var
    top_n = float(min(NSA_TOP_N, n_sel))
    for v in range(n_var):
        rows = rows_per * (v + 1)

        @pl.when((n_live > rows_per * v) & (n_live <= rows))
        def _():
            groups = rows // SUBLANES
            mine = [imp_ref[gi * SUBLANES:(gi + 1) * SUBLANES, :] for gi in range(groups)]
            rank = [jnp.zeros((SUBLANES, tq), F32) for _ in range(groups)]
            in_group = lax.broadcasted_iota(jnp.int32, (SUBLANES, tq), 0)
            for i2 in range(rows):
                other = imp_ref[i2:i2 + 1, :]
                for gi in range(groups):
                    if gi > i2 // SUBLANES:
                        beats = other >= mine[gi]
                    elif gi < i2 // SUBLANES:
                        beats = other > mine[gi]
                    else:
                        beats = (other > mine[gi]) | ((other == mine[gi]) & (in_group > i2 % SUBLANES))
                    rank[gi] = rank[gi] + jnp.where(beats, 1.0, 0.0)
            for gi in range(groups):
                bias = jnp.where(rank[gi] < top_n, 0.0, NEG_INF)
                bias_ref[gi * SUBLANES:(gi + 1) * SUBLANES, :] = jnp.concatenate([bias] * rep, axis=1)

    init = (jnp.full((1, width), NEG_INF, F32), jnp.zeros((d + ONES_ROWS, width), F32))

    def normalised(acc):
        return acc[:d, :] / acc[d:d + 1, :]

    chunk = 8 * sub
    n_sub = chunk // sub

    s_slots = (ss_ref, ss1_ref)
    p_slots = (ps_ref, ps1_ref)

    def sel_scores(c, slot, diagonal=False):
        start = pl.multiple_of(c * chunk, chunk)
        s = _dot_nt(ksl_ref[0, pl.ds(start, chunk), :], q4)
        if diagonal:
            kpos = start + lax.broadcasted_iota(jnp.int32, s.shape, 0)
            s = jnp.where(kpos <= qpos_of(s.shape), s, NEG_INF)
        s_slots[slot][...] = s

    def sel_update(c, slot, carry):
        m_prev, acc = carry
        s_ref, p_ref = s_slots[slot], p_slots[slot]
        biases = [bias_ref[pl.ds(c * n_sub + r, 1), :] for r in range(n_sub)]
        m8 = jnp.full((SUBLANES, width), NEG_INF, F32)
        for r in range(n_sub):
            block = s_ref[r * sub:(r + 1) * sub, :]
            m8 = jnp.maximum(m8, jnp.max(block.reshape(sub // SUBLANES, SUBLANES, width), axis=0) + biases[r])
        m_new = jnp.maximum(m_prev, jnp.max(m8, axis=0, keepdims=True))
        live = m_new > dead
        for r in range(n_sub):
            shift = jnp.where(live, biases[r] - m_new, NEG_INF)
            p_ref[r * sub:(r + 1) * sub, :] = jnp.exp2(s_ref[r * sub:(r + 1) * sub, :] + shift).astype(BF16)
        vt = _with_ones_rows(vslt_ref[v_rows, pl.ds(pl.multiple_of(c * chunk, chunk), chunk)])
        return m_new, jnp.exp2(m_prev - m_new) * acc + _dot(vt, p_ref[...])

    c_diag = (t0 + tq - 1) // chunk
    n_chunks = c_diag + 1
    last_past = jnp.maximum(c_diag - 1, 0)

    def chunk_at(j):
        return jnp.where(j == 0, c_diag, j - 1)

    sel_scores(c_diag, 0, diagonal=True)

    def pair(k, carry):
        sel_scores(jnp.minimum(2 * k, last_past), 1)
        carry = sel_update(chunk_at(2 * k), 0, carry)
        sel_scores(jnp.minimum(2 * k + 1, last_past), 0)
        return sel_update(2 * k, 1, carry)

    carry = lax.fori_loop(0, n_chunks // 2, pair, init)
    _, acc_s = lax.cond(n_chunks % 2 == 1, lambda cr: sel_update(chunk_at(n_chunks - 1), 0, cr),
                        lambda cr: cr, carry)
    o_s = normalised(acc_s)

    def win_chunk(c, carry):
        m_prev, acc = carry
        start = pl.multiple_of(c * tq, tq)
        s = _dot_nt(kwn_ref[0, pl.ds(start, tq), :], q4)
        kpos = start + lax.broadcasted_iota(jnp.int32, s.shape, 0)
        qpos = qpos_of(s.shape)
        s = jnp.where((kpos <= qpos) & (kpos > qpos - NSA_WINDOW), s, NEG_INF)
        m_new = jnp.maximum(m_prev, jnp.max(s, axis=0, keepdims=True))
        p = jnp.exp2(s + jnp.where(m_new > dead, -m_new, NEG_INF))
        vt = _with_ones_rows(vwnt_ref[v_rows, pl.ds(start, tq)])
        return m_new, jnp.exp2(m_prev - m_new) * acc + _dot(vt, p.astype(BF16))

    n_back = NSA_WINDOW // tq

    @pl.when(i < n_back)
    def _():
        _, acc_w = lax.fori_loop(0, i + 1, win_chunk, init)
        ow_ref[...] = normalised(acc_w)

    @pl.when(i >= n_back)
    def _():
        start = pl.multiple_of(t0 - NSA_WINDOW, tq)
        s = _dot_nt(kwn_ref[0, pl.ds(start, NSA_WINDOW + tq), :], q4)
        kpos = start + lax.broadcasted_iota(jnp.int32, (tq, width), 0)
        qpos = qpos_of((tq, width))
        sw_ref[0:tq, :] = jnp.where(kpos > qpos - NSA_WINDOW, s[0:tq, :], NEG_INF)
        sw_ref[tq:NSA_WINDOW, :] = s[tq:NSA_WINDOW, :]
        sw_ref[NSA_WINDOW:, :] = jnp.where(kpos + NSA_WINDOW <= qpos, s[NSA_WINDOW:, :], NEG_INF)
        m_w = jnp.max(sw_ref[...], axis=0, keepdims=True)
        for r in range((NSA_WINDOW + tq) // sub):
            pw_ref[r * sub:(r + 1) * sub, :] = jnp.exp2(sw_ref[r * sub:(r + 1) * sub, :] - m_w).astype(BF16)
        vt = _with_ones_rows(vwnt_ref[v_rows, pl.ds(start, NSA_WINDOW + tq)])
        ow_ref[...] = normalised(_dot(vt, pw_ref[...]))

    o_w = ow_ref[...]

    def gate(branch):
        rows = [glt_ref[pl.ds((g * rep + r) * 3 + branch, 1), :] for r in range(rep)]
        return _sigmoid(jnp.concatenate(rows, axis=1))

    out = gate(0) * o_c + gate(1) * o_s + gate(2) * o_w
    out_t = jnp.concatenate([out, jnp.zeros_like(out)], axis=0).T
    o_ref[0] = jnp.concatenate([out_t[r * tq:(r + 1) * tq, :d] for r in range(rep)],
                               axis=1).astype(o_ref.dtype)


def nsa_overlap_t(n_cmp, n_sel):
    c_start = np.arange(n_cmp)[None, :] * NSA_CMP_STRIDE
    s_start = np.arange(n_sel)[:, None] * NSA_SEL_BLOCK
    hit = (c_start < s_start + NSA_SEL_BLOCK) & (c_start + NSA_CMP_BLOCK > s_start)
    hit = hit & (np.arange(n_cmp)[None, :] < n_cmp - NSA_CMP_BLOCK // NSA_CMP_STRIDE + 1)
    return jnp.asarray(hit.astype(np.float32), dtype=BF16)


def nsa_attention(qn, ck, cvt, ksl, vslt, kwn, vwnt, glt, tq=128):
    b, s, _ = qn.shape
    g, d = NSA_KV_GROUPS, NSA_HEAD_DIM
    rep = NSA_HEADS // g
    n_cmp = ck.shape[1]
    n_sel = s // NSA_SEL_BLOCK
    nq = s // tq
    ovt = nsa_overlap_t(n_cmp, n_sel)
    width = rep * tq
    chunk = 8 * NSA_SEL_BLOCK
    full3 = lambda bb, gg, i: (bb, 0, 0)
    seq_t = lambda bb, gg, i: (0, bb)
    return pl.pallas_call(
        functools.partial(_nsa_body, tq=tq), grid=(b, g, nq),
        in_specs=[pl.BlockSpec((1, tq, rep * d), lambda bb, gg, i: (bb, i, gg)),
                  pl.BlockSpec((1, n_cmp, g * d), full3), pl.BlockSpec((1, g * d, n_cmp), full3),
                  pl.BlockSpec((1, s, g * d), full3), pl.BlockSpec((g * d, s), seq_t),
                  pl.BlockSpec((1, s, g * d), full3), pl.BlockSpec((g * d, s), seq_t),
                  pl.BlockSpec((n_sel, n_cmp), lambda bb, gg, i: (0, 0)),
                  pl.BlockSpec((glt.shape[0], tq), lambda bb, gg, i: (0, bb * nq + i))],
        out_specs=pl.BlockSpec((1, tq, rep * d), lambda bb, gg, i: (bb, i, gg)),
        out_shape=jax.ShapeDtypeStruct((b, s, g * rep * d), BF16),
        scratch_shapes=[pltpu.VMEM((n_cmp, width), F32), pltpu.VMEM((n_cmp, width), BF16),
                        pltpu.VMEM((n_cmp, width), BF16), pltpu.VMEM((n_sel, tq), F32),
                        pltpu.VMEM((n_sel, width), F32), pltpu.VMEM((chunk, width), F32),
                        pltpu.VMEM((chunk, width), BF16), pltpu.VMEM((chunk, width), F32),
                        pltpu.VMEM((chunk, width), BF16), pltpu.VMEM((NSA_WINDOW + tq, width), F32),
                        pltpu.VMEM((NSA_WINDOW + tq, width), BF16), pltpu.VMEM((d, width), F32),
                        pltpu.VMEM((d, width), F32)],
        compiler_params=_cparams(("parallel", "parallel", "arbitrary")), name="nsa_attention",
    )(qn, ck, cvt, ksl, vslt, kwn, vwnt, ovt, glt)


def _pad_cols(w, n):
    return jnp.pad(w, ((0, 0), (0, n - w.shape[1])))


def _even_layer(x2, b, s, layer_idx, norm_mix, w_in, q_gain, k_gain, lam, subln_gain, conv_w, conv_b,
                dt_bias, a_log, d_skip, ssm_norm_gain, w_out, norm_ffn, w_gate, w_up, w_down):
    nq = DA_HEADS * 2 * DA_HEAD_DIM
    nv = DA_HEADS * DA_V_DIM
    cch = SSM_D_INNER + 2 * SSM_GROUPS * SSM_STATE
    offs = np.cumsum([0, nq, nq, nv, SSM_D_INNER, cch, SSM_HEADS])
    wb = w_in.astype(BF16)
    pieces = [wb[:, offs[k]:offs[k + 1]] for k in range(6)]
    pieces[2] = pieces[2].T
    pieces[5] = _pad_cols(pieces[5], LANES)
    posts = [HeadNorm(q_gain, DA_HEAD_DIM, rope=True, mul=DA_HEAD_DIM ** -0.5 * LOG2E),
             HeadNorm(k_gain, DA_HEAD_DIM, rope=True), None, None, None, None]
    q, k, vt, z, xbc, dt = norm_proj(x2, norm_mix, pieces, [BF16, BF16, BF16, F32, F32, F32], posts, s,
                                     _rope_tables(s, DA_HEAD_DIM), transposed=(2,))
    qn = q.reshape(b, s, nq)
    kn = k.reshape(b, s, nq)
    lam_init = 0.8 - 0.6 * math.exp(-0.3 * layer_idx)
    lf = lam.astype(F32)
    lam_full = jnp.exp(jnp.sum(lf[0] * lf[1])) - jnp.exp(jnp.sum(lf[2] * lf[3])) + lam_init
    a_out = flash_attention(lam_full.reshape(1), [qn], [kn], vt, subln_gain, DA_HEADS, DA_V_DIM,
                            diff=True, out_scale=1.0 - lam_init)
    b_out = ssd_mixer(xbc.reshape(b, s, cch), z.reshape(b, s, SSM_D_INNER), dt.reshape(b, s, LANES),
                      conv_w, conv_b, dt_bias, a_log, d_skip, ssm_norm_gain)
    wo = w_out.astype(BF16)
    x2 = out_proj_residual(x2, a_out.reshape(-1, nv), b_out.reshape(-1, SSM_D_INNER), wo[:nv], wo[nv:])
    return ffn_residual(x2, norm_ffn, w_gate.astype(BF16), w_up.astype(BF16), w_down.astype(BF16))


def _odd_layer(x2, b, s, norm_mix, w_in, q_gain, k_gain, cmp_pos, cmp_w1, cmp_w2, cq_gain, ckv_gain,
               w_uq, w_ukv, qn_gain, qr_gain, kn_gain, kr_gain, w_out, norm_ffn, router_w, w_gate, w_up,
               w_down):
    g, d = NSA_KV_GROUPS, NSA_HEAD_DIM
    nq = NSA_HEADS * d
    nkv = g * d
    sizes = [nq] + [nkv] * 6 + [NSA_HEADS * 3, w_uq.shape[0], w_ukv.shape[0], MLA_ROPE_DIM]
    offs = np.cumsum([0] + sizes)
    wb = w_in.astype(BF16)
    pieces = [wb[:, offs[k]:offs[k + 1]] for k in range(len(sizes))]
    for k in (4, 6):
        pieces[k] = pieces[k].T
    pieces[7] = jnp.pad(pieces[7].T, ((0, 32 - NSA_HEADS * 3), (0, 0)))
    pieces[10] = _pad_cols(pieces[10], LANES)
    tables = _rope_tables(s, d)
    posts = [None] * len(sizes)
    posts[0] = HeadNorm(q_gain, d, rope=True, mul=d ** -0.5 * LOG2E)
    posts[3] = HeadNorm(k_gain[1], d, rope=True)
    posts[5] = HeadNorm(k_gain[2], d, rope=True)
    posts[10] = HeadNorm(kr_gain, MLA_ROPE_DIM, rope=True)
    (q, kc, vc, ksl, vslt, kwn, vwnt, glt, cq, ckv, k_rope) = norm_proj(
        x2, norm_mix, pieces, [BF16, F32, F32, BF16, BF16, BF16, BF16, F32, F32, F32, BF16], posts, s, tables,
        transposed=(4, 6, 7))

    qn = q.reshape(b, s, nq)
    ksl_n = ksl.reshape(b, s, nkv)
    kwn_n = kwn.reshape(b, s, nkv)
    ck = nsa_compress(kc.reshape(b, s, nkv), cmp_pos[0], cmp_w1[0], cmp_w2[0], k_gain[0], s, True)
    cv = nsa_compress(vc.reshape(b, s, nkv), cmp_pos[1], cmp_w1[1], cmp_w2[1], k_gain[0], s, False)
    n_cmp = ck.shape[2]
    ck = ck.transpose(0, 2, 1, 3).reshape(b, n_cmp, nkv).astype(BF16)
    cvt = cv.transpose(0, 1, 3, 2).reshape(b, nkv, n_cmp).astype(BF16)
    c_out = nsa_attention(qn, ck, cvt, ksl_n, vslt, kwn_n, vwnt, glt).reshape(b * s, nq)

    h = MLA_HEADS
    dqk = MLA_NOPE_DIM + MLA_ROPE_DIM
    wq = w_uq.astype(BF16).reshape(-1, h, dqk)
    wq_nope = wq[:, :, :MLA_NOPE_DIM].reshape(-1, h * MLA_NOPE_DIM)
    wq_rope = jnp.pad(wq[:, :, MLA_NOPE_DIM:], ((0, 0), (0, 0), (0, LANES - MLA_ROPE_DIM)))
    wq_rope = wq_rope.reshape(-1, h * LANES)
    wkv = w_ukv.astype(BF16).reshape(-1, h, MLA_NOPE_DIM + MLA_V_DIM)
    wk_nope = wkv[:, :, :MLA_NOPE_DIM].reshape(-1, h * MLA_NOPE_DIM)
    wv = wkv[:, :, MLA_NOPE_DIM:].reshape(-1, h * MLA_V_DIM)
    q_mul = dqk ** -0.5 * LOG2E
    q_nope, q_rope = norm_proj(
        cq, cq_gain, [wq_nope, wq_rope], [BF16, BF16],
        [HeadNorm(qn_gain, MLA_NOPE_DIM, mul=q_mul), HeadNorm(qr_gain, MLA_ROPE_DIM, rope=True, mul=q_mul)],
        s, tables)
    k_nope, vt = norm_proj(ckv, ckv_gain, [wk_nope, wv.T], [BF16, BF16],
                           [HeadNorm(kn_gain, MLA_NOPE_DIM), None], transposed=(1,))
    shp = lambda t: t.reshape(b, s, t.shape[-1])
    d_out = flash_attention(jnp.zeros((1,), F32), [shp(q_nope), shp(q_rope)], [shp(k_nope), shp(k_rope)],
                            vt, jnp.ones((MLA_V_DIM,), F32), h, MLA_V_DIM, diff=False)

    wo = w_out.astype(BF16)
    x2 = out_proj_residual(x2, c_out, d_out.reshape(b * s, h * MLA_V_DIM), wo[:nq], wo[nq:])
    h, route, counts = router(x2, norm_ffn, router_w)
    return moe_residual(x2, h, route, counts, w_gate.astype(BF16), w_up.astype(BF16), w_down.astype(BF16))


def kernel(x, ev_norm_mix, ev_w_in, da_q_gain, da_k_gain, da_lambda, da_subln_gain, ssm_conv_w, ssm_conv_b, ssm_dt_bias, ssm_a_log, ssm_d, ssm_norm_gain, ev_w_out, ev_norm_ffn, ffn_w_gate, ffn_w_up, ffn_w_down, od_norm_mix, od_w_in, nsa_q_gain, nsa_k_gain, nsa_cmp_pos, nsa_cmp_w1, nsa_cmp_w2, mla_cq_gain, mla_ckv_gain, mla_w_uq, mla_w_ukv, mla_qn_gain, mla_qr_gain, mla_kn_gain, mla_kr_gain, od_w_out, od_norm_ffn, moe_router, moe_w_gate, moe_w_up, moe_w_down):
    b, s, d = x.shape
    x2 = x.reshape(b * s, d)
    depth = ev_norm_mix.shape[0] + od_norm_mix.shape[0]
    for layer in range(depth):
        i = layer // 2
        if layer % 2 == 0:
            x2 = _even_layer(x2, b, s, layer, ev_norm_mix[i], ev_w_in[i], da_q_gain[i], da_k_gain[i],
                             da_lambda[i], da_subln_gain[i], ssm_conv_w[i], ssm_conv_b[i],
                             ssm_dt_bias[i], ssm_a_log[i], ssm_d[i], ssm_norm_gain[i], ev_w_out[i],
                             ev_norm_ffn[i], ffn_w_gate[i], ffn_w_up[i], ffn_w_down[i])
        else:
            x2 = _odd_layer(x2, b, s, od_norm_mix[i], od_w_in[i], nsa_q_gain[i], nsa_k_gain[i],
                            nsa_cmp_pos[i], nsa_cmp_w1[i], nsa_cmp_w2[i], mla_cq_gain[i],
                            mla_ckv_gain[i], mla_w_uq[i], mla_w_ukv[i], mla_qn_gain[i], mla_qr_gain[i],
                            mla_kn_gain[i], mla_kr_gain[i], od_w_out[i], od_norm_ffn[i], moe_router[i],
                            moe_w_gate[i], moe_w_up[i], moe_w_down[i])
    return x2.reshape(b, s, d)
```

```python
import functools
import math

import numpy as np
import jax
import jax.numpy as jnp
from jax import lax
from jax.experimental import pallas as pl
from jax.experimental.pallas import tpu as pltpu

F32 = jnp.float32
BF16 = jnp.bfloat16

ROPE_THETA = 10000.0
NORM_EPS = 1e-6
NEG_INF = -1e30
FORCE_SCORE = 1e6
LOG2E = 1.4426950408889634

DA_HEADS = 4
DA_HEAD_DIM = 64
DA_V_DIM = 2 * DA_HEAD_DIM
SSM_HEADS = 8
SSM_HEAD_DIM = 64
SSM_D_INNER = SSM_HEADS * SSM_HEAD_DIM
SSM_GROUPS = 2
SSM_STATE = 128
SSM_CONV = 4
SSM_CHUNK = 256
NSA_HEADS = 8
NSA_KV_GROUPS = 2
NSA_HEAD_DIM = 64
NSA_CMP_BLOCK = 32
NSA_CMP_STRIDE = 16
NSA_SEL_BLOCK = 64
NSA_TOP_N = 16
NSA_WINDOW = 512
MLA_HEADS = 4
MLA_NOPE_DIM = 128
MLA_ROPE_DIM = 64
MLA_V_DIM = 128
N_EXPERTS = 8

LANES = 128
SUBLANES = 8
VMEM_LIMIT = 48 * 1024 * 1024
MOE_VMEM_LIMIT = 58 * 1024 * 1024

NT_DIMS = (((1,), (1,)), ((), ()))


def _cparams(semantics):
    return pltpu.CompilerParams(dimension_semantics=semantics, vmem_limit_bytes=VMEM_LIMIT)


def _dot(a, b):
    return jnp.dot(a, b, preferred_element_type=F32)


def _dot_nt(a, b):
    return lax.dot_general(a, b, NT_DIMS, preferred_element_type=F32)


def _split_bf16(x, parts):
    out = []
    for _ in range(parts):
        hi = x.astype(BF16)
        out.append(hi)
        x = x - hi.astype(F32)
    return out


def _fold_rows(x):
    return jnp.sum(x.reshape(x.shape[0] // SUBLANES, SUBLANES, x.shape[1]), axis=0)


ONES_ROWS = 16


def _with_ones_rows(vt):
    return jnp.concatenate([vt, jnp.ones((ONES_ROWS, vt.shape[1]), vt.dtype)], axis=0)


def _sigmoid(x):
    return 1.0 / (1.0 + jnp.exp(-x))


def _silu(x):
    return x * _sigmoid(x)


def _softplus(x):
    return jnp.maximum(x, 0.0) + jnp.log(1.0 + jnp.exp(-jnp.abs(x)))


def _rms(x, gain):
    ms = jnp.mean(x * x, axis=-1, keepdims=True)
    return x * lax.rsqrt(ms + NORM_EPS) * gain


class HeadNorm:
    def __init__(self, gain, hd, rope=False, mul=1.0):
        self.gain, self.hd, self.rope, self.mul = gain, hd, rope, mul


def _head_norm(y, gain, bd, cos_ref, sin_ref, post):
    n = y.shape[1]
    hd = post.hd
    hi, lo = _split_bf16(y * y, 2)
    ss = _dot(hi, bd) + _dot(lo, bd)
    yn = y * lax.rsqrt(ss * (1.0 / hd) + NORM_EPS) * gain
    if post.rope:
        reps = n // LANES
        cos = jnp.concatenate([cos_ref[...]] * reps, axis=1) if reps > 1 else cos_ref[...]
        sin = jnp.concatenate([sin_ref[...]] * reps, axis=1) if reps > 1 else sin_ref[...]
        lane = lax.broadcasted_iota(jnp.int32, yn.shape, 1)
        first_half = (lane & (hd - 1)) < (hd // 2)
        partner = jnp.where(first_half, pltpu.roll(yn, n - hd // 2, 1), pltpu.roll(yn, hd // 2, 1))
        yn = yn * cos + partner * sin
    if post.mul != 1.0:
        yn = yn * post.mul
    return yn


def _norm_proj_body(x_ref, g_ref, *refs, posts, use_rope, transposed):
    if use_rope:
        cos_ref, sin_ref = refs[0], refs[1]
        refs = refs[2:]
    else:
        cos_ref = sin_ref = None
    n_out = len(posts)
    n_aux = 2 * sum(p is not None for p in posts)
    w_refs, aux, o_refs = refs[:n_out], refs[n_out:n_out + n_aux], refs[n_out + n_aux:]
    h = _rms(x_ref[...], g_ref[...]).astype(BF16)
    a = 0
    for k, (w_ref, o_ref, post) in enumerate(zip(w_refs, o_refs, posts)):
        if k in transposed:
            o_ref[...] = _dot_nt(w_ref[...], h).astype(o_ref.dtype)
            continue
        y = _dot(h, w_ref[...])
        if post is not None:
            y = _head_norm(y, aux[a][...], aux[a + 1][...], cos_ref, sin_ref, post)
            a += 2
        o_ref[...] = y.astype(o_ref.dtype)


def norm_proj(x2, gain, weights, out_dtypes, posts=None, seq=None, rope_tables=None, transposed=(), tm=512):
    t, d = x2.shape
    posts = posts or [None] * len(weights)
    transposed = frozenset(transposed)
    use_rope = any(p is not None and p.rope for p in posts)
    const = lambda i: (0, 0)
    args = [x2, gain.reshape(1, d).astype(F32)]
    in_specs = [pl.BlockSpec((tm, d), lambda i: (i, 0)), pl.BlockSpec((1, d), const)]
    if use_rope:
        per_seq = seq // tm
        args += list(rope_tables)
        in_specs += [pl.BlockSpec((tm, LANES), lambda i: (i % per_seq, 0))] * 2
    args += list(weights)
    in_specs += [pl.BlockSpec(w.shape, const) for w in weights]
    for w, p in zip(weights, posts):
        if p is not None:
            n = w.shape[1]
            args += [jnp.tile(p.gain.astype(F32), n // p.hd).reshape(1, n), _block_diag_ones(n, p.hd)]
            in_specs += [pl.BlockSpec((1, n), const), pl.BlockSpec((n, n), const)]
    out_specs, out_shape = [], []
    for k, (w, dt) in enumerate(zip(weights, out_dtypes)):
        if k in transposed:
            out_specs.append(pl.BlockSpec((w.shape[0], tm), lambda i: (0, i)))
            out_shape.append(jax.ShapeDtypeStruct((w.shape[0], t), dt))
        else:
            out_specs.append(pl.BlockSpec((tm, w.shape[1]), lambda i: (i, 0)))
            out_shape.append(jax.ShapeDtypeStruct((t, w.shape[1]), dt))
    return pl.pallas_call(
        functools.partial(_norm_proj_body, posts=tuple(posts), use_rope=use_rope, transposed=transposed),
        grid=(t // tm,), in_specs=in_specs, out_specs=out_specs, out_shape=out_shape,
        compiler_params=_cparams(("parallel",)), name="norm_proj",
    )(*args)


def _block_diag_ones(n, hd):
    idx = np.arange(n) // hd
    return jnp.asarray((idx[:, None] == idx[None, :]).astype(np.float32), dtype=BF16)


def _rope_tables(seq, hd):
    inv_freq = 1.0 / (ROPE_THETA ** (jnp.arange(0, hd, 2, dtype=F32) / hd))
    ang = jnp.arange(seq, dtype=F32)[:, None] * inv_freq[None, :]
    cos, sin = jnp.cos(ang), jnp.sin(ang)
    reps = LANES // hd
    cos_t = jnp.tile(jnp.concatenate([cos, cos], axis=1), (1, reps))
    sin_t = jnp.tile(jnp.concatenate([-sin, sin], axis=1), (1, reps))
    return cos_t, sin_t


def _flash_body(lam_ref, *refs, n_qk, diff, out_scale, sub):
    q_refs = refs[:n_qk]
    k_refs = refs[n_qk:2 * n_qk]
    vt_ref, gain_ref, o_ref, m_ref, l_ref, acc_ref, s0_ref, s1_ref, p0_ref, p1_ref = refs[2 * n_qk:]
    i = pl.program_id(2)
    n_sm = 2 if diff else 1
    _, tk, tq = s0_ref.shape
    s_slots = (s0_ref, s1_ref)
    p_slots = (p0_ref, p1_ref)

    m_ref[...] = jnp.full(m_ref.shape, NEG_INF, F32)
    l_ref[...] = jnp.zeros(l_ref.shape, F32)
    acc_ref[...] = jnp.zeros(acc_ref.shape, F32)

    qs = [r[0] for r in q_refs]
    q = qs[0] if n_qk == 1 else jnp.concatenate(qs, axis=1)
    if diff:
        lane = lax.broadcasted_iota(jnp.int32, q.shape, 1)
        half = q.shape[1] // 2
        zero = jnp.zeros_like(q)
        q_parts = [jnp.where(lane < half, q, zero), jnp.where(lane >= half, q, zero)]
    else:
        q_parts = [q]

    def scores(c, slot, diagonal=False):
        rows = pl.ds(pl.multiple_of(c * tk, tk), tk)
        ks = [r[0, rows, :] for r in k_refs]
        k = ks[0] if n_qk == 1 else jnp.concatenate(ks, axis=1)
        for sm in range(n_sm):
            s = _dot_nt(k, q_parts[sm])
            if diagonal:
                row = lax.broadcasted_iota(jnp.int32, s.shape, 0)
                col = lax.broadcasted_iota(jnp.int32, s.shape, 1)
                s = jnp.where(row <= col, s, NEG_INF)
            s_slots[slot][sm] = s

    def update(c, slot):
        vt = vt_ref[:, pl.ds(pl.multiple_of(c * tk, tk), tk)]
        for sm in range(n_sm):
            s_ref, p_ref = s_slots[slot], p_slots[slot]
            m_prev = m_ref[sm]
            m_new = jnp.maximum(m_prev, jnp.max(s_ref[sm], axis=0, keepdims=True))
            m_ref[sm] = m_new
            alpha = jnp.exp2(m_prev - m_new)
            lpart = jnp.zeros((SUBLANES, tq), F32)
            for r in range(tk // sub):
                p = jnp.exp2(s_ref[sm, r * sub:(r + 1) * sub, :] - m_new)
                lpart = lpart + _fold_rows(p)
                p_ref[sm, r * sub:(r + 1) * sub, :] = p.astype(BF16)
            l_ref[sm] = alpha * l_ref[sm] + jnp.sum(lpart, axis=0, keepdims=True)
            acc_ref[sm] = alpha * acc_ref[sm] + _dot(vt, p_ref[sm])

    n_chunks = i + 1
    last_past = jnp.maximum(i - 1, 0)

    def chunk_at(j):
        return jnp.where(j == 0, i, j - 1)

    scores(i, 0, diagonal=True)

    def pair(k2, carry):
        scores(jnp.minimum(2 * k2, last_past), 1)
        update(chunk_at(2 * k2), 0)
        scores(jnp.minimum(2 * k2 + 1, last_past), 0)
        update(2 * k2, 1)
        return carry

    lax.fori_loop(0, n_chunks // 2, pair, 0)

    @pl.when(n_chunks % 2 == 1)
    def _():
        update(chunk_at(n_chunks - 1), 0)

    o = acc_ref[0] / l_ref[0]
    if diff:
        o = o - lam_ref[0] * (acc_ref[1] / l_ref[1])
        ms = jnp.mean(o * o, axis=0, keepdims=True)
        o = o * lax.rsqrt(ms + NORM_EPS) * gain_ref[...] * out_scale
    o_ref[0] = o.T.astype(o_ref.dtype)


def flash_attention(lam, qs, ks, vt, gain, n_heads, dv, *, diff, out_scale=1.0, tile=512, sub=64):
    b, s, _ = qs[0].shape
    nt = s // tile
    n_qk = len(qs)
    in_specs = [pl.BlockSpec(memory_space=pltpu.SMEM)]
    for q in qs:
        w = q.shape[2] // n_heads
        in_specs.append(pl.BlockSpec((1, tile, w), lambda bb, h, i: (bb, i, h)))
    for q, k in zip(qs, ks):
        w = q.shape[2] // n_heads
        if k.shape[2] == w:
            in_specs.append(pl.BlockSpec((1, s, w), lambda bb, h, i: (bb, 0, 0)))
        else:
            in_specs.append(pl.BlockSpec((1, s, w), lambda bb, h, i: (bb, 0, h)))
    in_specs.append(pl.BlockSpec((dv, s), lambda bb, h, i: (h, bb)))
    in_specs.append(pl.BlockSpec((dv, 1), lambda bb, h, i: (0, 0)))
    n_sm = 2 if diff else 1
    return pl.pallas_call(
        functools.partial(_flash_body, n_qk=n_qk, diff=diff, out_scale=out_scale, sub=sub),
        grid=(b, n_heads, nt), in_specs=in_specs,
        out_specs=pl.BlockSpec((1, tile, dv), lambda bb, h, i: (bb, i, h)),
        out_shape=jax.ShapeDtypeStruct((b, s, n_heads * dv), BF16),
        scratch_shapes=[pltpu.VMEM((n_sm, 1, tile), F32), pltpu.VMEM((n_sm, 1, tile), F32),
                        pltpu.VMEM((n_sm, dv, tile), F32),
                        pltpu.VMEM((n_sm, tile, tile), F32), pltpu.VMEM((n_sm, tile, tile), F32),
                        pltpu.VMEM((n_sm, tile, tile), BF16), pltpu.VMEM((n_sm, tile, tile), BF16)],
        compiler_params=_cparams(("parallel", "parallel", "arbitrary")),
        name="flash_diff" if diff else "flash_plain",
    )(lam, *qs, *ks, vt, gain.reshape(dv, 1).astype(F32))


def _ssd_body(xbc_ref, z_ref, dt_ref, dtt_ref, cw_ref, cb_ref, dtb_ref, dtbt_ref, al_ref, alt_ref,
              dsk_ref, ng_ref, o_ref, xpad_ref, state_ref):
    chunk = xbc_ref.shape[1]
    d_in = z_ref.shape[2]
    gn = SSM_GROUPS * SSM_STATE
    c = pl.program_id(1)

    @pl.when(c == 0)
    def _():
        xpad_ref[0:8, :] = jnp.zeros((8, xpad_ref.shape[1]), F32)
        state_ref[...] = jnp.zeros(state_ref.shape, F32)

    xpad_ref[8:8 + chunk, :] = xbc_ref[0]
    conv = cb_ref[...]
    for w in range(SSM_CONV):
        conv = conv + cw_ref[w:w + 1, :] * xpad_ref[pl.ds(8 - (SSM_CONV - 1) + w, chunk), :]
    xpad_ref[0:8, :] = xpad_ref[chunk:chunk + 8, :]
    u = _silu(conv)
    xs = u[:, :d_in]
    bmat = u[:, d_in:d_in + gn]
    cmat = u[:, d_in + gn:]

    dt = _softplus(dt_ref[0] + dtb_ref[...])
    ad = dt * (-jnp.exp(al_ref[...]))
    dtt = _softplus(dtt_ref[0] + dtbt_ref[...])
    adt = dtt * (-jnp.exp(alt_ref[...]))
    row = lax.broadcasted_iota(jnp.int32, (chunk, chunk), 0)
    col = lax.broadcasted_iota(jnp.int32, (chunk, chunk), 1)
    lower = row >= col
    tril = jnp.where(lower, 1.0, 0.0).astype(BF16)
    triu = jnp.where(row <= col, 1.0, 0.0).astype(BF16)
    cs = sum(_dot(tril, part) for part in _split_bf16(ad, 3))
    cst = sum(_dot(part, triu) for part in _split_bf16(adt, 3))

    heads_per_group = SSM_HEADS // SSM_GROUPS
    dsk = dsk_ref[...]
    ys = []
    for g in range(SSM_GROUPS):
        bg = bmat[:, g * SSM_STATE:(g + 1) * SSM_STATE]
        cg = cmat[:, g * SSM_STATE:(g + 1) * SSM_STATE].astype(BF16)
        cb = _dot_nt(cg, bg.astype(BF16))
        bgt = bg.T.astype(BF16)
        for r in range(heads_per_group):
            h = g * heads_per_group + r
            ccol = cs[:, h:h + 1]
            crow = cst[h:h + 1, :]
            decay = jnp.exp(jnp.where(lower, ccol - crow, NEG_INF))
            x_h = xs[:, h * SSM_HEAD_DIM:(h + 1) * SSM_HEAD_DIM]
            xdt = x_h * dt[:, h:h + 1]
            y = _dot((cb * decay).astype(BF16), xdt.astype(BF16))
            st = state_ref[h]
            y = y + _dot(cg, st.astype(BF16)) * jnp.exp(ccol)
            last = cst[h:h + 1, chunk - 1:chunk]
            to_end = jnp.exp(last - ccol)
            state_ref[h] = st * jnp.exp(last) + _dot(bgt, (xdt * to_end).astype(BF16))
            ys.append(y + x_h * dsk[:, h * SSM_HEAD_DIM:(h + 1) * SSM_HEAD_DIM])

    y = jnp.concatenate(ys, axis=1) * _silu(z_ref[0])
    gw = d_in // SSM_GROUPS
    for g in range(SSM_GROUPS):
        seg = y[:, g * gw:(g + 1) * gw]
        o_ref[0, :, g * gw:(g + 1) * gw] = _rms(seg, ng_ref[:, g * gw:(g + 1) * gw]).astype(o_ref.dtype)


def ssd_mixer(xbc, z, dt_raw, conv_w, conv_b, dt_bias, a_log, d_skip, norm_gain):
    b, s, cch = xbc.shape
    d_in = z.shape[2]
    nc = s // SSM_CHUNK
    hpad = dt_raw.shape[2]
    dtt = jnp.transpose(dt_raw[:, :, :SSM_HEADS], (0, 2, 1))

    def lane_pad(v):
        return jnp.pad(v.astype(F32), (0, hpad - SSM_HEADS)).reshape(1, hpad)

    args = (xbc, z, dt_raw, dtt, conv_w.astype(F32), conv_b.reshape(1, cch).astype(F32),
            lane_pad(dt_bias), dt_bias.reshape(SSM_HEADS, 1).astype(F32),
            lane_pad(a_log), a_log.reshape(SSM_HEADS, 1).astype(F32),
            jnp.repeat(d_skip.astype(F32), SSM_HEAD_DIM).reshape(1, d_in),
            norm_gain.reshape(1, d_in).astype(F32))
    const = lambda bb, c: (0, 0)
    in_specs = [pl.BlockSpec((1, SSM_CHUNK, cch), lambda bb, c: (bb, c, 0)),
                pl.BlockSpec((1, SSM_CHUNK, d_in), lambda bb, c: (bb, c, 0)),
                pl.BlockSpec((1, SSM_CHUNK, hpad), lambda bb, c: (bb, c, 0)),
                pl.BlockSpec((1, SSM_HEADS, SSM_CHUNK), lambda bb, c: (bb, 0, c)),
                pl.BlockSpec((SSM_CONV, cch), const), pl.BlockSpec((1, cch), const),
                pl.BlockSpec((1, hpad), const), pl.BlockSpec((SSM_HEADS, 1), const),
                pl.BlockSpec((1, hpad), const), pl.BlockSpec((SSM_HEADS, 1), const),
                pl.BlockSpec((1, d_in), const), pl.BlockSpec((1, d_in), const)]
    return pl.pallas_call(
        _ssd_body, grid=(b, nc), in_specs=in_specs,
        out_specs=pl.BlockSpec((1, SSM_CHUNK, d_in), lambda bb, c: (bb, c, 0)),
        out_shape=jax.ShapeDtypeStruct((b, s, d_in), BF16),
        scratch_shapes=[pltpu.VMEM((SSM_CHUNK + 8, cch), F32),
                        pltpu.VMEM((SSM_HEADS, SSM_STATE, SSM_HEAD_DIM), F32)],
        compiler_params=_cparams(("parallel", "arbitrary")), name="ssd_mixer",
    )(*args)


def _out_proj_body(x_ref, a_ref, b_ref, wa_ref, wb_ref, o_ref):
    o_ref[...] = x_ref[...] + _dot(a_ref[...], wa_ref[...]) + _dot(b_ref[...], wb_ref[...])


def out_proj_residual(x2, a, bm, wa, wb, tm=512):
    t, d = x2.shape
    return pl.pallas_call(
        _out_proj_body, grid=(t // tm,),
        in_specs=[pl.BlockSpec((tm, d), lambda i: (i, 0)),
                  pl.BlockSpec((tm, a.shape[1]), lambda i: (i, 0)),
                  pl.BlockSpec((tm, bm.shape[1]), lambda i: (i, 0)),
                  pl.BlockSpec(wa.shape, lambda i: (0, 0)),
                  pl.BlockSpec(wb.shape, lambda i: (0, 0))],
        out_specs=pl.BlockSpec((tm, d), lambda i: (i, 0)),
        out_shape=jax.ShapeDtypeStruct((t, d), F32),
        compiler_params=_cparams(("parallel",)), name="out_proj",
    )(x2, a, bm, wa, wb)


def _ffn_body(x_ref, g_ref, wg_ref, wu_ref, wd_ref, o_ref, h_ref):
    f = pl.program_id(1)

    @pl.when(f == 0)
    def _():
        x = x_ref[...]
        h_ref[...] = _rms(x, g_ref[...]).astype(BF16)
        o_ref[...] = x

    half = h_ref.shape[0] // 2
    for r in (slice(0, half), slice(half, 2 * half)):
        h = h_ref[r, :]
        act = (_silu(_dot(h, wg_ref[...])) * _dot(h, wu_ref[...])).astype(BF16)
        o_ref[r, :] += _dot(act, wd_ref[...])


def ffn_residual(x2, gain, w_gate, w_up, w_down, tm=1024, tf=1408):
    t, d = x2.shape
    d_ff = w_gate.shape[1]
    return pl.pallas_call(
        _ffn_body, grid=(t // tm, d_ff // tf),
        in_specs=[pl.BlockSpec((tm, d), lambda i, f: (i, 0)),
                  pl.BlockSpec((1, d), lambda i, f: (0, 0)),
                  pl.BlockSpec((d, tf), lambda i, f: (0, f)),
                  pl.BlockSpec((d, tf), lambda i, f: (0, f)),
                  pl.BlockSpec((tf, d), lambda i, f: (f, 0))],
        out_specs=pl.BlockSpec((tm, d), lambda i, f: (i, 0)),
        out_shape=jax.ShapeDtypeStruct((t, d), F32),
        scratch_shapes=[pltpu.VMEM((tm, d), BF16)],
        compiler_params=_cparams(("parallel", "arbitrary")), name="ffn",
    )(x2, gain.reshape(1, d).astype(F32), w_gate, w_up, w_down)


MOE_ROWS = 256
ROUTE_IDX = 0
ROUTE_W = 2
ROUTE_RANK = 4


def _moe_ffn_body(block_expert_ref, n_used_ref, xs_ref, wg_ref, wu_ref, wd_ref, o_ref):
    i = pl.program_id(0)

    @pl.when(i < n_used_ref[0])
    def _():
        x = xs_ref[...]
        act = (_silu(_dot(x, wg_ref[0])) * _dot(x, wu_ref[0])).astype(BF16)
        o_ref[...] = _dot(act, wd_ref[0]).astype(o_ref.dtype)

    @pl.when(i >= n_used_ref[0])
    def _():
        o_ref[...] = jnp.zeros(o_ref.shape, o_ref.dtype)


def moe_expert_ffn(xs, block_expert, n_used, w_gate, w_up, w_down):
    p, d = xs.shape
    d_ff = w_gate.shape[2]
    rows = MOE_ROWS
    grid_spec = pltpu.PrefetchScalarGridSpec(
        num_scalar_prefetch=2, grid=(p // rows,),
        in_specs=[pl.BlockSpec((rows, d), lambda i, be, nu: (i, 0)),
                  pl.BlockSpec((1, d, d_ff), lambda i, be, nu: (be[i], 0, 0)),
                  pl.BlockSpec((1, d, d_ff), lambda i, be, nu: (be[i], 0, 0)),
                  pl.BlockSpec((1, d_ff, d), lambda i, be, nu: (be[i], 0, 0))],
        out_specs=pl.BlockSpec((rows, d), lambda i, be, nu: (i, 0)))
    return pl.pallas_call(
        _moe_ffn_body, grid_spec=grid_spec, out_shape=jax.ShapeDtypeStruct((p, d), BF16),
        compiler_params=pltpu.CompilerParams(
            dimension_semantics=("arbitrary",), vmem_limit_bytes=MOE_VMEM_LIMIT),
        name="moe_expert_ffn",
    )(block_expert, n_used, xs, w_gate, w_up, w_down)


def _moe_combine_body(x_ref, y_ref, route_ref, o_ref):
    route = route_ref[...]
    d = x_ref.shape[1]
    lane = lax.broadcasted_iota(jnp.int32, route.shape, 1)
    w0 = jnp.sum(jnp.where(lane == ROUTE_W, route, 0.0), axis=1, keepdims=True)
    w1 = jnp.sum(jnp.where(lane == ROUTE_W + 1, route, 0.0), axis=1, keepdims=True)
    o_ref[...] = x_ref[...] + w0 * y_ref[:, :d].astype(F32) + w1 * y_ref[:, d:].astype(F32)


def moe_combine(x2, y_pairs, route, tm=512):
    t, d = x2.shape
    row = lambda i: (i, 0)
    return pl.pallas_call(
        _moe_combine_body, grid=(t // tm,),
        in_specs=[pl.BlockSpec((tm, d), row), pl.BlockSpec((tm, 2 * d), row), pl.BlockSpec((tm, LANES), row)],
        out_specs=pl.BlockSpec((tm, d), row),
        out_shape=jax.ShapeDtypeStruct((t, d), F32),
        compiler_params=_cparams(("parallel",)), name="moe_combine",
    )(x2, y_pairs, route)


def moe_residual(x2, h, route, counts, w_gate, w_up, w_down):
    t, d = x2.shape
    n_e = w_gate.shape[0]
    rows = MOE_ROWS
    expert = route[:, ROUTE_IDX:ROUTE_IDX + 2].astype(jnp.int32).reshape(-1)
    rank = route[:, ROUTE_RANK:ROUTE_RANK + 2].astype(jnp.int32).reshape(-1)
    padded = (counts[0, :n_e].astype(jnp.int32) + rows - 1) // rows * rows
    ends = jnp.cumsum(padded)
    own = expert[:, None] == jnp.arange(n_e, dtype=jnp.int32)[None, :]
    slot = jnp.sum(jnp.where(own, (ends - padded)[None, :], 0), axis=1) + rank
    p_rows = 2 * t + n_e * rows
    token_of_slot = jnp.zeros((p_rows,), jnp.int32).at[slot].set(
        jnp.arange(2 * t, dtype=jnp.int32) // 2, unique_indices=True)
    block_start = jnp.arange(p_rows // rows, dtype=jnp.int32) * rows
    block_expert = jnp.minimum(jnp.searchsorted(ends, block_start, side="right"), n_e - 1).astype(jnp.int32)
    n_used = (ends[-1] // rows).astype(jnp.int32).reshape(1)

    take_rows = lambda a, idx: a.at[idx].get(mode="promise_in_bounds")
    xs = take_rows(h, token_of_slot)
    ys = moe_expert_ffn(xs, block_expert, n_used, w_gate, w_up, w_down)
    y_pairs = take_rows(ys, slot).reshape(t, 2 * d)
    return moe_combine(x2, y_pairs, route)


def _router_body(x_ref, g_ref, r_ref, h_ref, o_ref, count_ref, run_ref):
    @pl.when(pl.program_id(0) == 0)
    def _():
        run_ref[...] = jnp.zeros(run_ref.shape, F32)

    h = _rms(x_ref[...], g_ref[...])
    h_ref[...] = h.astype(h_ref.dtype)
    logits = jnp.dot(h, r_ref[...], precision=lax.Precision.HIGHEST, preferred_element_type=F32)
    lane = lax.broadcasted_iota(jnp.int32, logits.shape, 1).astype(F32)
    low = jnp.float32(-3.0e38)
    logits = jnp.where(lane < N_EXPERTS, logits, low)
    m1 = jnp.max(logits, axis=1, keepdims=True)
    i1 = jnp.min(jnp.where(logits == m1, lane, float(LANES)), axis=1, keepdims=True)
    rest = jnp.where(lane == i1, low, logits)
    m2 = jnp.max(rest, axis=1, keepdims=True)
    i2 = jnp.min(jnp.where(rest == m2, lane, float(LANES)), axis=1, keepdims=True)
    ex = jnp.exp(m2 - m1)
    w1 = 1.0 / (1.0 + ex)
    w2 = ex / (1.0 + ex)
    tm = logits.shape[0]
    routed = jnp.where((lane == i1) | (lane == i2), 1.0, 0.0)
    row = lax.broadcasted_iota(jnp.int32, (tm, tm), 0)
    col = lax.broadcasted_iota(jnp.int32, (tm, tm), 1)
    before = jnp.where(col < row, 1.0, 0.0).astype(BF16)
    rank = run_ref[0:1, :] + _dot(before, routed.astype(BF16))
    r1 = jnp.sum(jnp.where(lane == i1, rank, 0.0), axis=1, keepdims=True)
    r2 = jnp.sum(jnp.where(lane == i2, rank, 0.0), axis=1, keepdims=True)
    fields = ((ROUTE_IDX, i1), (ROUTE_IDX + 1, i2), (ROUTE_W, w1), (ROUTE_W + 1, w2),
              (ROUTE_RANK, r1), (ROUTE_RANK + 1, r2))
    out = jnp.zeros(logits.shape, F32)
    for pos, val in fields:
        out = jnp.where(lane == pos, val, out)
    o_ref[...] = out
    run_ref[...] = run_ref[...] + jnp.sum(routed, axis=0, keepdims=True)
    count_ref[...] = run_ref[...]


def router(x2, gain, router_w, tm=512):
    t, d = x2.shape
    r_pad = jnp.pad(router_w.astype(F32), ((0, 0), (0, LANES - router_w.shape[1])))
    return pl.pallas_call(
        _router_body, grid=(t // tm,),
        in_specs=[pl.BlockSpec((tm, d), lambda i: (i, 0)),
                  pl.BlockSpec((1, d), lambda i: (0, 0)),
                  pl.BlockSpec((d, LANES), lambda i: (0, 0))],
        out_specs=[pl.BlockSpec((tm, d), lambda i: (i, 0)), pl.BlockSpec((tm, LANES), lambda i: (i, 0)),
                   pl.BlockSpec((SUBLANES, LANES), lambda i: (0, 0))],
        out_shape=[jax.ShapeDtypeStruct((t, d), BF16), jax.ShapeDtypeStruct((t, LANES), F32),
                   jax.ShapeDtypeStruct((SUBLANES, LANES), F32)],
        scratch_shapes=[pltpu.VMEM((SUBLANES, LANES), F32)],
        compiler_params=_cparams(("arbitrary",)), name="router",
    )(x2, gain.reshape(1, d).astype(F32), r_pad)


def _compress_body(ch_ref, nx_ref, pos_ref, w1_ref, w2_ref, gain_ref, cos_ref, sin_ref, rot_ref, o_ref,
                   *, is_key):
    a = _dot((ch_ref[0] + pos_ref[0]).astype(BF16), w1_ref[0, 0])
    a = a + _dot((nx_ref[0] + pos_ref[1]).astype(BF16), w1_ref[0, 1])
    out = _dot(_silu(a).astype(BF16), w2_ref[...])
    if is_key:
        out = _rms(out, gain_ref[...])
        hi, lo = _split_bf16(out, 2)
        partner = _dot(hi, rot_ref[...]) + _dot(lo, rot_ref[...])
        out = out * cos_ref[...] + partner * sin_ref[...]
    o_ref[0, 0] = out


def nsa_compress(t, pos, w1, w2, gain, seq, is_key):
    b, s, _ = t.shape
    g, d = NSA_KV_GROUPS, NSA_HEAD_DIM
    n_ch = s // NSA_CMP_STRIDE
    half = NSA_CMP_STRIDE * g * d
    ch = t.reshape(b, n_ch, half)
    nxt = jnp.concatenate([ch[:, 1:], jnp.zeros((b, 1, half), F32)], axis=1)
    pos2 = jnp.broadcast_to(pos.astype(F32).reshape(2, NSA_CMP_STRIDE, 1, d),
                            (2, NSA_CMP_STRIDE, g, d)).reshape(2, 1, half)
    w1r = w1.astype(BF16).reshape(2, NSA_CMP_STRIDE, 1, d, d)
    own = (jnp.arange(g)[:, None] == jnp.arange(g)[None, :]).reshape(g, 1, 1, g, 1, 1)
    w1s = jnp.where(own, w1r[None], jnp.zeros((), BF16)).reshape(g, 2, half, d)
    cmp_end = jnp.arange(n_ch) * NSA_CMP_STRIDE + NSA_CMP_BLOCK - 1
    inv_freq = 1.0 / (ROPE_THETA ** (jnp.arange(0, d, 2, dtype=F32) / d))
    ang = cmp_end.astype(F32)[:, None] * inv_freq[None, :]
    cos = jnp.concatenate([jnp.cos(ang)] * 2, axis=1)
    sin = jnp.concatenate([jnp.sin(ang)] * 2, axis=1)
    rot = np.zeros((d, d), np.float32)
    rot[np.arange(d // 2) + d // 2, np.arange(d // 2)] = -1.0
    rot[np.arange(d // 2), np.arange(d // 2) + d // 2] = 1.0
    blk = lambda bb, gg: (bb, gg, 0, 0)
    seq = lambda bb, gg: (bb, 0, 0)
    c2 = lambda bb, gg: (0, 0)
    c3 = lambda bb, gg: (0, 0, 0)
    return pl.pallas_call(
        functools.partial(_compress_body, is_key=is_key), grid=(b, g),
        in_specs=[pl.BlockSpec((1, n_ch, half), seq), pl.BlockSpec((1, n_ch, half), seq),
                  pl.BlockSpec((2, 1, half), c3), pl.BlockSpec((1, 2, half, d), lambda bb, gg: (gg, 0, 0, 0)),
                  pl.BlockSpec((d, d), c2), pl.BlockSpec((1, d), c2),
                  pl.BlockSpec((n_ch, d), c2), pl.BlockSpec((n_ch, d), c2), pl.BlockSpec((d, d), c2)],
        out_specs=pl.BlockSpec((1, 1, n_ch, d), blk),
        out_shape=jax.ShapeDtypeStruct((b, g, n_ch, d), F32),
        compiler_params=_cparams(("parallel", "parallel")), name="nsa_compress",
    )(ch, nxt, pos2, w1s, w2.astype(BF16), gain.reshape(1, d).astype(F32), cos, sin,
      jnp.asarray(rot, dtype=BF16))


def _nsa_body(q_ref, ck_ref, cvt_ref, ksl_ref, vslt_ref, kwn_ref, vwnt_ref, ovt_ref, glt_ref, o_ref,
              sc_ref, phi_ref, plo_ref, imp_ref, bias_ref, ss_ref, ps_ref, ss1_ref, ps1_ref, sw_ref, pw_ref,
              ow_ref, oc_ref, *, tq):
    g = pl.program_id(1)
    i = pl.program_id(2)
    d = NSA_HEAD_DIM
    rep = NSA_HEADS // NSA_KV_GROUPS
    t0 = i * tq
    n_cmp = ck_ref.shape[1]
    n_sel = ovt_ref.shape[0]
    width = rep * tq
    sub = NSA_SEL_BLOCK
    dead = 0.5 * NEG_INF
    v_rows = pl.ds(pl.multiple_of(g * d, d), d)

    qb = q_ref[0]
    q4 = jnp.concatenate([qb[:, r * d:(r + 1) * d] for r in range(rep)], axis=0)
    q4 = jnp.concatenate([q4, q4], axis=1)
    lane = lax.broadcasted_iota(jnp.int32, q4.shape, 1)
    q4 = jnp.where(jnp.right_shift(lane, d.bit_length() - 1) == g, q4, jnp.zeros_like(q4))

    def qpos_of(shape):
        return t0 + (lax.broadcasted_iota(jnp.int32, shape, 1) & (tq - 1))

    def compressed(rows):
        s = _dot_nt(ck_ref[0, 0:rows, :], q4)
        cmp_end = lax.broadcasted_iota(jnp.int32, s.shape, 0) * NSA_CMP_STRIDE + (NSA_CMP_BLOCK - 1)
        s = jnp.where(cmp_end <= qpos_of(s.shape), s, NEG_INF)
        sc_ref[0:rows, :] = s
        m_c = jnp.max(s, axis=0, keepdims=True)
        lpart = jnp.zeros((SUBLANES, width), F32)
        for r in range(rows // sub):
            e = jnp.exp2(sc_ref[r * sub:(r + 1) * sub, :] - m_c)
            lpart = lpart + _fold_rows(e)
            hi = e.astype(BF16)
            phi_ref[r * sub:(r + 1) * sub, :] = hi
            plo_ref[r * sub:(r + 1) * sub, :] = (e - hi.astype(F32)).astype(BF16)
        inv_c = jnp.where(m_c > dead, 1.0 / jnp.sum(lpart, axis=0, keepdims=True), 0.0)
        oc_ref[...] = _dot(cvt_ref[0, v_rows, 0:rows], phi_ref[0:rows, :]) * inv_c
        ovt = ovt_ref[:, 0:rows]
        imp4 = (_dot(ovt, phi_ref[0:rows, :]) + _dot(ovt, plo_ref[0:rows, :])) * inv_c
        imp = imp4[:, 0:tq]
        for r in range(1, rep):
            imp = imp + imp4[:, r * tq:(r + 1) * tq]
        imp_ref[...] = imp

    cmp_live = jnp.clip((t0 + tq - NSA_CMP_BLOCK) // NSA_CMP_STRIDE + 1, 1, n_cmp)
    cmp_step = min(2 * LANES, n_cmp)
    for v in range(n_cmp // cmp_step):
        @pl.when((cmp_live > cmp_step * v) & (cmp_live <= cmp_step * (v + 1)))
        def _():
            compressed(cmp_step * (v + 1))

    o_c = oc_ref[...]
    imp = imp_ref[...]

    blk = lax.broadcasted_iota(jnp.int32, imp.shape, 0)
    qp = t0 + lax.broadcasted_iota(jnp.int32, imp.shape, 1)
    cur = jnp.right_shift(qp, NSA_SEL_BLOCK.bit_length() - 1)
    forced = (blk == 0) | (blk == cur) | (blk == cur - 1)
    future = blk * NSA_SEL_BLOCK > qp
    imp_ref[...] = jnp.where(future, -FORCE_SCORE, jnp.where(forced, FORCE_SCORE, imp))
    bias_ref[...] = jnp.full(bias_ref.shape, NEG_INF, F32)

    n_live = jnp.minimum((t0 + tq - 1) // NSA_SEL_BLOCK + 1, n_sel)
    n_var = max(n_sel // 32, 1)
    rows_per = n_sel // n_var
    top_n = float(min(NSA_TOP_N, n_sel))
    for v in range(n_var):
        rows = rows_per * (v + 1)

        @pl.when((n_live > rows_per * v) & (n_live <= rows))
        def _():
            groups = rows // SUBLANES
            mine = [imp_ref[gi * SUBLANES:(gi + 1) * SUBLANES, :] for gi in range(groups)]
            rank = [jnp.zeros((SUBLANES, tq), F32) for _ in range(groups)]
            in_group = lax.broadcasted_iota(jnp.int32, (SUBLANES, tq), 0)
            for i2 in range(rows):
                other = imp_ref[i2:i2 + 1, :]
                for gi in range(groups):
                    if gi > i2 // SUBLANES:
                        beats = other >= mine[gi]
                    elif gi < i2 // SUBLANES:
                        beats = other > mine[gi]
                    else:
                        beats = (other > mine[gi]) | ((other == mine[gi]) & (in_group > i2 % SUBLANES))
                    rank[gi] = rank[gi] + jnp.where(beats, 1.0, 0.0)
            for gi in range(groups):
                bias = jnp.where(rank[gi] < top_n, 0.0, NEG_INF)
                bias_ref[gi * SUBLANES:(gi + 1) * SUBLANES, :] = jnp.concatenate([bias] * rep, axis=1)

    init = (jnp.full((1, width), NEG_INF, F32), jnp.zeros((d + ONES_ROWS, width), F32))

    def normalised(acc):
        return acc[:d, :] / acc[d:d + 1, :]

    chunk = 8 * sub
    n_sub = chunk // sub

    s_slots = (ss_ref, ss1_ref)
    p_slots = (ps_ref, ps1_ref)

    def sel_scores(c, slot, diagonal=False):
        start = pl.multiple_of(c * chunk, chunk)
        s = _dot_nt(ksl_ref[0, pl.ds(start, chunk), :], q4)
        if diagonal:
            kpos = start + lax.broadcasted_iota(jnp.int32, s.shape, 0)
            s = jnp.where(kpos <= qpos_of(s.shape), s, NEG_INF)
        s_slots[slot][...] = s

    def sel_update(c, slot, carry):
        m_prev, acc = carry
        s_ref, p_ref = s_slots[slot], p_slots[slot]
        biases = [bias_ref[pl.ds(c * n_sub + r, 1), :] for r in range(n_sub)]
        m8 = jnp.full((SUBLANES, width), NEG_INF, F32)
        for r in range(n_sub):
            block = s_ref[r * sub:(r + 1) * sub, :]
            m8 = jnp.maximum(m8, jnp.max(block.reshape(sub // SUBLANES, SUBLANES, width), axis=0) + biases[r])
        m_new = jnp.maximum(m_prev, jnp.max(m8, axis=0, keepdims=True))
        live = m_new > dead
        for r in range(n_sub):
            shift = jnp.where(live, biases[r] - m_new, NEG_INF)
            p_ref[r * sub:(r + 1) * sub, :] = jnp.exp2(s_ref[r * sub:(r + 1) * sub, :] + shift).astype(BF16)
        vt = _with_ones_rows(vslt_ref[v_rows, pl.ds(pl.multiple_of(c * chunk, chunk), chunk)])
        return m_new, jnp.exp2(m_prev - m_new) * acc + _dot(vt, p_ref[...])

    c_diag = (t0 + tq - 1) // chunk
    n_chunks = c_diag + 1
    last_past = jnp.maximum(c_diag - 1, 0)

    def chunk_at(j):
        return jnp.where(j == 0, c_diag, j - 1)

    sel_scores(c_diag, 0, diagonal=True)

    def pair(k, carry):
        sel_scores(jnp.minimum(2 * k, last_past), 1)
        carry = sel_update(chunk_at(2 * k), 0, carry)
        sel_scores(jnp.minimum(2 * k + 1, last_past), 0)
        return sel_update(2 * k, 1, carry)

    carry = lax.fori_loop(0, n_chunks // 2, pair, init)
    _, acc_s = lax.cond(n_chunks % 2 == 1, lambda cr: sel_update(chunk_at(n_chunks - 1), 0, cr),
                        lambda cr: cr, carry)
    o_s = normalised(acc_s)

    def win_chunk(c, carry):
        m_prev, acc = carry
        start = pl.multiple_of(c * tq, tq)
        s = _dot_nt(kwn_ref[0, pl.ds(start, tq), :], q4)
        kpos = start + lax.broadcasted_iota(jnp.int32, s.shape, 0)
        qpos = qpos_of(s.shape)
        s = jnp.where((kpos <= qpos) & (kpos > qpos - NSA_WINDOW), s, NEG_INF)
        m_new = jnp.maximum(m_prev, jnp.max(s, axis=0, keepdims=True))
        p = jnp.exp2(s + jnp.where(m_new > dead, -m_new, NEG_INF))
        vt = _with_ones_rows(vwnt_ref[v_rows, pl.ds(start, tq)])
        return m_new, jnp.exp2(m_prev - m_new) * acc + _dot(vt, p.astype(BF16))

    n_back = NSA_WINDOW // tq

    @pl.when(i < n_back)
    def _():
        _, acc_w = lax.fori_loop(0, i + 1, win_chunk, init)
        ow_ref[...] = normalised(acc_w)

    @pl.when(i >= n_back)
    def _():
        start = pl.multiple_of(t0 - NSA_WINDOW, tq)
        s = _dot_nt(kwn_ref[0, pl.ds(start, NSA_WINDOW + tq), :], q4)
        kpos = start + lax.broadcasted_iota(jnp.int32, (tq, width), 0)
        qpos = qpos_of((tq, width))
        sw_ref[0:tq, :] = jnp.where(kpos > qpos - NSA_WINDOW, s[0:tq, :], NEG_INF)
        sw_ref[tq:NSA_WINDOW, :] = s[tq:NSA_WINDOW, :]
        sw_ref[NSA_WINDOW:, :] = jnp.where(kpos + NSA_WINDOW <= qpos, s[NSA_WINDOW:, :], NEG_INF)
        m_w = jnp.max(sw_ref[...], axis=0, keepdims=True)
        for r in range((NSA_WINDOW + tq) // sub):
            pw_ref[r * sub:(r + 1) * sub, :] = jnp.exp2(sw_ref[r * sub:(r + 1) * sub, :] - m_w).astype(BF16)
        vt = _with_ones_rows(vwnt_ref[v_rows, pl.ds(start, NSA_WINDOW + tq)])
        ow_ref[...] = normalised(_dot(vt, pw_ref[...]))

    o_w = ow_ref[...]

    def gate(branch):
        rows = [glt_ref[pl.ds((g * rep + r) * 3 + branch, 1), :] for r in range(rep)]
        return _sigmoid(jnp.concatenate(rows, axis=1))

    out = gate(0) * o_c + gate(1) * o_s + gate(2) * o_w
    out_t = jnp.concatenate([out, jnp.zeros_like(out)], axis=0).T
    o_ref[0] = jnp.concatenate([out_t[r * tq:(r + 1) * tq, :d] for r in range(rep)],
                               axis=1).astype(o_ref.dtype)


def nsa_overlap_t(n_cmp, n_sel):
    c_start = np.arange(n_cmp)[None, :] * NSA_CMP_STRIDE
    s_start = np.arange(n_sel)[:, None] * NSA_SEL_BLOCK
    hit = (c_start < s_start + NSA_SEL_BLOCK) & (c_start + NSA_CMP_BLOCK > s_start)
    hit = hit & (np.arange(n_cmp)[None, :] < n_cmp - NSA_CMP_BLOCK // NSA_CMP_STRIDE + 1)
    return jnp.asarray(hit.astype(np.float32), dtype=BF16)


def nsa_attention(qn, ck, cvt, ksl, vslt, kwn, vwnt, glt, tq=256):
    b, s, _ = qn.shape
    g, d = NSA_KV_GROUPS, NSA_HEAD_DIM
    rep = NSA_HEADS // g
    n_cmp = ck.shape[1]
    n_sel = s // NSA_SEL_BLOCK
    nq = s // tq
    ovt = nsa_overlap_t(n_cmp, n_sel)
    width = rep * tq
    chunk = 8 * NSA_SEL_BLOCK
    full3 = lambda bb, gg, i: (bb, 0, 0)
    seq_t = lambda bb, gg, i: (0, bb)
    return pl.pallas_call(
        functools.partial(_nsa_body, tq=tq), grid=(b, g, nq),
        in_specs=[pl.BlockSpec((1, tq, rep * d), lambda bb, gg, i: (bb, i, gg)),
                  pl.BlockSpec((1, n_cmp, g * d), full3), pl.BlockSpec((1, g * d, n_cmp), full3),
                  pl.BlockSpec((1, s, g * d), full3), pl.BlockSpec((g * d, s), seq_t),
                  pl.BlockSpec((1, s, g * d), full3), pl.BlockSpec((g * d, s), seq_t),
                  pl.BlockSpec((n_sel, n_cmp), lambda bb, gg, i: (0, 0)),
                  pl.BlockSpec((glt.shape[0], tq), lambda bb, gg, i: (0, bb * nq + i))],
        out_specs=pl.BlockSpec((1, tq, rep * d), lambda bb, gg, i: (bb, i, gg)),
        out_shape=jax.ShapeDtypeStruct((b, s, g * rep * d), BF16),
        scratch_shapes=[pltpu.VMEM((n_cmp, width), F32), pltpu.VMEM((n_cmp, width), BF16),
                        pltpu.VMEM((n_cmp, width), BF16), pltpu.VMEM((n_sel, tq), F32),
                        pltpu.VMEM((n_sel, width), F32), pltpu.VMEM((chunk, width), F32),
                        pltpu.VMEM((chunk, width), BF16), pltpu.VMEM((chunk, width), F32),
                        pltpu.VMEM((chunk, width), BF16), pltpu.VMEM((NSA_WINDOW + tq, width), F32),
                        pltpu.VMEM((NSA_WINDOW + tq, width), BF16), pltpu.VMEM((d, width), F32),
                        pltpu.VMEM((d, width), F32)],
        compiler_params=_cparams(("parallel", "parallel", "arbitrary")), name="nsa_attention",
    )(qn, ck, cvt, ksl, vslt, kwn, vwnt, ovt, glt)


def _pad_cols(w, n):
    return jnp.pad(w, ((0, 0), (0, n - w.shape[1])))


def _even_layer(x2, b, s, layer_idx, norm_mix, w_in, q_gain, k_gain, lam, subln_gain, conv_w, conv_b,
                dt_bias, a_log, d_skip, ssm_norm_gain, w_out, norm_ffn, w_gate, w_up, w_down):
    nq = DA_HEADS * 2 * DA_HEAD_DIM
    nv = DA_HEADS * DA_V_DIM
    cch = SSM_D_INNER + 2 * SSM_GROUPS * SSM_STATE
    offs = np.cumsum([0, nq, nq, nv, SSM_D_INNER, cch, SSM_HEADS])
    wb = w_in.astype(BF16)
    pieces = [wb[:, offs[k]:offs[k + 1]] for k in range(6)]
    pieces[2] = pieces[2].T
    pieces[5] = _pad_cols(pieces[5], LANES)
    posts = [HeadNorm(q_gain, DA_HEAD_DIM, rope=True, mul=DA_HEAD_DIM ** -0.5 * LOG2E),
             HeadNorm(k_gain, DA_HEAD_DIM, rope=True), None, None, None, None]
    q, k, vt, z, xbc, dt = norm_proj(x2, norm_mix, pieces, [BF16, BF16, BF16, F32, F32, F32], posts, s,
                                     _rope_tables(s, DA_HEAD_DIM), transposed=(2,))
    qn = q.reshape(b, s, nq)
    kn = k.reshape(b, s, nq)
    lam_init = 0.8 - 0.6 * math.exp(-0.3 * layer_idx)
    lf = lam.astype(F32)
    lam_full = jnp.exp(jnp.sum(lf[0] * lf[1])) - jnp.exp(jnp.sum(lf[2] * lf[3])) + lam_init
    a_out = flash_attention(lam_full.reshape(1), [qn], [kn], vt, subln_gain, DA_HEADS, DA_V_DIM,
                            diff=True, out_scale=1.0 - lam_init)
    b_out = ssd_mixer(xbc.reshape(b, s, cch), z.reshape(b, s, SSM_D_INNER), dt.reshape(b, s, LANES),
                      conv_w, conv_b, dt_bias, a_log, d_skip, ssm_norm_gain)
    wo = w_out.astype(BF16)
    x2 = out_proj_residual(x2, a_out.reshape(-1, nv), b_out.reshape(-1, SSM_D_INNER), wo[:nv], wo[nv:])
    return ffn_residual(x2, norm_ffn, w_gate.astype(BF16), w_up.astype(BF16), w_down.astype(BF16))


def _odd_layer(x2, b, s, norm_mix, w_in, q_gain, k_gain, cmp_pos, cmp_w1, cmp_w2, cq_gain, ckv_gain,
               w_uq, w_ukv, qn_gain, qr_gain, kn_gain, kr_gain, w_out, norm_ffn, router_w, w_gate, w_up,
               w_down):
    g, d = NSA_KV_GROUPS, NSA_HEAD_DIM
    nq = NSA_HEADS * d
    nkv = g * d
    sizes = [nq] + [nkv] * 6 + [NSA_HEADS * 3, w_uq.shape[0], w_ukv.shape[0], MLA_ROPE_DIM]
    offs = np.cumsum([0] + sizes)
    wb = w_in.astype(BF16)
    pieces = [wb[:, offs[k]:offs[k + 1]] for k in range(len(sizes))]
    for k in (4, 6):
        pieces[k] = pieces[k].T
    pieces[7] = jnp.pad(pieces[7].T, ((0, 32 - NSA_HEADS * 3), (0, 0)))
    pieces[10] = _pad_cols(pieces[10], LANES)
    tables = _rope_tables(s, d)
    posts = [None] * len(sizes)
    posts[0] = HeadNorm(q_gain, d, rope=True, mul=d ** -0.5 * LOG2E)
    posts[3] = HeadNorm(k_gain[1], d, rope=True)
    posts[5] = HeadNorm(k_gain[2], d, rope=True)
    posts[10] = HeadNorm(kr_gain, MLA_ROPE_DIM, rope=True)
    (q, kc, vc, ksl, vslt, kwn, vwnt, glt, cq, ckv, k_rope) = norm_proj(
        x2, norm_mix, pieces, [BF16, F32, F32, BF16, BF16, BF16, BF16, F32, F32, F32, BF16], posts, s, tables,
        transposed=(4, 6, 7))

    qn = q.reshape(b, s, nq)
    ksl_n = ksl.reshape(b, s, nkv)
    kwn_n = kwn.reshape(b, s, nkv)
    ck = nsa_compress(kc.reshape(b, s, nkv), cmp_pos[0], cmp_w1[0], cmp_w2[0], k_gain[0], s, True)
    cv = nsa_compress(vc.reshape(b, s, nkv), cmp_pos[1], cmp_w1[1], cmp_w2[1], k_gain[0], s, False)
    n_cmp = ck.shape[2]
    ck = ck.transpose(0, 2, 1, 3).reshape(b, n_cmp, nkv).astype(BF16)
    cvt = cv.transpose(0, 1, 3, 2).reshape(b, nkv, n_cmp).astype(BF16)
    c_out = nsa_attention(qn, ck, cvt, ksl_n, vslt, kwn_n, vwnt, glt).reshape(b * s, nq)

    h = MLA_HEADS
    dqk = MLA_NOPE_DIM + MLA_ROPE_DIM
    wq = w_uq.astype(BF16).reshape(-1, h, dqk)
    wq_nope = wq[:, :, :MLA_NOPE_DIM].reshape(-1, h * MLA_NOPE_DIM)
    wq_rope = jnp.pad(wq[:, :, MLA_NOPE_DIM:], ((0, 0), (0, 0), (0, LANES - MLA_ROPE_DIM)))
    wq_rope = wq_rope.reshape(-1, h * LANES)
    wkv = w_ukv.astype(BF16).reshape(-1, h, MLA_NOPE_DIM + MLA_V_DIM)
    wk_nope = wkv[:, :, :MLA_NOPE_DIM].reshape(-1, h * MLA_NOPE_DIM)
    wv = wkv[:, :, MLA_NOPE_DIM:].reshape(-1, h * MLA_V_DIM)
    q_mul = dqk ** -0.5 * LOG2E
    q_nope, q_rope = norm_proj(
        cq, cq_gain, [wq_nope, wq_rope], [BF16, BF16],
        [HeadNorm(qn_gain, MLA_NOPE_DIM, mul=q_mul), HeadNorm(qr_gain, MLA_ROPE_DIM, rope=True, mul=q_mul)],
        s, tables)
    k_nope, vt = norm_proj(ckv, ckv_gain, [wk_nope, wv.T], [BF16, BF16],
                           [HeadNorm(kn_gain, MLA_NOPE_DIM), None], transposed=(1,))
    shp = lambda t: t.reshape(b, s, t.shape[-1])
    d_out = flash_attention(jnp.zeros((1,), F32), [shp(q_nope), shp(q_rope)], [shp(k_nope), shp(k_rope)],
                            vt, jnp.ones((MLA_V_DIM,), F32), h, MLA_V_DIM, diff=False)

    wo = w_out.astype(BF16)
    x2 = out_proj_residual(x2, c_out, d_out.reshape(b * s, h * MLA_V_DIM), wo[:nq], wo[nq:])
    h, route, counts = router(x2, norm_ffn, router_w)
    return moe_residual(x2, h, route, counts, w_gate.astype(BF16), w_up.astype(BF16), w_down.astype(BF16))


def kernel(x, ev_norm_mix, ev_w_in, da_q_gain, da_k_gain, da_lambda, da_subln_gain, ssm_conv_w, ssm_conv_b, ssm_dt_bias, ssm_a_log, ssm_d, ssm_norm_gain, ev_w_out, ev_norm_ffn, ffn_w_gate, ffn_w_up, ffn_w_down, od_norm_mix, od_w_in, nsa_q_gain, nsa_k_gain, nsa_cmp_pos, nsa_cmp_w1, nsa_cmp_w2, mla_cq_gain, mla_ckv_gain, mla_w_uq, mla_w_ukv, mla_qn_gain, mla_qr_gain, mla_kn_gain, mla_kr_gain, od_w_out, od_norm_ffn, moe_router, moe_w_gate, moe_w_up, moe_w_down):
    b, s, d = x.shape
    x2 = x.reshape(b * s, d)
    depth = ev_norm_mix.shape[0] + od_norm_mix.shape[0]
    for layer in range(depth):
        i = layer // 2
        if layer % 2 == 0:
            x2 = _even_layer(x2, b, s, layer, ev_norm_mix[i], ev_w_in[i], da_q_gain[i], da_k_gain[i],
                             da_lambda[i], da_subln_gain[i], ssm_conv_w[i], ssm_conv_b[i],
                             ssm_dt_bias[i], ssm_a_log[i], ssm_d[i], ssm_norm_gain[i], ev_w_out[i],
                             ev_norm_ffn[i], ffn_w_gate[i], ffn_w_up[i], ffn_w_down[i])
        else:
            x2 = _odd_layer(x2, b, s, od_norm_mix[i], od_w_in[i], nsa_q_gain[i], nsa_k_gain[i],
                            nsa_cmp_pos[i], nsa_cmp_w1[i], nsa_cmp_w2[i], mla_cq_gain[i],
                            mla_ckv_gain[i], mla_w_uq[i], mla_w_ukv[i], mla_qn_gain[i], mla_qr_gain[i],
                            mla_kn_gain[i], mla_kr_gain[i], od_w_out[i], od_norm_ffn[i], moe_router[i],
                            moe_w_gate[i], moe_w_up[i], moe_w_down[i])
    return x2.reshape(b, s, d)
```

```python
import functools
import math

import numpy as np
import jax
import jax.numpy as jnp
from jax import lax
from jax.experimental import pallas as pl
from jax.experimental.pallas import tpu as pltpu

F32 = jnp.float32
BF16 = jnp.bfloat16

ROPE_THETA = 10000.0
NORM_EPS = 1e-6
NEG_INF = -1e30
FORCE_SCORE = 1e6
LOG2E = 1.4426950408889634

DA_HEADS = 4
DA_HEAD_DIM = 64
DA_V_DIM = 2 * DA_HEAD_DIM
SSM_HEADS = 8
SSM_HEAD_DIM = 64
SSM_D_INNER = SSM_HEADS * SSM_HEAD_DIM
SSM_GROUPS = 2
SSM_STATE = 128
SSM_CONV = 4
SSM_CHUNK = 256
NSA_HEADS = 8
NSA_KV_GROUPS = 2
NSA_HEAD_DIM = 64
NSA_CMP_BLOCK = 32
NSA_CMP_STRIDE = 16
NSA_SEL_BLOCK = 64
NSA_TOP_N = 16
NSA_WINDOW = 512
MLA_HEADS = 4
MLA_NOPE_DIM = 128
MLA_ROPE_DIM = 64
MLA_V_DIM = 128
N_EXPERTS = 8

LANES = 128
SUBLANES = 8
VMEM_LIMIT = 48 * 1024 * 1024
MOE_VMEM_LIMIT = 58 * 1024 * 1024

NT_DIMS = (((1,), (1,)), ((), ()))


def _cparams(semantics):
    return pltpu.CompilerParams(dimension_semantics=semantics, vmem_limit_bytes=VMEM_LIMIT)


def _dot(a, b):
    return jnp.dot(a, b, preferred_element_type=F32)


def _dot_nt(a, b):
    return lax.dot_general(a, b, NT_DIMS, preferred_element_type=F32)


def _split_bf16(x, parts):
    out = []
    for _ in range(parts):
        hi = x.astype(BF16)
        out.append(hi)
        x = x - hi.astype(F32)
    return out


def _fold_rows(x):
    return jnp.sum(x.reshape(x.shape[0] // SUBLANES, SUBLANES, x.shape[1]), axis=0)


ONES_ROWS = 16


def _with_ones_rows(vt):
    return jnp.concatenate([vt, jnp.ones((ONES_ROWS, vt.shape[1]), vt.dtype)], axis=0)


def _sigmoid(x):
    return 1.0 / (1.0 + jnp.exp(-x))


def _silu(x):
    return x * _sigmoid(x)


def _softplus(x):
    return jnp.maximum(x, 0.0) + jnp.log(1.0 + jnp.exp(-jnp.abs(x)))


def _rms(x, gain):
    ms = jnp.mean(x * x, axis=-1, keepdims=True)
    return x * lax.rsqrt(ms + NORM_EPS) * gain


class HeadNorm:
    def __init__(self, gain, hd, rope=False, mul=1.0):
        self.gain, self.hd, self.rope, self.mul = gain, hd, rope, mul


def _head_norm(y, gain, bd, cos_ref, sin_ref, post):
    n = y.shape[1]
    hd = post.hd
    hi, lo = _split_bf16(y * y, 2)
    ss = _dot(hi, bd) + _dot(lo, bd)
    yn = y * lax.rsqrt(ss * (1.0 / hd) + NORM_EPS) * gain
    if post.rope:
        reps = n // LANES
        cos = jnp.concatenate([cos_ref[...]] * reps, axis=1) if reps > 1 else cos_ref[...]
        sin = jnp.concatenate([sin_ref[...]] * reps, axis=1) if reps > 1 else sin_ref[...]
        lane = lax.broadcasted_iota(jnp.int32, yn.shape, 1)
        first_half = (lane & (hd - 1)) < (hd // 2)
        partner = jnp.where(first_half, pltpu.roll(yn, n - hd // 2, 1), pltpu.roll(yn, hd // 2, 1))
        yn = yn * cos + partner * sin
    if post.mul != 1.0:
        yn = yn * post.mul
    return yn


def _norm_proj_body(x_ref, g_ref, *refs, posts, use_rope, transposed):
    if use_rope:
        cos_ref, sin_ref = refs[0], refs[1]
        refs = refs[2:]
    else:
        cos_ref = sin_ref = None
    n_out = len(posts)
    n_aux = 2 * sum(p is not None for p in posts)
    w_refs, aux, o_refs = refs[:n_out], refs[n_out:n_out + n_aux], refs[n_out + n_aux:]
    h = _rms(x_ref[...], g_ref[...]).astype(BF16)
    a = 0
    for k, (w_ref, o_ref, post) in enumerate(zip(w_refs, o_refs, posts)):
        if k in transposed:
            o_ref[...] = _dot_nt(w_ref[...], h).astype(o_ref.dtype)
            continue
        y = _dot(h, w_ref[...])
        if post is not None:
            y = _head_norm(y, aux[a][...], aux[a + 1][...], cos_ref, sin_ref, post)
            a += 2
        o_ref[...] = y.astype(o_ref.dtype)


def norm_proj(x2, gain, weights, out_dtypes, posts=None, seq=None, rope_tables=None, transposed=(), tm=512):
    t, d = x2.shape
    posts = posts or [None] * len(weights)
    transposed = frozenset(transposed)
    use_rope = any(p is not None and p.rope for p in posts)
    const = lambda i: (0, 0)
    args = [x2, gain.reshape(1, d).astype(F32)]
    in_specs = [pl.BlockSpec((tm, d), lambda i: (i, 0)), pl.BlockSpec((1, d), const)]
    if use_rope:
        per_seq = seq // tm
        args += list(rope_tables)
        in_specs += [pl.BlockSpec((tm, LANES), lambda i: (i % per_seq, 0))] * 2
    args += list(weights)
    in_specs += [pl.BlockSpec(w.shape, const) for w in weights]
    for w, p in zip(weights, posts):
        if p is not None:
            n = w.shape[1]
            args += [jnp.tile(p.gain.astype(F32), n // p.hd).reshape(1, n), _block_diag_ones(n, p.hd)]
            in_specs += [pl.BlockSpec((1, n), const), pl.BlockSpec((n, n), const)]
    out_specs, out_shape = [], []
    for k, (w, dt) in enumerate(zip(weights, out_dtypes)):
        if k in transposed:
            out_specs.append(pl.BlockSpec((w.shape[0], tm), lambda i: (0, i)))
            out_shape.append(jax.ShapeDtypeStruct((w.shape[0], t), dt))
        else:
            out_specs.append(pl.BlockSpec((tm, w.shape[1]), lambda i: (i, 0)))
            out_shape.append(jax.ShapeDtypeStruct((t, w.shape[1]), dt))
    return pl.pallas_call(
        functools.partial(_norm_proj_body, posts=tuple(posts), use_rope=use_rope, transposed=transposed),
        grid=(t // tm,), in_specs=in_specs, out_specs=out_specs, out_shape=out_shape,
        compiler_params=_cparams(("parallel",)), name="norm_proj",
    )(*args)


def _block_diag_ones(n, hd):
    idx = np.arange(n) // hd
    return jnp.asarray((idx[:, None] == idx[None, :]).astype(np.float32), dtype=BF16)


def _rope_tables(seq, hd):
    inv_freq = 1.0 / (ROPE_THETA ** (jnp.arange(0, hd, 2, dtype=F32) / hd))
    ang = jnp.arange(seq, dtype=F32)[:, None] * inv_freq[None, :]
    cos, sin = jnp.cos(ang), jnp.sin(ang)
    reps = LANES // hd
    cos_t = jnp.tile(jnp.concatenate([cos, cos], axis=1), (1, reps))
    sin_t = jnp.tile(jnp.concatenate([-sin, sin], axis=1), (1, reps))
    return cos_t, sin_t


def _flash_body(lam_ref, *refs, n_qk, diff, out_scale, sub):
    q_refs = refs[:n_qk]
    k_refs = refs[n_qk:2 * n_qk]
    vt_ref, gain_ref, o_ref, m_ref, l_ref, acc_ref, s0_ref, s1_ref, p0_ref, p1_ref = refs[2 * n_qk:]
    i = pl.program_id(2)
    n_sm = 2 if diff else 1
    _, tk, tq = s0_ref.shape
    s_slots = (s0_ref, s1_ref)
    p_slots = (p0_ref, p1_ref)

    m_ref[...] = jnp.full(m_ref.shape, NEG_INF, F32)
    l_ref[...] = jnp.zeros(l_ref.shape, F32)
    acc_ref[...] = jnp.zeros(acc_ref.shape, F32)

    qs = [r[0] for r in q_refs]
    q = qs[0] if n_qk == 1 else jnp.concatenate(qs, axis=1)
    if diff:
        lane = lax.broadcasted_iota(jnp.int32, q.shape, 1)
        half = q.shape[1] // 2
        zero = jnp.zeros_like(q)
        q_parts = [jnp.where(lane < half, q, zero), jnp.where(lane >= half, q, zero)]
    else:
        q_parts = [q]

    def scores(c, slot, key_offset=None):
        rows = pl.ds(pl.multiple_of(c * tk, tk), tk)
        ks = [r[0, rows, :] for r in k_refs]
        k = ks[0] if n_qk == 1 else jnp.concatenate(ks, axis=1)
        for sm in range(n_sm):
            s = _dot_nt(k, q_parts[sm])
            if key_offset is not None:
                row = lax.broadcasted_iota(jnp.int32, s.shape, 0)
                col = lax.broadcasted_iota(jnp.int32, s.shape, 1)
                s = jnp.where(row + key_offset <= col, s, NEG_INF)
            s_slots[slot][sm] = s

    def update(c, slot):
        vt = vt_ref[:, pl.ds(pl.multiple_of(c * tk, tk), tk)]
        for sm in range(n_sm):
            s_ref, p_ref = s_slots[slot], p_slots[slot]
            m_prev = m_ref[sm]
            m_new = jnp.maximum(m_prev, jnp.max(s_ref[sm], axis=0, keepdims=True))
            m_ref[sm] = m_new
            alpha = jnp.exp2(m_prev - m_new)
            lpart = jnp.zeros((SUBLANES, tq), F32)
            for r in range(tk // sub):
                p = jnp.exp2(s_ref[sm, r * sub:(r + 1) * sub, :] - m_new)
                lpart = lpart + _fold_rows(p)
                p_ref[sm, r * sub:(r + 1) * sub, :] = p.astype(BF16)
            l_ref[sm] = alpha * l_ref[sm] + jnp.sum(lpart, axis=0, keepdims=True)
            acc_ref[sm] = alpha * acc_ref[sm] + _dot(vt, p_ref[sm])

    if tq == tk:
        n_chunks = i + 1
        last_past = jnp.maximum(i - 1, 0)

        def chunk_at(j):
            return jnp.where(j == 0, i, j - 1)

        scores(i, 0, key_offset=0)

        def pair(k2, carry):
            scores(jnp.minimum(2 * k2, last_past), 1)
            update(chunk_at(2 * k2), 0)
            scores(jnp.minimum(2 * k2 + 1, last_past), 0)
            update(2 * k2, 1)
            return carry

        lax.fori_loop(0, n_chunks // 2, pair, 0)

        @pl.when(n_chunks % 2 == 1)
        def _():
            update(chunk_at(n_chunks - 1), 0)
    else:
        n_past = 2 * i
        last_past = jnp.maximum(n_past - 1, 0)
        scores(n_past, 0, key_offset=0)
        scores(n_past + 1, 1, key_offset=tk)
        update(n_past, 0)
        scores(0, 0)
        update(n_past + 1, 1)

        def pair(k2, carry):
            scores(2 * k2 + 1, 1)
            update(2 * k2, 0)
            scores(jnp.minimum(2 * k2 + 2, last_past), 0)
            update(2 * k2 + 1, 1)
            return carry

        lax.fori_loop(0, i, pair, 0)

    o = acc_ref[0] / l_ref[0]
    if diff:
        o = o - lam_ref[0] * (acc_ref[1] / l_ref[1])
        ms = jnp.mean(o * o, axis=0, keepdims=True)
        o = o * lax.rsqrt(ms + NORM_EPS) * gain_ref[...] * out_scale
    o_ref[0] = o.T.astype(o_ref.dtype)


def flash_attention(lam, qs, ks, vt, gain, n_heads, dv, *, diff, out_scale=1.0, tk=512):
    b, s, _ = qs[0].shape
    tile = tk if diff else 2 * tk
    sub = 64 if diff else 32
    nt = s // tile
    n_qk = len(qs)
    in_specs = [pl.BlockSpec(memory_space=pltpu.SMEM)]
    for q in qs:
        w = q.shape[2] // n_heads
        in_specs.append(pl.BlockSpec((1, tile, w), lambda bb, h, i: (bb, i, h)))
    for q, k in zip(qs, ks):
        w = q.shape[2] // n_heads
        if k.shape[2] == w:
            in_specs.append(pl.BlockSpec((1, s, w), lambda bb, h, i: (bb, 0, 0)))
        else:
            in_specs.append(pl.BlockSpec((1, s, w), lambda bb, h, i: (bb, 0, h)))
    in_specs.append(pl.BlockSpec((dv, s), lambda bb, h, i: (h, bb)))
    in_specs.append(pl.BlockSpec((dv, 1), lambda bb, h, i: (0, 0)))
    n_sm = 2 if diff else 1
    return pl.pallas_call(
        functools.partial(_flash_body, n_qk=n_qk, diff=diff, out_scale=out_scale, sub=sub),
        grid=(b, n_heads, nt), in_specs=in_specs,
        out_specs=pl.BlockSpec((1, tile, dv), lambda bb, h, i: (bb, i, h)),
        out_shape=jax.ShapeDtypeStruct((b, s, n_heads * dv), BF16),
        scratch_shapes=[pltpu.VMEM((n_sm, 1, tile), F32), pltpu.VMEM((n_sm, 1, tile), F32),
                        pltpu.VMEM((n_sm, dv, tile), F32),
                        pltpu.VMEM((n_sm, tk, tile), F32), pltpu.VMEM((n_sm, tk, tile), F32),
                        pltpu.VMEM((n_sm, tk, tile), BF16), pltpu.VMEM((n_sm, tk, tile), BF16)],
        compiler_params=_cparams(("parallel", "parallel", "arbitrary")),
        name="flash_diff" if diff else "flash_plain",
    )(lam, *qs, *ks, vt, gain.reshape(dv, 1).astype(F32))


def _ssd_body(xbc_ref, z_ref, dt_ref, dtt_ref, cw_ref, cb_ref, dtb_ref, dtbt_ref, al_ref, alt_ref,
              dsk_ref, ng_ref, o_ref, xpad_ref, state_ref):
    chunk = xbc_ref.shape[1]
    d_in = z_ref.shape[2]
    gn = SSM_GROUPS * SSM_STATE
    c = pl.program_id(1)

    @pl.when(c == 0)
    def _():
        xpad_ref[0:8, :] = jnp.zeros((8, xpad_ref.shape[1]), F32)
        state_ref[...] = jnp.zeros(state_ref.shape, F32)

    xpad_ref[8:8 + chunk, :] = xbc_ref[0]
    conv = cb_ref[...]
    for w in range(SSM_CONV):
        conv = conv + cw_ref[w:w + 1, :] * xpad_ref[pl.ds(8 - (SSM_CONV - 1) + w, chunk), :]
    xpad_ref[0:8, :] = xpad_ref[chunk:chunk + 8, :]
    u = _silu(conv)
    xs = u[:, :d_in]
    bmat = u[:, d_in:d_in + gn]
    cmat = u[:, d_in + gn:]

    dt = _softplus(dt_ref[0] + dtb_ref[...])
    ad = dt * (-jnp.exp(al_ref[...]))
    dtt = _softplus(dtt_ref[0] + dtbt_ref[...])
    adt = dtt * (-jnp.exp(alt_ref[...]))
    row = lax.broadcasted_iota(jnp.int32, (chunk, chunk), 0)
    col = lax.broadcasted_iota(jnp.int32, (chunk, chunk), 1)
    lower = row >= col
    tril = jnp.where(lower, 1.0, 0.0).astype(BF16)
    triu = jnp.where(row <= col, 1.0, 0.0).astype(BF16)
    cs = sum(_dot(tril, part) for part in _split_bf16(ad, 3))
    cst = sum(_dot(part, triu) for part in _split_bf16(adt, 3))

    heads_per_group = SSM_HEADS // SSM_GROUPS
    dsk = dsk_ref[...]
    ys = []
    for g in range(SSM_GROUPS):
        bg = bmat[:, g * SSM_STATE:(g + 1) * SSM_STATE]
        cg = cmat[:, g * SSM_STATE:(g + 1) * SSM_STATE].astype(BF16)
        cb = _dot_nt(cg, bg.astype(BF16))
        bgt = bg.T.astype(BF16)
        for r in range(heads_per_group):
            h = g * heads_per_group + r
            ccol = cs[:, h:h + 1]
            crow = cst[h:h + 1, :]
            decay = jnp.exp(jnp.where(lower, ccol - crow, NEG_INF))
            x_h = xs[:, h * SSM_HEAD_DIM:(h + 1) * SSM_HEAD_DIM]
            xdt = x_h * dt[:, h:h + 1]
            y = _dot((cb * decay).astype(BF16), xdt.astype(BF16))
            st = state_ref[h]
            y = y + _dot(cg, st.astype(BF16)) * jnp.exp(ccol)
            last = cst[h:h + 1, chunk - 1:chunk]
            to_end = jnp.exp(last - ccol)
            state_ref[h] = st * jnp.exp(last) + _dot(bgt, (xdt * to_end).astype(BF16))
            ys.append(y + x_h * dsk[:, h * SSM_HEAD_DIM:(h + 1) * SSM_HEAD_DIM])

    y = jnp.concatenate(ys, axis=1) * _silu(z_ref[0])
    gw = d_in // SSM_GROUPS
    for g in range(SSM_GROUPS):
        seg = y[:, g * gw:(g + 1) * gw]
        o_ref[0, :, g * gw:(g + 1) * gw] = _rms(seg, ng_ref[:, g * gw:(g + 1) * gw]).astype(o_ref.dtype)


def ssd_mixer(xbc, z, dt_raw, conv_w, conv_b, dt_bias, a_log, d_skip, norm_gain):
    b, s, cch = xbc.shape
    d_in = z.shape[2]
    nc = s // SSM_CHUNK
    hpad = dt_raw.shape[2]
    dtt = jnp.transpose(dt_raw[:, :, :SSM_HEADS], (0, 2, 1))

    def lane_pad(v):
        return jnp.pad(v.astype(F32), (0, hpad - SSM_HEADS)).reshape(1, hpad)

    args = (xbc, z, dt_raw, dtt, conv_w.astype(F32), conv_b.reshape(1, cch).astype(F32),
            lane_pad(dt_bias), dt_bias.reshape(SSM_HEADS, 1).astype(F32),
            lane_pad(a_log), a_log.reshape(SSM_HEADS, 1).astype(F32),
            jnp.repeat(d_skip.astype(F32), SSM_HEAD_DIM).reshape(1, d_in),
            norm_gain.reshape(1, d_in).astype(F32))
    const = lambda bb, c: (0, 0)
    in_specs = [pl.BlockSpec((1, SSM_CHUNK, cch), lambda bb, c: (bb, c, 0)),
                pl.BlockSpec((1, SSM_CHUNK, d_in), lambda bb, c: (bb, c, 0)),
                pl.BlockSpec((1, SSM_CHUNK, hpad), lambda bb, c: (bb, c, 0)),
                pl.BlockSpec((1, SSM_HEADS, SSM_CHUNK), lambda bb, c: (bb, 0, c)),
                pl.BlockSpec((SSM_CONV, cch), const), pl.BlockSpec((1, cch), const),
                pl.BlockSpec((1, hpad), const), pl.BlockSpec((SSM_HEADS, 1), const),
                pl.BlockSpec((1, hpad), const), pl.BlockSpec((SSM_HEADS, 1), const),
                pl.BlockSpec((1, d_in), const), pl.BlockSpec((1, d_in), const)]
    return pl.pallas_call(
        _ssd_body, grid=(b, nc), in_specs=in_specs,
        out_specs=pl.BlockSpec((1, SSM_CHUNK, d_in), lambda bb, c: (bb, c, 0)),
        out_shape=jax.ShapeDtypeStruct((b, s, d_in), BF16),
        scratch_shapes=[pltpu.VMEM((SSM_CHUNK + 8, cch), F32),
                        pltpu.VMEM((SSM_HEADS, SSM_STATE, SSM_HEAD_DIM), F32)],
        compiler_params=_cparams(("parallel", "arbitrary")), name="ssd_mixer",
    )(*args)


def _out_proj_body(x_ref, a_ref, b_ref, wa_ref, wb_ref, o_ref):
    o_ref[...] = x_ref[...] + _dot(a_ref[...], wa_ref[...]) + _dot(b_ref[...], wb_ref[...])


def out_proj_residual(x2, a, bm, wa, wb, tm=512):
    t, d = x2.shape
    return pl.pallas_call(
        _out_proj_body, grid=(t // tm,),
        in_specs=[pl.BlockSpec((tm, d), lambda i: (i, 0)),
                  pl.BlockSpec((tm, a.shape[1]), lambda i: (i, 0)),
                  pl.BlockSpec((tm, bm.shape[1]), lambda i: (i, 0)),
                  pl.BlockSpec(wa.shape, lambda i: (0, 0)),
                  pl.BlockSpec(wb.shape, lambda i: (0, 0))],
        out_specs=pl.BlockSpec((tm, d), lambda i: (i, 0)),
        out_shape=jax.ShapeDtypeStruct((t, d), F32),
        compiler_params=_cparams(("parallel",)), name="out_proj",
    )(x2, a, bm, wa, wb)


def _ffn_body(x_ref, g_ref, wg_ref, wu_ref, wd_ref, o_ref, h_ref):
    f = pl.program_id(1)

    @pl.when(f == 0)
    def _():
        x = x_ref[...]
        h_ref[...] = _rms(x, g_ref[...]).astype(BF16)
        o_ref[...] = x

    half = h_ref.shape[0] // 2
    for r in (slice(0, half), slice(half, 2 * half)):
        h = h_ref[r, :]
        act = (_silu(_dot(h, wg_ref[...])) * _dot(h, wu_ref[...])).astype(BF16)
        o_ref[r, :] += _dot(act, wd_ref[...])


def ffn_residual(x2, gain, w_gate, w_up, w_down, tm=1024, tf=1408):
    t, d = x2.shape
    d_ff = w_gate.shape[1]
    return pl.pallas_call(
        _ffn_body, grid=(t // tm, d_ff // tf),
        in_specs=[pl.BlockSpec((tm, d), lambda i, f: (i, 0)),
                  pl.BlockSpec((1, d), lambda i, f: (0, 0)),
                  pl.BlockSpec((d, tf), lambda i, f: (0, f)),
                  pl.BlockSpec((d, tf), lambda i, f: (0, f)),
                  pl.BlockSpec((tf, d), lambda i, f: (f, 0))],
        out_specs=pl.BlockSpec((tm, d), lambda i, f: (i, 0)),
        out_shape=jax.ShapeDtypeStruct((t, d), F32),
        scratch_shapes=[pltpu.VMEM((tm, d), BF16)],
        compiler_params=_cparams(("parallel", "arbitrary")), name="ffn",
    )(x2, gain.reshape(1, d).astype(F32), w_gate, w_up, w_down)


MOE_ROWS = 256
ROUTE_IDX = 0
ROUTE_W = 2
ROUTE_RANK = 4


def _moe_ffn_body(block_expert_ref, n_used_ref, xs_ref, wg_ref, wu_ref, wd_ref, o_ref):
    i = pl.program_id(0)

    @pl.when(i < n_used_ref[0])
    def _():
        x = xs_ref[...]
        act = (_silu(_dot(x, wg_ref[0])) * _dot(x, wu_ref[0])).astype(BF16)
        o_ref[...] = _dot(act, wd_ref[0]).astype(o_ref.dtype)

    @pl.when(i >= n_used_ref[0])
    def _():
        o_ref[...] = jnp.zeros(o_ref.shape, o_ref.dtype)


def moe_expert_ffn(xs, block_expert, n_used, w_gate, w_up, w_down):
    p, d = xs.shape
    d_ff = w_gate.shape[2]
    rows = MOE_ROWS
    grid_spec = pltpu.PrefetchScalarGridSpec(
        num_scalar_prefetch=2, grid=(p // rows,),
        in_specs=[pl.BlockSpec((rows, d), lambda i, be, nu: (i, 0)),
                  pl.BlockSpec((1, d, d_ff), lambda i, be, nu: (be[i], 0, 0)),
                  pl.BlockSpec((1, d, d_ff), lambda i, be, nu: (be[i], 0, 0)),
                  pl.BlockSpec((1, d_ff, d), lambda i, be, nu: (be[i], 0, 0))],
        out_specs=pl.BlockSpec((rows, d), lambda i, be, nu: (i, 0)))
    return pl.pallas_call(
        _moe_ffn_body, grid_spec=grid_spec, out_shape=jax.ShapeDtypeStruct((p, d), BF16),
        compiler_params=pltpu.CompilerParams(
            dimension_semantics=("arbitrary",), vmem_limit_bytes=MOE_VMEM_LIMIT),
        name="moe_expert_ffn",
    )(block_expert, n_used, xs, w_gate, w_up, w_down)


def _moe_combine_body(x_ref, y0_ref, y1_ref, route_ref, o_ref):
    route = route_ref[...]
    lane = lax.broadcasted_iota(jnp.int32, route.shape, 1)
    w0 = jnp.sum(jnp.where(lane == ROUTE_W, route, 0.0), axis=1, keepdims=True)
    w1 = jnp.sum(jnp.where(lane == ROUTE_W + 1, route, 0.0), axis=1, keepdims=True)
    o_ref[...] = x_ref[...] + w0 * y0_ref[...].astype(F32) + w1 * y1_ref[...].astype(F32)


def moe_combine(x2, y0, y1, route, tm=512):
    t, d = x2.shape
    row = lambda i: (i, 0)
    return pl.pallas_call(
        _moe_combine_body, grid=(t // tm,),
        in_specs=[pl.BlockSpec((tm, d), row), pl.BlockSpec((tm, d), row), pl.BlockSpec((tm, d), row),
                  pl.BlockSpec((tm, LANES), row)],
        out_specs=pl.BlockSpec((tm, d), row),
        out_shape=jax.ShapeDtypeStruct((t, d), F32),
        compiler_params=_cparams(("parallel",)), name="moe_combine",
    )(x2, y0, y1, route)


def moe_residual(x2, h, route, counts, w_gate, w_up, w_down):
    t, d = x2.shape
    n_e = w_gate.shape[0]
    rows = MOE_ROWS
    expert = route[:, ROUTE_IDX:ROUTE_IDX + 2].astype(jnp.int32).reshape(-1)
    rank = route[:, ROUTE_RANK:ROUTE_RANK + 2].astype(jnp.int32).reshape(-1)
    padded = (counts[0, :n_e].astype(jnp.int32) + rows - 1) // rows * rows
    ends = jnp.cumsum(padded)
    own = expert[:, None] == jnp.arange(n_e, dtype=jnp.int32)[None, :]
    slot = jnp.sum(jnp.where(own, (ends - padded)[None, :], 0), axis=1) + rank
    p_rows = 2 * t + n_e * rows
    token_of_slot = jnp.zeros((p_rows,), jnp.int32).at[slot].set(
        jnp.arange(2 * t, dtype=jnp.int32) // 2, unique_indices=True)
    block_start = jnp.arange(p_rows // rows, dtype=jnp.int32) * rows
    block_expert = jnp.minimum(jnp.searchsorted(ends, block_start, side="right"), n_e - 1).astype(jnp.int32)
    n_used = (ends[-1] // rows).astype(jnp.int32).reshape(1)

    take_rows = lambda a, idx: a.at[idx].get(mode="promise_in_bounds")
    xs = take_rows(h, token_of_slot)
    ys = moe_expert_ffn(xs, block_expert, n_used, w_gate, w_up, w_down)
    slot2 = slot.reshape(t, 2)
    return moe_combine(x2, take_rows(ys, slot2[:, 0]), take_rows(ys, slot2[:, 1]), route)


def _router_body(x_ref, g_ref, r_ref, h_ref, o_ref, count_ref, run_ref):
    @pl.when(pl.program_id(0) == 0)
    def _():
        run_ref[...] = jnp.zeros(run_ref.shape, F32)

    h = _rms(x_ref[...], g_ref[...])
    h_ref[...] = h.astype(h_ref.dtype)
    logits = jnp.dot(h, r_ref[...], precision=lax.Precision.HIGHEST, preferred_element_type=F32)
    lane = lax.broadcasted_iota(jnp.int32, logits.shape, 1).astype(F32)
    low = jnp.float32(-3.0e38)
    logits = jnp.where(lane < N_EXPERTS, logits, low)
    m1 = jnp.max(logits, axis=1, keepdims=True)
    i1 = jnp.min(jnp.where(logits == m1, lane, float(LANES)), axis=1, keepdims=True)
    rest = jnp.where(lane == i1, low, logits)
    m2 = jnp.max(rest, axis=1, keepdims=True)
    i2 = jnp.min(jnp.where(rest == m2, lane, float(LANES)), axis=1, keepdims=True)
    ex = jnp.exp(m2 - m1)
    w1 = 1.0 / (1.0 + ex)
    w2 = ex / (1.0 + ex)
    tm = logits.shape[0]
    routed = jnp.where((lane == i1) | (lane == i2), 1.0, 0.0)
    row = lax.broadcasted_iota(jnp.int32, (tm, tm), 0)
    col = lax.broadcasted_iota(jnp.int32, (tm, tm), 1)
    before = jnp.where(col < row, 1.0, 0.0).astype(BF16)
    rank = run_ref[0:1, :] + _dot(before, routed.astype(BF16))
    r1 = jnp.sum(jnp.where(lane == i1, rank, 0.0), axis=1, keepdims=True)
    r2 = jnp.sum(jnp.where(lane == i2, rank, 0.0), axis=1, keepdims=True)
    fields = ((ROUTE_IDX, i1), (ROUTE_IDX + 1, i2), (ROUTE_W, w1), (ROUTE_W + 1, w2),
              (ROUTE_RANK, r1), (ROUTE_RANK + 1, r2))
    out = jnp.zeros(logits.shape, F32)
    for pos, val in fields:
        out = jnp.where(lane == pos, val, out)
    o_ref[...] = out
    run_ref[...] = run_ref[...] + jnp.sum(routed, axis=0, keepdims=True)
    count_ref[...] = run_ref[...]


def router(x2, gain, router_w, tm=512):
    t, d = x2.shape
    r_pad = jnp.pad(router_w.astype(F32), ((0, 0), (0, LANES - router_w.shape[1])))
    return pl.pallas_call(
        _router_body, grid=(t // tm,),
        in_specs=[pl.BlockSpec((tm, d), lambda i: (i, 0)),
                  pl.BlockSpec((1, d), lambda i: (0, 0)),
                  pl.BlockSpec((d, LANES), lambda i: (0, 0))],
        out_specs=[pl.BlockSpec((tm, d), lambda i: (i, 0)), pl.BlockSpec((tm, LANES), lambda i: (i, 0)),
                   pl.BlockSpec((SUBLANES, LANES), lambda i: (0, 0))],
        out_shape=[jax.ShapeDtypeStruct((t, d), BF16), jax.ShapeDtypeStruct((t, LANES), F32),
                   jax.ShapeDtypeStruct((SUBLANES, LANES), F32)],
        scratch_shapes=[pltpu.VMEM((SUBLANES, LANES), F32)],
        compiler_params=_cparams(("arbitrary",)), name="router",
    )(x2, gain.reshape(1, d).astype(F32), r_pad)


def _compress_body(ch_ref, nx_ref, pos_ref, w1_ref, w2_ref, gain_ref, cos_ref, sin_ref, rot_ref, o_ref,
                   *, is_key):
    a = _dot((ch_ref[0] + pos_ref[0]).astype(BF16), w1_ref[0, 0])
    a = a + _dot((nx_ref[0] + pos_ref[1]).astype(BF16), w1_ref[0, 1])
    out = _dot(_silu(a).astype(BF16), w2_ref[...])
    if is_key:
        out = _rms(out, gain_ref[...])
        hi, lo = _split_bf16(out, 2)
        partner = _dot(hi, rot_ref[...]) + _dot(lo, rot_ref[...])
        out = out * cos_ref[...] + partner * sin_ref[...]
    o_ref[0, 0] = out


def nsa_compress(t, pos, w1, w2, gain, seq, is_key):
    b, s, _ = t.shape
    g, d = NSA_KV_GROUPS, NSA_HEAD_DIM
    n_ch = s // NSA_CMP_STRIDE
    half = NSA_CMP_STRIDE * g * d
    ch = t.reshape(b, n_ch, half)
    nxt = jnp.concatenate([ch[:, 1:], jnp.zeros((b, 1, half), F32)], axis=1)
    pos2 = jnp.broadcast_to(pos.astype(F32).reshape(2, NSA_CMP_STRIDE, 1, d),
                            (2, NSA_CMP_STRIDE, g, d)).reshape(2, 1, half)
    w1r = w1.astype(BF16).reshape(2, NSA_CMP_STRIDE, 1, d, d)
    own = (jnp.arange(g)[:, None] == jnp.arange(g)[None, :]).reshape(g, 1, 1, g, 1, 1)
    w1s = jnp.where(own, w1r[None], jnp.zeros((), BF16)).reshape(g, 2, half, d)
    cmp_end = jnp.arange(n_ch) * NSA_CMP_STRIDE + NSA_CMP_BLOCK - 1
    inv_freq = 1.0 / (ROPE_THETA ** (jnp.arange(0, d, 2, dtype=F32) / d))
    ang = cmp_end.astype(F32)[:, None] * inv_freq[None, :]
    cos = jnp.concatenate([jnp.cos(ang)] * 2, axis=1)
    sin = jnp.concatenate([jnp.sin(ang)] * 2, axis=1)
    rot = np.zeros((d, d), np.float32)
    rot[np.arange(d // 2) + d // 2, np.arange(d // 2)] = -1.0
    rot[np.arange(d // 2), np.arange(d // 2) + d // 2] = 1.0
    blk = lambda bb, gg: (bb, gg, 0, 0)
    seq = lambda bb, gg: (bb, 0, 0)
    c2 = lambda bb, gg: (0, 0)
    c3 = lambda bb, gg: (0, 0, 0)
    return pl.pallas_call(
        functools.partial(_compress_body, is_key=is_key), grid=(b, g),
        in_specs=[pl.BlockSpec((1, n_ch, half), seq), pl.BlockSpec((1, n_ch, half), seq),
                  pl.BlockSpec((2, 1, half), c3), pl.BlockSpec((1, 2, half, d), lambda bb, gg: (gg, 0, 0, 0)),
                  pl.BlockSpec((d, d), c2), pl.BlockSpec((1, d), c2),
                  pl.BlockSpec((n_ch, d), c2), pl.BlockSpec((n_ch, d), c2), pl.BlockSpec((d, d), c2)],
        out_specs=pl.BlockSpec((1, 1, n_ch, d), blk),
        out_shape=jax.ShapeDtypeStruct((b, g, n_ch, d), F32),
        compiler_params=_cparams(("parallel", "parallel")), name="nsa_compress",
    )(ch, nxt, pos2, w1s, w2.astype(BF16), gain.reshape(1, d).astype(F32), cos, sin,
      jnp.asarray(rot, dtype=BF16))


def _nsa_body(q_ref, ck_ref, cvt_ref, ksl_ref, vslt_ref, kwn_ref, vwnt_ref, ovt_ref, glt_ref, o_ref,
              sc_ref, phi_ref, plo_ref, imp_ref, bias_ref, ss_ref, ps_ref, ss1_ref, ps1_ref, sw_ref, pw_ref,
              ow_ref, oc_ref, *, tq):
    g = pl.program_id(1)
    i = pl.program_id(2)
    d = NSA_HEAD_DIM
    rep = NSA_HEADS // NSA_KV_GROUPS
    t0 = i * tq
    n_cmp = ck_ref.shape[1]
    n_sel = ovt_ref.shape[0]
    width = rep * tq
    sub = NSA_SEL_BLOCK
    dead = 0.5 * NEG_INF
    v_rows = pl.ds(pl.multiple_of(g * d, d), d)

    qb = q_ref[0]
    q4 = jnp.concatenate([qb[:, r * d:(r + 1) * d] for r in range(rep)], axis=0)
    q4 = jnp.concatenate([q4, q4], axis=1)
    lane = lax.broadcasted_iota(jnp.int32, q4.shape, 1)
    q4 = jnp.where(jnp.right_shift(lane, d.bit_length() - 1) == g, q4, jnp.zeros_like(q4))

    def qpos_of(shape):
        return t0 + (lax.broadcasted_iota(jnp.int32, shape, 1) & (tq - 1))

    def compressed(rows):
        s = _dot_nt(ck_ref[0, 0:rows, :], q4)
        cmp_end = lax.broadcasted_iota(jnp.int32, s.shape, 0) * NSA_CMP_STRIDE + (NSA_CMP_BLOCK - 1)
        s = jnp.where(cmp_end <= qpos_of(s.shape), s, NEG_INF)
        sc_ref[0:rows, :] = s
        m_c = jnp.max(s, axis=0, keepdims=True)
        lpart = jnp.zeros((SUBLANES, width), F32)
        for r in range(rows // sub):
            e = jnp.exp2(sc_ref[r * sub:(r + 1) * sub, :] - m_c)
            lpart = lpart + _fold_rows(e)
            hi = e.astype(BF16)
            phi_ref[r * sub:(r + 1) * sub, :] = hi
            plo_ref[r * sub:(r + 1) * sub, :] = (e - hi.astype(F32)).astype(BF16)
        inv_c = jnp.where(m_c > dead, 1.0 / jnp.sum(lpart, axis=0, keepdims=True), 0.0)
        oc_ref[...] = _dot(cvt_ref[0, v_rows, 0:rows], phi_ref[0:rows, :]) * inv_c
        ovt = ovt_ref[:, 0:rows]
        imp4 = (_dot(ovt, phi_ref[0:rows, :]) + _dot(ovt, plo_ref[0:rows, :])) * inv_c
        imp = imp4[:, 0:tq]
        for r in range(1, rep):
            imp = imp + imp4[:, r * tq:(r + 1) * tq]
        imp_ref[...] = imp

    cmp_live = jnp.clip((t0 + tq - NSA_CMP_BLOCK) // NSA_CMP_STRIDE + 1, 1, n_cmp)
    cmp_step = min(2 * LANES, n_cmp)
    for v in range(n_cmp // cmp_step):
        @pl.when((cmp_live > cmp_step * v) & (cmp_live <= cmp_step * (v + 1)))
        def _():
            compressed(cmp_step * (v + 1))

    o_c = oc_ref[...]
    imp = imp_ref[...]

    blk = lax.broadcasted_iota(jnp.int32, imp.shape, 0)
    qp = t0 + lax.broadcasted_iota(jnp.int32, imp.shape, 1)
    cur = jnp.right_shift(qp, NSA_SEL_BLOCK.bit_length() - 1)
    forced = (blk == 0) | (blk == cur) | (blk == cur - 1)
    future = blk * NSA_SEL_BLOCK > qp
    imp_ref[...] = jnp.where(future, -FORCE_SCORE, jnp.where(forced, FORCE_SCORE, imp))
    bias_ref[...] = jnp.full(bias_ref.shape, NEG_INF, F32)

    n_live = jnp.minimum((t0 + tq - 1) // NSA_SEL_BLOCK + 1, n_sel)
    n_var = max(n_sel // 32, 1)
    rows_per = n_sel // n_var
    top_n = float(min(NSA_TOP_N, n_sel))
    for v in range(n_var):
        rows = rows_per * (v + 1)

        @pl.when((n_live > rows_per * v) & (n_live <= rows))
        def _():
            groups = rows // SUBLANES
            mine = [imp_ref[gi * SUBLANES:(gi + 1) * SUBLANES, :] for gi in range(groups)]
            rank = [jnp.zeros((SUBLANES, tq), F32) for _ in range(groups)]
            in_group = lax.broadcasted_iota(jnp.int32, (SUBLANES, tq), 0)
            for i2 in range(rows):
                other = imp_ref[i2:i2 + 1, :]
                for gi in range(groups):
                    if gi > i2 // SUBLANES:
                        beats = other >= mine[gi]
                    elif gi < i2 // SUBLANES:
                        beats = other > mine[gi]
                    else:
                        beats = (other > mine[gi]) | ((other == mine[gi]) & (in_group > i2 % SUBLANES))
                    rank[gi] = rank[gi] + jnp.where(beats, 1.0, 0.0)
            for gi in range(groups):
                bias = jnp.where(rank[gi] < top_n, 0.0, NEG_INF)
                bias_ref[gi * SUBLANES:(gi + 1) * SUBLANES, :] = jnp.concatenate([bias] * rep, axis=1)

    init = (jnp.full((1, width), NEG_INF, F32), jnp.zeros((d + ONES_ROWS, width), F32))

    def normalised(acc):
        return acc[:d, :] / acc[d:d + 1, :]

    chunk = 8 * sub
    n_sub = chunk // sub

    s_slots = (ss_ref, ss1_ref)
    p_slots = (ps_ref, ps1_ref)

    def sel_scores(c, slot, diagonal=False):
        start = pl.multiple_of(c * chunk, chunk)
        s = _dot_nt(ksl_ref[0, pl.ds(start, chunk), :], q4)
        if diagonal:
            kpos = start + lax.broadcasted_iota(jnp.int32, s.shape, 0)
            s = jnp.where(kpos <= qpos_of(s.shape), s, NEG_INF)
        s_slots[slot][...] = s

    def sel_update(c, slot, carry):
        m_prev, acc = carry
        s_ref, p_ref = s_slots[slot], p_slots[slot]
        biases = [bias_ref[pl.ds(c * n_sub + r, 1), :] for r in range(n_sub)]
        m8 = jnp.full((SUBLANES, width), NEG_INF, F32)
        for r in range(n_sub):
            block = s_ref[r * sub:(r + 1) * sub, :]
            m8 = jnp.maximum(m8, jnp.max(block.reshape(sub // SUBLANES, SUBLANES, width), axis=0) + biases[r])
        m_new = jnp.maximum(m_prev, jnp.max(m8, axis=0, keepdims=True))
        live = m_new > dead
        for r in range(n_sub):
            shift = jnp.where(live, biases[r] - m_new, NEG_INF)
            p_ref[r * sub:(r + 1) * sub, :] = jnp.exp2(s_ref[r * sub:(r + 1) * sub, :] + shift).astype(BF16)
        vt = _with_ones_rows(vslt_ref[v_rows, pl.ds(pl.multiple_of(c * chunk, chunk), chunk)])
        return m_new, jnp.exp2(m_prev - m_new) * acc + _dot(vt, p_ref[...])

    c_diag = (t0 + tq - 1) // chunk
    n_chunks = c_diag + 1
    last_past = jnp.maximum(c_diag - 1, 0)

    def chunk_at(j):
        return jnp.where(j == 0, c_diag, j - 1)

    sel_scores(c_diag, 0, diagonal=True)

    def pair(k, carry):
        sel_scores(jnp.minimum(2 * k, last_past), 1)
        carry = sel_update(chunk_at(2 * k), 0, carry)
        sel_scores(jnp.minimum(2 * k + 1, last_past), 0)
        return sel_update(2 * k, 1, carry)

    carry = lax.fori_loop(0, n_chunks // 2, pair, init)
    _, acc_s = lax.cond(n_chunks % 2 == 1, lambda cr: sel_update(chunk_at(n_chunks - 1), 0, cr),
                        lambda cr: cr, carry)
    o_s = normalised(acc_s)

    def win_chunk(c, carry):
        m_prev, acc = carry
        start = pl.multiple_of(c * tq, tq)
        s = _dot_nt(kwn_ref[0, pl.ds(start, tq), :], q4)
        kpos = start + lax.broadcasted_iota(jnp.int32, s.shape, 0)
        qpos = qpos_of(s.shape)
        s = jnp.where((kpos <= qpos) & (kpos > qpos - NSA_WINDOW), s, NEG_INF)
        m_new = jnp.maximum(m_prev, jnp.max(s, axis=0, keepdims=True))
        p = jnp.exp2(s + jnp.where(m_new > dead, -m_new, NEG_INF))
        vt = _with_ones_rows(vwnt_ref[v_rows, pl.ds(start, tq)])
        return m_new, jnp.exp2(m_prev - m_new) * acc + _dot(vt, p.astype(BF16))

    n_back = NSA_WINDOW // tq

    @pl.when(i < n_back)
    def _():
        _, acc_w = lax.fori_loop(0, i + 1, win_chunk, init)
        ow_ref[...] = normalised(acc_w)

    @pl.when(i >= n_back)
    def _():
        start = pl.multiple_of(t0 - NSA_WINDOW, tq)
        s = _dot_nt(kwn_ref[0, pl.ds(start, NSA_WINDOW + tq), :], q4)
        kpos = start + lax.broadcasted_iota(jnp.int32, (tq, width), 0)
        qpos = qpos_of((tq, width))
        sw_ref[0:tq, :] = jnp.where(kpos > qpos - NSA_WINDOW, s[0:tq, :], NEG_INF)
        sw_ref[tq:NSA_WINDOW, :] = s[tq:NSA_WINDOW, :]
        sw_ref[NSA_WINDOW:, :] = jnp.where(kpos + NSA_WINDOW <= qpos, s[NSA_WINDOW:, :], NEG_INF)
        m_w = jnp.max(sw_ref[...], axis=0, keepdims=True)
        for r in range((NSA_WINDOW + tq) // sub):
            pw_ref[r * sub:(r + 1) * sub, :] = jnp.exp2(sw_ref[r * sub:(r + 1) * sub, :] - m_w).astype(BF16)
        vt = _with_ones_rows(vwnt_ref[v_rows, pl.ds(start, NSA_WINDOW + tq)])
        ow_ref[...] = normalised(_dot(vt, pw_ref[...]))

    o_w = ow_ref[...]

    def gate(branch):
        rows = [glt_ref[pl.ds((g * rep + r) * 3 + branch, 1), :] for r in range(rep)]
        return _sigmoid(jnp.concatenate(rows, axis=1))

    out = gate(0) * o_c + gate(1) * o_s + gate(2) * o_w
    out_t = jnp.concatenate([out, jnp.zeros_like(out)], axis=0).T
    o_ref[0] = jnp.concatenate([out_t[r * tq:(r + 1) * tq, :d] for r in range(rep)],
                               axis=1).astype(o_ref.dtype)


def nsa_overlap_t(n_cmp, n_sel):
    c_start = np.arange(n_cmp)[None, :] * NSA_CMP_STRIDE
    s_start = np.arange(n_sel)[:, None] * NSA_SEL_BLOCK
    hit = (c_start < s_start + NSA_SEL_BLOCK) & (c_start + NSA_CMP_BLOCK > s_start)
    hit = hit & (np.arange(n_cmp)[None, :] < n_cmp - NSA_CMP_BLOCK // NSA_CMP_STRIDE + 1)
    return jnp.asarray(hit.astype(np.float32), dtype=BF16)


def nsa_attention(qn, ck, cvt, ksl, vslt, kwn, vwnt, glt, tq=256):
    b, s, _ = qn.shape
    g, d = NSA_KV_GROUPS, NSA_HEAD_DIM
    rep = NSA_HEADS // g
    n_cmp = ck.shape[1]
    n_sel = s // NSA_SEL_BLOCK
    nq = s // tq
    ovt = nsa_overlap_t(n_cmp, n_sel)
    width = rep * tq
    chunk = 8 * NSA_SEL_BLOCK
    full3 = lambda bb, gg, i: (bb, 0, 0)
    seq_t = lambda bb, gg, i: (0, bb)
    return pl.pallas_call(
        functools.partial(_nsa_body, tq=tq), grid=(b, g, nq),
        in_specs=[pl.BlockSpec((1, tq, rep * d), lambda bb, gg, i: (bb, i, gg)),
                  pl.BlockSpec((1, n_cmp, g * d), full3), pl.BlockSpec((1, g * d, n_cmp), full3),
                  pl.BlockSpec((1, s, g * d), full3), pl.BlockSpec((g * d, s), seq_t),
                  pl.BlockSpec((1, s, g * d), full3), pl.BlockSpec((g * d, s), seq_t),
                  pl.BlockSpec((n_sel, n_cmp), lambda bb, gg, i: (0, 0)),
                  pl.BlockSpec((glt.shape[0], tq), lambda bb, gg, i: (0, bb * nq + i))],
        out_specs=pl.BlockSpec((1, tq, rep * d), lambda bb, gg, i: (bb, i, gg)),
        out_shape=jax.ShapeDtypeStruct((b, s, g * rep * d), BF16),
        scratch_shapes=[pltpu.VMEM((n_cmp, width), F32), pltpu.VMEM((n_cmp, width), BF16),
                        pltpu.VMEM((n_cmp, width), BF16), pltpu.VMEM((n_sel, tq), F32),
                        pltpu.VMEM((n_sel, width), F32), pltpu.VMEM((chunk, width), F32),
                        pltpu.VMEM((chunk, width), BF16), pltpu.VMEM((chunk, width), F32),
                        pltpu.VMEM((chunk, width), BF16), pltpu.VMEM((NSA_WINDOW + tq, width), F32),
                        pltpu.VMEM((NSA_WINDOW + tq, width), BF16), pltpu.VMEM((d, width), F32),
                        pltpu.VMEM((d, width), F32)],
        compiler_params=_cparams(("parallel", "parallel", "arbitrary")), name="nsa_attention",
    )(qn, ck, cvt, ksl, vslt, kwn, vwnt, ovt, glt)


def _pad_cols(w, n):
    return jnp.pad(w, ((0, 0), (0, n - w.shape[1])))


def _even_layer(x2, b, s, layer_idx, norm_mix, w_in, q_gain, k_gain, lam, subln_gain, conv_w, conv_b,
                dt_bias, a_log, d_skip, ssm_norm_gain, w_out, norm_ffn, w_gate, w_up, w_down):
    nq = DA_HEADS * 2 * DA_HEAD_DIM
    nv = DA_HEADS * DA_V_DIM
    cch = SSM_D_INNER + 2 * SSM_GROUPS * SSM_STATE
    offs = np.cumsum([0, nq, nq, nv, SSM_D_INNER, cch, SSM_HEADS])
    wb = w_in.astype(BF16)
    pieces = [wb[:, offs[k]:offs[k + 1]] for k in range(6)]
    pieces[2] = pieces[2].T
    pieces[5] = _pad_cols(pieces[5], LANES)
    posts = [HeadNorm(q_gain, DA_HEAD_DIM, rope=True, mul=DA_HEAD_DIM ** -0.5 * LOG2E),
             HeadNorm(k_gain, DA_HEAD_DIM, rope=True), None, None, None, None]
    q, k, vt, z, xbc, dt = norm_proj(x2, norm_mix, pieces, [BF16, BF16, BF16, F32, F32, F32], posts, s,
                                     _rope_tables(s, DA_HEAD_DIM), transposed=(2,))
    qn = q.reshape(b, s, nq)
    kn = k.reshape(b, s, nq)
    lam_init = 0.8 - 0.6 * math.exp(-0.3 * layer_idx)
    lf = lam.astype(F32)
    lam_full = jnp.exp(jnp.sum(lf[0] * lf[1])) - jnp.exp(jnp.sum(lf[2] * lf[3])) + lam_init
    a_out = flash_attention(lam_full.reshape(1), [qn], [kn], vt, subln_gain, DA_HEADS, DA_V_DIM,
                            diff=True, out_scale=1.0 - lam_init)
    b_out = ssd_mixer(xbc.reshape(b, s, cch), z.reshape(b, s, SSM_D_INNER), dt.reshape(b, s, LANES),
                      conv_w, conv_b, dt_bias, a_log, d_skip, ssm_norm_gain)
    wo = w_out.astype(BF16)
    x2 = out_proj_residual(x2, a_out.reshape(-1, nv), b_out.reshape(-1, SSM_D_INNER), wo[:nv], wo[nv:])
    return ffn_residual(x2, norm_ffn, w_gate.astype(BF16), w_up.astype(BF16), w_down.astype(BF16))


def _odd_layer(x2, b, s, norm_mix, w_in, q_gain, k_gain, cmp_pos, cmp_w1, cmp_w2, cq_gain, ckv_gain,
               w_uq, w_ukv, qn_gain, qr_gain, kn_gain, kr_gain, w_out, norm_ffn, router_w, w_gate, w_up,
               w_down):
    g, d = NSA_KV_GROUPS, NSA_HEAD_DIM
    nq = NSA_HEADS * d
    nkv = g * d
    sizes = [nq] + [nkv] * 6 + [NSA_HEADS * 3, w_uq.shape[0], w_ukv.shape[0], MLA_ROPE_DIM]
    offs = np.cumsum([0] + sizes)
    wb = w_in.astype(BF16)
    pieces = [wb[:, offs[k]:offs[k + 1]] for k in range(len(sizes))]
    for k in (4, 6):
        pieces[k] = pieces[k].T
    pieces[7] = jnp.pad(pieces[7].T, ((0, 32 - NSA_HEADS * 3), (0, 0)))
    pieces[10] = _pad_cols(pieces[10], LANES)
    tables = _rope_tables(s, d)
    posts = [None] * len(sizes)
    posts[0] = HeadNorm(q_gain, d, rope=True, mul=d ** -0.5 * LOG2E)
    posts[3] = HeadNorm(k_gain[1], d, rope=True)
    posts[5] = HeadNorm(k_gain[2], d, rope=True)
    posts[10] = HeadNorm(kr_gain, MLA_ROPE_DIM, rope=True)
    (q, kc, vc, ksl, vslt, kwn, vwnt, glt, cq, ckv, k_rope) = norm_proj(
        x2, norm_mix, pieces, [BF16, F32, F32, BF16, BF16, BF16, BF16, F32, F32, F32, BF16], posts, s, tables,
        transposed=(4, 6, 7))

    qn = q.reshape(b, s, nq)
    ksl_n = ksl.reshape(b, s, nkv)
    kwn_n = kwn.reshape(b, s, nkv)
    ck = nsa_compress(kc.reshape(b, s, nkv), cmp_pos[0], cmp_w1[0], cmp_w2[0], k_gain[0], s, True)
    cv = nsa_compress(vc.reshape(b, s, nkv), cmp_pos[1], cmp_w1[1], cmp_w2[1], k_gain[0], s, False)
    n_cmp = ck.shape[2]
    ck = ck.transpose(0, 2, 1, 3).reshape(b, n_cmp, nkv).astype(BF16)
    cvt = cv.transpose(0, 1, 3, 2).reshape(b, nkv, n_cmp).astype(BF16)
    c_out = nsa_attention(qn, ck, cvt, ksl_n, vslt, kwn_n, vwnt, glt).reshape(b * s, nq)

    h = MLA_HEADS
    dqk = MLA_NOPE_DIM + MLA_ROPE_DIM
    wq = w_uq.astype(BF16).reshape(-1, h, dqk)
    wq_nope = wq[:, :, :MLA_NOPE_DIM].reshape(-1, h * MLA_NOPE_DIM)
    wq_rope = jnp.pad(wq[:, :, MLA_NOPE_DIM:], ((0, 0), (0, 0), (0, LANES - MLA_ROPE_DIM)))
    wq_rope = wq_rope.reshape(-1, h * LANES)
    wkv = w_ukv.astype(BF16).reshape(-1, h, MLA_NOPE_DIM + MLA_V_DIM)
    wk_nope = wkv[:, :, :MLA_NOPE_DIM].reshape(-1, h * MLA_NOPE_DIM)
    wv = wkv[:, :, MLA_NOPE_DIM:].reshape(-1, h * MLA_V_DIM)
    q_mul = dqk ** -0.5 * LOG2E
    q_nope, q_rope = norm_proj(
        cq, cq_gain, [wq_nope, wq_rope], [BF16, BF16],
        [HeadNorm(qn_gain, MLA_NOPE_DIM, mul=q_mul), HeadNorm(qr_gain, MLA_ROPE_DIM, rope=True, mul=q_mul)],
        s, tables)
    k_nope, vt = norm_proj(ckv, ckv_gain, [wk_nope, wv.T], [BF16, BF16],
                           [HeadNorm(kn_gain, MLA_NOPE_DIM), None], transposed=(1,))
    shp = lambda t: t.reshape(b, s, t.shape[-1])
    d_out = flash_attention(jnp.zeros((1,), F32), [shp(q_nope), shp(q_rope)], [shp(k_nope), shp(k_rope)],
                            vt, jnp.ones((MLA_V_DIM,), F32), h, MLA_V_DIM, diff=False)

    wo = w_out.astype(BF16)
    x2 = out_proj_residual(x2, c_out, d_out.reshape(b * s, h * MLA_V_DIM), wo[:nq], wo[nq:])
    h, route, counts = router(x2, norm_ffn, router_w)
    return moe_residual(x2, h, route, counts, w_gate.astype(BF16), w_up.astype(BF16), w_down.astype(BF16))


def kernel(x, ev_norm_mix, ev_w_in, da_q_gain, da_k_gain, da_lambda, da_subln_gain, ssm_conv_w, ssm_conv_b, ssm_dt_bias, ssm_a_log, ssm_d, ssm_norm_gain, ev_w_out, ev_norm_ffn, ffn_w_gate, ffn_w_up, ffn_w_down, od_norm_mix, od_w_in, nsa_q_gain, nsa_k_gain, nsa_cmp_pos, nsa_cmp_w1, nsa_cmp_w2, mla_cq_gain, mla_ckv_gain, mla_w_uq, mla_w_ukv, mla_qn_gain, mla_qr_gain, mla_kn_gain, mla_kr_gain, od_w_out, od_norm_ffn, moe_router, moe_w_gate, moe_w_up, moe_w_down):
    b, s, d = x.shape
    x2 = x.reshape(b * s, d)
    depth = ev_norm_mix.shape[0] + od_norm_mix.shape[0]
    for layer in range(depth):
        i = layer // 2
        if layer % 2 == 0:
            x2 = _even_layer(x2, b, s, layer, ev_norm_mix[i], ev_w_in[i], da_q_gain[i], da_k_gain[i],
                             da_lambda[i], da_subln_gain[i], ssm_conv_w[i], ssm_conv_b[i],
                             ssm_dt_bias[i], ssm_a_log[i], ssm_d[i], ssm_norm_gain[i], ev_w_out[i],
                             ev_norm_ffn[i], ffn_w_gate[i], ffn_w_up[i], ffn_w_down[i])
        else:
            x2 = _odd_layer(x2, b, s, od_norm_mix[i], od_w_in[i], nsa_q_gain[i], nsa_k_gain[i],
                            nsa_cmp_pos[i], nsa_cmp_w1[i], nsa_cmp_w2[i], mla_cq_gain[i],
                            mla_ckv_gain[i], mla_w_uq[i], mla_w_ukv[i], mla_qn_gain[i], mla_qr_gain[i],
                            mla_kn_gain[i], mla_kr_gain[i], od_w_out[i], od_norm_ffn[i], moe_router[i],
                            moe_w_gate[i], moe_w_up[i], moe_w_down[i])
    return x2.reshape(b, s, d)
```

```python
import functools
import math

import numpy as np
import jax
import jax.numpy as jnp
from jax import lax
from jax.experimental import pallas as pl
from jax.experimental.pallas import tpu as pltpu

F32 = jnp.float32
BF16 = jnp.bfloat16

ROPE_THETA = 10000.0
NORM_EPS = 1e-6
NEG_INF = -1e30
FORCE_SCORE = 1e6
LOG2E = 1.4426950408889634

DA_HEADS = 4
DA_HEAD_DIM = 64
DA_V_DIM = 2 * DA_HEAD_DIM
SSM_HEADS = 8
SSM_HEAD_DIM = 64
SSM_D_INNER = SSM_HEADS * SSM_HEAD_DIM
SSM_GROUPS = 2
SSM_STATE = 128
SSM_CONV = 4
SSM_CHUNK = 256
NSA_HEADS = 8
NSA_KV_GROUPS = 2
NSA_HEAD_DIM = 64
NSA_CMP_BLOCK = 32
NSA_CMP_STRIDE = 16
NSA_SEL_BLOCK = 64
NSA_TOP_N = 16
NSA_WINDOW = 512
MLA_HEADS = 4
MLA_NOPE_DIM = 128
MLA_ROPE_DIM = 64
MLA_V_DIM = 128
N_EXPERTS = 8

LANES = 128
SUBLANES = 8
VMEM_LIMIT = 48 * 1024 * 1024
MOE_VMEM_LIMIT = 58 * 1024 * 1024

NT_DIMS = (((1,), (1,)), ((), ()))


def _cparams(semantics):
    return pltpu.CompilerParams(dimension_semantics=semantics, vmem_limit_bytes=VMEM_LIMIT)


def _dot(a, b):
    return jnp.dot(a, b, preferred_element_type=F32)


def _dot_nt(a, b):
    return lax.dot_general(a, b, NT_DIMS, preferred_element_type=F32)


def _split_bf16(x, parts):
    out = []
    for _ in range(parts):
        hi = x.astype(BF16)
        out.append(hi)
        x = x - hi.astype(F32)
    return out


def _fold_rows(x):
    return jnp.sum(x.reshape(x.shape[0] // SUBLANES, SUBLANES, x.shape[1]), axis=0)


ONES_ROWS = 16


def _with_ones_rows(vt):
    return jnp.concatenate([vt, jnp.ones((ONES_ROWS, vt.shape[1]), vt.dtype)], axis=0)


def _sigmoid(x):
    return 1.0 / (1.0 + jnp.exp(-x))


def _silu(x):
    return x * _sigmoid(x)


def _softplus(x):
    return jnp.maximum(x, 0.0) + jnp.log(1.0 + jnp.exp(-jnp.abs(x)))


def _rms(x, gain):
    ms = jnp.mean(x * x, axis=-1, keepdims=True)
    return x * lax.rsqrt(ms + NORM_EPS) * gain


class HeadNorm:
    def __init__(self, gain, hd, rope=False, mul=1.0):
        self.gain, self.hd, self.rope, self.mul = gain, hd, rope, mul


def _head_norm(y, gain, bd, cos_ref, sin_ref, post):
    n = y.shape[1]
    hd = post.hd
    hi, lo = _split_bf16(y * y, 2)
    ss = _dot(hi, bd) + _dot(lo, bd)
    yn = y * lax.rsqrt(ss * (1.0 / hd) + NORM_EPS) * gain
    if post.rope:
        reps = n // LANES
        cos = jnp.concatenate([cos_ref[...]] * reps, axis=1) if reps > 1 else cos_ref[...]
        sin = jnp.concatenate([sin_ref[...]] * reps, axis=1) if reps > 1 else sin_ref[...]
        lane = lax.broadcasted_iota(jnp.int32, yn.shape, 1)
        first_half = (lane & (hd - 1)) < (hd // 2)
        partner = jnp.where(first_half, pltpu.roll(yn, n - hd // 2, 1), pltpu.roll(yn, hd // 2, 1))
        yn = yn * cos + partner * sin
    if post.mul != 1.0:
        yn = yn * post.mul
    return yn


def _norm_proj_body(x_ref, g_ref, *refs, posts, use_rope, transposed):
    if use_rope:
        cos_ref, sin_ref = refs[0], refs[1]
        refs = refs[2:]
    else:
        cos_ref = sin_ref = None
    n_out = len(posts)
    n_aux = 2 * sum(p is not None for p in posts)
    w_refs, aux, o_refs = refs[:n_out], refs[n_out:n_out + n_aux], refs[n_out + n_aux:]
    h = _rms(x_ref[...], g_ref[...]).astype(BF16)
    a = 0
    for k, (w_ref, o_ref, post) in enumerate(zip(w_refs, o_refs, posts)):
        if k in transposed:
            o_ref[...] = _dot_nt(w_ref[...], h).astype(o_ref.dtype)
            continue
        y = _dot(h, w_ref[...])
        if post is not None:
            y = _head_norm(y, aux[a][...], aux[a + 1][...], cos_ref, sin_ref, post)
            a += 2
        o_ref[...] = y.astype(o_ref.dtype)


def norm_proj(x2, gain, weights, out_dtypes, posts=None, seq=None, rope_tables=None, transposed=(), tm=512):
    t, d = x2.shape
    posts = posts or [None] * len(weights)
    transposed = frozenset(transposed)
    use_rope = any(p is not None and p.rope for p in posts)
    const = lambda i: (0, 0)
    args = [x2, gain.reshape(1, d).astype(F32)]
    in_specs = [pl.BlockSpec((tm, d), lambda i: (i, 0)), pl.BlockSpec((1, d), const)]
    if use_rope:
        per_seq = seq // tm
        args += list(rope_tables)
        in_specs += [pl.BlockSpec((tm, LANES), lambda i: (i % per_seq, 0))] * 2
    args += list(weights)
    in_specs += [pl.BlockSpec(w.shape, const) for w in weights]
    for w, p in zip(weights, posts):
        if p is not None:
            n = w.shape[1]
            args += [jnp.tile(p.gain.astype(F32), n // p.hd).reshape(1, n), _block_diag_ones(n, p.hd)]
            in_specs += [pl.BlockSpec((1, n), const), pl.BlockSpec((n, n), const)]
    out_specs, out_shape = [], []
    for k, (w, dt) in enumerate(zip(weights, out_dtypes)):
        if k in transposed:
            out_specs.append(pl.BlockSpec((w.shape[0], tm), lambda i: (0, i)))
            out_shape.append(jax.ShapeDtypeStruct((w.shape[0], t), dt))
        else:
            out_specs.append(pl.BlockSpec((tm, w.shape[1]), lambda i: (i, 0)))
            out_shape.append(jax.ShapeDtypeStruct((t, w.shape[1]), dt))
    return pl.pallas_call(
        functools.partial(_norm_proj_body, posts=tuple(posts), use_rope=use_rope, transposed=transposed),
        grid=(t // tm,), in_specs=in_specs, out_specs=out_specs, out_shape=out_shape,
        compiler_params=_cparams(("parallel",)), name="norm_proj",
    )(*args)


def _block_diag_ones(n, hd):
    idx = np.arange(n) // hd
    return jnp.asarray((idx[:, None] == idx[None, :]).astype(np.float32), dtype=BF16)


def _rope_tables(seq, hd):
    inv_freq = 1.0 / (ROPE_THETA ** (jnp.arange(0, hd, 2, dtype=F32) / hd))
    ang = jnp.arange(seq, dtype=F32)[:, None] * inv_freq[None, :]
    cos, sin = jnp.cos(ang), jnp.sin(ang)
    reps = LANES // hd
    cos_t = jnp.tile(jnp.concatenate([cos, cos], axis=1), (1, reps))
    sin_t = jnp.tile(jnp.concatenate([-sin, sin], axis=1), (1, reps))
    return cos_t, sin_t


def _flash_body(lam_ref, *refs, n_qk, diff, out_scale, sub):
    q_refs = refs[:n_qk]
    k_refs = refs[n_qk:2 * n_qk]
    vt_ref, gain_ref, o_ref, m_ref, l_ref, acc_ref, s0_ref, s1_ref, p0_ref, p1_ref = refs[2 * n_qk:]
    i = pl.program_id(2)
    n_sm = 2 if diff else 1
    _, tk, tq = s0_ref.shape
    s_slots = (s0_ref, s1_ref)
    p_slots = (p0_ref, p1_ref)

    m_ref[...] = jnp.full(m_ref.shape, NEG_INF, F32)
    l_ref[...] = jnp.zeros(l_ref.shape, F32)
    acc_ref[...] = jnp.zeros(acc_ref.shape, F32)

    qs = [r[0] for r in q_refs]
    q = qs[0] if n_qk == 1 else jnp.concatenate(qs, axis=1)
    if diff:
        lane = lax.broadcasted_iota(jnp.int32, q.shape, 1)
        half = q.shape[1] // 2
        zero = jnp.zeros_like(q)
        q_parts = [jnp.where(lane < half, q, zero), jnp.where(lane >= half, q, zero)]
    else:
        q_parts = [q]

    def scores(c, slot, key_offset=None):
        rows = pl.ds(pl.multiple_of(c * tk, tk), tk)
        ks = [r[0, rows, :] for r in k_refs]
        k = ks[0] if n_qk == 1 else jnp.concatenate(ks, axis=1)
        for sm in range(n_sm):
            s = _dot_nt(k, q_parts[sm])
            if key_offset is not None:
                row = lax.broadcasted_iota(jnp.int32, s.shape, 0)
                col = lax.broadcasted_iota(jnp.int32, s.shape, 1)
                s = jnp.where(row + key_offset <= col, s, NEG_INF)
            s_slots[slot][sm] = s

    def update(c, slot):
        vt = vt_ref[:, pl.ds(pl.multiple_of(c * tk, tk), tk)]
        for sm in range(n_sm):
            s_ref, p_ref = s_slots[slot], p_slots[slot]
            m_prev = m_ref[sm]
            m_new = jnp.maximum(m_prev, jnp.max(s_ref[sm], axis=0, keepdims=True))
            m_ref[sm] = m_new
            alpha = jnp.exp2(m_prev - m_new)
            lpart = jnp.zeros((SUBLANES, tq), F32)
            for r in range(tk // sub):
                p = jnp.exp2(s_ref[sm, r * sub:(r + 1) * sub, :] - m_new)
                lpart = lpart + _fold_rows(p)
                p_ref[sm, r * sub:(r + 1) * sub, :] = p.astype(BF16)
            l_ref[sm] = alpha * l_ref[sm] + jnp.sum(lpart, axis=0, keepdims=True)
            acc_ref[sm] = alpha * acc_ref[sm] + _dot(vt, p_ref[sm])

    if tq == tk:
        n_chunks = i + 1
        last_past = jnp.maximum(i - 1, 0)

        def chunk_at(j):
            return jnp.where(j == 0, i, j - 1)

        scores(i, 0, key_offset=0)

        def pair(k2, carry):
            scores(jnp.minimum(2 * k2, last_past), 1)
            update(chunk_at(2 * k2), 0)
            scores(jnp.minimum(2 * k2 + 1, last_past), 0)
            update(2 * k2, 1)
            return carry

        lax.fori_loop(0, n_chunks // 2, pair, 0)

        @pl.when(n_chunks % 2 == 1)
        def _():
            update(chunk_at(n_chunks - 1), 0)
    else:
        n_past = 2 * i
        last_past = jnp.maximum(n_past - 1, 0)
        scores(n_past, 0, key_offset=0)
        scores(n_past + 1, 1, key_offset=tk)
        update(n_past, 0)
        scores(0, 0)
        update(n_past + 1, 1)

        def pair(k2, carry):
            scores(2 * k2 + 1, 1)
            update(2 * k2, 0)
            scores(jnp.minimum(2 * k2 + 2, last_past), 0)
            update(2 * k2 + 1, 1)
            return carry

        lax.fori_loop(0, i, pair, 0)

    o = acc_ref[0] * (1.0 / l_ref[0])
    if diff:
        o = o - acc_ref[1] * (lam_ref[0] / l_ref[1])
        ms = jnp.mean(o * o, axis=0, keepdims=True)
        o = o * lax.rsqrt(ms + NORM_EPS) * gain_ref[...] * out_scale
    o_ref[0] = o.T.astype(o_ref.dtype)


def flash_attention(lam, qs, ks, vt, gain, n_heads, dv, *, diff, out_scale=1.0, tk=512):
    b, s, _ = qs[0].shape
    tile = tk if diff else 2 * tk
    sub = 64 if diff else 32
    nt = s // tile
    n_qk = len(qs)
    in_specs = [pl.BlockSpec(memory_space=pltpu.SMEM)]
    for q in qs:
        w = q.shape[2] // n_heads
        in_specs.append(pl.BlockSpec((1, tile, w), lambda bb, h, i: (bb, i, h)))
    for q, k in zip(qs, ks):
        w = q.shape[2] // n_heads
        if k.shape[2] == w:
            in_specs.append(pl.BlockSpec((1, s, w), lambda bb, h, i: (bb, 0, 0)))
        else:
            in_specs.append(pl.BlockSpec((1, s, w), lambda bb, h, i: (bb, 0, h)))
    in_specs.append(pl.BlockSpec((dv, s), lambda bb, h, i: (h, bb)))
    in_specs.append(pl.BlockSpec((dv, 1), lambda bb, h, i: (0, 0)))
    n_sm = 2 if diff else 1
    return pl.pallas_call(
        functools.partial(_flash_body, n_qk=n_qk, diff=diff, out_scale=out_scale, sub=sub),
        grid=(b, n_heads, nt), in_specs=in_specs,
        out_specs=pl.BlockSpec((1, tile, dv), lambda bb, h, i: (bb, i, h)),
        out_shape=jax.ShapeDtypeStruct((b, s, n_heads * dv), BF16),
        scratch_shapes=[pltpu.VMEM((n_sm, 1, tile), F32), pltpu.VMEM((n_sm, 1, tile), F32),
                        pltpu.VMEM((n_sm, dv, tile), F32),
                        pltpu.VMEM((n_sm, tk, tile), F32), pltpu.VMEM((n_sm, tk, tile), F32),
                        pltpu.VMEM((n_sm, tk, tile), BF16), pltpu.VMEM((n_sm, tk, tile), BF16)],
        compiler_params=_cparams(("parallel", "parallel", "arbitrary")),
        name="flash_diff" if diff else "flash_plain",
    )(lam, *qs, *ks, vt, gain.reshape(dv, 1).astype(F32))


def _ssd_body(xbc_ref, z_ref, dt_ref, dtt_ref, cw_ref, cb_ref, dtb_ref, dtbt_ref, al_ref, alt_ref,
              dsk_ref, ng_ref, o_ref, xpad_ref, state_ref):
    chunk = xbc_ref.shape[1]
    d_in = z_ref.shape[2]
    gn = SSM_GROUPS * SSM_STATE
    c = pl.program_id(1)

    @pl.when(c == 0)
    def _():
        xpad_ref[0:8, :] = jnp.zeros((8, xpad_ref.shape[1]), F32)
        state_ref[...] = jnp.zeros(state_ref.shape, F32)

    xpad_ref[8:8 + chunk, :] = xbc_ref[0]
    conv = cb_ref[...]
    for w in range(SSM_CONV):
        conv = conv + cw_ref[w:w + 1, :] * xpad_ref[pl.ds(8 - (SSM_CONV - 1) + w, chunk), :]
    xpad_ref[0:8, :] = xpad_ref[chunk:chunk + 8, :]
    u = _silu(conv)
    xs = u[:, :d_in]
    bmat = u[:, d_in:d_in + gn]
    cmat = u[:, d_in + gn:]

    dt = _softplus(dt_ref[0] + dtb_ref[...])
    ad = dt * (-jnp.exp(al_ref[...]))
    dtt = _softplus(dtt_ref[0] + dtbt_ref[...])
    adt = dtt * (-jnp.exp(alt_ref[...]))
    row = lax.broadcasted_iota(jnp.int32, (chunk, chunk), 0)
    col = lax.broadcasted_iota(jnp.int32, (chunk, chunk), 1)
    lower = row >= col
    tril = jnp.where(lower, 1.0, 0.0).astype(BF16)
    triu = jnp.where(row <= col, 1.0, 0.0).astype(BF16)
    cs = sum(_dot(tril, part) for part in _split_bf16(ad, 3))
    cst = sum(_dot(part, triu) for part in _split_bf16(adt, 3))

    heads_per_group = SSM_HEADS // SSM_GROUPS
    dsk = dsk_ref[...]
    ys = []
    for g in range(SSM_GROUPS):
        bg = bmat[:, g * SSM_STATE:(g + 1) * SSM_STATE]
        cg = cmat[:, g * SSM_STATE:(g + 1) * SSM_STATE].astype(BF16)
        cb = _dot_nt(cg, bg.astype(BF16))
        bgt = bg.T.astype(BF16)
        for r in range(heads_per_group):
            h = g * heads_per_group + r
            ccol = cs[:, h:h + 1]
            crow = cst[h:h + 1, :]
            decay = jnp.exp(jnp.where(lower, ccol - crow, NEG_INF))
            x_h = xs[:, h * SSM_HEAD_DIM:(h + 1) * SSM_HEAD_DIM]
            xdt = x_h * dt[:, h:h + 1]
            y = _dot((cb * decay).astype(BF16), xdt.astype(BF16))
            st = state_ref[h]
            y = y + _dot(cg, st.astype(BF16)) * jnp.exp(ccol)
            last = cst[h:h + 1, chunk - 1:chunk]
            to_end = jnp.exp(last - ccol)
            state_ref[h] = st * jnp.exp(last) + _dot(bgt, (xdt * to_end).astype(BF16))
            ys.append(y + x_h * dsk[:, h * SSM_HEAD_DIM:(h + 1) * SSM_HEAD_DIM])

    y = jnp.concatenate(ys, axis=1) * _silu(z_ref[0])
    gw = d_in // SSM_GROUPS
    for g in range(SSM_GROUPS):
        seg = y[:, g * gw:(g + 1) * gw]
        o_ref[0, :, g * gw:(g + 1) * gw] = _rms(seg, ng_ref[:, g * gw:(g + 1) * gw]).astype(o_ref.dtype)


def ssd_mixer(xbc, z, dt_raw, conv_w, conv_b, dt_bias, a_log, d_skip, norm_gain):
    b, s, cch = xbc.shape
    d_in = z.shape[2]
    nc = s // SSM_CHUNK
    hpad = dt_raw.shape[2]
    dtt = jnp.transpose(dt_raw[:, :, :SSM_HEADS], (0, 2, 1))

    def lane_pad(v):
        return jnp.pad(v.astype(F32), (0, hpad - SSM_HEADS)).reshape(1, hpad)

    args = (xbc, z, dt_raw, dtt, conv_w.astype(F32), conv_b.reshape(1, cch).astype(F32),
            lane_pad(dt_bias), dt_bias.reshape(SSM_HEADS, 1).astype(F32),
            lane_pad(a_log), a_log.reshape(SSM_HEADS, 1).astype(F32),
            jnp.repeat(d_skip.astype(F32), SSM_HEAD_DIM).reshape(1, d_in),
            norm_gain.reshape(1, d_in).astype(F32))
    const = lambda bb, c: (0, 0)
    in_specs = [pl.BlockSpec((1, SSM_CHUNK, cch), lambda bb, c: (bb, c, 0)),
                pl.BlockSpec((1, SSM_CHUNK, d_in), lambda bb, c: (bb, c, 0)),
                pl.BlockSpec((1, SSM_CHUNK, hpad), lambda bb, c: (bb, c, 0)),
                pl.BlockSpec((1, SSM_HEADS, SSM_CHUNK), lambda bb, c: (bb, 0, c)),
                pl.BlockSpec((SSM_CONV, cch), const), pl.BlockSpec((1, cch), const),
                pl.BlockSpec((1, hpad), const), pl.BlockSpec((SSM_HEADS, 1), const),
                pl.BlockSpec((1, hpad), const), pl.BlockSpec((SSM_HEADS, 1), const),
                pl.BlockSpec((1, d_in), const), pl.BlockSpec((1, d_in), const)]
    return pl.pallas_call(
        _ssd_body, grid=(b, nc), in_specs=in_specs,
        out_specs=pl.BlockSpec((1, SSM_CHUNK, d_in), lambda bb, c: (bb, c, 0)),
        out_shape=jax.ShapeDtypeStruct((b, s, d_in), BF16),
        scratch_shapes=[pltpu.VMEM((SSM_CHUNK + 8, cch), F32),
                        pltpu.VMEM((SSM_HEADS, SSM_STATE, SSM_HEAD_DIM), F32)],
        compiler_params=_cparams(("parallel", "arbitrary")), name="ssd_mixer",
    )(*args)


def _out_proj_body(x_ref, a_ref, b_ref, wa_ref, wb_ref, o_ref):
    o_ref[...] = x_ref[...] + _dot(a_ref[...], wa_ref[...]) + _dot(b_ref[...], wb_ref[...])


def out_proj_residual(x2, a, bm, wa, wb, tm=512):
    t, d = x2.shape
    return pl.pallas_call(
        _out_proj_body, grid=(t // tm,),
        in_specs=[pl.BlockSpec((tm, d), lambda i: (i, 0)),
                  pl.BlockSpec((tm, a.shape[1]), lambda i: (i, 0)),
                  pl.BlockSpec((tm, bm.shape[1]), lambda i: (i, 0)),
                  pl.BlockSpec(wa.shape, lambda i: (0, 0)),
                  pl.BlockSpec(wb.shape, lambda i: (0, 0))],
        out_specs=pl.BlockSpec((tm, d), lambda i: (i, 0)),
        out_shape=jax.ShapeDtypeStruct((t, d), F32),
        compiler_params=_cparams(("parallel",)), name="out_proj",
    )(x2, a, bm, wa, wb)


def _ffn_body(x_ref, g_ref, wg_ref, wu_ref, wd_ref, o_ref, h_ref):
    f = pl.program_id(1)

    @pl.when(f == 0)
    def _():
        x = x_ref[...]
        h_ref[...] = _rms(x, g_ref[...]).astype(BF16)
        o_ref[...] = x

    half = h_ref.shape[0] // 2
    for r in (slice(0, half), slice(half, 2 * half)):
        h = h_ref[r, :]
        act = (_silu(_dot(h, wg_ref[...])) * _dot(h, wu_ref[...])).astype(BF16)
        o_ref[r, :] += _dot(act, wd_ref[...])


def ffn_residual(x2, gain, w_gate, w_up, w_down, tm=1024, tf=1408):
    t, d = x2.shape
    d_ff = w_gate.shape[1]
    return pl.pallas_call(
        _ffn_body, grid=(t // tm, d_ff // tf),
        in_specs=[pl.BlockSpec((tm, d), lambda i, f: (i, 0)),
                  pl.BlockSpec((1, d), lambda i, f: (0, 0)),
                  pl.BlockSpec((d, tf), lambda i, f: (0, f)),
                  pl.BlockSpec((d, tf), lambda i, f: (0, f)),
                  pl.BlockSpec((tf, d), lambda i, f: (f, 0))],
        out_specs=pl.BlockSpec((tm, d), lambda i, f: (i, 0)),
        out_shape=jax.ShapeDtypeStruct((t, d), F32),
        scratch_shapes=[pltpu.VMEM((tm, d), BF16)],
        compiler_params=_cparams(("parallel", "arbitrary")), name="ffn",
    )(x2, gain.reshape(1, d).astype(F32), w_gate, w_up, w_down)


MOE_ROWS = 256
ROUTE_IDX = 0
ROUTE_W = 2
ROUTE_RANK = 4


def _moe_ffn_body(block_expert_ref, n_used_ref, xs_ref, wg_ref, wu_ref, wd_ref, o_ref):
    i = pl.program_id(0)

    @pl.when(i < n_used_ref[0])
    def _():
        x = xs_ref[...]
        act = (_silu(_dot(x, wg_ref[0])) * _dot(x, wu_ref[0])).astype(BF16)
        o_ref[...] = _dot(act, wd_ref[0]).astype(o_ref.dtype)

    @pl.when(i >= n_used_ref[0])
    def _():
        o_ref[...] = jnp.zeros(o_ref.shape, o_ref.dtype)


def moe_expert_ffn(xs, block_expert, n_used, w_gate, w_up, w_down):
    p, d = xs.shape
    d_ff = w_gate.shape[2]
    rows = MOE_ROWS
    grid_spec = pltpu.PrefetchScalarGridSpec(
        num_scalar_prefetch=2, grid=(p // rows,),
        in_specs=[pl.BlockSpec((rows, d), lambda i, be, nu: (i, 0)),
                  pl.BlockSpec((1, d, d_ff), lambda i, be, nu: (be[i], 0, 0)),
                  pl.BlockSpec((1, d, d_ff), lambda i, be, nu: (be[i], 0, 0)),
                  pl.BlockSpec((1, d_ff, d), lambda i, be, nu: (be[i], 0, 0))],
        out_specs=pl.BlockSpec((rows, d), lambda i, be, nu: (i, 0)))
    return pl.pallas_call(
        _moe_ffn_body, grid_spec=grid_spec, out_shape=jax.ShapeDtypeStruct((p, d), BF16),
        compiler_params=pltpu.CompilerParams(
            dimension_semantics=("arbitrary",), vmem_limit_bytes=MOE_VMEM_LIMIT),
        name="moe_expert_ffn",
    )(block_expert, n_used, xs, w_gate, w_up, w_down)


def _moe_combine_body(x_ref, y0_ref, y1_ref, route_ref, o_ref):
    route = route_ref[...]
    lane = lax.broadcasted_iota(jnp.int32, route.shape, 1)
    w0 = jnp.sum(jnp.where(lane == ROUTE_W, route, 0.0), axis=1, keepdims=True)
    w1 = jnp.sum(jnp.where(lane == ROUTE_W + 1, route, 0.0), axis=1, keepdims=True)
    o_ref[...] = x_ref[...] + w0 * y0_ref[...].astype(F32) + w1 * y1_ref[...].astype(F32)


def moe_combine(x2, y0, y1, route, tm=512):
    t, d = x2.shape
    row = lambda i: (i, 0)
    return pl.pallas_call(
        _moe_combine_body, grid=(t // tm,),
        in_specs=[pl.BlockSpec((tm, d), row), pl.BlockSpec((tm, d), row), pl.BlockSpec((tm, d), row),
                  pl.BlockSpec((tm, LANES), row)],
        out_specs=pl.BlockSpec((tm, d), row),
        out_shape=jax.ShapeDtypeStruct((t, d), F32),
        compiler_params=_cparams(("parallel",)), name="moe_combine",
    )(x2, y0, y1, route)


def moe_residual(x2, h, route, counts, w_gate, w_up, w_down):
    t, d = x2.shape
    n_e = w_gate.shape[0]
    rows = MOE_ROWS
    expert = route[:, ROUTE_IDX:ROUTE_IDX + 2].astype(jnp.int32).reshape(-1)
    rank = route[:, ROUTE_RANK:ROUTE_RANK + 2].astype(jnp.int32).reshape(-1)
    padded = (counts[0, :n_e].astype(jnp.int32) + rows - 1) // rows * rows
    ends = jnp.cumsum(padded)
    own = expert[:, None] == jnp.arange(n_e, dtype=jnp.int32)[None, :]
    slot = jnp.sum(jnp.where(own, (ends - padded)[None, :], 0), axis=1) + rank
    p_rows = 2 * t + n_e * rows
    token_of_slot = jnp.zeros((p_rows,), jnp.int32).at[slot].set(
        jnp.arange(2 * t, dtype=jnp.int32) // 2, unique_indices=True)
    block_start = jnp.arange(p_rows // rows, dtype=jnp.int32) * rows
    block_expert = jnp.minimum(jnp.sum(block_start[:, None] >= ends[None, :], axis=1), n_e - 1).astype(jnp.int32)
    n_used = (ends[-1] // rows).astype(jnp.int32).reshape(1)

    take_rows = lambda a, idx: a.at[idx].get(mode="promise_in_bounds")
    xs = take_rows(h, token_of_slot)
    ys = moe_expert_ffn(xs, block_expert, n_used, w_gate, w_up, w_down)
    slot2 = slot.reshape(t, 2)
    return moe_combine(x2, take_rows(ys, slot2[:, 0]), take_rows(ys, slot2[:, 1]), route)


def _router_body(x_ref, g_ref, r_ref, h_ref, o_ref, count_ref, run_ref):
    @pl.when(pl.program_id(0) == 0)
    def _():
        run_ref[...] = jnp.zeros(run_ref.shape, F32)

    h = _rms(x_ref[...], g_ref[...])
    h_ref[...] = h.astype(h_ref.dtype)
    logits = jnp.dot(h, r_ref[...], precision=lax.Precision.HIGHEST, preferred_element_type=F32)
    lane = lax.broadcasted_iota(jnp.int32, logits.shape, 1).astype(F32)
    low = jnp.float32(-3.0e38)
    logits = jnp.where(lane < N_EXPERTS, logits, low)
    m1 = jnp.max(logits, axis=1, keepdims=True)
    i1 = jnp.min(jnp.where(logits == m1, lane, float(LANES)), axis=1, keepdims=True)
    rest = jnp.where(lane == i1, low, logits)
    m2 = jnp.max(rest, axis=1, keepdims=True)
    i2 = jnp.min(jnp.where(rest == m2, lane, float(LANES)), axis=1, keepdims=True)
    ex = jnp.exp(m2 - m1)
    w1 = 1.0 / (1.0 + ex)
    w2 = ex / (1.0 + ex)
    tm = logits.shape[0]
    routed = jnp.where((lane == i1) | (lane == i2), 1.0, 0.0)
    row = lax.broadcasted_iota(jnp.int32, (tm, tm), 0)
    col = lax.broadcasted_iota(jnp.int32, (tm, tm), 1)
    before = jnp.where(col < row, 1.0, 0.0).astype(BF16)
    rank = run_ref[0:1, :] + _dot(before, routed.astype(BF16))
    r1 = jnp.sum(jnp.where(lane == i1, rank, 0.0), axis=1, keepdims=True)
    r2 = jnp.sum(jnp.where(lane == i2, rank, 0.0), axis=1, keepdims=True)
    fields = ((ROUTE_IDX, i1), (ROUTE_IDX + 1, i2), (ROUTE_W, w1), (ROUTE_W + 1, w2),
              (ROUTE_RANK, r1), (ROUTE_RANK + 1, r2))
    out = jnp.zeros(logits.shape, F32)
    for pos, val in fields:
        out = jnp.where(lane == pos, val, out)
    o_ref[...] = out
    run_ref[...] = run_ref[...] + jnp.sum(routed, axis=0, keepdims=True)
    count_ref[...] = run_ref[...]


def router(x2, gain, router_w, tm=512):
    t, d = x2.shape
    r_pad = jnp.pad(router_w.astype(F32), ((0, 0), (0, LANES - router_w.shape[1])))
    return pl.pallas_call(
        _router_body, grid=(t // tm,),
        in_specs=[pl.BlockSpec((tm, d), lambda i: (i, 0)),
                  pl.BlockSpec((1, d), lambda i: (0, 0)),
                  pl.BlockSpec((d, LANES), lambda i: (0, 0))],
        out_specs=[pl.BlockSpec((tm, d), lambda i: (i, 0)), pl.BlockSpec((tm, LANES), lambda i: (i, 0)),
                   pl.BlockSpec((SUBLANES, LANES), lambda i: (0, 0))],
        out_shape=[jax.ShapeDtypeStruct((t, d), BF16), jax.ShapeDtypeStruct((t, LANES), F32),
                   jax.ShapeDtypeStruct((SUBLANES, LANES), F32)],
        scratch_shapes=[pltpu.VMEM((SUBLANES, LANES), F32)],
        compiler_params=_cparams(("arbitrary",)), name="router",
    )(x2, gain.reshape(1, d).astype(F32), r_pad)


def _compress_body(ch_ref, nx_ref, pos_ref, w1_ref, w2_ref, gain_ref, cos_ref, sin_ref, rot_ref, o_ref,
                   *, is_key):
    a = _dot((ch_ref[0] + pos_ref[0]).astype(BF16), w1_ref[0, 0])
    a = a + _dot((nx_ref[0] + pos_ref[1]).astype(BF16), w1_ref[0, 1])
    out = _dot(_silu(a).astype(BF16), w2_ref[...])
    if is_key:
        out = _rms(out, gain_ref[...])
        hi, lo = _split_bf16(out, 2)
        partner = _dot(hi, rot_ref[...]) + _dot(lo, rot_ref[...])
        out = out * cos_ref[...] + partner * sin_ref[...]
    o_ref[0, 0] = out


def nsa_compress(t, pos, w1, w2, gain, seq, is_key):
    b, s, _ = t.shape
    g, d = NSA_KV_GROUPS, NSA_HEAD_DIM
    n_ch = s // NSA_CMP_STRIDE
    half = NSA_CMP_STRIDE * g * d
    ch = t.reshape(b, n_ch, half)
    nxt = jnp.concatenate([ch[:, 1:], jnp.zeros((b, 1, half), F32)], axis=1)
    pos2 = jnp.broadcast_to(pos.astype(F32).reshape(2, NSA_CMP_STRIDE, 1, d),
                            (2, NSA_CMP_STRIDE, g, d)).reshape(2, 1, half)
    w1r = w1.astype(BF16).reshape(2, NSA_CMP_STRIDE, 1, d, d)
    own = (jnp.arange(g)[:, None] == jnp.arange(g)[None, :]).reshape(g, 1, 1, g, 1, 1)
    w1s = jnp.where(own, w1r[None], jnp.zeros((), BF16)).reshape(g, 2, half, d)
    cmp_end = jnp.arange(n_ch) * NSA_CMP_STRIDE + NSA_CMP_BLOCK - 1
    inv_freq = 1.0 / (ROPE_THETA ** (jnp.arange(0, d, 2, dtype=F32) / d))
    ang = cmp_end.astype(F32)[:, None] * inv_freq[None, :]
    cos = jnp.concatenate([jnp.cos(ang)] * 2, axis=1)
    sin = jnp.concatenate([jnp.sin(ang)] * 2, axis=1)
    rot = np.zeros((d, d), np.float32)
    rot[np.arange(d // 2) + d // 2, np.arange(d // 2)] = -1.0
    rot[np.arange(d // 2), np.arange(d // 2) + d // 2] = 1.0
    blk = lambda bb, gg: (bb, gg, 0, 0)
    seq = lambda bb, gg: (bb, 0, 0)
    c2 = lambda bb, gg: (0, 0)
    c3 = lambda bb, gg: (0, 0, 0)
    return pl.pallas_call(
        functools.partial(_compress_body, is_key=is_key), grid=(b, g),
        in_specs=[pl.BlockSpec((1, n_ch, half), seq), pl.BlockSpec((1, n_ch, half), seq),
                  pl.BlockSpec((2, 1, half), c3), pl.BlockSpec((1, 2, half, d), lambda bb, gg: (gg, 0, 0, 0)),
                  pl.BlockSpec((d, d), c2), pl.BlockSpec((1, d), c2),
                  pl.BlockSpec((n_ch, d), c2), pl.BlockSpec((n_ch, d), c2), pl.BlockSpec((d, d), c2)],
        out_specs=pl.BlockSpec((1, 1, n_ch, d), blk),
        out_shape=jax.ShapeDtypeStruct((b, g, n_ch, d), F32),
        compiler_params=_cparams(("parallel", "parallel")), name="nsa_compress",
    )(ch, nxt, pos2, w1s, w2.astype(BF16), gain.reshape(1, d).astype(F32), cos, sin,
      jnp.asarray(rot, dtype=BF16))


def _nsa_body(q_ref, ck_ref, cvt_ref, ksl_ref, vslt_ref, kwn_ref, vwnt_ref, ovt_ref, glt_ref, o_ref,
              sc_ref, phi_ref, plo_ref, imp_ref, bias_ref, ss_ref, ps_ref, ss1_ref, ps1_ref, sw_ref, pw_ref,
              ow_ref, oc_ref, *, tq):
    g = pl.program_id(1)
    i = pl.program_id(2)
    d = NSA_HEAD_DIM
    rep = NSA_HEADS // NSA_KV_GROUPS
    t0 = i * tq
    n_cmp = ck_ref.shape[1]
    n_sel = ovt_ref.shape[0]
    width = rep * tq
    sub = NSA_SEL_BLOCK
    dead = 0.5 * NEG_INF
    v_rows = pl.ds(pl.multiple_of(g * d, d), d)

    qb = q_ref[0]
    q4 = jnp.concatenate([qb[:, r * d:(r + 1) * d] for r in range(rep)], axis=0)
    q4 = jnp.concatenate([q4, q4], axis=1)
    lane = lax.broadcasted_iota(jnp.int32, q4.shape, 1)
    q4 = jnp.where(jnp.right_shift(lane, d.bit_length() - 1) == g, q4, jnp.zeros_like(q4))

    def qpos_of(shape):
        return t0 + (lax.broadcasted_iota(jnp.int32, shape, 1) & (tq - 1))

    def compressed(rows):
        s = _dot_nt(ck_ref[0, 0:rows, :], q4)
        cmp_end = lax.broadcasted_iota(jnp.int32, s.shape, 0) * NSA_CMP_STRIDE + (NSA_CMP_BLOCK - 1)
        s = jnp.where(cmp_end <= qpos_of(s.shape), s, NEG_INF)
        sc_ref[0:rows, :] = s
        m_c = jnp.max(s, axis=0, keepdims=True)
        lpart = jnp.zeros((SUBLANES, width), F32)
        for r in range(rows // sub):
            e = jnp.exp2(sc_ref[r * sub:(r + 1) * sub, :] - m_c)
            lpart = lpart + _fold_rows(e)
            hi = e.astype(BF16)
            phi_ref[r * sub:(r + 1) * sub, :] = hi
            plo_ref[r * sub:(r + 1) * sub, :] = (e - hi.astype(F32)).astype(BF16)
        inv_c = jnp.where(m_c > dead, 1.0 / jnp.sum(lpart, axis=0, keepdims=True), 0.0)
        oc_ref[...] = _dot(cvt_ref[0, v_rows, 0:rows], phi_ref[0:rows, :]) * inv_c
        ovt = ovt_ref[:, 0:rows]
        imp4 = (_dot(ovt, phi_ref[0:rows, :]) + _dot(ovt, plo_ref[0:rows, :])) * inv_c
        imp = imp4[:, 0:tq]
        for r in range(1, rep):
            imp = imp + imp4[:, r * tq:(r + 1) * tq]
        imp_ref[...] = imp

    cmp_live = jnp.clip((t0 + tq - NSA_CMP_BLOCK) // NSA_CMP_STRIDE + 1, 1, n_cmp)
    cmp_step = min(2 * LANES, n_cmp)
    for v in range(n_cmp // cmp_step):
        @pl.when((cmp_live > cmp_step * v) & (cmp_live <= cmp_step * (v + 1)))
        def _():
            compressed(cmp_step * (v + 1))

    o_c = oc_ref[...]
    imp = imp_ref[...]

    blk = lax.broadcasted_iota(jnp.int32, imp.shape, 0)
    qp = t0 + lax.broadcasted_iota(jnp.int32, imp.shape, 1)
    cur = jnp.right_shift(qp, NSA_SEL_BLOCK.bit_length() - 1)
    forced = (blk == 0) | (blk == cur) | (blk == cur - 1)
    future = blk * NSA_SEL_BLOCK > qp
    imp_ref[...] = jnp.where(future, -FORCE_SCORE, jnp.where(forced, FORCE_SCORE, imp))
    bias_ref[...] = jnp.full(bias_ref.shape, NEG_INF, F32)

    n_live = jnp.minimum((t0 + tq - 1) // NSA_SEL_BLOCK + 1, n_sel)
    n_var = max(n_sel // 32, 1)
    rows_per = n_sel // n_var
    top_n = float(min(NSA_TOP_N, n_sel))
    for v in range(n_var):
        rows = rows_per * (v + 1)

        @pl.when((n_live > rows_per * v) & (n_live <= rows))
        def _():
            groups = rows // SUBLANES
            mine = [imp_ref[gi * SUBLANES:(gi + 1) * SUBLANES, :] for gi in range(groups)]
            rank = [jnp.zeros((SUBLANES, tq), F32) for _ in range(groups)]
            in_group = lax.broadcasted_iota(jnp.int32, (SUBLANES, tq), 0)
            for i2 in range(rows):
                other = imp_ref[i2:i2 + 1, :]
                for gi in range(groups):
                    if gi > i2 // SUBLANES:
                        beats = other >= mine[gi]
                    elif gi < i2 // SUBLANES:
                        beats = other > mine[gi]
                    else:
                        beats = (other > mine[gi]) | ((other == mine[gi]) & (in_group > i2 % SUBLANES))
                    rank[gi] = rank[gi] + jnp.where(beats, 1.0, 0.0)
            for gi in range(groups):
                bias = jnp.where(rank[gi] < top_n, 0.0, NEG_INF)
                bias_ref[gi * SUBLANES:(gi + 1) * SUBLANES, :] = jnp.concatenate([bias] * rep, axis=1)

    init = (jnp.full((1, width), NEG_INF, F32), jnp.zeros((d + ONES_ROWS, width), F32))

    def normalised(acc):
        return acc[:d, :] * (1.0 / acc[d:d + 1, :])

    chunk = 8 * sub
    n_sub = chunk // sub

    s_slots = (ss_ref, ss1_ref)
    p_slots = (ps_ref, ps1_ref)

    def sel_scores(c, slot, diagonal=False):
        start = pl.multiple_of(c * chunk, chunk)
        s = _dot_nt(ksl_ref[0, pl.ds(start, chunk), :], q4)
        if diagonal:
            kpos = start + lax.broadcasted_iota(jnp.int32, s.shape, 0)
            s = jnp.where(kpos <= qpos_of(s.shape), s, NEG_INF)
        s_slots[slot][...] = s

    def sel_update(c, slot, carry):
        m_prev, acc = carry
        s_ref, p_ref = s_slots[slot], p_slots[slot]
        biases = [bias_ref[pl.ds(c * n_sub + r, 1), :] for r in range(n_sub)]
        m8 = jnp.full((SUBLANES, width), NEG_INF, F32)
        for r in range(n_sub):
            block = s_ref[r * sub:(r + 1) * sub, :]
            m8 = jnp.maximum(m8, jnp.max(block.reshape(sub // SUBLANES, SUBLANES, width), axis=0) + biases[r])
        m_new = jnp.maximum(m_prev, jnp.max(m8, axis=0, keepdims=True))
        live = m_new > dead
        for r in range(n_sub):
            shift = jnp.where(live, biases[r] - m_new, NEG_INF)
            p_ref[r * sub:(r + 1) * sub, :] = jnp.exp2(s_ref[r * sub:(r + 1) * sub, :] + shift).astype(BF16)
        vt = _with_ones_rows(vslt_ref[v_rows, pl.ds(pl.multiple_of(c * chunk, chunk), chunk)])
        return m_new, jnp.exp2(m_prev - m_new) * acc + _dot(vt, p_ref[...])

    c_diag = (t0 + tq - 1) // chunk
    n_chunks = c_diag + 1
    last_past = jnp.maximum(c_diag - 1, 0)

    def chunk_at(j):
        return jnp.where(j == 0, c_diag, j - 1)

    sel_scores(c_diag, 0, diagonal=True)

    def pair(k, carry):
        sel_scores(jnp.minimum(2 * k, last_past), 1)
        carry = sel_update(chunk_at(2 * k), 0, carry)
        sel_scores(jnp.minimum(2 * k + 1, last_past), 0)
        return sel_update(2 * k, 1, carry)

    carry = lax.fori_loop(0, n_chunks // 2, pair, init)
    _, acc_s = lax.cond(n_chunks % 2 == 1, lambda cr: sel_update(chunk_at(n_chunks - 1), 0, cr),
                        lambda cr: cr, carry)
    o_s = normalised(acc_s)

    def win_chunk(c, carry):
        m_prev, acc = carry
        start = pl.multiple_of(c * tq, tq)
        s = _dot_nt(kwn_ref[0, pl.ds(start, tq), :], q4)
        kpos = start + lax.broadcasted_iota(jnp.int32, s.shape, 0)
        qpos = qpos_of(s.shape)
        s = jnp.where((kpos <= qpos) & (kpos > qpos - NSA_WINDOW), s, NEG_INF)
        m_new = jnp.maximum(m_prev, jnp.max(s, axis=0, keepdims=True))
        p = jnp.exp2(s + jnp.where(m_new > dead, -m_new, NEG_INF))
        vt = _with_ones_rows(vwnt_ref[v_rows, pl.ds(start, tq)])
        return m_new, jnp.exp2(m_prev - m_new) * acc + _dot(vt, p.astype(BF16))

    n_back = NSA_WINDOW // tq

    @pl.when(i < n_back)
    def _():
        _, acc_w = lax.fori_loop(0, i + 1, win_chunk, init)
        ow_ref[...] = normalised(acc_w)

    @pl.when(i >= n_back)
    def _():
        start = pl.multiple_of(t0 - NSA_WINDOW, tq)
        s = _dot_nt(kwn_ref[0, pl.ds(start, NSA_WINDOW + tq), :], q4)
        kpos = start + lax.broadcasted_iota(jnp.int32, (tq, width), 0)
        qpos = qpos_of((tq, width))
        sw_ref[0:tq, :] = jnp.where(kpos > qpos - NSA_WINDOW, s[0:tq, :], NEG_INF)
        sw_ref[tq:NSA_WINDOW, :] = s[tq:NSA_WINDOW, :]
        sw_ref[NSA_WINDOW:, :] = jnp.where(kpos + NSA_WINDOW <= qpos, s[NSA_WINDOW:, :], NEG_INF)
        m_w = jnp.max(sw_ref[...], axis=0, keepdims=True)
        for r in range((NSA_WINDOW + tq) // sub):
            pw_ref[r * sub:(r + 1) * sub, :] = jnp.exp2(sw_ref[r * sub:(r + 1) * sub, :] - m_w).astype(BF16)
        vt = _with_ones_rows(vwnt_ref[v_rows, pl.ds(start, NSA_WINDOW + tq)])
        ow_ref[...] = normalised(_dot(vt, pw_ref[...]))

    o_w = ow_ref[...]

    def gate(branch):
        rows = [glt_ref[pl.ds((g * rep + r) * 3 + branch, 1), :] for r in range(rep)]
        return _sigmoid(jnp.concatenate(rows, axis=1))

    out = gate(0) * o_c + gate(1) * o_s + gate(2) * o_w
    out_t = jnp.concatenate([out, jnp.zeros_like(out)], axis=0).T
    o_ref[0] = jnp.concatenate([out_t[r * tq:(r + 1) * tq, :d] for r in range(rep)],
                               axis=1).astype(o_ref.dtype)


def nsa_overlap_t(n_cmp, n_sel):
    c_start = np.arange(n_cmp)[None, :] * NSA_CMP_STRIDE
    s_start = np.arange(n_sel)[:, None] * NSA_SEL_BLOCK
    hit = (c_start < s_start + NSA_SEL_BLOCK) & (c_start + NSA_CMP_BLOCK > s_start)
    hit = hit & (np.arange(n_cmp)[None, :] < n_cmp - NSA_CMP_BLOCK // NSA_CMP_STRIDE + 1)
    return jnp.asarray(hit.astype(np.float32), dtype=BF16)


def nsa_attention(qn, ck, cvt, ksl, vslt, kwn, vwnt, glt, tq=256):
    b, s, _ = qn.shape
    g, d = NSA_KV_GROUPS, NSA_HEAD_DIM
    rep = NSA_HEADS // g
    n_cmp = ck.shape[1]
    n_sel = s // NSA_SEL_BLOCK
    nq = s // tq
    ovt = nsa_overlap_t(n_cmp, n_sel)
    width = rep * tq
    chunk = 8 * NSA_SEL_BLOCK
    full3 = lambda bb, gg, i: (bb, 0, 0)
    seq_t = lambda bb, gg, i: (0, bb)
    return pl.pallas_call(
        functools.partial(_nsa_body, tq=tq), grid=(b, g, nq),
        in_specs=[pl.BlockSpec((1, tq, rep * d), lambda bb, gg, i: (bb, i, gg)),
                  pl.BlockSpec((1, n_cmp, g * d), full3), pl.BlockSpec((1, g * d, n_cmp), full3),
                  pl.BlockSpec((1, s, g * d), full3), pl.BlockSpec((g * d, s), seq_t),
                  pl.BlockSpec((1, s, g * d), full3), pl.BlockSpec((g * d, s), seq_t),
                  pl.BlockSpec((n_sel, n_cmp), lambda bb, gg, i: (0, 0)),
                  pl.BlockSpec((glt.shape[0], tq), lambda bb, gg, i: (0, bb * nq + i))],
        out_specs=pl.BlockSpec((1, tq, rep * d), lambda bb, gg, i: (bb, i, gg)),
        out_shape=jax.ShapeDtypeStruct((b, s, g * rep * d), BF16),
        scratch_shapes=[pltpu.VMEM((n_cmp, width), F32), pltpu.VMEM((n_cmp, width), BF16),
                        pltpu.VMEM((n_cmp, width), BF16), pltpu.VMEM((n_sel, tq), F32),
                        pltpu.VMEM((n_sel, width), F32), pltpu.VMEM((chunk, width), F32),
                        pltpu.VMEM((chunk, width), BF16), pltpu.VMEM((chunk, width), F32),
                        pltpu.VMEM((chunk, width), BF16), pltpu.VMEM((NSA_WINDOW + tq, width), F32),
                        pltpu.VMEM((NSA_WINDOW + tq, width), BF16), pltpu.VMEM((d, width), F32),
                        pltpu.VMEM((d, width), F32)],
        compiler_params=_cparams(("parallel", "parallel", "arbitrary")), name="nsa_attention",
    )(qn, ck, cvt, ksl, vslt, kwn, vwnt, ovt, glt)


def _pad_cols(w, n):
    return jnp.pad(w, ((0, 0), (0, n - w.shape[1])))


def _even_layer(x2, b, s, layer_idx, norm_mix, w_in, q_gain, k_gain, lam, subln_gain, conv_w, conv_b,
                dt_bias, a_log, d_skip, ssm_norm_gain, w_out, norm_ffn, w_gate, w_up, w_down):
    nq = DA_HEADS * 2 * DA_HEAD_DIM
    nv = DA_HEADS * DA_V_DIM
    cch = SSM_D_INNER + 2 * SSM_GROUPS * SSM_STATE
    offs = np.cumsum([0, nq, nq, nv, SSM_D_INNER, cch, SSM_HEADS])
    wb = w_in.astype(BF16)
    pieces = [wb[:, offs[k]:offs[k + 1]] for k in range(6)]
    pieces[2] = pieces[2].T
    pieces[5] = _pad_cols(pieces[5], LANES)
    posts = [HeadNorm(q_gain, DA_HEAD_DIM, rope=True, mul=DA_HEAD_DIM ** -0.5 * LOG2E),
             HeadNorm(k_gain, DA_HEAD_DIM, rope=True), None, None, None, None]
    q, k, vt, z, xbc, dt = norm_proj(x2, norm_mix, pieces, [BF16, BF16, BF16, F32, F32, F32], posts, s,
                                     _rope_tables(s, DA_HEAD_DIM), transposed=(2,))
    qn = q.reshape(b, s, nq)
    kn = k.reshape(b, s, nq)
    lam_init = 0.8 - 0.6 * math.exp(-0.3 * layer_idx)
    lf = lam.astype(F32)
    lam_full = jnp.exp(jnp.sum(lf[0] * lf[1])) - jnp.exp(jnp.sum(lf[2] * lf[3])) + lam_init
    a_out = flash_attention(lam_full.reshape(1), [qn], [kn], vt, subln_gain, DA_HEADS, DA_V_DIM,
                            diff=True, out_scale=1.0 - lam_init)
    b_out = ssd_mixer(xbc.reshape(b, s, cch), z.reshape(b, s, SSM_D_INNER), dt.reshape(b, s, LANES),
                      conv_w, conv_b, dt_bias, a_log, d_skip, ssm_norm_gain)
    wo = w_out.astype(BF16)
    x2 = out_proj_residual(x2, a_out.reshape(-1, nv), b_out.reshape(-1, SSM_D_INNER), wo[:nv], wo[nv:])
    return ffn_residual(x2, norm_ffn, w_gate.astype(BF16), w_up.astype(BF16), w_down.astype(BF16))


def _odd_layer(x2, b, s, norm_mix, w_in, q_gain, k_gain, cmp_pos, cmp_w1, cmp_w2, cq_gain, ckv_gain,
               w_uq, w_ukv, qn_gain, qr_gain, kn_gain, kr_gain, w_out, norm_ffn, router_w, w_gate, w_up,
               w_down):
    g, d = NSA_KV_GROUPS, NSA_HEAD_DIM
    nq = NSA_HEADS * d
    nkv = g * d
    sizes = [nq] + [nkv] * 6 + [NSA_HEADS * 3, w_uq.shape[0], w_ukv.shape[0], MLA_ROPE_DIM]
    offs = np.cumsum([0] + sizes)
    wb = w_in.astype(BF16)
    pieces = [wb[:, offs[k]:offs[k + 1]] for k in range(len(sizes))]
    for k in (4, 6):
        pieces[k] = pieces[k].T
    pieces[7] = jnp.pad(pieces[7].T, ((0, 32 - NSA_HEADS * 3), (0, 0)))
    pieces[10] = _pad_cols(pieces[10], LANES)
    tables = _rope_tables(s, d)
    posts = [None] * len(sizes)
    posts[0] = HeadNorm(q_gain, d, rope=True, mul=d ** -0.5 * LOG2E)
    posts[3] = HeadNorm(k_gain[1], d, rope=True)
    posts[5] = HeadNorm(k_gain[2], d, rope=True)
    posts[10] = HeadNorm(kr_gain, MLA_ROPE_DIM, rope=True)
    (q, kc, vc, ksl, vslt, kwn, vwnt, glt, cq, ckv, k_rope) = norm_proj(
        x2, norm_mix, pieces, [BF16, F32, F32, BF16, BF16, BF16, BF16, F32, F32, F32, BF16], posts, s, tables,
        transposed=(4, 6, 7))

    qn = q.reshape(b, s, nq)
    ksl_n = ksl.reshape(b, s, nkv)
    kwn_n = kwn.reshape(b, s, nkv)
    ck = nsa_compress(kc.reshape(b, s, nkv), cmp_pos[0], cmp_w1[0], cmp_w2[0], k_gain[0], s, True)
    cv = nsa_compress(vc.reshape(b, s, nkv), cmp_pos[1], cmp_w1[1], cmp_w2[1], k_gain[0], s, False)
    n_cmp = ck.shape[2]
    ck = ck.transpose(0, 2, 1, 3).reshape(b, n_cmp, nkv).astype(BF16)
    cvt = cv.transpose(0, 1, 3, 2).reshape(b, nkv, n_cmp).astype(BF16)
    c_out = nsa_attention(qn, ck, cvt, ksl_n, vslt, kwn_n, vwnt, glt).reshape(b * s, nq)

    h = MLA_HEADS
    dqk = MLA_NOPE_DIM + MLA_ROPE_DIM
    wq = w_uq.astype(BF16).reshape(-1, h, dqk)
    wq_nope = wq[:, :, :MLA_NOPE_DIM].reshape(-1, h * MLA_NOPE_DIM)
    wq_rope = jnp.pad(wq[:, :, MLA_NOPE_DIM:], ((0, 0), (0, 0), (0, LANES - MLA_ROPE_DIM)))
    wq_rope = wq_rope.reshape(-1, h * LANES)
    wkv = w_ukv.astype(BF16).reshape(-1, h, MLA_NOPE_DIM + MLA_V_DIM)
    wk_nope = wkv[:, :, :MLA_NOPE_DIM].reshape(-1, h * MLA_NOPE_DIM)
    wv = wkv[:, :, MLA_NOPE_DIM:].reshape(-1, h * MLA_V_DIM)
    q_mul = dqk ** -0.5 * LOG2E
    q_nope, q_rope = norm_proj(
        cq, cq_gain, [wq_nope, wq_rope], [BF16, BF16],
        [HeadNorm(qn_gain, MLA_NOPE_DIM, mul=q_mul), HeadNorm(qr_gain, MLA_ROPE_DIM, rope=True, mul=q_mul)],
        s, tables)
    k_nope, vt = norm_proj(ckv, ckv_gain, [wk_nope, wv.T], [BF16, BF16],
                           [HeadNorm(kn_gain, MLA_NOPE_DIM), None], transposed=(1,))
    shp = lambda t: t.reshape(b, s, t.shape[-1])
    d_out = flash_attention(jnp.zeros((1,), F32), [shp(q_nope), shp(q_rope)], [shp(k_nope), shp(k_rope)],
                            vt, jnp.ones((MLA_V_DIM,), F32), h, MLA_V_DIM, diff=False)

    wo = w_out.astype(BF16)
    x2 = out_proj_residual(x2, c_out, d_out.reshape(b * s, h * MLA_V_DIM), wo[:nq], wo[nq:])
    h, route, counts = router(x2, norm_ffn, router_w)
    return moe_residual(x2, h, route, counts, w_gate.astype(BF16), w_up.astype(BF16), w_down.astype(BF16))


def kernel(x, ev_norm_mix, ev_w_in, da_q_gain, da_k_gain, da_lambda, da_subln_gain, ssm_conv_w, ssm_conv_b, ssm_dt_bias, ssm_a_log, ssm_d, ssm_norm_gain, ev_w_out, ev_norm_ffn, ffn_w_gate, ffn_w_up, ffn_w_down, od_norm_mix, od_w_in, nsa_q_gain, nsa_k_gain, nsa_cmp_pos, nsa_cmp_w1, nsa_cmp_w2, mla_cq_gain, mla_ckv_gain, mla_w_uq, mla_w_ukv, mla_qn_gain, mla_qr_gain, mla_kn_gain, mla_kr_gain, od_w_out, od_norm_ffn, moe_router, moe_w_gate, moe_w_up, moe_w_down):
    b, s, d = x.shape
    x2 = x.reshape(b * s, d)
    depth = ev_norm_mix.shape[0] + od_norm_mix.shape[0]
    for layer in range(depth):
        i = layer // 2
        if layer % 2 == 0:
            x2 = _even_layer(x2, b, s, layer, ev_norm_mix[i], ev_w_in[i], da_q_gain[i], da_k_gain[i],
                             da_lambda[i], da_subln_gain[i], ssm_conv_w[i], ssm_conv_b[i],
                             ssm_dt_bias[i], ssm_a_log[i], ssm_d[i], ssm_norm_gain[i], ev_w_out[i],
                             ev_norm_ffn[i], ffn_w_gate[i], ffn_w_up[i], ffn_w_down[i])
        else:
            x2 = _odd_layer(x2, b, s, od_norm_mix[i], od_w_in[i], nsa_q_gain[i], nsa_k_gain[i],
                            nsa_cmp_pos[i], nsa_cmp_w1[i], nsa_cmp_w2[i], mla_cq_gain[i],
                            mla_ckv_gain[i], mla_w_uq[i], mla_w_ukv[i], mla_qn_gain[i], mla_qr_gain[i],
                            mla_kn_gain[i], mla_kr_gain[i], od_w_out[i], od_norm_ffn[i], moe_router[i],
                            moe_w_gate[i], moe_w_up[i], moe_w_down[i])
    return x2.reshape(b, s, d)
```

```python
import functools
import math

import numpy as np
import jax
import jax.numpy as jnp
from jax import lax
from jax.experimental import pallas as pl
from jax.experimental.pallas import tpu as pltpu

F32 = jnp.float32
BF16 = jnp.bfloat16

ROPE_THETA = 10000.0
NORM_EPS = 1e-6
NEG_INF = -1e30
FORCE_SCORE = 1e6
LOG2E = 1.4426950408889634

DA_HEADS = 4
DA_HEAD_DIM = 64
DA_V_DIM = 2 * DA_HEAD_DIM
SSM_HEADS = 8
SSM_HEAD_DIM = 64
SSM_D_INNER = SSM_HEADS * SSM_HEAD_DIM
SSM_GROUPS = 2
SSM_STATE = 128
SSM_CONV = 4
SSM_CHUNK = 256
NSA_HEADS = 8
NSA_KV_GROUPS = 2
NSA_HEAD_DIM = 64
NSA_CMP_BLOCK = 32
NSA_CMP_STRIDE = 16
NSA_SEL_BLOCK = 64
NSA_TOP_N = 16
NSA_WINDOW = 512
MLA_HEADS = 4
MLA_NOPE_DIM = 128
MLA_ROPE_DIM = 64
MLA_V_DIM = 128
N_EXPERTS = 8

LANES = 128
SUBLANES = 8
VMEM_LIMIT = 48 * 1024 * 1024
MOE_VMEM_LIMIT = 58 * 1024 * 1024

NT_DIMS = (((1,), (1,)), ((), ()))


def _cparams(semantics):
    return pltpu.CompilerParams(dimension_semantics=semantics, vmem_limit_bytes=VMEM_LIMIT)


def _dot(a, b):
    return jnp.dot(a, b, preferred_element_type=F32)


def _dot_nt(a, b):
    return lax.dot_general(a, b, NT_DIMS, preferred_element_type=F32)


def _split_bf16(x, parts):
    out = []
    for _ in range(parts):
        hi = x.astype(BF16)
        out.append(hi)
        x = x - hi.astype(F32)
    return out


def _fold_rows(x):
    return jnp.sum(x.reshape(x.shape[0] // SUBLANES, SUBLANES, x.shape[1]), axis=0)


ONES_ROWS = 16


def _with_ones_rows(vt):
    return jnp.concatenate([vt, jnp.ones((ONES_ROWS, vt.shape[1]), vt.dtype)], axis=0)


def _sigmoid(x):
    return 1.0 / (1.0 + jnp.exp(-x))


def _silu(x):
    return x * _sigmoid(x)


def _softplus(x):
    return jnp.maximum(x, 0.0) + jnp.log(1.0 + jnp.exp(-jnp.abs(x)))


def _rms(x, gain):
    ms = jnp.mean(x * x, axis=-1, keepdims=True)
    return x * lax.rsqrt(ms + NORM_EPS) * gain


class HeadNorm:
    def __init__(self, gain, hd, rope=False, mul=1.0):
        self.gain, self.hd, self.rope, self.mul = gain, hd, rope, mul


def _head_norm(y, gain, bd, cos_ref, sin_ref, post):
    n = y.shape[1]
    hd = post.hd
    hi, lo = _split_bf16(y * y, 2)
    ss = _dot(hi, bd) + _dot(lo, bd)
    yn = y * lax.rsqrt(ss * (1.0 / hd) + NORM_EPS) * gain
    if post.rope:
        reps = n // LANES
        cos = jnp.concatenate([cos_ref[...]] * reps, axis=1) if reps > 1 else cos_ref[...]
        sin = jnp.concatenate([sin_ref[...]] * reps, axis=1) if reps > 1 else sin_ref[...]
        lane = lax.broadcasted_iota(jnp.int32, yn.shape, 1)
        first_half = (lane & (hd - 1)) < (hd // 2)
        partner = jnp.where(first_half, pltpu.roll(yn, n - hd // 2, 1), pltpu.roll(yn, hd // 2, 1))
        yn = yn * cos + partner * sin
    if post.mul != 1.0:
        yn = yn * post.mul
    return yn


def _norm_proj_body(x_ref, g_ref, *refs, posts, use_rope, transposed):
    if use_rope:
        cos_ref, sin_ref = refs[0], refs[1]
        refs = refs[2:]
    else:
        cos_ref = sin_ref = None
    n_out = len(posts)
    n_aux = 2 * sum(p is not None for p in posts)
    w_refs, aux, o_refs = refs[:n_out], refs[n_out:n_out + n_aux], refs[n_out + n_aux:]
    h = _rms(x_ref[...], g_ref[...]).astype(BF16)
    a = 0
    for k, (w_ref, o_ref, post) in enumerate(zip(w_refs, o_refs, posts)):
        if k in transposed:
            o_ref[...] = _dot_nt(w_ref[...], h).astype(o_ref.dtype)
            continue
        y = _dot(h, w_ref[...])
        if post is not None:
            y = _head_norm(y, aux[a][...], aux[a + 1][...], cos_ref, sin_ref, post)
            a += 2
        o_ref[...] = y.astype(o_ref.dtype)


def norm_proj(x2, gain, weights, out_dtypes, posts=None, seq=None, rope_tables=None, transposed=(), tm=512):
    t, d = x2.shape
    posts = posts or [None] * len(weights)
    transposed = frozenset(transposed)
    use_rope = any(p is not None and p.rope for p in posts)
    const = lambda i: (0, 0)
    args = [x2, gain.reshape(1, d).astype(F32)]
    in_specs = [pl.BlockSpec((tm, d), lambda i: (i, 0)), pl.BlockSpec((1, d), const)]
    if use_rope:
        per_seq = seq // tm
        args += list(rope_tables)
        in_specs += [pl.BlockSpec((tm, LANES), lambda i: (i % per_seq, 0))] * 2
    args += list(weights)
    in_specs += [pl.BlockSpec(w.shape, const) for w in weights]
    for w, p in zip(weights, posts):
        if p is not None:
            n = w.shape[1]
            args += [jnp.tile(p.gain.astype(F32), n // p.hd).reshape(1, n), _block_diag_ones(n, p.hd)]
            in_specs += [pl.BlockSpec((1, n), const), pl.BlockSpec((n, n), const)]
    out_specs, out_shape = [], []
    for k, (w, dt) in enumerate(zip(weights, out_dtypes)):
        if k in transposed:
            out_specs.append(pl.BlockSpec((w.shape[0], tm), lambda i: (0, i)))
            out_shape.append(jax.ShapeDtypeStruct((w.shape[0], t), dt))
        else:
            out_specs.append(pl.BlockSpec((tm, w.shape[1]), lambda i: (i, 0)))
            out_shape.append(jax.ShapeDtypeStruct((t, w.shape[1]), dt))
    return pl.pallas_call(
        functools.partial(_norm_proj_body, posts=tuple(posts), use_rope=use_rope, transposed=transposed),
        grid=(t // tm,), in_specs=in_specs, out_specs=out_specs, out_shape=out_shape,
        compiler_params=_cparams(("parallel",)), name="norm_proj",
    )(*args)


def _block_diag_ones(n, hd):
    idx = np.arange(n) // hd
    return jnp.asarray((idx[:, None] == idx[None, :]).astype(np.float32), dtype=BF16)


def _rope_tables(seq, hd):
    inv_freq = 1.0 / (ROPE_THETA ** (jnp.arange(0, hd, 2, dtype=F32) / hd))
    ang = jnp.arange(seq, dtype=F32)[:, None] * inv_freq[None, :]
    cos, sin = jnp.cos(ang), jnp.sin(ang)
    reps = LANES // hd
    cos_t = jnp.tile(jnp.concatenate([cos, cos], axis=1), (1, reps))
    sin_t = jnp.tile(jnp.concatenate([-sin, sin], axis=1), (1, reps))
    return cos_t, sin_t


def _flash_body(lam_ref, *refs, n_qk, diff, out_scale, sub):
    q_refs = refs[:n_qk]
    k_refs = refs[n_qk:2 * n_qk]
    vt_ref, gain_ref, o_ref, m_ref, l_ref, acc_ref, s0_ref, s1_ref, p0_ref, p1_ref = refs[2 * n_qk:]
    i = pl.program_id(2)
    n_sm = 2 if diff else 1
    _, tk, tq = s0_ref.shape
    s_slots = (s0_ref, s1_ref)
    p_slots = (p0_ref, p1_ref)

    m_ref[...] = jnp.full(m_ref.shape, NEG_INF, F32)
    l_ref[...] = jnp.zeros(l_ref.shape, F32)
    acc_ref[...] = jnp.zeros(acc_ref.shape, F32)

    qs = [r[0] for r in q_refs]
    q = qs[0] if n_qk == 1 else jnp.concatenate(qs, axis=1)
    if diff:
        lane = lax.broadcasted_iota(jnp.int32, q.shape, 1)
        half = q.shape[1] // 2
        zero = jnp.zeros_like(q)
        q_parts = [jnp.where(lane < half, q, zero), jnp.where(lane >= half, q, zero)]
    else:
        q_parts = [q]

    def scores(c, slot, key_offset=None):
        rows = pl.ds(pl.multiple_of(c * tk, tk), tk)
        ks = [r[0, rows, :] for r in k_refs]
        k = ks[0] if n_qk == 1 else jnp.concatenate(ks, axis=1)
        for sm in range(n_sm):
            s = _dot_nt(k, q_parts[sm])
            if key_offset is not None:
                row = lax.broadcasted_iota(jnp.int32, s.shape, 0)
                col = lax.broadcasted_iota(jnp.int32, s.shape, 1)
                s = jnp.where(row + key_offset <= col, s, NEG_INF)
            s_slots[slot][sm] = s

    def update(c, slot):
        vt = vt_ref[:, pl.ds(pl.multiple_of(c * tk, tk), tk)]
        for sm in range(n_sm):
            s_ref, p_ref = s_slots[slot], p_slots[slot]
            m_prev = m_ref[sm]
            m_new = jnp.maximum(m_prev, jnp.max(s_ref[sm], axis=0, keepdims=True))
            m_ref[sm] = m_new
            alpha = jnp.exp2(m_prev - m_new)
            lpart = jnp.zeros((SUBLANES, tq), F32)
            for r in range(tk // sub):
                p = jnp.exp2(s_ref[sm, r * sub:(r + 1) * sub, :] - m_new)
                lpart = lpart + _fold_rows(p)
                p_ref[sm, r * sub:(r + 1) * sub, :] = p.astype(BF16)
            l_ref[sm] = alpha * l_ref[sm] + jnp.sum(lpart, axis=0, keepdims=True)
            acc_ref[sm] = alpha * acc_ref[sm] + _dot(vt, p_ref[sm])

    if tq == tk:
        n_chunks = i + 1
        last_past = jnp.maximum(i - 1, 0)

        def chunk_at(j):
            return jnp.where(j == 0, i, j - 1)

        scores(i, 0, key_offset=0)

        def pair(k2, carry):
            scores(jnp.minimum(2 * k2, last_past), 1)
            update(chunk_at(2 * k2), 0)
            scores(jnp.minimum(2 * k2 + 1, last_past), 0)
            update(2 * k2, 1)
            return carry

        lax.fori_loop(0, n_chunks // 2, pair, 0)

        @pl.when(n_chunks % 2 == 1)
        def _():
            update(chunk_at(n_chunks - 1), 0)
    else:
        n_past = 2 * i
        last_past = jnp.maximum(n_past - 1, 0)
        scores(n_past, 0, key_offset=0)
        scores(n_past + 1, 1, key_offset=tk)
        update(n_past, 0)
        scores(0, 0)
        update(n_past + 1, 1)

        def pair(k2, carry):
            scores(2 * k2 + 1, 1)
            update(2 * k2, 0)
            scores(jnp.minimum(2 * k2 + 2, last_past), 0)
            update(2 * k2 + 1, 1)
            return carry

        lax.fori_loop(0, i, pair, 0)

    o = acc_ref[0] * (1.0 / l_ref[0])
    if diff:
        o = o - acc_ref[1] * (lam_ref[0] / l_ref[1])
        ms = jnp.mean(o * o, axis=0, keepdims=True)
        o = o * lax.rsqrt(ms + NORM_EPS) * gain_ref[...] * out_scale
    o_ref[0] = o.T.astype(o_ref.dtype)


def flash_attention(lam, qs, ks, vt, gain, n_heads, dv, *, diff, out_scale=1.0, tk=512):
    b, s, _ = qs[0].shape
    tile = tk if diff else 2 * tk
    sub = 64 if diff else 32
    nt = s // tile
    n_qk = len(qs)
    in_specs = [pl.BlockSpec(memory_space=pltpu.SMEM)]
    for q in qs:
        w = q.shape[2] // n_heads
        in_specs.append(pl.BlockSpec((1, tile, w), lambda bb, h, i: (bb, i, h)))
    for q, k in zip(qs, ks):
        w = q.shape[2] // n_heads
        if k.shape[2] == w:
            in_specs.append(pl.BlockSpec((1, s, w), lambda bb, h, i: (bb, 0, 0)))
        else:
            in_specs.append(pl.BlockSpec((1, s, w), lambda bb, h, i: (bb, 0, h)))
    in_specs.append(pl.BlockSpec((dv, s), lambda bb, h, i: (h, bb)))
    in_specs.append(pl.BlockSpec((dv, 1), lambda bb, h, i: (0, 0)))
    n_sm = 2 if diff else 1
    return pl.pallas_call(
        functools.partial(_flash_body, n_qk=n_qk, diff=diff, out_scale=out_scale, sub=sub),
        grid=(b, n_heads, nt), in_specs=in_specs,
        out_specs=pl.BlockSpec((1, tile, dv), lambda bb, h, i: (bb, i, h)),
        out_shape=jax.ShapeDtypeStruct((b, s, n_heads * dv), BF16),
        scratch_shapes=[pltpu.VMEM((n_sm, 1, tile), F32), pltpu.VMEM((n_sm, 1, tile), F32),
                        pltpu.VMEM((n_sm, dv, tile), F32),
                        pltpu.VMEM((n_sm, tk, tile), F32), pltpu.VMEM((n_sm, tk, tile), F32),
                        pltpu.VMEM((n_sm, tk, tile), BF16), pltpu.VMEM((n_sm, tk, tile), BF16)],
        compiler_params=_cparams(("parallel", "parallel", "arbitrary")),
        name="flash_diff" if diff else "flash_plain",
    )(lam, *qs, *ks, vt, gain.reshape(dv, 1).astype(F32))


def _ssd_body(xbc_ref, z_ref, dt_ref, dtt_ref, cw_ref, cb_ref, dtb_ref, dtbt_ref, al_ref, alt_ref,
              dsk_ref, ng_ref, o_ref, xpad_ref, state_ref):
    chunk = xbc_ref.shape[1]
    d_in = z_ref.shape[2]
    gn = SSM_GROUPS * SSM_STATE
    c = pl.program_id(1)

    @pl.when(c == 0)
    def _():
        xpad_ref[0:8, :] = jnp.zeros((8, xpad_ref.shape[1]), F32)
        state_ref[...] = jnp.zeros(state_ref.shape, F32)

    xpad_ref[8:8 + chunk, :] = xbc_ref[0]
    conv = cb_ref[...]
    for w in range(SSM_CONV):
        conv = conv + cw_ref[w:w + 1, :] * xpad_ref[pl.ds(8 - (SSM_CONV - 1) + w, chunk), :]
    xpad_ref[0:8, :] = xpad_ref[chunk:chunk + 8, :]
    u = _silu(conv)
    xs = u[:, :d_in]
    bmat = u[:, d_in:d_in + gn]
    cmat = u[:, d_in + gn:]

    dt = _softplus(dt_ref[0] + dtb_ref[...])
    ad = dt * (-jnp.exp(al_ref[...]))
    dtt = _softplus(dtt_ref[0] + dtbt_ref[...])
    adt = dtt * (-jnp.exp(alt_ref[...]))
    row = lax.broadcasted_iota(jnp.int32, (chunk, chunk), 0)
    col = lax.broadcasted_iota(jnp.int32, (chunk, chunk), 1)
    lower = row >= col
    tril = jnp.where(lower, 1.0, 0.0).astype(BF16)
    triu = jnp.where(row <= col, 1.0, 0.0).astype(BF16)
    cs = sum(_dot(tril, part) for part in _split_bf16(ad, 3))
    cst = sum(_dot(part, triu) for part in _split_bf16(adt, 3))

    heads_per_group = SSM_HEADS // SSM_GROUPS
    dsk = dsk_ref[...]
    ys = []
    for g in range(SSM_GROUPS):
        bg = bmat[:, g * SSM_STATE:(g + 1) * SSM_STATE]
        cg = cmat[:, g * SSM_STATE:(g + 1) * SSM_STATE].astype(BF16)
        cb = _dot_nt(cg, bg.astype(BF16))
        bgt = bg.T.astype(BF16)
        for r in range(heads_per_group):
            h = g * heads_per_group + r
            ccol = cs[:, h:h + 1]
            crow = cst[h:h + 1, :]
            decay = jnp.exp(jnp.where(lower, ccol - crow, NEG_INF))
            x_h = xs[:, h * SSM_HEAD_DIM:(h + 1) * SSM_HEAD_DIM]
            xdt = x_h * dt[:, h:h + 1]
            y = _dot((cb * decay).astype(BF16), xdt.astype(BF16))
            st = state_ref[h]
            y = y + _dot(cg, st.astype(BF16)) * jnp.exp(ccol)
            last = cst[h:h + 1, chunk - 1:chunk]
            to_end = jnp.exp(last - ccol)
            state_ref[h] = st * jnp.exp(last) + _dot(bgt, (xdt * to_end).astype(BF16))
            ys.append(y + x_h * dsk[:, h * SSM_HEAD_DIM:(h + 1) * SSM_HEAD_DIM])

    y = jnp.concatenate(ys, axis=1) * _silu(z_ref[0])
    gw = d_in // SSM_GROUPS
    for g in range(SSM_GROUPS):
        seg = y[:, g * gw:(g + 1) * gw]
        o_ref[0, :, g * gw:(g + 1) * gw] = _rms(seg, ng_ref[:, g * gw:(g + 1) * gw]).astype(o_ref.dtype)


def ssd_mixer(xbc, z, dt_raw, conv_w, conv_b, dt_bias, a_log, d_skip, norm_gain):
    b, s, cch = xbc.shape
    d_in = z.shape[2]
    nc = s // SSM_CHUNK
    hpad = dt_raw.shape[2]
    dtt = jnp.transpose(dt_raw[:, :, :SSM_HEADS], (0, 2, 1))

    def lane_pad(v):
        return jnp.pad(v.astype(F32), (0, hpad - SSM_HEADS)).reshape(1, hpad)

    args = (xbc, z, dt_raw, dtt, conv_w.astype(F32), conv_b.reshape(1, cch).astype(F32),
            lane_pad(dt_bias), dt_bias.reshape(SSM_HEADS, 1).astype(F32),
            lane_pad(a_log), a_log.reshape(SSM_HEADS, 1).astype(F32),
            jnp.repeat(d_skip.astype(F32), SSM_HEAD_DIM).reshape(1, d_in),
            norm_gain.reshape(1, d_in).astype(F32))
    const = lambda bb, c: (0, 0)
    in_specs = [pl.BlockSpec((1, SSM_CHUNK, cch), lambda bb, c: (bb, c, 0)),
                pl.BlockSpec((1, SSM_CHUNK, d_in), lambda bb, c: (bb, c, 0)),
                pl.BlockSpec((1, SSM_CHUNK, hpad), lambda bb, c: (bb, c, 0)),
                pl.BlockSpec((1, SSM_HEADS, SSM_CHUNK), lambda bb, c: (bb, 0, c)),
                pl.BlockSpec((SSM_CONV, cch), const), pl.BlockSpec((1, cch), const),
                pl.BlockSpec((1, hpad), const), pl.BlockSpec((SSM_HEADS, 1), const),
                pl.BlockSpec((1, hpad), const), pl.BlockSpec((SSM_HEADS, 1), const),
                pl.BlockSpec((1, d_in), const), pl.BlockSpec((1, d_in), const)]
    return pl.pallas_call(
        _ssd_body, grid=(b, nc), in_specs=in_specs,
        out_specs=pl.BlockSpec((1, SSM_CHUNK, d_in), lambda bb, c: (bb, c, 0)),
        out_shape=jax.ShapeDtypeStruct((b, s, d_in), BF16),
        scratch_shapes=[pltpu.VMEM((SSM_CHUNK + 8, cch), F32),
                        pltpu.VMEM((SSM_HEADS, SSM_STATE, SSM_HEAD_DIM), F32)],
        compiler_params=_cparams(("parallel", "arbitrary")), name="ssd_mixer",
    )(*args)


def _out_proj_body(x_ref, a_ref, b_ref, wa_ref, wb_ref, o_ref):
    o_ref[...] = x_ref[...] + _dot(a_ref[...], wa_ref[...]) + _dot(b_ref[...], wb_ref[...])


def out_proj_residual(x2, a, bm, wa, wb, tm=512):
    t, d = x2.shape
    return pl.pallas_call(
        _out_proj_body, grid=(t // tm,),
        in_specs=[pl.BlockSpec((tm, d), lambda i: (i, 0)),
                  pl.BlockSpec((tm, a.shape[1]), lambda i: (i, 0)),
                  pl.BlockSpec((tm, bm.shape[1]), lambda i: (i, 0)),
                  pl.BlockSpec(wa.shape, lambda i: (0, 0)),
                  pl.BlockSpec(wb.shape, lambda i: (0, 0))],
        out_specs=pl.BlockSpec((tm, d), lambda i: (i, 0)),
        out_shape=jax.ShapeDtypeStruct((t, d), F32),
        compiler_params=_cparams(("parallel",)), name="out_proj",
    )(x2, a, bm, wa, wb)


def _ffn_body(x_ref, g_ref, wg_ref, wu_ref, wd_ref, o_ref, h_ref):
    f = pl.program_id(1)

    @pl.when(f == 0)
    def _():
        x = x_ref[...]
        h_ref[...] = _rms(x, g_ref[...]).astype(BF16)
        o_ref[...] = x

    half = h_ref.shape[0] // 2
    for r in (slice(0, half), slice(half, 2 * half)):
        h = h_ref[r, :]
        act = (_silu(_dot(h, wg_ref[...])) * _dot(h, wu_ref[...])).astype(BF16)
        o_ref[r, :] += _dot(act, wd_ref[...])


def ffn_residual(x2, gain, w_gate, w_up, w_down, tm=1024, tf=1408):
    t, d = x2.shape
    d_ff = w_gate.shape[1]
    return pl.pallas_call(
        _ffn_body, grid=(t // tm, d_ff // tf),
        in_specs=[pl.BlockSpec((tm, d), lambda i, f: (i, 0)),
                  pl.BlockSpec((1, d), lambda i, f: (0, 0)),
                  pl.BlockSpec((d, tf), lambda i, f: (0, f)),
                  pl.BlockSpec((d, tf), lambda i, f: (0, f)),
                  pl.BlockSpec((tf, d), lambda i, f: (f, 0))],
        out_specs=pl.BlockSpec((tm, d), lambda i, f: (i, 0)),
        out_shape=jax.ShapeDtypeStruct((t, d), F32),
        scratch_shapes=[pltpu.VMEM((tm, d), BF16)],
        compiler_params=_cparams(("parallel", "arbitrary")), name="ffn",
    )(x2, gain.reshape(1, d).astype(F32), w_gate, w_up, w_down)


MOE_ROWS = 256
ROUTE_IDX = 0
ROUTE_W = 2
ROUTE_RANK = 4


def _moe_ffn_body(block_expert_ref, n_used_ref, xs_ref, wg_ref, wu_ref, wd_ref, o_ref):
    i = pl.program_id(0)

    @pl.when(i < n_used_ref[0])
    def _():
        x = xs_ref[...]
        act = (_silu(_dot(x, wg_ref[0])) * _dot(x, wu_ref[0])).astype(BF16)
        o_ref[...] = _dot(act, wd_ref[0]).astype(o_ref.dtype)

    @pl.when(i >= n_used_ref[0])
    def _():
        o_ref[...] = jnp.zeros(o_ref.shape, o_ref.dtype)


def moe_expert_ffn(xs, block_expert, n_used, w_gate, w_up, w_down):
    p, d = xs.shape
    d_ff = w_gate.shape[2]
    rows = MOE_ROWS
    grid_spec = pltpu.PrefetchScalarGridSpec(
        num_scalar_prefetch=2, grid=(p // rows,),
        in_specs=[pl.BlockSpec((rows, d), lambda i, be, nu: (i, 0)),
                  pl.BlockSpec((1, d, d_ff), lambda i, be, nu: (be[i], 0, 0)),
                  pl.BlockSpec((1, d, d_ff), lambda i, be, nu: (be[i], 0, 0)),
                  pl.BlockSpec((1, d_ff, d), lambda i, be, nu: (be[i], 0, 0))],
        out_specs=pl.BlockSpec((rows, d), lambda i, be, nu: (i, 0)))
    return pl.pallas_call(
        _moe_ffn_body, grid_spec=grid_spec, out_shape=jax.ShapeDtypeStruct((p, d), BF16),
        compiler_params=pltpu.CompilerParams(
            dimension_semantics=("arbitrary",), vmem_limit_bytes=MOE_VMEM_LIMIT),
        name="moe_expert_ffn",
    )(block_expert, n_used, xs, w_gate, w_up, w_down)


def _moe_combine_body(x_ref, y0_ref, y1_ref, route_ref, o_ref):
    route = route_ref[...]
    lane = lax.broadcasted_iota(jnp.int32, route.shape, 1)
    w0 = jnp.sum(jnp.where(lane == ROUTE_W, route, 0.0), axis=1, keepdims=True)
    w1 = jnp.sum(jnp.where(lane == ROUTE_W + 1, route, 0.0), axis=1, keepdims=True)
    o_ref[...] = x_ref[...] + w0 * y0_ref[...].astype(F32) + w1 * y1_ref[...].astype(F32)


def moe_combine(x2, y0, y1, route, tm=512):
    t, d = x2.shape
    row = lambda i: (i, 0)
    return pl.pallas_call(
        _moe_combine_body, grid=(t // tm,),
        in_specs=[pl.BlockSpec((tm, d), row), pl.BlockSpec((tm, d), row), pl.BlockSpec((tm, d), row),
                  pl.BlockSpec((tm, LANES), row)],
        out_specs=pl.BlockSpec((tm, d), row),
        out_shape=jax.ShapeDtypeStruct((t, d), F32),
        compiler_params=_cparams(("parallel",)), name="moe_combine",
    )(x2, y0, y1, route)


def moe_residual(x2, h, route, counts, w_gate, w_up, w_down):
    t, d = x2.shape
    n_e = w_gate.shape[0]
    rows = MOE_ROWS
    expert = route[:, ROUTE_IDX:ROUTE_IDX + 2].astype(jnp.int32).reshape(-1)
    rank = route[:, ROUTE_RANK:ROUTE_RANK + 2].astype(jnp.int32).reshape(-1)
    count = counts[0, :n_e].astype(jnp.int32)
    padded = (count + rows - 1) // rows * rows
    ends = jnp.cumsum(padded)
    starts = ends - padded
    own = expert[:, None] == jnp.arange(n_e, dtype=jnp.int32)[None, :]
    slot = jnp.sum(jnp.where(own, starts[None, :], 0), axis=1) + rank
    p_rows = 2 * t + n_e * rows
    block_start = jnp.arange(p_rows // rows, dtype=jnp.int32) * rows
    block_expert = jnp.minimum(jnp.sum(block_start[:, None] >= ends[None, :], axis=1), n_e - 1).astype(jnp.int32)
    n_used = (ends[-1] // rows).astype(jnp.int32).reshape(1)
    copy_order = jnp.argsort(slot).astype(jnp.int32)
    within = (block_start - starts[block_expert])[:, None] + jnp.arange(rows, dtype=jnp.int32)[None, :]
    dense = (jnp.cumsum(count) - count)[block_expert][:, None] + within
    real = within < count[block_expert][:, None]
    picked = copy_order.at[jnp.clip(dense, 0, 2 * t - 1)].get(mode="promise_in_bounds")
    token_of_slot = jnp.where(real, picked // 2, 0).reshape(p_rows)

    take_rows = lambda a, idx: a.at[idx].get(mode="promise_in_bounds")
    xs = take_rows(h, token_of_slot)
    ys = moe_expert_ffn(xs, block_expert, n_used, w_gate, w_up, w_down)
    slot2 = slot.reshape(t, 2)
    return moe_combine(x2, take_rows(ys, slot2[:, 0]), take_rows(ys, slot2[:, 1]), route)


def _router_body(x_ref, g_ref, r_ref, h_ref, o_ref, count_ref, run_ref):
    @pl.when(pl.program_id(0) == 0)
    def _():
        run_ref[...] = jnp.zeros(run_ref.shape, F32)

    h = _rms(x_ref[...], g_ref[...])
    h_ref[...] = h.astype(h_ref.dtype)
    logits = jnp.dot(h, r_ref[...], precision=lax.Precision.HIGHEST, preferred_element_type=F32)
    lane = lax.broadcasted_iota(jnp.int32, logits.shape, 1).astype(F32)
    low = jnp.float32(-3.0e38)
    logits = jnp.where(lane < N_EXPERTS, logits, low)
    m1 = jnp.max(logits, axis=1, keepdims=True)
    i1 = jnp.min(jnp.where(logits == m1, lane, float(LANES)), axis=1, keepdims=True)
    rest = jnp.where(lane == i1, low, logits)
    m2 = jnp.max(rest, axis=1, keepdims=True)
    i2 = jnp.min(jnp.where(rest == m2, lane, float(LANES)), axis=1, keepdims=True)
    ex = jnp.exp(m2 - m1)
    w1 = 1.0 / (1.0 + ex)
    w2 = ex / (1.0 + ex)
    tm = logits.shape[0]
    routed = jnp.where((lane == i1) | (lane == i2), 1.0, 0.0)
    row = lax.broadcasted_iota(jnp.int32, (tm, tm), 0)
    col = lax.broadcasted_iota(jnp.int32, (tm, tm), 1)
    before = jnp.where(col < row, 1.0, 0.0).astype(BF16)
    rank = run_ref[0:1, :] + _dot(before, routed.astype(BF16))
    r1 = jnp.sum(jnp.where(lane == i1, rank, 0.0), axis=1, keepdims=True)
    r2 = jnp.sum(jnp.where(lane == i2, rank, 0.0), axis=1, keepdims=True)
    fields = ((ROUTE_IDX, i1), (ROUTE_IDX + 1, i2), (ROUTE_W, w1), (ROUTE_W + 1, w2),
              (ROUTE_RANK, r1), (ROUTE_RANK + 1, r2))
    out = jnp.zeros(logits.shape, F32)
    for pos, val in fields:
        out = jnp.where(lane == pos, val, out)
    o_ref[...] = out
    run_ref[...] = run_ref[...] + jnp.sum(routed, axis=0, keepdims=True)
    count_ref[...] = run_ref[...]


def router(x2, gain, router_w, tm=512):
    t, d = x2.shape
    r_pad = jnp.pad(router_w.astype(F32), ((0, 0), (0, LANES - router_w.shape[1])))
    return pl.pallas_call(
        _router_body, grid=(t // tm,),
        in_specs=[pl.BlockSpec((tm, d), lambda i: (i, 0)),
                  pl.BlockSpec((1, d), lambda i: (0, 0)),
                  pl.BlockSpec((d, LANES), lambda i: (0, 0))],
        out_specs=[pl.BlockSpec((tm, d), lambda i: (i, 0)), pl.BlockSpec((tm, LANES), lambda i: (i, 0)),
                   pl.BlockSpec((SUBLANES, LANES), lambda i: (0, 0))],
        out_shape=[jax.ShapeDtypeStruct((t, d), BF16), jax.ShapeDtypeStruct((t, LANES), F32),
                   jax.ShapeDtypeStruct((SUBLANES, LANES), F32)],
        scratch_shapes=[pltpu.VMEM((SUBLANES, LANES), F32)],
        compiler_params=_cparams(("arbitrary",)), name="router",
    )(x2, gain.reshape(1, d).astype(F32), r_pad)


def _compress_body(ch_ref, nx_ref, pos_ref, w1_ref, w2_ref, gain_ref, cos_ref, sin_ref, rot_ref, o_ref,
                   *, is_key):
    a = _dot((ch_ref[0] + pos_ref[0]).astype(BF16), w1_ref[0, 0])
    a = a + _dot((nx_ref[0] + pos_ref[1]).astype(BF16), w1_ref[0, 1])
    out = _dot(_silu(a).astype(BF16), w2_ref[...])
    if is_key:
        out = _rms(out, gain_ref[...])
        hi, lo = _split_bf16(out, 2)
        partner = _dot(hi, rot_ref[...]) + _dot(lo, rot_ref[...])
        out = out * cos_ref[...] + partner * sin_ref[...]
    o_ref[0, 0] = out


def nsa_compress(t, pos, w1, w2, gain, seq, is_key):
    b, s, _ = t.shape
    g, d = NSA_KV_GROUPS, NSA_HEAD_DIM
    n_ch = s // NSA_CMP_STRIDE
    half = NSA_CMP_STRIDE * g * d
    ch = t.reshape(b, n_ch, half)
    nxt = jnp.concatenate([ch[:, 1:], jnp.zeros((b, 1, half), F32)], axis=1)
    pos2 = jnp.broadcast_to(pos.astype(F32).reshape(2, NSA_CMP_STRIDE, 1, d),
                            (2, NSA_CMP_STRIDE, g, d)).reshape(2, 1, half)
    w1r = w1.astype(BF16).reshape(2, NSA_CMP_STRIDE, 1, d, d)
    own = (jnp.arange(g)[:, None] == jnp.arange(g)[None, :]).reshape(g, 1, 1, g, 1, 1)
    w1s = jnp.where(own, w1r[None], jnp.zeros((), BF16)).reshape(g, 2, half, d)
    cmp_end = jnp.arange(n_ch) * NSA_CMP_STRIDE + NSA_CMP_BLOCK - 1
    inv_freq = 1.0 / (ROPE_THETA ** (jnp.arange(0, d, 2, dtype=F32) / d))
    ang = cmp_end.astype(F32)[:, None] * inv_freq[None, :]
    cos = jnp.concatenate([jnp.cos(ang)] * 2, axis=1)
    sin = jnp.concatenate([jnp.sin(ang)] * 2, axis=1)
    rot = np.zeros((d, d), np.float32)
    rot[np.arange(d // 2) + d // 2, np.arange(d // 2)] = -1.0
    rot[np.arange(d // 2), np.arange(d // 2) + d // 2] = 1.0
    blk = lambda bb, gg: (bb, gg, 0, 0)
    seq = lambda bb, gg: (bb, 0, 0)
    c2 = lambda bb, gg: (0, 0)
    c3 = lambda bb, gg: (0, 0, 0)
    return pl.pallas_call(
        functools.partial(_compress_body, is_key=is_key), grid=(b, g),
        in_specs=[pl.BlockSpec((1, n_ch, half), seq), pl.BlockSpec((1, n_ch, half), seq),
                  pl.BlockSpec((2, 1, half), c3), pl.BlockSpec((1, 2, half, d), lambda bb, gg: (gg, 0, 0, 0)),
                  pl.BlockSpec((d, d), c2), pl.BlockSpec((1, d), c2),
                  pl.BlockSpec((n_ch, d), c2), pl.BlockSpec((n_ch, d), c2), pl.BlockSpec((d, d), c2)],
        out_specs=pl.BlockSpec((1, 1, n_ch, d), blk),
        out_shape=jax.ShapeDtypeStruct((b, g, n_ch, d), F32),
        compiler_params=_cparams(("parallel", "parallel")), name="nsa_compress",
    )(ch, nxt, pos2, w1s, w2.astype(BF16), gain.reshape(1, d).astype(F32), cos, sin,
      jnp.asarray(rot, dtype=BF16))


def _nsa_body(q_ref, ck_ref, cvt_ref, ksl_ref, vslt_ref, kwn_ref, vwnt_ref, ovt_ref, glt_ref, o_ref,
              sc_ref, phi_ref, plo_ref, imp_ref, bias_ref, ss_ref, ps_ref, ss1_ref, ps1_ref, sw_ref, pw_ref,
              ow_ref, oc_ref, *, tq):
    g = pl.program_id(1)
    i = pl.program_id(2)
    d = NSA_HEAD_DIM
    rep = NSA_HEADS // NSA_KV_GROUPS
    t0 = i * tq
    n_cmp = ck_ref.shape[1]
    n_sel = ovt_ref.shape[0]
    width = rep * tq
    sub = NSA_SEL_BLOCK
    dead = 0.5 * NEG_INF
    v_rows = pl.ds(pl.multiple_of(g * d, d), d)

    qb = q_ref[0]
    q4 = jnp.concatenate([qb[:, r * d:(r + 1) * d] for r in range(rep)], axis=0)
    q4 = jnp.concatenate([q4, q4], axis=1)
    lane = lax.broadcasted_iota(jnp.int32, q4.shape, 1)
    q4 = jnp.where(jnp.right_shift(lane, d.bit_length() - 1) == g, q4, jnp.zeros_like(q4))

    def qpos_of(shape):
        return t0 + (lax.broadcasted_iota(jnp.int32, shape, 1) & (tq - 1))

    def compressed(rows):
        s = _dot_nt(ck_ref[0, 0:rows, :], q4)
        cmp_end = lax.broadcasted_iota(jnp.int32, s.shape, 0) * NSA_CMP_STRIDE + (NSA_CMP_BLOCK - 1)
        s = jnp.where(cmp_end <= qpos_of(s.shape), s, NEG_INF)
        sc_ref[0:rows, :] = s
        m_c = jnp.max(s, axis=0, keepdims=True)
        lpart = jnp.zeros((SUBLANES, width), F32)
        for r in range(rows // sub):
            e = jnp.exp2(sc_ref[r * sub:(r + 1) * sub, :] - m_c)
            lpart = lpart + _fold_rows(e)
            hi = e.astype(BF16)
            phi_ref[r * sub:(r + 1) * sub, :] = hi
            plo_ref[r * sub:(r + 1) * sub, :] = (e - hi.astype(F32)).astype(BF16)
        inv_c = jnp.where(m_c > dead, 1.0 / jnp.sum(lpart, axis=0, keepdims=True), 0.0)
        oc_ref[...] = _dot(cvt_ref[0, v_rows, 0:rows], phi_ref[0:rows, :]) * inv_c
        ovt = ovt_ref[:, 0:rows]
        imp4 = (_dot(ovt, phi_ref[0:rows, :]) + _dot(ovt, plo_ref[0:rows, :])) * inv_c
        imp = imp4[:, 0:tq]
        for r in range(1, rep):
            imp = imp + imp4[:, r * tq:(r + 1) * tq]
        imp_ref[...] = imp

    cmp_live = jnp.clip((t0 + tq - NSA_CMP_BLOCK) // NSA_CMP_STRIDE + 1, 1, n_cmp)
    cmp_step = min(2 * LANES, n_cmp)
    for v in range(n_cmp // cmp_step):
        @pl.when((cmp_live > cmp_step * v) & (cmp_live <= cmp_step * (v + 1)))
        def _():
            compressed(cmp_step * (v + 1))

    o_c = oc_ref[...]
    imp = imp_ref[...]

    blk = lax.broadcasted_iota(jnp.int32, imp.shape, 0)
    qp = t0 + lax.broadcasted_iota(jnp.int32, imp.shape, 1)
    cur = jnp.right_shift(qp, NSA_SEL_BLOCK.bit_length() - 1)
    forced = (blk == 0) | (blk == cur) | (blk == cur - 1)
    future = blk * NSA_SEL_BLOCK > qp
    imp_ref[...] = jnp.where(future, -FORCE_SCORE, jnp.where(forced, FORCE_SCORE, imp))
    bias_ref[...] = jnp.full(bias_ref.shape, NEG_INF, F32)

    n_live = jnp.minimum((t0 + tq - 1) // NSA_SEL_BLOCK + 1, n_sel)
    n_var = max(n_sel // 32, 1)
    rows_per = n_sel // n_var
    top_n = float(min(NSA_TOP_N, n_sel))
    for v in range(n_var):
        rows = rows_per * (v + 1)

        @pl.when((n_live > rows_per * v) & (n_live <= rows))
        def _():
            groups = rows // SUBLANES
            mine = [imp_ref[gi * SUBLANES:(gi + 1) * SUBLANES, :] for gi in range(groups)]
            rank = [jnp.zeros((SUBLANES, tq), F32) for _ in range(groups)]
            in_group = lax.broadcasted_iota(jnp.int32, (SUBLANES, tq), 0)
            for i2 in range(rows):
                other = imp_ref[i2:i2 + 1, :]
                for gi in range(groups):
                    if gi > i2 // SUBLANES:
                        beats = other >= mine[gi]
                    elif gi < i2 // SUBLANES:
                        beats = other > mine[gi]
                    else:
                        beats = (other > mine[gi]) | ((other == mine[gi]) & (in_group > i2 % SUBLANES))
                    rank[gi] = rank[gi] + jnp.where(beats, 1.0, 0.0)
            for gi in range(groups):
                bias = jnp.where(rank[gi] < top_n, 0.0, NEG_INF)
                bias_ref[gi * SUBLANES:(gi + 1) * SUBLANES, :] = jnp.concatenate([bias] * rep, axis=1)

    init = (jnp.full((1, width), NEG_INF, F32), jnp.zeros((d + ONES_ROWS, width), F32))

    def normalised(acc):
        return acc[:d, :] * (1.0 / acc[d:d + 1, :])

    chunk = 8 * sub
    n_sub = chunk // sub

    s_slots = (ss_ref, ss1_ref)
    p_slots = (ps_ref, ps1_ref)

    def sel_scores(c, slot, diagonal=False):
        start = pl.multiple_of(c * chunk, chunk)
        s = _dot_nt(ksl_ref[0, pl.ds(start, chunk), :], q4)
        if diagonal:
            kpos = start + lax.broadcasted_iota(jnp.int32, s.shape, 0)
            s = jnp.where(kpos <= qpos_of(s.shape), s, NEG_INF)
        s_slots[slot][...] = s

    def sel_update(c, slot, carry):
        m_prev, acc = carry
        s_ref, p_ref = s_slots[slot], p_slots[slot]
        biases = [bias_ref[pl.ds(c * n_sub + r, 1), :] for r in range(n_sub)]
        m8 = jnp.full((SUBLANES, width), NEG_INF, F32)
        for r in range(n_sub):
            block = s_ref[r * sub:(r + 1) * sub, :]
            m8 = jnp.maximum(m8, jnp.max(block.reshape(sub // SUBLANES, SUBLANES, width), axis=0) + biases[r])
        m_new = jnp.maximum(m_prev, jnp.max(m8, axis=0, keepdims=True))
        live = m_new > dead
        for r in range(n_sub):
            shift = jnp.where(live, biases[r] - m_new, NEG_INF)
            p_ref[r * sub:(r + 1) * sub, :] = jnp.exp2(s_ref[r * sub:(r + 1) * sub, :] + shift).astype(BF16)
        vt = _with_ones_rows(vslt_ref[v_rows, pl.ds(pl.multiple_of(c * chunk, chunk), chunk)])
        return m_new, jnp.exp2(m_prev - m_new) * acc + _dot(vt, p_ref[...])

    c_diag = (t0 + tq - 1) // chunk
    n_chunks = c_diag + 1
    last_past = jnp.maximum(c_diag - 1, 0)

    def chunk_at(j):
        return jnp.where(j == 0, c_diag, j - 1)

    sel_scores(c_diag, 0, diagonal=True)

    def pair(k, carry):
        sel_scores(jnp.minimum(2 * k, last_past), 1)
        carry = sel_update(chunk_at(2 * k), 0, carry)
        sel_scores(jnp.minimum(2 * k + 1, last_past), 0)
        return sel_update(2 * k, 1, carry)

    carry = lax.fori_loop(0, n_chunks // 2, pair, init)
    _, acc_s = lax.cond(n_chunks % 2 == 1, lambda cr: sel_update(chunk_at(n_chunks - 1), 0, cr),
                        lambda cr: cr, carry)
    o_s = normalised(acc_s)

    def win_chunk(c, carry):
        m_prev, acc = carry
        start = pl.multiple_of(c * tq, tq)
        s = _dot_nt(kwn_ref[0, pl.ds(start, tq), :], q4)
        kpos = start + lax.broadcasted_iota(jnp.int32, s.shape, 0)
        qpos = qpos_of(s.shape)
        s = jnp.where((kpos <= qpos) & (kpos > qpos - NSA_WINDOW), s, NEG_INF)
        m_new = jnp.maximum(m_prev, jnp.max(s, axis=0, keepdims=True))
        p = jnp.exp2(s + jnp.where(m_new > dead, -m_new, NEG_INF))
        vt = _with_ones_rows(vwnt_ref[v_rows, pl.ds(start, tq)])
        return m_new, jnp.exp2(m_prev - m_new) * acc + _dot(vt, p.astype(BF16))

    n_back = NSA_WINDOW // tq

    @pl.when(i < n_back)
    def _():
        _, acc_w = lax.fori_loop(0, i + 1, win_chunk, init)
        ow_ref[...] = normalised(acc_w)

    @pl.when(i >= n_back)
    def _():
        start = pl.multiple_of(t0 - NSA_WINDOW, tq)
        s = _dot_nt(kwn_ref[0, pl.ds(start, NSA_WINDOW + tq), :], q4)
        kpos = start + lax.broadcasted_iota(jnp.int32, (tq, width), 0)
        qpos = qpos_of((tq, width))
        sw_ref[0:tq, :] = jnp.where(kpos > qpos - NSA_WINDOW, s[0:tq, :], NEG_INF)
        sw_ref[tq:NSA_WINDOW, :] = s[tq:NSA_WINDOW, :]
        sw_ref[NSA_WINDOW:, :] = jnp.where(kpos + NSA_WINDOW <= qpos, s[NSA_WINDOW:, :], NEG_INF)
        m_w = jnp.max(sw_ref[...], axis=0, keepdims=True)
        for r in range((NSA_WINDOW + tq) // sub):
            pw_ref[r * sub:(r + 1) * sub, :] = jnp.exp2(sw_ref[r * sub:(r + 1) * sub, :] - m_w).astype(BF16)
        vt = _with_ones_rows(vwnt_ref[v_rows, pl.ds(start, NSA_WINDOW + tq)])
        ow_ref[...] = normalised(_dot(vt, pw_ref[...]))

    o_w = ow_ref[...]

    def gate(branch):
        rows = [glt_ref[pl.ds((g * rep + r) * 3 + branch, 1), :] for r in range(rep)]
        return _sigmoid(jnp.concatenate(rows, axis=1))

    out = gate(0) * o_c + gate(1) * o_s + gate(2) * o_w
    out_t = jnp.concatenate([out, jnp.zeros_like(out)], axis=0).T
    o_ref[0] = jnp.concatenate([out_t[r * tq:(r + 1) * tq, :d] for r in range(rep)],
                               axis=1).astype(o_ref.dtype)


def nsa_overlap_t(n_cmp, n_sel):
    c_start = np.arange(n_cmp)[None, :] * NSA_CMP_STRIDE
    s_start = np.arange(n_sel)[:, None] * NSA_SEL_BLOCK
    hit = (c_start < s_start + NSA_SEL_BLOCK) & (c_start + NSA_CMP_BLOCK > s_start)
    hit = hit & (np.arange(n_cmp)[None, :] < n_cmp - NSA_CMP_BLOCK // NSA_CMP_STRIDE + 1)
    return jnp.asarray(hit.astype(np.float32), dtype=BF16)


def nsa_attention(qn, ck, cvt, ksl, vslt, kwn, vwnt, glt, tq=256):
    b, s, _ = qn.shape
    g, d = NSA_KV_GROUPS, NSA_HEAD_DIM
    rep = NSA_HEADS // g
    n_cmp = ck.shape[1]
    n_sel = s // NSA_SEL_BLOCK
    nq = s // tq
    ovt = nsa_overlap_t(n_cmp, n_sel)
    width = rep * tq
    chunk = 8 * NSA_SEL_BLOCK
    full3 = lambda bb, gg, i: (bb, 0, 0)
    seq_t = lambda bb, gg, i: (0, bb)
    return pl.pallas_call(
        functools.partial(_nsa_body, tq=tq), grid=(b, g, nq),
        in_specs=[pl.BlockSpec((1, tq, rep * d), lambda bb, gg, i: (bb, i, gg)),
                  pl.BlockSpec((1, n_cmp, g * d), full3), pl.BlockSpec((1, g * d, n_cmp), full3),
                  pl.BlockSpec((1, s, g * d), full3), pl.BlockSpec((g * d, s), seq_t),
                  pl.BlockSpec((1, s, g * d), full3), pl.BlockSpec((g * d, s), seq_t),
                  pl.BlockSpec((n_sel, n_cmp), lambda bb, gg, i: (0, 0)),
                  pl.BlockSpec((glt.shape[0], tq), lambda bb, gg, i: (0, bb * nq + i))],
        out_specs=pl.BlockSpec((1, tq, rep * d), lambda bb, gg, i: (bb, i, gg)),
        out_shape=jax.ShapeDtypeStruct((b, s, g * rep * d), BF16),
        scratch_shapes=[pltpu.VMEM((n_cmp, width), F32), pltpu.VMEM((n_cmp, width), BF16),
                        pltpu.VMEM((n_cmp, width), BF16), pltpu.VMEM((n_sel, tq), F32),
                        pltpu.VMEM((n_sel, width), F32), pltpu.VMEM((chunk, width), F32),
                        pltpu.VMEM((chunk, width), BF16), pltpu.VMEM((chunk, width), F32),
                        pltpu.VMEM((chunk, width), BF16), pltpu.VMEM((NSA_WINDOW + tq, width), F32),
                        pltpu.VMEM((NSA_WINDOW + tq, width), BF16), pltpu.VMEM((d, width), F32),
                        pltpu.VMEM((d, width), F32)],
        compiler_params=_cparams(("parallel", "parallel", "arbitrary")), name="nsa_attention",
    )(qn, ck, cvt, ksl, vslt, kwn, vwnt, ovt, glt)


def _pad_cols(w, n):
    return jnp.pad(w, ((0, 0), (0, n - w.shape[1])))


def _even_layer(x2, b, s, layer_idx, norm_mix, w_in, q_gain, k_gain, lam, subln_gain, conv_w, conv_b,
                dt_bias, a_log, d_skip, ssm_norm_gain, w_out, norm_ffn, w_gate, w_up, w_down):
    nq = DA_HEADS * 2 * DA_HEAD_DIM
    nv = DA_HEADS * DA_V_DIM
    cch = SSM_D_INNER + 2 * SSM_GROUPS * SSM_STATE
    offs = np.cumsum([0, nq, nq, nv, SSM_D_INNER, cch, SSM_HEADS])
    wb = w_in.astype(BF16)
    pieces = [wb[:, offs[k]:offs[k + 1]] for k in range(6)]
    pieces[2] = pieces[2].T
    pieces[5] = _pad_cols(pieces[5], LANES)
    posts = [HeadNorm(q_gain, DA_HEAD_DIM, rope=True, mul=DA_HEAD_DIM ** -0.5 * LOG2E),
             HeadNorm(k_gain, DA_HEAD_DIM, rope=True), None, None, None, None]
    q, k, vt, z, xbc, dt = norm_proj(x2, norm_mix, pieces, [BF16, BF16, BF16, F32, F32, F32], posts, s,
                                     _rope_tables(s, DA_HEAD_DIM), transposed=(2,))
    qn = q.reshape(b, s, nq)
    kn = k.reshape(b, s, nq)
    lam_init = 0.8 - 0.6 * math.exp(-0.3 * layer_idx)
    lf = lam.astype(F32)
    lam_full = jnp.exp(jnp.sum(lf[0] * lf[1])) - jnp.exp(jnp.sum(lf[2] * lf[3])) + lam_init
    a_out = flash_attention(lam_full.reshape(1), [qn], [kn], vt, subln_gain, DA_HEADS, DA_V_DIM,
                            diff=True, out_scale=1.0 - lam_init)
    b_out = ssd_mixer(xbc.reshape(b, s, cch), z.reshape(b, s, SSM_D_INNER), dt.reshape(b, s, LANES),
                      conv_w, conv_b, dt_bias, a_log, d_skip, ssm_norm_gain)
    wo = w_out.astype(BF16)
    x2 = out_proj_residual(x2, a_out.reshape(-1, nv), b_out.reshape(-1, SSM_D_INNER), wo[:nv], wo[nv:])
    return ffn_residual(x2, norm_ffn, w_gate.astype(BF16), w_up.astype(BF16), w_down.astype(BF16))


def _odd_layer(x2, b, s, norm_mix, w_in, q_gain, k_gain, cmp_pos, cmp_w1, cmp_w2, cq_gain, ckv_gain,
               w_uq, w_ukv, qn_gain, qr_gain, kn_gain, kr_gain, w_out, norm_ffn, router_w, w_gate, w_up,
               w_down):
    g, d = NSA_KV_GROUPS, NSA_HEAD_DIM
    nq = NSA_HEADS * d
    nkv = g * d
    sizes = [nq] + [nkv] * 6 + [NSA_HEADS * 3, w_uq.shape[0], w_ukv.shape[0], MLA_ROPE_DIM]
    offs = np.cumsum([0] + sizes)
    wb = w_in.astype(BF16)
    pieces = [wb[:, offs[k]:offs[k + 1]] for k in range(len(sizes))]
    for k in (4, 6):
        pieces[k] = pieces[k].T
    pieces[7] = jnp.pad(pieces[7].T, ((0, 32 - NSA_HEADS * 3), (0, 0)))
    pieces[10] = _pad_cols(pieces[10], LANES)
    tables = _rope_tables(s, d)
    posts = [None] * len(sizes)
    posts[0] = HeadNorm(q_gain, d, rope=True, mul=d ** -0.5 * LOG2E)
    posts[3] = HeadNorm(k_gain[1], d, rope=True)
    posts[5] = HeadNorm(k_gain[2], d, rope=True)
    posts[10] = HeadNorm(kr_gain, MLA_ROPE_DIM, rope=True)
    (q, kc, vc, ksl, vslt, kwn, vwnt, glt, cq, ckv, k_rope) = norm_proj(
        x2, norm_mix, pieces, [BF16, F32, F32, BF16, BF16, BF16, BF16, F32, F32, F32, BF16], posts, s, tables,
        transposed=(4, 6, 7))

    qn = q.reshape(b, s, nq)
    ksl_n = ksl.reshape(b, s, nkv)
    kwn_n = kwn.reshape(b, s, nkv)
    ck = nsa_compress(kc.reshape(b, s, nkv), cmp_pos[0], cmp_w1[0], cmp_w2[0], k_gain[0], s, True)
    cv = nsa_compress(vc.reshape(b, s, nkv), cmp_pos[1], cmp_w1[1], cmp_w2[1], k_gain[0], s, False)
    n_cmp = ck.shape[2]
    ck = ck.transpose(0, 2, 1, 3).reshape(b, n_cmp, nkv).astype(BF16)
    cvt = cv.transpose(0, 1, 3, 2).reshape(b, nkv, n_cmp).astype(BF16)
    c_out = nsa_attention(qn, ck, cvt, ksl_n, vslt, kwn_n, vwnt, glt).reshape(b * s, nq)

    h = MLA_HEADS
    dqk = MLA_NOPE_DIM + MLA_ROPE_DIM
    wq = w_uq.astype(BF16).reshape(-1, h, dqk)
    wq_nope = wq[:, :, :MLA_NOPE_DIM].reshape(-1, h * MLA_NOPE_DIM)
    wq_rope = jnp.pad(wq[:, :, MLA_NOPE_DIM:], ((0, 0), (0, 0), (0, LANES - MLA_ROPE_DIM)))
    wq_rope = wq_rope.reshape(-1, h * LANES)
    wkv = w_ukv.astype(BF16).reshape(-1, h, MLA_NOPE_DIM + MLA_V_DIM)
    wk_nope = wkv[:, :, :MLA_NOPE_DIM].reshape(-1, h * MLA_NOPE_DIM)
    wv = wkv[:, :, MLA_NOPE_DIM:].reshape(-1, h * MLA_V_DIM)
    q_mul = dqk ** -0.5 * LOG2E
    q_nope, q_rope = norm_proj(
        cq, cq_gain, [wq_nope, wq_rope], [BF16, BF16],
        [HeadNorm(qn_gain, MLA_NOPE_DIM, mul=q_mul), HeadNorm(qr_gain, MLA_ROPE_DIM, rope=True, mul=q_mul)],
        s, tables)
    k_nope, vt = norm_proj(ckv, ckv_gain, [wk_nope, wv.T], [BF16, BF16],
                           [HeadNorm(kn_gain, MLA_NOPE_DIM), None], transposed=(1,))
    shp = lambda t: t.reshape(b, s, t.shape[-1])
    d_out = flash_attention(jnp.zeros((1,), F32), [shp(q_nope), shp(q_rope)], [shp(k_nope), shp(k_rope)],
                            vt, jnp.ones((MLA_V_DIM,), F32), h, MLA_V_DIM, diff=False)

    wo = w_out.astype(BF16)
    x2 = out_proj_residual(x2, c_out, d_out.reshape(b * s, h * MLA_V_DIM), wo[:nq], wo[nq:])
    h, route, counts = router(x2, norm_ffn, router_w)
    return moe_residual(x2, h, route, counts, w_gate.astype(BF16), w_up.astype(BF16), w_down.astype(BF16))


def kernel(x, ev_norm_mix, ev_w_in, da_q_gain, da_k_gain, da_lambda, da_subln_gain, ssm_conv_w, ssm_conv_b, ssm_dt_bias, ssm_a_log, ssm_d, ssm_norm_gain, ev_w_out, ev_norm_ffn, ffn_w_gate, ffn_w_up, ffn_w_down, od_norm_mix, od_w_in, nsa_q_gain, nsa_k_gain, nsa_cmp_pos, nsa_cmp_w1, nsa_cmp_w2, mla_cq_gain, mla_ckv_gain, mla_w_uq, mla_w_ukv, mla_qn_gain, mla_qr_gain, mla_kn_gain, mla_kr_gain, od_w_out, od_norm_ffn, moe_router, moe_w_gate, moe_w_up, moe_w_down):
    b, s, d = x.shape
    x2 = x.reshape(b * s, d)
    depth = ev_norm_mix.shape[0] + od_norm_mix.shape[0]
    for layer in range(depth):
        i = layer // 2
        if layer % 2 == 0:
            x2 = _even_layer(x2, b, s, layer, ev_norm_mix[i], ev_w_in[i], da_q_gain[i], da_k_gain[i],
                             da_lambda[i], da_subln_gain[i], ssm_conv_w[i], ssm_conv_b[i],
                             ssm_dt_bias[i], ssm_a_log[i], ssm_d[i], ssm_norm_gain[i], ev_w_out[i],
                             ev_norm_ffn[i], ffn_w_gate[i], ffn_w_up[i], ffn_w_down[i])
        else:
            x2 = _odd_layer(x2, b, s, od_norm_mix[i], od_w_in[i], nsa_q_gain[i], nsa_k_gain[i],
                            nsa_cmp_pos[i], nsa_cmp_w1[i], nsa_cmp_w2[i], mla_cq_gain[i],
                            mla_ckv_gain[i], mla_w_uq[i], mla_w_ukv[i], mla_qn_gain[i], mla_qr_gain[i],
                            mla_kn_gain[i], mla_kr_gain[i], od_w_out[i], od_norm_ffn[i], moe_router[i],
                            moe_w_gate[i], moe_w_up[i], moe_w_down[i])
    return x2.reshape(b, s, d)
```

```python
import functools
import math

import numpy as np
import jax
import jax.numpy as jnp
from jax import lax
from jax.experimental import pallas as pl
from jax.experimental.pallas import tpu as pltpu

F32 = jnp.float32
BF16 = jnp.bfloat16

ROPE_THETA = 10000.0
NORM_EPS = 1e-6
NEG_INF = -1e30
FORCE_SCORE = 1e6
LOG2E = 1.4426950408889634

DA_HEADS = 4
DA_HEAD_DIM = 64
DA_V_DIM = 2 * DA_HEAD_DIM
SSM_HEADS = 8
SSM_HEAD_DIM = 64
SSM_D_INNER = SSM_HEADS * SSM_HEAD_DIM
SSM_GROUPS = 2
SSM_STATE = 128
SSM_CONV = 4
SSM_CHUNK = 256
NSA_HEADS = 8
NSA_KV_GROUPS = 2
NSA_HEAD_DIM = 64
NSA_CMP_BLOCK = 32
NSA_CMP_STRIDE = 16
NSA_SEL_BLOCK = 64
NSA_TOP_N = 16
NSA_WINDOW = 512
MLA_HEADS = 4
MLA_NOPE_DIM = 128
MLA_ROPE_DIM = 64
MLA_V_DIM = 128
N_EXPERTS = 8

LANES = 128
SUBLANES = 8
VMEM_LIMIT = 48 * 1024 * 1024
MOE_VMEM_LIMIT = 58 * 1024 * 1024

NT_DIMS = (((1,), (1,)), ((), ()))


def _cparams(semantics):
    return pltpu.CompilerParams(dimension_semantics=semantics, vmem_limit_bytes=VMEM_LIMIT)


def _dot(a, b):
    return jnp.dot(a, b, preferred_element_type=F32)


def _dot_nt(a, b):
    return lax.dot_general(a, b, NT_DIMS, preferred_element_type=F32)


def _split_bf16(x, parts):
    out = []
    for _ in range(parts):
        hi = x.astype(BF16)
        out.append(hi)
        x = x - hi.astype(F32)
    return out


def _fold_rows(x):
    return jnp.sum(x.reshape(x.shape[0] // SUBLANES, SUBLANES, x.shape[1]), axis=0)


ONES_ROWS = 16


def _with_ones_rows(vt):
    return jnp.concatenate([vt, jnp.ones((ONES_ROWS, vt.shape[1]), vt.dtype)], axis=0)


def _sigmoid(x):
    return 1.0 / (1.0 + jnp.exp(-x))


def _silu(x):
    return x * _sigmoid(x)


def _softplus(x):
    return jnp.maximum(x, 0.0) + jnp.log(1.0 + jnp.exp(-jnp.abs(x)))


def _rms(x, gain):
    ms = jnp.mean(x * x, axis=-1, keepdims=True)
    return x * lax.rsqrt(ms + NORM_EPS) * gain


class HeadNorm:
    def __init__(self, gain, hd, rope=False, mul=1.0):
        self.gain, self.hd, self.rope, self.mul = gain, hd, rope, mul


def _head_norm(y, gain, bd, cos_ref, sin_ref, post):
    n = y.shape[1]
    hd = post.hd
    hi, lo = _split_bf16(y * y, 2)
    ss = _dot(hi, bd) + _dot(lo, bd)
    yn = y * lax.rsqrt(ss * (1.0 / hd) + NORM_EPS) * gain
    if post.rope:
        reps = n // LANES
        cos = jnp.concatenate([cos_ref[...]] * reps, axis=1) if reps > 1 else cos_ref[...]
        sin = jnp.concatenate([sin_ref[...]] * reps, axis=1) if reps > 1 else sin_ref[...]
        lane = lax.broadcasted_iota(jnp.int32, yn.shape, 1)
        first_half = (lane & (hd - 1)) < (hd // 2)
        partner = jnp.where(first_half, pltpu.roll(yn, n - hd // 2, 1), pltpu.roll(yn, hd // 2, 1))
        yn = yn * cos + partner * sin
    if post.mul != 1.0:
        yn = yn * post.mul
    return yn


def _norm_proj_body(x_ref, g_ref, *refs, posts, use_rope, transposed):
    if use_rope:
        cos_ref, sin_ref = refs[0], refs[1]
        refs = refs[2:]
    else:
        cos_ref = sin_ref = None
    n_out = len(posts)
    n_aux = 2 * sum(p is not None for p in posts)
    w_refs, aux, o_refs = refs[:n_out], refs[n_out:n_out + n_aux], refs[n_out + n_aux:]
    h = _rms(x_ref[...], g_ref[...]).astype(BF16)
    a = 0
    for k, (w_ref, o_ref, post) in enumerate(zip(w_refs, o_refs, posts)):
        if k in transposed:
            o_ref[...] = _dot_nt(w_ref[...], h).astype(o_ref.dtype)
            continue
        y = _dot(h, w_ref[...])
        if post is not None:
            y = _head_norm(y, aux[a][...], aux[a + 1][...], cos_ref, sin_ref, post)
            a += 2
        o_ref[...] = y.astype(o_ref.dtype)


def norm_proj(x2, gain, weights, out_dtypes, posts=None, seq=None, rope_tables=None, transposed=(), tm=512):
    t, d = x2.shape
    posts = posts or [None] * len(weights)
    transposed = frozenset(transposed)
    use_rope = any(p is not None and p.rope for p in posts)
    const = lambda i: (0, 0)
    args = [x2, gain.reshape(1, d).astype(F32)]
    in_specs = [pl.BlockSpec((tm, d), lambda i: (i, 0)), pl.BlockSpec((1, d), const)]
    if use_rope:
        per_seq = seq // tm
        args += list(rope_tables)
        in_specs += [pl.BlockSpec((tm, LANES), lambda i: (i % per_seq, 0))] * 2
    args += list(weights)
    in_specs += [pl.BlockSpec(w.shape, const) for w in weights]
    for w, p in zip(weights, posts):
        if p is not None:
            n = w.shape[1]
            args += [jnp.tile(p.gain.astype(F32), n // p.hd).reshape(1, n), _block_diag_ones(n, p.hd)]
            in_specs += [pl.BlockSpec((1, n), const), pl.BlockSpec((n, n), const)]
    out_specs, out_shape = [], []
    for k, (w, dt) in enumerate(zip(weights, out_dtypes)):
        if k in transposed:
            out_specs.append(pl.BlockSpec((w.shape[0], tm), lambda i: (0, i)))
            out_shape.append(jax.ShapeDtypeStruct((w.shape[0], t), dt))
        else:
            out_specs.append(pl.BlockSpec((tm, w.shape[1]), lambda i: (i, 0)))
            out_shape.append(jax.ShapeDtypeStruct((t, w.shape[1]), dt))
    return pl.pallas_call(
        functools.partial(_norm_proj_body, posts=tuple(posts), use_rope=use_rope, transposed=transposed),
        grid=(t // tm,), in_specs=in_specs, out_specs=out_specs, out_shape=out_shape,
        compiler_params=_cparams(("parallel",)), name="norm_proj",
    )(*args)


def _block_diag_ones(n, hd):
    idx = np.arange(n) // hd
    return jnp.asarray((idx[:, None] == idx[None, :]).astype(np.float32), dtype=BF16)


def _rope_tables(seq, hd):
    inv_freq = 1.0 / (ROPE_THETA ** (jnp.arange(0, hd, 2, dtype=F32) / hd))
    ang = jnp.arange(seq, dtype=F32)[:, None] * inv_freq[None, :]
    cos, sin = jnp.cos(ang), jnp.sin(ang)
    reps = LANES // hd
    cos_t = jnp.tile(jnp.concatenate([cos, cos], axis=1), (1, reps))
    sin_t = jnp.tile(jnp.concatenate([-sin, sin], axis=1), (1, reps))
    return cos_t, sin_t


def _flash_body(lam_ref, *refs, n_qk, diff, out_scale, sub):
    q_refs = refs[:n_qk]
    k_refs = refs[n_qk:2 * n_qk]
    vt_ref, gain_ref, o_ref, m_ref, l_ref, acc_ref, s0_ref, s1_ref, p0_ref, p1_ref = refs[2 * n_qk:]
    i = pl.program_id(2)
    n_sm = 2 if diff else 1
    _, tk, tq = s0_ref.shape
    s_slots = (s0_ref, s1_ref)
    p_slots = (p0_ref, p1_ref)

    m_ref[...] = jnp.full(m_ref.shape, NEG_INF, F32)
    l_ref[...] = jnp.zeros(l_ref.shape, F32)
    acc_ref[...] = jnp.zeros(acc_ref.shape, F32)

    qs = [r[0] for r in q_refs]
    q = qs[0] if n_qk == 1 else jnp.concatenate(qs, axis=1)
    if diff:
        lane = lax.broadcasted_iota(jnp.int32, q.shape, 1)
        half = q.shape[1] // 2
        zero = jnp.zeros_like(q)
        q_parts = [jnp.where(lane < half, q, zero), jnp.where(lane >= half, q, zero)]
    else:
        q_parts = [q]

    def scores(c, slot, key_offset=None):
        rows = pl.ds(pl.multiple_of(c * tk, tk), tk)
        ks = [r[0, rows, :] for r in k_refs]
        k = ks[0] if n_qk == 1 else jnp.concatenate(ks, axis=1)
        for sm in range(n_sm):
            s = _dot_nt(k, q_parts[sm])
            if key_offset is not None:
                row = lax.broadcasted_iota(jnp.int32, s.shape, 0)
                col = lax.broadcasted_iota(jnp.int32, s.shape, 1)
                s = jnp.where(row + key_offset <= col, s, NEG_INF)
            s_slots[slot][sm] = s

    def update(c, slot):
        vt = vt_ref[:, pl.ds(pl.multiple_of(c * tk, tk), tk)]
        for sm in range(n_sm):
            s_ref, p_ref = s_slots[slot], p_slots[slot]
            m_prev = m_ref[sm]
            m_new = jnp.maximum(m_prev, jnp.max(s_ref[sm], axis=0, keepdims=True))
            m_ref[sm] = m_new
            alpha = jnp.exp2(m_prev - m_new)
            lpart = jnp.zeros((SUBLANES, tq), F32)
            for r in range(tk // sub):
                p = jnp.exp2(s_ref[sm, r * sub:(r + 1) * sub, :] - m_new)
                lpart = lpart + _fold_rows(p)
                p_ref[sm, r * sub:(r + 1) * sub, :] = p.astype(BF16)
            l_ref[sm] = alpha * l_ref[sm] + jnp.sum(lpart, axis=0, keepdims=True)
            acc_ref[sm] = alpha * acc_ref[sm] + _dot(vt, p_ref[sm])

    if tq == tk:
        n_chunks = i + 1
        last_past = jnp.maximum(i - 1, 0)

        def chunk_at(j):
            return jnp.where(j == 0, i, j - 1)

        scores(i, 0, key_offset=0)

        def pair(k2, carry):
            scores(jnp.minimum(2 * k2, last_past), 1)
            update(chunk_at(2 * k2), 0)
            scores(jnp.minimum(2 * k2 + 1, last_past), 0)
            update(2 * k2, 1)
            return carry

        lax.fori_loop(0, n_chunks // 2, pair, 0)

        @pl.when(n_chunks % 2 == 1)
        def _():
            update(chunk_at(n_chunks - 1), 0)
    else:
        n_past = 2 * i
        last_past = jnp.maximum(n_past - 1, 0)
        scores(n_past, 0, key_offset=0)
        scores(n_past + 1, 1, key_offset=tk)
        update(n_past, 0)
        scores(0, 0)
        update(n_past + 1, 1)

        def pair(k2, carry):
            scores(2 * k2 + 1, 1)
            update(2 * k2, 0)
            scores(jnp.minimum(2 * k2 + 2, last_past), 0)
            update(2 * k2 + 1, 1)
            return carry

        lax.fori_loop(0, i, pair, 0)

    o = acc_ref[0] * (1.0 / l_ref[0])
    if diff:
        o = o - acc_ref[1] * (lam_ref[0] / l_ref[1])
        ms = jnp.mean(o * o, axis=0, keepdims=True)
        o = o * lax.rsqrt(ms + NORM_EPS) * gain_ref[...] * out_scale
    o_ref[0] = o.T.astype(o_ref.dtype)


def flash_attention(lam, qs, ks, vt, gain, n_heads, dv, *, diff, out_scale=1.0, tk=512):
    b, s, _ = qs[0].shape
    tile = tk if diff else 2 * tk
    sub = 64 if diff else 32
    nt = s // tile
    n_qk = len(qs)
    in_specs = [pl.BlockSpec(memory_space=pltpu.SMEM)]
    for q in qs:
        w = q.shape[2] // n_heads
        in_specs.append(pl.BlockSpec((1, tile, w), lambda bb, h, i: (bb, i, h)))
    for q, k in zip(qs, ks):
        w = q.shape[2] // n_heads
        if k.shape[2] == w:
            in_specs.append(pl.BlockSpec((1, s, w), lambda bb, h, i: (bb, 0, 0)))
        else:
            in_specs.append(pl.BlockSpec((1, s, w), lambda bb, h, i: (bb, 0, h)))
    in_specs.append(pl.BlockSpec((dv, s), lambda bb, h, i: (h, bb)))
    in_specs.append(pl.BlockSpec((dv, 1), lambda bb, h, i: (0, 0)))
    n_sm = 2 if diff else 1
    return pl.pallas_call(
        functools.partial(_flash_body, n_qk=n_qk, diff=diff, out_scale=out_scale, sub=sub),
        grid=(b, n_heads, nt), in_specs=in_specs,
        out_specs=pl.BlockSpec((1, tile, dv), lambda bb, h, i: (bb, i, h)),
        out_shape=jax.ShapeDtypeStruct((b, s, n_heads * dv), BF16),
        scratch_shapes=[pltpu.VMEM((n_sm, 1, tile), F32), pltpu.VMEM((n_sm, 1, tile), F32),
                        pltpu.VMEM((n_sm, dv, tile), F32),
                        pltpu.VMEM((n_sm, tk, tile), F32), pltpu.VMEM((n_sm, tk, tile), F32),
                        pltpu.VMEM((n_sm, tk, tile), BF16), pltpu.VMEM((n_sm, tk, tile), BF16)],
        compiler_params=_cparams(("parallel", "parallel", "arbitrary")),
        name="flash_diff" if diff else "flash_plain",
    )(lam, *qs, *ks, vt, gain.reshape(dv, 1).astype(F32))


def _ssd_body(xbc_ref, z_ref, dt_ref, dtt_ref, cw_ref, cb_ref, dtb_ref, dtbt_ref, al_ref, alt_ref,
              dsk_ref, ng_ref, o_ref, xpad_ref, state_ref):
    chunk = xbc_ref.shape[1]
    d_in = z_ref.shape[2]
    gn = SSM_GROUPS * SSM_STATE
    c = pl.program_id(1)

    @pl.when(c == 0)
    def _():
        xpad_ref[0:8, :] = jnp.zeros((8, xpad_ref.shape[1]), F32)
        state_ref[...] = jnp.zeros(state_ref.shape, F32)

    xpad_ref[8:8 + chunk, :] = xbc_ref[0]
    conv = cb_ref[...]
    for w in range(SSM_CONV):
        conv = conv + cw_ref[w:w + 1, :] * xpad_ref[pl.ds(8 - (SSM_CONV - 1) + w, chunk), :]
    xpad_ref[0:8, :] = xpad_ref[chunk:chunk + 8, :]
    u = _silu(conv)
    xs = u[:, :d_in]
    bmat = u[:, d_in:d_in + gn]
    cmat = u[:, d_in + gn:]

    dt = _softplus(dt_ref[0] + dtb_ref[...])
    ad = dt * (-jnp.exp(al_ref[...]))
    dtt = _softplus(dtt_ref[0] + dtbt_ref[...])
    adt = dtt * (-jnp.exp(alt_ref[...]))
    row = lax.broadcasted_iota(jnp.int32, (chunk, chunk), 0)
    col = lax.broadcasted_iota(jnp.int32, (chunk, chunk), 1)
    lower = row >= col
    tril = jnp.where(lower, 1.0, 0.0).astype(BF16)
    triu = jnp.where(row <= col, 1.0, 0.0).astype(BF16)
    cs = sum(_dot(tril, part) for part in _split_bf16(ad, 3))
    cst = sum(_dot(part, triu) for part in _split_bf16(adt, 3))

    heads_per_group = SSM_HEADS // SSM_GROUPS
    dsk = dsk_ref[...]
    ys = []
    for g in range(SSM_GROUPS):
        bg = bmat[:, g * SSM_STATE:(g + 1) * SSM_STATE]
        cg = cmat[:, g * SSM_STATE:(g + 1) * SSM_STATE].astype(BF16)
        cb = _dot_nt(cg, bg.astype(BF16))
        bgt = bg.T.astype(BF16)
        for r in range(heads_per_group):
            h = g * heads_per_group + r
            ccol = cs[:, h:h + 1]
            crow = cst[h:h + 1, :]
            decay = jnp.exp(jnp.where(lower, ccol - crow, NEG_INF))
            x_h = xs[:, h * SSM_HEAD_DIM:(h + 1) * SSM_HEAD_DIM]
            xdt = x_h * dt[:, h:h + 1]
            y = _dot((cb * decay).astype(BF16), xdt.astype(BF16))
            st = state_ref[h]
            y = y + _dot(cg, st.astype(BF16)) * jnp.exp(ccol)
            last = cst[h:h + 1, chunk - 1:chunk]
            to_end = jnp.exp(last - ccol)
            state_ref[h] = st * jnp.exp(last) + _dot(bgt, (xdt * to_end).astype(BF16))
            ys.append(y + x_h * dsk[:, h * SSM_HEAD_DIM:(h + 1) * SSM_HEAD_DIM])

    y = jnp.concatenate(ys, axis=1) * _silu(z_ref[0])
    gw = d_in // SSM_GROUPS
    for g in range(SSM_GROUPS):
        seg = y[:, g * gw:(g + 1) * gw]
        o_ref[0, :, g * gw:(g + 1) * gw] = _rms(seg, ng_ref[:, g * gw:(g + 1) * gw]).astype(o_ref.dtype)


def ssd_mixer(xbc, z, dt_raw, conv_w, conv_b, dt_bias, a_log, d_skip, norm_gain):
    b, s, cch = xbc.shape
    d_in = z.shape[2]
    nc = s // SSM_CHUNK
    hpad = dt_raw.shape[2]
    dtt = jnp.transpose(dt_raw[:, :, :SSM_HEADS], (0, 2, 1))

    def lane_pad(v):
        return jnp.pad(v.astype(F32), (0, hpad - SSM_HEADS)).reshape(1, hpad)

    args = (xbc, z, dt_raw, dtt, conv_w.astype(F32), conv_b.reshape(1, cch).astype(F32),
            lane_pad(dt_bias), dt_bias.reshape(SSM_HEADS, 1).astype(F32),
            lane_pad(a_log), a_log.reshape(SSM_HEADS, 1).astype(F32),
            jnp.repeat(d_skip.astype(F32), SSM_HEAD_DIM).reshape(1, d_in),
            norm_gain.reshape(1, d_in).astype(F32))
    const = lambda bb, c: (0, 0)
    in_specs = [pl.BlockSpec((1, SSM_CHUNK, cch), lambda bb, c: (bb, c, 0)),
                pl.BlockSpec((1, SSM_CHUNK, d_in), lambda bb, c: (bb, c, 0)),
                pl.BlockSpec((1, SSM_CHUNK, hpad), lambda bb, c: (bb, c, 0)),
                pl.BlockSpec((1, SSM_HEADS, SSM_CHUNK), lambda bb, c: (bb, 0, c)),
                pl.BlockSpec((SSM_CONV, cch), const), pl.BlockSpec((1, cch), const),
                pl.BlockSpec((1, hpad), const), pl.BlockSpec((SSM_HEADS, 1), const),
                pl.BlockSpec((1, hpad), const), pl.BlockSpec((SSM_HEADS, 1), const),
                pl.BlockSpec((1, d_in), const), pl.BlockSpec((1, d_in), const)]
    return pl.pallas_call(
        _ssd_body, grid=(b, nc), in_specs=in_specs,
        out_specs=pl.BlockSpec((1, SSM_CHUNK, d_in), lambda bb, c: (bb, c, 0)),
        out_shape=jax.ShapeDtypeStruct((b, s, d_in), BF16),
        scratch_shapes=[pltpu.VMEM((SSM_CHUNK + 8, cch), F32),
                        pltpu.VMEM((SSM_HEADS, SSM_STATE, SSM_HEAD_DIM), F32)],
        compiler_params=_cparams(("parallel", "arbitrary")), name="ssd_mixer",
    )(*args)


def _out_proj_body(x_ref, a_ref, b_ref, wa_ref, wb_ref, o_ref):
    o_ref[...] = x_ref[...] + _dot(a_ref[...], wa_ref[...]) + _dot(b_ref[...], wb_ref[...])


def out_proj_residual(x2, a, bm, wa, wb, tm=512):
    t, d = x2.shape
    return pl.pallas_call(
        _out_proj_body, grid=(t // tm,),
        in_specs=[pl.BlockSpec((tm, d), lambda i: (i, 0)),
                  pl.BlockSpec((tm, a.shape[1]), lambda i: (i, 0)),
                  pl.BlockSpec((tm, bm.shape[1]), lambda i: (i, 0)),
                  pl.BlockSpec(wa.shape, lambda i: (0, 0)),
                  pl.BlockSpec(wb.shape, lambda i: (0, 0))],
        out_specs=pl.BlockSpec((tm, d), lambda i: (i, 0)),
        out_shape=jax.ShapeDtypeStruct((t, d), F32),
        compiler_params=_cparams(("parallel",)), name="out_proj",
    )(x2, a, bm, wa, wb)


def _ffn_body(x_ref, g_ref, wg_ref, wu_ref, wd_ref, o_ref, h_ref):
    f = pl.program_id(1)

    @pl.when(f == 0)
    def _():
        x = x_ref[...]
        h_ref[...] = _rms(x, g_ref[...]).astype(BF16)
        o_ref[...] = x

    half = h_ref.shape[0] // 2
    for r in (slice(0, half), slice(half, 2 * half)):
        h = h_ref[r, :]
        act = (_silu(_dot(h, wg_ref[...])) * _dot(h, wu_ref[...])).astype(BF16)
        o_ref[r, :] += _dot(act, wd_ref[...])


def ffn_residual(x2, gain, w_gate, w_up, w_down, tm=1024, tf=1408):
    t, d = x2.shape
    d_ff = w_gate.shape[1]
    return pl.pallas_call(
        _ffn_body, grid=(t // tm, d_ff // tf),
        in_specs=[pl.BlockSpec((tm, d), lambda i, f: (i, 0)),
                  pl.BlockSpec((1, d), lambda i, f: (0, 0)),
                  pl.BlockSpec((d, tf), lambda i, f: (0, f)),
                  pl.BlockSpec((d, tf), lambda i, f: (0, f)),
                  pl.BlockSpec((tf, d), lambda i, f: (f, 0))],
        out_specs=pl.BlockSpec((tm, d), lambda i, f: (i, 0)),
        out_shape=jax.ShapeDtypeStruct((t, d), F32),
        scratch_shapes=[pltpu.VMEM((tm, d), BF16)],
        compiler_params=_cparams(("parallel", "arbitrary")), name="ffn",
    )(x2, gain.reshape(1, d).astype(F32), w_gate, w_up, w_down)


MOE_ROWS = 256
ROUTE_IDX = 0
ROUTE_W = 2
ROUTE_RANK = 4


def _moe_ffn_body(block_expert_ref, n_used_ref, xs_ref, wg_ref, wu_ref, wd_ref, o_ref):
    i = pl.program_id(0)

    @pl.when(i < n_used_ref[0])
    def _():
        x = xs_ref[...]
        act = (_silu(_dot(x, wg_ref[0])) * _dot(x, wu_ref[0])).astype(BF16)
        o_ref[...] = _dot(act, wd_ref[0]).astype(o_ref.dtype)

    @pl.when(i >= n_used_ref[0])
    def _():
        o_ref[...] = jnp.zeros(o_ref.shape, o_ref.dtype)


def moe_expert_ffn(xs, block_expert, n_used, w_gate, w_up, w_down):
    p, d = xs.shape
    d_ff = w_gate.shape[2]
    rows = MOE_ROWS
    grid_spec = pltpu.PrefetchScalarGridSpec(
        num_scalar_prefetch=2, grid=(p // rows,),
        in_specs=[pl.BlockSpec((rows, d), lambda i, be, nu: (i, 0)),
                  pl.BlockSpec((1, d, d_ff), lambda i, be, nu: (be[i], 0, 0)),
                  pl.BlockSpec((1, d, d_ff), lambda i, be, nu: (be[i], 0, 0)),
                  pl.BlockSpec((1, d_ff, d), lambda i, be, nu: (be[i], 0, 0))],
        out_specs=pl.BlockSpec((rows, d), lambda i, be, nu: (i, 0)))
    return pl.pallas_call(
        _moe_ffn_body, grid_spec=grid_spec, out_shape=jax.ShapeDtypeStruct((p, d), BF16),
        compiler_params=pltpu.CompilerParams(
            dimension_semantics=("arbitrary",), vmem_limit_bytes=MOE_VMEM_LIMIT),
        name="moe_expert_ffn",
    )(block_expert, n_used, xs, w_gate, w_up, w_down)


def _moe_combine_body(x_ref, y0_ref, y1_ref, route_ref, o_ref):
    route = route_ref[...]
    lane = lax.broadcasted_iota(jnp.int32, route.shape, 1)
    w0 = jnp.sum(jnp.where(lane == ROUTE_W, route, 0.0), axis=1, keepdims=True)
    w1 = jnp.sum(jnp.where(lane == ROUTE_W + 1, route, 0.0), axis=1, keepdims=True)
    o_ref[...] = x_ref[...] + w0 * y0_ref[...].astype(F32) + w1 * y1_ref[...].astype(F32)


def moe_combine(x2, y0, y1, route, tm=512):
    t, d = x2.shape
    row = lambda i: (i, 0)
    return pl.pallas_call(
        _moe_combine_body, grid=(t // tm,),
        in_specs=[pl.BlockSpec((tm, d), row), pl.BlockSpec((tm, d), row), pl.BlockSpec((tm, d), row),
                  pl.BlockSpec((tm, LANES), row)],
        out_specs=pl.BlockSpec((tm, d), row),
        out_shape=jax.ShapeDtypeStruct((t, d), F32),
        compiler_params=_cparams(("parallel",)), name="moe_combine",
    )(x2, y0, y1, route)


def moe_residual(x2, h, route, counts, w_gate, w_up, w_down):
    t, d = x2.shape
    n_e = w_gate.shape[0]
    rows = MOE_ROWS
    expert = route[:, ROUTE_IDX:ROUTE_IDX + 2].astype(jnp.int32).reshape(-1)
    rank = route[:, ROUTE_RANK:ROUTE_RANK + 2].astype(jnp.int32).reshape(-1)
    count = counts[0, :n_e].astype(jnp.int32)
    padded = (count + rows - 1) // rows * rows
    ends = jnp.cumsum(padded)
    starts = ends - padded
    own = expert[:, None] == jnp.arange(n_e, dtype=jnp.int32)[None, :]
    slot = jnp.sum(jnp.where(own, starts[None, :], 0), axis=1) + rank
    p_rows = 2 * t + n_e * rows
    block_start = jnp.arange(p_rows // rows, dtype=jnp.int32) * rows
    block_expert = jnp.minimum(jnp.sum(block_start[:, None] >= ends[None, :], axis=1), n_e - 1).astype(jnp.int32)
    n_used = (ends[-1] // rows).astype(jnp.int32).reshape(1)
    copy_order = jnp.argsort(slot).astype(jnp.int32)
    within = (block_start - starts[block_expert])[:, None] + jnp.arange(rows, dtype=jnp.int32)[None, :]
    dense = (jnp.cumsum(count) - count)[block_expert][:, None] + within
    real = within < count[block_expert][:, None]
    picked = copy_order.at[jnp.clip(dense, 0, 2 * t - 1)].get(mode="promise_in_bounds")
    token_of_slot = jnp.where(real, picked // 2, 0).reshape(p_rows)

    take_rows = lambda a, idx: a.at[idx].get(mode="promise_in_bounds")
    xs = take_rows(h, token_of_slot)
    ys = moe_expert_ffn(xs, block_expert, n_used, w_gate, w_up, w_down)
    slot2 = slot.reshape(t, 2)
    return moe_combine(x2, take_rows(ys, slot2[:, 0]), take_rows(ys, slot2[:, 1]), route)


def _router_body(x_ref, a_ref, b_ref, wa_ref, wb_ref, g_ref, rhi_ref, rlo_ref,
                 xo_ref, h_ref, o_ref, count_ref, run_ref):
    @pl.when(pl.program_id(0) == 0)
    def _():
        run_ref[...] = jnp.zeros(run_ref.shape, F32)

    x = x_ref[...] + _dot(a_ref[...], wa_ref[...]) + _dot(b_ref[...], wb_ref[...])
    xo_ref[...] = x
    h = _rms(x, g_ref[...])
    h_ref[...] = h.astype(h_ref.dtype)
    h_hi, h_lo = _split_bf16(h, 2)
    logits = _dot(h_hi, rhi_ref[...]) + _dot(h_hi, rlo_ref[...]) + _dot(h_lo, rhi_ref[...])
    lane = lax.broadcasted_iota(jnp.int32, logits.shape, 1).astype(F32)
    low = jnp.float32(-3.0e38)
    logits = jnp.where(lane < N_EXPERTS, logits, low)
    m1 = jnp.max(logits, axis=1, keepdims=True)
    i1 = jnp.min(jnp.where(logits == m1, lane, float(LANES)), axis=1, keepdims=True)
    rest = jnp.where(lane == i1, low, logits)
    m2 = jnp.max(rest, axis=1, keepdims=True)
    i2 = jnp.min(jnp.where(rest == m2, lane, float(LANES)), axis=1, keepdims=True)
    ex = jnp.exp(m2 - m1)
    w1 = 1.0 / (1.0 + ex)
    w2 = ex / (1.0 + ex)
    tm = logits.shape[0]
    routed = jnp.where((lane == i1) | (lane == i2), 1.0, 0.0)
    row = lax.broadcasted_iota(jnp.int32, (tm, tm), 0)
    col = lax.broadcasted_iota(jnp.int32, (tm, tm), 1)
    before = jnp.where(col < row, 1.0, 0.0).astype(BF16)
    rank = run_ref[0:1, :] + _dot(before, routed.astype(BF16))
    r1 = jnp.sum(jnp.where(lane == i1, rank, 0.0), axis=1, keepdims=True)
    r2 = jnp.sum(jnp.where(lane == i2, rank, 0.0), axis=1, keepdims=True)
    fields = ((ROUTE_IDX, i1), (ROUTE_IDX + 1, i2), (ROUTE_W, w1), (ROUTE_W + 1, w2),
              (ROUTE_RANK, r1), (ROUTE_RANK + 1, r2))
    out = jnp.zeros(logits.shape, F32)
    for pos, val in fields:
        out = jnp.where(lane == pos, val, out)
    o_ref[...] = out
    run_ref[...] = run_ref[...] + jnp.sum(routed, axis=0, keepdims=True)
    count_ref[...] = run_ref[...]


def out_proj_router(x2, a, bm, wa, wb, gain, router_w, tm=512):
    t, d = x2.shape
    r_pad = jnp.pad(router_w.astype(F32), ((0, 0), (0, LANES - router_w.shape[1])))
    r_hi, r_lo = _split_bf16(r_pad, 2)
    row = lambda i: (i, 0)
    const = lambda i: (0, 0)
    return pl.pallas_call(
        _router_body, grid=(t // tm,),
        in_specs=[pl.BlockSpec((tm, d), row), pl.BlockSpec((tm, a.shape[1]), row),
                  pl.BlockSpec((tm, bm.shape[1]), row), pl.BlockSpec(wa.shape, const),
                  pl.BlockSpec(wb.shape, const), pl.BlockSpec((1, d), const),
                  pl.BlockSpec((d, LANES), const), pl.BlockSpec((d, LANES), const)],
        out_specs=[pl.BlockSpec((tm, d), row), pl.BlockSpec((tm, d), row), pl.BlockSpec((tm, LANES), row),
                   pl.BlockSpec((SUBLANES, LANES), const)],
        out_shape=[jax.ShapeDtypeStruct((t, d), F32), jax.ShapeDtypeStruct((t, d), BF16),
                   jax.ShapeDtypeStruct((t, LANES), F32), jax.ShapeDtypeStruct((SUBLANES, LANES), F32)],
        scratch_shapes=[pltpu.VMEM((SUBLANES, LANES), F32)],
        compiler_params=_cparams(("arbitrary",)), name="out_proj_router",
    )(x2, a, bm, wa, wb, gain.reshape(1, d).astype(F32), r_hi, r_lo)


def _compress_body(ch_ref, nx_ref, pos_ref, w1_ref, w2_ref, gain_ref, cos_ref, sin_ref, rot_ref, o_ref,
                   *, is_key):
    a = _dot((ch_ref[0] + pos_ref[0]).astype(BF16), w1_ref[0, 0])
    a = a + _dot((nx_ref[0] + pos_ref[1]).astype(BF16), w1_ref[0, 1])
    out = _dot(_silu(a).astype(BF16), w2_ref[...])
    if is_key:
        out = _rms(out, gain_ref[...])
        hi, lo = _split_bf16(out, 2)
        partner = _dot(hi, rot_ref[...]) + _dot(lo, rot_ref[...])
        out = out * cos_ref[...] + partner * sin_ref[...]
    o_ref[0, 0] = out


def nsa_compress(t, pos, w1, w2, gain, seq, is_key):
    b, s, _ = t.shape
    g, d = NSA_KV_GROUPS, NSA_HEAD_DIM
    n_ch = s // NSA_CMP_STRIDE
    half = NSA_CMP_STRIDE * g * d
    ch = t.reshape(b, n_ch, half)
    nxt = jnp.concatenate([ch[:, 1:], jnp.zeros((b, 1, half), F32)], axis=1)
    pos2 = jnp.broadcast_to(pos.astype(F32).reshape(2, NSA_CMP_STRIDE, 1, d),
                            (2, NSA_CMP_STRIDE, g, d)).reshape(2, 1, half)
    w1r = w1.astype(BF16).reshape(2, NSA_CMP_STRIDE, 1, d, d)
    own = (jnp.arange(g)[:, None] == jnp.arange(g)[None, :]).reshape(g, 1, 1, g, 1, 1)
    w1s = jnp.where(own, w1r[None], jnp.zeros((), BF16)).reshape(g, 2, half, d)
    cmp_end = jnp.arange(n_ch) * NSA_CMP_STRIDE + NSA_CMP_BLOCK - 1
    inv_freq = 1.0 / (ROPE_THETA ** (jnp.arange(0, d, 2, dtype=F32) / d))
    ang = cmp_end.astype(F32)[:, None] * inv_freq[None, :]
    cos = jnp.concatenate([jnp.cos(ang)] * 2, axis=1)
    sin = jnp.concatenate([jnp.sin(ang)] * 2, axis=1)
    rot = np.zeros((d, d), np.float32)
    rot[np.arange(d // 2) + d // 2, np.arange(d // 2)] = -1.0
    rot[np.arange(d // 2), np.arange(d // 2) + d // 2] = 1.0
    blk = lambda bb, gg: (bb, gg, 0, 0)
    seq = lambda bb, gg: (bb, 0, 0)
    c2 = lambda bb, gg: (0, 0)
    c3 = lambda bb, gg: (0, 0, 0)
    return pl.pallas_call(
        functools.partial(_compress_body, is_key=is_key), grid=(b, g),
        in_specs=[pl.BlockSpec((1, n_ch, half), seq), pl.BlockSpec((1, n_ch, half), seq),
                  pl.BlockSpec((2, 1, half), c3), pl.BlockSpec((1, 2, half, d), lambda bb, gg: (gg, 0, 0, 0)),
                  pl.BlockSpec((d, d), c2), pl.BlockSpec((1, d), c2),
                  pl.BlockSpec((n_ch, d), c2), pl.BlockSpec((n_ch, d), c2), pl.BlockSpec((d, d), c2)],
        out_specs=pl.BlockSpec((1, 1, n_ch, d), blk),
        out_shape=jax.ShapeDtypeStruct((b, g, n_ch, d), F32),
        compiler_params=_cparams(("parallel", "parallel")), name="nsa_compress",
    )(ch, nxt, pos2, w1s, w2.astype(BF16), gain.reshape(1, d).astype(F32), cos, sin,
      jnp.asarray(rot, dtype=BF16))


def _nsa_body(q_ref, ck_ref, cvt_ref, ksl_ref, vslt_ref, kwn_ref, vwnt_ref, ovt_ref, glt_ref, o_ref,
              sc_ref, phi_ref, plo_ref, imp_ref, bias_ref, ss_ref, ps_ref, ss1_ref, ps1_ref, sw_ref, pw_ref,
              ow_ref, oc_ref, *, tq):
    g = pl.program_id(1)
    i = pl.program_id(2)
    d = NSA_HEAD_DIM
    rep = NSA_HEADS // NSA_KV_GROUPS
    t0 = i * tq
    n_cmp = ck_ref.shape[1]
    n_sel = ovt_ref.shape[0]
    width = rep * tq
    sub = NSA_SEL_BLOCK
    dead = 0.5 * NEG_INF
    v_rows = pl.ds(pl.multiple_of(g * d, d), d)

    qb = q_ref[0]
    q4 = jnp.concatenate([qb[:, r * d:(r + 1) * d] for r in range(rep)], axis=0)
    q4 = jnp.concatenate([q4, q4], axis=1)
    lane = lax.broadcasted_iota(jnp.int32, q4.shape, 1)
    q4 = jnp.where(jnp.right_shift(lane, d.bit_length() - 1) == g, q4, jnp.zeros_like(q4))

    def qpos_of(shape):
        return t0 + (lax.broadcasted_iota(jnp.int32, shape, 1) & (tq - 1))

    def compressed(rows):
        s = _dot_nt(ck_ref[0, 0:rows, :], q4)
        cmp_end = lax.broadcasted_iota(jnp.int32, s.shape, 0) * NSA_CMP_STRIDE + (NSA_CMP_BLOCK - 1)
        s = jnp.where(cmp_end <= qpos_of(s.shape), s, NEG_INF)
        sc_ref[0:rows, :] = s
        m_c = jnp.max(s, axis=0, keepdims=True)
        lpart = jnp.zeros((SUBLANES, width), F32)
        for r in range(rows // sub):
            e = jnp.exp2(sc_ref[r * sub:(r + 1) * sub, :] - m_c)
            lpart = lpart + _fold_rows(e)
            hi = e.astype(BF16)
            phi_ref[r * sub:(r + 1) * sub, :] = hi
            plo_ref[r * sub:(r + 1) * sub, :] = (e - hi.astype(F32)).astype(BF16)
        inv_c = jnp.where(m_c > dead, 1.0 / jnp.sum(lpart, axis=0, keepdims=True), 0.0)
        oc_ref[...] = _dot(cvt_ref[0, v_rows, 0:rows], phi_ref[0:rows, :]) * inv_c
        ovt = ovt_ref[:, 0:rows]
        imp4 = (_dot(ovt, phi_ref[0:rows, :]) + _dot(ovt, plo_ref[0:rows, :])) * inv_c
        imp = imp4[:, 0:tq]
        for r in range(1, rep):
            imp = imp + imp4[:, r * tq:(r + 1) * tq]
        imp_ref[...] = imp

    cmp_live = jnp.clip((t0 + tq - NSA_CMP_BLOCK) // NSA_CMP_STRIDE + 1, 1, n_cmp)
    cmp_step = min(2 * LANES, n_cmp)
    for v in range(n_cmp // cmp_step):
        @pl.when((cmp_live > cmp_step * v) & (cmp_live <= cmp_step * (v + 1)))
        def _():
            compressed(cmp_step * (v + 1))

    o_c = oc_ref[...]
    imp = imp_ref[...]

    blk = lax.broadcasted_iota(jnp.int32, imp.shape, 0)
    qp = t0 + lax.broadcasted_iota(jnp.int32, imp.shape, 1)
    cur = jnp.right_shift(qp, NSA_SEL_BLOCK.bit_length() - 1)
    forced = (blk == 0) | (blk == cur) | (blk == cur - 1)
    future = blk * NSA_SEL_BLOCK > qp
    imp_ref[...] = jnp.where(future, -FORCE_SCORE, jnp.where(forced, FORCE_SCORE, imp))
    bias_ref[...] = jnp.full(bias_ref.shape, NEG_INF, F32)

    n_live = jnp.minimum((t0 + tq - 1) // NSA_SEL_BLOCK + 1, n_sel)
    n_var = max(n_sel // 32, 1)
    rows_per = n_sel // n_var
    top_n = float(min(NSA_TOP_N, n_sel))
    for v in range(n_var):
        rows = rows_per * (v + 1)

        @pl.when((n_live > rows_per * v) & (n_live <= rows))
        def _():
            groups = rows // SUBLANES
            mine = [imp_ref[gi * SUBLANES:(gi + 1) * SUBLANES, :] for gi in range(groups)]
            rank = [jnp.zeros((SUBLANES, tq), F32) for _ in range(groups)]
            in_group = lax.broadcasted_iota(jnp.int32, (SUBLANES, tq), 0)
            for i2 in range(rows):
                other = imp_ref[i2:i2 + 1, :]
                for gi in range(groups):
                    if gi > i2 // SUBLANES:
                        beats = other >= mine[gi]
                    elif gi < i2 // SUBLANES:
                        beats = other > mine[gi]
                    else:
                        beats = (other > mine[gi]) | ((other == mine[gi]) & (in_group > i2 % SUBLANES))
                    rank[gi] = rank[gi] + jnp.where(beats, 1.0, 0.0)
            for gi in range(groups):
                bias = jnp.where(rank[gi] < top_n, 0.0, NEG_INF)
                bias_ref[gi * SUBLANES:(gi + 1) * SUBLANES, :] = jnp.concatenate([bias] * rep, axis=1)

    init = (jnp.full((1, width), NEG_INF, F32), jnp.zeros((d + ONES_ROWS, width), F32))

    def normalised(acc):
        return acc[:d, :] * (1.0 / acc[d:d + 1, :])

    chunk = 8 * sub
    n_sub = chunk // sub

    s_slots = (ss_ref, ss1_ref)
    p_slots = (ps_ref, ps1_ref)

    def sel_scores(c, slot, diagonal=False):
        start = pl.multiple_of(c * chunk, chunk)
        s = _dot_nt(ksl_ref[0, pl.ds(start, chunk), :], q4)
        if diagonal:
            kpos = start + lax.broadcasted_iota(jnp.int32, s.shape, 0)
            s = jnp.where(kpos <= qpos_of(s.shape), s, NEG_INF)
        s_slots[slot][...] = s

    def sel_update(c, slot, carry):
        m_prev, acc = carry
        s_ref, p_ref = s_slots[slot], p_slots[slot]
        biases = [bias_ref[pl.ds(c * n_sub + r, 1), :] for r in range(n_sub)]
        m8 = jnp.full((SUBLANES, width), NEG_INF, F32)
        for r in range(n_sub):
            block = s_ref[r * sub:(r + 1) * sub, :]
            m8 = jnp.maximum(m8, jnp.max(block.reshape(sub // SUBLANES, SUBLANES, width), axis=0) + biases[r])
        m_new = jnp.maximum(m_prev, jnp.max(m8, axis=0, keepdims=True))
        live = m_new > dead
        for r in range(n_sub):
            shift = jnp.where(live, biases[r] - m_new, NEG_INF)
            p_ref[r * sub:(r + 1) * sub, :] = jnp.exp2(s_ref[r * sub:(r + 1) * sub, :] + shift).astype(BF16)
        vt = _with_ones_rows(vslt_ref[v_rows, pl.ds(pl.multiple_of(c * chunk, chunk), chunk)])
        return m_new, jnp.exp2(m_prev - m_new) * acc + _dot(vt, p_ref[...])

    c_diag = (t0 + tq - 1) // chunk
    n_chunks = c_diag + 1
    last_past = jnp.maximum(c_diag - 1, 0)

    def chunk_at(j):
        return jnp.where(j == 0, c_diag, j - 1)

    sel_scores(c_diag, 0, diagonal=True)

    def pair(k, carry):
        sel_scores(jnp.minimum(2 * k, last_past), 1)
        carry = sel_update(chunk_at(2 * k), 0, carry)
        sel_scores(jnp.minimum(2 * k + 1, last_past), 0)
        return sel_update(2 * k, 1, carry)

    carry = lax.fori_loop(0, n_chunks // 2, pair, init)
    _, acc_s = lax.cond(n_chunks % 2 == 1, lambda cr: sel_update(chunk_at(n_chunks - 1), 0, cr),
                        lambda cr: cr, carry)
    o_s = normalised(acc_s)

    def win_chunk(c, carry):
        m_prev, acc = carry
        start = pl.multiple_of(c * tq, tq)
        s = _dot_nt(kwn_ref[0, pl.ds(start, tq), :], q4)
        kpos = start + lax.broadcasted_iota(jnp.int32, s.shape, 0)
        qpos = qpos_of(s.shape)
        s = jnp.where((kpos <= qpos) & (kpos > qpos - NSA_WINDOW), s, NEG_INF)
        m_new = jnp.maximum(m_prev, jnp.max(s, axis=0, keepdims=True))
        p = jnp.exp2(s + jnp.where(m_new > dead, -m_new, NEG_INF))
        vt = _with_ones_rows(vwnt_ref[v_rows, pl.ds(start, tq)])
        return m_new, jnp.exp2(m_prev - m_new) * acc + _dot(vt, p.astype(BF16))

    n_back = NSA_WINDOW // tq

    @pl.when(i < n_back)
    def _():
        _, acc_w = lax.fori_loop(0, i + 1, win_chunk, init)
        ow_ref[...] = normalised(acc_w)

    @pl.when(i >= n_back)
    def _():
        start = pl.multiple_of(t0 - NSA_WINDOW, tq)
        s = _dot_nt(kwn_ref[0, pl.ds(start, NSA_WINDOW + tq), :], q4)
        kpos = start + lax.broadcasted_iota(jnp.int32, (tq, width), 0)
        qpos = qpos_of((tq, width))
        sw_ref[0:tq, :] = jnp.where(kpos > qpos - NSA_WINDOW, s[0:tq, :], NEG_INF)
        sw_ref[tq:NSA_WINDOW, :] = s[tq:NSA_WINDOW, :]
        sw_ref[NSA_WINDOW:, :] = jnp.where(kpos + NSA_WINDOW <= qpos, s[NSA_WINDOW:, :], NEG_INF)
        m_w = jnp.max(sw_ref[...], axis=0, keepdims=True)
        for r in range((NSA_WINDOW + tq) // sub):
            pw_ref[r * sub:(r + 1) * sub, :] = jnp.exp2(sw_ref[r * sub:(r + 1) * sub, :] - m_w).astype(BF16)
        vt = _with_ones_rows(vwnt_ref[v_rows, pl.ds(start, NSA_WINDOW + tq)])
        ow_ref[...] = normalised(_dot(vt, pw_ref[...]))

    o_w = ow_ref[...]

    def gate(branch):
        rows = [glt_ref[pl.ds((g * rep + r) * 3 + branch, 1), :] for r in range(rep)]
        return _sigmoid(jnp.concatenate(rows, axis=1))

    out = gate(0) * o_c + gate(1) * o_s + gate(2) * o_w
    out_t = jnp.concatenate([out, jnp.zeros_like(out)], axis=0).T
    o_ref[0] = jnp.concatenate([out_t[r * tq:(r + 1) * tq, :d] for r in range(rep)],
                               axis=1).astype(o_ref.dtype)


def nsa_overlap_t(n_cmp, n_sel):
    c_start = np.arange(n_cmp)[None, :] * NSA_CMP_STRIDE
    s_start = np.arange(n_sel)[:, None] * NSA_SEL_BLOCK
    hit = (c_start < s_start + NSA_SEL_BLOCK) & (c_start + NSA_CMP_BLOCK > s_start)
    hit = hit & (np.arange(n_cmp)[None, :] < n_cmp - NSA_CMP_BLOCK // NSA_CMP_STRIDE + 1)
    return jnp.asarray(hit.astype(np.float32), dtype=BF16)


def nsa_attention(qn, ck, cvt, ksl, vslt, kwn, vwnt, glt, tq=256):
    b, s, _ = qn.shape
    g, d = NSA_KV_GROUPS, NSA_HEAD_DIM
    rep = NSA_HEADS // g
    n_cmp = ck.shape[1]
    n_sel = s // NSA_SEL_BLOCK
    nq = s // tq
    ovt = nsa_overlap_t(n_cmp, n_sel)
    width = rep * tq
    chunk = 8 * NSA_SEL_BLOCK
    full3 = lambda bb, gg, i: (bb, 0, 0)
    seq_t = lambda bb, gg, i: (0, bb)
    return pl.pallas_call(
        functools.partial(_nsa_body, tq=tq), grid=(b, g, nq),
        in_specs=[pl.BlockSpec((1, tq, rep * d), lambda bb, gg, i: (bb, i, gg)),
                  pl.BlockSpec((1, n_cmp, g * d), full3), pl.BlockSpec((1, g * d, n_cmp), full3),
                  pl.BlockSpec((1, s, g * d), full3), pl.BlockSpec((g * d, s), seq_t),
                  pl.BlockSpec((1, s, g * d), full3), pl.BlockSpec((g * d, s), seq_t),
                  pl.BlockSpec((n_sel, n_cmp), lambda bb, gg, i: (0, 0)),
                  pl.BlockSpec((glt.shape[0], tq), lambda bb, gg, i: (0, bb * nq + i))],
        out_specs=pl.BlockSpec((1, tq, rep * d), lambda bb, gg, i: (bb, i, gg)),
        out_shape=jax.ShapeDtypeStruct((b, s, g * rep * d), BF16),
        scratch_shapes=[pltpu.VMEM((n_cmp, width), F32), pltpu.VMEM((n_cmp, width), BF16),
                        pltpu.VMEM((n_cmp, width), BF16), pltpu.VMEM((n_sel, tq), F32),
                        pltpu.VMEM((n_sel, width), F32), pltpu.VMEM((chunk, width), F32),
                        pltpu.VMEM((chunk, width), BF16), pltpu.VMEM((chunk, width), F32),
                        pltpu.VMEM((chunk, width), BF16), pltpu.VMEM((NSA_WINDOW + tq, width), F32),
                        pltpu.VMEM((NSA_WINDOW + tq, width), BF16), pltpu.VMEM((d, width), F32),
                        pltpu.VMEM((d, width), F32)],
        compiler_params=_cparams(("parallel", "parallel", "arbitrary")), name="nsa_attention",
    )(qn, ck, cvt, ksl, vslt, kwn, vwnt, ovt, glt)


def _pad_cols(w, n):
    return jnp.pad(w, ((0, 0), (0, n - w.shape[1])))


def _even_layer(x2, b, s, layer_idx, norm_mix, w_in, q_gain, k_gain, lam, subln_gain, conv_w, conv_b,
                dt_bias, a_log, d_skip, ssm_norm_gain, w_out, norm_ffn, w_gate, w_up, w_down):
    nq = DA_HEADS * 2 * DA_HEAD_DIM
    nv = DA_HEADS * DA_V_DIM
    cch = SSM_D_INNER + 2 * SSM_GROUPS * SSM_STATE
    offs = np.cumsum([0, nq, nq, nv, SSM_D_INNER, cch, SSM_HEADS])
    wb = w_in.astype(BF16)
    pieces = [wb[:, offs[k]:offs[k + 1]] for k in range(6)]
    pieces[2] = pieces[2].T
    pieces[5] = _pad_cols(pieces[5], LANES)
    posts = [HeadNorm(q_gain, DA_HEAD_DIM, rope=True, mul=DA_HEAD_DIM ** -0.5 * LOG2E),
             HeadNorm(k_gain, DA_HEAD_DIM, rope=True), None, None, None, None]
    q, k, vt, z, xbc, dt = norm_proj(x2, norm_mix, pieces, [BF16, BF16, BF16, F32, F32, F32], posts, s,
                                     _rope_tables(s, DA_HEAD_DIM), transposed=(2,))
    qn = q.reshape(b, s, nq)
    kn = k.reshape(b, s, nq)
    lam_init = 0.8 - 0.6 * math.exp(-0.3 * layer_idx)
    lf = lam.astype(F32)
    lam_full = jnp.exp(jnp.sum(lf[0] * lf[1])) - jnp.exp(jnp.sum(lf[2] * lf[3])) + lam_init
    a_out = flash_attention(lam_full.reshape(1), [qn], [kn], vt, subln_gain, DA_HEADS, DA_V_DIM,
                            diff=True, out_scale=1.0 - lam_init)
    b_out = ssd_mixer(xbc.reshape(b, s, cch), z.reshape(b, s, SSM_D_INNER), dt.reshape(b, s, LANES),
                      conv_w, conv_b, dt_bias, a_log, d_skip, ssm_norm_gain)
    wo = w_out.astype(BF16)
    x2 = out_proj_residual(x2, a_out.reshape(-1, nv), b_out.reshape(-1, SSM_D_INNER), wo[:nv], wo[nv:])
    return ffn_residual(x2, norm_ffn, w_gate.astype(BF16), w_up.astype(BF16), w_down.astype(BF16))


def _odd_layer(x2, b, s, norm_mix, w_in, q_gain, k_gain, cmp_pos, cmp_w1, cmp_w2, cq_gain, ckv_gain,
               w_uq, w_ukv, qn_gain, qr_gain, kn_gain, kr_gain, w_out, norm_ffn, router_w, w_gate, w_up,
               w_down):
    g, d = NSA_KV_GROUPS, NSA_HEAD_DIM
    nq = NSA_HEADS * d
    nkv = g * d
    sizes = [nq] + [nkv] * 6 + [NSA_HEADS * 3, w_uq.shape[0], w_ukv.shape[0], MLA_ROPE_DIM]
    offs = np.cumsum([0] + sizes)
    wb = w_in.astype(BF16)
    pieces = [wb[:, offs[k]:offs[k + 1]] for k in range(len(sizes))]
    for k in (4, 6):
        pieces[k] = pieces[k].T
    pieces[7] = jnp.pad(pieces[7].T, ((0, 32 - NSA_HEADS * 3), (0, 0)))
    pieces[10] = _pad_cols(pieces[10], LANES)
    tables = _rope_tables(s, d)
    posts = [None] * len(sizes)
    posts[0] = HeadNorm(q_gain, d, rope=True, mul=d ** -0.5 * LOG2E)
    posts[3] = HeadNorm(k_gain[1], d, rope=True)
    posts[5] = HeadNorm(k_gain[2], d, rope=True)
    posts[10] = HeadNorm(kr_gain, MLA_ROPE_DIM, rope=True)
    (q, kc, vc, ksl, vslt, kwn, vwnt, glt, cq, ckv, k_rope) = norm_proj(
        x2, norm_mix, pieces, [BF16, F32, F32, BF16, BF16, BF16, BF16, F32, F32, F32, BF16], posts, s, tables,
        transposed=(4, 6, 7))

    qn = q.reshape(b, s, nq)
    ksl_n = ksl.reshape(b, s, nkv)
    kwn_n = kwn.reshape(b, s, nkv)
    ck = nsa_compress(kc.reshape(b, s, nkv), cmp_pos[0], cmp_w1[0], cmp_w2[0], k_gain[0], s, True)
    cv = nsa_compress(vc.reshape(b, s, nkv), cmp_pos[1], cmp_w1[1], cmp_w2[1], k_gain[0], s, False)
    n_cmp = ck.shape[2]
    ck = ck.transpose(0, 2, 1, 3).reshape(b, n_cmp, nkv).astype(BF16)
    cvt = cv.transpose(0, 1, 3, 2).reshape(b, nkv, n_cmp).astype(BF16)
    c_out = nsa_attention(qn, ck, cvt, ksl_n, vslt, kwn_n, vwnt, glt).reshape(b * s, nq)

    h = MLA_HEADS
    dqk = MLA_NOPE_DIM + MLA_ROPE_DIM
    wq = w_uq.astype(BF16).reshape(-1, h, dqk)
    wq_nope = wq[:, :, :MLA_NOPE_DIM].reshape(-1, h * MLA_NOPE_DIM)
    wq_rope = jnp.pad(wq[:, :, MLA_NOPE_DIM:], ((0, 0), (0, 0), (0, LANES - MLA_ROPE_DIM)))
    wq_rope = wq_rope.reshape(-1, h * LANES)
    wkv = w_ukv.astype(BF16).reshape(-1, h, MLA_NOPE_DIM + MLA_V_DIM)
    wk_nope = wkv[:, :, :MLA_NOPE_DIM].reshape(-1, h * MLA_NOPE_DIM)
    wv = wkv[:, :, MLA_NOPE_DIM:].reshape(-1, h * MLA_V_DIM)
    q_mul = dqk ** -0.5 * LOG2E
    q_nope, q_rope = norm_proj(
        cq, cq_gain, [wq_nope, wq_rope], [BF16, BF16],
        [HeadNorm(qn_gain, MLA_NOPE_DIM, mul=q_mul), HeadNorm(qr_gain, MLA_ROPE_DIM, rope=True, mul=q_mul)],
        s, tables)
    k_nope, vt = norm_proj(ckv, ckv_gain, [wk_nope, wv.T], [BF16, BF16],
                           [HeadNorm(kn_gain, MLA_NOPE_DIM), None], transposed=(1,))
    shp = lambda t: t.reshape(b, s, t.shape[-1])
    d_out = flash_attention(jnp.zeros((1,), F32), [shp(q_nope), shp(q_rope)], [shp(k_nope), shp(k_rope)],
                            vt, jnp.ones((MLA_V_DIM,), F32), h, MLA_V_DIM, diff=False)

    wo = w_out.astype(BF16)
    x2, hn, route, counts = out_proj_router(x2, c_out, d_out.reshape(b * s, h * MLA_V_DIM), wo[:nq], wo[nq:],
                                            norm_ffn, router_w)
    return moe_residual(x2, hn, route, counts, w_gate.astype(BF16), w_up.astype(BF16), w_down.astype(BF16))


def kernel(x, ev_norm_mix, ev_w_in, da_q_gain, da_k_gain, da_lambda, da_subln_gain, ssm_conv_w, ssm_conv_b, ssm_dt_bias, ssm_a_log, ssm_d, ssm_norm_gain, ev_w_out, ev_norm_ffn, ffn_w_gate, ffn_w_up, ffn_w_down, od_norm_mix, od_w_in, nsa_q_gain, nsa_k_gain, nsa_cmp_pos, nsa_cmp_w1, nsa_cmp_w2, mla_cq_gain, mla_ckv_gain, mla_w_uq, mla_w_ukv, mla_qn_gain, mla_qr_gain, mla_kn_gain, mla_kr_gain, od_w_out, od_norm_ffn, moe_router, moe_w_gate, moe_w_up, moe_w_down):
    b, s, d = x.shape
    x2 = x.reshape(b * s, d)
    depth = ev_norm_mix.shape[0] + od_norm_mix.shape[0]
    for layer in range(depth):
        i = layer // 2
        if layer % 2 == 0:
            x2 = _even_layer(x2, b, s, layer, ev_norm_mix[i], ev_w_in[i], da_q_gain[i], da_k_gain[i],
                             da_lambda[i], da_subln_gain[i], ssm_conv_w[i], ssm_conv_b[i],
                             ssm_dt_bias[i], ssm_a_log[i], ssm_d[i], ssm_norm_gain[i], ev_w_out[i],
                             ev_norm_ffn[i], ffn_w_gate[i], ffn_w_up[i], ffn_w_down[i])
        else:
            x2 = _odd_layer(x2, b, s, od_norm_mix[i], od_w_in[i], nsa_q_gain[i], nsa_k_gain[i],
                            nsa_cmp_pos[i], nsa_cmp_w1[i], nsa_cmp_w2[i], mla_cq_gain[i],
                            mla_ckv_gain[i], mla_w_uq[i], mla_w_ukv[i], mla_qn_gain[i], mla_qr_gain[i],
                            mla_kn_gain[i], mla_kr_gain[i], od_w_out[i], od_norm_ffn[i], moe_router[i],
                            moe_w_gate[i], moe_w_up[i], moe_w_down[i])
    return x2.reshape(b, s, d)
```

```python
import functools
import math

import numpy as np
import jax
import jax.numpy as jnp
from jax import lax
from jax.experimental import pallas as pl
from jax.experimental.pallas import tpu as pltpu

F32 = jnp.float32
BF16 = jnp.bfloat16

ROPE_THETA = 10000.0
NORM_EPS = 1e-6
NEG_INF = -1e30
FORCE_SCORE = 1e6
LOG2E = 1.4426950408889634

DA_HEADS = 4
DA_HEAD_DIM = 64
DA_V_DIM = 2 * DA_HEAD_DIM
SSM_HEADS = 8
SSM_HEAD_DIM = 64
SSM_D_INNER = SSM_HEADS * SSM_HEAD_DIM
SSM_GROUPS = 2
SSM_STATE = 128
SSM_CONV = 4
SSM_CHUNK = 256
NSA_HEADS = 8
NSA_KV_GROUPS = 2
NSA_HEAD_DIM = 64
NSA_CMP_BLOCK = 32
NSA_CMP_STRIDE = 16
NSA_SEL_BLOCK = 64
NSA_TOP_N = 16
NSA_WINDOW = 512
MLA_HEADS = 4
MLA_NOPE_DIM = 128
MLA_ROPE_DIM = 64
MLA_V_DIM = 128
N_EXPERTS = 8

LANES = 128
SUBLANES = 8
VMEM_LIMIT = 48 * 1024 * 1024
MOE_VMEM_LIMIT = 58 * 1024 * 1024

NT_DIMS = (((1,), (1,)), ((), ()))


def _cparams(semantics):
    return pltpu.CompilerParams(dimension_semantics=semantics, vmem_limit_bytes=VMEM_LIMIT)


def _dot(a, b):
    return jnp.dot(a, b, preferred_element_type=F32)


def _dot_nt(a, b):
    return lax.dot_general(a, b, NT_DIMS, preferred_element_type=F32)


def _split_bf16(x, parts):
    out = []
    for _ in range(parts):
        hi = x.astype(BF16)
        out.append(hi)
        x = x - hi.astype(F32)
    return out


def _fold_rows(x):
    return jnp.sum(x.reshape(x.shape[0] // SUBLANES, SUBLANES, x.shape[1]), axis=0)


ONES_ROWS = 16


def _with_ones_rows(vt):
    return jnp.concatenate([vt, jnp.ones((ONES_ROWS, vt.shape[1]), vt.dtype)], axis=0)


def _sigmoid(x):
    return 1.0 / (1.0 + jnp.exp(-x))


def _silu(x):
    return x * _sigmoid(x)


def _softplus(x):
    return jnp.maximum(x, 0.0) + jnp.log(1.0 + jnp.exp(-jnp.abs(x)))


def _rms(x, gain):
    ms = jnp.mean(x * x, axis=-1, keepdims=True)
    return x * lax.rsqrt(ms + NORM_EPS) * gain


class HeadNorm:
    def __init__(self, gain, hd, rope=False, mul=1.0):
        self.gain, self.hd, self.rope, self.mul = gain, hd, rope, mul


def _head_norm(y, gain, bd, cos_ref, sin_ref, post):
    n = y.shape[1]
    hd = post.hd
    hi, lo = _split_bf16(y * y, 2)
    ss = _dot(hi, bd) + _dot(lo, bd)
    yn = y * lax.rsqrt(ss * (1.0 / hd) + NORM_EPS) * gain
    if post.rope:
        reps = n // LANES
        cos = jnp.concatenate([cos_ref[...]] * reps, axis=1) if reps > 1 else cos_ref[...]
        sin = jnp.concatenate([sin_ref[...]] * reps, axis=1) if reps > 1 else sin_ref[...]
        lane = lax.broadcasted_iota(jnp.int32, yn.shape, 1)
        first_half = (lane & (hd - 1)) < (hd // 2)
        partner = jnp.where(first_half, pltpu.roll(yn, n - hd // 2, 1), pltpu.roll(yn, hd // 2, 1))
        yn = yn * cos + partner * sin
    if post.mul != 1.0:
        yn = yn * post.mul
    return yn


def _norm_proj_body(x_ref, g_ref, *refs, posts, use_rope, transposed):
    if use_rope:
        cos_ref, sin_ref = refs[0], refs[1]
        refs = refs[2:]
    else:
        cos_ref = sin_ref = None
    n_out = len(posts)
    n_aux = 2 * sum(p is not None for p in posts)
    w_refs, aux, o_refs = refs[:n_out], refs[n_out:n_out + n_aux], refs[n_out + n_aux:]
    h = _rms(x_ref[...], g_ref[...]).astype(BF16)
    a = 0
    for k, (w_ref, o_ref, post) in enumerate(zip(w_refs, o_refs, posts)):
        if k in transposed:
            o_ref[...] = _dot_nt(w_ref[...], h).astype(o_ref.dtype)
            continue
        y = _dot(h, w_ref[...])
        if post is not None:
            y = _head_norm(y, aux[a][...], aux[a + 1][...], cos_ref, sin_ref, post)
            a += 2
        o_ref[...] = y.astype(o_ref.dtype)


def norm_proj(x2, gain, weights, out_dtypes, posts=None, seq=None, rope_tables=None, transposed=(), tm=512):
    t, d = x2.shape
    posts = posts or [None] * len(weights)
    transposed = frozenset(transposed)
    use_rope = any(p is not None and p.rope for p in posts)
    const = lambda i: (0, 0)
    args = [x2, gain.reshape(1, d).astype(F32)]
    in_specs = [pl.BlockSpec((tm, d), lambda i: (i, 0)), pl.BlockSpec((1, d), const)]
    if use_rope:
        per_seq = seq // tm
        args += list(rope_tables)
        in_specs += [pl.BlockSpec((tm, LANES), lambda i: (i % per_seq, 0))] * 2
    args += list(weights)
    in_specs += [pl.BlockSpec(w.shape, const) for w in weights]
    for w, p in zip(weights, posts):
        if p is not None:
            n = w.shape[1]
            args += [jnp.tile(p.gain.astype(F32), n // p.hd).reshape(1, n), _block_diag_ones(n, p.hd)]
            in_specs += [pl.BlockSpec((1, n), const), pl.BlockSpec((n, n), const)]
    out_specs, out_shape = [], []
    for k, (w, dt) in enumerate(zip(weights, out_dtypes)):
        if k in transposed:
            out_specs.append(pl.BlockSpec((w.shape[0], tm), lambda i: (0, i)))
            out_shape.append(jax.ShapeDtypeStruct((w.shape[0], t), dt))
        else:
            out_specs.append(pl.BlockSpec((tm, w.shape[1]), lambda i: (i, 0)))
            out_shape.append(jax.ShapeDtypeStruct((t, w.shape[1]), dt))
    return pl.pallas_call(
        functools.partial(_norm_proj_body, posts=tuple(posts), use_rope=use_rope, transposed=transposed),
        grid=(t // tm,), in_specs=in_specs, out_specs=out_specs, out_shape=out_shape,
        compiler_params=_cparams(("parallel",)), name="norm_proj",
    )(*args)


def _block_diag_ones(n, hd):
    idx = np.arange(n) // hd
    return jnp.asarray((idx[:, None] == idx[None, :]).astype(np.float32), dtype=BF16)


def _rope_tables(seq, hd):
    inv_freq = 1.0 / (ROPE_THETA ** (jnp.arange(0, hd, 2, dtype=F32) / hd))
    ang = jnp.arange(seq, dtype=F32)[:, None] * inv_freq[None, :]
    cos, sin = jnp.cos(ang), jnp.sin(ang)
    reps = LANES // hd
    cos_t = jnp.tile(jnp.concatenate([cos, cos], axis=1), (1, reps))
    sin_t = jnp.tile(jnp.concatenate([-sin, sin], axis=1), (1, reps))
    return cos_t, sin_t


def _flash_body(lam_ref, *refs, n_qk, diff, out_scale, sub):
    q_refs = refs[:n_qk]
    k_refs = refs[n_qk:2 * n_qk]
    vt_ref, gain_ref, o_ref, m_ref, l_ref, acc_ref, s0_ref, s1_ref, p0_ref, p1_ref = refs[2 * n_qk:]
    i = pl.program_id(2)
    n_sm = 2 if diff else 1
    _, tk, tq = s0_ref.shape
    s_slots = (s0_ref, s1_ref)
    p_slots = (p0_ref, p1_ref)

    m_ref[...] = jnp.full(m_ref.shape, NEG_INF, F32)
    l_ref[...] = jnp.zeros(l_ref.shape, F32)
    acc_ref[...] = jnp.zeros(acc_ref.shape, F32)

    qs = [r[0] for r in q_refs]
    q = qs[0] if n_qk == 1 else jnp.concatenate(qs, axis=1)
    if diff:
        lane = lax.broadcasted_iota(jnp.int32, q.shape, 1)
        half = q.shape[1] // 2
        zero = jnp.zeros_like(q)
        q_parts = [jnp.where(lane < half, q, zero), jnp.where(lane >= half, q, zero)]
    else:
        q_parts = [q]

    def scores(c, slot, key_offset=None):
        rows = pl.ds(pl.multiple_of(c * tk, tk), tk)
        ks = [r[0, rows, :] for r in k_refs]
        k = ks[0] if n_qk == 1 else jnp.concatenate(ks, axis=1)
        for sm in range(n_sm):
            s = _dot_nt(k, q_parts[sm])
            if key_offset is not None:
                row = lax.broadcasted_iota(jnp.int32, s.shape, 0)
                col = lax.broadcasted_iota(jnp.int32, s.shape, 1)
                s = jnp.where(row + key_offset <= col, s, NEG_INF)
            s_slots[slot][sm] = s

    def update(c, slot):
        vt = vt_ref[:, pl.ds(pl.multiple_of(c * tk, tk), tk)]
        for sm in range(n_sm):
            s_ref, p_ref = s_slots[slot], p_slots[slot]
            m_prev = m_ref[sm]
            m_new = jnp.maximum(m_prev, jnp.max(s_ref[sm], axis=0, keepdims=True))
            m_ref[sm] = m_new
            alpha = jnp.exp2(m_prev - m_new)
            lpart = jnp.zeros((SUBLANES, tq), F32)
            for r in range(tk // sub):
                p = jnp.exp2(s_ref[sm, r * sub:(r + 1) * sub, :] - m_new)
                lpart = lpart + _fold_rows(p)
                p_ref[sm, r * sub:(r + 1) * sub, :] = p.astype(BF16)
            l_ref[sm] = alpha * l_ref[sm] + jnp.sum(lpart, axis=0, keepdims=True)
            acc_ref[sm] = alpha * acc_ref[sm] + _dot(vt, p_ref[sm])

    if tq == tk:
        n_chunks = i + 1
        last_past = jnp.maximum(i - 1, 0)

        def chunk_at(j):
            return jnp.where(j == 0, i, j - 1)

        scores(i, 0, key_offset=0)

        def pair(k2, carry):
            scores(jnp.minimum(2 * k2, last_past), 1)
            update(chunk_at(2 * k2), 0)
            scores(jnp.minimum(2 * k2 + 1, last_past), 0)
            update(2 * k2, 1)
            return carry

        lax.fori_loop(0, n_chunks // 2, pair, 0)

        @pl.when(n_chunks % 2 == 1)
        def _():
            update(chunk_at(n_chunks - 1), 0)
    else:
        n_past = 2 * i
        last_past = jnp.maximum(n_past - 1, 0)
        scores(n_past, 0, key_offset=0)
        scores(n_past + 1, 1, key_offset=tk)
        update(n_past, 0)
        scores(0, 0)
        update(n_past + 1, 1)

        def pair(k2, carry):
            scores(2 * k2 + 1, 1)
            update(2 * k2, 0)
            scores(jnp.minimum(2 * k2 + 2, last_past), 0)
            update(2 * k2 + 1, 1)
            return carry

        lax.fori_loop(0, i, pair, 0)

    o = acc_ref[0] * (1.0 / l_ref[0])
    if diff:
        o = o - acc_ref[1] * (lam_ref[0] / l_ref[1])
        ms = jnp.mean(o * o, axis=0, keepdims=True)
        o = o * lax.rsqrt(ms + NORM_EPS) * gain_ref[...] * out_scale
    o_ref[0] = o.T.astype(o_ref.dtype)


def flash_attention(lam, qs, ks, vt, gain, n_heads, dv, *, diff, out_scale=1.0, tk=512):
    b, s, _ = qs[0].shape
    tile = tk if diff else 2 * tk
    sub = 64 if diff else 32
    nt = s // tile
    n_qk = len(qs)
    in_specs = [pl.BlockSpec(memory_space=pltpu.SMEM)]
    for q in qs:
        w = q.shape[2] // n_heads
        in_specs.append(pl.BlockSpec((1, tile, w), lambda bb, h, i: (bb, i, h)))
    for q, k in zip(qs, ks):
        w = q.shape[2] // n_heads
        if k.shape[2] == w:
            in_specs.append(pl.BlockSpec((1, s, w), lambda bb, h, i: (bb, 0, 0)))
        else:
            in_specs.append(pl.BlockSpec((1, s, w), lambda bb, h, i: (bb, 0, h)))
    in_specs.append(pl.BlockSpec((dv, s), lambda bb, h, i: (h, bb)))
    in_specs.append(pl.BlockSpec((dv, 1), lambda bb, h, i: (0, 0)))
    n_sm = 2 if diff else 1
    return pl.pallas_call(
        functools.partial(_flash_body, n_qk=n_qk, diff=diff, out_scale=out_scale, sub=sub),
        grid=(b, n_heads, nt), in_specs=in_specs,
        out_specs=pl.BlockSpec((1, tile, dv), lambda bb, h, i: (bb, i, h)),
        out_shape=jax.ShapeDtypeStruct((b, s, n_heads * dv), BF16),
        scratch_shapes=[pltpu.VMEM((n_sm, 1, tile), F32), pltpu.VMEM((n_sm, 1, tile), F32),
                        pltpu.VMEM((n_sm, dv, tile), F32),
                        pltpu.VMEM((n_sm, tk, tile), F32), pltpu.VMEM((n_sm, tk, tile), F32),
                        pltpu.VMEM((n_sm, tk, tile), BF16), pltpu.VMEM((n_sm, tk, tile), BF16)],
        compiler_params=_cparams(("parallel", "parallel", "arbitrary")),
        name="flash_diff" if diff else "flash_plain",
    )(lam, *qs, *ks, vt, gain.reshape(dv, 1).astype(F32))


def _ssd_body(xbc_ref, z_ref, dt_ref, dtt_ref, cw_ref, cb_ref, dtb_ref, dtbt_ref, al_ref, alt_ref,
              dsk_ref, ng_ref, o_ref, xpad_ref, state_ref):
    chunk = xbc_ref.shape[1]
    d_in = z_ref.shape[2]
    gn = SSM_GROUPS * SSM_STATE
    c = pl.program_id(1)

    @pl.when(c == 0)
    def _():
        xpad_ref[0:8, :] = jnp.zeros((8, xpad_ref.shape[1]), F32)
        state_ref[...] = jnp.zeros(state_ref.shape, F32)

    xpad_ref[8:8 + chunk, :] = xbc_ref[0]
    conv = cb_ref[...]
    for w in range(SSM_CONV):
        conv = conv + cw_ref[w:w + 1, :] * xpad_ref[pl.ds(8 - (SSM_CONV - 1) + w, chunk), :]
    xpad_ref[0:8, :] = xpad_ref[chunk:chunk + 8, :]
    u = _silu(conv)
    xs = u[:, :d_in]
    bmat = u[:, d_in:d_in + gn]
    cmat = u[:, d_in + gn:]

    dt = _softplus(dt_ref[0] + dtb_ref[...])
    ad = dt * (-jnp.exp(al_ref[...]))
    dtt = _softplus(dtt_ref[0] + dtbt_ref[...])
    adt = dtt * (-jnp.exp(alt_ref[...]))
    row = lax.broadcasted_iota(jnp.int32, (chunk, chunk), 0)
    col = lax.broadcasted_iota(jnp.int32, (chunk, chunk), 1)
    lower = row >= col
    tril = jnp.where(lower, 1.0, 0.0).astype(BF16)
    triu = jnp.where(row <= col, 1.0, 0.0).astype(BF16)
    cs = sum(_dot(tril, part) for part in _split_bf16(ad, 3))
    cst = sum(_dot(part, triu) for part in _split_bf16(adt, 3))

    heads_per_group = SSM_HEADS // SSM_GROUPS
    dsk = dsk_ref[...]
    ys = []
    for g in range(SSM_GROUPS):
        bg = bmat[:, g * SSM_STATE:(g + 1) * SSM_STATE]
        cg = cmat[:, g * SSM_STATE:(g + 1) * SSM_STATE].astype(BF16)
        cb = _dot_nt(cg, bg.astype(BF16))
        bgt = bg.T.astype(BF16)
        for r in range(heads_per_group):
            h = g * heads_per_group + r
            ccol = cs[:, h:h + 1]
            crow = cst[h:h + 1, :]
            decay = jnp.exp(jnp.where(lower, ccol - crow, NEG_INF))
            x_h = xs[:, h * SSM_HEAD_DIM:(h + 1) * SSM_HEAD_DIM]
            xdt = x_h * dt[:, h:h + 1]
            y = _dot((cb * decay).astype(BF16), xdt.astype(BF16))
            st = state_ref[h]
            y = y + _dot(cg, st.astype(BF16)) * jnp.exp(ccol)
            last = cst[h:h + 1, chunk - 1:chunk]
            to_end = jnp.exp(last - ccol)
            state_ref[h] = st * jnp.exp(last) + _dot(bgt, (xdt * to_end).astype(BF16))
            ys.append(y + x_h * dsk[:, h * SSM_HEAD_DIM:(h + 1) * SSM_HEAD_DIM])

    y = jnp.concatenate(ys, axis=1) * _silu(z_ref[0])
    gw = d_in // SSM_GROUPS
    for g in range(SSM_GROUPS):
        seg = y[:, g * gw:(g + 1) * gw]
        o_ref[0, :, g * gw:(g + 1) * gw] = _rms(seg, ng_ref[:, g * gw:(g + 1) * gw]).astype(o_ref.dtype)


def ssd_mixer(xbc, z, dt_raw, conv_w, conv_b, dt_bias, a_log, d_skip, norm_gain):
    b, s, cch = xbc.shape
    d_in = z.shape[2]
    nc = s // SSM_CHUNK
    hpad = dt_raw.shape[2]
    dtt = jnp.transpose(dt_raw[:, :, :SSM_HEADS], (0, 2, 1))

    def lane_pad(v):
        return jnp.pad(v.astype(F32), (0, hpad - SSM_HEADS)).reshape(1, hpad)

    args = (xbc, z, dt_raw, dtt, conv_w.astype(F32), conv_b.reshape(1, cch).astype(F32),
            lane_pad(dt_bias), dt_bias.reshape(SSM_HEADS, 1).astype(F32),
            lane_pad(a_log), a_log.reshape(SSM_HEADS, 1).astype(F32),
            jnp.repeat(d_skip.astype(F32), SSM_HEAD_DIM).reshape(1, d_in),
            norm_gain.reshape(1, d_in).astype(F32))
    const = lambda bb, c: (0, 0)
    in_specs = [pl.BlockSpec((1, SSM_CHUNK, cch), lambda bb, c: (bb, c, 0)),
                pl.BlockSpec((1, SSM_CHUNK, d_in), lambda bb, c: (bb, c, 0)),
                pl.BlockSpec((1, SSM_CHUNK, hpad), lambda bb, c: (bb, c, 0)),
                pl.BlockSpec((1, SSM_HEADS, SSM_CHUNK), lambda bb, c: (bb, 0, c)),
                pl.BlockSpec((SSM_CONV, cch), const), pl.BlockSpec((1, cch), const),
                pl.BlockSpec((1, hpad), const), pl.BlockSpec((SSM_HEADS, 1), const),
                pl.BlockSpec((1, hpad), const), pl.BlockSpec((SSM_HEADS, 1), const),
                pl.BlockSpec((1, d_in), const), pl.BlockSpec((1, d_in), const)]
    return pl.pallas_call(
        _ssd_body, grid=(b, nc), in_specs=in_specs,
        out_specs=pl.BlockSpec((1, SSM_CHUNK, d_in), lambda bb, c: (bb, c, 0)),
        out_shape=jax.ShapeDtypeStruct((b, s, d_in), BF16),
        scratch_shapes=[pltpu.VMEM((SSM_CHUNK + 8, cch), F32),
                        pltpu.VMEM((SSM_HEADS, SSM_STATE, SSM_HEAD_DIM), F32)],
        compiler_params=_cparams(("parallel", "arbitrary")), name="ssd_mixer",
    )(*args)


def _ffn_body(x_ref, a_ref, b_ref, wa_ref, wb_ref, g_ref, wg_ref, wu_ref, wd_ref, o_ref, h_ref):
    f = pl.program_id(1)
    half = h_ref.shape[0] // 2
    halves = (slice(0, half), slice(half, 2 * half))

    @pl.when(f == 0)
    def _():
        for r in halves:
            x = x_ref[r, :] + _dot(a_ref[r, :], wa_ref[...]) + _dot(b_ref[r, :], wb_ref[...])
            h_ref[r, :] = _rms(x, g_ref[...]).astype(BF16)
            o_ref[r, :] = x

    for r in halves:
        h = h_ref[r, :]
        act = (_silu(_dot(h, wg_ref[...])) * _dot(h, wu_ref[...])).astype(BF16)
        o_ref[r, :] += _dot(act, wd_ref[...])


def out_proj_ffn_residual(x2, a, bm, wa, wb, gain, w_gate, w_up, w_down, tm=1024, tf=1408):
    t, d = x2.shape
    d_ff = w_gate.shape[1]
    row = lambda i, f: (i, 0)
    const = lambda i, f: (0, 0)
    return pl.pallas_call(
        _ffn_body, grid=(t // tm, d_ff // tf),
        in_specs=[pl.BlockSpec((tm, d), row), pl.BlockSpec((tm, a.shape[1]), row),
                  pl.BlockSpec((tm, bm.shape[1]), row), pl.BlockSpec(wa.shape, const),
                  pl.BlockSpec(wb.shape, const), pl.BlockSpec((1, d), const),
                  pl.BlockSpec((d, tf), lambda i, f: (0, f)),
                  pl.BlockSpec((d, tf), lambda i, f: (0, f)),
                  pl.BlockSpec((tf, d), lambda i, f: (f, 0))],
        out_specs=pl.BlockSpec((tm, d), row),
        out_shape=jax.ShapeDtypeStruct((t, d), F32),
        scratch_shapes=[pltpu.VMEM((tm, d), BF16)],
        compiler_params=pltpu.CompilerParams(
            dimension_semantics=("parallel", "arbitrary"), vmem_limit_bytes=MOE_VMEM_LIMIT),
        name="out_proj_ffn",
    )(x2, a, bm, wa, wb, gain.reshape(1, d).astype(F32), w_gate, w_up, w_down)


MOE_ROWS = 256
ROUTE_IDX = 0
ROUTE_W = 2
ROUTE_RANK = 4


def _moe_ffn_body(block_expert_ref, n_used_ref, xs_ref, wg_ref, wu_ref, wd_ref, o_ref):
    i = pl.program_id(0)

    @pl.when(i < n_used_ref[0])
    def _():
        x = xs_ref[...]
        act = (_silu(_dot(x, wg_ref[0])) * _dot(x, wu_ref[0])).astype(BF16)
        o_ref[...] = _dot(act, wd_ref[0]).astype(o_ref.dtype)

    @pl.when(i >= n_used_ref[0])
    def _():
        o_ref[...] = jnp.zeros(o_ref.shape, o_ref.dtype)


def moe_expert_ffn(xs, block_expert, n_used, w_gate, w_up, w_down):
    p, d = xs.shape
    d_ff = w_gate.shape[2]
    rows = MOE_ROWS
    grid_spec = pltpu.PrefetchScalarGridSpec(
        num_scalar_prefetch=2, grid=(p // rows,),
        in_specs=[pl.BlockSpec((rows, d), lambda i, be, nu: (i, 0)),
                  pl.BlockSpec((1, d, d_ff), lambda i, be, nu: (be[i], 0, 0)),
                  pl.BlockSpec((1, d, d_ff), lambda i, be, nu: (be[i], 0, 0)),
                  pl.BlockSpec((1, d_ff, d), lambda i, be, nu: (be[i], 0, 0))],
        out_specs=pl.BlockSpec((rows, d), lambda i, be, nu: (i, 0)))
    return pl.pallas_call(
        _moe_ffn_body, grid_spec=grid_spec, out_shape=jax.ShapeDtypeStruct((p, d), BF16),
        compiler_params=pltpu.CompilerParams(
            dimension_semantics=("arbitrary",), vmem_limit_bytes=MOE_VMEM_LIMIT),
        name="moe_expert_ffn",
    )(block_expert, n_used, xs, w_gate, w_up, w_down)


def _moe_combine_body(x_ref, y0_ref, y1_ref, route_ref, o_ref):
    route = route_ref[...]
    lane = lax.broadcasted_iota(jnp.int32, route.shape, 1)
    w0 = jnp.sum(jnp.where(lane == ROUTE_W, route, 0.0), axis=1, keepdims=True)
    w1 = jnp.sum(jnp.where(lane == ROUTE_W + 1, route, 0.0), axis=1, keepdims=True)
    o_ref[...] = x_ref[...] + w0 * y0_ref[...].astype(F32) + w1 * y1_ref[...].astype(F32)


def moe_combine(x2, y0, y1, route, tm=512):
    t, d = x2.shape
    row = lambda i: (i, 0)
    return pl.pallas_call(
        _moe_combine_body, grid=(t // tm,),
        in_specs=[pl.BlockSpec((tm, d), row), pl.BlockSpec((tm, d), row), pl.BlockSpec((tm, d), row),
                  pl.BlockSpec((tm, LANES), row)],
        out_specs=pl.BlockSpec((tm, d), row),
        out_shape=jax.ShapeDtypeStruct((t, d), F32),
        compiler_params=_cparams(("parallel",)), name="moe_combine",
    )(x2, y0, y1, route)


def moe_residual(x2, h, route, counts, w_gate, w_up, w_down):
    t, d = x2.shape
    n_e = w_gate.shape[0]
    rows = MOE_ROWS
    expert = route[:, ROUTE_IDX:ROUTE_IDX + 2].astype(jnp.int32).reshape(-1)
    rank = route[:, ROUTE_RANK:ROUTE_RANK + 2].astype(jnp.int32).reshape(-1)
    count = counts[0, :n_e].astype(jnp.int32)
    padded = (count + rows - 1) // rows * rows
    ends = jnp.cumsum(padded)
    starts = ends - padded
    own = expert[:, None] == jnp.arange(n_e, dtype=jnp.int32)[None, :]
    slot = jnp.sum(jnp.where(own, starts[None, :], 0), axis=1) + rank
    p_rows = 2 * t + n_e * rows
    block_start = jnp.arange(p_rows // rows, dtype=jnp.int32) * rows
    block_expert = jnp.minimum(jnp.sum(block_start[:, None] >= ends[None, :], axis=1), n_e - 1).astype(jnp.int32)
    n_used = (ends[-1] // rows).astype(jnp.int32).reshape(1)
    copy_order = jnp.argsort(slot).astype(jnp.int32)
    within = (block_start - starts[block_expert])[:, None] + jnp.arange(rows, dtype=jnp.int32)[None, :]
    dense = (jnp.cumsum(count) - count)[block_expert][:, None] + within
    real = within < count[block_expert][:, None]
    picked = copy_order.at[jnp.clip(dense, 0, 2 * t - 1)].get(mode="promise_in_bounds")
    token_of_slot = jnp.where(real, picked // 2, 0).reshape(p_rows)

    take_rows = lambda a, idx: a.at[idx].get(mode="promise_in_bounds")
    xs = take_rows(h, token_of_slot)
    ys = moe_expert_ffn(xs, block_expert, n_used, w_gate, w_up, w_down)
    slot2 = slot.reshape(t, 2)
    return moe_combine(x2, take_rows(ys, slot2[:, 0]), take_rows(ys, slot2[:, 1]), route)


def _router_body(x_ref, a_ref, b_ref, wa_ref, wb_ref, g_ref, rhi_ref, rlo_ref,
                 xo_ref, h_ref, o_ref, count_ref, run_ref):
    @pl.when(pl.program_id(0) == 0)
    def _():
        run_ref[...] = jnp.zeros(run_ref.shape, F32)

    x = x_ref[...] + _dot(a_ref[...], wa_ref[...]) + _dot(b_ref[...], wb_ref[...])
    xo_ref[...] = x
    h = _rms(x, g_ref[...])
    h_ref[...] = h.astype(h_ref.dtype)
    h_hi, h_lo = _split_bf16(h, 2)
    logits = _dot(h_hi, rhi_ref[...]) + _dot(h_hi, rlo_ref[...]) + _dot(h_lo, rhi_ref[...])
    lane = lax.broadcasted_iota(jnp.int32, logits.shape, 1).astype(F32)
    low = jnp.float32(-3.0e38)
    logits = jnp.where(lane < N_EXPERTS, logits, low)
    m1 = jnp.max(logits, axis=1, keepdims=True)
    i1 = jnp.min(jnp.where(logits == m1, lane, float(LANES)), axis=1, keepdims=True)
    rest = jnp.where(lane == i1, low, logits)
    m2 = jnp.max(rest, axis=1, keepdims=True)
    i2 = jnp.min(jnp.where(rest == m2, lane, float(LANES)), axis=1, keepdims=True)
    ex = jnp.exp(m2 - m1)
    w1 = 1.0 / (1.0 + ex)
    w2 = ex / (1.0 + ex)
    tm = logits.shape[0]
    routed = jnp.where((lane == i1) | (lane == i2), 1.0, 0.0)
    row = lax.broadcasted_iota(jnp.int32, (tm, tm), 0)
    col = lax.broadcasted_iota(jnp.int32, (tm, tm), 1)
    before = jnp.where(col < row, 1.0, 0.0).astype(BF16)
    rank = run_ref[0:1, :] + _dot(before, routed.astype(BF16))
    r1 = jnp.sum(jnp.where(lane == i1, rank, 0.0), axis=1, keepdims=True)
    r2 = jnp.sum(jnp.where(lane == i2, rank, 0.0), axis=1, keepdims=True)
    fields = ((ROUTE_IDX, i1), (ROUTE_IDX + 1, i2), (ROUTE_W, w1), (ROUTE_W + 1, w2),
              (ROUTE_RANK, r1), (ROUTE_RANK + 1, r2))
    out = jnp.zeros(logits.shape, F32)
    for pos, val in fields:
        out = jnp.where(lane == pos, val, out)
    o_ref[...] = out
    run_ref[...] = run_ref[...] + jnp.sum(routed, axis=0, keepdims=True)
    count_ref[...] = run_ref[...]


def out_proj_router(x2, a, bm, wa, wb, gain, router_w, tm=512):
    t, d = x2.shape
    r_pad = jnp.pad(router_w.astype(F32), ((0, 0), (0, LANES - router_w.shape[1])))
    r_hi, r_lo = _split_bf16(r_pad, 2)
    row = lambda i: (i, 0)
    const = lambda i: (0, 0)
    return pl.pallas_call(
        _router_body, grid=(t // tm,),
        in_specs=[pl.BlockSpec((tm, d), row), pl.BlockSpec((tm, a.shape[1]), row),
                  pl.BlockSpec((tm, bm.shape[1]), row), pl.BlockSpec(wa.shape, const),
                  pl.BlockSpec(wb.shape, const), pl.BlockSpec((1, d), const),
                  pl.BlockSpec((d, LANES), const), pl.BlockSpec((d, LANES), const)],
        out_specs=[pl.BlockSpec((tm, d), row), pl.BlockSpec((tm, d), row), pl.BlockSpec((tm, LANES), row),
                   pl.BlockSpec((SUBLANES, LANES), const)],
        out_shape=[jax.ShapeDtypeStruct((t, d), F32), jax.ShapeDtypeStruct((t, d), BF16),
                   jax.ShapeDtypeStruct((t, LANES), F32), jax.ShapeDtypeStruct((SUBLANES, LANES), F32)],
        scratch_shapes=[pltpu.VMEM((SUBLANES, LANES), F32)],
        compiler_params=_cparams(("arbitrary",)), name="out_proj_router",
    )(x2, a, bm, wa, wb, gain.reshape(1, d).astype(F32), r_hi, r_lo)


def _compress_body(ch_ref, nx_ref, pos_ref, w1_ref, w2_ref, gain_ref, cos_ref, sin_ref, rot_ref, o_ref,
                   *, is_key):
    a = _dot((ch_ref[0] + pos_ref[0]).astype(BF16), w1_ref[0, 0])
    a = a + _dot((nx_ref[0] + pos_ref[1]).astype(BF16), w1_ref[0, 1])
    out = _dot(_silu(a).astype(BF16), w2_ref[...])
    if is_key:
        out = _rms(out, gain_ref[...])
        hi, lo = _split_bf16(out, 2)
        partner = _dot(hi, rot_ref[...]) + _dot(lo, rot_ref[...])
        out = out * cos_ref[...] + partner * sin_ref[...]
    o_ref[0, 0] = out


def nsa_compress(t, pos, w1, w2, gain, seq, is_key):
    b, s, _ = t.shape
    g, d = NSA_KV_GROUPS, NSA_HEAD_DIM
    n_ch = s // NSA_CMP_STRIDE
    half = NSA_CMP_STRIDE * g * d
    ch = t.reshape(b, n_ch, half)
    nxt = jnp.concatenate([ch[:, 1:], jnp.zeros((b, 1, half), F32)], axis=1)
    pos2 = jnp.broadcast_to(pos.astype(F32).reshape(2, NSA_CMP_STRIDE, 1, d),
                            (2, NSA_CMP_STRIDE, g, d)).reshape(2, 1, half)
    w1r = w1.astype(BF16).reshape(2, NSA_CMP_STRIDE, 1, d, d)
    own = (jnp.arange(g)[:, None] == jnp.arange(g)[None, :]).reshape(g, 1, 1, g, 1, 1)
    w1s = jnp.where(own, w1r[None], jnp.zeros((), BF16)).reshape(g, 2, half, d)
    cmp_end = jnp.arange(n_ch) * NSA_CMP_STRIDE + NSA_CMP_BLOCK - 1
    inv_freq = 1.0 / (ROPE_THETA ** (jnp.arange(0, d, 2, dtype=F32) / d))
    ang = cmp_end.astype(F32)[:, None] * inv_freq[None, :]
    cos = jnp.concatenate([jnp.cos(ang)] * 2, axis=1)
    sin = jnp.concatenate([jnp.sin(ang)] * 2, axis=1)
    rot = np.zeros((d, d), np.float32)
    rot[np.arange(d // 2) + d // 2, np.arange(d // 2)] = -1.0
    rot[np.arange(d // 2), np.arange(d // 2) + d // 2] = 1.0
    blk = lambda bb, gg: (bb, gg, 0, 0)
    seq = lambda bb, gg: (bb, 0, 0)
    c2 = lambda bb, gg: (0, 0)
    c3 = lambda bb, gg: (0, 0, 0)
    return pl.pallas_call(
        functools.partial(_compress_body, is_key=is_key), grid=(b, g),
        in_specs=[pl.BlockSpec((1, n_ch, half), seq), pl.BlockSpec((1, n_ch, half), seq),
                  pl.BlockSpec((2, 1, half), c3), pl.BlockSpec((1, 2, half, d), lambda bb, gg: (gg, 0, 0, 0)),
                  pl.BlockSpec((d, d), c2), pl.BlockSpec((1, d), c2),
                  pl.BlockSpec((n_ch, d), c2), pl.BlockSpec((n_ch, d), c2), pl.BlockSpec((d, d), c2)],
        out_specs=pl.BlockSpec((1, 1, n_ch, d), blk),
        out_shape=jax.ShapeDtypeStruct((b, g, n_ch, d), F32),
        compiler_params=_cparams(("parallel", "parallel")), name="nsa_compress",
    )(ch, nxt, pos2, w1s, w2.astype(BF16), gain.reshape(1, d).astype(F32), cos, sin,
      jnp.asarray(rot, dtype=BF16))


def _nsa_body(q_ref, ck_ref, cvt_ref, ksl_ref, vslt_ref, kwn_ref, vwnt_ref, ovt_ref, glt_ref, o_ref,
              sc_ref, phi_ref, plo_ref, imp_ref, bias_ref, ss_ref, ps_ref, ss1_ref, ps1_ref, sw_ref, pw_ref,
              ow_ref, oc_ref, *, tq):
    g = pl.program_id(1)
    i = pl.program_id(2)
    d = NSA_HEAD_DIM
    rep = NSA_HEADS // NSA_KV_GROUPS
    t0 = i * tq
    n_cmp = ck_ref.shape[1]
    n_sel = ovt_ref.shape[0]
    width = rep * tq
    sub = NSA_SEL_BLOCK
    dead = 0.5 * NEG_INF
    v_rows = pl.ds(pl.multiple_of(g * d, d), d)

    qb = q_ref[0]
    q4 = jnp.concatenate([qb[:, r * d:(r + 1) * d] for r in range(rep)], axis=0)
    q4 = jnp.concatenate([q4, q4], axis=1)
    lane = lax.broadcasted_iota(jnp.int32, q4.shape, 1)
    q4 = jnp.where(jnp.right_shift(lane, d.bit_length() - 1) == g, q4, jnp.zeros_like(q4))

    def qpos_of(shape):
        return t0 + (lax.broadcasted_iota(jnp.int32, shape, 1) & (tq - 1))

    def compressed(rows):
        s = _dot_nt(ck_ref[0, 0:rows, :], q4)
        cmp_end = lax.broadcasted_iota(jnp.int32, s.shape, 0) * NSA_CMP_STRIDE + (NSA_CMP_BLOCK - 1)
        s = jnp.where(cmp_end <= qpos_of(s.shape), s, NEG_INF)
        sc_ref[0:rows, :] = s
        m_c = jnp.max(s, axis=0, keepdims=True)
        lpart = jnp.zeros((SUBLANES, width), F32)
        for r in range(rows // sub):
            e = jnp.exp2(sc_ref[r * sub:(r + 1) * sub, :] - m_c)
            lpart = lpart + _fold_rows(e)
            hi = e.astype(BF16)
            phi_ref[r * sub:(r + 1) * sub, :] = hi
            plo_ref[r * sub:(r + 1) * sub, :] = (e - hi.astype(F32)).astype(BF16)
        inv_c = jnp.where(m_c > dead, 1.0 / jnp.sum(lpart, axis=0, keepdims=True), 0.0)
        oc_ref[...] = _dot(cvt_ref[0, v_rows, 0:rows], phi_ref[0:rows, :]) * inv_c
        ovt = ovt_ref[:, 0:rows]
        imp4 = (_dot(ovt, phi_ref[0:rows, :]) + _dot(ovt, plo_ref[0:rows, :])) * inv_c
        imp = imp4[:, 0:tq]
        for r in range(1, rep):
            imp = imp + imp4[:, r * tq:(r + 1) * tq]
        imp_ref[...] = imp

    cmp_live = jnp.clip((t0 + tq - NSA_CMP_BLOCK) // NSA_CMP_STRIDE + 1, 1, n_cmp)
    cmp_step = min(2 * LANES, n_cmp)
    for v in range(n_cmp // cmp_step):
        @pl.when((cmp_live > cmp_step * v) & (cmp_live <= cmp_step * (v + 1)))
        def _():
            compressed(cmp_step * (v + 1))

    o_c = oc_ref[...]
    imp = imp_ref[...]

    blk = lax.broadcasted_iota(jnp.int32, imp.shape, 0)
    qp = t0 + lax.broadcasted_iota(jnp.int32, imp.shape, 1)
    cur = jnp.right_shift(qp, NSA_SEL_BLOCK.bit_length() - 1)
    forced = (blk == 0) | (blk == cur) | (blk == cur - 1)
    future = blk * NSA_SEL_BLOCK > qp
    imp_ref[...] = jnp.where(future, -FORCE_SCORE, jnp.where(forced, FORCE_SCORE, imp))
    bias_ref[...] = jnp.full(bias_ref.shape, NEG_INF, F32)

    n_live = jnp.minimum((t0 + tq - 1) // NSA_SEL_BLOCK + 1, n_sel)
    n_var = max(n_sel // 32, 1)
    rows_per = n_sel // n_var
    top_n = float(min(NSA_TOP_N, n_sel))
    for v in range(n_var):
        rows = rows_per * (v + 1)

        @pl.when((n_live > rows_per * v) & (n_live <= rows))
        def _():
            groups = rows // SUBLANES
            mine = [imp_ref[gi * SUBLANES:(gi + 1) * SUBLANES, :] for gi in range(groups)]
            rank = [jnp.zeros((SUBLANES, tq), F32) for _ in range(groups)]
            in_group = lax.broadcasted_iota(jnp.int32, (SUBLANES, tq), 0)
            for i2 in range(rows):
                other = imp_ref[i2:i2 + 1, :]
                for gi in range(groups):
                    if gi > i2 // SUBLANES:
                        beats = other >= mine[gi]
                    elif gi < i2 // SUBLANES:
                        beats = other > mine[gi]
                    else:
                        beats = (other > mine[gi]) | ((other == mine[gi]) & (in_group > i2 % SUBLANES))
                    rank[gi] = rank[gi] + jnp.where(beats, 1.0, 0.0)
            for gi in range(groups):
                bias = jnp.where(rank[gi] < top_n, 0.0, NEG_INF)
                bias_ref[gi * SUBLANES:(gi + 1) * SUBLANES, :] = jnp.concatenate([bias] * rep, axis=1)

    init = (jnp.full((1, width), NEG_INF, F32), jnp.zeros((d + ONES_ROWS, width), F32))

    def normalised(acc):
        return acc[:d, :] * (1.0 / acc[d:d + 1, :])

    chunk = 8 * sub
    n_sub = chunk // sub

    s_slots = (ss_ref, ss1_ref)
    p_slots = (ps_ref, ps1_ref)

    def sel_scores(c, slot, diagonal=False):
        start = pl.multiple_of(c * chunk, chunk)
        s = _dot_nt(ksl_ref[0, pl.ds(start, chunk), :], q4)
        if diagonal:
            kpos = start + lax.broadcasted_iota(jnp.int32, s.shape, 0)
            s = jnp.where(kpos <= qpos_of(s.shape), s, NEG_INF)
        s_slots[slot][...] = s

    def sel_update(c, slot, carry):
        m_prev, acc = carry
        s_ref, p_ref = s_slots[slot], p_slots[slot]
        biases = [bias_ref[pl.ds(c * n_sub + r, 1), :] for r in range(n_sub)]
        m8 = jnp.full((SUBLANES, width), NEG_INF, F32)
        for r in range(n_sub):
            block = s_ref[r * sub:(r + 1) * sub, :]
            m8 = jnp.maximum(m8, jnp.max(block.reshape(sub // SUBLANES, SUBLANES, width), axis=0) + biases[r])
        m_new = jnp.maximum(m_prev, jnp.max(m8, axis=0, keepdims=True))
        live = m_new > dead
        for r in range(n_sub):
            shift = jnp.where(live, biases[r] - m_new, NEG_INF)
            p_ref[r * sub:(r + 1) * sub, :] = jnp.exp2(s_ref[r * sub:(r + 1) * sub, :] + shift).astype(BF16)
        vt = _with_ones_rows(vslt_ref[v_rows, pl.ds(pl.multiple_of(c * chunk, chunk), chunk)])
        return m_new, jnp.exp2(m_prev - m_new) * acc + _dot(vt, p_ref[...])

    c_diag = (t0 + tq - 1) // chunk
    n_chunks = c_diag + 1
    last_past = jnp.maximum(c_diag - 1, 0)

    def chunk_at(j):
        return jnp.where(j == 0, c_diag, j - 1)

    sel_scores(c_diag, 0, diagonal=True)

    def pair(k, carry):
        sel_scores(jnp.minimum(2 * k, last_past), 1)
        carry = sel_update(chunk_at(2 * k), 0, carry)
        sel_scores(jnp.minimum(2 * k + 1, last_past), 0)
        return sel_update(2 * k, 1, carry)

    carry = lax.fori_loop(0, n_chunks // 2, pair, init)
    _, acc_s = lax.cond(n_chunks % 2 == 1, lambda cr: sel_update(chunk_at(n_chunks - 1), 0, cr),
                        lambda cr: cr, carry)
    o_s = normalised(acc_s)

    def win_chunk(c, carry):
        m_prev, acc = carry
        start = pl.multiple_of(c * tq, tq)
        s = _dot_nt(kwn_ref[0, pl.ds(start, tq), :], q4)
        kpos = start + lax.broadcasted_iota(jnp.int32, s.shape, 0)
        qpos = qpos_of(s.shape)
        s = jnp.where((kpos <= qpos) & (kpos > qpos - NSA_WINDOW), s, NEG_INF)
        m_new = jnp.maximum(m_prev, jnp.max(s, axis=0, keepdims=True))
        p = jnp.exp2(s + jnp.where(m_new > dead, -m_new, NEG_INF))
        vt = _with_ones_rows(vwnt_ref[v_rows, pl.ds(start, tq)])
        return m_new, jnp.exp2(m_prev - m_new) * acc + _dot(vt, p.astype(BF16))

    n_back = NSA_WINDOW // tq

    @pl.when(i < n_back)
    def _():
        _, acc_w = lax.fori_loop(0, i + 1, win_chunk, init)
        ow_ref[...] = normalised(acc_w)

    @pl.when(i >= n_back)
    def _():
        start = pl.multiple_of(t0 - NSA_WINDOW, tq)
        s = _dot_nt(kwn_ref[0, pl.ds(start, NSA_WINDOW + tq), :], q4)
        kpos = start + lax.broadcasted_iota(jnp.int32, (tq, width), 0)
        qpos = qpos_of((tq, width))
        sw_ref[0:tq, :] = jnp.where(kpos > qpos - NSA_WINDOW, s[0:tq, :], NEG_INF)
        sw_ref[tq:NSA_WINDOW, :] = s[tq:NSA_WINDOW, :]
        sw_ref[NSA_WINDOW:, :] = jnp.where(kpos + NSA_WINDOW <= qpos, s[NSA_WINDOW:, :], NEG_INF)
        m_w = jnp.max(sw_ref[...], axis=0, keepdims=True)
        for r in range((NSA_WINDOW + tq) // sub):
            pw_ref[r * sub:(r + 1) * sub, :] = jnp.exp2(sw_ref[r * sub:(r + 1) * sub, :] - m_w).astype(BF16)
        vt = _with_ones_rows(vwnt_ref[v_rows, pl.ds(start, NSA_WINDOW + tq)])
        ow_ref[...] = normalised(_dot(vt, pw_ref[...]))

    o_w = ow_ref[...]

    def gate(branch):
        rows = [glt_ref[pl.ds((g * rep + r) * 3 + branch, 1), :] for r in range(rep)]
        return _sigmoid(jnp.concatenate(rows, axis=1))

    out = gate(0) * o_c + gate(1) * o_s + gate(2) * o_w
    out_t = jnp.concatenate([out, jnp.zeros_like(out)], axis=0).T
    o_ref[0] = jnp.concatenate([out_t[r * tq:(r + 1) * tq, :d] for r in range(rep)],
                               axis=1).astype(o_ref.dtype)


def nsa_overlap_t(n_cmp, n_sel):
    c_start = np.arange(n_cmp)[None, :] * NSA_CMP_STRIDE
    s_start = np.arange(n_sel)[:, None] * NSA_SEL_BLOCK
    hit = (c_start < s_start + NSA_SEL_BLOCK) & (c_start + NSA_CMP_BLOCK > s_start)
    hit = hit & (np.arange(n_cmp)[None, :] < n_cmp - NSA_CMP_BLOCK // NSA_CMP_STRIDE + 1)
    return jnp.asarray(hit.astype(np.float32), dtype=BF16)


def nsa_attention(qn, ck, cvt, ksl, vslt, kwn, vwnt, glt, tq=256):
    b, s, _ = qn.shape
    g, d = NSA_KV_GROUPS, NSA_HEAD_DIM
    rep = NSA_HEADS // g
    n_cmp = ck.shape[1]
    n_sel = s // NSA_SEL_BLOCK
    nq = s // tq
    ovt = nsa_overlap_t(n_cmp, n_sel)
    width = rep * tq
    chunk = 8 * NSA_SEL_BLOCK
    full3 = lambda bb, gg, i: (bb, 0, 0)
    seq_t = lambda bb, gg, i: (0, bb)
    return pl.pallas_call(
        functools.partial(_nsa_body, tq=tq), grid=(b, g, nq),
        in_specs=[pl.BlockSpec((1, tq, rep * d), lambda bb, gg, i: (bb, i, gg)),
                  pl.BlockSpec((1, n_cmp, g * d), full3), pl.BlockSpec((1, g * d, n_cmp), full3),
                  pl.BlockSpec((1, s, g * d), full3), pl.BlockSpec((g * d, s), seq_t),
                  pl.BlockSpec((1, s, g * d), full3), pl.BlockSpec((g * d, s), seq_t),
                  pl.BlockSpec((n_sel, n_cmp), lambda bb, gg, i: (0, 0)),
                  pl.BlockSpec((glt.shape[0], tq), lambda bb, gg, i: (0, bb * nq + i))],
        out_specs=pl.BlockSpec((1, tq, rep * d), lambda bb, gg, i: (bb, i, gg)),
        out_shape=jax.ShapeDtypeStruct((b, s, g * rep * d), BF16),
        scratch_shapes=[pltpu.VMEM((n_cmp, width), F32), pltpu.VMEM((n_cmp, width), BF16),
                        pltpu.VMEM((n_cmp, width), BF16), pltpu.VMEM((n_sel, tq), F32),
                        pltpu.VMEM((n_sel, width), F32), pltpu.VMEM((chunk, width), F32),
                        pltpu.VMEM((chunk, width), BF16), pltpu.VMEM((chunk, width), F32),
                        pltpu.VMEM((chunk, width), BF16), pltpu.VMEM((NSA_WINDOW + tq, width), F32),
                        pltpu.VMEM((NSA_WINDOW + tq, width), BF16), pltpu.VMEM((d, width), F32),
                        pltpu.VMEM((d, width), F32)],
        compiler_params=_cparams(("parallel", "parallel", "arbitrary")), name="nsa_attention",
    )(qn, ck, cvt, ksl, vslt, kwn, vwnt, ovt, glt)


def _pad_cols(w, n):
    return jnp.pad(w, ((0, 0), (0, n - w.shape[1])))


def _even_layer(x2, b, s, layer_idx, norm_mix, w_in, q_gain, k_gain, lam, subln_gain, conv_w, conv_b,
                dt_bias, a_log, d_skip, ssm_norm_gain, w_out, norm_ffn, w_gate, w_up, w_down):
    nq = DA_HEADS * 2 * DA_HEAD_DIM
    nv = DA_HEADS * DA_V_DIM
    cch = SSM_D_INNER + 2 * SSM_GROUPS * SSM_STATE
    offs = np.cumsum([0, nq, nq, nv, SSM_D_INNER, cch, SSM_HEADS])
    wb = w_in.astype(BF16)
    pieces = [wb[:, offs[k]:offs[k + 1]] for k in range(6)]
    pieces[2] = pieces[2].T
    pieces[5] = _pad_cols(pieces[5], LANES)
    posts = [HeadNorm(q_gain, DA_HEAD_DIM, rope=True, mul=DA_HEAD_DIM ** -0.5 * LOG2E),
             HeadNorm(k_gain, DA_HEAD_DIM, rope=True), None, None, None, None]
    q, k, vt, z, xbc, dt = norm_proj(x2, norm_mix, pieces, [BF16, BF16, BF16, F32, F32, F32], posts, s,
                                     _rope_tables(s, DA_HEAD_DIM), transposed=(2,))
    qn = q.reshape(b, s, nq)
    kn = k.reshape(b, s, nq)
    lam_init = 0.8 - 0.6 * math.exp(-0.3 * layer_idx)
    lf = lam.astype(F32)
    lam_full = jnp.exp(jnp.sum(lf[0] * lf[1])) - jnp.exp(jnp.sum(lf[2] * lf[3])) + lam_init
    a_out = flash_attention(lam_full.reshape(1), [qn], [kn], vt, subln_gain, DA_HEADS, DA_V_DIM,
                            diff=True, out_scale=1.0 - lam_init)
    b_out = ssd_mixer(xbc.reshape(b, s, cch), z.reshape(b, s, SSM_D_INNER), dt.reshape(b, s, LANES),
                      conv_w, conv_b, dt_bias, a_log, d_skip, ssm_norm_gain)
    wo = w_out.astype(BF16)
    return out_proj_ffn_residual(x2, a_out.reshape(-1, nv), b_out.reshape(-1, SSM_D_INNER), wo[:nv], wo[nv:],
                                 norm_ffn, w_gate.astype(BF16), w_up.astype(BF16), w_down.astype(BF16))


def _odd_layer(x2, b, s, norm_mix, w_in, q_gain, k_gain, cmp_pos, cmp_w1, cmp_w2, cq_gain, ckv_gain,
               w_uq, w_ukv, qn_gain, qr_gain, kn_gain, kr_gain, w_out, norm_ffn, router_w, w_gate, w_up,
               w_down):
    g, d = NSA_KV_GROUPS, NSA_HEAD_DIM
    nq = NSA_HEADS * d
    nkv = g * d
    sizes = [nq] + [nkv] * 6 + [NSA_HEADS * 3, w_uq.shape[0], w_ukv.shape[0], MLA_ROPE_DIM]
    offs = np.cumsum([0] + sizes)
    wb = w_in.astype(BF16)
    pieces = [wb[:, offs[k]:offs[k + 1]] for k in range(len(sizes))]
    for k in (4, 6):
        pieces[k] = pieces[k].T
    pieces[7] = jnp.pad(pieces[7].T, ((0, 32 - NSA_HEADS * 3), (0, 0)))
    pieces[10] = _pad_cols(pieces[10], LANES)
    tables = _rope_tables(s, d)
    posts = [None] * len(sizes)
    posts[0] = HeadNorm(q_gain, d, rope=True, mul=d ** -0.5 * LOG2E)
    posts[3] = HeadNorm(k_gain[1], d, rope=True)
    posts[5] = HeadNorm(k_gain[2], d, rope=True)
    posts[10] = HeadNorm(kr_gain, MLA_ROPE_DIM, rope=True)
    (q, kc, vc, ksl, vslt, kwn, vwnt, glt, cq, ckv, k_rope) = norm_proj(
        x2, norm_mix, pieces, [BF16, F32, F32, BF16, BF16, BF16, BF16, F32, F32, F32, BF16], posts, s, tables,
        transposed=(4, 6, 7))

    qn = q.reshape(b, s, nq)
    ksl_n = ksl.reshape(b, s, nkv)
    kwn_n = kwn.reshape(b, s, nkv)
    ck = nsa_compress(kc.reshape(b, s, nkv), cmp_pos[0], cmp_w1[0], cmp_w2[0], k_gain[0], s, True)
    cv = nsa_compress(vc.reshape(b, s, nkv), cmp_pos[1], cmp_w1[1], cmp_w2[1], k_gain[0], s, False)
    n_cmp = ck.shape[2]
    ck = ck.transpose(0, 2, 1, 3).reshape(b, n_cmp, nkv).astype(BF16)
    cvt = cv.transpose(0, 1, 3, 2).reshape(b, nkv, n_cmp).astype(BF16)
    c_out = nsa_attention(qn, ck, cvt, ksl_n, vslt, kwn_n, vwnt, glt).reshape(b * s, nq)

    h = MLA_HEADS
    dqk = MLA_NOPE_DIM + MLA_ROPE_DIM
    wq = w_uq.astype(BF16).reshape(-1, h, dqk)
    wq_nope = wq[:, :, :MLA_NOPE_DIM].reshape(-1, h * MLA_NOPE_DIM)
    wq_rope = jnp.pad(wq[:, :, MLA_NOPE_DIM:], ((0, 0), (0, 0), (0, LANES - MLA_ROPE_DIM)))
    wq_rope = wq_rope.reshape(-1, h * LANES)
    wkv = w_ukv.astype(BF16).reshape(-1, h, MLA_NOPE_DIM + MLA_V_DIM)
    wk_nope = wkv[:, :, :MLA_NOPE_DIM].reshape(-1, h * MLA_NOPE_DIM)
    wv = wkv[:, :, MLA_NOPE_DIM:].reshape(-1, h * MLA_V_DIM)
    q_mul = dqk ** -0.5 * LOG2E
    q_nope, q_rope = norm_proj(
        cq, cq_gain, [wq_nope, wq_rope], [BF16, BF16],
        [HeadNorm(qn_gain, MLA_NOPE_DIM, mul=q_mul), HeadNorm(qr_gain, MLA_ROPE_DIM, rope=True, mul=q_mul)],
        s, tables)
    k_nope, vt = norm_proj(ckv, ckv_gain, [wk_nope, wv.T], [BF16, BF16],
                           [HeadNorm(kn_gain, MLA_NOPE_DIM), None], transposed=(1,))
    shp = lambda t: t.reshape(b, s, t.shape[-1])
    d_out = flash_attention(jnp.zeros((1,), F32), [shp(q_nope), shp(q_rope)], [shp(k_nope), shp(k_rope)],
                            vt, jnp.ones((MLA_V_DIM,), F32), h, MLA_V_DIM, diff=False)

    wo = w_out.astype(BF16)
    x2, hn, route, counts = out_proj_router(x2, c_out, d_out.reshape(b * s, h * MLA_V_DIM), wo[:nq], wo[nq:],
                                            norm_ffn, router_w)
    return moe_residual(x2, hn, route, counts, w_gate.astype(BF16), w_up.astype(BF16), w_down.astype(BF16))


def kernel(x, ev_norm_mix, ev_w_in, da_q_gain, da_k_gain, da_lambda, da_subln_gain, ssm_conv_w, ssm_conv_b, ssm_dt_bias, ssm_a_log, ssm_d, ssm_norm_gain, ev_w_out, ev_norm_ffn, ffn_w_gate, ffn_w_up, ffn_w_down, od_norm_mix, od_w_in, nsa_q_gain, nsa_k_gain, nsa_cmp_pos, nsa_cmp_w1, nsa_cmp_w2, mla_cq_gain, mla_ckv_gain, mla_w_uq, mla_w_ukv, mla_qn_gain, mla_qr_gain, mla_kn_gain, mla_kr_gain, od_w_out, od_norm_ffn, moe_router, moe_w_gate, moe_w_up, moe_w_down):
    b, s, d = x.shape
    x2 = x.reshape(b * s, d)
    depth = ev_norm_mix.shape[0] + od_norm_mix.shape[0]
    for layer in range(depth):
        i = layer // 2
        if layer % 2 == 0:
            x2 = _even_layer(x2, b, s, layer, ev_norm_mix[i], ev_w_in[i], da_q_gain[i], da_k_gain[i],
                             da_lambda[i], da_subln_gain[i], ssm_conv_w[i], ssm_conv_b[i],
                             ssm_dt_bias[i], ssm_a_log[i], ssm_d[i], ssm_norm_gain[i], ev_w_out[i],
                             ev_norm_ffn[i], ffn_w_gate[i], ffn_w_up[i], ffn_w_down[i])
        else:
            x2 = _odd_layer(x2, b, s, od_norm_mix[i], od_w_in[i], nsa_q_gain[i], nsa_k_gain[i],
                            nsa_cmp_pos[i], nsa_cmp_w1[i], nsa_cmp_w2[i], mla_cq_gain[i],
                            mla_ckv_gain[i], mla_w_uq[i], mla_w_ukv[i], mla_qn_gain[i], mla_qr_gain[i],
                            mla_kn_gain[i], mla_kr_gain[i], od_w_out[i], od_norm_ffn[i], moe_router[i],
                            moe_w_gate[i], moe_w_up[i], moe_w_down[i])
    return x2.reshape(b, s, d)
```

```python
import functools
import math

import numpy as np
import jax
import jax.numpy as jnp
from jax import lax
from jax.experimental import pallas as pl
from jax.experimental.pallas import tpu as pltpu

F32 = jnp.float32
BF16 = jnp.bfloat16

ROPE_THETA = 10000.0
NORM_EPS = 1e-6
NEG_INF = -1e30
FORCE_SCORE = 1e6
LOG2E = 1.4426950408889634

DA_HEADS = 4
DA_HEAD_DIM = 64
DA_V_DIM = 2 * DA_HEAD_DIM
SSM_HEADS = 8
SSM_HEAD_DIM = 64
SSM_D_INNER = SSM_HEADS * SSM_HEAD_DIM
SSM_GROUPS = 2
SSM_STATE = 128
SSM_CONV = 4
SSM_CHUNK = 256
NSA_HEADS = 8
NSA_KV_GROUPS = 2
NSA_HEAD_DIM = 64
NSA_CMP_BLOCK = 32
NSA_CMP_STRIDE = 16
NSA_SEL_BLOCK = 64
NSA_TOP_N = 16
NSA_WINDOW = 512
MLA_HEADS = 4
MLA_NOPE_DIM = 128
MLA_ROPE_DIM = 64
MLA_V_DIM = 128
N_EXPERTS = 8

LANES = 128
SUBLANES = 8
VMEM_LIMIT = 48 * 1024 * 1024
MOE_VMEM_LIMIT = 58 * 1024 * 1024

NT_DIMS = (((1,), (1,)), ((), ()))


def _cparams(semantics):
    return pltpu.CompilerParams(dimension_semantics=semantics, vmem_limit_bytes=VMEM_LIMIT)


def _dot(a, b):
    return jnp.dot(a, b, preferred_element_type=F32)


def _dot_nt(a, b):
    return lax.dot_general(a, b, NT_DIMS, preferred_element_type=F32)


def _split_bf16(x, parts):
    out = []
    for _ in range(parts):
        hi = x.astype(BF16)
        out.append(hi)
        x = x - hi.astype(F32)
    return out


def _fold_rows(x):
    return jnp.sum(x.reshape(x.shape[0] // SUBLANES, SUBLANES, x.shape[1]), axis=0)


ONES_ROWS = 16


def _with_ones_rows(vt):
    return jnp.concatenate([vt, jnp.ones((ONES_ROWS, vt.shape[1]), vt.dtype)], axis=0)


def _sigmoid(x):
    return 1.0 / (1.0 + jnp.exp(-x))


def _silu(x):
    return x * _sigmoid(x)


def _softplus(x):
    return jnp.maximum(x, 0.0) + jnp.log(1.0 + jnp.exp(-jnp.abs(x)))


def _rms(x, gain):
    ms = jnp.mean(x * x, axis=-1, keepdims=True)
    return x * lax.rsqrt(ms + NORM_EPS) * gain


class HeadNorm:
    def __init__(self, gain, hd, rope=False, mul=1.0):
        self.gain, self.hd, self.rope, self.mul = gain, hd, rope, mul


def _head_norm(y, gain, bd, cos_ref, sin_ref, post):
    n = y.shape[1]
    hd = post.hd
    hi, lo = _split_bf16(y * y, 2)
    ss = _dot(hi, bd) + _dot(lo, bd)
    yn = y * lax.rsqrt(ss * (1.0 / hd) + NORM_EPS) * gain
    if post.rope:
        reps = n // LANES
        cos = jnp.concatenate([cos_ref[...]] * reps, axis=1) if reps > 1 else cos_ref[...]
        sin = jnp.concatenate([sin_ref[...]] * reps, axis=1) if reps > 1 else sin_ref[...]
        lane = lax.broadcasted_iota(jnp.int32, yn.shape, 1)
        first_half = (lane & (hd - 1)) < (hd // 2)
        partner = jnp.where(first_half, pltpu.roll(yn, n - hd // 2, 1), pltpu.roll(yn, hd // 2, 1))
        yn = yn * cos + partner * sin
    if post.mul != 1.0:
        yn = yn * post.mul
    return yn


def _norm_proj_body(x_ref, g_ref, *refs, posts, use_rope, transposed):
    if use_rope:
        cos_ref, sin_ref = refs[0], refs[1]
        refs = refs[2:]
    else:
        cos_ref = sin_ref = None
    n_out = len(posts)
    n_aux = 2 * sum(p is not None for p in posts)
    w_refs, aux, o_refs = refs[:n_out], refs[n_out:n_out + n_aux], refs[n_out + n_aux:]
    h = _rms(x_ref[...], g_ref[...]).astype(BF16)
    a = 0
    for k, (w_ref, o_ref, post) in enumerate(zip(w_refs, o_refs, posts)):
        if k in transposed:
            o_ref[...] = _dot_nt(w_ref[...], h).astype(o_ref.dtype)
            continue
        y = _dot(h, w_ref[...])
        if post is not None:
            y = _head_norm(y, aux[a][...], aux[a + 1][...], cos_ref, sin_ref, post)
            a += 2
        o_ref[...] = y.astype(o_ref.dtype)


def norm_proj(x2, gain, weights, out_dtypes, posts=None, seq=None, rope_tables=None, transposed=(), tm=512):
    t, d = x2.shape
    posts = posts or [None] * len(weights)
    transposed = frozenset(transposed)
    use_rope = any(p is not None and p.rope for p in posts)
    const = lambda i: (0, 0)
    args = [x2, gain.reshape(1, d).astype(F32)]
    in_specs = [pl.BlockSpec((tm, d), lambda i: (i, 0)), pl.BlockSpec((1, d), const)]
    if use_rope:
        per_seq = seq // tm
        args += list(rope_tables)
        in_specs += [pl.BlockSpec((tm, LANES), lambda i: (i % per_seq, 0))] * 2
    args += list(weights)
    in_specs += [pl.BlockSpec(w.shape, const) for w in weights]
    for w, p in zip(weights, posts):
        if p is not None:
            n = w.shape[1]
            args += [jnp.tile(p.gain.astype(F32), n // p.hd).reshape(1, n), _block_diag_ones(n, p.hd)]
            in_specs += [pl.BlockSpec((1, n), const), pl.BlockSpec((n, n), const)]
    out_specs, out_shape = [], []
    for k, (w, dt) in enumerate(zip(weights, out_dtypes)):
        if k in transposed:
            out_specs.append(pl.BlockSpec((w.shape[0], tm), lambda i: (0, i)))
            out_shape.append(jax.ShapeDtypeStruct((w.shape[0], t), dt))
        else:
            out_specs.append(pl.BlockSpec((tm, w.shape[1]), lambda i: (i, 0)))
            out_shape.append(jax.ShapeDtypeStruct((t, w.shape[1]), dt))
    return pl.pallas_call(
        functools.partial(_norm_proj_body, posts=tuple(posts), use_rope=use_rope, transposed=transposed),
        grid=(t // tm,), in_specs=in_specs, out_specs=out_specs, out_shape=out_shape,
        compiler_params=_cparams(("parallel",)), name="norm_proj",
    )(*args)


def _block_diag_ones(n, hd):
    idx = np.arange(n) // hd
    return jnp.asarray((idx[:, None] == idx[None, :]).astype(np.float32), dtype=BF16)


def _rope_tables(seq, hd):
    inv_freq = 1.0 / (ROPE_THETA ** (jnp.arange(0, hd, 2, dtype=F32) / hd))
    ang = jnp.arange(seq, dtype=F32)[:, None] * inv_freq[None, :]
    cos, sin = jnp.cos(ang), jnp.sin(ang)
    reps = LANES // hd
    cos_t = jnp.tile(jnp.concatenate([cos, cos], axis=1), (1, reps))
    sin_t = jnp.tile(jnp.concatenate([-sin, sin], axis=1), (1, reps))
    return cos_t, sin_t


def _flash_body(lam_ref, *refs, n_qk, diff, out_scale, sub):
    q_refs = refs[:n_qk]
    k_refs = refs[n_qk:2 * n_qk]
    vt_ref, gain_ref, o_ref, m_ref, l_ref, acc_ref, s0_ref, s1_ref, p0_ref, p1_ref = refs[2 * n_qk:]
    i = pl.program_id(2)
    n_sm = 2 if diff else 1
    _, tk, tq = s0_ref.shape
    s_slots = (s0_ref, s1_ref)
    p_slots = (p0_ref, p1_ref)

    m_ref[...] = jnp.full(m_ref.shape, NEG_INF, F32)
    l_ref[...] = jnp.zeros(l_ref.shape, F32)
    acc_ref[...] = jnp.zeros(acc_ref.shape, F32)

    qs = [r[0] for r in q_refs]
    q = qs[0] if n_qk == 1 else jnp.concatenate(qs, axis=1)
    if diff:
        lane = lax.broadcasted_iota(jnp.int32, q.shape, 1)
        half = q.shape[1] // 2
        zero = jnp.zeros_like(q)
        q_parts = [jnp.where(lane < half, q, zero), jnp.where(lane >= half, q, zero)]
    else:
        q_parts = [q]

    def scores(c, slot, key_offset=None):
        rows = pl.ds(pl.multiple_of(c * tk, tk), tk)
        ks = [r[0, rows, :] for r in k_refs]
        k = ks[0] if n_qk == 1 else jnp.concatenate(ks, axis=1)
        for sm in range(n_sm):
            s = _dot_nt(k, q_parts[sm])
            if key_offset is not None:
                row = lax.broadcasted_iota(jnp.int32, s.shape, 0)
                col = lax.broadcasted_iota(jnp.int32, s.shape, 1)
                s = jnp.where(row + key_offset <= col, s, NEG_INF)
            s_slots[slot][sm] = s

    def update(c, slot):
        vt = vt_ref[:, pl.ds(pl.multiple_of(c * tk, tk), tk)]
        for sm in range(n_sm):
            s_ref, p_ref = s_slots[slot], p_slots[slot]
            m_prev = m_ref[sm]
            m_new = jnp.maximum(m_prev, jnp.max(s_ref[sm], axis=0, keepdims=True))
            m_ref[sm] = m_new
            alpha = jnp.exp2(m_prev - m_new)
            lpart = jnp.zeros((SUBLANES, tq), F32)
            for r in range(tk // sub):
                p = jnp.exp2(s_ref[sm, r * sub:(r + 1) * sub, :] - m_new)
                lpart = lpart + _fold_rows(p)
                p_ref[sm, r * sub:(r + 1) * sub, :] = p.astype(BF16)
            l_ref[sm] = alpha * l_ref[sm] + jnp.sum(lpart, axis=0, keepdims=True)
            acc_ref[sm] = alpha * acc_ref[sm] + _dot(vt, p_ref[sm])

    if tq == tk:
        n_chunks = i + 1
        last_past = jnp.maximum(i - 1, 0)

        def chunk_at(j):
            return jnp.where(j == 0, i, j - 1)

        scores(i, 0, key_offset=0)

        def pair(k2, carry):
            scores(jnp.minimum(2 * k2, last_past), 1)
            update(chunk_at(2 * k2), 0)
            scores(jnp.minimum(2 * k2 + 1, last_past), 0)
            update(2 * k2, 1)
            return carry

        lax.fori_loop(0, n_chunks // 2, pair, 0)

        @pl.when(n_chunks % 2 == 1)
        def _():
            update(chunk_at(n_chunks - 1), 0)
    else:
        n_past = 2 * i
        last_past = jnp.maximum(n_past - 1, 0)
        scores(n_past, 0, key_offset=0)
        scores(n_past + 1, 1, key_offset=tk)
        update(n_past, 0)
        scores(0, 0)
        update(n_past + 1, 1)

        def pair(k2, carry):
            scores(2 * k2 + 1, 1)
            update(2 * k2, 0)
            scores(jnp.minimum(2 * k2 + 2, last_past), 0)
            update(2 * k2 + 1, 1)
            return carry

        lax.fori_loop(0, i, pair, 0)

    o = acc_ref[0] * (1.0 / l_ref[0])
    if diff:
        o = o - acc_ref[1] * (lam_ref[0] / l_ref[1])
        ms = jnp.mean(o * o, axis=0, keepdims=True)
        o = o * lax.rsqrt(ms + NORM_EPS) * gain_ref[...] * out_scale
    o_ref[0] = o.T.astype(o_ref.dtype)


def flash_attention(lam, qs, ks, vt, gain, n_heads, dv, *, diff, out_scale=1.0, tk=512):
    b, s, _ = qs[0].shape
    if diff:
        tk = tk // 2
    tile = 2 * tk
    sub = 64 if diff else 32
    nt = s // tile
    n_qk = len(qs)
    in_specs = [pl.BlockSpec(memory_space=pltpu.SMEM)]
    for q in qs:
        w = q.shape[2] // n_heads
        in_specs.append(pl.BlockSpec((1, tile, w), lambda bb, h, i: (bb, i, h)))
    for q, k in zip(qs, ks):
        w = q.shape[2] // n_heads
        if k.shape[2] == w:
            in_specs.append(pl.BlockSpec((1, s, w), lambda bb, h, i: (bb, 0, 0)))
        else:
            in_specs.append(pl.BlockSpec((1, s, w), lambda bb, h, i: (bb, 0, h)))
    in_specs.append(pl.BlockSpec((dv, s), lambda bb, h, i: (h, bb)))
    in_specs.append(pl.BlockSpec((dv, 1), lambda bb, h, i: (0, 0)))
    n_sm = 2 if diff else 1
    return pl.pallas_call(
        functools.partial(_flash_body, n_qk=n_qk, diff=diff, out_scale=out_scale, sub=sub),
        grid=(b, n_heads, nt), in_specs=in_specs,
        out_specs=pl.BlockSpec((1, tile, dv), lambda bb, h, i: (bb, i, h)),
        out_shape=jax.ShapeDtypeStruct((b, s, n_heads * dv), BF16),
        scratch_shapes=[pltpu.VMEM((n_sm, 1, tile), F32), pltpu.VMEM((n_sm, 1, tile), F32),
                        pltpu.VMEM((n_sm, dv, tile), F32),
                        pltpu.VMEM((n_sm, tk, tile), F32), pltpu.VMEM((n_sm, tk, tile), F32),
                        pltpu.VMEM((n_sm, tk, tile), BF16), pltpu.VMEM((n_sm, tk, tile), BF16)],
        compiler_params=_cparams(("parallel", "parallel", "arbitrary")),
        name="flash_diff" if diff else "flash_plain",
    )(lam, *qs, *ks, vt, gain.reshape(dv, 1).astype(F32))


def _ssd_body(xbc_ref, z_ref, dt_ref, dtt_ref, cw_ref, cb_ref, dtb_ref, dtbt_ref, al_ref, alt_ref,
              dsk_ref, ng_ref, o_ref, xpad_ref, state_ref):
    chunk = xbc_ref.shape[1]
    d_in = z_ref.shape[2]
    gn = SSM_GROUPS * SSM_STATE
    c = pl.program_id(1)

    @pl.when(c == 0)
    def _():
        xpad_ref[0:8, :] = jnp.zeros((8, xpad_ref.shape[1]), F32)
        state_ref[...] = jnp.zeros(state_ref.shape, F32)

    xpad_ref[8:8 + chunk, :] = xbc_ref[0]
    conv = cb_ref[...]
    for w in range(SSM_CONV):
        conv = conv + cw_ref[w:w + 1, :] * xpad_ref[pl.ds(8 - (SSM_CONV - 1) + w, chunk), :]
    xpad_ref[0:8, :] = xpad_ref[chunk:chunk + 8, :]
    u = _silu(conv)
    xs = u[:, :d_in]
    bmat = u[:, d_in:d_in + gn]
    cmat = u[:, d_in + gn:]

    dt = _softplus(dt_ref[0] + dtb_ref[...])
    ad = dt * (-jnp.exp(al_ref[...]))
    dtt = _softplus(dtt_ref[0] + dtbt_ref[...])
    adt = dtt * (-jnp.exp(alt_ref[...]))
    row = lax.broadcasted_iota(jnp.int32, (chunk, chunk), 0)
    col = lax.broadcasted_iota(jnp.int32, (chunk, chunk), 1)
    lower = row >= col
    tril = jnp.where(lower, 1.0, 0.0).astype(BF16)
    triu = jnp.where(row <= col, 1.0, 0.0).astype(BF16)
    cs = sum(_dot(tril, part) for part in _split_bf16(ad, 3))
    cst = sum(_dot(part, triu) for part in _split_bf16(adt, 3))

    heads_per_group = SSM_HEADS // SSM_GROUPS
    dsk = dsk_ref[...]
    ys = []
    for g in range(SSM_GROUPS):
        bg = bmat[:, g * SSM_STATE:(g + 1) * SSM_STATE]
        cg = cmat[:, g * SSM_STATE:(g + 1) * SSM_STATE].astype(BF16)
        cb = _dot_nt(cg, bg.astype(BF16))
        bgt = bg.T.astype(BF16)
        for r in range(heads_per_group):
            h = g * heads_per_group + r
            ccol = cs[:, h:h + 1]
            crow = cst[h:h + 1, :]
            decay = jnp.exp(jnp.where(lower, ccol - crow, NEG_INF))
            x_h = xs[:, h * SSM_HEAD_DIM:(h + 1) * SSM_HEAD_DIM]
            xdt = x_h * dt[:, h:h + 1]
            y = _dot((cb * decay).astype(BF16), xdt.astype(BF16))
            st = state_ref[h]
            y = y + _dot(cg, st.astype(BF16)) * jnp.exp(ccol)
            last = cst[h:h + 1, chunk - 1:chunk]
            to_end = jnp.exp(last - ccol)
            state_ref[h] = st * jnp.exp(last) + _dot(bgt, (xdt * to_end).astype(BF16))
            ys.append(y + x_h * dsk[:, h * SSM_HEAD_DIM:(h + 1) * SSM_HEAD_DIM])

    y = jnp.concatenate(ys, axis=1) * _silu(z_ref[0])
    gw = d_in // SSM_GROUPS
    for g in range(SSM_GROUPS):
        seg = y[:, g * gw:(g + 1) * gw]
        o_ref[0, :, g * gw:(g + 1) * gw] = _rms(seg, ng_ref[:, g * gw:(g + 1) * gw]).astype(o_ref.dtype)


def ssd_mixer(xbc, z, dt_raw, conv_w, conv_b, dt_bias, a_log, d_skip, norm_gain):
    b, s, cch = xbc.shape
    d_in = z.shape[2]
    nc = s // SSM_CHUNK
    hpad = dt_raw.shape[2]
    dtt = jnp.transpose(dt_raw[:, :, :SSM_HEADS], (0, 2, 1))

    def lane_pad(v):
        return jnp.pad(v.astype(F32), (0, hpad - SSM_HEADS)).reshape(1, hpad)

    args = (xbc, z, dt_raw, dtt, conv_w.astype(F32), conv_b.reshape(1, cch).astype(F32),
            lane_pad(dt_bias), dt_bias.reshape(SSM_HEADS, 1).astype(F32),
            lane_pad(a_log), a_log.reshape(SSM_HEADS, 1).astype(F32),
            jnp.repeat(d_skip.astype(F32), SSM_HEAD_DIM).reshape(1, d_in),
            norm_gain.reshape(1, d_in).astype(F32))
    const = lambda bb, c: (0, 0)
    in_specs = [pl.BlockSpec((1, SSM_CHUNK, cch), lambda bb, c: (bb, c, 0)),
                pl.BlockSpec((1, SSM_CHUNK, d_in), lambda bb, c: (bb, c, 0)),
                pl.BlockSpec((1, SSM_CHUNK, hpad), lambda bb, c: (bb, c, 0)),
                pl.BlockSpec((1, SSM_HEADS, SSM_CHUNK), lambda bb, c: (bb, 0, c)),
                pl.BlockSpec((SSM_CONV, cch), const), pl.BlockSpec((1, cch), const),
                pl.BlockSpec((1, hpad), const), pl.BlockSpec((SSM_HEADS, 1), const),
                pl.BlockSpec((1, hpad), const), pl.BlockSpec((SSM_HEADS, 1), const),
                pl.BlockSpec((1, d_in), const), pl.BlockSpec((1, d_in), const)]
    return pl.pallas_call(
        _ssd_body, grid=(b, nc), in_specs=in_specs,
        out_specs=pl.BlockSpec((1, SSM_CHUNK, d_in), lambda bb, c: (bb, c, 0)),
        out_shape=jax.ShapeDtypeStruct((b, s, d_in), BF16),
        scratch_shapes=[pltpu.VMEM((SSM_CHUNK + 8, cch), F32),
                        pltpu.VMEM((SSM_HEADS, SSM_STATE, SSM_HEAD_DIM), F32)],
        compiler_params=_cparams(("parallel", "arbitrary")), name="ssd_mixer",
    )(*args)


def _ffn_body(x_ref, a_ref, b_ref, wa_ref, wb_ref, g_ref, wg_ref, wu_ref, wd_ref, o_ref, h_ref):
    f = pl.program_id(1)
    half = h_ref.shape[0] // 2
    halves = (slice(0, half), slice(half, 2 * half))

    @pl.when(f == 0)
    def _():
        for r in halves:
            x = x_ref[r, :] + _dot(a_ref[r, :], wa_ref[...]) + _dot(b_ref[r, :], wb_ref[...])
            h_ref[r, :] = _rms(x, g_ref[...]).astype(BF16)
            o_ref[r, :] = x

    for r in halves:
        h = h_ref[r, :]
        act = (_silu(_dot(h, wg_ref[...])) * _dot(h, wu_ref[...])).astype(BF16)
        o_ref[r, :] += _dot(act, wd_ref[...])


def out_proj_ffn_residual(x2, a, bm, wa, wb, gain, w_gate, w_up, w_down, tm=1024, tf=1408):
    t, d = x2.shape
    d_ff = w_gate.shape[1]
    row = lambda i, f: (i, 0)
    const = lambda i, f: (0, 0)
    return pl.pallas_call(
        _ffn_body, grid=(t // tm, d_ff // tf),
        in_specs=[pl.BlockSpec((tm, d), row), pl.BlockSpec((tm, a.shape[1]), row),
                  pl.BlockSpec((tm, bm.shape[1]), row), pl.BlockSpec(wa.shape, const),
                  pl.BlockSpec(wb.shape, const), pl.BlockSpec((1, d), const),
                  pl.BlockSpec((d, tf), lambda i, f: (0, f)),
                  pl.BlockSpec((d, tf), lambda i, f: (0, f)),
                  pl.BlockSpec((tf, d), lambda i, f: (f, 0))],
        out_specs=pl.BlockSpec((tm, d), row),
        out_shape=jax.ShapeDtypeStruct((t, d), F32),
        scratch_shapes=[pltpu.VMEM((tm, d), BF16)],
        compiler_params=pltpu.CompilerParams(
            dimension_semantics=("parallel", "arbitrary"), vmem_limit_bytes=MOE_VMEM_LIMIT),
        name="out_proj_ffn",
    )(x2, a, bm, wa, wb, gain.reshape(1, d).astype(F32), w_gate, w_up, w_down)


MOE_ROWS = 256
ROUTE_IDX = 0
ROUTE_W = 2
ROUTE_RANK = 4


def _moe_ffn_body(block_expert_ref, n_used_ref, xs_ref, wg_ref, wu_ref, wd_ref, o_ref):
    i = pl.program_id(0)

    @pl.when(i < n_used_ref[0])
    def _():
        x = xs_ref[...]
        act = (_silu(_dot(x, wg_ref[0])) * _dot(x, wu_ref[0])).astype(BF16)
        o_ref[...] = _dot(act, wd_ref[0]).astype(o_ref.dtype)

    @pl.when(i >= n_used_ref[0])
    def _():
        o_ref[...] = jnp.zeros(o_ref.shape, o_ref.dtype)


def moe_expert_ffn(xs, block_expert, n_used, w_gate, w_up, w_down):
    p, d = xs.shape
    d_ff = w_gate.shape[2]
    rows = MOE_ROWS
    grid_spec = pltpu.PrefetchScalarGridSpec(
        num_scalar_prefetch=2, grid=(p // rows,),
        in_specs=[pl.BlockSpec((rows, d), lambda i, be, nu: (i, 0)),
                  pl.BlockSpec((1, d, d_ff), lambda i, be, nu: (be[i], 0, 0)),
                  pl.BlockSpec((1, d, d_ff), lambda i, be, nu: (be[i], 0, 0)),
                  pl.BlockSpec((1, d_ff, d), lambda i, be, nu: (be[i], 0, 0))],
        out_specs=pl.BlockSpec((rows, d), lambda i, be, nu: (i, 0)))
    return pl.pallas_call(
        _moe_ffn_body, grid_spec=grid_spec, out_shape=jax.ShapeDtypeStruct((p, d), BF16),
        compiler_params=pltpu.CompilerParams(
            dimension_semantics=("arbitrary",), vmem_limit_bytes=MOE_VMEM_LIMIT),
        name="moe_expert_ffn",
    )(block_expert, n_used, xs, w_gate, w_up, w_down)


def _moe_combine_body(x_ref, y0_ref, y1_ref, route_ref, o_ref):
    route = route_ref[...]
    lane = lax.broadcasted_iota(jnp.int32, route.shape, 1)
    w0 = jnp.sum(jnp.where(lane == ROUTE_W, route, 0.0), axis=1, keepdims=True)
    w1 = jnp.sum(jnp.where(lane == ROUTE_W + 1, route, 0.0), axis=1, keepdims=True)
    o_ref[...] = x_ref[...] + w0 * y0_ref[...].astype(F32) + w1 * y1_ref[...].astype(F32)


def moe_combine(x2, y0, y1, route, tm=512):
    t, d = x2.shape
    row = lambda i: (i, 0)
    return pl.pallas_call(
        _moe_combine_body, grid=(t // tm,),
        in_specs=[pl.BlockSpec((tm, d), row), pl.BlockSpec((tm, d), row), pl.BlockSpec((tm, d), row),
                  pl.BlockSpec((tm, LANES), row)],
        out_specs=pl.BlockSpec((tm, d), row),
        out_shape=jax.ShapeDtypeStruct((t, d), F32),
        compiler_params=_cparams(("parallel",)), name="moe_combine",
    )(x2, y0, y1, route)


def moe_residual(x2, h, route, counts, w_gate, w_up, w_down):
    t, d = x2.shape
    n_e = w_gate.shape[0]
    rows = MOE_ROWS
    expert = route[:, ROUTE_IDX:ROUTE_IDX + 2].astype(jnp.int32).reshape(-1)
    rank = route[:, ROUTE_RANK:ROUTE_RANK + 2].astype(jnp.int32).reshape(-1)
    count = counts[0, :n_e].astype(jnp.int32)
    padded = (count + rows - 1) // rows * rows
    ends = jnp.cumsum(padded)
    starts = ends - padded
    own = expert[:, None] == jnp.arange(n_e, dtype=jnp.int32)[None, :]
    slot = jnp.sum(jnp.where(own, starts[None, :], 0), axis=1) + rank
    p_rows = 2 * t + n_e * rows
    block_start = jnp.arange(p_rows // rows, dtype=jnp.int32) * rows
    block_expert = jnp.minimum(jnp.sum(block_start[:, None] >= ends[None, :], axis=1), n_e - 1).astype(jnp.int32)
    n_used = (ends[-1] // rows).astype(jnp.int32).reshape(1)
    copy_order = jnp.argsort(slot).astype(jnp.int32)
    within = (block_start - starts[block_expert])[:, None] + jnp.arange(rows, dtype=jnp.int32)[None, :]
    dense = (jnp.cumsum(count) - count)[block_expert][:, None] + within
    real = within < count[block_expert][:, None]
    picked = copy_order.at[jnp.clip(dense, 0, 2 * t - 1)].get(mode="promise_in_bounds")
    token_of_slot = jnp.where(real, picked // 2, 0).reshape(p_rows)

    take_rows = lambda a, idx: a.at[idx].get(mode="promise_in_bounds")
    xs = take_rows(h, token_of_slot)
    ys = moe_expert_ffn(xs, block_expert, n_used, w_gate, w_up, w_down)
    slot2 = slot.reshape(t, 2)
    return moe_combine(x2, take_rows(ys, slot2[:, 0]), take_rows(ys, slot2[:, 1]), route)


def _router_body(x_ref, a_ref, b_ref, wa_ref, wb_ref, g_ref, rhi_ref, rlo_ref,
                 xo_ref, h_ref, o_ref, count_ref, run_ref):
    @pl.when(pl.program_id(0) == 0)
    def _():
        run_ref[...] = jnp.zeros(run_ref.shape, F32)

    x = x_ref[...] + _dot(a_ref[...], wa_ref[...]) + _dot(b_ref[...], wb_ref[...])
    xo_ref[...] = x
    h = _rms(x, g_ref[...])
    h_ref[...] = h.astype(h_ref.dtype)
    h_hi, h_lo = _split_bf16(h, 2)
    logits = _dot(h_hi, rhi_ref[...]) + _dot(h_hi, rlo_ref[...]) + _dot(h_lo, rhi_ref[...])
    lane = lax.broadcasted_iota(jnp.int32, logits.shape, 1).astype(F32)
    low = jnp.float32(-3.0e38)
    logits = jnp.where(lane < N_EXPERTS, logits, low)
    m1 = jnp.max(logits, axis=1, keepdims=True)
    i1 = jnp.min(jnp.where(logits == m1, lane, float(LANES)), axis=1, keepdims=True)
    rest = jnp.where(lane == i1, low, logits)
    m2 = jnp.max(rest, axis=1, keepdims=True)
    i2 = jnp.min(jnp.where(rest == m2, lane, float(LANES)), axis=1, keepdims=True)
    ex = jnp.exp(m2 - m1)
    w1 = 1.0 / (1.0 + ex)
    w2 = ex / (1.0 + ex)
    tm = logits.shape[0]
    routed = jnp.where((lane == i1) | (lane == i2), 1.0, 0.0)
    row = lax.broadcasted_iota(jnp.int32, (tm, tm), 0)
    col = lax.broadcasted_iota(jnp.int32, (tm, tm), 1)
    before = jnp.where(col < row, 1.0, 0.0).astype(BF16)
    rank = run_ref[0:1, :] + _dot(before, routed.astype(BF16))
    r1 = jnp.sum(jnp.where(lane == i1, rank, 0.0), axis=1, keepdims=True)
    r2 = jnp.sum(jnp.where(lane == i2, rank, 0.0), axis=1, keepdims=True)
    fields = ((ROUTE_IDX, i1), (ROUTE_IDX + 1, i2), (ROUTE_W, w1), (ROUTE_W + 1, w2),
              (ROUTE_RANK, r1), (ROUTE_RANK + 1, r2))
    out = jnp.zeros(logits.shape, F32)
    for pos, val in fields:
        out = jnp.where(lane == pos, val, out)
    o_ref[...] = out
    run_ref[...] = run_ref[...] + jnp.sum(routed, axis=0, keepdims=True)
    count_ref[...] = run_ref[...]


def out_proj_router(x2, a, bm, wa, wb, gain, router_w, tm=512):
    t, d = x2.shape
    r_pad = jnp.pad(router_w.astype(F32), ((0, 0), (0, LANES - router_w.shape[1])))
    r_hi, r_lo = _split_bf16(r_pad, 2)
    row = lambda i: (i, 0)
    const = lambda i: (0, 0)
    return pl.pallas_call(
        _router_body, grid=(t // tm,),
        in_specs=[pl.BlockSpec((tm, d), row), pl.BlockSpec((tm, a.shape[1]), row),
                  pl.BlockSpec((tm, bm.shape[1]), row), pl.BlockSpec(wa.shape, const),
                  pl.BlockSpec(wb.shape, const), pl.BlockSpec((1, d), const),
                  pl.BlockSpec((d, LANES), const), pl.BlockSpec((d, LANES), const)],
        out_specs=[pl.BlockSpec((tm, d), row), pl.BlockSpec((tm, d), row), pl.BlockSpec((tm, LANES), row),
                   pl.BlockSpec((SUBLANES, LANES), const)],
        out_shape=[jax.ShapeDtypeStruct((t, d), F32), jax.ShapeDtypeStruct((t, d), BF16),
                   jax.ShapeDtypeStruct((t, LANES), F32), jax.ShapeDtypeStruct((SUBLANES, LANES), F32)],
        scratch_shapes=[pltpu.VMEM((SUBLANES, LANES), F32)],
        compiler_params=_cparams(("arbitrary",)), name="out_proj_router",
    )(x2, a, bm, wa, wb, gain.reshape(1, d).astype(F32), r_hi, r_lo)


def _compress_body(ch_ref, nx_ref, pos_ref, w1_ref, w2_ref, gain_ref, cos_ref, sin_ref, rot_ref, o_ref,
                   *, is_key):
    a = _dot((ch_ref[0] + pos_ref[0]).astype(BF16), w1_ref[0, 0])
    a = a + _dot((nx_ref[0] + pos_ref[1]).astype(BF16), w1_ref[0, 1])
    out = _dot(_silu(a).astype(BF16), w2_ref[...])
    if is_key:
        out = _rms(out, gain_ref[...])
        hi, lo = _split_bf16(out, 2)
        partner = _dot(hi, rot_ref[...]) + _dot(lo, rot_ref[...])
        out = out * cos_ref[...] + partner * sin_ref[...]
    o_ref[0, 0] = out


def nsa_compress(t, pos, w1, w2, gain, seq, is_key):
    b, s, _ = t.shape
    g, d = NSA_KV_GROUPS, NSA_HEAD_DIM
    n_ch = s // NSA_CMP_STRIDE
    half = NSA_CMP_STRIDE * g * d
    ch = t.reshape(b, n_ch, half)
    nxt = jnp.concatenate([ch[:, 1:], jnp.zeros((b, 1, half), F32)], axis=1)
    pos2 = jnp.broadcast_to(pos.astype(F32).reshape(2, NSA_CMP_STRIDE, 1, d),
                            (2, NSA_CMP_STRIDE, g, d)).reshape(2, 1, half)
    w1r = w1.astype(BF16).reshape(2, NSA_CMP_STRIDE, 1, d, d)
    own = (jnp.arange(g)[:, None] == jnp.arange(g)[None, :]).reshape(g, 1, 1, g, 1, 1)
    w1s = jnp.where(own, w1r[None], jnp.zeros((), BF16)).reshape(g, 2, half, d)
    cmp_end = jnp.arange(n_ch) * NSA_CMP_STRIDE + NSA_CMP_BLOCK - 1
    inv_freq = 1.0 / (ROPE_THETA ** (jnp.arange(0, d, 2, dtype=F32) / d))
    ang = cmp_end.astype(F32)[:, None] * inv_freq[None, :]
    cos = jnp.concatenate([jnp.cos(ang)] * 2, axis=1)
    sin = jnp.concatenate([jnp.sin(ang)] * 2, axis=1)
    rot = np.zeros((d, d), np.float32)
    rot[np.arange(d // 2) + d // 2, np.arange(d // 2)] = -1.0
    rot[np.arange(d // 2), np.arange(d // 2) + d // 2] = 1.0
    blk = lambda bb, gg: (bb, gg, 0, 0)
    seq = lambda bb, gg: (bb, 0, 0)
    c2 = lambda bb, gg: (0, 0)
    c3 = lambda bb, gg: (0, 0, 0)
    return pl.pallas_call(
        functools.partial(_compress_body, is_key=is_key), grid=(b, g),
        in_specs=[pl.BlockSpec((1, n_ch, half), seq), pl.BlockSpec((1, n_ch, half), seq),
                  pl.BlockSpec((2, 1, half), c3), pl.BlockSpec((1, 2, half, d), lambda bb, gg: (gg, 0, 0, 0)),
                  pl.BlockSpec((d, d), c2), pl.BlockSpec((1, d), c2),
                  pl.BlockSpec((n_ch, d), c2), pl.BlockSpec((n_ch, d), c2), pl.BlockSpec((d, d), c2)],
        out_specs=pl.BlockSpec((1, 1, n_ch, d), blk),
        out_shape=jax.ShapeDtypeStruct((b, g, n_ch, d), F32),
        compiler_params=_cparams(("parallel", "parallel")), name="nsa_compress",
    )(ch, nxt, pos2, w1s, w2.astype(BF16), gain.reshape(1, d).astype(F32), cos, sin,
      jnp.asarray(rot, dtype=BF16))


def _nsa_body(q_ref, ck_ref, cvt_ref, ksl_ref, vslt_ref, kwn_ref, vwnt_ref, ovt_ref, glt_ref, o_ref,
              sc_ref, phi_ref, plo_ref, imp_ref, bias_ref, ss_ref, ps_ref, ss1_ref, ps1_ref, sw_ref, pw_ref,
              ow_ref, oc_ref, *, tq):
    g = pl.program_id(1)
    i = pl.program_id(2)
    d = NSA_HEAD_DIM
    rep = NSA_HEADS // NSA_KV_GROUPS
    t0 = i * tq
    n_cmp = ck_ref.shape[1]
    n_sel = ovt_ref.shape[0]
    width = rep * tq
    sub = NSA_SEL_BLOCK
    dead = 0.5 * NEG_INF
    v_rows = pl.ds(pl.multiple_of(g * d, d), d)

    qb = q_ref[0]
    q4 = jnp.concatenate([qb[:, r * d:(r + 1) * d] for r in range(rep)], axis=0)
    q4 = jnp.concatenate([q4, q4], axis=1)
    lane = lax.broadcasted_iota(jnp.int32, q4.shape, 1)
    q4 = jnp.where(jnp.right_shift(lane, d.bit_length() - 1) == g, q4, jnp.zeros_like(q4))

    def qpos_of(shape):
        return t0 + (lax.broadcasted_iota(jnp.int32, shape, 1) & (tq - 1))

    def compressed(rows):
        s = _dot_nt(ck_ref[0, 0:rows, :], q4)
        cmp_end = lax.broadcasted_iota(jnp.int32, s.shape, 0) * NSA_CMP_STRIDE + (NSA_CMP_BLOCK - 1)
        s = jnp.where(cmp_end <= qpos_of(s.shape), s, NEG_INF)
        sc_ref[0:rows, :] = s
        m_c = jnp.max(s, axis=0, keepdims=True)
        lpart = jnp.zeros((SUBLANES, width), F32)
        for r in range(rows // sub):
            e = jnp.exp2(sc_ref[r * sub:(r + 1) * sub, :] - m_c)
            lpart = lpart + _fold_rows(e)
            hi = e.astype(BF16)
            phi_ref[r * sub:(r + 1) * sub, :] = hi
            plo_ref[r * sub:(r + 1) * sub, :] = (e - hi.astype(F32)).astype(BF16)
        inv_c = jnp.where(m_c > dead, 1.0 / jnp.sum(lpart, axis=0, keepdims=True), 0.0)
        oc_ref[...] = _dot(cvt_ref[0, v_rows, 0:rows], phi_ref[0:rows, :]) * inv_c
        ovt = ovt_ref[:, 0:rows]
        imp4 = (_dot(ovt, phi_ref[0:rows, :]) + _dot(ovt, plo_ref[0:rows, :])) * inv_c
        imp = imp4[:, 0:tq]
        for r in range(1, rep):
            imp = imp + imp4[:, r * tq:(r + 1) * tq]
        imp_ref[...] = imp

    cmp_live = jnp.clip((t0 + tq - NSA_CMP_BLOCK) // NSA_CMP_STRIDE + 1, 1, n_cmp)
    cmp_step = min(2 * LANES, n_cmp)
    for v in range(n_cmp // cmp_step):
        @pl.when((cmp_live > cmp_step * v) & (cmp_live <= cmp_step * (v + 1)))
        def _():
            compressed(cmp_step * (v + 1))

    o_c = oc_ref[...]
    imp = imp_ref[...]

    blk = lax.broadcasted_iota(jnp.int32, imp.shape, 0)
    qp = t0 + lax.broadcasted_iota(jnp.int32, imp.shape, 1)
    cur = jnp.right_shift(qp, NSA_SEL_BLOCK.bit_length() - 1)
    forced = (blk == 0) | (blk == cur) | (blk == cur - 1)
    future = blk * NSA_SEL_BLOCK > qp
    imp_ref[...] = jnp.where(future, -FORCE_SCORE, jnp.where(forced, FORCE_SCORE, imp))
    bias_ref[...] = jnp.full(bias_ref.shape, NEG_INF, F32)

    n_live = jnp.minimum((t0 + tq - 1) // NSA_SEL_BLOCK + 1, n_sel)
    n_var = max(n_sel // 32, 1)
    rows_per = n_sel // n_var
    top_n = float(min(NSA_TOP_N, n_sel))
    for v in range(n_var):
        rows = rows_per * (v + 1)

        @pl.when((n_live > rows_per * v) & (n_live <= rows))
        def _():
            groups = rows // SUBLANES
            mine = [imp_ref[gi * SUBLANES:(gi + 1) * SUBLANES, :] for gi in range(groups)]
            rank = [jnp.zeros((SUBLANES, tq), F32) for _ in range(groups)]
            in_group = lax.broadcasted_iota(jnp.int32, (SUBLANES, tq), 0)
            for i2 in range(rows):
                other = imp_ref[i2:i2 + 1, :]
                for gi in range(groups):
                    if gi > i2 // SUBLANES:
                        beats = other >= mine[gi]
                    elif gi < i2 // SUBLANES:
                        beats = other > mine[gi]
                    else:
                        beats = (other > mine[gi]) | ((other == mine[gi]) & (in_group > i2 % SUBLANES))
                    rank[gi] = rank[gi] + jnp.where(beats, 1.0, 0.0)
            for gi in range(groups):
                bias = jnp.where(rank[gi] < top_n, 0.0, NEG_INF)
                bias_ref[gi * SUBLANES:(gi + 1) * SUBLANES, :] = jnp.concatenate([bias] * rep, axis=1)

    init = (jnp.full((1, width), NEG_INF, F32), jnp.zeros((d + ONES_ROWS, width), F32))

    def normalised(acc):
        return acc[:d, :] * (1.0 / acc[d:d + 1, :])

    chunk = 8 * sub
    n_sub = chunk // sub

    s_slots = (ss_ref, ss1_ref)
    p_slots = (ps_ref, ps1_ref)

    def sel_scores(c, slot, diagonal=False):
        start = pl.multiple_of(c * chunk, chunk)
        s = _dot_nt(ksl_ref[0, pl.ds(start, chunk), :], q4)
        if diagonal:
            kpos = start + lax.broadcasted_iota(jnp.int32, s.shape, 0)
            s = jnp.where(kpos <= qpos_of(s.shape), s, NEG_INF)
        s_slots[slot][...] = s

    def sel_update(c, slot, carry):
        m_prev, acc = carry
        s_ref, p_ref = s_slots[slot], p_slots[slot]
        biases = [bias_ref[pl.ds(c * n_sub + r, 1), :] for r in range(n_sub)]
        m8 = jnp.full((SUBLANES, width), NEG_INF, F32)
        for r in range(n_sub):
            block = s_ref[r * sub:(r + 1) * sub, :]
            m8 = jnp.maximum(m8, jnp.max(block.reshape(sub // SUBLANES, SUBLANES, width), axis=0) + biases[r])
        m_new = jnp.maximum(m_prev, jnp.max(m8, axis=0, keepdims=True))
        live = m_new > dead
        for r in range(n_sub):
            shift = jnp.where(live, biases[r] - m_new, NEG_INF)
            p_ref[r * sub:(r + 1) * sub, :] = jnp.exp2(s_ref[r * sub:(r + 1) * sub, :] + shift).astype(BF16)
        vt = _with_ones_rows(vslt_ref[v_rows, pl.ds(pl.multiple_of(c * chunk, chunk), chunk)])
        return m_new, jnp.exp2(m_prev - m_new) * acc + _dot(vt, p_ref[...])

    c_diag = (t0 + tq - 1) // chunk
    n_chunks = c_diag + 1
    last_past = jnp.maximum(c_diag - 1, 0)

    def chunk_at(j):
        return jnp.where(j == 0, c_diag, j - 1)

    sel_scores(c_diag, 0, diagonal=True)

    def pair(k, carry):
        sel_scores(jnp.minimum(2 * k, last_past), 1)
        carry = sel_update(chunk_at(2 * k), 0, carry)
        sel_scores(jnp.minimum(2 * k + 1, last_past), 0)
        return sel_update(2 * k, 1, carry)

    carry = lax.fori_loop(0, n_chunks // 2, pair, init)
    _, acc_s = lax.cond(n_chunks % 2 == 1, lambda cr: sel_update(chunk_at(n_chunks - 1), 0, cr),
                        lambda cr: cr, carry)
    o_s = normalised(acc_s)

    def win_chunk(c, carry):
        m_prev, acc = carry
        start = pl.multiple_of(c * tq, tq)
        s = _dot_nt(kwn_ref[0, pl.ds(start, tq), :], q4)
        kpos = start + lax.broadcasted_iota(jnp.int32, s.shape, 0)
        qpos = qpos_of(s.shape)
        s = jnp.where((kpos <= qpos) & (kpos > qpos - NSA_WINDOW), s, NEG_INF)
        m_new = jnp.maximum(m_prev, jnp.max(s, axis=0, keepdims=True))
        p = jnp.exp2(s + jnp.where(m_new > dead, -m_new, NEG_INF))
        vt = _with_ones_rows(vwnt_ref[v_rows, pl.ds(start, tq)])
        return m_new, jnp.exp2(m_prev - m_new) * acc + _dot(vt, p.astype(BF16))

    n_back = NSA_WINDOW // tq

    @pl.when(i < n_back)
    def _():
        _, acc_w = lax.fori_loop(0, i + 1, win_chunk, init)
        ow_ref[...] = normalised(acc_w)

    @pl.when(i >= n_back)
    def _():
        start = pl.multiple_of(t0 - NSA_WINDOW, tq)
        s = _dot_nt(kwn_ref[0, pl.ds(start, NSA_WINDOW + tq), :], q4)
        kpos = start + lax.broadcasted_iota(jnp.int32, (tq, width), 0)
        qpos = qpos_of((tq, width))
        sw_ref[0:tq, :] = jnp.where(kpos > qpos - NSA_WINDOW, s[0:tq, :], NEG_INF)
        sw_ref[tq:NSA_WINDOW, :] = s[tq:NSA_WINDOW, :]
        sw_ref[NSA_WINDOW:, :] = jnp.where(kpos + NSA_WINDOW <= qpos, s[NSA_WINDOW:, :], NEG_INF)
        m_w = jnp.max(sw_ref[...], axis=0, keepdims=True)
        for r in range((NSA_WINDOW + tq) // sub):
            pw_ref[r * sub:(r + 1) * sub, :] = jnp.exp2(sw_ref[r * sub:(r + 1) * sub, :] - m_w).astype(BF16)
        vt = _with_ones_rows(vwnt_ref[v_rows, pl.ds(start, NSA_WINDOW + tq)])
        ow_ref[...] = normalised(_dot(vt, pw_ref[...]))

    o_w = ow_ref[...]

    def gate(branch):
        rows = [glt_ref[pl.ds((g * rep + r) * 3 + branch, 1), :] for r in range(rep)]
        return _sigmoid(jnp.concatenate(rows, axis=1))

    out = gate(0) * o_c + gate(1) * o_s + gate(2) * o_w
    out_t = jnp.concatenate([out, jnp.zeros_like(out)], axis=0).T
    o_ref[0] = jnp.concatenate([out_t[r * tq:(r + 1) * tq, :d] for r in range(rep)],
                               axis=1).astype(o_ref.dtype)


def nsa_overlap_t(n_cmp, n_sel):
    c_start = np.arange(n_cmp)[None, :] * NSA_CMP_STRIDE
    s_start = np.arange(n_sel)[:, None] * NSA_SEL_BLOCK
    hit = (c_start < s_start + NSA_SEL_BLOCK) & (c_start + NSA_CMP_BLOCK > s_start)
    hit = hit & (np.arange(n_cmp)[None, :] < n_cmp - NSA_CMP_BLOCK // NSA_CMP_STRIDE + 1)
    return jnp.asarray(hit.astype(np.float32), dtype=BF16)


def nsa_attention(qn, ck, cvt, ksl, vslt, kwn, vwnt, glt, tq=256):
    b, s, _ = qn.shape
    g, d = NSA_KV_GROUPS, NSA_HEAD_DIM
    rep = NSA_HEADS // g
    n_cmp = ck.shape[1]
    n_sel = s // NSA_SEL_BLOCK
    nq = s // tq
    ovt = nsa_overlap_t(n_cmp, n_sel)
    width = rep * tq
    chunk = 8 * NSA_SEL_BLOCK
    full3 = lambda bb, gg, i: (bb, 0, 0)
    seq_t = lambda bb, gg, i: (0, bb)
    return pl.pallas_call(
        functools.partial(_nsa_body, tq=tq), grid=(b, g, nq),
        in_specs=[pl.BlockSpec((1, tq, rep * d), lambda bb, gg, i: (bb, i, gg)),
                  pl.BlockSpec((1, n_cmp, g * d), full3), pl.BlockSpec((1, g * d, n_cmp), full3),
                  pl.BlockSpec((1, s, g * d), full3), pl.BlockSpec((g * d, s), seq_t),
                  pl.BlockSpec((1, s, g * d), full3), pl.BlockSpec((g * d, s), seq_t),
                  pl.BlockSpec((n_sel, n_cmp), lambda bb, gg, i: (0, 0)),
                  pl.BlockSpec((glt.shape[0], tq), lambda bb, gg, i: (0, bb * nq + i))],
        out_specs=pl.BlockSpec((1, tq, rep * d), lambda bb, gg, i: (bb, i, gg)),
        out_shape=jax.ShapeDtypeStruct((b, s, g * rep * d), BF16),
        scratch_shapes=[pltpu.VMEM((n_cmp, width), F32), pltpu.VMEM((n_cmp, width), BF16),
                        pltpu.VMEM((n_cmp, width), BF16), pltpu.VMEM((n_sel, tq), F32),
                        pltpu.VMEM((n_sel, width), F32), pltpu.VMEM((chunk, width), F32),
                        pltpu.VMEM((chunk, width), BF16), pltpu.VMEM((chunk, width), F32),
                        pltpu.VMEM((chunk, width), BF16), pltpu.VMEM((NSA_WINDOW + tq, width), F32),
                        pltpu.VMEM((NSA_WINDOW + tq, width), BF16), pltpu.VMEM((d, width), F32),
                        pltpu.VMEM((d, width), F32)],
        compiler_params=_cparams(("parallel", "parallel", "arbitrary")), name="nsa_attention",
    )(qn, ck, cvt, ksl, vslt, kwn, vwnt, ovt, glt)


def _pad_cols(w, n):
    return jnp.pad(w, ((0, 0), (0, n - w.shape[1])))


def _even_layer(x2, b, s, layer_idx, norm_mix, w_in, q_gain, k_gain, lam, subln_gain, conv_w, conv_b,
                dt_bias, a_log, d_skip, ssm_norm_gain, w_out, norm_ffn, w_gate, w_up, w_down):
    nq = DA_HEADS * 2 * DA_HEAD_DIM
    nv = DA_HEADS * DA_V_DIM
    cch = SSM_D_INNER + 2 * SSM_GROUPS * SSM_STATE
    offs = np.cumsum([0, nq, nq, nv, SSM_D_INNER, cch, SSM_HEADS])
    wb = w_in.astype(BF16)
    pieces = [wb[:, offs[k]:offs[k + 1]] for k in range(6)]
    pieces[2] = pieces[2].T
    pieces[5] = _pad_cols(pieces[5], LANES)
    posts = [HeadNorm(q_gain, DA_HEAD_DIM, rope=True, mul=DA_HEAD_DIM ** -0.5 * LOG2E),
             HeadNorm(k_gain, DA_HEAD_DIM, rope=True), None, None, None, None]
    q, k, vt, z, xbc, dt = norm_proj(x2, norm_mix, pieces, [BF16, BF16, BF16, F32, F32, F32], posts, s,
                                     _rope_tables(s, DA_HEAD_DIM), transposed=(2,))
    qn = q.reshape(b, s, nq)
    kn = k.reshape(b, s, nq)
    lam_init = 0.8 - 0.6 * math.exp(-0.3 * layer_idx)
    lf = lam.astype(F32)
    lam_full = jnp.exp(jnp.sum(lf[0] * lf[1])) - jnp.exp(jnp.sum(lf[2] * lf[3])) + lam_init
    a_out = flash_attention(lam_full.reshape(1), [qn], [kn], vt, subln_gain, DA_HEADS, DA_V_DIM,
                            diff=True, out_scale=1.0 - lam_init)
    b_out = ssd_mixer(xbc.reshape(b, s, cch), z.reshape(b, s, SSM_D_INNER), dt.reshape(b, s, LANES),
                      conv_w, conv_b, dt_bias, a_log, d_skip, ssm_norm_gain)
    wo = w_out.astype(BF16)
    return out_proj_ffn_residual(x2, a_out.reshape(-1, nv), b_out.reshape(-1, SSM_D_INNER), wo[:nv], wo[nv:],
                                 norm_ffn, w_gate.astype(BF16), w_up.astype(BF16), w_down.astype(BF16))


def _odd_layer(x2, b, s, norm_mix, w_in, q_gain, k_gain, cmp_pos, cmp_w1, cmp_w2, cq_gain, ckv_gain,
               w_uq, w_ukv, qn_gain, qr_gain, kn_gain, kr_gain, w_out, norm_ffn, router_w, w_gate, w_up,
               w_down):
    g, d = NSA_KV_GROUPS, NSA_HEAD_DIM
    nq = NSA_HEADS * d
    nkv = g * d
    sizes = [nq] + [nkv] * 6 + [NSA_HEADS * 3, w_uq.shape[0], w_ukv.shape[0], MLA_ROPE_DIM]
    offs = np.cumsum([0] + sizes)
    wb = w_in.astype(BF16)
    pieces = [wb[:, offs[k]:offs[k + 1]] for k in range(len(sizes))]
    for k in (4, 6):
        pieces[k] = pieces[k].T
    pieces[7] = jnp.pad(pieces[7].T, ((0, 32 - NSA_HEADS * 3), (0, 0)))
    pieces[10] = _pad_cols(pieces[10], LANES)
    tables = _rope_tables(s, d)
    posts = [None] * len(sizes)
    posts[0] = HeadNorm(q_gain, d, rope=True, mul=d ** -0.5 * LOG2E)
    posts[3] = HeadNorm(k_gain[1], d, rope=True)
    posts[5] = HeadNorm(k_gain[2], d, rope=True)
    posts[10] = HeadNorm(kr_gain, MLA_ROPE_DIM, rope=True)
    (q, kc, vc, ksl, vslt, kwn, vwnt, glt, cq, ckv, k_rope) = norm_proj(
        x2, norm_mix, pieces, [BF16, F32, F32, BF16, BF16, BF16, BF16, F32, F32, F32, BF16], posts, s, tables,
        transposed=(4, 6, 7))

    qn = q.reshape(b, s, nq)
    ksl_n = ksl.reshape(b, s, nkv)
    kwn_n = kwn.reshape(b, s, nkv)
    ck = nsa_compress(kc.reshape(b, s, nkv), cmp_pos[0], cmp_w1[0], cmp_w2[0], k_gain[0], s, True)
    cv = nsa_compress(vc.reshape(b, s, nkv), cmp_pos[1], cmp_w1[1], cmp_w2[1], k_gain[0], s, False)
    n_cmp = ck.shape[2]
    ck = ck.transpose(0, 2, 1, 3).reshape(b, n_cmp, nkv).astype(BF16)
    cvt = cv.transpose(0, 1, 3, 2).reshape(b, nkv, n_cmp).astype(BF16)
    c_out = nsa_attention(qn, ck, cvt, ksl_n, vslt, kwn_n, vwnt, glt).reshape(b * s, nq)

    h = MLA_HEADS
    dqk = MLA_NOPE_DIM + MLA_ROPE_DIM
    wq = w_uq.astype(BF16).reshape(-1, h, dqk)
    wq_nope = wq[:, :, :MLA_NOPE_DIM].reshape(-1, h * MLA_NOPE_DIM)
    wq_rope = jnp.pad(wq[:, :, MLA_NOPE_DIM:], ((0, 0), (0, 0), (0, LANES - MLA_ROPE_DIM)))
    wq_rope = wq_rope.reshape(-1, h * LANES)
    wkv = w_ukv.astype(BF16).reshape(-1, h, MLA_NOPE_DIM + MLA_V_DIM)
    wk_nope = wkv[:, :, :MLA_NOPE_DIM].reshape(-1, h * MLA_NOPE_DIM)
    wv = wkv[:, :, MLA_NOPE_DIM:].reshape(-1, h * MLA_V_DIM)
    q_mul = dqk ** -0.5 * LOG2E
    q_nope, q_rope = norm_proj(
        cq, cq_gain, [wq_nope, wq_rope], [BF16, BF16],
        [HeadNorm(qn_gain, MLA_NOPE_DIM, mul=q_mul), HeadNorm(qr_gain, MLA_ROPE_DIM, rope=True, mul=q_mul)],
        s, tables)
    k_nope, vt = norm_proj(ckv, ckv_gain, [wk_nope, wv.T], [BF16, BF16],
                           [HeadNorm(kn_gain, MLA_NOPE_DIM), None], transposed=(1,))
    shp = lambda t: t.reshape(b, s, t.shape[-1])
    d_out = flash_attention(jnp.zeros((1,), F32), [shp(q_nope), shp(q_rope)], [shp(k_nope), shp(k_rope)],
                            vt, jnp.ones((MLA_V_DIM,), F32), h, MLA_V_DIM, diff=False)

    wo = w_out.astype(BF16)
    x2, hn, route, counts = out_proj_router(x2, c_out, d_out.reshape(b * s, h * MLA_V_DIM), wo[:nq], wo[nq:],
                                            norm_ffn, router_w)
    return moe_residual(x2, hn, route, counts, w_gate.astype(BF16), w_up.astype(BF16), w_down.astype(BF16))


def kernel(x, ev_norm_mix, ev_w_in, da_q_gain, da_k_gain, da_lambda, da_subln_gain, ssm_conv_w, ssm_conv_b, ssm_dt_bias, ssm_a_log, ssm_d, ssm_norm_gain, ev_w_out, ev_norm_ffn, ffn_w_gate, ffn_w_up, ffn_w_down, od_norm_mix, od_w_in, nsa_q_gain, nsa_k_gain, nsa_cmp_pos, nsa_cmp_w1, nsa_cmp_w2, mla_cq_gain, mla_ckv_gain, mla_w_uq, mla_w_ukv, mla_qn_gain, mla_qr_gain, mla_kn_gain, mla_kr_gain, od_w_out, od_norm_ffn, moe_router, moe_w_gate, moe_w_up, moe_w_down):
    b, s, d = x.shape
    x2 = x.reshape(b * s, d)
    depth = ev_norm_mix.shape[0] + od_norm_mix.shape[0]
    for layer in range(depth):
        i = layer // 2
        if layer % 2 == 0:
            x2 = _even_layer(x2, b, s, layer, ev_norm_mix[i], ev_w_in[i], da_q_gain[i], da_k_gain[i],
                             da_lambda[i], da_subln_gain[i], ssm_conv_w[i], ssm_conv_b[i],
                             ssm_dt_bias[i], ssm_a_log[i], ssm_d[i], ssm_norm_gain[i], ev_w_out[i],
                             ev_norm_ffn[i], ffn_w_gate[i], ffn_w_up[i], ffn_w_down[i])
        else:
            x2 = _odd_layer(x2, b, s, od_norm_mix[i], od_w_in[i], nsa_q_gain[i], nsa_k_gain[i],
                            nsa_cmp_pos[i], nsa_cmp_w1[i], nsa_cmp_w2[i], mla_cq_gain[i],
                            mla_ckv_gain[i], mla_w_uq[i], mla_w_ukv[i], mla_qn_gain[i], mla_qr_gain[i],
                            mla_kn_gain[i], mla_kr_gain[i], od_w_out[i], od_norm_ffn[i], moe_router[i],
                            moe_w_gate[i], moe_w_up[i], moe_w_down[i])
    return x2.reshape(b, s, d)
```

```python
import functools
import math

import numpy as np
import jax
import jax.numpy as jnp
from jax import lax
from jax.experimental import pallas as pl
from jax.experimental.pallas import tpu as pltpu

F32 = jnp.float32
BF16 = jnp.bfloat16

ROPE_THETA = 10000.0
NORM_EPS = 1e-6
NEG_INF = -1e30
FORCE_SCORE = 1e6
LOG2E = 1.4426950408889634

DA_HEADS = 4
DA_HEAD_DIM = 64
DA_V_DIM = 2 * DA_HEAD_DIM
SSM_HEADS = 8
SSM_HEAD_DIM = 64
SSM_D_INNER = SSM_HEADS * SSM_HEAD_DIM
SSM_GROUPS = 2
SSM_STATE = 128
SSM_CONV = 4
SSM_CHUNK = 256
NSA_HEADS = 8
NSA_KV_GROUPS = 2
NSA_HEAD_DIM = 64
NSA_CMP_BLOCK = 32
NSA_CMP_STRIDE = 16
NSA_SEL_BLOCK = 64
NSA_TOP_N = 16
NSA_WINDOW = 512
MLA_HEADS = 4
MLA_NOPE_DIM = 128
MLA_ROPE_DIM = 64
MLA_V_DIM = 128
N_EXPERTS = 8

LANES = 128
SUBLANES = 8
VMEM_LIMIT = 48 * 1024 * 1024
MOE_VMEM_LIMIT = 58 * 1024 * 1024

NT_DIMS = (((1,), (1,)), ((), ()))


def _cparams(semantics):
    return pltpu.CompilerParams(dimension_semantics=semantics, vmem_limit_bytes=VMEM_LIMIT)


def _dot(a, b):
    return jnp.dot(a, b, preferred_element_type=F32)


def _dot_nt(a, b):
    return lax.dot_general(a, b, NT_DIMS, preferred_element_type=F32)


def _split_bf16(x, parts):
    out = []
    for _ in range(parts):
        hi = x.astype(BF16)
        out.append(hi)
        x = x - hi.astype(F32)
    return out


def _fold_rows(x):
    return jnp.sum(x.reshape(x.shape[0] // SUBLANES, SUBLANES, x.shape[1]), axis=0)


ONES_ROWS = 16


def _with_ones_rows(vt):
    return jnp.concatenate([vt, jnp.ones((ONES_ROWS, vt.shape[1]), vt.dtype)], axis=0)


def _sigmoid(x):
    return 1.0 / (1.0 + jnp.exp(-x))


def _silu(x):
    return x * _sigmoid(x)


def _softplus(x):
    return jnp.maximum(x, 0.0) + jnp.log(1.0 + jnp.exp(-jnp.abs(x)))


def _rms(x, gain):
    ms = jnp.mean(x * x, axis=-1, keepdims=True)
    return x * lax.rsqrt(ms + NORM_EPS) * gain


class HeadNorm:
    def __init__(self, gain, hd, rope=False, mul=1.0):
        self.gain, self.hd, self.rope, self.mul = gain, hd, rope, mul


def _head_norm(y, gain, bd, cos_ref, sin_ref, post):
    n = y.shape[1]
    hd = post.hd
    hi, lo = _split_bf16(y * y, 2)
    ss = _dot(hi, bd) + _dot(lo, bd)
    yn = y * lax.rsqrt(ss * (1.0 / hd) + NORM_EPS) * gain
    if post.rope:
        reps = n // LANES
        cos = jnp.concatenate([cos_ref[...]] * reps, axis=1) if reps > 1 else cos_ref[...]
        sin = jnp.concatenate([sin_ref[...]] * reps, axis=1) if reps > 1 else sin_ref[...]
        lane = lax.broadcasted_iota(jnp.int32, yn.shape, 1)
        first_half = (lane & (hd - 1)) < (hd // 2)
        partner = jnp.where(first_half, pltpu.roll(yn, n - hd // 2, 1), pltpu.roll(yn, hd // 2, 1))
        yn = yn * cos + partner * sin
    if post.mul != 1.0:
        yn = yn * post.mul
    return yn


def _norm_proj_body(x_ref, g_ref, *refs, posts, use_rope, transposed):
    if use_rope:
        cos_ref, sin_ref = refs[0], refs[1]
        refs = refs[2:]
    else:
        cos_ref = sin_ref = None
    n_out = len(posts)
    n_aux = 2 * sum(p is not None for p in posts)
    w_refs, aux, o_refs = refs[:n_out], refs[n_out:n_out + n_aux], refs[n_out + n_aux:]
    h = _rms(x_ref[...], g_ref[...]).astype(BF16)
    a = 0
    for k, (w_ref, o_ref, post) in enumerate(zip(w_refs, o_refs, posts)):
        if k in transposed:
            o_ref[...] = _dot_nt(w_ref[...], h).astype(o_ref.dtype)
            continue
        y = _dot(h, w_ref[...])
        if post is not None:
            y = _head_norm(y, aux[a][...], aux[a + 1][...], cos_ref, sin_ref, post)
            a += 2
        o_ref[...] = y.astype(o_ref.dtype)


def norm_proj(x2, gain, weights, out_dtypes, posts=None, seq=None, rope_tables=None, transposed=(), tm=512):
    t, d = x2.shape
    posts = posts or [None] * len(weights)
    transposed = frozenset(transposed)
    use_rope = any(p is not None and p.rope for p in posts)
    const = lambda i: (0, 0)
    args = [x2, gain.reshape(1, d).astype(F32)]
    in_specs = [pl.BlockSpec((tm, d), lambda i: (i, 0)), pl.BlockSpec((1, d), const)]
    if use_rope:
        per_seq = seq // tm
        args += list(rope_tables)
        in_specs += [pl.BlockSpec((tm, LANES), lambda i: (i % per_seq, 0))] * 2
    args += list(weights)
    in_specs += [pl.BlockSpec(w.shape, const) for w in weights]
    for w, p in zip(weights, posts):
        if p is not None:
            n = w.shape[1]
            args += [jnp.tile(p.gain.astype(F32), n // p.hd).reshape(1, n), _block_diag_ones(n, p.hd)]
            in_specs += [pl.BlockSpec((1, n), const), pl.BlockSpec((n, n), const)]
    out_specs, out_shape = [], []
    for k, (w, dt) in enumerate(zip(weights, out_dtypes)):
        if k in transposed:
            out_specs.append(pl.BlockSpec((w.shape[0], tm), lambda i: (0, i)))
            out_shape.append(jax.ShapeDtypeStruct((w.shape[0], t), dt))
        else:
            out_specs.append(pl.BlockSpec((tm, w.shape[1]), lambda i: (i, 0)))
            out_shape.append(jax.ShapeDtypeStruct((t, w.shape[1]), dt))
    return pl.pallas_call(
        functools.partial(_norm_proj_body, posts=tuple(posts), use_rope=use_rope, transposed=transposed),
        grid=(t // tm,), in_specs=in_specs, out_specs=out_specs, out_shape=out_shape,
        compiler_params=_cparams(("parallel",)), name="norm_proj",
    )(*args)


def _block_diag_ones(n, hd):
    idx = np.arange(n) // hd
    return jnp.asarray((idx[:, None] == idx[None, :]).astype(np.float32), dtype=BF16)


def _rope_tables(seq, hd):
    inv_freq = 1.0 / (ROPE_THETA ** (jnp.arange(0, hd, 2, dtype=F32) / hd))
    ang = jnp.arange(seq, dtype=F32)[:, None] * inv_freq[None, :]
    cos, sin = jnp.cos(ang), jnp.sin(ang)
    reps = LANES // hd
    cos_t = jnp.tile(jnp.concatenate([cos, cos], axis=1), (1, reps))
    sin_t = jnp.tile(jnp.concatenate([-sin, sin], axis=1), (1, reps))
    return cos_t, sin_t


def _flash_body(lam_ref, *refs, n_qk, diff, out_scale, sub):
    q_refs = refs[:n_qk]
    k_refs = refs[n_qk:2 * n_qk]
    vt_ref, gain_ref, o_ref, m_ref, l_ref, acc_ref, s0_ref, s1_ref, p0_ref, p1_ref = refs[2 * n_qk:]
    i = pl.program_id(2)
    n_sm = 2 if diff else 1
    _, tk, tq = s0_ref.shape
    s_slots = (s0_ref, s1_ref)
    p_slots = (p0_ref, p1_ref)

    m_ref[...] = jnp.full(m_ref.shape, NEG_INF, F32)
    l_ref[...] = jnp.zeros(l_ref.shape, F32)
    acc_ref[...] = jnp.zeros(acc_ref.shape, F32)

    qs = [r[0] for r in q_refs]
    q = qs[0] if n_qk == 1 else jnp.concatenate(qs, axis=1)
    if diff:
        lane = lax.broadcasted_iota(jnp.int32, q.shape, 1)
        half = q.shape[1] // 2
        zero = jnp.zeros_like(q)
        q_parts = [jnp.where(lane < half, q, zero), jnp.where(lane >= half, q, zero)]
    else:
        q_parts = [q]

    def scores(c, slot, key_offset=None):
        rows = pl.ds(pl.multiple_of(c * tk, tk), tk)
        ks = [r[0, rows, :] for r in k_refs]
        k = ks[0] if n_qk == 1 else jnp.concatenate(ks, axis=1)
        for sm in range(n_sm):
            s = _dot_nt(k, q_parts[sm])
            if key_offset is not None:
                row = lax.broadcasted_iota(jnp.int32, s.shape, 0)
                col = lax.broadcasted_iota(jnp.int32, s.shape, 1)
                s = jnp.where(row + key_offset <= col, s, NEG_INF)
            s_slots[slot][sm] = s

    def update(c, slot):
        vt = vt_ref[:, pl.ds(pl.multiple_of(c * tk, tk), tk)]
        for sm in range(n_sm):
            s_ref, p_ref = s_slots[slot], p_slots[slot]
            m_prev = m_ref[sm]
            m_new = jnp.maximum(m_prev, jnp.max(s_ref[sm], axis=0, keepdims=True))
            m_ref[sm] = m_new
            alpha = jnp.exp2(m_prev - m_new)
            lpart = jnp.zeros((SUBLANES, tq), F32)
            for r in range(tk // sub):
                p = jnp.exp2(s_ref[sm, r * sub:(r + 1) * sub, :] - m_new)
                lpart = lpart + _fold_rows(p)
                p_ref[sm, r * sub:(r + 1) * sub, :] = p.astype(BF16)
            l_ref[sm] = alpha * l_ref[sm] + jnp.sum(lpart, axis=0, keepdims=True)
            acc_ref[sm] = alpha * acc_ref[sm] + _dot(vt, p_ref[sm])

    assert tq == 2 * tk
    n_past = 2 * i
    last_past = jnp.maximum(n_past - 1, 0)
    scores(n_past, 0, key_offset=0)
    scores(n_past + 1, 1, key_offset=tk)
    update(n_past, 0)
    scores(0, 0)
    update(n_past + 1, 1)

    def pair(k2, carry):
        scores(2 * k2 + 1, 1)
        update(2 * k2, 0)
        scores(jnp.minimum(2 * k2 + 2, last_past), 0)
        update(2 * k2 + 1, 1)
        return carry

    lax.fori_loop(0, i, pair, 0)

    o = acc_ref[0] * (1.0 / l_ref[0])
    if diff:
        o = o - acc_ref[1] * (lam_ref[0] / l_ref[1])
        ms = jnp.mean(o * o, axis=0, keepdims=True)
        o = o * lax.rsqrt(ms + NORM_EPS) * gain_ref[...] * out_scale
    o_ref[0] = o.T.astype(o_ref.dtype)


def flash_attention(lam, qs, ks, vt, gain, n_heads, dv, *, diff, out_scale=1.0, tk=512):
    b, s, _ = qs[0].shape
    if diff:
        tk = tk // 2
    tile = 2 * tk
    sub = 64 if diff else 32
    nt = s // tile
    n_qk = len(qs)
    in_specs = [pl.BlockSpec(memory_space=pltpu.SMEM)]
    for q in qs:
        w = q.shape[2] // n_heads
        in_specs.append(pl.BlockSpec((1, tile, w), lambda bb, h, i: (bb, i, h)))
    for q, k in zip(qs, ks):
        w = q.shape[2] // n_heads
        if k.shape[2] == w:
            in_specs.append(pl.BlockSpec((1, s, w), lambda bb, h, i: (bb, 0, 0)))
        else:
            in_specs.append(pl.BlockSpec((1, s, w), lambda bb, h, i: (bb, 0, h)))
    in_specs.append(pl.BlockSpec((dv, s), lambda bb, h, i: (h, bb)))
    in_specs.append(pl.BlockSpec((dv, 1), lambda bb, h, i: (0, 0)))
    n_sm = 2 if diff else 1
    return pl.pallas_call(
        functools.partial(_flash_body, n_qk=n_qk, diff=diff, out_scale=out_scale, sub=sub),
        grid=(b, n_heads, nt), in_specs=in_specs,
        out_specs=pl.BlockSpec((1, tile, dv), lambda bb, h, i: (bb, i, h)),
        out_shape=jax.ShapeDtypeStruct((b, s, n_heads * dv), BF16),
        scratch_shapes=[pltpu.VMEM((n_sm, 1, tile), F32), pltpu.VMEM((n_sm, 1, tile), F32),
                        pltpu.VMEM((n_sm, dv, tile), F32),
                        pltpu.VMEM((n_sm, tk, tile), F32), pltpu.VMEM((n_sm, tk, tile), F32),
                        pltpu.VMEM((n_sm, tk, tile), BF16), pltpu.VMEM((n_sm, tk, tile), BF16)],
        compiler_params=_cparams(("parallel", "parallel", "arbitrary")),
        name="flash_diff" if diff else "flash_plain",
    )(lam, *qs, *ks, vt, gain.reshape(dv, 1).astype(F32))


def _ssd_body(xbc_ref, z_ref, dt_ref, dtt_ref, cw_ref, cb_ref, dtb_ref, dtbt_ref, al_ref, alt_ref,
              dsk_ref, ng_ref, o_ref, xpad_ref, state_ref):
    chunk = xbc_ref.shape[1]
    d_in = z_ref.shape[2]
    gn = SSM_GROUPS * SSM_STATE
    c = pl.program_id(1)

    @pl.when(c == 0)
    def _():
        xpad_ref[0:8, :] = jnp.zeros((8, xpad_ref.shape[1]), F32)
        state_ref[...] = jnp.zeros(state_ref.shape, F32)

    xpad_ref[8:8 + chunk, :] = xbc_ref[0]
    conv = cb_ref[...]
    for w in range(SSM_CONV):
        conv = conv + cw_ref[w:w + 1, :] * xpad_ref[pl.ds(8 - (SSM_CONV - 1) + w, chunk), :]
    xpad_ref[0:8, :] = xpad_ref[chunk:chunk + 8, :]
    u = _silu(conv)
    xs = u[:, :d_in]
    bmat = u[:, d_in:d_in + gn]
    cmat = u[:, d_in + gn:]

    dt = _softplus(dt_ref[0] + dtb_ref[...])
    ad = dt * (-jnp.exp(al_ref[...]))
    dtt = _softplus(dtt_ref[0] + dtbt_ref[...])
    adt = dtt * (-jnp.exp(alt_ref[...]))
    row = lax.broadcasted_iota(jnp.int32, (chunk, chunk), 0)
    col = lax.broadcasted_iota(jnp.int32, (chunk, chunk), 1)
    lower = row >= col
    tril = jnp.where(lower, 1.0, 0.0).astype(BF16)
    triu = jnp.where(row <= col, 1.0, 0.0).astype(BF16)
    cs = sum(_dot(tril, part) for part in _split_bf16(ad, 3))
    cst = sum(_dot(part, triu) for part in _split_bf16(adt, 3))

    heads_per_group = SSM_HEADS // SSM_GROUPS
    dsk = dsk_ref[...]
    ys = []
    for g in range(SSM_GROUPS):
        bg = bmat[:, g * SSM_STATE:(g + 1) * SSM_STATE]
        cg = cmat[:, g * SSM_STATE:(g + 1) * SSM_STATE].astype(BF16)
        cb = _dot_nt(cg, bg.astype(BF16))
        bgt = bg.T.astype(BF16)
        for r in range(heads_per_group):
            h = g * heads_per_group + r
            ccol = cs[:, h:h + 1]
            crow = cst[h:h + 1, :]
            decay = jnp.exp(jnp.where(lower, ccol - crow, NEG_INF))
            x_h = xs[:, h * SSM_HEAD_DIM:(h + 1) * SSM_HEAD_DIM]
            xdt = x_h * dt[:, h:h + 1]
            y = _dot((cb * decay).astype(BF16), xdt.astype(BF16))
            st = state_ref[h]
            y = y + _dot(cg, st.astype(BF16)) * jnp.exp(ccol)
            last = cst[h:h + 1, chunk - 1:chunk]
            to_end = jnp.exp(last - ccol)
            state_ref[h] = st * jnp.exp(last) + _dot(bgt, (xdt * to_end).astype(BF16))
            ys.append(y + x_h * dsk[:, h * SSM_HEAD_DIM:(h + 1) * SSM_HEAD_DIM])

    y = jnp.concatenate(ys, axis=1) * _silu(z_ref[0])
    gw = d_in // SSM_GROUPS
    for g in range(SSM_GROUPS):
        seg = y[:, g * gw:(g + 1) * gw]
        o_ref[0, :, g * gw:(g + 1) * gw] = _rms(seg, ng_ref[:, g * gw:(g + 1) * gw]).astype(o_ref.dtype)


def ssd_mixer(xbc, z, dt_raw, conv_w, conv_b, dt_bias, a_log, d_skip, norm_gain):
    b, s, cch = xbc.shape
    d_in = z.shape[2]
    nc = s // SSM_CHUNK
    hpad = dt_raw.shape[2]
    dtt = jnp.transpose(dt_raw[:, :, :SSM_HEADS], (0, 2, 1))

    def lane_pad(v):
        return jnp.pad(v.astype(F32), (0, hpad - SSM_HEADS)).reshape(1, hpad)

    args = (xbc, z, dt_raw, dtt, conv_w.astype(F32), conv_b.reshape(1, cch).astype(F32),
            lane_pad(dt_bias), dt_bias.reshape(SSM_HEADS, 1).astype(F32),
            lane_pad(a_log), a_log.reshape(SSM_HEADS, 1).astype(F32),
            jnp.repeat(d_skip.astype(F32), SSM_HEAD_DIM).reshape(1, d_in),
            norm_gain.reshape(1, d_in).astype(F32))
    const = lambda bb, c: (0, 0)
    in_specs = [pl.BlockSpec((1, SSM_CHUNK, cch), lambda bb, c: (bb, c, 0)),
                pl.BlockSpec((1, SSM_CHUNK, d_in), lambda bb, c: (bb, c, 0)),
                pl.BlockSpec((1, SSM_CHUNK, hpad), lambda bb, c: (bb, c, 0)),
                pl.BlockSpec((1, SSM_HEADS, SSM_CHUNK), lambda bb, c: (bb, 0, c)),
                pl.BlockSpec((SSM_CONV, cch), const), pl.BlockSpec((1, cch), const),
                pl.BlockSpec((1, hpad), const), pl.BlockSpec((SSM_HEADS, 1), const),
                pl.BlockSpec((1, hpad), const), pl.BlockSpec((SSM_HEADS, 1), const),
                pl.BlockSpec((1, d_in), const), pl.BlockSpec((1, d_in), const)]
    return pl.pallas_call(
        _ssd_body, grid=(b, nc), in_specs=in_specs,
        out_specs=pl.BlockSpec((1, SSM_CHUNK, d_in), lambda bb, c: (bb, c, 0)),
        out_shape=jax.ShapeDtypeStruct((b, s, d_in), BF16),
        scratch_shapes=[pltpu.VMEM((SSM_CHUNK + 8, cch), F32),
                        pltpu.VMEM((SSM_HEADS, SSM_STATE, SSM_HEAD_DIM), F32)],
        compiler_params=_cparams(("parallel", "arbitrary")), name="ssd_mixer",
    )(*args)


def _ffn_body(x_ref, a_ref, b_ref, wa_ref, wb_ref, g_ref, wg_ref, wu_ref, wd_ref, o_ref, h_ref):
    f = pl.program_id(1)
    half = h_ref.shape[0] // 2
    halves = (slice(0, half), slice(half, 2 * half))

    @pl.when(f == 0)
    def _():
        for r in halves:
            x = x_ref[r, :] + _dot(a_ref[r, :], wa_ref[...]) + _dot(b_ref[r, :], wb_ref[...])
            h_ref[r, :] = _rms(x, g_ref[...]).astype(BF16)
            o_ref[r, :] = x

    for r in halves:
        h = h_ref[r, :]
        act = (_silu(_dot(h, wg_ref[...])) * _dot(h, wu_ref[...])).astype(BF16)
        o_ref[r, :] += _dot(act, wd_ref[...])


def out_proj_ffn_residual(x2, a, bm, wa, wb, gain, w_gate, w_up, w_down, tm=1024, tf=1408):
    t, d = x2.shape
    d_ff = w_gate.shape[1]
    row = lambda i, f: (i, 0)
    const = lambda i, f: (0, 0)
    return pl.pallas_call(
        _ffn_body, grid=(t // tm, d_ff // tf),
        in_specs=[pl.BlockSpec((tm, d), row), pl.BlockSpec((tm, a.shape[1]), row),
                  pl.BlockSpec((tm, bm.shape[1]), row), pl.BlockSpec(wa.shape, const),
                  pl.BlockSpec(wb.shape, const), pl.BlockSpec((1, d), const),
                  pl.BlockSpec((d, tf), lambda i, f: (0, f)),
                  pl.BlockSpec((d, tf), lambda i, f: (0, f)),
                  pl.BlockSpec((tf, d), lambda i, f: (f, 0))],
        out_specs=pl.BlockSpec((tm, d), row),
        out_shape=jax.ShapeDtypeStruct((t, d), F32),
        scratch_shapes=[pltpu.VMEM((tm, d), BF16)],
        compiler_params=pltpu.CompilerParams(
            dimension_semantics=("parallel", "arbitrary"), vmem_limit_bytes=MOE_VMEM_LIMIT),
        name="out_proj_ffn",
    )(x2, a, bm, wa, wb, gain.reshape(1, d).astype(F32), w_gate, w_up, w_down)


MOE_ROWS = 256
ROUTE_IDX = 0
ROUTE_W = 2
ROUTE_RANK = 4


def _moe_ffn_body(block_expert_ref, n_used_ref, xs_ref, wg_ref, wu_ref, wd_ref, o_ref):
    i = pl.program_id(0)

    @pl.when(i < n_used_ref[0])
    def _():
        x = xs_ref[...]
        act = (_silu(_dot(x, wg_ref[0])) * _dot(x, wu_ref[0])).astype(BF16)
        o_ref[...] = _dot(act, wd_ref[0]).astype(o_ref.dtype)

    @pl.when(i >= n_used_ref[0])
    def _():
        o_ref[...] = jnp.zeros(o_ref.shape, o_ref.dtype)


def moe_expert_ffn(xs, block_expert, n_used, w_gate, w_up, w_down):
    p, d = xs.shape
    d_ff = w_gate.shape[2]
    rows = MOE_ROWS
    grid_spec = pltpu.PrefetchScalarGridSpec(
        num_scalar_prefetch=2, grid=(p // rows,),
        in_specs=[pl.BlockSpec((rows, d), lambda i, be, nu: (i, 0)),
                  pl.BlockSpec((1, d, d_ff), lambda i, be, nu: (be[i], 0, 0)),
                  pl.BlockSpec((1, d, d_ff), lambda i, be, nu: (be[i], 0, 0)),
                  pl.BlockSpec((1, d_ff, d), lambda i, be, nu: (be[i], 0, 0))],
        out_specs=pl.BlockSpec((rows, d), lambda i, be, nu: (i, 0)))
    return pl.pallas_call(
        _moe_ffn_body, grid_spec=grid_spec, out_shape=jax.ShapeDtypeStruct((p, d), BF16),
        compiler_params=pltpu.CompilerParams(
            dimension_semantics=("arbitrary",), vmem_limit_bytes=MOE_VMEM_LIMIT),
        name="moe_expert_ffn",
    )(block_expert, n_used, xs, w_gate, w_up, w_down)


def _moe_combine_body(x_ref, y0_ref, y1_ref, route_ref, o_ref):
    route = route_ref[...]
    lane = lax.broadcasted_iota(jnp.int32, route.shape, 1)
    w0 = jnp.sum(jnp.where(lane == ROUTE_W, route, 0.0), axis=1, keepdims=True)
    w1 = jnp.sum(jnp.where(lane == ROUTE_W + 1, route, 0.0), axis=1, keepdims=True)
    o_ref[...] = x_ref[...] + w0 * y0_ref[...].astype(F32) + w1 * y1_ref[...].astype(F32)


def moe_combine(x2, y0, y1, route, tm=512):
    t, d = x2.shape
    row = lambda i: (i, 0)
    return pl.pallas_call(
        _moe_combine_body, grid=(t // tm,),
        in_specs=[pl.BlockSpec((tm, d), row), pl.BlockSpec((tm, d), row), pl.BlockSpec((tm, d), row),
                  pl.BlockSpec((tm, LANES), row)],
        out_specs=pl.BlockSpec((tm, d), row),
        out_shape=jax.ShapeDtypeStruct((t, d), F32),
        compiler_params=_cparams(("parallel",)), name="moe_combine",
    )(x2, y0, y1, route)


def moe_residual(x2, h, route, counts, w_gate, w_up, w_down):
    t, d = x2.shape
    n_e = w_gate.shape[0]
    rows = MOE_ROWS
    expert = route[:, ROUTE_IDX:ROUTE_IDX + 2].astype(jnp.int32).reshape(-1)
    rank = route[:, ROUTE_RANK:ROUTE_RANK + 2].astype(jnp.int32).reshape(-1)
    count = counts[0, :n_e].astype(jnp.int32)
    padded = (count + rows - 1) // rows * rows
    ends = jnp.cumsum(padded)
    starts = ends - padded
    own = expert[:, None] == jnp.arange(n_e, dtype=jnp.int32)[None, :]
    slot = jnp.sum(jnp.where(own, starts[None, :], 0), axis=1) + rank
    p_rows = 2 * t + n_e * rows
    block_start = jnp.arange(p_rows // rows, dtype=jnp.int32) * rows
    block_expert = jnp.minimum(jnp.sum(block_start[:, None] >= ends[None, :], axis=1), n_e - 1).astype(jnp.int32)
    n_used = (ends[-1] // rows).astype(jnp.int32).reshape(1)
    copy_order = jnp.argsort(slot).astype(jnp.int32)
    within = (block_start - starts[block_expert])[:, None] + jnp.arange(rows, dtype=jnp.int32)[None, :]
    dense = (jnp.cumsum(count) - count)[block_expert][:, None] + within
    real = within < count[block_expert][:, None]
    picked = copy_order.at[jnp.clip(dense, 0, 2 * t - 1)].get(mode="promise_in_bounds")
    token_of_slot = jnp.where(real, picked // 2, 0).reshape(p_rows)

    take_rows = lambda a, idx: a.at[idx].get(mode="promise_in_bounds")
    xs = take_rows(h, token_of_slot)
    ys = moe_expert_ffn(xs, block_expert, n_used, w_gate, w_up, w_down)
    slot2 = slot.reshape(t, 2)
    return moe_combine(x2, take_rows(ys, slot2[:, 0]), take_rows(ys, slot2[:, 1]), route)


def _router_body(x_ref, a_ref, b_ref, wa_ref, wb_ref, g_ref, rhi_ref, rlo_ref,
                 xo_ref, h_ref, o_ref, count_ref, run_ref):
    @pl.when(pl.program_id(0) == 0)
    def _():
        run_ref[...] = jnp.zeros(run_ref.shape, F32)

    x = x_ref[...] + _dot(a_ref[...], wa_ref[...]) + _dot(b_ref[...], wb_ref[...])
    xo_ref[...] = x
    h = _rms(x, g_ref[...])
    h_ref[...] = h.astype(h_ref.dtype)
    h_hi, h_lo = _split_bf16(h, 2)
    logits = _dot(h_hi, rhi_ref[...]) + _dot(h_hi, rlo_ref[...]) + _dot(h_lo, rhi_ref[...])
    lane = lax.broadcasted_iota(jnp.int32, logits.shape, 1).astype(F32)
    low = jnp.float32(-3.0e38)
    logits = jnp.where(lane < N_EXPERTS, logits, low)
    m1 = jnp.max(logits, axis=1, keepdims=True)
    i1 = jnp.min(jnp.where(logits == m1, lane, float(LANES)), axis=1, keepdims=True)
    rest = jnp.where(lane == i1, low, logits)
    m2 = jnp.max(rest, axis=1, keepdims=True)
    i2 = jnp.min(jnp.where(rest == m2, lane, float(LANES)), axis=1, keepdims=True)
    ex = jnp.exp(m2 - m1)
    w1 = 1.0 / (1.0 + ex)
    w2 = ex / (1.0 + ex)
    tm = logits.shape[0]
    routed = jnp.where((lane == i1) | (lane == i2), 1.0, 0.0)
    row = lax.broadcasted_iota(jnp.int32, (tm, tm), 0)
    col = lax.broadcasted_iota(jnp.int32, (tm, tm), 1)
    before = jnp.where(col < row, 1.0, 0.0).astype(BF16)
    rank = run_ref[0:1, :] + _dot(before, routed.astype(BF16))
    r1 = jnp.sum(jnp.where(lane == i1, rank, 0.0), axis=1, keepdims=True)
    r2 = jnp.sum(jnp.where(lane == i2, rank, 0.0), axis=1, keepdims=True)
    fields = ((ROUTE_IDX, i1), (ROUTE_IDX + 1, i2), (ROUTE_W, w1), (ROUTE_W + 1, w2),
              (ROUTE_RANK, r1), (ROUTE_RANK + 1, r2))
    out = jnp.zeros(logits.shape, F32)
    for pos, val in fields:
        out = jnp.where(lane == pos, val, out)
    o_ref[...] = out
    run_ref[...] = run_ref[...] + jnp.sum(routed, axis=0, keepdims=True)
    count_ref[...] = run_ref[...]


def out_proj_router(x2, a, bm, wa, wb, gain, router_w, tm=512):
    t, d = x2.shape
    r_pad = jnp.pad(router_w.astype(F32), ((0, 0), (0, LANES - router_w.shape[1])))
    r_hi, r_lo = _split_bf16(r_pad, 2)
    row = lambda i: (i, 0)
    const = lambda i: (0, 0)
    return pl.pallas_call(
        _router_body, grid=(t // tm,),
        in_specs=[pl.BlockSpec((tm, d), row), pl.BlockSpec((tm, a.shape[1]), row),
                  pl.BlockSpec((tm, bm.shape[1]), row), pl.BlockSpec(wa.shape, const),
                  pl.BlockSpec(wb.shape, const), pl.BlockSpec((1, d), const),
                  pl.BlockSpec((d, LANES), const), pl.BlockSpec((d, LANES), const)],
        out_specs=[pl.BlockSpec((tm, d), row), pl.BlockSpec((tm, d), row), pl.BlockSpec((tm, LANES), row),
                   pl.BlockSpec((SUBLANES, LANES), const)],
        out_shape=[jax.ShapeDtypeStruct((t, d), F32), jax.ShapeDtypeStruct((t, d), BF16),
                   jax.ShapeDtypeStruct((t, LANES), F32), jax.ShapeDtypeStruct((SUBLANES, LANES), F32)],
        scratch_shapes=[pltpu.VMEM((SUBLANES, LANES), F32)],
        compiler_params=_cparams(("arbitrary",)), name="out_proj_router",
    )(x2, a, bm, wa, wb, gain.reshape(1, d).astype(F32), r_hi, r_lo)


def _compress_body(ch_ref, nx_ref, pos_ref, w1_ref, w2_ref, gain_ref, cos_ref, sin_ref, rot_ref, o_ref,
                   *, is_key):
    a = _dot((ch_ref[0] + pos_ref[0]).astype(BF16), w1_ref[0, 0])
    a = a + _dot((nx_ref[0] + pos_ref[1]).astype(BF16), w1_ref[0, 1])
    out = _dot(_silu(a).astype(BF16), w2_ref[...])
    if is_key:
        out = _rms(out, gain_ref[...])
        hi, lo = _split_bf16(out, 2)
        partner = _dot(hi, rot_ref[...]) + _dot(lo, rot_ref[...])
        out = out * cos_ref[...] + partner * sin_ref[...]
    o_ref[0, 0] = out


def nsa_compress(t, pos, w1, w2, gain, seq, is_key):
    b, s, _ = t.shape
    g, d = NSA_KV_GROUPS, NSA_HEAD_DIM
    n_ch = s // NSA_CMP_STRIDE
    half = NSA_CMP_STRIDE * g * d
    ch = t.reshape(b, n_ch, half)
    nxt = jnp.concatenate([ch[:, 1:], jnp.zeros((b, 1, half), F32)], axis=1)
    pos2 = jnp.broadcast_to(pos.astype(F32).reshape(2, NSA_CMP_STRIDE, 1, d),
                            (2, NSA_CMP_STRIDE, g, d)).reshape(2, 1, half)
    w1r = w1.astype(BF16).reshape(2, NSA_CMP_STRIDE, 1, d, d)
    own = (jnp.arange(g)[:, None] == jnp.arange(g)[None, :]).reshape(g, 1, 1, g, 1, 1)
    w1s = jnp.where(own, w1r[None], jnp.zeros((), BF16)).reshape(g, 2, half, d)
    cmp_end = jnp.arange(n_ch) * NSA_CMP_STRIDE + NSA_CMP_BLOCK - 1
    inv_freq = 1.0 / (ROPE_THETA ** (jnp.arange(0, d, 2, dtype=F32) / d))
    ang = cmp_end.astype(F32)[:, None] * inv_freq[None, :]
    cos = jnp.concatenate([jnp.cos(ang)] * 2, axis=1)
    sin = jnp.concatenate([jnp.sin(ang)] * 2, axis=1)
    rot = np.zeros((d, d), np.float32)
    rot[np.arange(d // 2) + d // 2, np.arange(d // 2)] = -1.0
    rot[np.arange(d // 2), np.arange(d // 2) + d // 2] = 1.0
    blk = lambda bb, gg: (bb, gg, 0, 0)
    seq = lambda bb, gg: (bb, 0, 0)
    c2 = lambda bb, gg: (0, 0)
    c3 = lambda bb, gg: (0, 0, 0)
    return pl.pallas_call(
        functools.partial(_compress_body, is_key=is_key), grid=(b, g),
        in_specs=[pl.BlockSpec((1, n_ch, half), seq), pl.BlockSpec((1, n_ch, half), seq),
                  pl.BlockSpec((2, 1, half), c3), pl.BlockSpec((1, 2, half, d), lambda bb, gg: (gg, 0, 0, 0)),
                  pl.BlockSpec((d, d), c2), pl.BlockSpec((1, d), c2),
                  pl.BlockSpec((n_ch, d), c2), pl.BlockSpec((n_ch, d), c2), pl.BlockSpec((d, d), c2)],
        out_specs=pl.BlockSpec((1, 1, n_ch, d), blk),
        out_shape=jax.ShapeDtypeStruct((b, g, n_ch, d), F32),
        compiler_params=_cparams(("parallel", "parallel")), name="nsa_compress",
    )(ch, nxt, pos2, w1s, w2.astype(BF16), gain.reshape(1, d).astype(F32), cos, sin,
      jnp.asarray(rot, dtype=BF16))


def _nsa_body(q_ref, ck_ref, cvt_ref, ksl_ref, vslt_ref, kwn_ref, vwnt_ref, ovt_ref, glt_ref, o_ref,
              sc_ref, phi_ref, plo_ref, imp_ref, bias_ref, ss_ref, ps_ref, ss1_ref, ps1_ref, sw_ref, pw_ref,
              ow_ref, oc_ref, *, tq):
    g = pl.program_id(1)
    i = pl.program_id(2)
    d = NSA_HEAD_DIM
    rep = NSA_HEADS // NSA_KV_GROUPS
    t0 = i * tq
    n_cmp = ck_ref.shape[1]
    n_sel = ovt_ref.shape[0]
    width = rep * tq
    sub = NSA_SEL_BLOCK
    dead = 0.5 * NEG_INF
    v_rows = pl.ds(pl.multiple_of(g * d, d), d)

    qb = q_ref[0]
    q4 = jnp.concatenate([qb[:, r * d:(r + 1) * d] for r in range(rep)], axis=0)
    q4 = jnp.concatenate([q4, q4], axis=1)
    lane = lax.broadcasted_iota(jnp.int32, q4.shape, 1)
    q4 = jnp.where(jnp.right_shift(lane, d.bit_length() - 1) == g, q4, jnp.zeros_like(q4))

    def qpos_of(shape):
        return t0 + (lax.broadcasted_iota(jnp.int32, shape, 1) & (tq - 1))

    def compressed(rows):
        s = _dot_nt(ck_ref[0, 0:rows, :], q4)
        cmp_end = lax.broadcasted_iota(jnp.int32, s.shape, 0) * NSA_CMP_STRIDE + (NSA_CMP_BLOCK - 1)
        s = jnp.where(cmp_end <= qpos_of(s.shape), s, NEG_INF)
        sc_ref[0:rows, :] = s
        m_c = jnp.max(s, axis=0, keepdims=True)
        lpart = jnp.zeros((SUBLANES, width), F32)
        for r in range(rows // sub):
            e = jnp.exp2(sc_ref[r * sub:(r + 1) * sub, :] - m_c)
            lpart = lpart + _fold_rows(e)
            hi = e.astype(BF16)
            phi_ref[r * sub:(r + 1) * sub, :] = hi
            plo_ref[r * sub:(r + 1) * sub, :] = (e - hi.astype(F32)).astype(BF16)
        inv_c = jnp.where(m_c > dead, 1.0 / jnp.sum(lpart, axis=0, keepdims=True), 0.0)
        oc_ref[...] = _dot(cvt_ref[0, v_rows, 0:rows], phi_ref[0:rows, :]) * inv_c
        ovt = ovt_ref[:, 0:rows]
        imp4 = (_dot(ovt, phi_ref[0:rows, :]) + _dot(ovt, plo_ref[0:rows, :])) * inv_c
        imp = imp4[:, 0:tq]
        for r in range(1, rep):
            imp = imp + imp4[:, r * tq:(r + 1) * tq]
        imp_ref[...] = imp

    cmp_live = jnp.clip((t0 + tq - NSA_CMP_BLOCK) // NSA_CMP_STRIDE + 1, 1, n_cmp)
    cmp_step = min(2 * LANES, n_cmp)
    for v in range(n_cmp // cmp_step):
        @pl.when((cmp_live > cmp_step * v) & (cmp_live <= cmp_step * (v + 1)))
        def _():
            compressed(cmp_step * (v + 1))

    o_c = oc_ref[...]
    imp = imp_ref[...]

    blk = lax.broadcasted_iota(jnp.int32, imp.shape, 0)
    qp = t0 + lax.broadcasted_iota(jnp.int32, imp.shape, 1)
    cur = jnp.right_shift(qp, NSA_SEL_BLOCK.bit_length() - 1)
    forced = (blk == 0) | (blk == cur) | (blk == cur - 1)
    future = blk * NSA_SEL_BLOCK > qp
    imp_ref[...] = jnp.where(future, -FORCE_SCORE, jnp.where(forced, FORCE_SCORE, imp))
    bias_ref[...] = jnp.full(bias_ref.shape, NEG_INF, F32)

    n_live = jnp.minimum((t0 + tq - 1) // NSA_SEL_BLOCK + 1, n_sel)
    n_var = max(n_sel // 32, 1)
    rows_per = n_sel // n_var
    top_n = float(min(NSA_TOP_N, n_sel))
    for v in range(n_var):
        rows = rows_per * (v + 1)

        @pl.when((n_live > rows_per * v) & (n_live <= rows))
        def _():
            groups = rows // SUBLANES
            mine = [imp_ref[gi * SUBLANES:(gi + 1) * SUBLANES, :] for gi in range(groups)]
            rank = [jnp.zeros((SUBLANES, tq), F32) for _ in range(groups)]
            in_group = lax.broadcasted_iota(jnp.int32, (SUBLANES, tq), 0)
            for i2 in range(rows):
                other = imp_ref[i2:i2 + 1, :]
                for gi in range(groups):
                    if gi > i2 // SUBLANES:
                        beats = other >= mine[gi]
                    elif gi < i2 // SUBLANES:
                        beats = other > mine[gi]
                    else:
                        beats = (other > mine[gi]) | ((other == mine[gi]) & (in_group > i2 % SUBLANES))
                    rank[gi] = rank[gi] + jnp.where(beats, 1.0, 0.0)
            for gi in range(groups):
                bias = jnp.where(rank[gi] < top_n, 0.0, NEG_INF)
                bias_ref[gi * SUBLANES:(gi + 1) * SUBLANES, :] = jnp.concatenate([bias] * rep, axis=1)

    init = (jnp.full((1, width), NEG_INF, F32), jnp.zeros((d + ONES_ROWS, width), F32))

    def normalised(acc):
        return acc[:d, :] * (1.0 / acc[d:d + 1, :])

    chunk = 8 * sub
    n_sub = chunk // sub

    s_slots = (ss_ref, ss1_ref)
    p_slots = (ps_ref, ps1_ref)

    def sel_scores(c, slot, diagonal=False):
        start = pl.multiple_of(c * chunk, chunk)
        s = _dot_nt(ksl_ref[0, pl.ds(start, chunk), :], q4)
        if diagonal:
            kpos = start + lax.broadcasted_iota(jnp.int32, s.shape, 0)
            s = jnp.where(kpos <= qpos_of(s.shape), s, NEG_INF)
        s_slots[slot][...] = s

    def sel_update(c, slot, carry):
        m_prev, acc = carry
        s_ref, p_ref = s_slots[slot], p_slots[slot]
        biases = [bias_ref[pl.ds(c * n_sub + r, 1), :] for r in range(n_sub)]
        m8 = jnp.full((SUBLANES, width), NEG_INF, F32)
        for r in range(n_sub):
            block = s_ref[r * sub:(r + 1) * sub, :]
            m8 = jnp.maximum(m8, jnp.max(block.reshape(sub // SUBLANES, SUBLANES, width), axis=0) + biases[r])
        m_new = jnp.maximum(m_prev, jnp.max(m8, axis=0, keepdims=True))
        live = m_new > dead
        for r in range(n_sub):
            shift = jnp.where(live, biases[r] - m_new, NEG_INF)
            p_ref[r * sub:(r + 1) * sub, :] = jnp.exp2(s_ref[r * sub:(r + 1) * sub, :] + shift).astype(BF16)
        vt = _with_ones_rows(vslt_ref[v_rows, pl.ds(pl.multiple_of(c * chunk, chunk), chunk)])
        return m_new, jnp.exp2(m_prev - m_new) * acc + _dot(vt, p_ref[...])

    c_diag = (t0 + tq - 1) // chunk
    n_chunks = c_diag + 1
    last_past = jnp.maximum(c_diag - 1, 0)

    def chunk_at(j):
        return jnp.where(j == 0, c_diag, j - 1)

    sel_scores(c_diag, 0, diagonal=True)

    def pair(k, carry):
        sel_scores(jnp.minimum(2 * k, last_past), 1)
        carry = sel_update(chunk_at(2 * k), 0, carry)
        sel_scores(jnp.minimum(2 * k + 1, last_past), 0)
        return sel_update(2 * k, 1, carry)

    carry = lax.fori_loop(0, n_chunks // 2, pair, init)
    _, acc_s = lax.cond(n_chunks % 2 == 1, lambda cr: sel_update(chunk_at(n_chunks - 1), 0, cr),
                        lambda cr: cr, carry)
    o_s = normalised(acc_s)

    def win_chunk(c, carry):
        m_prev, acc = carry
        start = pl.multiple_of(c * tq, tq)
        s = _dot_nt(kwn_ref[0, pl.ds(start, tq), :], q4)
        kpos = start + lax.broadcasted_iota(jnp.int32, s.shape, 0)
        qpos = qpos_of(s.shape)
        s = jnp.where((kpos <= qpos) & (kpos > qpos - NSA_WINDOW), s, NEG_INF)
        m_new = jnp.maximum(m_prev, jnp.max(s, axis=0, keepdims=True))
        p = jnp.exp2(s + jnp.where(m_new > dead, -m_new, NEG_INF))
        vt = _with_ones_rows(vwnt_ref[v_rows, pl.ds(start, tq)])
        return m_new, jnp.exp2(m_prev - m_new) * acc + _dot(vt, p.astype(BF16))

    n_back = NSA_WINDOW // tq

    @pl.when(i < n_back)
    def _():
        _, acc_w = lax.fori_loop(0, i + 1, win_chunk, init)
        ow_ref[...] = normalised(acc_w)

    @pl.when(i >= n_back)
    def _():
        start = pl.multiple_of(t0 - NSA_WINDOW, tq)
        s = _dot_nt(kwn_ref[0, pl.ds(start, NSA_WINDOW + tq), :], q4)
        kpos = start + lax.broadcasted_iota(jnp.int32, (tq, width), 0)
        qpos = qpos_of((tq, width))
        sw_ref[0:tq, :] = jnp.where(kpos > qpos - NSA_WINDOW, s[0:tq, :], NEG_INF)
        sw_ref[tq:NSA_WINDOW, :] = s[tq:NSA_WINDOW, :]
        sw_ref[NSA_WINDOW:, :] = jnp.where(kpos + NSA_WINDOW <= qpos, s[NSA_WINDOW:, :], NEG_INF)
        m_w = jnp.max(sw_ref[...], axis=0, keepdims=True)
        for r in range((NSA_WINDOW + tq) // sub):
            pw_ref[r * sub:(r + 1) * sub, :] = jnp.exp2(sw_ref[r * sub:(r + 1) * sub, :] - m_w).astype(BF16)
        vt = _with_ones_rows(vwnt_ref[v_rows, pl.ds(start, NSA_WINDOW + tq)])
        ow_ref[...] = normalised(_dot(vt, pw_ref[...]))

    o_w = ow_ref[...]

    def gate(branch):
        rows = [glt_ref[pl.ds((g * rep + r) * 3 + branch, 1), :] for r in range(rep)]
        return _sigmoid(jnp.concatenate(rows, axis=1))

    out = gate(0) * o_c + gate(1) * o_s + gate(2) * o_w
    out_t = jnp.concatenate([out, jnp.zeros_like(out)], axis=0).T
    o_ref[0] = jnp.concatenate([out_t[r * tq:(r + 1) * tq, :d] for r in range(rep)],
                               axis=1).astype(o_ref.dtype)


def nsa_overlap_t(n_cmp, n_sel):
    c_start = np.arange(n_cmp)[None, :] * NSA_CMP_STRIDE
    s_start = np.arange(n_sel)[:, None] * NSA_SEL_BLOCK
    hit = (c_start < s_start + NSA_SEL_BLOCK) & (c_start + NSA_CMP_BLOCK > s_start)
    hit = hit & (np.arange(n_cmp)[None, :] < n_cmp - NSA_CMP_BLOCK // NSA_CMP_STRIDE + 1)
    return jnp.asarray(hit.astype(np.float32), dtype=BF16)


def nsa_attention(qn, ck, cvt, ksl, vslt, kwn, vwnt, glt, tq=256):
    b, s, _ = qn.shape
    g, d = NSA_KV_GROUPS, NSA_HEAD_DIM
    rep = NSA_HEADS // g
    n_cmp = ck.shape[1]
    n_sel = s // NSA_SEL_BLOCK
    nq = s // tq
    ovt = nsa_overlap_t(n_cmp, n_sel)
    width = rep * tq
    chunk = 8 * NSA_SEL_BLOCK
    full3 = lambda bb, gg, i: (bb, 0, 0)
    seq_t = lambda bb, gg, i: (0, bb)
    return pl.pallas_call(
        functools.partial(_nsa_body, tq=tq), grid=(b, g, nq),
        in_specs=[pl.BlockSpec((1, tq, rep * d), lambda bb, gg, i: (bb, i, gg)),
                  pl.BlockSpec((1, n_cmp, g * d), full3), pl.BlockSpec((1, g * d, n_cmp), full3),
                  pl.BlockSpec((1, s, g * d), full3), pl.BlockSpec((g * d, s), seq_t),
                  pl.BlockSpec((1, s, g * d), full3), pl.BlockSpec((g * d, s), seq_t),
                  pl.BlockSpec((n_sel, n_cmp), lambda bb, gg, i: (0, 0)),
                  pl.BlockSpec((glt.shape[0], tq), lambda bb, gg, i: (0, bb * nq + i))],
        out_specs=pl.BlockSpec((1, tq, rep * d), lambda bb, gg, i: (bb, i, gg)),
        out_shape=jax.ShapeDtypeStruct((b, s, g * rep * d), BF16),
        scratch_shapes=[pltpu.VMEM((n_cmp, width), F32), pltpu.VMEM((n_cmp, width), BF16),
                        pltpu.VMEM((n_cmp, width), BF16), pltpu.VMEM((n_sel, tq), F32),
                        pltpu.VMEM((n_sel, width), F32), pltpu.VMEM((chunk, width), F32),
                        pltpu.VMEM((chunk, width), BF16), pltpu.VMEM((chunk, width), F32),
                        pltpu.VMEM((chunk, width), BF16), pltpu.VMEM((NSA_WINDOW + tq, width), F32),
                        pltpu.VMEM((NSA_WINDOW + tq, width), BF16), pltpu.VMEM((d, width), F32),
                        pltpu.VMEM((d, width), F32)],
        compiler_params=_cparams(("parallel", "parallel", "arbitrary")), name="nsa_attention",
    )(qn, ck, cvt, ksl, vslt, kwn, vwnt, ovt, glt)


def _pad_cols(w, n):
    return jnp.pad(w, ((0, 0), (0, n - w.shape[1])))


def _even_layer(x2, b, s, layer_idx, norm_mix, w_in, q_gain, k_gain, lam, subln_gain, conv_w, conv_b,
                dt_bias, a_log, d_skip, ssm_norm_gain, w_out, norm_ffn, w_gate, w_up, w_down):
    nq = DA_HEADS * 2 * DA_HEAD_DIM
    nv = DA_HEADS * DA_V_DIM
    cch = SSM_D_INNER + 2 * SSM_GROUPS * SSM_STATE
    offs = np.cumsum([0, nq, nq, nv, SSM_D_INNER, cch, SSM_HEADS])
    wb = w_in.astype(BF16)
    pieces = [wb[:, offs[k]:offs[k + 1]] for k in range(6)]
    pieces[2] = pieces[2].T
    pieces[5] = _pad_cols(pieces[5], LANES)
    posts = [HeadNorm(q_gain, DA_HEAD_DIM, rope=True, mul=DA_HEAD_DIM ** -0.5 * LOG2E),
             HeadNorm(k_gain, DA_HEAD_DIM, rope=True), None, None, None, None]
    q, k, vt, z, xbc, dt = norm_proj(x2, norm_mix, pieces, [BF16, BF16, BF16, F32, F32, F32], posts, s,
                                     _rope_tables(s, DA_HEAD_DIM), transposed=(2,))
    qn = q.reshape(b, s, nq)
    kn = k.reshape(b, s, nq)
    lam_init = 0.8 - 0.6 * math.exp(-0.3 * layer_idx)
    lf = lam.astype(F32)
    lam_full = jnp.exp(jnp.sum(lf[0] * lf[1])) - jnp.exp(jnp.sum(lf[2] * lf[3])) + lam_init
    a_out = flash_attention(lam_full.reshape(1), [qn], [kn], vt, subln_gain, DA_HEADS, DA_V_DIM,
                            diff=True, out_scale=1.0 - lam_init)
    b_out = ssd_mixer(xbc.reshape(b, s, cch), z.reshape(b, s, SSM_D_INNER), dt.reshape(b, s, LANES),
                      conv_w, conv_b, dt_bias, a_log, d_skip, ssm_norm_gain)
    wo = w_out.astype(BF16)
    return out_proj_ffn_residual(x2, a_out.reshape(-1, nv), b_out.reshape(-1, SSM_D_INNER), wo[:nv], wo[nv:],
                                 norm_ffn, w_gate.astype(BF16), w_up.astype(BF16), w_down.astype(BF16))


def _odd_layer(x2, b, s, norm_mix, w_in, q_gain, k_gain, cmp_pos, cmp_w1, cmp_w2, cq_gain, ckv_gain,
               w_uq, w_ukv, qn_gain, qr_gain, kn_gain, kr_gain, w_out, norm_ffn, router_w, w_gate, w_up,
               w_down):
    g, d = NSA_KV_GROUPS, NSA_HEAD_DIM
    nq = NSA_HEADS * d
    nkv = g * d
    sizes = [nq] + [nkv] * 6 + [NSA_HEADS * 3, w_uq.shape[0], w_ukv.shape[0], MLA_ROPE_DIM]
    offs = np.cumsum([0] + sizes)
    wb = w_in.astype(BF16)
    pieces = [wb[:, offs[k]:offs[k + 1]] for k in range(len(sizes))]
    for k in (4, 6):
        pieces[k] = pieces[k].T
    pieces[7] = jnp.pad(pieces[7].T, ((0, 32 - NSA_HEADS * 3), (0, 0)))
    pieces[10] = _pad_cols(pieces[10], LANES)
    tables = _rope_tables(s, d)
    posts = [None] * len(sizes)
    posts[0] = HeadNorm(q_gain, d, rope=True, mul=d ** -0.5 * LOG2E)
    posts[3] = HeadNorm(k_gain[1], d, rope=True)
    posts[5] = HeadNorm(k_gain[2], d, rope=True)
    posts[10] = HeadNorm(kr_gain, MLA_ROPE_DIM, rope=True)
    (q, kc, vc, ksl, vslt, kwn, vwnt, glt, cq, ckv, k_rope) = norm_proj(
        x2, norm_mix, pieces, [BF16, F32, F32, BF16, BF16, BF16, BF16, F32, F32, F32, BF16], posts, s, tables,
        transposed=(4, 6, 7))

    qn = q.reshape(b, s, nq)
    ksl_n = ksl.reshape(b, s, nkv)
    kwn_n = kwn.reshape(b, s, nkv)
    ck = nsa_compress(kc.reshape(b, s, nkv), cmp_pos[0], cmp_w1[0], cmp_w2[0], k_gain[0], s, True)
    cv = nsa_compress(vc.reshape(b, s, nkv), cmp_pos[1], cmp_w1[1], cmp_w2[1], k_gain[0], s, False)
    n_cmp = ck.shape[2]
    ck = ck.transpose(0, 2, 1, 3).reshape(b, n_cmp, nkv).astype(BF16)
    cvt = cv.transpose(0, 1, 3, 2).reshape(b, nkv, n_cmp).astype(BF16)
    c_out = nsa_attention(qn, ck, cvt, ksl_n, vslt, kwn_n, vwnt, glt).reshape(b * s, nq)

    h = MLA_HEADS
    dqk = MLA_NOPE_DIM + MLA_ROPE_DIM
    wq = w_uq.astype(BF16).reshape(-1, h, dqk)
    wq_nope = wq[:, :, :MLA_NOPE_DIM].reshape(-1, h * MLA_NOPE_DIM)
    wq_rope = jnp.pad(wq[:, :, MLA_NOPE_DIM:], ((0, 0), (0, 0), (0, LANES - MLA_ROPE_DIM)))
    wq_rope = wq_rope.reshape(-1, h * LANES)
    wkv = w_ukv.astype(BF16).reshape(-1, h, MLA_NOPE_DIM + MLA_V_DIM)
    wk_nope = wkv[:, :, :MLA_NOPE_DIM].reshape(-1, h * MLA_NOPE_DIM)
    wv = wkv[:, :, MLA_NOPE_DIM:].reshape(-1, h * MLA_V_DIM)
    q_mul = dqk ** -0.5 * LOG2E
    q_nope, q_rope = norm_proj(
        cq, cq_gain, [wq_nope, wq_rope], [BF16, BF16],
        [HeadNorm(qn_gain, MLA_NOPE_DIM, mul=q_mul), HeadNorm(qr_gain, MLA_ROPE_DIM, rope=True, mul=q_mul)],
        s, tables)
    k_nope, vt = norm_proj(ckv, ckv_gain, [wk_nope, wv.T], [BF16, BF16],
                           [HeadNorm(kn_gain, MLA_NOPE_DIM), None], transposed=(1,))
    shp = lambda t: t.reshape(b, s, t.shape[-1])
    d_out = flash_attention(jnp.zeros((1,), F32), [shp(q_nope), shp(q_rope)], [shp(k_nope), shp(k_rope)],
                            vt, jnp.ones((MLA_V_DIM,), F32), h, MLA_V_DIM, diff=False)

    wo = w_out.astype(BF16)
    x2, hn, route, counts = out_proj_router(x2, c_out, d_out.reshape(b * s, h * MLA_V_DIM), wo[:nq], wo[nq:],
                                            norm_ffn, router_w)
    return moe_residual(x2, hn, route, counts, w_gate.astype(BF16), w_up.astype(BF16), w_down.astype(BF16))


def kernel(x, ev_norm_mix, ev_w_in, da_q_gain, da_k_gain, da_lambda, da_subln_gain, ssm_conv_w, ssm_conv_b, ssm_dt_bias, ssm_a_log, ssm_d, ssm_norm_gain, ev_w_out, ev_norm_ffn, ffn_w_gate, ffn_w_up, ffn_w_down, od_norm_mix, od_w_in, nsa_q_gain, nsa_k_gain, nsa_cmp_pos, nsa_cmp_w1, nsa_cmp_w2, mla_cq_gain, mla_ckv_gain, mla_w_uq, mla_w_ukv, mla_qn_gain, mla_qr_gain, mla_kn_gain, mla_kr_gain, od_w_out, od_norm_ffn, moe_router, moe_w_gate, moe_w_up, moe_w_down):
    b, s, d = x.shape
    x2 = x.reshape(b * s, d)
    depth = ev_norm_mix.shape[0] + od_norm_mix.shape[0]
    for layer in range(depth):
        i = layer // 2
        if layer % 2 == 0:
            x2 = _even_layer(x2, b, s, layer, ev_norm_mix[i], ev_w_in[i], da_q_gain[i], da_k_gain[i],
                             da_lambda[i], da_subln_gain[i], ssm_conv_w[i], ssm_conv_b[i],
                             ssm_dt_bias[i], ssm_a_log[i], ssm_d[i], ssm_norm_gain[i], ev_w_out[i],
                             ev_norm_ffn[i], ffn_w_gate[i], ffn_w_up[i], ffn_w_down[i])
        else:
            x2 = _odd_layer(x2, b, s, od_norm_mix[i], od_w_in[i], nsa_q_gain[i], nsa_k_gain[i],
                            nsa_cmp_pos[i], nsa_cmp_w1[i], nsa_cmp_w2[i], mla_cq_gain[i],
                            mla_ckv_gain[i], mla_w_uq[i], mla_w_ukv[i], mla_qn_gain[i], mla_qr_gain[i],
                            mla_kn_gain[i], mla_kr_gain[i], od_w_out[i], od_norm_ffn[i], moe_router[i],
                            moe_w_gate[i], moe_w_up[i], moe_w_down[i])
    return x2.reshape(b, s, d)
```
